```python
import math
import jax, jax.numpy as jnp
from jax import lax
import numpy as np

D_MODEL = 1024
BATCH = 8
SEQ = 4096
DEPTH = 1

SSM_WIDTH = D_MODEL // 2
ATTN_WIDTH = D_MODEL - SSM_WIDTH
SSM_GROUP = 16
SSM_GROUPS = SSM_WIDTH // SSM_GROUP
SSM_STATE = 64
DT_MIN = 1e-3
DT_MAX = 1e-1
DA_HEADS = 4
DA_VDIM = ATTN_WIDTH // DA_HEADS
DA_QKDIM = DA_VDIM // 2
Q_BLOCK = 128
ROPE_THETA = 10000.0
MEM_LEN = 256
X_HEADS = 4
X_HEAD_DIM = D_MODEL // X_HEADS
N_EXPERTS = 64
TOP_K = 8
N_EXPERT_GROUPS = 8
TOPK_GROUPS = 4
D_EXPERT = D_MODEL // 4
D_SHARED = D_EXPERT
ROUTED_SCALE = 2.5
MOE_BLOCK = 128
EPS = 1e-6

kernel_name = "hybrid_s5_diffattn_moe_layer"


def rmsnorm(x, g):
    xf = x.astype(jnp.float32)
    y = xf * lax.rsqrt(jnp.mean(xf * xf, axis=-1, keepdims=True) + EPS)
    return (y * g.astype(jnp.float32)).astype(x.dtype)


def rope(x, positions):
    d = x.shape[-1]
    inv = ROPE_THETA ** (-jnp.arange(0, d, 2, dtype=jnp.float32) / d)
    ang = positions.astype(jnp.float32)[:, None, :, None] * inv
    cos, sin = jnp.cos(ang), jnp.sin(ang)
    xf = x.astype(jnp.float32)
    x1, x2 = xf[..., : d // 2], xf[..., d // 2:]
    return jnp.concatenate([x1 * cos - x2 * sin, x2 * cos + x1 * sin], axis=-1).astype(x.dtype)


def s5_mixer(u, a_re, a_im, log_dt, b_re, b_im, c_re, c_im, d_skip, w_glu):
    bsz, seqlen, _ = u.shape
    f32 = jnp.float32
    uf = u.astype(f32).reshape(bsz, seqlen, SSM_GROUPS, SSM_GROUP)
    lr, li = a_re.astype(f32), a_im.astype(f32)
    dt = jnp.exp(log_dt.astype(f32))[:, None]
    mag = jnp.exp(lr * dt)
    ab_re, ab_im = mag * jnp.cos(li * dt), mag * jnp.sin(li * dt)
    den = lr * lr + li * li
    zr, zi = ab_re - 1.0, ab_im
    k_re = (zr * lr + zi * li) / den
    k_im = (zi * lr - zr * li) / den
    br, bi = b_re.astype(f32), b_im.astype(f32)
    bb_re = k_re[..., None] * br - k_im[..., None] * bi
    bb_im = k_re[..., None] * bi + k_im[..., None] * br
    bu_re = jnp.einsum('blgc,gpc->blgp', uf, bb_re)
    bu_im = jnp.einsum('blgc,gpc->blgp', uf, bb_im)
    a_full_re = jnp.broadcast_to(ab_re, bu_re.shape)
    a_full_im = jnp.broadcast_to(ab_im, bu_im.shape)

    def combine(e1, e2):
        ar1, ai1, br1, bi1 = e1
        ar2, ai2, br2, bi2 = e2
        return (ar2 * ar1 - ai2 * ai1,
                ar2 * ai1 + ai2 * ar1,
                ar2 * br1 - ai2 * bi1 + br2,
                ar2 * bi1 + ai2 * br1 + bi2)

    _, _, xs_re, xs_im = lax.associative_scan(combine, (a_full_re, a_full_im, bu_re, bu_im), axis=1)
    y = (jnp.einsum('blgp,gcp->blgc', xs_re, c_re.astype(f32))
         - jnp.einsum('blgp,gcp->blgc', xs_im, c_im.astype(f32)))
    y = y.reshape(bsz, seqlen, SSM_WIDTH) + d_skip.astype(f32) * u.astype(f32)
    y = jax.nn.gelu(y)
    y = y * jax.nn.sigmoid(y @ w_glu.astype(f32))
    return y.astype(u.dtype)


def diff_attention(q, k, v, positions, lam_q1, lam_k1, lam_q2, lam_k2, g_sub, lambda_init):
    bsz, seqlen, _ = q.shape
    q = q.reshape(bsz, seqlen, 2 * DA_HEADS, DA_QKDIM).transpose(0, 2, 1, 3)
    k = k.reshape(bsz, seqlen, 2 * DA_HEADS, DA_QKDIM).transpose(0, 2, 1, 3)
    v = v.reshape(bsz, seqlen, DA_HEADS, DA_VDIM).transpose(0, 2, 1, 3)
    q = rope(q, positions) * (DA_QKDIM ** -0.5)
    k = rope(k, positions)
    f32 = jnp.float32
    lam = (jnp.exp(jnp.sum(lam_q1.astype(f32) * lam_k1.astype(f32)))
           - jnp.exp(jnp.sum(lam_q2.astype(f32) * lam_k2.astype(f32))) + lambda_init)
    n_blocks = seqlen // Q_BLOCK
    qb = q.reshape(bsz, 2 * DA_HEADS, n_blocks, Q_BLOCK, DA_QKDIM).transpose(2, 0, 1, 3, 4)
    kpos = jnp.arange(seqlen)

    def block(args):
        qi, bi = args
        s = jnp.einsum('bhqd,bhkd->bhqk', qi, k).astype(f32)
        qpos = bi * Q_BLOCK + jnp.arange(Q_BLOCK)
        s = jnp.where(kpos[None, :] <= qpos[:, None], s, jnp.finfo(f32).min)
        p = jax.nn.softmax(s, axis=-1).reshape(bsz, DA_HEADS, 2, Q_BLOCK, seqlen)
        w = p[:, :, 0] - lam * p[:, :, 1]
        return jnp.einsum('bhqk,bhkd->bhqd', w.astype(v.dtype), v)

    o = lax.map(block, (qb, jnp.arange(n_blocks)))
    o = o.transpose(1, 0, 3, 2, 4).reshape(bsz, seqlen, DA_HEADS, DA_VDIM)
    o = rmsnorm(o, g_sub) * (1.0 - lambda_init)
    return o.reshape(bsz, seqlen, ATTN_WIDTH)


def cross_attention(hq, m, wq, wk, wv, wo):
    bsz, seqlen, _ = hq.shape
    q = (hq @ wq).reshape(bsz, seqlen, X_HEADS, X_HEAD_DIM)
    k = (m @ wk).reshape(bsz, m.shape[1], X_HEADS, X_HEAD_DIM)
    v = (m @ wv).reshape(bsz, m.shape[1], X_HEADS, X_HEAD_DIM)
    s = jnp.einsum('blhd,bmhd->bhlm', q, k).astype(jnp.float32) * (X_HEAD_DIM ** -0.5)
    p = jax.nn.softmax(s, axis=-1)
    o = jnp.einsum('bhlm,bmhd->blhd', p.astype(v.dtype), v).reshape(bsz, seqlen, D_MODEL)
    return o @ wo


def moe(hn, w_router, router_bias, w_gate, w_up, w_down, ws_gate, ws_up, ws_down):
    bsz, seqlen, d = hn.shape
    n_tok = bsz * seqlen
    t = hn.reshape(n_tok, d)
    f32 = jnp.float32
    scores = jax.nn.sigmoid((t @ w_router).astype(f32))
    biased = scores + router_bias.astype(f32)
    per_group = N_EXPERTS // N_EXPERT_GROUPS
    grp_score = lax.top_k(biased.reshape(n_tok, N_EXPERT_GROUPS, per_group), 2)[0].sum(-1)
    _, gidx = lax.top_k(grp_score, TOPK_GROUPS)
    gmask = jnp.any(gidx[..., None] == jnp.arange(N_EXPERT_GROUPS), axis=1)
    masked = jnp.where(jnp.repeat(gmask, per_group, axis=1), biased, -jnp.inf)
    _, eidx = lax.top_k(masked, TOP_K)
    gw = jnp.take_along_axis(scores, eidx, axis=1)
    gw = gw / jnp.sum(gw, axis=-1, keepdims=True) * ROUTED_SCALE

    n_assign = n_tok * TOP_K
    e_flat = eidx.reshape(n_assign)
    tok_flat = jnp.arange(n_assign, dtype=jnp.int32) // TOP_K
    g_flat = gw.reshape(n_assign)
    order = jnp.argsort(e_flat)
    e_sorted, tok_sorted, g_sorted = e_flat[order], tok_flat[order], g_flat[order]
    counts = jnp.bincount(e_flat, length=N_EXPERTS)
    start = jnp.cumsum(counts) - counts
    padded = ((counts + MOE_BLOCK - 1) // MOE_BLOCK) * MOE_BLOCK
    pend = jnp.cumsum(padded)
    pstart = pend - padded
    dest = pstart[e_sorted] + jnp.arange(n_assign) - start[e_sorted]
    n_slots = n_assign + N_EXPERTS * MOE_BLOCK
    n_blocks = n_slots // MOE_BLOCK
    tok_buf = jnp.full((n_slots,), n_tok, jnp.int32).at[dest].set(tok_sorted)
    g_buf = jnp.zeros((n_slots,), f32).at[dest].set(g_sorted)
    block_e = jnp.minimum(jnp.searchsorted(pend, jnp.arange(n_blocks) * MOE_BLOCK, side='right'),
                          N_EXPERTS - 1)
    t_pad = jnp.concatenate([t, jnp.zeros((1, d), t.dtype)], axis=0)

    def expert_block(args):
        tok, g, e = args
        xb = t_pad[tok]
        hid = jax.nn.silu(xb @ w_gate[e]) * (xb @ w_up[e])
        return (hid @ w_down[e]) * g[:, None].astype(xb.dtype)

    yb = lax.map(expert_block, (tok_buf.reshape(n_blocks, MOE_BLOCK),
                                g_buf.reshape(n_blocks, MOE_BLOCK), block_e))
    routed = jax.ops.segment_sum(yb.reshape(n_slots, d), tok_buf, num_segments=n_tok + 1)[:n_tok]
    shared = (jax.nn.silu(t @ ws_gate) * (t @ ws_up)) @ ws_down
    return (routed + shared).reshape(bsz, seqlen, d)


def setup_inputs(seed: int = 0) -> dict:
    key = jax.random.key(seed)
    ks = iter(jax.random.split(key, 64))
    f32 = jnp.float32

    def nrm(shape, scale):
        return jax.random.normal(next(ks), shape, f32) * scale

    def gain(shape):
        return 1.0 + 0.02 * jax.random.normal(next(ks), shape, f32)

    L, G, P, C = DEPTH, SSM_GROUPS, SSM_STATE, SSM_GROUP
    inp = {}
    inp["x"] = jax.random.normal(next(ks), (BATCH, SEQ, D_MODEL), f32)
    inp["mem"] = jax.random.normal(next(ks), (BATCH, MEM_LEN, D_MODEL), f32)
    offs = jax.random.randint(next(ks), (BATCH, 1), 0, 1024, jnp.int32)
    inp["positions"] = (jnp.arange(SEQ, dtype=jnp.int32)[None, :] + offs).astype(jnp.int32)
    inp["g_mix"] = gain((L, D_MODEL))
    inp["w_in"] = nrm((L, D_MODEL, SSM_WIDTH + 3 * ATTN_WIDTH), D_MODEL ** -0.5)
    inp["a_re"] = -0.5 + 0.01 * jax.random.normal(next(ks), (L, G, P), f32)
    inp["a_im"] = jnp.pi * jnp.arange(P, dtype=f32) + 0.01 * jax.random.normal(next(ks), (L, G, P), f32)
    inp["log_dt"] = jax.random.uniform(next(ks), (L, G), f32, math.log(DT_MIN), math.log(DT_MAX))
    inp["b_re"] = nrm((L, G, P, C), (2.0 * C) ** -0.5)
    inp["b_im"] = nrm((L, G, P, C), (2.0 * C) ** -0.5)
    inp["c_re"] = nrm((L, G, C, P), (2.0 * P) ** -0.5)
    inp["c_im"] = nrm((L, G, C, P), (2.0 * P) ** -0.5)
    inp["d_skip"] = nrm((L, SSM_WIDTH), 1.0)
    inp["w_glu"] = nrm((L, SSM_WIDTH, SSM_WIDTH), SSM_WIDTH ** -0.5)
    inp["g_ssm_out"] = gain((L, SSM_WIDTH))
    inp["lam_q1"] = nrm((L, DA_QKDIM), 0.1)
    inp["lam_k1"] = nrm((L, DA_QKDIM), 0.1)
    inp["lam_q2"] = nrm((L, DA_QKDIM), 0.1)
    inp["lam_k2"] = nrm((L, DA_QKDIM), 0.1)
    inp["g_sub"] = gain((L, DA_VDIM))
    inp["w_out"] = nrm((L, D_MODEL, D_MODEL), D_MODEL ** -0.5)
    inp["g_x"] = gain((L, D_MODEL))
    inp["g_mem"] = gain((L, D_MODEL))
    inp["wq_x"] = nrm((L, D_MODEL, D_MODEL), D_MODEL ** -0.5)
    inp["wk_x"] = nrm((L, D_MODEL, D_MODEL), D_MODEL ** -0.5)
    inp["wv_x"] = nrm((L, D_MODEL, D_MODEL), D_MODEL ** -0.5)
    inp["wo_x"] = nrm((L, D_MODEL, D_MODEL), D_MODEL ** -0.5)
    inp["g_ffn"] = gain((L, D_MODEL))
    inp["w_router"] = nrm((L, D_MODEL, N_EXPERTS), D_MODEL ** -0.5)
    inp["router_bias"] = nrm((L, N_EXPERTS), 0.01)
    inp["w_gate"] = nrm((L, N_EXPERTS, D_MODEL, D_EXPERT), D_MODEL ** -0.5)
    inp["w_up"] = nrm((L, N_EXPERTS, D_MODEL, D_EXPERT), D_MODEL ** -0.5)
    inp["w_down"] = nrm((L, N_EXPERTS, D_EXPERT, D_MODEL), D_EXPERT ** -0.5)
    inp["ws_gate"] = nrm((L, D_MODEL, D_SHARED), D_MODEL ** -0.5)
    inp["ws_up"] = nrm((L, D_MODEL, D_SHARED), D_MODEL ** -0.5)
    inp["ws_down"] = nrm((L, D_SHARED, D_MODEL), D_SHARED ** -0.5)
    inp["g_final"] = gain((D_MODEL,))
    return inp


def reference(x, mem, positions, g_mix, w_in, a_re, a_im, log_dt, b_re, b_im, c_re, c_im,
              d_skip, w_glu, g_ssm_out, lam_q1, lam_k1, lam_q2, lam_k2, g_sub, w_out,
              g_x, g_mem, wq_x, wk_x, wv_x, wo_x, g_ffn, w_router, router_bias,
              w_gate, w_up, w_down, ws_gate, ws_up, ws_down, g_final):
    h = x
    for l in range(DEPTH):
        lambda_init = 0.8 - 0.6 * math.exp(-0.3 * l)
        a = rmsnorm(h, g_mix[l])
        z = a @ w_in[l]
        u = z[..., :SSM_WIDTH]
        q = z[..., SSM_WIDTH:SSM_WIDTH + ATTN_WIDTH]
        k = z[..., SSM_WIDTH + ATTN_WIDTH:SSM_WIDTH + 2 * ATTN_WIDTH]
        v = z[..., SSM_WIDTH + 2 * ATTN_WIDTH:]
        y_ssm = rmsnorm(s5_mixer(u, a_re[l], a_im[l], log_dt[l], b_re[l], b_im[l],
                                 c_re[l], c_im[l], d_skip[l], w_glu[l]), g_ssm_out[l])
        y_att = diff_attention(q, k, v, positions, lam_q1[l], lam_k1[l], lam_q2[l], lam_k2[l],
                               g_sub[l], lambda_init)
        h = h + jnp.concatenate([y_ssm, y_att], axis=-1) @ w_out[l]
        h = h + cross_attention(rmsnorm(h, g_x[l]), rmsnorm(mem, g_mem[l]),
                                wq_x[l], wk_x[l], wv_x[l], wo_x[l])
        h = h + moe(rmsnorm(h, g_ffn[l]), w_router[l], router_bias[l], w_gate[l], w_up[l],
                    w_down[l], ws_gate[l], ws_up[l], ws_down[l])
    return rmsnorm(h, g_final)
```

```python
import functools
import math

import jax
import jax.numpy as jnp
from jax import lax
from jax.experimental import pallas as pl
from jax.experimental.pallas import tpu as pltpu

F32 = jnp.float32
BF16 = jnp.bfloat16
I32 = jnp.int32

D_MODEL = 1024
SSM_WIDTH = 512
ATTN_WIDTH = 512
SSM_GROUP = 16
SSM_GROUPS = 32
SSM_STATE = 64
N_STATE = SSM_GROUPS * SSM_STATE
DA_HEADS = 4
DA_VDIM = 128
DA_QKDIM = 64
ROPE_THETA = 10000.0
X_HEADS = 4
X_HEAD_DIM = 256
N_EXPERTS = 64
TOP_K = 8
N_EXPERT_GROUPS = 8
PER_GROUP = N_EXPERTS // N_EXPERT_GROUPS
TOPK_GROUPS = 4
D_EXPERT = 256
ROUTED_SCALE = 2.5
EPS = 1e-6

VMEM_LIMIT_V7X = 56 * 1024 * 1024
SUBLANES = 8

IN_TILE = 512
SSM_STEPS = 64
SSM_COLS = 512
ATT_TILE = 256
MIX_TILE = 512
ROUTE_TILE = 512
MOE_TILE = 256
FIN_TILE = 256


def _params(sem):
    return pltpu.CompilerParams(dimension_semantics=sem, vmem_limit_bytes=VMEM_LIMIT_V7X)


def _rms(x, g):
    return x * lax.rsqrt(jnp.mean(x * x, axis=-1, keepdims=True) + EPS) * g


def _full(shape):
    return pl.BlockSpec(shape, lambda *_: (0,) * len(shape))


def _inproj_kernel(x_ref, pos_ref, g_ref, w_ref, inv_ref, u_ref, q_ref, k_ref, v_ref):
    x = x_ref[0]
    a = _rms(x, g_ref[...]).astype(BF16)
    z = jnp.dot(a, w_ref[...], preferred_element_type=F32)
    u_ref[0] = z[:, :SSM_WIDTH]
    ang = pos_ref[0].astype(F32) * inv_ref[...]
    cos = jnp.concatenate([jnp.cos(ang)] * 4, axis=1)
    sin = jnp.concatenate([jnp.sin(ang)] * 4, axis=1)
    lane = lax.broadcasted_iota(I32, cos.shape, 1)
    first = (lane & (DA_QKDIM - 1)) < DA_QKDIM // 2
    half = DA_QKDIM // 2

    def rope(t):
        rot = jnp.where(first, -pltpu.roll(t, ATTN_WIDTH - half, 1), pltpu.roll(t, half, 1))
        return t * cos + rot * sin

    q = z[:, SSM_WIDTH:SSM_WIDTH + ATTN_WIDTH]
    k = z[:, SSM_WIDTH + ATTN_WIDTH:SSM_WIDTH + 2 * ATTN_WIDTH]
    q_ref[0] = (rope(q) * (DA_QKDIM ** -0.5)).astype(BF16)
    k_ref[0] = rope(k).astype(BF16)
    v_ref[0] = z[:, SSM_WIDTH + 2 * ATTN_WIDTH:].astype(BF16)


def _inproj(x, pos3, g_mix, w_in, inv_row):
    b, l, d = x.shape
    t = min(IN_TILE, l)
    n_out = w_in.shape[1]
    row = lambda w: pl.BlockSpec((1, t, w), lambda i, j: (i, j, 0))
    return pl.pallas_call(
        _inproj_kernel,
        grid=(b, l // t),
        in_specs=[row(d), row(1), _full((1, d)), _full((d, n_out)), _full((1, 128))],
        out_specs=[row(SSM_WIDTH), row(ATTN_WIDTH), row(ATTN_WIDTH), row(ATTN_WIDTH)],
        out_shape=[jax.ShapeDtypeStruct((b, l, SSM_WIDTH), F32)]
        + [jax.ShapeDtypeStruct((b, l, ATTN_WIDTH), BF16)] * 3,
        compiler_params=_params(("parallel", "parallel")),
        name="inproj",
    )(x, pos3, g_mix, w_in, inv_row)


def _ssm_kernel(u_ref, bm_ref, are_ref, aim_ref, cm_ref, dskip_ref, wglu_ref, g_ref, o_ref, bu_ref, st_ref,
                *, steps):
    @pl.when(pl.program_id(0) == 0)
    def _():
        st_ref[...] = jnp.zeros_like(st_ref)

    u = u_ref[...]
    bu_ref[...] = jnp.dot(u.astype(BF16), bm_ref[...], preferred_element_type=F32)

    for c0 in range(0, N_STATE, SSM_COLS):
        re = slice(c0, c0 + SSM_COLS)
        im = slice(N_STATE + c0, N_STATE + c0 + SSM_COLS)
        ar = are_ref[:, re]
        ai = aim_ref[:, re]

        def step(t, carry, re=re, im=im, ar=ar, ai=ai):
            sr, si = carry
            rows = pl.ds(pl.multiple_of(t * SUBLANES, SUBLANES), SUBLANES)
            nr = ar * sr - ai * si + bu_ref[rows, re]
            ni = ar * si + ai * sr + bu_ref[rows, im]
            bu_ref[rows, re] = nr
            bu_ref[rows, im] = ni
            return nr, ni

        sr, si = lax.fori_loop(0, steps, step, (st_ref[:, re], st_ref[:, im]))
        st_ref[:, re] = sr
        st_ref[:, im] = si

    y = jnp.dot(bu_ref[...].astype(BF16), cm_ref[...], preferred_element_type=F32) + dskip_ref[...] * u
    y = jax.nn.gelu(y)
    y = y * jax.nn.sigmoid(jnp.dot(y.astype(BF16), wglu_ref[...], preferred_element_type=F32))
    o_ref[...] = _rms(y, g_ref[...]).astype(BF16)


def _ssm(u_tm, bmat, a_re8, a_im8, cmat, d_skip, w_glu, g_out, batch):
    assert batch == SUBLANES, "one time step of all sequences must fill the sublanes"
    n = u_tm.shape[0]
    steps = min(SSM_STEPS, n // batch)
    rows = steps * batch
    return pl.pallas_call(
        functools.partial(_ssm_kernel, steps=steps),
        grid=(n // rows,),
        in_specs=[pl.BlockSpec((rows, SSM_WIDTH), lambda i: (i, 0)),
                  _full((SSM_WIDTH, 2 * N_STATE)), _full((SUBLANES, N_STATE)), _full((SUBLANES, N_STATE)),
                  _full((2 * N_STATE, SSM_WIDTH)), _full((1, SSM_WIDTH)), _full((SSM_WIDTH, SSM_WIDTH)),
                  _full((1, SSM_WIDTH))],
        out_specs=pl.BlockSpec((rows, SSM_WIDTH), lambda i: (i, 0)),
        out_shape=jax.ShapeDtypeStruct((n, SSM_WIDTH), BF16),
        scratch_shapes=[pltpu.VMEM((rows, 2 * N_STATE), F32), pltpu.VMEM((SUBLANES, 2 * N_STATE), F32)],
        compiler_params=_params(("arbitrary",)),
        name="ssm",
    )(u_tm, bmat, a_re8, a_im8, cmat, d_skip, w_glu, g_out)


def _attn_kernel(lam_ref, q_ref, k_ref, v_ref, g_ref, o_ref, qs_ref, m_ref, l_ref, acc_ref, *, tile, out_scale):
    qi = pl.program_id(2)
    ki = pl.program_id(3)

    @pl.when(ki == 0)
    def _():
        q = q_ref[0]
        lane = lax.broadcasted_iota(I32, q.shape, 1)
        zero = jnp.zeros_like(q)
        qs_ref[:tile] = jnp.where(lane < DA_QKDIM, q, zero)
        qs_ref[tile:] = jnp.where(lane >= DA_QKDIM, q, zero)
        m_ref[...] = jnp.full_like(m_ref, -jnp.inf)
        l_ref[...] = jnp.zeros_like(l_ref)
        acc_ref[...] = jnp.zeros_like(acc_ref)

    def update(masked):
        s = lax.dot_general(qs_ref[...], k_ref[0], (((1,), (1,)), ((), ())), preferred_element_type=F32)
        if masked:
            row = lax.broadcasted_iota(I32, s.shape, 0) & (tile - 1)
            col = lax.broadcasted_iota(I32, s.shape, 1)
            s = jnp.where(col <= row, s, jnp.finfo(F32).min)
        m_old = m_ref[...]
        m_new = jnp.maximum(m_old, jnp.max(s, axis=1, keepdims=True))
        alpha = jnp.exp(m_old - m_new)
        p = jnp.exp(s - m_new)
        l_ref[...] = alpha * l_ref[...] + jnp.sum(p, axis=1, keepdims=True)
        acc_ref[...] = alpha * acc_ref[...] + jnp.dot(p.astype(BF16), v_ref[0], preferred_element_type=F32)
        m_ref[...] = m_new

    @pl.when(ki < qi)
    def _():
        update(False)

    @pl.when(ki == qi)
    def _():
        update(True)
        o = acc_ref[...] / l_ref[...]
        o = o[:tile] - lam_ref[0] * o[tile:]
        o_ref[0] = (_rms(o, g_ref[...]) * out_scale).astype(BF16)


def _diff_attention(q, k, v, lam, g_sub, lambda_init):
    b, l, _ = q.shape
    t = min(ATT_TILE, l)
    n = l // t
    qspec = pl.BlockSpec((1, t, DA_VDIM), lambda bi, h, qi, ki: (bi, qi, h))
    kspec = pl.BlockSpec((1, t, DA_VDIM), lambda bi, h, qi, ki: (bi, jnp.minimum(ki, qi), h))
    return pl.pallas_call(
        functools.partial(_attn_kernel, tile=t, out_scale=1.0 - lambda_init),
        grid=(b, DA_HEADS, n, n),
        in_specs=[pl.BlockSpec(memory_space=pltpu.SMEM), qspec, kspec, kspec, _full((1, DA_VDIM))],
        out_specs=qspec,
        out_shape=jax.ShapeDtypeStruct((b, l, ATTN_WIDTH), BF16),
        scratch_shapes=[pltpu.VMEM((2 * t, DA_VDIM), BF16), pltpu.VMEM((2 * t, 1), F32),
                        pltpu.VMEM((2 * t, 1), F32), pltpu.VMEM((2 * t, DA_VDIM), F32)],
        compiler_params=_params(("parallel", "parallel", "parallel", "arbitrary")),
        name="diffattn",
    )(lam, q, k, v, g_sub)


def _memkv_kernel(m_ref, g_ref, wk_ref, wv_ref, k_ref, v_ref):
    a = _rms(m_ref[0], g_ref[...]).astype(BF16)
    k_ref[0] = jnp.dot(a, wk_ref[...], preferred_element_type=F32).astype(BF16)
    v_ref[0] = jnp.dot(a, wv_ref[...], preferred_element_type=F32).astype(BF16)


def _memkv(mem, g_mem, wk, wv):
    b, m, d = mem.shape
    blk = pl.BlockSpec((1, m, d), lambda i: (i, 0, 0))
    return pl.pallas_call(
        _memkv_kernel,
        grid=(b,),
        in_specs=[blk, _full((1, d)), _full((d, d)), _full((d, d))],
        out_specs=[blk, blk],
        out_shape=[jax.ShapeDtypeStruct((b, m, d), BF16)] * 2,
        compiler_params=_params(("parallel",)),
        name="memkv",
    )(mem, g_mem, wk, wv)


def _mix_kernel(x_ref, ys_ref, ya_ref, wo1_ref, wo2_ref, gx_ref, wq_ref, km_ref, vm_ref, wo_ref, h_ref):
    h = (x_ref[0]
         + jnp.dot(ys_ref[0], wo1_ref[...], preferred_element_type=F32)
         + jnp.dot(ya_ref[0], wo2_ref[...], preferred_element_type=F32))
    hq = _rms(h, gx_ref[...]).astype(BF16)
    q = jnp.dot(hq, wq_ref[...], preferred_element_type=F32).astype(BF16)
    outs = []
    for hd in range(X_HEADS):
        cols = slice(hd * X_HEAD_DIM, (hd + 1) * X_HEAD_DIM)
        s = lax.dot_general(q[:, cols], km_ref[0, :, cols], (((1,), (1,)), ((), ())),
                            preferred_element_type=F32) * (X_HEAD_DIM ** -0.5)
        s = s - jnp.max(s, axis=1, keepdims=True)
        p = jnp.exp(s)
        p = p / jnp.sum(p, axis=1, keepdims=True)
        outs.append(jnp.dot(p.astype(BF16), vm_ref[0, :, cols], preferred_element_type=F32).astype(BF16))
    o = jnp.concatenate(outs, axis=1)
    h_ref[0] = h + jnp.dot(o, wo_ref[...], preferred_element_type=F32)


def _mix(x, y_ssm, y_att, wo1, wo2, g_x, wq, kmem, vmem, wo):
    b, l, d = x.shape
    t = min(MIX_TILE, l)
    m = kmem.shape[1]
    row = lambda w: pl.BlockSpec((1, t, w), lambda i, j: (i, j, 0))
    mem = pl.BlockSpec((1, m, d), lambda i, j: (i, 0, 0))
    return pl.pallas_call(
        _mix_kernel,
        grid=(b, l // t),
        in_specs=[row(d), row(SSM_WIDTH), row(ATTN_WIDTH), _full((SSM_WIDTH, d)), _full((ATTN_WIDTH, d)),
                  _full((1, d)), _full((d, d)), mem, mem, _full((d, d))],
        out_specs=row(d),
        out_shape=jax.ShapeDtypeStruct((b, l, d), F32),
        compiler_params=_params(("parallel", "parallel")),
        name="mix",
    )(x, y_ssm, y_att, wo1, wo2, g_x, wq, kmem, vmem, wo)


def _first_index(hit, idx, sentinel):
    return jnp.min(jnp.where(hit, idx, sentinel), axis=0, keepdims=True)


def _route_kernel(h_ref, g_ref, wr_ref, bias_ref, wsgu_ref, wsd_ref, tri_ref,
                  t_ref, base_ref, eidx_ref, gw_ref, rank_ref, cnt_ref, carry_ref):
    @pl.when(pl.program_id(0) == 0)
    def _():
        carry_ref[...] = jnp.zeros_like(carry_ref)

    h = h_ref[...]
    t = _rms(h, g_ref[...])
    tb = t.astype(BF16)
    t_ref[...] = tb
    gu = jnp.dot(tb, wsgu_ref[...], preferred_element_type=F32)
    hid = jax.nn.silu(gu[:, :D_EXPERT]) * gu[:, D_EXPERT:]
    base_ref[...] = h + jnp.dot(hid.astype(BF16), wsd_ref[...], preferred_element_type=F32)

    logits = lax.dot_general(wr_ref[...], t, (((1,), (1,)), ((), ())), preferred_element_type=F32,
                             precision=lax.Precision.HIGHEST)
    scores = jax.nn.sigmoid(logits)
    biased = scores + bias_ref[...]
    n_tok = scores.shape[1]
    neg = -jnp.inf
    sub = lax.broadcasted_iota(I32, (PER_GROUP, n_tok), 0)

    gs = []
    for g in range(N_EXPERT_GROUPS):
        blk = biased[g * PER_GROUP:(g + 1) * PER_GROUP]
        m1 = jnp.max(blk, axis=0, keepdims=True)
        i1 = _first_index(blk == m1, sub, PER_GROUP)
        m2 = jnp.max(jnp.where(sub == i1, neg, blk), axis=0, keepdims=True)
        gs.append(m1 + m2)
    gs = jnp.concatenate(gs, axis=0)

    gsel = jnp.zeros(gs.shape, jnp.bool_)
    for _ in range(TOPK_GROUPS):
        m = jnp.max(gs, axis=0, keepdims=True)
        hit = sub == _first_index(gs == m, sub, N_EXPERT_GROUPS)
        gsel = jnp.logical_or(gsel, hit)
        gs = jnp.where(hit, neg, gs)

    masked = jnp.concatenate(
        [jnp.where(gsel[g:g + 1], biased[g * PER_GROUP:(g + 1) * PER_GROUP], neg)
         for g in range(N_EXPERT_GROUPS)], axis=0)
    eid = lax.broadcasted_iota(I32, masked.shape, 0)
    sel = jnp.zeros(masked.shape, jnp.bool_)
    idxs, gws = [], []
    for _ in range(TOP_K):
        m = jnp.max(masked, axis=0, keepdims=True)
        i = _first_index(masked == m, eid, N_EXPERTS)
        hit = eid == i
        idxs.append(i)
        gws.append(jnp.sum(jnp.where(hit, scores, 0.0), axis=0, keepdims=True))
        sel = jnp.logical_or(sel, hit)
        masked = jnp.where(hit, neg, masked)
    eidx = jnp.concatenate(idxs, axis=0)
    gw = jnp.concatenate(gws, axis=0)
    gw = gw / jnp.sum(gw, axis=0, keepdims=True) * ROUTED_SCALE
    eidx_ref[...] = eidx
    gw_ref[...] = gw

    before = jnp.dot(sel.astype(BF16), tri_ref[...], preferred_element_type=F32) + carry_ref[...]
    rank_ref[...] = jnp.concatenate(
        [jnp.sum(jnp.where(eid == idxs[k], before, 0.0), axis=0, keepdims=True) for k in range(TOP_K)],
        axis=0).astype(I32)
    carry = carry_ref[...] + jnp.sum(sel.astype(F32), axis=1, keepdims=True)
    carry_ref[...] = carry
    cnt_ref[...] = carry.astype(I32)


def _route(h2, g_ffn, wr_t, bias_col, wsgu, wsd):
    n, d = h2.shape
    t = min(ROUTE_TILE, n)
    tri = (lax.broadcasted_iota(I32, (t, t), 0) < lax.broadcasted_iota(I32, (t, t), 1)).astype(BF16)
    row = lambda w, dt=None: pl.BlockSpec((t, w), lambda i: (i, 0))
    col = pl.BlockSpec((TOP_K, t), lambda i: (0, i))
    return pl.pallas_call(
        _route_kernel,
        grid=(n // t,),
        in_specs=[row(d), _full((1, d)), _full((N_EXPERTS, d)), _full((N_EXPERTS, 1)),
                  _full((d, 2 * D_EXPERT)), _full((D_EXPERT, d)), _full((t, t))],
        out_specs=[row(d), row(d), col, col, col, _full((N_EXPERTS, 1))],
        out_shape=[jax.ShapeDtypeStruct((n, d), BF16), jax.ShapeDtypeStruct((n, d), F32),
                   jax.ShapeDtypeStruct((TOP_K, n), I32), jax.ShapeDtypeStruct((TOP_K, n), F32),
                   jax.ShapeDtypeStruct((TOP_K, n), I32), jax.ShapeDtypeStruct((N_EXPERTS, 1), I32)],
        scratch_shapes=[pltpu.VMEM((N_EXPERTS, 1), F32)],
        compiler_params=_params(("arbitrary",)),
        name="route",
    )(h2, g_ffn, wr_t, bias_col, wsgu, wsd, tri)


def _expert_kernel(be_ref, nu_ref, x_ref, wgu_ref, wd_ref, y_ref):
    @pl.when(pl.program_id(0) < nu_ref[0])
    def _():
        gu = jnp.dot(x_ref[...], wgu_ref[0], preferred_element_type=F32)
        hid = jax.nn.silu(gu[:, :D_EXPERT]) * gu[:, D_EXPERT:]
        y_ref[...] = jnp.dot(hid.astype(BF16), wd_ref[0], preferred_element_type=F32).astype(BF16)


def _experts(block_e, n_used, xs, wgu, wd):
    n_slots, d = xs.shape
    nb = n_slots // MOE_TILE
    rows = pl.BlockSpec((MOE_TILE, d), lambda i, be, nu: (jnp.minimum(i, nu[0] - 1), 0))
    return pl.pallas_call(
        _expert_kernel,
        grid_spec=pltpu.PrefetchScalarGridSpec(
            num_scalar_prefetch=2,
            grid=(nb,),
            in_specs=[rows,
                      pl.BlockSpec((1, d, 2 * D_EXPERT), lambda i, be, nu: (be[i], 0, 0)),
                      pl.BlockSpec((1, D_EXPERT, d), lambda i, be, nu: (be[i], 0, 0))],
            out_specs=rows),
        out_shape=jax.ShapeDtypeStruct((n_slots, d), BF16),
        compiler_params=_params(("arbitrary",)),
        name="experts",
    )(block_e, n_used, xs, wgu, wd)


def _final_kernel(base_ref, y_ref, gw_ref, g_ref, o_ref, *, normalize):
    h = base_ref[...]
    gw = gw_ref[...]
    for k in range(TOP_K):
        h = h + gw[:, k:k + 1] * y_ref[k].astype(F32)
    o_ref[...] = _rms(h, g_ref[...]) if normalize else h


def _final(base, yg, gw_rows, g_final, normalize):
    n, d = base.shape
    t = min(FIN_TILE, n)
    return pl.pallas_call(
        functools.partial(_final_kernel, normalize=normalize),
        grid=(n // t,),
        in_specs=[pl.BlockSpec((t, d), lambda i: (i, 0)), pl.BlockSpec((TOP_K, t, d), lambda i: (0, i, 0)),
                  pl.BlockSpec((t, TOP_K), lambda i: (i, 0)), _full((1, d))],
        out_specs=pl.BlockSpec((t, d), lambda i: (i, 0)),
        out_shape=jax.ShapeDtypeStruct((n, d), F32),
        compiler_params=_params(("parallel",)),
        name="final",
    )(base, yg, gw_rows, g_final)


def _ssm_matrices(a_re, a_im, log_dt, b_re, b_im, c_re, c_im):
    lr, li = a_re.astype(F32), a_im.astype(F32)
    dt = jnp.exp(log_dt.astype(F32))[:, None]
    mag = jnp.exp(lr * dt)
    ab_re, ab_im = mag * jnp.cos(li * dt), mag * jnp.sin(li * dt)
    den = lr * lr + li * li
    zr, zi = ab_re - 1.0, ab_im
    k_re = (zr * lr + zi * li) / den
    k_im = (zi * lr - zr * li) / den
    br, bi = b_re.astype(F32), b_im.astype(F32)
    bb_re = k_re[..., None] * br - k_im[..., None] * bi
    bb_im = k_re[..., None] * bi + k_im[..., None] * br
    eye = jnp.eye(SSM_GROUPS, dtype=F32)

    def in_mat(bb):
        return jnp.einsum("gpc,gh->gchp", bb, eye).reshape(SSM_WIDTH, N_STATE)

    def out_mat(c):
        return jnp.einsum("gcp,gh->gphc", c.astype(F32), eye).reshape(N_STATE, SSM_WIDTH)

    bmat = jnp.concatenate([in_mat(bb_re), in_mat(bb_im)], axis=1)
    cmat = jnp.concatenate([out_mat(c_re), -out_mat(c_im)], axis=0)
    tile = lambda a: jnp.broadcast_to(a.reshape(1, N_STATE), (SUBLANES, N_STATE))
    return bmat.astype(BF16), tile(ab_re), tile(ab_im), cmat.astype(BF16)


def _dispatch_plan(eidx, rank, counts, n_tok):
    padded = ((counts + MOE_TILE - 1) // MOE_TILE) * MOE_TILE
    pend = jnp.cumsum(padded)
    pstart = pend - padded
    dest = pstart[eidx] + rank
    n_slots = n_tok * TOP_K + N_EXPERTS * MOE_TILE
    nb = n_slots // MOE_TILE
    n_used = (pend[-1] // MOE_TILE).astype(I32)
    blk = jnp.minimum(jnp.arange(nb, dtype=I32), n_used - 1)
    block_e = jnp.minimum(jnp.searchsorted(pend, blk * MOE_TILE, side="right"), N_EXPERTS - 1).astype(I32)
    return dest.astype(I32), block_e, n_used.reshape(1), n_slots


def kernel(x, mem, positions, g_mix, w_in, a_re, a_im, log_dt, b_re, b_im, c_re, c_im, d_skip, w_glu, g_ssm_out, lam_q1, lam_k1, lam_q2, lam_k2, g_sub, w_out, g_x, g_mem, wq_x, wk_x, wv_x, wo_x, g_ffn, w_router, router_bias, w_gate, w_up, w_down, ws_gate, ws_up, ws_down, g_final):
    b, l, d = x.shape
    n = b * l
    depth = w_in.shape[0]
    row = lambda a: a.reshape(1, -1).astype(F32)
    inv = ROPE_THETA ** (-jnp.arange(0, DA_QKDIM, 2, dtype=F32) / DA_QKDIM)
    inv_row = jnp.tile(inv, 128 // inv.shape[0]).reshape(1, 128)
    pos3 = positions.reshape(b, l, 1)

    h = x
    for i in range(depth):
        lambda_init = 0.8 - 0.6 * math.exp(-0.3 * i)
        u, q, k, v = _inproj(h, pos3, row(g_mix[i]), w_in[i].astype(BF16), inv_row)

        bmat, are8, aim8, cmat = _ssm_matrices(a_re[i], a_im[i], log_dt[i], b_re[i], b_im[i], c_re[i], c_im[i])
        u_tm = u.transpose(1, 0, 2).reshape(n, SSM_WIDTH)
        y_ssm = _ssm(u_tm, bmat, are8, aim8, cmat, row(d_skip[i]), w_glu[i].astype(BF16), row(g_ssm_out[i]), b)
        y_ssm = y_ssm.reshape(l, b, SSM_WIDTH).transpose(1, 0, 2)

        lam = (jnp.exp(jnp.sum(lam_q1[i].astype(F32) * lam_k1[i].astype(F32)))
               - jnp.exp(jnp.sum(lam_q2[i].astype(F32) * lam_k2[i].astype(F32))) + lambda_init).reshape(1)
        y_att = _diff_attention(q, k, v, lam.astype(F32), row(g_sub[i]), lambda_init)

        kmem, vmem = _memkv(mem, row(g_mem[i]), wk_x[i].astype(BF16), wv_x[i].astype(BF16))
        wo = w_out[i].astype(BF16)
        h2 = _mix(h, y_ssm, y_att, wo[:SSM_WIDTH], wo[SSM_WIDTH:], row(g_x[i]), wq_x[i].astype(BF16),
                  kmem, vmem, wo_x[i].astype(BF16))

        wsgu = jnp.concatenate([ws_gate[i], ws_up[i]], axis=1).astype(BF16)
        t_bf, base, eidx, gw, rank, counts = _route(
            h2.reshape(n, d), row(g_ffn[i]), w_router[i].T.astype(F32),
            router_bias[i].reshape(N_EXPERTS, 1).astype(F32), wsgu, ws_down[i].astype(BF16))

        dest, block_e, n_used, n_slots = _dispatch_plan(eidx, rank, counts[:, 0], n)
        flat = dest.reshape(-1)
        xs = jnp.zeros((n_slots, d), BF16).at[flat].set(jnp.tile(t_bf, (TOP_K, 1)))
        wgu = jnp.concatenate([w_gate[i], w_up[i]], axis=2).astype(BF16)
        ys = _experts(block_e, n_used, xs, wgu, w_down[i].astype(BF16))
        yg = ys[flat].reshape(TOP_K, n, d)
        h = _final(base, yg, gw.T, row(g_final), normalize=i == depth - 1)
        h = h.reshape(b, l, d)
    return h
```

```python
import functools
import math

import jax
import jax.numpy as jnp
from jax import lax
from jax.experimental import pallas as pl
from jax.experimental.pallas import tpu as pltpu
from jax.experimental.pallas import tpu_sc as plsc

F32 = jnp.float32
BF16 = jnp.bfloat16
I32 = jnp.int32

D_MODEL = 1024
SSM_WIDTH = 512
ATTN_WIDTH = 512
SSM_GROUP = 16
SSM_GROUPS = 32
SSM_STATE = 64
N_STATE = SSM_GROUPS * SSM_STATE
DA_HEADS = 4
DA_VDIM = 128
DA_QKDIM = 64
ROPE_THETA = 10000.0
X_HEADS = 4
X_HEAD_DIM = 256
N_EXPERTS = 64
TOP_K = 8
N_EXPERT_GROUPS = 8
PER_GROUP = N_EXPERTS // N_EXPERT_GROUPS
TOPK_GROUPS = 4
D_EXPERT = 256
ROUTED_SCALE = 2.5
EPS = 1e-6

VMEM_LIMIT_V7X = 56 * 1024 * 1024
SUBLANES = 8

IN_TILE = 512
SSM_STEPS = 64
SSM_COLS = 512
ATT_TILE = 512
MIX_TILE = 512
ROUTE_TILE = 512
MOE_TILE = 256
FIN_TILE = 256

SC_CORES = 2
SC_WORKERS = 32
SC_WINDOW = 64


def _params(sem):
    return pltpu.CompilerParams(dimension_semantics=sem, vmem_limit_bytes=VMEM_LIMIT_V7X)


def _rms(x, g):
    return x * lax.rsqrt(jnp.mean(x * x, axis=-1, keepdims=True) + EPS) * g


def _full(shape):
    return pl.BlockSpec(shape, lambda *_: (0,) * len(shape))


def _pack_rows(a):
    w = a.shape[1] // 2
    bits = lambda v: lax.bitcast_convert_type(v.astype(BF16).astype(F32), I32)
    return (bits(a[:, w:]) & jnp.int32(-65536)) | lax.shift_right_logical(bits(a[:, :w]), 16)


def _unpack_rows(p):
    lo = lax.bitcast_convert_type(lax.shift_left(p, 16), F32)
    hi = lax.bitcast_convert_type(p & jnp.int32(-65536), F32)
    return jnp.concatenate([lo, hi], axis=1)


def _inproj_kernel(x_ref, pos_ref, g_ref, w_ref, inv_ref, u_ref, q_ref, k_ref, v_ref):
    x = x_ref[0]
    a = _rms(x, g_ref[...]).astype(BF16)
    z = jnp.dot(a, w_ref[...], preferred_element_type=F32)
    u_ref[0] = z[:, :SSM_WIDTH]
    ang = pos_ref[0].astype(F32) * inv_ref[...]
    cos = jnp.concatenate([jnp.cos(ang)] * 4, axis=1)
    sin = jnp.concatenate([jnp.sin(ang)] * 4, axis=1)
    lane = lax.broadcasted_iota(I32, cos.shape, 1)
    first = (lane & (DA_QKDIM - 1)) < DA_QKDIM // 2
    half = DA_QKDIM // 2

    def rope(t):
        rot = jnp.where(first, -pltpu.roll(t, ATTN_WIDTH - half, 1), pltpu.roll(t, half, 1))
        return t * cos + rot * sin

    q = z[:, SSM_WIDTH:SSM_WIDTH + ATTN_WIDTH]
    k = z[:, SSM_WIDTH + ATTN_WIDTH:SSM_WIDTH + 2 * ATTN_WIDTH]
    q_ref[0] = (rope(q) * (DA_QKDIM ** -0.5)).astype(BF16)
    k_ref[0] = rope(k).astype(BF16)
    v_ref[0] = z[:, SSM_WIDTH + 2 * ATTN_WIDTH:].astype(BF16)


def _inproj(x, pos3, g_mix, w_in, inv_row):
    b, l, d = x.shape
    t = min(IN_TILE, l)
    n_out = w_in.shape[1]
    row = lambda w: pl.BlockSpec((1, t, w), lambda i, j: (i, j, 0))
    return pl.pallas_call(
        _inproj_kernel,
        grid=(b, l // t),
        in_specs=[row(d), row(1), _full((1, d)), _full((d, n_out)), _full((1, 128))],
        out_specs=[row(SSM_WIDTH), row(ATTN_WIDTH), row(ATTN_WIDTH), row(ATTN_WIDTH)],
        out_shape=[jax.ShapeDtypeStruct((b, l, SSM_WIDTH), F32)]
        + [jax.ShapeDtypeStruct((b, l, ATTN_WIDTH), BF16)] * 3,
        compiler_params=_params(("parallel", "parallel")),
        name="inproj",
    )(x, pos3, g_mix, w_in, inv_row)


def _ssm_kernel(u_ref, bm_ref, are_ref, aim_ref, cm_ref, dskip_ref, wglu_ref, g_ref, o_ref, bu_ref, st_ref,
                *, steps):
    @pl.when(pl.program_id(0) == 0)
    def _():
        st_ref[...] = jnp.zeros_like(st_ref)

    u = u_ref[...]
    bu_ref[...] = jnp.dot(u.astype(BF16), bm_ref[...], preferred_element_type=F32)

    for c0 in range(0, N_STATE, SSM_COLS):
        re = slice(c0, c0 + SSM_COLS)
        im = slice(N_STATE + c0, N_STATE + c0 + SSM_COLS)
        ar = are_ref[:, re]
        ai = aim_ref[:, re]

        def step(t, carry, re=re, im=im, ar=ar, ai=ai):
            sr, si = carry
            rows = pl.ds(pl.multiple_of(t * SUBLANES, SUBLANES), SUBLANES)
            nr = ar * sr - ai * si + bu_ref[rows, re]
            ni = ar * si + ai * sr + bu_ref[rows, im]
            bu_ref[rows, re] = nr
            bu_ref[rows, im] = ni
            return nr, ni

        sr, si = lax.fori_loop(0, steps, step, (st_ref[:, re], st_ref[:, im]))
        st_ref[:, re] = sr
        st_ref[:, im] = si

    y = jnp.dot(bu_ref[...].astype(BF16), cm_ref[...], preferred_element_type=F32) + dskip_ref[...] * u
    y = jax.nn.gelu(y)
    y = y * jax.nn.sigmoid(jnp.dot(y.astype(BF16), wglu_ref[...], preferred_element_type=F32))
    o_ref[...] = _rms(y, g_ref[...]).astype(BF16)


def _ssm(u_tm, bmat, a_re8, a_im8, cmat, d_skip, w_glu, g_out, batch):
    assert batch == SUBLANES, "one time step of all sequences must fill the sublanes"
    n = u_tm.shape[0]
    steps = min(SSM_STEPS, n // batch)
    rows = steps * batch
    return pl.pallas_call(
        functools.partial(_ssm_kernel, steps=steps),
        grid=(n // rows,),
        in_specs=[pl.BlockSpec((rows, SSM_WIDTH), lambda i: (i, 0)),
                  _full((SSM_WIDTH, 2 * N_STATE)), _full((SUBLANES, N_STATE)), _full((SUBLANES, N_STATE)),
                  _full((2 * N_STATE, SSM_WIDTH)), _full((1, SSM_WIDTH)), _full((SSM_WIDTH, SSM_WIDTH)),
                  _full((1, SSM_WIDTH))],
        out_specs=pl.BlockSpec((rows, SSM_WIDTH), lambda i: (i, 0)),
        out_shape=jax.ShapeDtypeStruct((n, SSM_WIDTH), BF16),
        scratch_shapes=[pltpu.VMEM((rows, 2 * N_STATE), F32), pltpu.VMEM((SUBLANES, 2 * N_STATE), F32)],
        compiler_params=_params(("arbitrary",)),
        name="ssm",
    )(u_tm, bmat, a_re8, a_im8, cmat, d_skip, w_glu, g_out)


def _attn_kernel(lam_ref, q_ref, k_ref, v_ref, g_ref, o_ref, qs_ref, m_ref, acc_ref, *, tile, out_scale):
    qi = pl.program_id(2)
    ki = pl.program_id(3)

    @pl.when(ki == 0)
    def _():
        q = q_ref[0]
        lane = lax.broadcasted_iota(I32, q.shape, 1)
        zero = jnp.zeros_like(q)
        qs_ref[:tile] = jnp.where(lane < DA_QKDIM, q, zero)
        qs_ref[tile:] = jnp.where(lane >= DA_QKDIM, q, zero)
        m_ref[...] = jnp.full_like(m_ref, -jnp.inf)
        acc_ref[...] = jnp.zeros_like(acc_ref)

    def update(masked):
        s = lax.dot_general(qs_ref[...], k_ref[0], (((1,), (1,)), ((), ())), preferred_element_type=F32)
        if masked:
            row = lax.broadcasted_iota(I32, s.shape, 0) & (tile - 1)
            col = lax.broadcasted_iota(I32, s.shape, 1)
            s = jnp.where(col <= row, s, jnp.finfo(F32).min)
        m_old = m_ref[...]
        m_new = jnp.maximum(m_old, jnp.max(s, axis=1, keepdims=True))
        p = jnp.exp(s - jnp.concatenate([m_new] * (tile // DA_VDIM), axis=1))
        alpha = jnp.exp(m_old - m_new)
        v_ext = jnp.concatenate([v_ref[0], jnp.ones((tile, DA_VDIM), BF16)], axis=1)
        acc_ref[...] = (jnp.concatenate([alpha, alpha], axis=1) * acc_ref[...]
                        + jnp.dot(p.astype(BF16), v_ext, preferred_element_type=F32))
        m_ref[...] = m_new

    @pl.when(ki < qi)
    def _():
        update(False)

    @pl.when(ki == qi)
    def _():
        update(True)
        o = acc_ref[:, :DA_VDIM] / acc_ref[:, DA_VDIM:]
        o = o[:tile] - lam_ref[0] * o[tile:]
        o_ref[0] = (_rms(o, g_ref[...]) * out_scale).astype(BF16)


def _diff_attention(q, k, v, lam, g_sub, lambda_init):
    b, l, _ = q.shape
    t = min(ATT_TILE, l)
    n = l // t
    qspec = pl.BlockSpec((1, t, DA_VDIM), lambda bi, h, qi, ki: (bi, qi, h))
    kspec = pl.BlockSpec((1, t, DA_VDIM), lambda bi, h, qi, ki: (bi, jnp.minimum(ki, qi), h))
    return pl.pallas_call(
        functools.partial(_attn_kernel, tile=t, out_scale=1.0 - lambda_init),
        grid=(b, DA_HEADS, n, n),
        in_specs=[pl.BlockSpec(memory_space=pltpu.SMEM), qspec, kspec, kspec, _full((1, DA_VDIM))],
        out_specs=qspec,
        out_shape=jax.ShapeDtypeStruct((b, l, ATTN_WIDTH), BF16),
        scratch_shapes=[pltpu.VMEM((2 * t, DA_VDIM), BF16), pltpu.VMEM((2 * t, DA_VDIM), F32),
                        pltpu.VMEM((2 * t, 2 * DA_VDIM), F32)],
        compiler_params=_params(("parallel", "parallel", "parallel", "arbitrary")),
        name="diffattn",
    )(lam, q, k, v, g_sub)


def _memkv_kernel(m_ref, g_ref, wk_ref, wv_ref, k_ref, v_ref):
    a = _rms(m_ref[0], g_ref[...]).astype(BF16)
    k_ref[0] = jnp.dot(a, wk_ref[...], preferred_element_type=F32).astype(BF16)
    v_ref[0] = jnp.dot(a, wv_ref[...], preferred_element_type=F32).astype(BF16)


def _memkv(mem, g_mem, wk, wv):
    b, m, d = mem.shape
    blk = pl.BlockSpec((1, m, d), lambda i: (i, 0, 0))
    return pl.pallas_call(
        _memkv_kernel,
        grid=(b,),
        in_specs=[blk, _full((1, d)), _full((d, d)), _full((d, d))],
        out_specs=[blk, blk],
        out_shape=[jax.ShapeDtypeStruct((b, m, d), BF16)] * 2,
        compiler_params=_params(("parallel",)),
        name="memkv",
    )(mem, g_mem, wk, wv)


def _mix_kernel(x_ref, ys_ref, ya_ref, wo1_ref, wo2_ref, gx_ref, wq_ref, km_ref, vm_ref, wo_ref, h_ref):
    h = (x_ref[0]
         + jnp.dot(ys_ref[0], wo1_ref[...], preferred_element_type=F32)
         + jnp.dot(ya_ref[0], wo2_ref[...], preferred_element_type=F32))
    hq = _rms(h, gx_ref[...]).astype(BF16)
    q = jnp.dot(hq, wq_ref[...], preferred_element_type=F32).astype(BF16)
    outs = []
    for hd in range(X_HEADS):
        cols = slice(hd * X_HEAD_DIM, (hd + 1) * X_HEAD_DIM)
        s = lax.dot_general(q[:, cols], km_ref[0, :, cols], (((1,), (1,)), ((), ())),
                            preferred_element_type=F32) * (X_HEAD_DIM ** -0.5)
        s = s - jnp.max(s, axis=1, keepdims=True)
        p = jnp.exp(s)
        p = p / jnp.sum(p, axis=1, keepdims=True)
        outs.append(jnp.dot(p.astype(BF16), vm_ref[0, :, cols], preferred_element_type=F32).astype(BF16))
    o = jnp.concatenate(outs, axis=1)
    h_ref[0] = h + jnp.dot(o, wo_ref[...], preferred_element_type=F32)


def _mix(x, y_ssm, y_att, wo1, wo2, g_x, wq, kmem, vmem, wo):
    b, l, d = x.shape
    t = min(MIX_TILE, l)
    m = kmem.shape[1]
    row = lambda w: pl.BlockSpec((1, t, w), lambda i, j: (i, j, 0))
    mem = pl.BlockSpec((1, m, d), lambda i, j: (i, 0, 0))
    return pl.pallas_call(
        _mix_kernel,
        grid=(b, l // t),
        in_specs=[row(d), row(SSM_WIDTH), row(ATTN_WIDTH), _full((SSM_WIDTH, d)), _full((ATTN_WIDTH, d)),
                  _full((1, d)), _full((d, d)), mem, mem, _full((d, d))],
        out_specs=row(d),
        out_shape=jax.ShapeDtypeStruct((b, l, d), F32),
        compiler_params=_params(("parallel", "parallel")),
        name="mix",
    )(x, y_ssm, y_att, wo1, wo2, g_x, wq, kmem, vmem, wo)


def _first_index(hit, idx, sentinel):
    return jnp.min(jnp.where(hit, idx, sentinel), axis=0, keepdims=True)


def _route_kernel(h_ref, g_ref, wr_ref, bias_ref, wsgu_ref, wsd_ref, tri_ref,
                  t_ref, base_ref, eidx_ref, gw_ref, rank_ref, cnt_ref, carry_ref):
    @pl.when(pl.program_id(0) == 0)
    def _():
        carry_ref[...] = jnp.zeros_like(carry_ref)

    h = h_ref[...]
    t = _rms(h, g_ref[...])
    tb = t.astype(BF16)
    t_ref[...] = _pack_rows(t)
    gu = jnp.dot(tb, wsgu_ref[...], preferred_element_type=F32)
    hid = jax.nn.silu(gu[:, :D_EXPERT]) * gu[:, D_EXPERT:]
    base_ref[...] = h + jnp.dot(hid.astype(BF16), wsd_ref[...], preferred_element_type=F32)

    logits = lax.dot_general(wr_ref[...], t, (((1,), (1,)), ((), ())), preferred_element_type=F32,
                             precision=lax.Precision.HIGHEST)
    scores = jax.nn.sigmoid(logits)
    biased = scores + bias_ref[...]
    n_tok = scores.shape[1]
    neg = -jnp.inf
    sub = lax.broadcasted_iota(I32, (PER_GROUP, n_tok), 0)

    gs = []
    for g in range(N_EXPERT_GROUPS):
        blk = biased[g * PER_GROUP:(g + 1) * PER_GROUP]
        m1 = jnp.max(blk, axis=0, keepdims=True)
        i1 = _first_index(blk == m1, sub, PER_GROUP)
        m2 = jnp.max(jnp.where(sub == i1, neg, blk), axis=0, keepdims=True)
        gs.append(m1 + m2)
    gs = jnp.concatenate(gs, axis=0)

    gsel = jnp.zeros(gs.shape, jnp.bool_)
    for _ in range(TOPK_GROUPS):
        m = jnp.max(gs, axis=0, keepdims=True)
        hit = sub == _first_index(gs == m, sub, N_EXPERT_GROUPS)
        gsel = jnp.logical_or(gsel, hit)
        gs = jnp.where(hit, neg, gs)

    masked = jnp.concatenate(
        [jnp.where(gsel[g:g + 1], biased[g * PER_GROUP:(g + 1) * PER_GROUP], neg)
         for g in range(N_EXPERT_GROUPS)], axis=0)
    eid = lax.broadcasted_iota(I32, masked.shape, 0)
    sel = jnp.zeros(masked.shape, jnp.bool_)
    idxs, gws = [], []
    for _ in range(TOP_K):
        m = jnp.max(masked, axis=0, keepdims=True)
        i = _first_index(masked == m, eid, N_EXPERTS)
        hit = eid == i
        idxs.append(i)
        gws.append(jnp.sum(jnp.where(hit, scores, 0.0), axis=0, keepdims=True))
        sel = jnp.logical_or(sel, hit)
        masked = jnp.where(hit, neg, masked)
    eidx = jnp.concatenate(idxs, axis=0)
    gw = jnp.concatenate(gws, axis=0)
    gw = gw / jnp.sum(gw, axis=0, keepdims=True) * ROUTED_SCALE
    eidx_ref[...] = eidx
    gw_ref[...] = gw

    before = jnp.dot(sel.astype(BF16), tri_ref[...], preferred_element_type=F32) + carry_ref[...]
    rank_ref[...] = jnp.concatenate(
        [jnp.sum(jnp.where(eid == idxs[k], before, 0.0), axis=0, keepdims=True) for k in range(TOP_K)],
        axis=0).astype(I32)
    carry = carry_ref[...] + jnp.sum(sel.astype(F32), axis=1, keepdims=True)
    carry_ref[...] = carry
    cnt_ref[...] = carry.astype(I32)


def _route(h2, g_ffn, wr_t, bias_col, wsgu, wsd):
    n, d = h2.shape
    t = min(ROUTE_TILE, n)
    tri = (lax.broadcasted_iota(I32, (t, t), 0) < lax.broadcasted_iota(I32, (t, t), 1)).astype(BF16)
    row = lambda w, dt=None: pl.BlockSpec((t, w), lambda i: (i, 0))
    col = pl.BlockSpec((TOP_K, t), lambda i: (0, i))
    return pl.pallas_call(
        _route_kernel,
        grid=(n // t,),
        in_specs=[row(d), _full((1, d)), _full((N_EXPERTS, d)), _full((N_EXPERTS, 1)),
                  _full((d, 2 * D_EXPERT)), _full((D_EXPERT, d)), _full((t, t))],
        out_specs=[row(d // 2), row(d), col, col, col, _full((N_EXPERTS, 1))],
        out_shape=[jax.ShapeDtypeStruct((n, d // 2), I32), jax.ShapeDtypeStruct((n, d), F32),
                   jax.ShapeDtypeStruct((TOP_K, n), I32), jax.ShapeDtypeStruct((TOP_K, n), F32),
                   jax.ShapeDtypeStruct((TOP_K, n), I32), jax.ShapeDtypeStruct((N_EXPERTS, 1), I32)],
        scratch_shapes=[pltpu.VMEM((N_EXPERTS, 1), F32)],
        compiler_params=_params(("arbitrary",)),
        name="route",
    )(h2, g_ffn, wr_t, bias_col, wsgu, wsd, tri)


def _expert_kernel(be_ref, nu_ref, x_ref, wgu_ref, wd_ref, y_ref):
    @pl.when(pl.program_id(0) < nu_ref[0])
    def _():
        x = _unpack_rows(x_ref[...]).astype(BF16)
        gu = jnp.dot(x, wgu_ref[0], preferred_element_type=F32)
        hid = jax.nn.silu(gu[:, :D_EXPERT]) * gu[:, D_EXPERT:]
        y_ref[...] = _pack_rows(jnp.dot(hid.astype(BF16), wd_ref[0], preferred_element_type=F32))


def _experts(block_e, n_used, xs, wgu, wd):
    n_slots, dw = xs.shape
    d = 2 * dw
    nb = n_slots // MOE_TILE
    rows = pl.BlockSpec((MOE_TILE, dw), lambda i, be, nu: (jnp.minimum(i, nu[0] - 1), 0))
    return pl.pallas_call(
        _expert_kernel,
        grid_spec=pltpu.PrefetchScalarGridSpec(
            num_scalar_prefetch=2,
            grid=(nb,),
            in_specs=[rows,
                      pl.BlockSpec((1, d, 2 * D_EXPERT), lambda i, be, nu: (be[i], 0, 0)),
                      pl.BlockSpec((1, D_EXPERT, d), lambda i, be, nu: (be[i], 0, 0))],
            out_specs=rows),
        out_shape=jax.ShapeDtypeStruct((n_slots, dw), I32),
        compiler_params=_params(("arbitrary",)),
        name="experts",
    )(block_e, n_used, xs, wgu, wd)


def _sc_worker_id():
    return lax.axis_index("s") * SC_CORES + lax.axis_index("c")


def _sc_dispatch(t_rows, dest3, n_slots):
    _, dw = t_rows.shape
    n_chunks, _, w = dest3.shape
    per_worker = n_chunks // SC_WORKERS
    mesh = plsc.VectorSubcoreMesh(core_axis_name="c", subcore_axis_name="s")

    @functools.partial(
        pl.kernel, mesh=mesh,
        out_type=jax.ShapeDtypeStruct((n_slots, dw), t_rows.dtype),
        scratch_types=[pltpu.VMEM((TOP_K, w), I32), pltpu.VMEM((w, dw), t_rows.dtype), pltpu.SemaphoreType.DMA],
    )
    def scatter_rows(t_hbm, dest_hbm, xs_hbm, idx_v, rows_v, sem):
        wid = _sc_worker_id()

        @pl.loop(0, per_worker)
        def _(j):
            c = wid * per_worker + j
            pltpu.sync_copy(dest_hbm.at[c], idx_v)
            pltpu.sync_copy(t_hbm.at[pl.ds(c * w, w)], rows_v)
            copies = [pltpu.async_copy(rows_v, xs_hbm.at[idx_v.at[k]], sem) for k in range(TOP_K)]
            for cp in copies:
                cp.wait()

    return scatter_rows(t_rows, dest3)


def _sc_combine(y_rows, dest3):
    _, dw = y_rows.shape
    n_chunks, _, w = dest3.shape
    per_worker = n_chunks // SC_WORKERS
    mesh = plsc.VectorSubcoreMesh(core_axis_name="c", subcore_axis_name="s")

    @functools.partial(
        pl.kernel, mesh=mesh,
        out_type=jax.ShapeDtypeStruct((TOP_K, n_chunks * w, dw), y_rows.dtype),
        scratch_types=[pltpu.VMEM((TOP_K, w), I32), pltpu.VMEM((w, dw), y_rows.dtype), pltpu.SemaphoreType.DMA],
    )
    def gather_rows(y_hbm, dest_hbm, out_hbm, idx_v, rows_v, sem):
        wid = _sc_worker_id()

        @pl.loop(0, per_worker)
        def _(j):
            c = wid * per_worker + j
            pltpu.sync_copy(dest_hbm.at[c], idx_v)
            for k in range(TOP_K):
                pltpu.async_copy(y_hbm.at[idx_v.at[k]], rows_v, sem).wait()
                pltpu.sync_copy(rows_v, out_hbm.at[k, pl.ds(c * w, w)])

    return gather_rows(y_rows, dest3)


def _final_kernel(base_ref, y_ref, gw_ref, g_ref, o_ref, *, normalize):
    h = base_ref[...]
    gw = gw_ref[...]
    for k in range(TOP_K):
        h = h + gw[:, k:k + 1] * _unpack_rows(y_ref[k])
    o_ref[...] = _rms(h, g_ref[...]) if normalize else h


def _final(base, yg, gw_rows, g_final, normalize):
    n, d = base.shape
    t = min(FIN_TILE, n)
    return pl.pallas_call(
        functools.partial(_final_kernel, normalize=normalize),
        grid=(n // t,),
        in_specs=[pl.BlockSpec((t, d), lambda i: (i, 0)), pl.BlockSpec((TOP_K, t, d // 2), lambda i: (0, i, 0)),
                  pl.BlockSpec((t, TOP_K), lambda i: (i, 0)), _full((1, d))],
        out_specs=pl.BlockSpec((t, d), lambda i: (i, 0)),
        out_shape=jax.ShapeDtypeStruct((n, d), F32),
        compiler_params=_params(("parallel",)),
        name="final",
    )(base, yg, gw_rows, g_final)


def _ssm_matrices(a_re, a_im, log_dt, b_re, b_im, c_re, c_im):
    lr, li = a_re.astype(F32), a_im.astype(F32)
    dt = jnp.exp(log_dt.astype(F32))[:, None]
    mag = jnp.exp(lr * dt)
    ab_re, ab_im = mag * jnp.cos(li * dt), mag * jnp.sin(li * dt)
    den = lr * lr + li * li
    zr, zi = ab_re - 1.0, ab_im
    k_re = (zr * lr + zi * li) / den
    k_im = (zi * lr - zr * li) / den
    br, bi = b_re.astype(F32), b_im.astype(F32)
    bb_re = k_re[..., None] * br - k_im[..., None] * bi
    bb_im = k_re[..., None] * bi + k_im[..., None] * br
    eye = jnp.eye(SSM_GROUPS, dtype=F32)

    def in_mat(bb):
        return jnp.einsum("gpc,gh->gchp", bb, eye).reshape(SSM_WIDTH, N_STATE)

    def out_mat(c):
        return jnp.einsum("gcp,gh->gphc", c.astype(F32), eye).reshape(N_STATE, SSM_WIDTH)

    bmat = jnp.concatenate([in_mat(bb_re), in_mat(bb_im)], axis=1)
    cmat = jnp.concatenate([out_mat(c_re), -out_mat(c_im)], axis=0)
    tile = lambda a: jnp.broadcast_to(a.reshape(1, N_STATE), (SUBLANES, N_STATE))
    return bmat.astype(BF16), tile(ab_re), tile(ab_im), cmat.astype(BF16)


def _dispatch_plan(eidx, rank, counts, n_tok):
    padded = ((counts + MOE_TILE - 1) // MOE_TILE) * MOE_TILE
    pend = jnp.cumsum(padded)
    pstart = pend - padded
    dest = pstart[eidx] + rank
    n_slots = n_tok * TOP_K + N_EXPERTS * MOE_TILE
    nb = n_slots // MOE_TILE
    n_used = (pend[-1] // MOE_TILE).astype(I32)
    blk = jnp.minimum(jnp.arange(nb, dtype=I32), n_used - 1)
    block_e = jnp.minimum(jnp.searchsorted(pend, blk * MOE_TILE, side="right"), N_EXPERTS - 1).astype(I32)
    return dest.astype(I32), block_e, n_used.reshape(1), n_slots


def kernel(x, mem, positions, g_mix, w_in, a_re, a_im, log_dt, b_re, b_im, c_re, c_im, d_skip, w_glu, g_ssm_out, lam_q1, lam_k1, lam_q2, lam_k2, g_sub, w_out, g_x, g_mem, wq_x, wk_x, wv_x, wo_x, g_ffn, w_router, router_bias, w_gate, w_up, w_down, ws_gate, ws_up, ws_down, g_final):
    b, l, d = x.shape
    n = b * l
    depth = w_in.shape[0]
    row = lambda a: a.reshape(1, -1).astype(F32)
    inv = ROPE_THETA ** (-jnp.arange(0, DA_QKDIM, 2, dtype=F32) / DA_QKDIM)
    inv_row = jnp.tile(inv, 128 // inv.shape[0]).reshape(1, 128)
    pos3 = positions.reshape(b, l, 1)

    h = x
    for i in range(depth):
        lambda_init = 0.8 - 0.6 * math.exp(-0.3 * i)
        u, q, k, v = _inproj(h, pos3, row(g_mix[i]), w_in[i].astype(BF16), inv_row)

        bmat, are8, aim8, cmat = _ssm_matrices(a_re[i], a_im[i], log_dt[i], b_re[i], b_im[i], c_re[i], c_im[i])
        u_tm = u.transpose(1, 0, 2).reshape(n, SSM_WIDTH)
        y_ssm = _ssm(u_tm, bmat, are8, aim8, cmat, row(d_skip[i]), w_glu[i].astype(BF16), row(g_ssm_out[i]), b)
        y_ssm = y_ssm.reshape(l, b, SSM_WIDTH).transpose(1, 0, 2)

        lam = (jnp.exp(jnp.sum(lam_q1[i].astype(F32) * lam_k1[i].astype(F32)))
               - jnp.exp(jnp.sum(lam_q2[i].astype(F32) * lam_k2[i].astype(F32))) + lambda_init).reshape(1)
        y_att = _diff_attention(q, k, v, lam.astype(F32), row(g_sub[i]), lambda_init)

        kmem, vmem = _memkv(mem, row(g_mem[i]), wk_x[i].astype(BF16), wv_x[i].astype(BF16))
        wo = w_out[i].astype(BF16)
        h2 = _mix(h, y_ssm, y_att, wo[:SSM_WIDTH], wo[SSM_WIDTH:], row(g_x[i]), wq_x[i].astype(BF16),
                  kmem, vmem, wo_x[i].astype(BF16))

        wsgu = jnp.concatenate([ws_gate[i], ws_up[i]], axis=1).astype(BF16)
        t_pk, base, eidx, gw, rank, counts = _route(
            h2.reshape(n, d), row(g_ffn[i]), w_router[i].T.astype(F32),
            router_bias[i].reshape(N_EXPERTS, 1).astype(F32), wsgu, ws_down[i].astype(BF16))

        dest, block_e, n_used, n_slots = _dispatch_plan(eidx, rank, counts[:, 0], n)
        dest3 = dest.reshape(TOP_K, n // SC_WINDOW, SC_WINDOW).transpose(1, 0, 2)
        xs = _sc_dispatch(t_pk, dest3, n_slots)
        wgu = jnp.concatenate([w_gate[i], w_up[i]], axis=2).astype(BF16)
        ys = _experts(block_e, n_used, xs, wgu, w_down[i].astype(BF16))
        yg = _sc_combine(ys, dest3)
        h = _final(base, yg, gw.T, row(g_final), normalize=i == depth - 1)
        h = h.reshape(b, l, d)
    return h
```

```python
import functools
import math

import jax
import jax.numpy as jnp
from jax import lax
from jax.experimental import pallas as pl
from jax.experimental.pallas import tpu as pltpu
from jax.experimental.pallas import tpu_sc as plsc

F32 = jnp.float32
BF16 = jnp.bfloat16
I32 = jnp.int32

D_MODEL = 1024
SSM_WIDTH = 512
ATTN_WIDTH = 512
SSM_GROUP = 16
SSM_GROUPS = 32
SSM_STATE = 64
N_STATE = SSM_GROUPS * SSM_STATE
DA_HEADS = 4
DA_VDIM = 128
DA_QKDIM = 64
ROPE_THETA = 10000.0
X_HEADS = 4
X_HEAD_DIM = 256
N_EXPERTS = 64
TOP_K = 8
N_EXPERT_GROUPS = 8
PER_GROUP = N_EXPERTS // N_EXPERT_GROUPS
TOPK_GROUPS = 4
D_EXPERT = 256
ROUTED_SCALE = 2.5
EPS = 1e-6

VMEM_LIMIT_V7X = 56 * 1024 * 1024
SUBLANES = 8

IN_TILE = 512
SSM_STEPS = 64
SSM_COLS = 512
ATT_TILE = 512
MIX_TILE = 512
ROUTE_TILE = 512
MOE_TILE = 512
PLAN_TILE = 2048
FIN_TILE = 256

SC_CORES = 2
SC_WORKERS = 32
SC_WINDOW = 64


def _params(sem):
    return pltpu.CompilerParams(dimension_semantics=sem, vmem_limit_bytes=VMEM_LIMIT_V7X)


def _rms(x, g):
    return x * lax.rsqrt(jnp.mean(x * x, axis=-1, keepdims=True) + EPS) * g


def _full(shape):
    return pl.BlockSpec(shape, lambda *_: (0,) * len(shape))


def _pack_rows(a):
    w = a.shape[1] // 2
    bits = lambda v: lax.bitcast_convert_type(v.astype(BF16).astype(F32), I32)
    return (bits(a[:, w:]) & jnp.int32(-65536)) | lax.shift_right_logical(bits(a[:, :w]), 16)


def _unpack_rows(p):
    lo = lax.bitcast_convert_type(lax.shift_left(p, 16), F32)
    hi = lax.bitcast_convert_type(p & jnp.int32(-65536), F32)
    return jnp.concatenate([lo, hi], axis=1)


def _inproj_kernel(x_ref, pos_ref, g_ref, w_ref, inv_ref, u_ref, q_ref, k_ref, v_ref):
    x = x_ref[0]
    a = _rms(x, g_ref[...]).astype(BF16)
    z = jnp.dot(a, w_ref[...], preferred_element_type=F32)
    u_ref[0] = z[:, :SSM_WIDTH]
    ang = pos_ref[0].astype(F32) * inv_ref[...]
    cos = jnp.concatenate([jnp.cos(ang)] * 4, axis=1)
    sin = jnp.concatenate([jnp.sin(ang)] * 4, axis=1)
    lane = lax.broadcasted_iota(I32, cos.shape, 1)
    first = (lane & (DA_QKDIM - 1)) < DA_QKDIM // 2
    half = DA_QKDIM // 2

    def rope(t):
        rot = jnp.where(first, -pltpu.roll(t, ATTN_WIDTH - half, 1), pltpu.roll(t, half, 1))
        return t * cos + rot * sin

    q = z[:, SSM_WIDTH:SSM_WIDTH + ATTN_WIDTH]
    k = z[:, SSM_WIDTH + ATTN_WIDTH:SSM_WIDTH + 2 * ATTN_WIDTH]
    q_ref[0] = (rope(q) * (DA_QKDIM ** -0.5)).astype(BF16)
    k_ref[0] = rope(k).astype(BF16)
    v_ref[0] = z[:, SSM_WIDTH + 2 * ATTN_WIDTH:].astype(BF16)


def _inproj(x, pos3, g_mix, w_in, inv_row):
    b, l, d = x.shape
    t = min(IN_TILE, l)
    n_out = w_in.shape[1]
    row = lambda w: pl.BlockSpec((1, t, w), lambda i, j: (i, j, 0))
    return pl.pallas_call(
        _inproj_kernel,
        grid=(b, l // t),
        in_specs=[row(d), row(1), _full((1, d)), _full((d, n_out)), _full((1, 128))],
        out_specs=[row(SSM_WIDTH), row(ATTN_WIDTH), row(ATTN_WIDTH), row(ATTN_WIDTH)],
        out_shape=[jax.ShapeDtypeStruct((b, l, SSM_WIDTH), F32)]
        + [jax.ShapeDtypeStruct((b, l, ATTN_WIDTH), BF16)] * 3,
        compiler_params=_params(("parallel", "parallel")),
        name="inproj",
    )(x, pos3, g_mix, w_in, inv_row)


def _ssm_kernel(u_ref, bm_ref, are_ref, aim_ref, cm_ref, dskip_ref, wglu_ref, g_ref, o_ref, bu_ref, st_ref,
                *, steps):
    @pl.when(pl.program_id(0) == 0)
    def _():
        st_ref[...] = jnp.zeros_like(st_ref)

    u = u_ref[...]
    bu_ref[...] = jnp.dot(u.astype(BF16), bm_ref[...], preferred_element_type=F32)

    for c0 in range(0, N_STATE, SSM_COLS):
        re = slice(c0, c0 + SSM_COLS)
        im = slice(N_STATE + c0, N_STATE + c0 + SSM_COLS)
        ar = are_ref[:, re]
        ai = aim_ref[:, re]

        def step(t, carry, re=re, im=im, ar=ar, ai=ai):
            sr, si = carry
            rows = pl.ds(pl.multiple_of(t * SUBLANES, SUBLANES), SUBLANES)
            nr = ar * sr - ai * si + bu_ref[rows, re]
            ni = ar * si + ai * sr + bu_ref[rows, im]
            bu_ref[rows, re] = nr
            bu_ref[rows, im] = ni
            return nr, ni

        sr, si = lax.fori_loop(0, steps, step, (st_ref[:, re], st_ref[:, im]))
        st_ref[:, re] = sr
        st_ref[:, im] = si

    y = jnp.dot(bu_ref[...].astype(BF16), cm_ref[...], preferred_element_type=F32) + dskip_ref[...] * u
    y = jax.nn.gelu(y)
    y = y * jax.nn.sigmoid(jnp.dot(y.astype(BF16), wglu_ref[...], preferred_element_type=F32))
    o_ref[...] = _rms(y, g_ref[...]).astype(BF16)


def _ssm(u_tm, bmat, a_re8, a_im8, cmat, d_skip, w_glu, g_out, batch):
    assert batch == SUBLANES, "one time step of all sequences must fill the sublanes"
    n = u_tm.shape[0]
    steps = min(SSM_STEPS, n // batch)
    rows = steps * batch
    return pl.pallas_call(
        functools.partial(_ssm_kernel, steps=steps),
        grid=(n // rows,),
        in_specs=[pl.BlockSpec((rows, SSM_WIDTH), lambda i: (i, 0)),
                  _full((SSM_WIDTH, 2 * N_STATE)), _full((SUBLANES, N_STATE)), _full((SUBLANES, N_STATE)),
                  _full((2 * N_STATE, SSM_WIDTH)), _full((1, SSM_WIDTH)), _full((SSM_WIDTH, SSM_WIDTH)),
                  _full((1, SSM_WIDTH))],
        out_specs=pl.BlockSpec((rows, SSM_WIDTH), lambda i: (i, 0)),
        out_shape=jax.ShapeDtypeStruct((n, SSM_WIDTH), BF16),
        scratch_shapes=[pltpu.VMEM((rows, 2 * N_STATE), F32), pltpu.VMEM((SUBLANES, 2 * N_STATE), F32)],
        compiler_params=_params(("arbitrary",)),
        name="ssm",
    )(u_tm, bmat, a_re8, a_im8, cmat, d_skip, w_glu, g_out)


def _attn_kernel(lam_ref, q_ref, k_ref, v_ref, g_ref, o_ref, qs_ref, m_ref, acc_ref, *, tile, out_scale):
    qi = pl.program_id(2)
    ki = pl.program_id(3)

    @pl.when(ki == 0)
    def _():
        q = q_ref[0]
        lane = lax.broadcasted_iota(I32, q.shape, 1)
        zero = jnp.zeros_like(q)
        qs_ref[:tile] = jnp.where(lane < DA_QKDIM, q, zero)
        qs_ref[tile:] = jnp.where(lane >= DA_QKDIM, q, zero)
        m_ref[...] = jnp.full_like(m_ref, -jnp.inf)
        acc_ref[...] = jnp.zeros_like(acc_ref)

    def update(masked):
        s = lax.dot_general(qs_ref[...], k_ref[0], (((1,), (1,)), ((), ())), preferred_element_type=F32)
        if masked:
            row = lax.broadcasted_iota(I32, s.shape, 0) & (tile - 1)
            col = lax.broadcasted_iota(I32, s.shape, 1)
            s = jnp.where(col <= row, s, jnp.finfo(F32).min)
        m_old = m_ref[...]
        m_new = jnp.maximum(m_old, jnp.max(s, axis=1, keepdims=True))
        p = jnp.exp(s - jnp.concatenate([m_new] * (tile // DA_VDIM), axis=1))
        alpha = jnp.exp(m_old - m_new)
        v_ext = jnp.concatenate([v_ref[0], jnp.ones((tile, DA_VDIM), BF16)], axis=1)
        acc_ref[...] = (jnp.concatenate([alpha, alpha], axis=1) * acc_ref[...]
                        + jnp.dot(p.astype(BF16), v_ext, preferred_element_type=F32))
        m_ref[...] = m_new

    @pl.when(ki < qi)
    def _():
        update(False)

    @pl.when(ki == qi)
    def _():
        update(True)
        o = acc_ref[:, :DA_VDIM] / acc_ref[:, DA_VDIM:]
        o = o[:tile] - lam_ref[0] * o[tile:]
        o_ref[0] = (_rms(o, g_ref[...]) * out_scale).astype(BF16)


def _diff_attention(q, k, v, lam, g_sub, lambda_init):
    b, l, _ = q.shape
    t = min(ATT_TILE, l)
    n = l // t
    qspec = pl.BlockSpec((1, t, DA_VDIM), lambda bi, h, qi, ki: (bi, qi, h))
    kspec = pl.BlockSpec((1, t, DA_VDIM), lambda bi, h, qi, ki: (bi, jnp.minimum(ki, qi), h))
    return pl.pallas_call(
        functools.partial(_attn_kernel, tile=t, out_scale=1.0 - lambda_init),
        grid=(b, DA_HEADS, n, n),
        in_specs=[pl.BlockSpec(memory_space=pltpu.SMEM), qspec, kspec, kspec, _full((1, DA_VDIM))],
        out_specs=qspec,
        out_shape=jax.ShapeDtypeStruct((b, l, ATTN_WIDTH), BF16),
        scratch_shapes=[pltpu.VMEM((2 * t, DA_VDIM), BF16), pltpu.VMEM((2 * t, DA_VDIM), F32),
                        pltpu.VMEM((2 * t, 2 * DA_VDIM), F32)],
        compiler_params=_params(("parallel", "parallel", "parallel", "arbitrary")),
        name="diffattn",
    )(lam, q, k, v, g_sub)


def _memkv_kernel(m_ref, g_ref, wk_ref, wv_ref, k_ref, v_ref):
    a = _rms(m_ref[0], g_ref[...]).astype(BF16)
    k_ref[0] = jnp.dot(a, wk_ref[...], preferred_element_type=F32).astype(BF16)
    v_ref[0] = jnp.dot(a, wv_ref[...], preferred_element_type=F32).astype(BF16)


def _memkv(mem, g_mem, wk, wv):
    b, m, d = mem.shape
    blk = pl.BlockSpec((1, m, d), lambda i: (i, 0, 0))
    return pl.pallas_call(
        _memkv_kernel,
        grid=(b,),
        in_specs=[blk, _full((1, d)), _full((d, d)), _full((d, d))],
        out_specs=[blk, blk],
        out_shape=[jax.ShapeDtypeStruct((b, m, d), BF16)] * 2,
        compiler_params=_params(("parallel",)),
        name="memkv",
    )(mem, g_mem, wk, wv)


def _mix_kernel(x_ref, ys_ref, ya_ref, wo1_ref, wo2_ref, gx_ref, wq_ref, km_ref, vm_ref, wo_ref, h_ref):
    h = (x_ref[0]
         + jnp.dot(ys_ref[0], wo1_ref[...], preferred_element_type=F32)
         + jnp.dot(ya_ref[0], wo2_ref[...], preferred_element_type=F32))
    hq = _rms(h, gx_ref[...]).astype(BF16)
    q = jnp.dot(hq, wq_ref[...], preferred_element_type=F32).astype(BF16)
    outs = []
    for hd in range(X_HEADS):
        cols = slice(hd * X_HEAD_DIM, (hd + 1) * X_HEAD_DIM)
        s = lax.dot_general(q[:, cols], km_ref[0, :, cols], (((1,), (1,)), ((), ())),
                            preferred_element_type=F32) * (X_HEAD_DIM ** -0.5)
        s = s - jnp.max(s, axis=1, keepdims=True)
        p = jnp.exp(s)
        p = p / jnp.sum(p, axis=1, keepdims=True)
        outs.append(jnp.dot(p.astype(BF16), vm_ref[0, :, cols], preferred_element_type=F32).astype(BF16))
    o = jnp.concatenate(outs, axis=1)
    h_ref[0] = h + jnp.dot(o, wo_ref[...], preferred_element_type=F32)


def _mix(x, y_ssm, y_att, wo1, wo2, g_x, wq, kmem, vmem, wo):
    b, l, d = x.shape
    t = min(MIX_TILE, l)
    m = kmem.shape[1]
    row = lambda w: pl.BlockSpec((1, t, w), lambda i, j: (i, j, 0))
    mem = pl.BlockSpec((1, m, d), lambda i, j: (i, 0, 0))
    return pl.pallas_call(
        _mix_kernel,
        grid=(b, l // t),
        in_specs=[row(d), row(SSM_WIDTH), row(ATTN_WIDTH), _full((SSM_WIDTH, d)), _full((ATTN_WIDTH, d)),
                  _full((1, d)), _full((d, d)), mem, mem, _full((d, d))],
        out_specs=row(d),
        out_shape=jax.ShapeDtypeStruct((b, l, d), F32),
        compiler_params=_params(("parallel", "parallel")),
        name="mix",
    )(x, y_ssm, y_att, wo1, wo2, g_x, wq, kmem, vmem, wo)


def _first_index(hit, idx, sentinel):
    return jnp.min(jnp.where(hit, idx, sentinel), axis=0, keepdims=True)


def _route_kernel(h_ref, g_ref, wr_ref, bias_ref, wsgu_ref, wsd_ref, tri_ref,
                  t_ref, base_ref, eidx_ref, gw_ref, rank_ref, cnt_ref, carry_ref):
    @pl.when(pl.program_id(0) == 0)
    def _():
        carry_ref[...] = jnp.zeros_like(carry_ref)

    h = h_ref[...]
    t = _rms(h, g_ref[...])
    tb = t.astype(BF16)
    t_ref[...] = _pack_rows(t)
    gu = jnp.dot(tb, wsgu_ref[...], preferred_element_type=F32)
    hid = jax.nn.silu(gu[:, :D_EXPERT]) * gu[:, D_EXPERT:]
    base_ref[...] = h + jnp.dot(hid.astype(BF16), wsd_ref[...], preferred_element_type=F32)

    logits = lax.dot_general(wr_ref[...], t, (((1,), (1,)), ((), ())), preferred_element_type=F32,
                             precision=lax.Precision.HIGHEST)
    scores = jax.nn.sigmoid(logits)
    biased = scores + bias_ref[...]
    n_tok = scores.shape[1]
    neg = -jnp.inf
    sub = lax.broadcasted_iota(I32, (PER_GROUP, n_tok), 0)

    gs = []
    for g in range(N_EXPERT_GROUPS):
        blk = biased[g * PER_GROUP:(g + 1) * PER_GROUP]
        m1 = jnp.max(blk, axis=0, keepdims=True)
        i1 = _first_index(blk == m1, sub, PER_GROUP)
        m2 = jnp.max(jnp.where(sub == i1, neg, blk), axis=0, keepdims=True)
        gs.append(m1 + m2)
    gs = jnp.concatenate(gs, axis=0)

    gsel = jnp.zeros(gs.shape, jnp.bool_)
    for _ in range(TOPK_GROUPS):
        m = jnp.max(gs, axis=0, keepdims=True)
        hit = sub == _first_index(gs == m, sub, N_EXPERT_GROUPS)
        gsel = jnp.logical_or(gsel, hit)
        gs = jnp.where(hit, neg, gs)

    masked = jnp.concatenate(
        [jnp.where(gsel[g:g + 1], biased[g * PER_GROUP:(g + 1) * PER_GROUP], neg)
         for g in range(N_EXPERT_GROUPS)], axis=0)
    eid = lax.broadcasted_iota(I32, masked.shape, 0)
    sel = jnp.zeros(masked.shape, jnp.bool_)
    idxs, gws = [], []
    for _ in range(TOP_K):
        m = jnp.max(masked, axis=0, keepdims=True)
        i = _first_index(masked == m, eid, N_EXPERTS)
        hit = eid == i
        idxs.append(i)
        gws.append(jnp.sum(jnp.where(hit, scores, 0.0), axis=0, keepdims=True))
        sel = jnp.logical_or(sel, hit)
        masked = jnp.where(hit, neg, masked)
    eidx = jnp.concatenate(idxs, axis=0)
    gw = jnp.concatenate(gws, axis=0)
    gw = gw / jnp.sum(gw, axis=0, keepdims=True) * ROUTED_SCALE
    eidx_ref[...] = eidx
    gw_ref[...] = gw

    before = jnp.dot(sel.astype(BF16), tri_ref[...], preferred_element_type=F32) + carry_ref[...]
    rank_ref[...] = jnp.concatenate(
        [jnp.sum(jnp.where(eid == idxs[k], before, 0.0), axis=0, keepdims=True) for k in range(TOP_K)],
        axis=0).astype(I32)
    carry = carry_ref[...] + jnp.sum(sel.astype(F32), axis=1, keepdims=True)
    carry_ref[...] = carry
    cnt_ref[...] = carry.astype(I32)


def _route(h2, g_ffn, wr_t, bias_col, wsgu, wsd):
    n, d = h2.shape
    t = min(ROUTE_TILE, n)
    tri = (lax.broadcasted_iota(I32, (t, t), 0) < lax.broadcasted_iota(I32, (t, t), 1)).astype(BF16)
    row = lambda w, dt=None: pl.BlockSpec((t, w), lambda i: (i, 0))
    col = pl.BlockSpec((TOP_K, t), lambda i: (0, i))
    return pl.pallas_call(
        _route_kernel,
        grid=(n // t,),
        in_specs=[row(d), _full((1, d)), _full((N_EXPERTS, d)), _full((N_EXPERTS, 1)),
                  _full((d, 2 * D_EXPERT)), _full((D_EXPERT, d)), _full((t, t))],
        out_specs=[row(d // 2), row(d), col, col, col, _full((N_EXPERTS, 1))],
        out_shape=[jax.ShapeDtypeStruct((n, d // 2), I32), jax.ShapeDtypeStruct((n, d), F32),
                   jax.ShapeDtypeStruct((TOP_K, n), I32), jax.ShapeDtypeStruct((TOP_K, n), F32),
                   jax.ShapeDtypeStruct((TOP_K, n), I32), jax.ShapeDtypeStruct((N_EXPERTS, 1), I32)],
        scratch_shapes=[pltpu.VMEM((N_EXPERTS, 1), F32)],
        compiler_params=_params(("arbitrary",)),
        name="route",
    )(h2, g_ffn, wr_t, bias_col, wsgu, wsd, tri)


def _expert_kernel(be_ref, nu_ref, x_ref, wgu_ref, wd_ref, y_ref):
    @pl.when(pl.program_id(0) < nu_ref[0])
    def _():
        x = _unpack_rows(x_ref[...]).astype(BF16)
        gu = jnp.dot(x, wgu_ref[0], preferred_element_type=F32)
        hid = jax.nn.silu(gu[:, :D_EXPERT]) * gu[:, D_EXPERT:]
        y_ref[...] = _pack_rows(jnp.dot(hid.astype(BF16), wd_ref[0], preferred_element_type=F32))


def _experts(block_e, n_used, xs, wgu, wd):
    n_slots, dw = xs.shape
    d = 2 * dw
    nb = n_slots // MOE_TILE
    rows = pl.BlockSpec((MOE_TILE, dw), lambda i, be, nu: (jnp.minimum(i, nu[0] - 1), 0))
    return pl.pallas_call(
        _expert_kernel,
        grid_spec=pltpu.PrefetchScalarGridSpec(
            num_scalar_prefetch=2,
            grid=(nb,),
            in_specs=[rows,
                      pl.BlockSpec((1, d, 2 * D_EXPERT), lambda i, be, nu: (be[i], 0, 0)),
                      pl.BlockSpec((1, D_EXPERT, d), lambda i, be, nu: (be[i], 0, 0))],
            out_specs=rows),
        out_shape=jax.ShapeDtypeStruct((n_slots, dw), I32),
        compiler_params=_params(("arbitrary",)),
        name="experts",
    )(block_e, n_used, xs, wgu, wd)


def _sc_worker_id():
    return lax.axis_index("s") * SC_CORES + lax.axis_index("c")


def _sc_dispatch(t_rows, dest3, n_slots):
    _, dw = t_rows.shape
    n_chunks, _, w = dest3.shape
    per_worker = n_chunks // SC_WORKERS
    mesh = plsc.VectorSubcoreMesh(core_axis_name="c", subcore_axis_name="s")

    @functools.partial(
        pl.kernel, mesh=mesh,
        out_type=jax.ShapeDtypeStruct((n_slots, dw), t_rows.dtype),
        scratch_types=[pltpu.VMEM((TOP_K, w), I32), pltpu.VMEM((w, dw), t_rows.dtype), pltpu.SemaphoreType.DMA],
    )
    def scatter_rows(t_hbm, dest_hbm, xs_hbm, idx_v, rows_v, sem):
        wid = _sc_worker_id()

        @pl.loop(0, per_worker)
        def _(j):
            c = wid * per_worker + j
            pltpu.sync_copy(dest_hbm.at[c], idx_v)
            pltpu.sync_copy(t_hbm.at[pl.ds(c * w, w)], rows_v)
            copies = [pltpu.async_copy(rows_v, xs_hbm.at[idx_v.at[k]], sem) for k in range(TOP_K)]
            for cp in copies:
                cp.wait()

    return scatter_rows(t_rows, dest3)


def _sc_combine(y_rows, dest3):
    _, dw = y_rows.shape
    n_chunks, _, w = dest3.shape
    per_worker = n_chunks // SC_WORKERS
    mesh = plsc.VectorSubcoreMesh(core_axis_name="c", subcore_axis_name="s")

    @functools.partial(
        pl.kernel, mesh=mesh,
        out_type=jax.ShapeDtypeStruct((TOP_K, n_chunks * w, dw), y_rows.dtype),
        scratch_types=[pltpu.VMEM((TOP_K, w), I32), pltpu.VMEM((w, dw), y_rows.dtype), pltpu.SemaphoreType.DMA],
    )
    def gather_rows(y_hbm, dest_hbm, out_hbm, idx_v, rows_v, sem):
        wid = _sc_worker_id()

        @pl.loop(0, per_worker)
        def _(j):
            c = wid * per_worker + j
            pltpu.sync_copy(dest_hbm.at[c], idx_v)
            for k in range(TOP_K):
                pltpu.async_copy(y_hbm.at[idx_v.at[k]], rows_v, sem).wait()
                pltpu.sync_copy(rows_v, out_hbm.at[k, pl.ds(c * w, w)])

    return gather_rows(y_rows, dest3)


def _final_kernel(base_ref, y_ref, gw_ref, g_ref, o_ref, *, normalize):
    h = base_ref[...]
    gw = gw_ref[...]
    for k in range(TOP_K):
        h = h + gw[:, k:k + 1] * _unpack_rows(y_ref[k])
    o_ref[...] = _rms(h, g_ref[...]) if normalize else h


def _final(base, yg, gw_rows, g_final, normalize):
    n, d = base.shape
    t = min(FIN_TILE, n)
    return pl.pallas_call(
        functools.partial(_final_kernel, normalize=normalize),
        grid=(n // t,),
        in_specs=[pl.BlockSpec((t, d), lambda i: (i, 0)), pl.BlockSpec((TOP_K, t, d // 2), lambda i: (0, i, 0)),
                  pl.BlockSpec((t, TOP_K), lambda i: (i, 0)), _full((1, d))],
        out_specs=pl.BlockSpec((t, d), lambda i: (i, 0)),
        out_shape=jax.ShapeDtypeStruct((n, d), F32),
        compiler_params=_params(("parallel",)),
        name="final",
    )(base, yg, gw_rows, g_final)


def _ssm_matrices(a_re, a_im, log_dt, b_re, b_im, c_re, c_im):
    lr, li = a_re.astype(F32), a_im.astype(F32)
    dt = jnp.exp(log_dt.astype(F32))[:, None]
    mag = jnp.exp(lr * dt)
    ab_re, ab_im = mag * jnp.cos(li * dt), mag * jnp.sin(li * dt)
    den = lr * lr + li * li
    zr, zi = ab_re - 1.0, ab_im
    k_re = (zr * lr + zi * li) / den
    k_im = (zi * lr - zr * li) / den
    br, bi = b_re.astype(F32), b_im.astype(F32)
    bb_re = k_re[..., None] * br - k_im[..., None] * bi
    bb_im = k_re[..., None] * bi + k_im[..., None] * br
    eye = jnp.eye(SSM_GROUPS, dtype=F32)

    def in_mat(bb):
        return jnp.einsum("gpc,gh->gchp", bb, eye).reshape(SSM_WIDTH, N_STATE)

    def out_mat(c):
        return jnp.einsum("gcp,gh->gphc", c.astype(F32), eye).reshape(N_STATE, SSM_WIDTH)

    bmat = jnp.concatenate([in_mat(bb_re), in_mat(bb_im)], axis=1)
    cmat = jnp.concatenate([out_mat(c_re), -out_mat(c_im)], axis=0)
    tile = lambda a: jnp.broadcast_to(a.reshape(1, N_STATE), (SUBLANES, N_STATE))
    return bmat.astype(BF16), tile(ab_re), tile(ab_im), cmat.astype(BF16)


def _plan_kernel(pstart_ref, eidx_ref, rank_ref, dest_ref, *, window):
    eidx = eidx_ref[...]
    dest = rank_ref[...]
    for e in range(N_EXPERTS):
        dest = dest + jnp.where(eidx == e, pstart_ref[e], 0)
    for c in range(dest.shape[1] // window):
        dest_ref[c] = dest[:, c * window:(c + 1) * window]


def _plan(pstart, eidx, rank, window):
    k, n = eidx.shape
    t = min(PLAN_TILE, n)
    cols = pl.BlockSpec((k, t), lambda i, ps: (0, i))
    return pl.pallas_call(
        functools.partial(_plan_kernel, window=window),
        grid_spec=pltpu.PrefetchScalarGridSpec(
            num_scalar_prefetch=1,
            grid=(n // t,),
            in_specs=[cols, cols],
            out_specs=pl.BlockSpec((t // window, k, window), lambda i, ps: (i, 0, 0))),
        out_shape=jax.ShapeDtypeStruct((n // window, k, window), I32),
        compiler_params=_params(("parallel",)),
        name="plan",
    )(pstart, eidx, rank)


def _block_schedule(counts, n_tok):
    padded = ((counts + MOE_TILE - 1) // MOE_TILE) * MOE_TILE
    pend = jnp.cumsum(padded)
    pstart = (pend - padded).astype(I32)
    n_slots = n_tok * TOP_K + N_EXPERTS * MOE_TILE
    nb = n_slots // MOE_TILE
    n_used = (pend[-1] // MOE_TILE).astype(I32)
    blk = jnp.minimum(jnp.arange(nb, dtype=I32), n_used - 1)
    block_e = jnp.sum((pend[None, :] <= (blk * MOE_TILE)[:, None]).astype(I32), axis=1)
    return pstart, jnp.minimum(block_e, N_EXPERTS - 1), n_used.reshape(1), n_slots


def kernel(x, mem, positions, g_mix, w_in, a_re, a_im, log_dt, b_re, b_im, c_re, c_im, d_skip, w_glu, g_ssm_out, lam_q1, lam_k1, lam_q2, lam_k2, g_sub, w_out, g_x, g_mem, wq_x, wk_x, wv_x, wo_x, g_ffn, w_router, router_bias, w_gate, w_up, w_down, ws_gate, ws_up, ws_down, g_final):
    b, l, d = x.shape
    n = b * l
    depth = w_in.shape[0]
    row = lambda a: a.reshape(1, -1).astype(F32)
    inv = ROPE_THETA ** (-jnp.arange(0, DA_QKDIM, 2, dtype=F32) / DA_QKDIM)
    inv_row = jnp.tile(inv, 128 // inv.shape[0]).reshape(1, 128)
    pos3 = positions.reshape(b, l, 1)

    h = x
    for i in range(depth):
        lambda_init = 0.8 - 0.6 * math.exp(-0.3 * i)
        u, q, k, v = _inproj(h, pos3, row(g_mix[i]), w_in[i].astype(BF16), inv_row)

        bmat, are8, aim8, cmat = _ssm_matrices(a_re[i], a_im[i], log_dt[i], b_re[i], b_im[i], c_re[i], c_im[i])
        u_tm = u.transpose(1, 0, 2).reshape(n, SSM_WIDTH)
        y_ssm = _ssm(u_tm, bmat, are8, aim8, cmat, row(d_skip[i]), w_glu[i].astype(BF16), row(g_ssm_out[i]), b)
        y_ssm = y_ssm.reshape(l, b, SSM_WIDTH).transpose(1, 0, 2)

        lam = (jnp.exp(jnp.sum(lam_q1[i].astype(F32) * lam_k1[i].astype(F32)))
               - jnp.exp(jnp.sum(lam_q2[i].astype(F32) * lam_k2[i].astype(F32))) + lambda_init).reshape(1)
        y_att = _diff_attention(q, k, v, lam.astype(F32), row(g_sub[i]), lambda_init)

        kmem, vmem = _memkv(mem, row(g_mem[i]), wk_x[i].astype(BF16), wv_x[i].astype(BF16))
        wo = w_out[i].astype(BF16)
        h2 = _mix(h, y_ssm, y_att, wo[:SSM_WIDTH], wo[SSM_WIDTH:], row(g_x[i]), wq_x[i].astype(BF16),
                  kmem, vmem, wo_x[i].astype(BF16))

        wsgu = jnp.concatenate([ws_gate[i], ws_up[i]], axis=1).astype(BF16)
        t_pk, base, eidx, gw, rank, counts = _route(
            h2.reshape(n, d), row(g_ffn[i]), w_router[i].T.astype(F32),
            router_bias[i].reshape(N_EXPERTS, 1).astype(F32), wsgu, ws_down[i].astype(BF16))

        pstart, block_e, n_used, n_slots = _block_schedule(counts[:, 0], n)
        dest3 = _plan(pstart, eidx, rank, SC_WINDOW)
        xs = _sc_dispatch(t_pk, dest3, n_slots)
        wgu = jnp.concatenate([w_gate[i], w_up[i]], axis=2).astype(BF16)
        ys = _experts(block_e, n_used, xs, wgu, w_down[i].astype(BF16))
        yg = _sc_combine(ys, dest3)
        h = _final(base, yg, gw.T, row(g_final), normalize=i == depth - 1)
        h = h.reshape(b, l, d)
    return h
```

```python
import functools
import math

import jax
import jax.numpy as jnp
from jax import lax
from jax.experimental import pallas as pl
from jax.experimental.pallas import tpu as pltpu
from jax.experimental.pallas import tpu_sc as plsc

F32 = jnp.float32
BF16 = jnp.bfloat16
I32 = jnp.int32

D_MODEL = 1024
SSM_WIDTH = 512
ATTN_WIDTH = 512
SSM_GROUP = 16
SSM_GROUPS = 32
SSM_STATE = 64
N_STATE = SSM_GROUPS * SSM_STATE
DA_HEADS = 4
DA_VDIM = 128
DA_QKDIM = 64
ROPE_THETA = 10000.0
X_HEADS = 4
X_HEAD_DIM = 256
N_EXPERTS = 64
TOP_K = 8
N_EXPERT_GROUPS = 8
PER_GROUP = N_EXPERTS // N_EXPERT_GROUPS
TOPK_GROUPS = 4
D_EXPERT = 256
ROUTED_SCALE = 2.5
EPS = 1e-6

VMEM_LIMIT_V7X = 56 * 1024 * 1024
SUBLANES = 8

IN_TILE = 512
SSM_STEPS = 64
SSM_COLS = 512
ATT_TILE = 512
ATT_ROWS = 256
MIX_TILE = 512
ROUTE_TILE = 512
MOE_TILE = 512
PLAN_TILE = 2048
FIN_TILE = 256

SC_CORES = 2
SC_WORKERS = 32
SC_WINDOW = 64


def _params(sem):
    return pltpu.CompilerParams(dimension_semantics=sem, vmem_limit_bytes=VMEM_LIMIT_V7X)


def _rms(x, g):
    return x * lax.rsqrt(jnp.mean(x * x, axis=-1, keepdims=True) + EPS) * g


def _full(shape):
    return pl.BlockSpec(shape, lambda *_: (0,) * len(shape))


def _pack_rows(a):
    w = a.shape[1] // 2
    bits = lambda v: lax.bitcast_convert_type(v.astype(BF16).astype(F32), I32)
    return (bits(a[:, w:]) & jnp.int32(-65536)) | lax.shift_right_logical(bits(a[:, :w]), 16)


def _unpack_rows(p):
    lo = lax.bitcast_convert_type(lax.shift_left(p, 16), F32)
    hi = lax.bitcast_convert_type(p & jnp.int32(-65536), F32)
    return jnp.concatenate([lo, hi], axis=1)


def _inproj_kernel(x_ref, pos_ref, g_ref, w_ref, inv_ref, u_ref, q_ref, k_ref, v_ref):
    x = x_ref[0]
    a = _rms(x, g_ref[...]).astype(BF16)
    z = jnp.dot(a, w_ref[...], preferred_element_type=F32)
    u_ref[0] = z[:, :SSM_WIDTH]
    ang = pos_ref[0].astype(F32) * inv_ref[...]
    cos = jnp.concatenate([jnp.cos(ang)] * 4, axis=1)
    sin = jnp.concatenate([jnp.sin(ang)] * 4, axis=1)
    lane = lax.broadcasted_iota(I32, cos.shape, 1)
    first = (lane & (DA_QKDIM - 1)) < DA_QKDIM // 2
    half = DA_QKDIM // 2

    def rope(t):
        rot = jnp.where(first, -pltpu.roll(t, ATTN_WIDTH - half, 1), pltpu.roll(t, half, 1))
        return t * cos + rot * sin

    q = z[:, SSM_WIDTH:SSM_WIDTH + ATTN_WIDTH]
    k = z[:, SSM_WIDTH + ATTN_WIDTH:SSM_WIDTH + 2 * ATTN_WIDTH]
    q_ref[0] = (rope(q) * (DA_QKDIM ** -0.5)).astype(BF16)
    k_ref[0] = rope(k).astype(BF16)
    v_ref[0] = z[:, SSM_WIDTH + 2 * ATTN_WIDTH:].astype(BF16)


def _inproj(x, pos3, g_mix, w_in, inv_row):
    b, l, d = x.shape
    t = min(IN_TILE, l)
    n_out = w_in.shape[1]
    row = lambda w: pl.BlockSpec((1, t, w), lambda i, j: (i, j, 0))
    return pl.pallas_call(
        _inproj_kernel,
        grid=(b, l // t),
        in_specs=[row(d), row(1), _full((1, d)), _full((d, n_out)), _full((1, 128))],
        out_specs=[row(SSM_WIDTH), row(ATTN_WIDTH), row(ATTN_WIDTH), row(ATTN_WIDTH)],
        out_shape=[jax.ShapeDtypeStruct((b, l, SSM_WIDTH), F32)]
        + [jax.ShapeDtypeStruct((b, l, ATTN_WIDTH), BF16)] * 3,
        compiler_params=_params(("parallel", "parallel")),
        name="inproj",
    )(x, pos3, g_mix, w_in, inv_row)


def _ssm_kernel(u_ref, bm_ref, are_ref, aim_ref, cm_ref, dskip_ref, wglu_ref, g_ref, o_ref, bu_ref, st_ref,
                *, steps):
    @pl.when(pl.program_id(0) == 0)
    def _():
        st_ref[...] = jnp.zeros_like(st_ref)

    u = u_ref[...]
    bu_ref[...] = jnp.dot(u.astype(BF16), bm_ref[...], preferred_element_type=F32)

    for c0 in range(0, N_STATE, SSM_COLS):
        re = slice(c0, c0 + SSM_COLS)
        im = slice(N_STATE + c0, N_STATE + c0 + SSM_COLS)
        ar = are_ref[:, re]
        ai = aim_ref[:, re]

        def step(t, carry, re=re, im=im, ar=ar, ai=ai):
            sr, si = carry
            rows = pl.ds(pl.multiple_of(t * SUBLANES, SUBLANES), SUBLANES)
            nr = ar * sr - ai * si + bu_ref[rows, re]
            ni = ar * si + ai * sr + bu_ref[rows, im]
            bu_ref[rows, re] = nr
            bu_ref[rows, im] = ni
            return nr, ni

        sr, si = lax.fori_loop(0, steps, step, (st_ref[:, re], st_ref[:, im]))
        st_ref[:, re] = sr
        st_ref[:, im] = si

    y = jnp.dot(bu_ref[...].astype(BF16), cm_ref[...], preferred_element_type=F32) + dskip_ref[...] * u
    y = jax.nn.gelu(y)
    y = y * jax.nn.sigmoid(jnp.dot(y.astype(BF16), wglu_ref[...], preferred_element_type=F32))
    o_ref[...] = _rms(y, g_ref[...]).astype(BF16)


def _ssm(u_tm, bmat, a_re8, a_im8, cmat, d_skip, w_glu, g_out, batch):
    assert batch == SUBLANES, "one time step of all sequences must fill the sublanes"
    n = u_tm.shape[0]
    steps = min(SSM_STEPS, n // batch)
    rows = steps * batch
    return pl.pallas_call(
        functools.partial(_ssm_kernel, steps=steps),
        grid=(n // rows,),
        in_specs=[pl.BlockSpec((rows, SSM_WIDTH), lambda i: (i, 0)),
                  _full((SSM_WIDTH, 2 * N_STATE)), _full((SUBLANES, N_STATE)), _full((SUBLANES, N_STATE)),
                  _full((2 * N_STATE, SSM_WIDTH)), _full((1, SSM_WIDTH)), _full((SSM_WIDTH, SSM_WIDTH)),
                  _full((1, SSM_WIDTH))],
        out_specs=pl.BlockSpec((rows, SSM_WIDTH), lambda i: (i, 0)),
        out_shape=jax.ShapeDtypeStruct((n, SSM_WIDTH), BF16),
        scratch_shapes=[pltpu.VMEM((rows, 2 * N_STATE), F32), pltpu.VMEM((SUBLANES, 2 * N_STATE), F32)],
        compiler_params=_params(("arbitrary",)),
        name="ssm",
    )(u_tm, bmat, a_re8, a_im8, cmat, d_skip, w_glu, g_out)


def _attn_kernel(lam_ref, q_ref, k_ref, v_ref, g_ref, o_ref, qs_ref, m_ref, acc_ref, *, tile, out_scale):
    qi = pl.program_id(2)
    ki = pl.program_id(3)

    @pl.when(ki == 0)
    def _():
        q = q_ref[0]
        lane = lax.broadcasted_iota(I32, q.shape, 1)
        zero = jnp.zeros_like(q)
        qs_ref[:tile] = jnp.where(lane < DA_QKDIM, q, zero)
        qs_ref[tile:] = jnp.where(lane >= DA_QKDIM, q, zero)
        m_ref[...] = jnp.full_like(m_ref, -jnp.inf)
        acc_ref[...] = jnp.zeros_like(acc_ref)

    def update(masked):
        v_ext = jnp.concatenate([v_ref[0], jnp.ones((tile, DA_VDIM), BF16)], axis=1)
        rb = min(ATT_ROWS, tile)
        n_rb = 2 * tile // rb

        def n_keys(r):
            return (r * rb) % tile + rb if masked else tile

        def scores(r):
            return lax.dot_general(qs_ref[r * rb:(r + 1) * rb], k_ref[0, :n_keys(r)], (((1,), (1,)), ((), ())),
                                   preferred_element_type=F32)

        s_next = scores(0)
        for r in range(n_rb):
            s = s_next
            if r + 1 < n_rb:
                s_next = scores(r + 1)
            rows = slice(r * rb, (r + 1) * rb)
            nk = n_keys(r)
            if masked:
                qpos = lax.broadcasted_iota(I32, s.shape, 0) + (r * rb) % tile
                s = jnp.where(lax.broadcasted_iota(I32, s.shape, 1) <= qpos, s, jnp.finfo(F32).min)
            m_old = m_ref[rows]
            m_new = jnp.maximum(m_old, jnp.max(s, axis=1, keepdims=True))
            p = jnp.exp(s - jnp.concatenate([m_new] * (nk // DA_VDIM), axis=1))
            alpha = jnp.exp(m_old - m_new)
            acc_ref[rows] = (jnp.concatenate([alpha, alpha], axis=1) * acc_ref[rows]
                             + jnp.dot(p.astype(BF16), v_ext[:nk], preferred_element_type=F32))
            m_ref[rows] = m_new

    @pl.when(ki < qi)
    def _():
        update(False)

    @pl.when(ki == qi)
    def _():
        update(True)
        o = acc_ref[:, :DA_VDIM] / acc_ref[:, DA_VDIM:]
        o = o[:tile] - lam_ref[0] * o[tile:]
        o_ref[0] = (_rms(o, g_ref[...]) * out_scale).astype(BF16)


def _diff_attention(q, k, v, lam, g_sub, lambda_init):
    b, l, _ = q.shape
    t = min(ATT_TILE, l)
    n = l // t
    qspec = pl.BlockSpec((1, t, DA_VDIM), lambda bi, h, qi, ki: (bi, qi, h))
    kspec = pl.BlockSpec((1, t, DA_VDIM), lambda bi, h, qi, ki: (bi, jnp.minimum(ki, qi), h))
    return pl.pallas_call(
        functools.partial(_attn_kernel, tile=t, out_scale=1.0 - lambda_init),
        grid=(b, DA_HEADS, n, n),
        in_specs=[pl.BlockSpec(memory_space=pltpu.SMEM), qspec, kspec, kspec, _full((1, DA_VDIM))],
        out_specs=qspec,
        out_shape=jax.ShapeDtypeStruct((b, l, ATTN_WIDTH), BF16),
        scratch_shapes=[pltpu.VMEM((2 * t, DA_VDIM), BF16), pltpu.VMEM((2 * t, DA_VDIM), F32),
                        pltpu.VMEM((2 * t, 2 * DA_VDIM), F32)],
        compiler_params=_params(("parallel", "parallel", "parallel", "arbitrary")),
        name="diffattn",
    )(lam, q, k, v, g_sub)


def _memkv_kernel(m_ref, g_ref, wk_ref, wv_ref, k_ref, v_ref):
    a = _rms(m_ref[0], g_ref[...]).astype(BF16)
    k_ref[0] = jnp.dot(a, wk_ref[...], preferred_element_type=F32).astype(BF16)
    v_ref[0] = jnp.dot(a, wv_ref[...], preferred_element_type=F32).astype(BF16)


def _memkv(mem, g_mem, wk, wv):
    b, m, d = mem.shape
    blk = pl.BlockSpec((1, m, d), lambda i: (i, 0, 0))
    return pl.pallas_call(
        _memkv_kernel,
        grid=(b,),
        in_specs=[blk, _full((1, d)), _full((d, d)), _full((d, d))],
        out_specs=[blk, blk],
        out_shape=[jax.ShapeDtypeStruct((b, m, d), BF16)] * 2,
        compiler_params=_params(("parallel",)),
        name="memkv",
    )(mem, g_mem, wk, wv)


def _mix_kernel(x_ref, ys_ref, ya_ref, wo1_ref, wo2_ref, gx_ref, wq_ref, km_ref, vm_ref, wo_ref, h_ref):
    h = (x_ref[0]
         + jnp.dot(ys_ref[0], wo1_ref[...], preferred_element_type=F32)
         + jnp.dot(ya_ref[0], wo2_ref[...], preferred_element_type=F32))
    hq = _rms(h, gx_ref[...]).astype(BF16)
    q = jnp.dot(hq, wq_ref[...], preferred_element_type=F32).astype(BF16)
    outs = []
    for hd in range(X_HEADS):
        cols = slice(hd * X_HEAD_DIM, (hd + 1) * X_HEAD_DIM)
        s = lax.dot_general(q[:, cols], km_ref[0, :, cols], (((1,), (1,)), ((), ())),
                            preferred_element_type=F32) * (X_HEAD_DIM ** -0.5)
        s = s - jnp.max(s, axis=1, keepdims=True)
        p = jnp.exp(s)
        p = p / jnp.sum(p, axis=1, keepdims=True)
        outs.append(jnp.dot(p.astype(BF16), vm_ref[0, :, cols], preferred_element_type=F32).astype(BF16))
    o = jnp.concatenate(outs, axis=1)
    h_ref[0] = h + jnp.dot(o, wo_ref[...], preferred_element_type=F32)


def _mix(x, y_ssm, y_att, wo1, wo2, g_x, wq, kmem, vmem, wo):
    b, l, d = x.shape
    t = min(MIX_TILE, l)
    m = kmem.shape[1]
    row = lambda w: pl.BlockSpec((1, t, w), lambda i, j: (i, j, 0))
    mem = pl.BlockSpec((1, m, d), lambda i, j: (i, 0, 0))
    return pl.pallas_call(
        _mix_kernel,
        grid=(b, l // t),
        in_specs=[row(d), row(SSM_WIDTH), row(ATTN_WIDTH), _full((SSM_WIDTH, d)), _full((ATTN_WIDTH, d)),
                  _full((1, d)), _full((d, d)), mem, mem, _full((d, d))],
        out_specs=row(d),
        out_shape=jax.ShapeDtypeStruct((b, l, d), F32),
        compiler_params=_params(("parallel", "parallel")),
        name="mix",
    )(x, y_ssm, y_att, wo1, wo2, g_x, wq, kmem, vmem, wo)


def _first_index(hit, idx, sentinel):
    return jnp.min(jnp.where(hit, idx, sentinel), axis=0, keepdims=True)


def _route_kernel(h_ref, g_ref, wr_ref, bias_ref, wsgu_ref, wsd_ref, tri_ref,
                  t_ref, base_ref, eidx_ref, gw_ref, rank_ref, cnt_ref, carry_ref):
    @pl.when(pl.program_id(0) == 0)
    def _():
        carry_ref[...] = jnp.zeros_like(carry_ref)

    h = h_ref[...]
    t = _rms(h, g_ref[...])
    tb = t.astype(BF16)
    t_ref[...] = _pack_rows(t)
    gu = jnp.dot(tb, wsgu_ref[...], preferred_element_type=F32)
    hid = jax.nn.silu(gu[:, :D_EXPERT]) * gu[:, D_EXPERT:]
    base_ref[...] = h + jnp.dot(hid.astype(BF16), wsd_ref[...], preferred_element_type=F32)

    logits = lax.dot_general(wr_ref[...], t, (((1,), (1,)), ((), ())), preferred_element_type=F32,
                             precision=lax.Precision.HIGHEST)
    scores = jax.nn.sigmoid(logits)
    biased = scores + bias_ref[...]
    n_tok = scores.shape[1]
    neg = -jnp.inf
    sub = lax.broadcasted_iota(I32, (PER_GROUP, n_tok), 0)

    gs = []
    for g in range(N_EXPERT_GROUPS):
        blk = biased[g * PER_GROUP:(g + 1) * PER_GROUP]
        m1 = jnp.max(blk, axis=0, keepdims=True)
        i1 = _first_index(blk == m1, sub, PER_GROUP)
        m2 = jnp.max(jnp.where(sub == i1, neg, blk), axis=0, keepdims=True)
        gs.append(m1 + m2)
    gs = jnp.concatenate(gs, axis=0)

    gsel = jnp.zeros(gs.shape, jnp.bool_)
    for _ in range(TOPK_GROUPS):
        m = jnp.max(gs, axis=0, keepdims=True)
        hit = sub == _first_index(gs == m, sub, N_EXPERT_GROUPS)
        gsel = jnp.logical_or(gsel, hit)
        gs = jnp.where(hit, neg, gs)

    masked = jnp.concatenate(
        [jnp.where(gsel[g:g + 1], biased[g * PER_GROUP:(g + 1) * PER_GROUP], neg)
         for g in range(N_EXPERT_GROUPS)], axis=0)
    eid = lax.broadcasted_iota(I32, masked.shape, 0)
    sel = jnp.zeros(masked.shape, jnp.bool_)
    idxs, gws = [], []
    for _ in range(TOP_K):
        m = jnp.max(masked, axis=0, keepdims=True)
        i = _first_index(masked == m, eid, N_EXPERTS)
        hit = eid == i
        idxs.append(i)
        gws.append(jnp.sum(jnp.where(hit, scores, 0.0), axis=0, keepdims=True))
        sel = jnp.logical_or(sel, hit)
        masked = jnp.where(hit, neg, masked)
    eidx = jnp.concatenate(idxs, axis=0)
    gw = jnp.concatenate(gws, axis=0)
    gw = gw / jnp.sum(gw, axis=0, keepdims=True) * ROUTED_SCALE
    eidx_ref[...] = eidx
    gw_ref[...] = gw

    before = jnp.dot(sel.astype(BF16), tri_ref[...], preferred_element_type=F32) + carry_ref[...]
    rank_ref[...] = jnp.concatenate(
        [jnp.sum(jnp.where(eid == idxs[k], before, 0.0), axis=0, keepdims=True) for k in range(TOP_K)],
        axis=0).astype(I32)
    carry = carry_ref[...] + jnp.sum(sel.astype(F32), axis=1, keepdims=True)
    carry_ref[...] = carry
    cnt_ref[...] = carry.astype(I32)


def _route(h2, g_ffn, wr_t, bias_col, wsgu, wsd):
    n, d = h2.shape
    t = min(ROUTE_TILE, n)
    tri = (lax.broadcasted_iota(I32, (t, t), 0) < lax.broadcasted_iota(I32, (t, t), 1)).astype(BF16)
    row = lambda w, dt=None: pl.BlockSpec((t, w), lambda i: (i, 0))
    col = pl.BlockSpec((TOP_K, t), lambda i: (0, i))
    return pl.pallas_call(
        _route_kernel,
        grid=(n // t,),
        in_specs=[row(d), _full((1, d)), _full((N_EXPERTS, d)), _full((N_EXPERTS, 1)),
                  _full((d, 2 * D_EXPERT)), _full((D_EXPERT, d)), _full((t, t))],
        out_specs=[row(d // 2), row(d), col, col, col, _full((N_EXPERTS, 1))],
        out_shape=[jax.ShapeDtypeStruct((n, d // 2), I32), jax.ShapeDtypeStruct((n, d), F32),
                   jax.ShapeDtypeStruct((TOP_K, n), I32), jax.ShapeDtypeStruct((TOP_K, n), F32),
                   jax.ShapeDtypeStruct((TOP_K, n), I32), jax.ShapeDtypeStruct((N_EXPERTS, 1), I32)],
        scratch_shapes=[pltpu.VMEM((N_EXPERTS, 1), F32)],
        compiler_params=_params(("arbitrary",)),
        name="route",
    )(h2, g_ffn, wr_t, bias_col, wsgu, wsd, tri)


def _expert_kernel(be_ref, nu_ref, x_ref, wg_ref, wu_ref, wd_ref, y_ref, wg_s, wu_s, wd_s):
    i = pl.program_id(0)

    @pl.when(jnp.logical_or(i == 0, be_ref[i] != be_ref[jnp.maximum(i - 1, 0)]))
    def _():
        wg_s[...] = wg_ref[0].astype(BF16)
        wu_s[...] = wu_ref[0].astype(BF16)
        wd_s[...] = wd_ref[0].astype(BF16)

    @pl.when(i < nu_ref[0])
    def _():
        x = _unpack_rows(x_ref[...]).astype(BF16)
        gate = jnp.dot(x, wg_s[...], preferred_element_type=F32)
        up = jnp.dot(x, wu_s[...], preferred_element_type=F32)
        hid = (jax.nn.silu(gate) * up).astype(BF16)
        y_ref[...] = _pack_rows(jnp.dot(hid, wd_s[...], preferred_element_type=F32))


def _experts(block_e, n_used, xs, w_gate, w_up, w_down):
    n_slots, dw = xs.shape
    _, d, de = w_gate.shape
    nb = n_slots // MOE_TILE
    rows = pl.BlockSpec((MOE_TILE, dw), lambda i, be, nu: (jnp.minimum(i, nu[0] - 1), 0))
    w_in = pl.BlockSpec((1, d, de), lambda i, be, nu: (be[i], 0, 0))
    return pl.pallas_call(
        _expert_kernel,
        grid_spec=pltpu.PrefetchScalarGridSpec(
            num_scalar_prefetch=2,
            grid=(nb,),
            in_specs=[rows, w_in, w_in, pl.BlockSpec((1, de, d), lambda i, be, nu: (be[i], 0, 0))],
            out_specs=rows,
            scratch_shapes=[pltpu.VMEM((d, de), BF16), pltpu.VMEM((d, de), BF16), pltpu.VMEM((de, d), BF16)]),
        out_shape=jax.ShapeDtypeStruct((n_slots, dw), I32),
        compiler_params=_params(("arbitrary",)),
        name="experts",
    )(block_e, n_used, xs, w_gate, w_up, w_down)


def _sc_worker_id():
    return lax.axis_index("s") * SC_CORES + lax.axis_index("c")


def _sc_dispatch(t_rows, dest3, n_slots):
    _, dw = t_rows.shape
    n_chunks, _, w = dest3.shape
    per_worker = n_chunks // SC_WORKERS
    assert per_worker % 2 == 0
    mesh = plsc.VectorSubcoreMesh(core_axis_name="c", subcore_axis_name="s")
    dt = t_rows.dtype

    @functools.partial(
        pl.kernel, mesh=mesh,
        out_type=jax.ShapeDtypeStruct((n_slots, dw), dt),
        scratch_types=[pltpu.VMEM((TOP_K, w), I32), pltpu.VMEM((TOP_K, w), I32),
                       pltpu.VMEM((w, dw), dt), pltpu.VMEM((w, dw), dt),
                       pltpu.SemaphoreType.DMA, pltpu.SemaphoreType.DMA, pltpu.SemaphoreType.DMA],
    )
    def scatter_rows(t_hbm, dest_hbm, xs_hbm, idx_a, idx_b, rows_a, rows_b, sem_load, sem_a, sem_b):
        wid = _sc_worker_id()

        def scatter(idx_v, rows_v, sem):
            return [pltpu.async_copy(rows_v, xs_hbm.at[idx_v.at[k]], sem) for k in range(TOP_K)]

        @pl.loop(0, per_worker, step=2)
        def _(j):
            ca = wid * per_worker + j
            cb = ca + 1
            pltpu.sync_copy(dest_hbm.at[ca], idx_a)
            pltpu.sync_copy(t_hbm.at[pl.ds(ca * w, w)], rows_a)
            load_idx = pltpu.async_copy(dest_hbm.at[cb], idx_b, sem_load)
            load_rows = pltpu.async_copy(t_hbm.at[pl.ds(cb * w, w)], rows_b, sem_load)
            out_a = scatter(idx_a, rows_a, sem_a)
            load_idx.wait()
            load_rows.wait()
            out_b = scatter(idx_b, rows_b, sem_b)
            for cp in out_a + out_b:
                cp.wait()

    return scatter_rows(t_rows, dest3)


def _sc_combine(y_rows, dest3):
    _, dw = y_rows.shape
    n_chunks, _, w = dest3.shape
    per_worker = n_chunks // SC_WORKERS
    mesh = plsc.VectorSubcoreMesh(core_axis_name="c", subcore_axis_name="s")
    dt = y_rows.dtype

    @functools.partial(
        pl.kernel, mesh=mesh,
        out_type=jax.ShapeDtypeStruct((TOP_K, n_chunks * w, dw), dt),
        scratch_types=[pltpu.VMEM((TOP_K, w), I32), pltpu.VMEM((w, dw), dt), pltpu.VMEM((w, dw), dt),
                       pltpu.SemaphoreType.DMA, pltpu.SemaphoreType.DMA,
                       pltpu.SemaphoreType.DMA, pltpu.SemaphoreType.DMA],
    )
    def gather_rows(y_hbm, dest_hbm, out_hbm, idx_v, buf0, buf1, gsem0, gsem1, wsem0, wsem1):
        wid = _sc_worker_id()
        bufs, gsems, wsems = (buf0, buf1), (gsem0, gsem1), (wsem0, wsem1)

        @pl.loop(0, per_worker)
        def _(j):
            c = wid * per_worker + j
            pltpu.sync_copy(dest_hbm.at[c], idx_v)
            gathers = [None] * TOP_K
            writes = [None] * TOP_K
            gathers[0] = pltpu.async_copy(y_hbm.at[idx_v.at[0]], bufs[0], gsems[0])
            for k in range(TOP_K):
                b = k % 2
                if k + 1 < TOP_K:
                    if k >= 1:
                        writes[k - 1].wait()
                    gathers[k + 1] = pltpu.async_copy(y_hbm.at[idx_v.at[k + 1]], bufs[1 - b], gsems[1 - b])
                gathers[k].wait()
                writes[k] = pltpu.async_copy(bufs[b], out_hbm.at[k, pl.ds(c * w, w)], wsems[b])
            writes[TOP_K - 2].wait()
            writes[TOP_K - 1].wait()

    return gather_rows(y_rows, dest3)


def _final_kernel(base_ref, y_ref, gw_ref, g_ref, o_ref, *, normalize):
    h = base_ref[...]
    gw = gw_ref[...]
    for k in range(TOP_K):
        h = h + gw[:, k:k + 1] * _unpack_rows(y_ref[k])
    o_ref[...] = _rms(h, g_ref[...]) if normalize else h


def _final(base, yg, gw_rows, g_final, normalize):
    n, d = base.shape
    t = min(FIN_TILE, n)
    return pl.pallas_call(
        functools.partial(_final_kernel, normalize=normalize),
        grid=(n // t,),
        in_specs=[pl.BlockSpec((t, d), lambda i: (i, 0)), pl.BlockSpec((TOP_K, t, d // 2), lambda i: (0, i, 0)),
                  pl.BlockSpec((t, TOP_K), lambda i: (i, 0)), _full((1, d))],
        out_specs=pl.BlockSpec((t, d), lambda i: (i, 0)),
        out_shape=jax.ShapeDtypeStruct((n, d), F32),
        compiler_params=_params(("parallel",)),
        name="final",
    )(base, yg, gw_rows, g_final)


def _ssm_matrices(a_re, a_im, log_dt, b_re, b_im, c_re, c_im):
    lr, li = a_re.astype(F32), a_im.astype(F32)
    dt = jnp.exp(log_dt.astype(F32))[:, None]
    mag = jnp.exp(lr * dt)
    ab_re, ab_im = mag * jnp.cos(li * dt), mag * jnp.sin(li * dt)
    den = lr * lr + li * li
    zr, zi = ab_re - 1.0, ab_im
    k_re = (zr * lr + zi * li) / den
    k_im = (zi * lr - zr * li) / den
    br, bi = b_re.astype(F32), b_im.astype(F32)
    bb_re = k_re[..., None] * br - k_im[..., None] * bi
    bb_im = k_re[..., None] * bi + k_im[..., None] * br
    eye = jnp.eye(SSM_GROUPS, dtype=F32)

    def in_mat(bb):
        return jnp.einsum("gpc,gh->gchp", bb, eye).reshape(SSM_WIDTH, N_STATE)

    def out_mat(c):
        return jnp.einsum("gcp,gh->gphc", c.astype(F32), eye).reshape(N_STATE, SSM_WIDTH)

    bmat = jnp.concatenate([in_mat(bb_re), in_mat(bb_im)], axis=1)
    cmat = jnp.concatenate([out_mat(c_re), -out_mat(c_im)], axis=0)
    tile = lambda a: jnp.broadcast_to(a.reshape(1, N_STATE), (SUBLANES, N_STATE))
    return bmat.astype(BF16), tile(ab_re), tile(ab_im), cmat.astype(BF16)


def _plan_kernel(pstart_ref, eidx_ref, rank_ref, dest_ref, *, window):
    eidx = eidx_ref[...]
    dest = rank_ref[...]
    for e in range(N_EXPERTS):
        dest = dest + jnp.where(eidx == e, pstart_ref[e], 0)
    for c in range(dest.shape[1] // window):
        dest_ref[c] = dest[:, c * window:(c + 1) * window]


def _plan(pstart, eidx, rank, window):
    k, n = eidx.shape
    t = min(PLAN_TILE, n)
    cols = pl.BlockSpec((k, t), lambda i, ps: (0, i))
    return pl.pallas_call(
        functools.partial(_plan_kernel, window=window),
        grid_spec=pltpu.PrefetchScalarGridSpec(
            num_scalar_prefetch=1,
            grid=(n // t,),
            in_specs=[cols, cols],
            out_specs=pl.BlockSpec((t // window, k, window), lambda i, ps: (i, 0, 0))),
        out_shape=jax.ShapeDtypeStruct((n // window, k, window), I32),
        compiler_params=_params(("parallel",)),
        name="plan",
    )(pstart, eidx, rank)


def _block_schedule(counts, n_tok):
    padded = ((counts + MOE_TILE - 1) // MOE_TILE) * MOE_TILE
    pend = jnp.cumsum(padded)
    pstart = (pend - padded).astype(I32)
    n_slots = n_tok * TOP_K + N_EXPERTS * MOE_TILE
    nb = n_slots // MOE_TILE
    n_used = (pend[-1] // MOE_TILE).astype(I32)
    blk = jnp.minimum(jnp.arange(nb, dtype=I32), n_used - 1)
    block_e = jnp.sum((pend[None, :] <= (blk * MOE_TILE)[:, None]).astype(I32), axis=1)
    return pstart, jnp.minimum(block_e, N_EXPERTS - 1), n_used.reshape(1), n_slots


def kernel(x, mem, positions, g_mix, w_in, a_re, a_im, log_dt, b_re, b_im, c_re, c_im, d_skip, w_glu, g_ssm_out, lam_q1, lam_k1, lam_q2, lam_k2, g_sub, w_out, g_x, g_mem, wq_x, wk_x, wv_x, wo_x, g_ffn, w_router, router_bias, w_gate, w_up, w_down, ws_gate, ws_up, ws_down, g_final):
    b, l, d = x.shape
    n = b * l
    depth = w_in.shape[0]
    row = lambda a: a.reshape(1, -1).astype(F32)
    inv = ROPE_THETA ** (-jnp.arange(0, DA_QKDIM, 2, dtype=F32) / DA_QKDIM)
    inv_row = jnp.tile(inv, 128 // inv.shape[0]).reshape(1, 128)
    pos3 = positions.reshape(b, l, 1)

    h = x
    for i in range(depth):
        lambda_init = 0.8 - 0.6 * math.exp(-0.3 * i)
        u, q, k, v = _inproj(h, pos3, row(g_mix[i]), w_in[i].astype(BF16), inv_row)

        bmat, are8, aim8, cmat = _ssm_matrices(a_re[i], a_im[i], log_dt[i], b_re[i], b_im[i], c_re[i], c_im[i])
        u_tm = u.transpose(1, 0, 2).reshape(n, SSM_WIDTH)
        y_ssm = _ssm(u_tm, bmat, are8, aim8, cmat, row(d_skip[i]), w_glu[i].astype(BF16), row(g_ssm_out[i]), b)
        y_ssm = y_ssm.reshape(l, b, SSM_WIDTH).transpose(1, 0, 2)

        lam = (jnp.exp(jnp.sum(lam_q1[i].astype(F32) * lam_k1[i].astype(F32)))
               - jnp.exp(jnp.sum(lam_q2[i].astype(F32) * lam_k2[i].astype(F32))) + lambda_init).reshape(1)
        y_att = _diff_attention(q, k, v, lam.astype(F32), row(g_sub[i]), lambda_init)

        kmem, vmem = _memkv(mem, row(g_mem[i]), wk_x[i].astype(BF16), wv_x[i].astype(BF16))
        wo = w_out[i].astype(BF16)
        h2 = _mix(h, y_ssm, y_att, wo[:SSM_WIDTH], wo[SSM_WIDTH:], row(g_x[i]), wq_x[i].astype(BF16),
                  kmem, vmem, wo_x[i].astype(BF16))

        wsgu = jnp.concatenate([ws_gate[i], ws_up[i]], axis=1).astype(BF16)
        t_pk, base, eidx, gw, rank, counts = _route(
            h2.reshape(n, d), row(g_ffn[i]), w_router[i].T.astype(F32),
            router_bias[i].reshape(N_EXPERTS, 1).astype(F32), wsgu, ws_down[i].astype(BF16))

        pstart, block_e, n_used, n_slots = _block_schedule(counts[:, 0], n)
        dest3 = _plan(pstart, eidx, rank, SC_WINDOW)
        xs = _sc_dispatch(t_pk, dest3, n_slots)
        ys = _experts(block_e, n_used, xs, w_gate[i], w_up[i], w_down[i])
        yg = _sc_combine(ys, dest3)
        h = _final(base, yg, gw.T, row(g_final), normalize=i == depth - 1)
        h = h.reshape(b, l, d)
    return h
```

```python
import functools
import math

import jax
import jax.numpy as jnp
from jax import lax
from jax.experimental import pallas as pl
from jax.experimental.pallas import tpu as pltpu
from jax.experimental.pallas import tpu_sc as plsc

F32 = jnp.float32
BF16 = jnp.bfloat16
I32 = jnp.int32

D_MODEL = 1024
SSM_WIDTH = 512
ATTN_WIDTH = 512
SSM_GROUP = 16
SSM_GROUPS = 32
SSM_STATE = 64
N_STATE = SSM_GROUPS * SSM_STATE
DA_HEADS = 4
DA_VDIM = 128
DA_QKDIM = 64
ROPE_THETA = 10000.0
X_HEADS = 4
X_HEAD_DIM = 256
N_EXPERTS = 64
TOP_K = 8
N_EXPERT_GROUPS = 8
PER_GROUP = N_EXPERTS // N_EXPERT_GROUPS
TOPK_GROUPS = 4
D_EXPERT = 256
ROUTED_SCALE = 2.5
EPS = 1e-6

VMEM_LIMIT_V7X = 56 * 1024 * 1024
SUBLANES = 8

IN_TILE = 512
SSM_STEPS = 64
SSM_COLS = 512
ATT_TILE = 1024
ATT_ROWS = 256
MIX_TILE = 512
ROUTE_TILE = 512
MOE_TILE = 512
PLAN_TILE = 2048
MOE_PARTS = 2
FIN_TILE = 256

SC_CORES = 2
SC_WORKERS = 32
SC_WINDOW = 64


def _params(sem):
    return pltpu.CompilerParams(dimension_semantics=sem, vmem_limit_bytes=VMEM_LIMIT_V7X)


def _rms(x, g):
    return x * lax.rsqrt(jnp.mean(x * x, axis=-1, keepdims=True) + EPS) * g


def _full(shape):
    return pl.BlockSpec(shape, lambda *_: (0,) * len(shape))


def _pack_rows(a):
    w = a.shape[1] // 2
    bits = lambda v: lax.bitcast_convert_type(v.astype(BF16).astype(F32), I32)
    return (bits(a[:, w:]) & jnp.int32(-65536)) | lax.shift_right_logical(bits(a[:, :w]), 16)


def _unpack_rows(p):
    lo = lax.bitcast_convert_type(lax.shift_left(p, 16), F32)
    hi = lax.bitcast_convert_type(p & jnp.int32(-65536), F32)
    return jnp.concatenate([lo, hi], axis=1)


def _inproj_kernel(x_ref, pos_ref, g_ref, w_ref, inv_ref, u_ref, q_ref, k_ref, v_ref):
    x = x_ref[0]
    a = _rms(x, g_ref[...]).astype(BF16)
    z = jnp.dot(a, w_ref[...], preferred_element_type=F32)
    u_ref[0] = z[:, :SSM_WIDTH]
    ang = pos_ref[0].astype(F32) * inv_ref[...]
    cos = jnp.concatenate([jnp.cos(ang)] * 4, axis=1)
    sin = jnp.concatenate([jnp.sin(ang)] * 4, axis=1)
    lane = lax.broadcasted_iota(I32, cos.shape, 1)
    first = (lane & (DA_QKDIM - 1)) < DA_QKDIM // 2
    half = DA_QKDIM // 2

    def rope(t):
        rot = jnp.where(first, -pltpu.roll(t, ATTN_WIDTH - half, 1), pltpu.roll(t, half, 1))
        return t * cos + rot * sin

    q = z[:, SSM_WIDTH:SSM_WIDTH + ATTN_WIDTH]
    k = z[:, SSM_WIDTH + ATTN_WIDTH:SSM_WIDTH + 2 * ATTN_WIDTH]
    q_ref[0] = (rope(q) * (DA_QKDIM ** -0.5)).astype(BF16)
    k_ref[0] = rope(k).astype(BF16)
    v_ref[0] = z[:, SSM_WIDTH + 2 * ATTN_WIDTH:].astype(BF16)


def _inproj(x, pos3, g_mix, w_in, inv_row):
    b, l, d = x.shape
    t = min(IN_TILE, l)
    n_out = w_in.shape[1]
    row = lambda w: pl.BlockSpec((1, t, w), lambda i, j: (i, j, 0))
    return pl.pallas_call(
        _inproj_kernel,
        grid=(b, l // t),
        in_specs=[row(d), row(1), _full((1, d)), _full((d, n_out)), _full((1, 128))],
        out_specs=[row(SSM_WIDTH), row(ATTN_WIDTH), row(ATTN_WIDTH), row(ATTN_WIDTH)],
        out_shape=[jax.ShapeDtypeStruct((b, l, SSM_WIDTH), F32)]
        + [jax.ShapeDtypeStruct((b, l, ATTN_WIDTH), BF16)] * 3,
        compiler_params=_params(("parallel", "parallel")),
        name="inproj",
    )(x, pos3, g_mix, w_in, inv_row)


def _ssm_kernel(u_ref, bm_ref, are_ref, aim_ref, cm_ref, dskip_ref, wglu_ref, g_ref, o_ref, bu_ref, st_ref,
                *, steps):
    @pl.when(pl.program_id(0) == 0)
    def _():
        st_ref[...] = jnp.zeros_like(st_ref)

    u = u_ref[...]
    bu_ref[...] = jnp.dot(u.astype(BF16), bm_ref[...], preferred_element_type=F32)

    for c0 in range(0, N_STATE, SSM_COLS):
        re = slice(c0, c0 + SSM_COLS)
        im = slice(N_STATE + c0, N_STATE + c0 + SSM_COLS)
        ar = are_ref[:, re]
        ai = aim_ref[:, re]

        def step(t, carry, re=re, im=im, ar=ar, ai=ai):
            sr, si = carry
            rows = pl.ds(pl.multiple_of(t * SUBLANES, SUBLANES), SUBLANES)
            nr = ar * sr - ai * si + bu_ref[rows, re]
            ni = ar * si + ai * sr + bu_ref[rows, im]
            bu_ref[rows, re] = nr
            bu_ref[rows, im] = ni
            return nr, ni

        sr, si = lax.fori_loop(0, steps, step, (st_ref[:, re], st_ref[:, im]))
        st_ref[:, re] = sr
        st_ref[:, im] = si

    y = jnp.dot(bu_ref[...].astype(BF16), cm_ref[...], preferred_element_type=F32) + dskip_ref[...] * u
    y = jax.nn.gelu(y)
    y = y * jax.nn.sigmoid(jnp.dot(y.astype(BF16), wglu_ref[...], preferred_element_type=F32))
    o_ref[...] = _rms(y, g_ref[...]).astype(BF16)


def _ssm(u_tm, bmat, a_re8, a_im8, cmat, d_skip, w_glu, g_out, batch):
    assert batch == SUBLANES, "one time step of all sequences must fill the sublanes"
    n = u_tm.shape[0]
    steps = min(SSM_STEPS, n // batch)
    rows = steps * batch
    return pl.pallas_call(
        functools.partial(_ssm_kernel, steps=steps),
        grid=(n // rows,),
        in_specs=[pl.BlockSpec((rows, SSM_WIDTH), lambda i: (i, 0)),
                  _full((SSM_WIDTH, 2 * N_STATE)), _full((SUBLANES, N_STATE)), _full((SUBLANES, N_STATE)),
                  _full((2 * N_STATE, SSM_WIDTH)), _full((1, SSM_WIDTH)), _full((SSM_WIDTH, SSM_WIDTH)),
                  _full((1, SSM_WIDTH))],
        out_specs=pl.BlockSpec((rows, SSM_WIDTH), lambda i: (i, 0)),
        out_shape=jax.ShapeDtypeStruct((n, SSM_WIDTH), BF16),
        scratch_shapes=[pltpu.VMEM((rows, 2 * N_STATE), F32), pltpu.VMEM((SUBLANES, 2 * N_STATE), F32)],
        compiler_params=_params(("arbitrary",)),
        name="ssm",
    )(u_tm, bmat, a_re8, a_im8, cmat, d_skip, w_glu, g_out)


def _attn_kernel(qi_ref, ki_ref, lam_ref, q_ref, k_ref, v_ref, g_ref, o_ref, qs_ref, m_ref, acc_ref, *,
                 tile, out_scale):
    qi = qi_ref[pl.program_id(2)]
    ki = ki_ref[pl.program_id(2)]

    @pl.when(ki == 0)
    def _():
        q = q_ref[0]
        lane = lax.broadcasted_iota(I32, q.shape, 1)
        zero = jnp.zeros_like(q)
        qs_ref[:tile] = jnp.where(lane < DA_QKDIM, q, zero)
        qs_ref[tile:] = jnp.where(lane >= DA_QKDIM, q, zero)
        m_ref[...] = jnp.full_like(m_ref, -jnp.inf)
        acc_ref[...] = jnp.zeros_like(acc_ref)

    def update(masked):
        v_ext = jnp.concatenate([v_ref[0], jnp.ones((tile, DA_VDIM), BF16)], axis=1)
        rb = min(ATT_ROWS, tile)
        n_rb = 2 * tile // rb

        def n_keys(r):
            return (r * rb) % tile + rb if masked else tile

        def scores(r):
            return lax.dot_general(qs_ref[r * rb:(r + 1) * rb], k_ref[0, :n_keys(r)], (((1,), (1,)), ((), ())),
                                   preferred_element_type=F32)

        s_next = scores(0)
        for r in range(n_rb):
            s = s_next
            if r + 1 < n_rb:
                s_next = scores(r + 1)
            rows = slice(r * rb, (r + 1) * rb)
            nk = n_keys(r)
            if masked:
                qpos = lax.broadcasted_iota(I32, s.shape, 0) + (r * rb) % tile
                s = jnp.where(lax.broadcasted_iota(I32, s.shape, 1) <= qpos, s, jnp.finfo(F32).min)
            m_old = m_ref[rows]
            m_new = jnp.maximum(m_old, jnp.max(s, axis=1, keepdims=True))
            p = jnp.exp(s - jnp.concatenate([m_new] * (nk // DA_VDIM), axis=1))
            alpha = jnp.exp(m_old - m_new)
            acc_ref[rows] = (jnp.concatenate([alpha, alpha], axis=1) * acc_ref[rows]
                             + jnp.dot(p.astype(BF16), v_ext[:nk], preferred_element_type=F32))
            m_ref[rows] = m_new

    @pl.when(ki < qi)
    def _():
        update(False)

    @pl.when(ki == qi)
    def _():
        update(True)
        o = acc_ref[:, :DA_VDIM] / acc_ref[:, DA_VDIM:]
        o = o[:tile] - lam_ref[0] * o[tile:]
        o_ref[0] = (_rms(o, g_ref[...]) * out_scale).astype(BF16)


def _diff_attention(q, k, v, lam, g_sub, lambda_init):
    b, l, _ = q.shape
    t = min(ATT_TILE, l)
    n = l // t
    pairs = [(qi, ki) for qi in range(n) for ki in range(qi + 1)]
    qi_tab = jnp.asarray([p[0] for p in pairs], I32)
    ki_tab = jnp.asarray([p[1] for p in pairs], I32)
    qspec = pl.BlockSpec((1, t, DA_VDIM), lambda bi, h, s, qt, kt: (bi, qt[s], h))
    kspec = pl.BlockSpec((1, t, DA_VDIM), lambda bi, h, s, qt, kt: (bi, kt[s], h))
    return pl.pallas_call(
        functools.partial(_attn_kernel, tile=t, out_scale=1.0 - lambda_init),
        grid_spec=pltpu.PrefetchScalarGridSpec(
            num_scalar_prefetch=2,
            grid=(b, DA_HEADS, len(pairs)),
            in_specs=[pl.BlockSpec(memory_space=pltpu.SMEM), qspec, kspec, kspec,
                      pl.BlockSpec((1, DA_VDIM), lambda bi, h, s, qt, kt: (0, 0))],
            out_specs=qspec,
            scratch_shapes=[pltpu.VMEM((2 * t, DA_VDIM), BF16), pltpu.VMEM((2 * t, DA_VDIM), F32),
                            pltpu.VMEM((2 * t, 2 * DA_VDIM), F32)]),
        out_shape=jax.ShapeDtypeStruct((b, l, ATTN_WIDTH), BF16),
        compiler_params=_params(("parallel", "parallel", "arbitrary")),
        name="diffattn",
    )(qi_tab, ki_tab, lam, q, k, v, g_sub)


def _memkv_kernel(m_ref, g_ref, wk_ref, wv_ref, k_ref, v_ref):
    a = _rms(m_ref[0], g_ref[...]).astype(BF16)
    k_ref[0] = jnp.dot(a, wk_ref[...], preferred_element_type=F32).astype(BF16)
    v_ref[0] = jnp.dot(a, wv_ref[...], preferred_element_type=F32).astype(BF16)


def _memkv(mem, g_mem, wk, wv):
    b, m, d = mem.shape
    blk = pl.BlockSpec((1, m, d), lambda i: (i, 0, 0))
    return pl.pallas_call(
        _memkv_kernel,
        grid=(b,),
        in_specs=[blk, _full((1, d)), _full((d, d)), _full((d, d))],
        out_specs=[blk, blk],
        out_shape=[jax.ShapeDtypeStruct((b, m, d), BF16)] * 2,
        compiler_params=_params(("parallel",)),
        name="memkv",
    )(mem, g_mem, wk, wv)


def _mix_kernel(x_ref, ys_ref, ya_ref, wo1_ref, wo2_ref, gx_ref, wq_ref, km_ref, vm_ref, wo_ref, h_ref):
    h = (x_ref[0]
         + jnp.dot(ys_ref[0], wo1_ref[...], preferred_element_type=F32)
         + jnp.dot(ya_ref[0], wo2_ref[...], preferred_element_type=F32))
    hq = _rms(h, gx_ref[...]).astype(BF16)
    q = jnp.dot(hq, wq_ref[...], preferred_element_type=F32).astype(BF16)
    outs = []
    for hd in range(X_HEADS):
        cols = slice(hd * X_HEAD_DIM, (hd + 1) * X_HEAD_DIM)
        s = lax.dot_general(q[:, cols], km_ref[0, :, cols], (((1,), (1,)), ((), ())),
                            preferred_element_type=F32) * (X_HEAD_DIM ** -0.5)
        s = s - jnp.max(s, axis=1, keepdims=True)
        p = jnp.exp(s)
        p = p / jnp.sum(p, axis=1, keepdims=True)
        outs.append(jnp.dot(p.astype(BF16), vm_ref[0, :, cols], preferred_element_type=F32).astype(BF16))
    o = jnp.concatenate(outs, axis=1)
    h_ref[0] = h + jnp.dot(o, wo_ref[...], preferred_element_type=F32)


def _mix(x, y_ssm, y_att, wo1, wo2, g_x, wq, kmem, vmem, wo):
    b, l, d = x.shape
    t = min(MIX_TILE, l)
    m = kmem.shape[1]
    row = lambda w: pl.BlockSpec((1, t, w), lambda i, j: (i, j, 0))
    mem = pl.BlockSpec((1, m, d), lambda i, j: (i, 0, 0))
    return pl.pallas_call(
        _mix_kernel,
        grid=(b, l // t),
        in_specs=[row(d), row(SSM_WIDTH), row(ATTN_WIDTH), _full((SSM_WIDTH, d)), _full((ATTN_WIDTH, d)),
                  _full((1, d)), _full((d, d)), mem, mem, _full((d, d))],
        out_specs=row(d),
        out_shape=jax.ShapeDtypeStruct((b, l, d), F32),
        compiler_params=_params(("parallel", "parallel")),
        name="mix",
    )(x, y_ssm, y_att, wo1, wo2, g_x, wq, kmem, vmem, wo)


def _first_index(hit, idx, sentinel):
    return jnp.min(jnp.where(hit, idx, sentinel), axis=0, keepdims=True)


def _route_kernel(h_ref, g_ref, wr_ref, bias_ref, wsgu_ref, wsd_ref, tri_ref,
                  t_ref, base_ref, eidx_ref, gw_ref, rank_ref, cnt_ref, carry_ref):
    @pl.when(pl.program_id(0) == 0)
    def _():
        carry_ref[...] = jnp.zeros_like(carry_ref)

    h = h_ref[...]
    t = _rms(h, g_ref[...])
    tb = t.astype(BF16)
    t_ref[...] = _pack_rows(t)
    gu = jnp.dot(tb, wsgu_ref[...], preferred_element_type=F32)
    hid = jax.nn.silu(gu[:, :D_EXPERT]) * gu[:, D_EXPERT:]
    base_ref[...] = h + jnp.dot(hid.astype(BF16), wsd_ref[...], preferred_element_type=F32)

    logits = lax.dot_general(wr_ref[...], t, (((1,), (1,)), ((), ())), preferred_element_type=F32,
                             precision=lax.Precision.HIGHEST)
    scores = jax.nn.sigmoid(logits)
    biased = scores + bias_ref[...]
    n_tok = scores.shape[1]
    neg = -jnp.inf
    sub = lax.broadcasted_iota(I32, (PER_GROUP, n_tok), 0)

    gs = []
    for g in range(N_EXPERT_GROUPS):
        blk = biased[g * PER_GROUP:(g + 1) * PER_GROUP]
        m1 = jnp.max(blk, axis=0, keepdims=True)
        i1 = _first_index(blk == m1, sub, PER_GROUP)
        m2 = jnp.max(jnp.where(sub == i1, neg, blk), axis=0, keepdims=True)
        gs.append(m1 + m2)
    gs = jnp.concatenate(gs, axis=0)

    gsel = jnp.zeros(gs.shape, jnp.bool_)
    for _ in range(TOPK_GROUPS):
        m = jnp.max(gs, axis=0, keepdims=True)
        hit = sub == _first_index(gs == m, sub, N_EXPERT_GROUPS)
        gsel = jnp.logical_or(gsel, hit)
        gs = jnp.where(hit, neg, gs)

    masked = jnp.concatenate(
        [jnp.where(gsel[g:g + 1], biased[g * PER_GROUP:(g + 1) * PER_GROUP], neg)
         for g in range(N_EXPERT_GROUPS)], axis=0)
    eid = lax.broadcasted_iota(I32, masked.shape, 0)
    sel = jnp.zeros(masked.shape, jnp.bool_)
    idxs, gws = [], []
    for _ in range(TOP_K):
        m = jnp.max(masked, axis=0, keepdims=True)
        i = _first_index(masked == m, eid, N_EXPERTS)
        hit = eid == i
        idxs.append(i)
        gws.append(jnp.sum(jnp.where(hit, scores, 0.0), axis=0, keepdims=True))
        sel = jnp.logical_or(sel, hit)
        masked = jnp.where(hit, neg, masked)
    eidx = jnp.concatenate(idxs, axis=0)
    gw = jnp.concatenate(gws, axis=0)
    gw = gw / jnp.sum(gw, axis=0, keepdims=True) * ROUTED_SCALE
    eidx_ref[...] = eidx
    gw_ref[...] = gw

    before = jnp.dot(sel.astype(BF16), tri_ref[...], preferred_element_type=F32) + carry_ref[...]
    rank_ref[...] = jnp.concatenate(
        [jnp.sum(jnp.where(eid == idxs[k], before, 0.0), axis=0, keepdims=True) for k in range(TOP_K)],
        axis=0).astype(I32)
    carry = carry_ref[...] + jnp.sum(sel.astype(F32), axis=1, keepdims=True)
    carry_ref[...] = carry
    cnt_ref[...] = carry.astype(I32)


def _route(h2, g_ffn, wr_t, bias_col, wsgu, wsd, part, n_parts):
    n_all, d = h2.shape
    n = n_all // n_parts
    t = min(ROUTE_TILE, n)
    blk0 = part * (n // t)
    tri = (lax.broadcasted_iota(I32, (t, t), 0) < lax.broadcasted_iota(I32, (t, t), 1)).astype(BF16)
    row = lambda w: pl.BlockSpec((t, w), lambda i: (i, 0))
    col = pl.BlockSpec((TOP_K, t), lambda i: (0, i))
    return pl.pallas_call(
        _route_kernel,
        grid=(n // t,),
        in_specs=[pl.BlockSpec((t, d), lambda i: (i + blk0, 0)), _full((1, d)), _full((N_EXPERTS, d)),
                  _full((N_EXPERTS, 1)),
                  _full((d, 2 * D_EXPERT)), _full((D_EXPERT, d)), _full((t, t))],
        out_specs=[row(d // 2), row(d), col, col, col, _full((N_EXPERTS, 1))],
        out_shape=[jax.ShapeDtypeStruct((n, d // 2), I32), jax.ShapeDtypeStruct((n, d), F32),
                   jax.ShapeDtypeStruct((TOP_K, n), I32), jax.ShapeDtypeStruct((TOP_K, n), F32),
                   jax.ShapeDtypeStruct((TOP_K, n), I32), jax.ShapeDtypeStruct((N_EXPERTS, 1), I32)],
        scratch_shapes=[pltpu.VMEM((N_EXPERTS, 1), F32)],
        compiler_params=_params(("arbitrary",)),
        name="route",
    )(h2, g_ffn, wr_t, bias_col, wsgu, wsd, tri)


def _expert_kernel(be_ref, nu_ref, x_ref, wg_ref, wu_ref, wd_ref, y_ref, wg_s, wu_s, wd_s):
    i = pl.program_id(0)

    @pl.when(jnp.logical_or(i == 0, be_ref[i] != be_ref[jnp.maximum(i - 1, 0)]))
    def _():
        wg_s[...] = wg_ref[0].astype(BF16)
        wu_s[...] = wu_ref[0].astype(BF16)
        wd_s[...] = wd_ref[0].astype(BF16)

    @pl.when(i < nu_ref[0])
    def _():
        x = _unpack_rows(x_ref[...]).astype(BF16)
        gate = jnp.dot(x, wg_s[...], preferred_element_type=F32)
        up = jnp.dot(x, wu_s[...], preferred_element_type=F32)
        hid = (jax.nn.silu(gate) * up).astype(BF16)
        y_ref[...] = _pack_rows(jnp.dot(hid, wd_s[...], preferred_element_type=F32))


def _experts(block_e, n_used, xs, w_gate, w_up, w_down):
    n_slots, dw = xs.shape
    _, d, de = w_gate.shape
    nb = n_slots // MOE_TILE
    rows = pl.BlockSpec((MOE_TILE, dw), lambda i, be, nu: (jnp.minimum(i, nu[0] - 1), 0))
    w_in = pl.BlockSpec((1, d, de), lambda i, be, nu: (be[i], 0, 0))
    return pl.pallas_call(
        _expert_kernel,
        grid_spec=pltpu.PrefetchScalarGridSpec(
            num_scalar_prefetch=2,
            grid=(nb,),
            in_specs=[rows, w_in, w_in, pl.BlockSpec((1, de, d), lambda i, be, nu: (be[i], 0, 0))],
            out_specs=rows,
            scratch_shapes=[pltpu.VMEM((d, de), BF16), pltpu.VMEM((d, de), BF16), pltpu.VMEM((de, d), BF16)]),
        out_shape=jax.ShapeDtypeStruct((n_slots, dw), I32),
        compiler_params=_params(("arbitrary",)),
        name="experts",
    )(block_e, n_used, xs, w_gate, w_up, w_down)


def _sc_worker_id():
    return lax.axis_index("s") * SC_CORES + lax.axis_index("c")


def _sc_dispatch(t_rows, dest3, n_slots):
    _, dw = t_rows.shape
    n_chunks, _, w = dest3.shape
    per_worker = n_chunks // SC_WORKERS
    assert per_worker % 2 == 0
    mesh = plsc.VectorSubcoreMesh(core_axis_name="c", subcore_axis_name="s")
    dt = t_rows.dtype

    @functools.partial(
        pl.kernel, mesh=mesh,
        out_type=jax.ShapeDtypeStruct((n_slots, dw), dt),
        scratch_types=[pltpu.VMEM((TOP_K, w), I32), pltpu.VMEM((TOP_K, w), I32),
                       pltpu.VMEM((w, dw), dt), pltpu.VMEM((w, dw), dt),
                       pltpu.SemaphoreType.DMA, pltpu.SemaphoreType.DMA, pltpu.SemaphoreType.DMA],
    )
    def scatter_rows(t_hbm, dest_hbm, xs_hbm, idx_a, idx_b, rows_a, rows_b, sem_load, sem_a, sem_b):
        wid = _sc_worker_id()

        def scatter(idx_v, rows_v, sem):
            return [pltpu.async_copy(rows_v, xs_hbm.at[idx_v.at[k]], sem) for k in range(TOP_K)]

        @pl.loop(0, per_worker, step=2)
        def _(j):
            ca = wid * per_worker + j
            cb = ca + 1
            pltpu.sync_copy(dest_hbm.at[ca], idx_a)
            pltpu.sync_copy(t_hbm.at[pl.ds(ca * w, w)], rows_a)
            load_idx = pltpu.async_copy(dest_hbm.at[cb], idx_b, sem_load)
            load_rows = pltpu.async_copy(t_hbm.at[pl.ds(cb * w, w)], rows_b, sem_load)
            out_a = scatter(idx_a, rows_a, sem_a)
            load_idx.wait()
            load_rows.wait()
            out_b = scatter(idx_b, rows_b, sem_b)
            for cp in out_a + out_b:
                cp.wait()

    return scatter_rows(t_rows, dest3)


def _sc_combine(y_rows, dest3):
    _, dw = y_rows.shape
    n_chunks, _, w = dest3.shape
    per_worker = n_chunks // SC_WORKERS
    mesh = plsc.VectorSubcoreMesh(core_axis_name="c", subcore_axis_name="s")
    dt = y_rows.dtype

    @functools.partial(
        pl.kernel, mesh=mesh,
        out_type=jax.ShapeDtypeStruct((TOP_K, n_chunks * w, dw), dt),
        scratch_types=[pltpu.VMEM((TOP_K, w), I32), pltpu.VMEM((w, dw), dt), pltpu.VMEM((w, dw), dt),
                       pltpu.SemaphoreType.DMA, pltpu.SemaphoreType.DMA,
                       pltpu.SemaphoreType.DMA, pltpu.SemaphoreType.DMA],
    )
    def gather_rows(y_hbm, dest_hbm, out_hbm, idx_v, buf0, buf1, gsem0, gsem1, wsem0, wsem1):
        wid = _sc_worker_id()
        bufs, gsems, wsems = (buf0, buf1), (gsem0, gsem1), (wsem0, wsem1)

        @pl.loop(0, per_worker)
        def _(j):
            c = wid * per_worker + j
            pltpu.sync_copy(dest_hbm.at[c], idx_v)
            gathers = [None] * TOP_K
            writes = [None] * TOP_K
            gathers[0] = pltpu.async_copy(y_hbm.at[idx_v.at[0]], bufs[0], gsems[0])
            for k in range(TOP_K):
                b = k % 2
                if k + 1 < TOP_K:
                    if k >= 1:
                        writes[k - 1].wait()
                    gathers[k + 1] = pltpu.async_copy(y_hbm.at[idx_v.at[k + 1]], bufs[1 - b], gsems[1 - b])
                gathers[k].wait()
                writes[k] = pltpu.async_copy(bufs[b], out_hbm.at[k, pl.ds(c * w, w)], wsems[b])
            writes[TOP_K - 2].wait()
            writes[TOP_K - 1].wait()

    return gather_rows(y_rows, dest3)


def _final_kernel(base_ref, y_ref, gw_ref, g_ref, *rest, normalize):
    o_ref = rest[-1]
    h = base_ref[...]
    gw = gw_ref[...]
    for k in range(TOP_K):
        h = h + gw[:, k:k + 1] * _unpack_rows(y_ref[k])
    o_ref[...] = _rms(h, g_ref[...]) if normalize else h


def _final(base, yg, gw_rows, g_final, normalize, out_prev, part, n_parts):
    n, d = base.shape
    t = min(FIN_TILE, n)
    blk0 = part * (n // t)
    in_specs = [pl.BlockSpec((t, d), lambda i: (i, 0)), pl.BlockSpec((TOP_K, t, d // 2), lambda i: (0, i, 0)),
                pl.BlockSpec((t, TOP_K), lambda i: (i, 0)), _full((1, d))]
    args = [base, yg, gw_rows, g_final]
    aliases = {}
    if out_prev is not None:
        in_specs.append(pl.BlockSpec(memory_space=pl.ANY))
        args.append(out_prev)
        aliases = {len(args) - 1: 0}
    return pl.pallas_call(
        functools.partial(_final_kernel, normalize=normalize),
        grid=(n // t,),
        in_specs=in_specs,
        out_specs=pl.BlockSpec((t, d), lambda i: (i + blk0, 0)),
        out_shape=jax.ShapeDtypeStruct((n * n_parts, d), F32),
        input_output_aliases=aliases,
        compiler_params=_params(("parallel",)),
        name="final",
    )(*args)


def _ssm_matrices(a_re, a_im, log_dt, b_re, b_im, c_re, c_im):
    lr, li = a_re.astype(F32), a_im.astype(F32)
    dt = jnp.exp(log_dt.astype(F32))[:, None]
    mag = jnp.exp(lr * dt)
    ab_re, ab_im = mag * jnp.cos(li * dt), mag * jnp.sin(li * dt)
    den = lr * lr + li * li
    zr, zi = ab_re - 1.0, ab_im
    k_re = (zr * lr + zi * li) / den
    k_im = (zi * lr - zr * li) / den
    br, bi = b_re.astype(F32), b_im.astype(F32)
    bb_re = k_re[..., None] * br - k_im[..., None] * bi
    bb_im = k_re[..., None] * bi + k_im[..., None] * br
    eye = jnp.eye(SSM_GROUPS, dtype=F32)

    def in_mat(bb):
        return jnp.einsum("gpc,gh->gchp", bb, eye).reshape(SSM_WIDTH, N_STATE)

    def out_mat(c):
        return jnp.einsum("gcp,gh->gphc", c.astype(F32), eye).reshape(N_STATE, SSM_WIDTH)

    bmat = jnp.concatenate([in_mat(bb_re), in_mat(bb_im)], axis=1)
    cmat = jnp.concatenate([out_mat(c_re), -out_mat(c_im)], axis=0)
    tile = lambda a: jnp.broadcast_to(a.reshape(1, N_STATE), (SUBLANES, N_STATE))
    return bmat.astype(BF16), tile(ab_re), tile(ab_im), cmat.astype(BF16)


def _plan_kernel(pstart_ref, eidx_ref, rank_ref, dest_ref, *, window):
    eidx = eidx_ref[...]
    dest = rank_ref[...]
    for e in range(N_EXPERTS):
        dest = dest + jnp.where(eidx == e, pstart_ref[e], 0)
    for c in range(dest.shape[1] // window):
        dest_ref[c] = dest[:, c * window:(c + 1) * window]


def _plan(pstart, eidx, rank, window):
    k, n = eidx.shape
    t = min(PLAN_TILE, n)
    cols = pl.BlockSpec((k, t), lambda i, ps: (0, i))
    return pl.pallas_call(
        functools.partial(_plan_kernel, window=window),
        grid_spec=pltpu.PrefetchScalarGridSpec(
            num_scalar_prefetch=1,
            grid=(n // t,),
            in_specs=[cols, cols],
            out_specs=pl.BlockSpec((t // window, k, window), lambda i, ps: (i, 0, 0))),
        out_shape=jax.ShapeDtypeStruct((n // window, k, window), I32),
        compiler_params=_params(("parallel",)),
        name="plan",
    )(pstart, eidx, rank)


def _block_schedule(counts, n_tok):
    padded = ((counts + MOE_TILE - 1) // MOE_TILE) * MOE_TILE
    pend = jnp.cumsum(padded)
    pstart = (pend - padded).astype(I32)
    n_slots = n_tok * TOP_K + N_EXPERTS * MOE_TILE
    nb = n_slots // MOE_TILE
    n_used = (pend[-1] // MOE_TILE).astype(I32)
    blk = jnp.minimum(jnp.arange(nb, dtype=I32), n_used - 1)
    block_e = jnp.sum((pend[None, :] <= (blk * MOE_TILE)[:, None]).astype(I32), axis=1)
    return pstart, jnp.minimum(block_e, N_EXPERTS - 1), n_used.reshape(1), n_slots


def kernel(x, mem, positions, g_mix, w_in, a_re, a_im, log_dt, b_re, b_im, c_re, c_im, d_skip, w_glu, g_ssm_out, lam_q1, lam_k1, lam_q2, lam_k2, g_sub, w_out, g_x, g_mem, wq_x, wk_x, wv_x, wo_x, g_ffn, w_router, router_bias, w_gate, w_up, w_down, ws_gate, ws_up, ws_down, g_final):
    b, l, d = x.shape
    n = b * l
    depth = w_in.shape[0]
    row = lambda a: a.reshape(1, -1).astype(F32)
    inv = ROPE_THETA ** (-jnp.arange(0, DA_QKDIM, 2, dtype=F32) / DA_QKDIM)
    inv_row = jnp.tile(inv, 128 // inv.shape[0]).reshape(1, 128)
    pos3 = positions.reshape(b, l, 1)

    h = x
    for i in range(depth):
        lambda_init = 0.8 - 0.6 * math.exp(-0.3 * i)
        u, q, k, v = _inproj(h, pos3, row(g_mix[i]), w_in[i].astype(BF16), inv_row)

        bmat, are8, aim8, cmat = _ssm_matrices(a_re[i], a_im[i], log_dt[i], b_re[i], b_im[i], c_re[i], c_im[i])
        u_tm = u.transpose(1, 0, 2).reshape(n, SSM_WIDTH)
        y_ssm = _ssm(u_tm, bmat, are8, aim8, cmat, row(d_skip[i]), w_glu[i].astype(BF16), row(g_ssm_out[i]), b)
        y_ssm = y_ssm.reshape(l, b, SSM_WIDTH).transpose(1, 0, 2)

        lam = (jnp.exp(jnp.sum(lam_q1[i].astype(F32) * lam_k1[i].astype(F32)))
               - jnp.exp(jnp.sum(lam_q2[i].astype(F32) * lam_k2[i].astype(F32))) + lambda_init).reshape(1)
        y_att = _diff_attention(q, k, v, lam.astype(F32), row(g_sub[i]), lambda_init)

        kmem, vmem = _memkv(mem, row(g_mem[i]), wk_x[i].astype(BF16), wv_x[i].astype(BF16))
        wo = w_out[i].astype(BF16)
        h2 = _mix(h, y_ssm, y_att, wo[:SSM_WIDTH], wo[SSM_WIDTH:], row(g_x[i]), wq_x[i].astype(BF16),
                  kmem, vmem, wo_x[i].astype(BF16))

        wsgu = jnp.concatenate([ws_gate[i], ws_up[i]], axis=1).astype(BF16)
        n_part = n // MOE_PARTS
        out = None
        for part in range(MOE_PARTS):
            t_pk, base, eidx, gw, rank, counts = _route(
                h2.reshape(n, d), row(g_ffn[i]), w_router[i].T.astype(F32),
                router_bias[i].reshape(N_EXPERTS, 1).astype(F32), wsgu, ws_down[i].astype(BF16), part, MOE_PARTS)
            pstart, block_e, n_used, n_slots = _block_schedule(counts[:, 0], n_part)
            dest3 = _plan(pstart, eidx, rank, SC_WINDOW)
            xs = _sc_dispatch(t_pk, dest3, n_slots)
            ys = _experts(block_e, n_used, xs, w_gate[i], w_up[i], w_down[i])
            yg = _sc_combine(ys, dest3)
            out = _final(base, yg, gw.T, row(g_final), i == depth - 1, out, part, MOE_PARTS)
        h = out.reshape(b, l, d)
    return h
```

```python
import functools
import math

import jax
import jax.numpy as jnp
from jax import lax
from jax.experimental import pallas as pl
from jax.experimental.pallas import tpu as pltpu
from jax.experimental.pallas import tpu_sc as plsc

F32 = jnp.float32
BF16 = jnp.bfloat16
I32 = jnp.int32

D_MODEL = 1024
SSM_WIDTH = 512
ATTN_WIDTH = 512
SSM_GROUP = 16
SSM_GROUPS = 32
SSM_STATE = 64
N_STATE = SSM_GROUPS * SSM_STATE
DA_HEADS = 4
DA_VDIM = 128
DA_QKDIM = 64
ROPE_THETA = 10000.0
X_HEADS = 4
X_HEAD_DIM = 256
N_EXPERTS = 64
TOP_K = 8
N_EXPERT_GROUPS = 8
PER_GROUP = N_EXPERTS // N_EXPERT_GROUPS
TOPK_GROUPS = 4
D_EXPERT = 256
ROUTED_SCALE = 2.5
EPS = 1e-6

VMEM_LIMIT_V7X = 56 * 1024 * 1024
SUBLANES = 8

IN_TILE = 512
SSM_STEPS = 64
SSM_COLS = 1024
SSM_CH = 128
SSM_SLAB = 512
ATT_TILE = 1024
ATT_ROWS = 256
MIX_TILE = 512
ROUTE_TILE = 512
MOE_TILE = 512
PLAN_TILE = 2048
MOE_PARTS = 2
FIN_TILE = 256

SC_CORES = 2
SC_WORKERS = 32
SC_WINDOW = 64


def _params(sem):
    return pltpu.CompilerParams(dimension_semantics=sem, vmem_limit_bytes=VMEM_LIMIT_V7X)


def _rms(x, g):
    return x * lax.rsqrt(jnp.mean(x * x, axis=-1, keepdims=True) + EPS) * g


def _full(shape):
    return pl.BlockSpec(shape, lambda *_: (0,) * len(shape))


def _pack_rows(a):
    w = a.shape[1] // 2
    bits = lambda v: lax.bitcast_convert_type(v.astype(BF16).astype(F32), I32)
    return (bits(a[:, w:]) & jnp.int32(-65536)) | lax.shift_right_logical(bits(a[:, :w]), 16)


def _unpack_rows(p):
    lo = lax.bitcast_convert_type(lax.shift_left(p, 16), F32)
    hi = lax.bitcast_convert_type(p & jnp.int32(-65536), F32)
    return jnp.concatenate([lo, hi], axis=1)


def _inproj_kernel(x_ref, pos_ref, g_ref, w_ref, inv_ref, u_ref, q_ref, k_ref, v_ref):
    x = x_ref[0]
    a = _rms(x, g_ref[...]).astype(BF16)
    z = jnp.dot(a, w_ref[...], preferred_element_type=F32)
    u_ref[0] = z[:, :SSM_WIDTH]
    ang = pos_ref[0].astype(F32) * inv_ref[...]
    cos = jnp.concatenate([jnp.cos(ang)] * 4, axis=1)
    sin = jnp.concatenate([jnp.sin(ang)] * 4, axis=1)
    lane = lax.broadcasted_iota(I32, cos.shape, 1)
    first = (lane & (DA_QKDIM - 1)) < DA_QKDIM // 2
    half = DA_QKDIM // 2

    def rope(t):
        rot = jnp.where(first, -pltpu.roll(t, ATTN_WIDTH - half, 1), pltpu.roll(t, half, 1))
        return t * cos + rot * sin

    q = z[:, SSM_WIDTH:SSM_WIDTH + ATTN_WIDTH]
    k = z[:, SSM_WIDTH + ATTN_WIDTH:SSM_WIDTH + 2 * ATTN_WIDTH]
    q_ref[0] = (rope(q) * (DA_QKDIM ** -0.5)).astype(BF16)
    k_ref[0] = rope(k).astype(BF16)
    v_ref[0] = z[:, SSM_WIDTH + 2 * ATTN_WIDTH:].astype(BF16)


def _inproj(x, pos3, g_mix, w_in, inv_row):
    b, l, d = x.shape
    t = min(IN_TILE, l)
    n_out = w_in.shape[1]
    row = lambda w: pl.BlockSpec((1, t, w), lambda i, j: (i, j, 0))
    return pl.pallas_call(
        _inproj_kernel,
        grid=(b, l // t),
        in_specs=[row(d), row(1), _full((1, d)), _full((d, n_out)), _full((1, 128))],
        out_specs=[row(SSM_WIDTH), row(ATTN_WIDTH), row(ATTN_WIDTH), row(ATTN_WIDTH)],
        out_shape=[jax.ShapeDtypeStruct((b, l, SSM_WIDTH), F32)]
        + [jax.ShapeDtypeStruct((b, l, ATTN_WIDTH), BF16)] * 3,
        compiler_params=_params(("parallel", "parallel")),
        name="inproj",
    )(x, pos3, g_mix, w_in, inv_row)


def _ssm_kernel(u_ref, bm_ref, are_ref, aim_ref, cm_ref, dskip_ref, wglu_ref, g_ref, o_ref, bu_ref, st_ref,
                *, steps):
    @pl.when(pl.program_id(0) == 0)
    def _():
        st_ref[...] = jnp.zeros_like(st_ref)

    u = u_ref[...]
    ub = u.astype(BF16)

    n_slabs = SSM_WIDTH // SSM_CH
    for s in range(n_slabs):
        ch = slice(s * SSM_CH, (s + 1) * SSM_CH)
        for part in (0, N_STATE):
            cols = slice(part + s * SSM_SLAB, part + (s + 1) * SSM_SLAB)
            bu_ref[:, cols] = jnp.dot(ub[:, ch], bm_ref[ch, cols], preferred_element_type=F32)

    for c0 in range(0, N_STATE, SSM_COLS):
        re = slice(c0, c0 + SSM_COLS)
        im = slice(N_STATE + c0, N_STATE + c0 + SSM_COLS)
        ar = are_ref[:, re]
        ai = aim_ref[:, re]

        def step(t, carry, re=re, im=im, ar=ar, ai=ai):
            sr, si = carry
            rows = pl.ds(pl.multiple_of(t * SUBLANES, SUBLANES), SUBLANES)
            nr = ar * sr - ai * si + bu_ref[rows, re]
            ni = ar * si + ai * sr + bu_ref[rows, im]
            bu_ref[rows, re] = nr
            bu_ref[rows, im] = ni
            return nr, ni

        sr, si = lax.fori_loop(0, steps, step, (st_ref[:, re], st_ref[:, im]))
        st_ref[:, re] = sr
        st_ref[:, im] = si

    ys = []
    for s in range(n_slabs):
        ch = slice(s * SSM_CH, (s + 1) * SSM_CH)
        acc = None
        for part in (0, N_STATE):
            cols = slice(part + s * SSM_SLAB, part + (s + 1) * SSM_SLAB)
            d = jnp.dot(bu_ref[:, cols].astype(BF16), cm_ref[cols, ch], preferred_element_type=F32)
            acc = d if acc is None else acc + d
        ys.append(acc)
    y = jnp.concatenate(ys, axis=1) + dskip_ref[...] * u
    y = jax.nn.gelu(y)
    y = y * jax.nn.sigmoid(jnp.dot(y.astype(BF16), wglu_ref[...], preferred_element_type=F32))
    o_ref[...] = _rms(y, g_ref[...]).astype(BF16)


def _ssm(u_tm, bmat, a_re8, a_im8, cmat, d_skip, w_glu, g_out, batch):
    assert batch == SUBLANES, "one time step of all sequences must fill the sublanes"
    n = u_tm.shape[0]
    steps = min(SSM_STEPS, n // batch)
    rows = steps * batch
    return pl.pallas_call(
        functools.partial(_ssm_kernel, steps=steps),
        grid=(n // rows,),
        in_specs=[pl.BlockSpec((rows, SSM_WIDTH), lambda i: (i, 0)),
                  _full((SSM_WIDTH, 2 * N_STATE)), _full((SUBLANES, N_STATE)), _full((SUBLANES, N_STATE)),
                  _full((2 * N_STATE, SSM_WIDTH)), _full((1, SSM_WIDTH)), _full((SSM_WIDTH, SSM_WIDTH)),
                  _full((1, SSM_WIDTH))],
        out_specs=pl.BlockSpec((rows, SSM_WIDTH), lambda i: (i, 0)),
        out_shape=jax.ShapeDtypeStruct((n, SSM_WIDTH), BF16),
        scratch_shapes=[pltpu.VMEM((rows, 2 * N_STATE), F32), pltpu.VMEM((SUBLANES, 2 * N_STATE), F32)],
        compiler_params=_params(("arbitrary",)),
        name="ssm",
    )(u_tm, bmat, a_re8, a_im8, cmat, d_skip, w_glu, g_out)


def _attn_kernel(qi_ref, ki_ref, lam_ref, q_ref, k_ref, v_ref, g_ref, o_ref, qs_ref, m_ref, acc_ref, *,
                 tile, out_scale):
    qi = qi_ref[pl.program_id(2)]
    ki = ki_ref[pl.program_id(2)]

    @pl.when(ki == 0)
    def _():
        q = q_ref[0]
        lane = lax.broadcasted_iota(I32, q.shape, 1)
        zero = jnp.zeros_like(q)
        qs_ref[:tile] = jnp.where(lane < DA_QKDIM, q, zero)
        qs_ref[tile:] = jnp.where(lane >= DA_QKDIM, q, zero)
        m_ref[...] = jnp.full_like(m_ref, -jnp.inf)
        acc_ref[...] = jnp.zeros_like(acc_ref)

    def update(masked):
        v_ext = jnp.concatenate([v_ref[0], jnp.ones((tile, DA_VDIM), BF16)], axis=1)
        rb = min(ATT_ROWS, tile)
        n_rb = 2 * tile // rb

        def n_keys(r):
            return (r * rb) % tile + rb if masked else tile

        def scores(r):
            return lax.dot_general(qs_ref[r * rb:(r + 1) * rb], k_ref[0, :n_keys(r)], (((1,), (1,)), ((), ())),
                                   preferred_element_type=F32)

        s_next = scores(0)
        for r in range(n_rb):
            s = s_next
            if r + 1 < n_rb:
                s_next = scores(r + 1)
            rows = slice(r * rb, (r + 1) * rb)
            nk = n_keys(r)
            if masked:
                qpos = lax.broadcasted_iota(I32, s.shape, 0) + (r * rb) % tile
                s = jnp.where(lax.broadcasted_iota(I32, s.shape, 1) <= qpos, s, jnp.finfo(F32).min)
            m_old = m_ref[rows]
            m_new = jnp.maximum(m_old, jnp.max(s, axis=1, keepdims=True))
            p = jnp.exp(s - jnp.concatenate([m_new] * (nk // DA_VDIM), axis=1))
            alpha = jnp.exp(m_old - m_new)
            acc_ref[rows] = (jnp.concatenate([alpha, alpha], axis=1) * acc_ref[rows]
                             + jnp.dot(p.astype(BF16), v_ext[:nk], preferred_element_type=F32))
            m_ref[rows] = m_new

    @pl.when(ki < qi)
    def _():
        update(False)

    @pl.when(ki == qi)
    def _():
        update(True)
        o = acc_ref[:, :DA_VDIM] / acc_ref[:, DA_VDIM:]
        o = o[:tile] - lam_ref[0] * o[tile:]
        o_ref[0] = (_rms(o, g_ref[...]) * out_scale).astype(BF16)


def _diff_attention(q, k, v, lam, g_sub, lambda_init):
    b, l, _ = q.shape
    t = min(ATT_TILE, l)
    n = l // t
    pairs = [(qi, ki) for qi in range(n) for ki in range(qi + 1)]
    qi_tab = jnp.asarray([p[0] for p in pairs], I32)
    ki_tab = jnp.asarray([p[1] for p in pairs], I32)
    qspec = pl.BlockSpec((1, t, DA_VDIM), lambda bi, h, s, qt, kt: (bi, qt[s], h))
    kspec = pl.BlockSpec((1, t, DA_VDIM), lambda bi, h, s, qt, kt: (bi, kt[s], h))
    return pl.pallas_call(
        functools.partial(_attn_kernel, tile=t, out_scale=1.0 - lambda_init),
        grid_spec=pltpu.PrefetchScalarGridSpec(
            num_scalar_prefetch=2,
            grid=(b, DA_HEADS, len(pairs)),
            in_specs=[pl.BlockSpec(memory_space=pltpu.SMEM), qspec, kspec, kspec,
                      pl.BlockSpec((1, DA_VDIM), lambda bi, h, s, qt, kt: (0, 0))],
            out_specs=qspec,
            scratch_shapes=[pltpu.VMEM((2 * t, DA_VDIM), BF16), pltpu.VMEM((2 * t, DA_VDIM), F32),
                            pltpu.VMEM((2 * t, 2 * DA_VDIM), F32)]),
        out_shape=jax.ShapeDtypeStruct((b, l, ATTN_WIDTH), BF16),
        compiler_params=_params(("parallel", "parallel", "arbitrary")),
        name="diffattn",
    )(qi_tab, ki_tab, lam, q, k, v, g_sub)


def _memkv_kernel(m_ref, g_ref, wk_ref, wv_ref, k_ref, v_ref):
    a = _rms(m_ref[0], g_ref[...]).astype(BF16)
    k_ref[0] = jnp.dot(a, wk_ref[...], preferred_element_type=F32).astype(BF16)
    v_ref[0] = jnp.dot(a, wv_ref[...], preferred_element_type=F32).astype(BF16)


def _memkv(mem, g_mem, wk, wv):
    b, m, d = mem.shape
    blk = pl.BlockSpec((1, m, d), lambda i: (i, 0, 0))
    return pl.pallas_call(
        _memkv_kernel,
        grid=(b,),
        in_specs=[blk, _full((1, d)), _full((d, d)), _full((d, d))],
        out_specs=[blk, blk],
        out_shape=[jax.ShapeDtypeStruct((b, m, d), BF16)] * 2,
        compiler_params=_params(("parallel",)),
        name="memkv",
    )(mem, g_mem, wk, wv)


def _mix_kernel(x_ref, ys_ref, ya_ref, wo1_ref, wo2_ref, gx_ref, wq_ref, km_ref, vm_ref, wo_ref, h_ref):
    h = (x_ref[0]
         + jnp.dot(ys_ref[0], wo1_ref[...], preferred_element_type=F32)
         + jnp.dot(ya_ref[0], wo2_ref[...], preferred_element_type=F32))
    hq = _rms(h, gx_ref[...]).astype(BF16)
    q = jnp.dot(hq, wq_ref[...], preferred_element_type=F32).astype(BF16)
    outs = []
    for hd in range(X_HEADS):
        cols = slice(hd * X_HEAD_DIM, (hd + 1) * X_HEAD_DIM)
        s = lax.dot_general(q[:, cols], km_ref[0, :, cols], (((1,), (1,)), ((), ())),
                            preferred_element_type=F32) * (X_HEAD_DIM ** -0.5)
        s = s - jnp.max(s, axis=1, keepdims=True)
        p = jnp.exp(s)
        p = p / jnp.sum(p, axis=1, keepdims=True)
        outs.append(jnp.dot(p.astype(BF16), vm_ref[0, :, cols], preferred_element_type=F32).astype(BF16))
    o = jnp.concatenate(outs, axis=1)
    h_ref[0] = h + jnp.dot(o, wo_ref[...], preferred_element_type=F32)


def _mix(x, y_ssm, y_att, wo1, wo2, g_x, wq, kmem, vmem, wo):
    b, l, d = x.shape
    t = min(MIX_TILE, l)
    m = kmem.shape[1]
    row = lambda w: pl.BlockSpec((1, t, w), lambda i, j: (i, j, 0))
    mem = pl.BlockSpec((1, m, d), lambda i, j: (i, 0, 0))
    return pl.pallas_call(
        _mix_kernel,
        grid=(b, l // t),
        in_specs=[row(d), row(SSM_WIDTH), row(ATTN_WIDTH), _full((SSM_WIDTH, d)), _full((ATTN_WIDTH, d)),
                  _full((1, d)), _full((d, d)), mem, mem, _full((d, d))],
        out_specs=row(d),
        out_shape=jax.ShapeDtypeStruct((b, l, d), F32),
        compiler_params=_params(("parallel", "parallel")),
        name="mix",
    )(x, y_ssm, y_att, wo1, wo2, g_x, wq, kmem, vmem, wo)


def _first_index(hit, idx, sentinel):
    return jnp.min(jnp.where(hit, idx, sentinel), axis=0, keepdims=True)


def _route_kernel(h_ref, g_ref, wr_ref, bias_ref, wsgu_ref, wsd_ref, tri_ref,
                  t_ref, base_ref, eidx_ref, gw_ref, rank_ref, cnt_ref, carry_ref):
    @pl.when(pl.program_id(0) == 0)
    def _():
        carry_ref[...] = jnp.zeros_like(carry_ref)

    h = h_ref[...]
    t = _rms(h, g_ref[...])
    tb = t.astype(BF16)
    t_ref[...] = _pack_rows(t)
    gu = jnp.dot(tb, wsgu_ref[...], preferred_element_type=F32)
    hid = jax.nn.silu(gu[:, :D_EXPERT]) * gu[:, D_EXPERT:]
    base_ref[...] = h + jnp.dot(hid.astype(BF16), wsd_ref[...], preferred_element_type=F32)

    logits = lax.dot_general(wr_ref[...], t, (((1,), (1,)), ((), ())), preferred_element_type=F32,
                             precision=lax.Precision.HIGHEST)
    scores = jax.nn.sigmoid(logits)
    biased = scores + bias_ref[...]
    n_tok = scores.shape[1]
    neg = -jnp.inf
    sub = lax.broadcasted_iota(I32, (PER_GROUP, n_tok), 0)

    gs = []
    for g in range(N_EXPERT_GROUPS):
        blk = biased[g * PER_GROUP:(g + 1) * PER_GROUP]
        m1 = jnp.max(blk, axis=0, keepdims=True)
        i1 = _first_index(blk == m1, sub, PER_GROUP)
        m2 = jnp.max(jnp.where(sub == i1, neg, blk), axis=0, keepdims=True)
        gs.append(m1 + m2)
    gs = jnp.concatenate(gs, axis=0)

    gsel = jnp.zeros(gs.shape, jnp.bool_)
    for _ in range(TOPK_GROUPS):
        m = jnp.max(gs, axis=0, keepdims=True)
        hit = sub == _first_index(gs == m, sub, N_EXPERT_GROUPS)
        gsel = jnp.logical_or(gsel, hit)
        gs = jnp.where(hit, neg, gs)

    masked = jnp.concatenate(
        [jnp.where(gsel[g:g + 1], biased[g * PER_GROUP:(g + 1) * PER_GROUP], neg)
         for g in range(N_EXPERT_GROUPS)], axis=0)
    eid = lax.broadcasted_iota(I32, masked.shape, 0)
    sel = jnp.zeros(masked.shape, jnp.bool_)
    idxs, gws = [], []
    for _ in range(TOP_K):
        m = jnp.max(masked, axis=0, keepdims=True)
        i = _first_index(masked == m, eid, N_EXPERTS)
        hit = eid == i
        idxs.append(i)
        gws.append(jnp.sum(jnp.where(hit, scores, 0.0), axis=0, keepdims=True))
        sel = jnp.logical_or(sel, hit)
        masked = jnp.where(hit, neg, masked)
    eidx = jnp.concatenate(idxs, axis=0)
    gw = jnp.concatenate(gws, axis=0)
    gw = gw / jnp.sum(gw, axis=0, keepdims=True) * ROUTED_SCALE
    eidx_ref[...] = eidx
    gw_ref[...] = gw

    before = jnp.dot(sel.astype(BF16), tri_ref[...], preferred_element_type=F32) + carry_ref[...]
    rank_ref[...] = jnp.concatenate(
        [jnp.sum(jnp.where(eid == idxs[k], before, 0.0), axis=0, keepdims=True) for k in range(TOP_K)],
        axis=0).astype(I32)
    carry = carry_ref[...] + jnp.sum(sel.astype(F32), axis=1, keepdims=True)
    carry_ref[...] = carry
    cnt_ref[...] = carry.astype(I32)


def _route(h2, g_ffn, wr_t, bias_col, wsgu, wsd, part, n_parts):
    n_all, d = h2.shape
    n = n_all // n_parts
    t = min(ROUTE_TILE, n)
    blk0 = part * (n // t)
    tri = (lax.broadcasted_iota(I32, (t, t), 0) < lax.broadcasted_iota(I32, (t, t), 1)).astype(BF16)
    row = lambda w: pl.BlockSpec((t, w), lambda i: (i, 0))
    col = pl.BlockSpec((TOP_K, t), lambda i: (0, i))
    return pl.pallas_call(
        _route_kernel,
        grid=(n // t,),
        in_specs=[pl.BlockSpec((t, d), lambda i: (i + blk0, 0)), _full((1, d)), _full((N_EXPERTS, d)),
                  _full((N_EXPERTS, 1)),
                  _full((d, 2 * D_EXPERT)), _full((D_EXPERT, d)), _full((t, t))],
        out_specs=[row(d // 2), row(d), col, col, col, _full((N_EXPERTS, 1))],
        out_shape=[jax.ShapeDtypeStruct((n, d // 2), I32), jax.ShapeDtypeStruct((n, d), F32),
                   jax.ShapeDtypeStruct((TOP_K, n), I32), jax.ShapeDtypeStruct((TOP_K, n), F32),
                   jax.ShapeDtypeStruct((TOP_K, n), I32), jax.ShapeDtypeStruct((N_EXPERTS, 1), I32)],
        scratch_shapes=[pltpu.VMEM((N_EXPERTS, 1), F32)],
        compiler_params=_params(("arbitrary",)),
        name="route",
    )(h2, g_ffn, wr_t, bias_col, wsgu, wsd, tri)


def _expert_kernel(be_ref, nu_ref, x_ref, wg_ref, wu_ref, wd_ref, y_ref, wg_s, wu_s, wd_s):
    i = pl.program_id(0)

    @pl.when(jnp.logical_or(i == 0, be_ref[i] != be_ref[jnp.maximum(i - 1, 0)]))
    def _():
        wg_s[...] = wg_ref[0].astype(BF16)
        wu_s[...] = wu_ref[0].astype(BF16)
        wd_s[...] = wd_ref[0].astype(BF16)

    @pl.when(i < nu_ref[0])
    def _():
        x = _unpack_rows(x_ref[...]).astype(BF16)
        gate = jnp.dot(x, wg_s[...], preferred_element_type=F32)
        up = jnp.dot(x, wu_s[...], preferred_element_type=F32)
        hid = (jax.nn.silu(gate) * up).astype(BF16)
        y_ref[...] = _pack_rows(jnp.dot(hid, wd_s[...], preferred_element_type=F32))


def _experts(block_e, n_used, xs, w_gate, w_up, w_down):
    n_slots, dw = xs.shape
    _, d, de = w_gate.shape
    nb = n_slots // MOE_TILE
    rows = pl.BlockSpec((MOE_TILE, dw), lambda i, be, nu: (jnp.minimum(i, nu[0] - 1), 0))
    w_in = pl.BlockSpec((1, d, de), lambda i, be, nu: (be[i], 0, 0))
    return pl.pallas_call(
        _expert_kernel,
        grid_spec=pltpu.PrefetchScalarGridSpec(
            num_scalar_prefetch=2,
            grid=(nb,),
            in_specs=[rows, w_in, w_in, pl.BlockSpec((1, de, d), lambda i, be, nu: (be[i], 0, 0))],
            out_specs=rows,
            scratch_shapes=[pltpu.VMEM((d, de), BF16), pltpu.VMEM((d, de), BF16), pltpu.VMEM((de, d), BF16)]),
        out_shape=jax.ShapeDtypeStruct((n_slots, dw), I32),
        compiler_params=_params(("arbitrary",)),
        name="experts",
    )(block_e, n_used, xs, w_gate, w_up, w_down)


def _sc_worker_id():
    return lax.axis_index("s") * SC_CORES + lax.axis_index("c")


def _sc_dispatch(t_rows, dest3, n_slots):
    _, dw = t_rows.shape
    n_chunks, _, w = dest3.shape
    per_worker = n_chunks // SC_WORKERS
    assert per_worker % 2 == 0
    mesh = plsc.VectorSubcoreMesh(core_axis_name="c", subcore_axis_name="s")
    dt = t_rows.dtype

    @functools.partial(
        pl.kernel, mesh=mesh,
        out_type=jax.ShapeDtypeStruct((n_slots, dw), dt),
        scratch_types=[pltpu.VMEM((TOP_K, w), I32), pltpu.VMEM((TOP_K, w), I32),
                       pltpu.VMEM((w, dw), dt), pltpu.VMEM((w, dw), dt),
                       pltpu.SemaphoreType.DMA, pltpu.SemaphoreType.DMA, pltpu.SemaphoreType.DMA],
    )
    def scatter_rows(t_hbm, dest_hbm, xs_hbm, idx_a, idx_b, rows_a, rows_b, sem_load, sem_a, sem_b):
        wid = _sc_worker_id()

        def scatter(idx_v, rows_v, sem):
            return [pltpu.async_copy(rows_v, xs_hbm.at[idx_v.at[k]], sem) for k in range(TOP_K)]

        @pl.loop(0, per_worker, step=2)
        def _(j):
            ca = wid * per_worker + j
            cb = ca + 1
            pltpu.sync_copy(dest_hbm.at[ca], idx_a)
            pltpu.sync_copy(t_hbm.at[pl.ds(ca * w, w)], rows_a)
            load_idx = pltpu.async_copy(dest_hbm.at[cb], idx_b, sem_load)
            load_rows = pltpu.async_copy(t_hbm.at[pl.ds(cb * w, w)], rows_b, sem_load)
            out_a = scatter(idx_a, rows_a, sem_a)
            load_idx.wait()
            load_rows.wait()
            out_b = scatter(idx_b, rows_b, sem_b)
            for cp in out_a + out_b:
                cp.wait()

    return scatter_rows(t_rows, dest3)


def _sc_combine(y_rows, dest3):
    _, dw = y_rows.shape
    n_chunks, _, w = dest3.shape
    per_worker = n_chunks // SC_WORKERS
    mesh = plsc.VectorSubcoreMesh(core_axis_name="c", subcore_axis_name="s")
    dt = y_rows.dtype

    @functools.partial(
        pl.kernel, mesh=mesh,
        out_type=jax.ShapeDtypeStruct((TOP_K, n_chunks * w, dw), dt),
        scratch_types=[pltpu.VMEM((TOP_K, w), I32), pltpu.VMEM((w, dw), dt), pltpu.VMEM((w, dw), dt),
                       pltpu.SemaphoreType.DMA, pltpu.SemaphoreType.DMA,
                       pltpu.SemaphoreType.DMA, pltpu.SemaphoreType.DMA],
    )
    def gather_rows(y_hbm, dest_hbm, out_hbm, idx_v, buf0, buf1, gsem0, gsem1, wsem0, wsem1):
        wid = _sc_worker_id()
        bufs, gsems, wsems = (buf0, buf1), (gsem0, gsem1), (wsem0, wsem1)

        @pl.loop(0, per_worker)
        def _(j):
            c = wid * per_worker + j
            pltpu.sync_copy(dest_hbm.at[c], idx_v)
            gathers = [None] * TOP_K
            writes = [None] * TOP_K
            gathers[0] = pltpu.async_copy(y_hbm.at[idx_v.at[0]], bufs[0], gsems[0])
            for k in range(TOP_K):
                b = k % 2
                if k + 1 < TOP_K:
                    if k >= 1:
                        writes[k - 1].wait()
                    gathers[k + 1] = pltpu.async_copy(y_hbm.at[idx_v.at[k + 1]], bufs[1 - b], gsems[1 - b])
                gathers[k].wait()
                writes[k] = pltpu.async_copy(bufs[b], out_hbm.at[k, pl.ds(c * w, w)], wsems[b])
            writes[TOP_K - 2].wait()
            writes[TOP_K - 1].wait()

    return gather_rows(y_rows, dest3)


def _final_kernel(base_ref, y_ref, gw_ref, g_ref, *rest, normalize):
    o_ref = rest[-1]
    h = base_ref[...]
    gw = gw_ref[...]
    for k in range(TOP_K):
        h = h + gw[:, k:k + 1] * _unpack_rows(y_ref[k])
    o_ref[...] = _rms(h, g_ref[...]) if normalize else h


def _final(base, yg, gw_rows, g_final, normalize, out_prev, part, n_parts):
    n, d = base.shape
    t = min(FIN_TILE, n)
    blk0 = part * (n // t)
    in_specs = [pl.BlockSpec((t, d), lambda i: (i, 0)), pl.BlockSpec((TOP_K, t, d // 2), lambda i: (0, i, 0)),
                pl.BlockSpec((t, TOP_K), lambda i: (i, 0)), _full((1, d))]
    args = [base, yg, gw_rows, g_final]
    aliases = {}
    if out_prev is not None:
        in_specs.append(pl.BlockSpec(memory_space=pl.ANY))
        args.append(out_prev)
        aliases = {len(args) - 1: 0}
    return pl.pallas_call(
        functools.partial(_final_kernel, normalize=normalize),
        grid=(n // t,),
        in_specs=in_specs,
        out_specs=pl.BlockSpec((t, d), lambda i: (i + blk0, 0)),
        out_shape=jax.ShapeDtypeStruct((n * n_parts, d), F32),
        input_output_aliases=aliases,
        compiler_params=_params(("parallel",)),
        name="final",
    )(*args)


def _ssm_matrices(a_re, a_im, log_dt, b_re, b_im, c_re, c_im):
    lr, li = a_re.astype(F32), a_im.astype(F32)
    dt = jnp.exp(log_dt.astype(F32))[:, None]
    mag = jnp.exp(lr * dt)
    ab_re, ab_im = mag * jnp.cos(li * dt), mag * jnp.sin(li * dt)
    den = lr * lr + li * li
    zr, zi = ab_re - 1.0, ab_im
    k_re = (zr * lr + zi * li) / den
    k_im = (zi * lr - zr * li) / den
    br, bi = b_re.astype(F32), b_im.astype(F32)
    bb_re = k_re[..., None] * br - k_im[..., None] * bi
    bb_im = k_re[..., None] * bi + k_im[..., None] * br
    eye = jnp.eye(SSM_GROUPS, dtype=F32)

    def in_mat(bb):
        return jnp.einsum("gpc,gh->gchp", bb, eye).reshape(SSM_WIDTH, N_STATE)

    def out_mat(c):
        return jnp.einsum("gcp,gh->gphc", c.astype(F32), eye).reshape(N_STATE, SSM_WIDTH)

    bmat = jnp.concatenate([in_mat(bb_re), in_mat(bb_im)], axis=1)
    cmat = jnp.concatenate([out_mat(c_re), -out_mat(c_im)], axis=0)
    tile = lambda a: jnp.broadcast_to(a.reshape(1, N_STATE), (SUBLANES, N_STATE))
    return bmat.astype(BF16), tile(ab_re), tile(ab_im), cmat.astype(BF16)


def _plan_kernel(pstart_ref, eidx_ref, rank_ref, dest_ref, *, window):
    eidx = eidx_ref[...]
    dest = rank_ref[...]
    for e in range(N_EXPERTS):
        dest = dest + jnp.where(eidx == e, pstart_ref[e], 0)
    for c in range(dest.shape[1] // window):
        dest_ref[c] = dest[:, c * window:(c + 1) * window]


def _plan(pstart, eidx, rank, window):
    k, n = eidx.shape
    t = min(PLAN_TILE, n)
    cols = pl.BlockSpec((k, t), lambda i, ps: (0, i))
    return pl.pallas_call(
        functools.partial(_plan_kernel, window=window),
        grid_spec=pltpu.PrefetchScalarGridSpec(
            num_scalar_prefetch=1,
            grid=(n // t,),
            in_specs=[cols, cols],
            out_specs=pl.BlockSpec((t // window, k, window), lambda i, ps: (i, 0, 0))),
        out_shape=jax.ShapeDtypeStruct((n // window, k, window), I32),
        compiler_params=_params(("parallel",)),
        name="plan",
    )(pstart, eidx, rank)


def _block_schedule(counts, n_tok):
    padded = ((counts + MOE_TILE - 1) // MOE_TILE) * MOE_TILE
    pend = jnp.cumsum(padded)
    pstart = (pend - padded).astype(I32)
    n_slots = n_tok * TOP_K + N_EXPERTS * MOE_TILE
    nb = n_slots // MOE_TILE
    n_used = (pend[-1] // MOE_TILE).astype(I32)
    blk = jnp.minimum(jnp.arange(nb, dtype=I32), n_used - 1)
    block_e = jnp.sum((pend[None, :] <= (blk * MOE_TILE)[:, None]).astype(I32), axis=1)
    return pstart, jnp.minimum(block_e, N_EXPERTS - 1), n_used.reshape(1), n_slots


def kernel(x, mem, positions, g_mix, w_in, a_re, a_im, log_dt, b_re, b_im, c_re, c_im, d_skip, w_glu, g_ssm_out, lam_q1, lam_k1, lam_q2, lam_k2, g_sub, w_out, g_x, g_mem, wq_x, wk_x, wv_x, wo_x, g_ffn, w_router, router_bias, w_gate, w_up, w_down, ws_gate, ws_up, ws_down, g_final):
    b, l, d = x.shape
    n = b * l
    depth = w_in.shape[0]
    row = lambda a: a.reshape(1, -1).astype(F32)
    inv = ROPE_THETA ** (-jnp.arange(0, DA_QKDIM, 2, dtype=F32) / DA_QKDIM)
    inv_row = jnp.tile(inv, 128 // inv.shape[0]).reshape(1, 128)
    pos3 = positions.reshape(b, l, 1)

    h = x
    for i in range(depth):
        lambda_init = 0.8 - 0.6 * math.exp(-0.3 * i)
        u, q, k, v = _inproj(h, pos3, row(g_mix[i]), w_in[i].astype(BF16), inv_row)

        bmat, are8, aim8, cmat = _ssm_matrices(a_re[i], a_im[i], log_dt[i], b_re[i], b_im[i], c_re[i], c_im[i])
        u_tm = u.transpose(1, 0, 2).reshape(n, SSM_WIDTH)
        y_ssm = _ssm(u_tm, bmat, are8, aim8, cmat, row(d_skip[i]), w_glu[i].astype(BF16), row(g_ssm_out[i]), b)
        y_ssm = y_ssm.reshape(l, b, SSM_WIDTH).transpose(1, 0, 2)

        lam = (jnp.exp(jnp.sum(lam_q1[i].astype(F32) * lam_k1[i].astype(F32)))
               - jnp.exp(jnp.sum(lam_q2[i].astype(F32) * lam_k2[i].astype(F32))) + lambda_init).reshape(1)
        y_att = _diff_attention(q, k, v, lam.astype(F32), row(g_sub[i]), lambda_init)

        kmem, vmem = _memkv(mem, row(g_mem[i]), wk_x[i].astype(BF16), wv_x[i].astype(BF16))
        wo = w_out[i].astype(BF16)
        h2 = _mix(h, y_ssm, y_att, wo[:SSM_WIDTH], wo[SSM_WIDTH:], row(g_x[i]), wq_x[i].astype(BF16),
                  kmem, vmem, wo_x[i].astype(BF16))

        wsgu = jnp.concatenate([ws_gate[i], ws_up[i]], axis=1).astype(BF16)
        n_part = n // MOE_PARTS
        out = None
        for part in range(MOE_PARTS):
            t_pk, base, eidx, gw, rank, counts = _route(
                h2.reshape(n, d), row(g_ffn[i]), w_router[i].T.astype(F32),
                router_bias[i].reshape(N_EXPERTS, 1).astype(F32), wsgu, ws_down[i].astype(BF16), part, MOE_PARTS)
            pstart, block_e, n_used, n_slots = _block_schedule(counts[:, 0], n_part)
            dest3 = _plan(pstart, eidx, rank, SC_WINDOW)
            xs = _sc_dispatch(t_pk, dest3, n_slots)
            ys = _experts(block_e, n_used, xs, w_gate[i], w_up[i], w_down[i])
            yg = _sc_combine(ys, dest3)
            out = _final(base, yg, gw.T, row(g_final), i == depth - 1, out, part, MOE_PARTS)
        h = out.reshape(b, l, d)
    return h
```

```python
import functools
import math

import jax
import jax.numpy as jnp
from jax import lax
from jax.experimental import pallas as pl
from jax.experimental.pallas import tpu as pltpu
from jax.experimental.pallas import tpu_sc as plsc

F32 = jnp.float32
BF16 = jnp.bfloat16
I32 = jnp.int32

D_MODEL = 1024
SSM_WIDTH = 512
ATTN_WIDTH = 512
SSM_GROUP = 16
SSM_GROUPS = 32
SSM_STATE = 64
N_STATE = SSM_GROUPS * SSM_STATE
DA_HEADS = 4
DA_VDIM = 128
DA_QKDIM = 64
ROPE_THETA = 10000.0
X_HEADS = 4
X_HEAD_DIM = 256
N_EXPERTS = 64
TOP_K = 8
N_EXPERT_GROUPS = 8
PER_GROUP = N_EXPERTS // N_EXPERT_GROUPS
TOPK_GROUPS = 4
D_EXPERT = 256
ROUTED_SCALE = 2.5
EPS = 1e-6

VMEM_LIMIT_V7X = 56 * 1024 * 1024
SUBLANES = 8

IN_TILE = 512
SSM_STEPS = 64
SSM_COLS = 1024
SSM_CH = 128
SSM_SLAB = 512
ATT_TILE = 1024
ATT_ROWS = 256
MIX_TILE = 512
ROUTE_TILE = 512
MOE_TILE = 512
X_SLOTS = 3
PLAN_TILE = 2048
MOE_PARTS = 2
FIN_TILE = 256

SC_CORES = 2
SC_WORKERS = 32
SC_WINDOW = 64


def _params(sem):
    return pltpu.CompilerParams(dimension_semantics=sem, vmem_limit_bytes=VMEM_LIMIT_V7X)


def _rms(x, g):
    return x * lax.rsqrt(jnp.mean(x * x, axis=-1, keepdims=True) + EPS) * g


def _full(shape):
    return pl.BlockSpec(shape, lambda *_: (0,) * len(shape))


def _pack_rows(a):
    w = a.shape[1] // 2
    bits = lambda v: lax.bitcast_convert_type(v.astype(BF16).astype(F32), I32)
    return (bits(a[:, w:]) & jnp.int32(-65536)) | lax.shift_right_logical(bits(a[:, :w]), 16)


def _unpack_rows(p):
    lo = lax.bitcast_convert_type(lax.shift_left(p, 16), F32)
    hi = lax.bitcast_convert_type(p & jnp.int32(-65536), F32)
    return jnp.concatenate([lo, hi], axis=1)


def _inproj_kernel(x_ref, pos_ref, g_ref, w_ref, inv_ref, u_ref, q_ref, k_ref, v_ref):
    x = x_ref[0]
    a = _rms(x, g_ref[...]).astype(BF16)
    z = jnp.dot(a, w_ref[...], preferred_element_type=F32)
    u_ref[0] = z[:, :SSM_WIDTH]
    ang = pos_ref[0].astype(F32) * inv_ref[...]
    cos = jnp.concatenate([jnp.cos(ang)] * 4, axis=1)
    sin = jnp.concatenate([jnp.sin(ang)] * 4, axis=1)
    lane = lax.broadcasted_iota(I32, cos.shape, 1)
    first = (lane & (DA_QKDIM - 1)) < DA_QKDIM // 2
    half = DA_QKDIM // 2

    def rope(t):
        rot = jnp.where(first, -pltpu.roll(t, ATTN_WIDTH - half, 1), pltpu.roll(t, half, 1))
        return t * cos + rot * sin

    q = z[:, SSM_WIDTH:SSM_WIDTH + ATTN_WIDTH]
    k = z[:, SSM_WIDTH + ATTN_WIDTH:SSM_WIDTH + 2 * ATTN_WIDTH]
    q_ref[0] = (rope(q) * (DA_QKDIM ** -0.5)).astype(BF16)
    k_ref[0] = rope(k).astype(BF16)
    v_ref[0] = z[:, SSM_WIDTH + 2 * ATTN_WIDTH:].astype(BF16)


def _inproj(x, pos3, g_mix, w_in, inv_row):
    b, l, d = x.shape
    t = min(IN_TILE, l)
    n_out = w_in.shape[1]
    row = lambda w: pl.BlockSpec((1, t, w), lambda i, j: (i, j, 0))
    return pl.pallas_call(
        _inproj_kernel,
        grid=(b, l // t),
        in_specs=[row(d), row(1), _full((1, d)), _full((d, n_out)), _full((1, 128))],
        out_specs=[row(SSM_WIDTH), row(ATTN_WIDTH), row(ATTN_WIDTH), row(ATTN_WIDTH)],
        out_shape=[jax.ShapeDtypeStruct((b, l, SSM_WIDTH), F32)]
        + [jax.ShapeDtypeStruct((b, l, ATTN_WIDTH), BF16)] * 3,
        compiler_params=_params(("parallel", "parallel")),
        name="inproj",
    )(x, pos3, g_mix, w_in, inv_row)


def _ssm_kernel(u_ref, bm_ref, are_ref, aim_ref, cm_ref, dskip_ref, wglu_ref, g_ref, o_ref, bu_ref, st_ref,
                *, steps):
    @pl.when(pl.program_id(0) == 0)
    def _():
        st_ref[...] = jnp.zeros_like(st_ref)

    u = u_ref[...]
    ub = u.astype(BF16)

    n_slabs = SSM_WIDTH // SSM_CH
    for s in range(n_slabs):
        ch = slice(s * SSM_CH, (s + 1) * SSM_CH)
        for part in (0, N_STATE):
            cols = slice(part + s * SSM_SLAB, part + (s + 1) * SSM_SLAB)
            bu_ref[:, cols] = jnp.dot(ub[:, ch], bm_ref[ch, cols], preferred_element_type=F32)

    for c0 in range(0, N_STATE, SSM_COLS):
        re = slice(c0, c0 + SSM_COLS)
        im = slice(N_STATE + c0, N_STATE + c0 + SSM_COLS)
        ar = are_ref[:, re]
        ai = aim_ref[:, re]

        def step(t, carry, re=re, im=im, ar=ar, ai=ai):
            sr, si = carry
            rows = pl.ds(pl.multiple_of(t * SUBLANES, SUBLANES), SUBLANES)
            nr = ar * sr - ai * si + bu_ref[rows, re]
            ni = ar * si + ai * sr + bu_ref[rows, im]
            bu_ref[rows, re] = nr
            bu_ref[rows, im] = ni
            return nr, ni

        sr, si = lax.fori_loop(0, steps, step, (st_ref[:, re], st_ref[:, im]))
        st_ref[:, re] = sr
        st_ref[:, im] = si

    ys = []
    for s in range(n_slabs):
        ch = slice(s * SSM_CH, (s + 1) * SSM_CH)
        acc = None
        for part in (0, N_STATE):
            cols = slice(part + s * SSM_SLAB, part + (s + 1) * SSM_SLAB)
            d = jnp.dot(bu_ref[:, cols].astype(BF16), cm_ref[cols, ch], preferred_element_type=F32)
            acc = d if acc is None else acc + d
        ys.append(acc)
    y = jnp.concatenate(ys, axis=1) + dskip_ref[...] * u
    y = jax.nn.gelu(y)
    y = y * jax.nn.sigmoid(jnp.dot(y.astype(BF16), wglu_ref[...], preferred_element_type=F32))
    o_ref[...] = _rms(y, g_ref[...]).astype(BF16)


def _ssm(u_tm, bmat, a_re8, a_im8, cmat, d_skip, w_glu, g_out, batch):
    assert batch == SUBLANES, "one time step of all sequences must fill the sublanes"
    n = u_tm.shape[0]
    steps = min(SSM_STEPS, n // batch)
    rows = steps * batch
    return pl.pallas_call(
        functools.partial(_ssm_kernel, steps=steps),
        grid=(n // rows,),
        in_specs=[pl.BlockSpec((rows, SSM_WIDTH), lambda i: (i, 0)),
                  _full((SSM_WIDTH, 2 * N_STATE)), _full((SUBLANES, N_STATE)), _full((SUBLANES, N_STATE)),
                  _full((2 * N_STATE, SSM_WIDTH)), _full((1, SSM_WIDTH)), _full((SSM_WIDTH, SSM_WIDTH)),
                  _full((1, SSM_WIDTH))],
        out_specs=pl.BlockSpec((rows, SSM_WIDTH), lambda i: (i, 0)),
        out_shape=jax.ShapeDtypeStruct((n, SSM_WIDTH), BF16),
        scratch_shapes=[pltpu.VMEM((rows, 2 * N_STATE), F32), pltpu.VMEM((SUBLANES, 2 * N_STATE), F32)],
        compiler_params=_params(("arbitrary",)),
        name="ssm",
    )(u_tm, bmat, a_re8, a_im8, cmat, d_skip, w_glu, g_out)


def _attn_kernel(qi_ref, ki_ref, lam_ref, q_ref, k_ref, v_ref, g_ref, o_ref, qs_ref, m_ref, acc_ref, *,
                 tile, out_scale):
    qi = qi_ref[pl.program_id(2)]
    ki = ki_ref[pl.program_id(2)]

    @pl.when(ki == 0)
    def _():
        q = q_ref[0]
        lane = lax.broadcasted_iota(I32, q.shape, 1)
        zero = jnp.zeros_like(q)
        qs_ref[:tile] = jnp.where(lane < DA_QKDIM, q, zero)
        qs_ref[tile:] = jnp.where(lane >= DA_QKDIM, q, zero)
        m_ref[...] = jnp.full_like(m_ref, -jnp.inf)
        acc_ref[...] = jnp.zeros_like(acc_ref)

    def update(masked):
        v_ext = jnp.concatenate([v_ref[0], jnp.ones((tile, DA_VDIM), BF16)], axis=1)
        rb = min(ATT_ROWS, tile)
        n_rb = 2 * tile // rb

        def n_keys(r):
            return (r * rb) % tile + rb if masked else tile

        def scores(r):
            return lax.dot_general(qs_ref[r * rb:(r + 1) * rb], k_ref[0, :n_keys(r)], (((1,), (1,)), ((), ())),
                                   preferred_element_type=F32)

        s_next = scores(0)
        for r in range(n_rb):
            s = s_next
            if r + 1 < n_rb:
                s_next = scores(r + 1)
            rows = slice(r * rb, (r + 1) * rb)
            nk = n_keys(r)
            if masked:
                qpos = lax.broadcasted_iota(I32, s.shape, 0) + (r * rb) % tile
                s = jnp.where(lax.broadcasted_iota(I32, s.shape, 1) <= qpos, s, jnp.finfo(F32).min)
            m_old = m_ref[rows]
            m_new = jnp.maximum(m_old, jnp.max(s, axis=1, keepdims=True))
            p = jnp.exp(s - jnp.concatenate([m_new] * (nk // DA_VDIM), axis=1))
            alpha = jnp.exp(m_old - m_new)
            acc_ref[rows] = (jnp.concatenate([alpha, alpha], axis=1) * acc_ref[rows]
                             + jnp.dot(p.astype(BF16), v_ext[:nk], preferred_element_type=F32))
            m_ref[rows] = m_new

    @pl.when(ki < qi)
    def _():
        update(False)

    @pl.when(ki == qi)
    def _():
        update(True)
        o = acc_ref[:, :DA_VDIM] / acc_ref[:, DA_VDIM:]
        o = o[:tile] - lam_ref[0] * o[tile:]
        o_ref[0] = (_rms(o, g_ref[...]) * out_scale).astype(BF16)


def _diff_attention(q, k, v, lam, g_sub, lambda_init):
    b, l, _ = q.shape
    t = min(ATT_TILE, l)
    n = l // t
    pairs = [(qi, ki) for qi in range(n) for ki in range(qi + 1)]
    qi_tab = jnp.asarray([p[0] for p in pairs], I32)
    ki_tab = jnp.asarray([p[1] for p in pairs], I32)
    qspec = pl.BlockSpec((1, t, DA_VDIM), lambda bi, h, s, qt, kt: (bi, qt[s], h))
    kspec = pl.BlockSpec((1, t, DA_VDIM), lambda bi, h, s, qt, kt: (bi, kt[s], h))
    return pl.pallas_call(
        functools.partial(_attn_kernel, tile=t, out_scale=1.0 - lambda_init),
        grid_spec=pltpu.PrefetchScalarGridSpec(
            num_scalar_prefetch=2,
            grid=(b, DA_HEADS, len(pairs)),
            in_specs=[pl.BlockSpec(memory_space=pltpu.SMEM), qspec, kspec, kspec,
                      pl.BlockSpec((1, DA_VDIM), lambda bi, h, s, qt, kt: (0, 0))],
            out_specs=qspec,
            scratch_shapes=[pltpu.VMEM((2 * t, DA_VDIM), BF16), pltpu.VMEM((2 * t, DA_VDIM), F32),
                            pltpu.VMEM((2 * t, 2 * DA_VDIM), F32)]),
        out_shape=jax.ShapeDtypeStruct((b, l, ATTN_WIDTH), BF16),
        compiler_params=_params(("parallel", "parallel", "arbitrary")),
        name="diffattn",
    )(qi_tab, ki_tab, lam, q, k, v, g_sub)


def _memkv_kernel(m_ref, g_ref, wk_ref, wv_ref, k_ref, v_ref):
    a = _rms(m_ref[0], g_ref[...]).astype(BF16)
    k_ref[0] = jnp.dot(a, wk_ref[...], preferred_element_type=F32).astype(BF16)
    v_ref[0] = jnp.dot(a, wv_ref[...], preferred_element_type=F32).astype(BF16)


def _memkv(mem, g_mem, wk, wv):
    b, m, d = mem.shape
    blk = pl.BlockSpec((1, m, d), lambda i: (i, 0, 0))
    return pl.pallas_call(
        _memkv_kernel,
        grid=(b,),
        in_specs=[blk, _full((1, d)), _full((d, d)), _full((d, d))],
        out_specs=[blk, blk],
        out_shape=[jax.ShapeDtypeStruct((b, m, d), BF16)] * 2,
        compiler_params=_params(("parallel",)),
        name="memkv",
    )(mem, g_mem, wk, wv)


def _mix_kernel(x_ref, ys_ref, ya_ref, wo1_ref, wo2_ref, gx_ref, wq_ref, km_ref, vm_ref, wo_ref, h_ref):
    h = (x_ref[0]
         + jnp.dot(ys_ref[0], wo1_ref[...], preferred_element_type=F32)
         + jnp.dot(ya_ref[0], wo2_ref[...], preferred_element_type=F32))
    hq = _rms(h, gx_ref[...]).astype(BF16)
    q = jnp.dot(hq, wq_ref[...], preferred_element_type=F32).astype(BF16)
    outs = []
    for hd in range(X_HEADS):
        cols = slice(hd * X_HEAD_DIM, (hd + 1) * X_HEAD_DIM)
        s = lax.dot_general(q[:, cols], km_ref[0, :, cols], (((1,), (1,)), ((), ())),
                            preferred_element_type=F32) * (X_HEAD_DIM ** -0.5)
        s = s - jnp.max(s, axis=1, keepdims=True)
        p = jnp.exp(s)
        p = p / jnp.sum(p, axis=1, keepdims=True)
        outs.append(jnp.dot(p.astype(BF16), vm_ref[0, :, cols], preferred_element_type=F32).astype(BF16))
    o = jnp.concatenate(outs, axis=1)
    h_ref[0] = h + jnp.dot(o, wo_ref[...], preferred_element_type=F32)


def _mix(x, y_ssm, y_att, wo1, wo2, g_x, wq, kmem, vmem, wo):
    b, l, d = x.shape
    t = min(MIX_TILE, l)
    m = kmem.shape[1]
    row = lambda w: pl.BlockSpec((1, t, w), lambda i, j: (i, j, 0))
    mem = pl.BlockSpec((1, m, d), lambda i, j: (i, 0, 0))
    return pl.pallas_call(
        _mix_kernel,
        grid=(b, l // t),
        in_specs=[row(d), row(SSM_WIDTH), row(ATTN_WIDTH), _full((SSM_WIDTH, d)), _full((ATTN_WIDTH, d)),
                  _full((1, d)), _full((d, d)), mem, mem, _full((d, d))],
        out_specs=row(d),
        out_shape=jax.ShapeDtypeStruct((b, l, d), F32),
        compiler_params=_params(("parallel", "parallel")),
        name="mix",
    )(x, y_ssm, y_att, wo1, wo2, g_x, wq, kmem, vmem, wo)


def _first_index(hit, idx, sentinel):
    return jnp.min(jnp.where(hit, idx, sentinel), axis=0, keepdims=True)


def _route_kernel(h_ref, g_ref, wr_ref, bias_ref, wsgu_ref, wsd_ref, tri_ref,
                  t_ref, base_ref, eidx_ref, gw_ref, rank_ref, cnt_ref, carry_ref):
    @pl.when(pl.program_id(0) == 0)
    def _():
        carry_ref[...] = jnp.zeros_like(carry_ref)

    h = h_ref[...]
    t = _rms(h, g_ref[...])
    tb = t.astype(BF16)
    t_ref[...] = _pack_rows(t)
    gu = jnp.dot(tb, wsgu_ref[...], preferred_element_type=F32)
    hid = jax.nn.silu(gu[:, :D_EXPERT]) * gu[:, D_EXPERT:]
    base_ref[...] = h + jnp.dot(hid.astype(BF16), wsd_ref[...], preferred_element_type=F32)

    logits = lax.dot_general(wr_ref[...], t, (((1,), (1,)), ((), ())), preferred_element_type=F32,
                             precision=lax.Precision.HIGHEST)
    scores = jax.nn.sigmoid(logits)
    biased = scores + bias_ref[...]
    n_tok = scores.shape[1]
    neg = -jnp.inf
    sub = lax.broadcasted_iota(I32, (PER_GROUP, n_tok), 0)

    gs = []
    for g in range(N_EXPERT_GROUPS):
        blk = biased[g * PER_GROUP:(g + 1) * PER_GROUP]
        m1 = jnp.max(blk, axis=0, keepdims=True)
        i1 = _first_index(blk == m1, sub, PER_GROUP)
        m2 = jnp.max(jnp.where(sub == i1, neg, blk), axis=0, keepdims=True)
        gs.append(m1 + m2)
    gs = jnp.concatenate(gs, axis=0)

    gsel = jnp.zeros(gs.shape, jnp.bool_)
    for _ in range(TOPK_GROUPS):
        m = jnp.max(gs, axis=0, keepdims=True)
        hit = sub == _first_index(gs == m, sub, N_EXPERT_GROUPS)
        gsel = jnp.logical_or(gsel, hit)
        gs = jnp.where(hit, neg, gs)

    masked = jnp.concatenate(
        [jnp.where(gsel[g:g + 1], biased[g * PER_GROUP:(g + 1) * PER_GROUP], neg)
         for g in range(N_EXPERT_GROUPS)], axis=0)
    eid = lax.broadcasted_iota(I32, masked.shape, 0)
    sel = jnp.zeros(masked.shape, jnp.bool_)
    idxs, gws = [], []
    for _ in range(TOP_K):
        m = jnp.max(masked, axis=0, keepdims=True)
        i = _first_index(masked == m, eid, N_EXPERTS)
        hit = eid == i
        idxs.append(i)
        gws.append(jnp.sum(jnp.where(hit, scores, 0.0), axis=0, keepdims=True))
        sel = jnp.logical_or(sel, hit)
        masked = jnp.where(hit, neg, masked)
    eidx = jnp.concatenate(idxs, axis=0)
    gw = jnp.concatenate(gws, axis=0)
    gw = gw / jnp.sum(gw, axis=0, keepdims=True) * ROUTED_SCALE
    eidx_ref[...] = eidx
    gw_ref[...] = gw

    before = jnp.dot(sel.astype(BF16), tri_ref[...], preferred_element_type=F32) + carry_ref[...]
    rank_ref[...] = jnp.concatenate(
        [jnp.sum(jnp.where(eid == idxs[k], before, 0.0), axis=0, keepdims=True) for k in range(TOP_K)],
        axis=0).astype(I32)
    carry = carry_ref[...] + jnp.sum(sel.astype(F32), axis=1, keepdims=True)
    carry_ref[...] = carry
    cnt_ref[...] = carry.astype(I32)


def _route(h2, g_ffn, wr_t, bias_col, wsgu, wsd, part, n_parts):
    n_all, d = h2.shape
    n = n_all // n_parts
    t = min(ROUTE_TILE, n)
    blk0 = part * (n // t)
    tri = (lax.broadcasted_iota(I32, (t, t), 0) < lax.broadcasted_iota(I32, (t, t), 1)).astype(BF16)
    row = lambda w: pl.BlockSpec((t, w), lambda i: (i, 0))
    col = pl.BlockSpec((TOP_K, t), lambda i: (0, i))
    return pl.pallas_call(
        _route_kernel,
        grid=(n // t,),
        in_specs=[pl.BlockSpec((t, d), lambda i: (i + blk0, 0)), _full((1, d)), _full((N_EXPERTS, d)),
                  _full((N_EXPERTS, 1)),
                  _full((d, 2 * D_EXPERT)), _full((D_EXPERT, d)), _full((t, t))],
        out_specs=[row(d // 2), row(d), col, col, col, _full((N_EXPERTS, 1))],
        out_shape=[jax.ShapeDtypeStruct((n, d // 2), I32), jax.ShapeDtypeStruct((n, d), F32),
                   jax.ShapeDtypeStruct((TOP_K, n), I32), jax.ShapeDtypeStruct((TOP_K, n), F32),
                   jax.ShapeDtypeStruct((TOP_K, n), I32), jax.ShapeDtypeStruct((N_EXPERTS, 1), I32)],
        scratch_shapes=[pltpu.VMEM((N_EXPERTS, 1), F32)],
        compiler_params=_params(("arbitrary",)),
        name="route",
    )(h2, g_ffn, wr_t, bias_col, wsgu, wsd, tri)


def _expert_kernel(be_ref, nu_ref, x_hbm, wg_ref, wu_ref, wd_ref, y_ref, wg_s, wu_s, wd_s, x_buf, x_sem):
    i = pl.program_id(0)
    n_used = nu_ref[0]

    def fetch(step):
        slot = step % X_SLOTS
        rows = pl.ds(pl.multiple_of(step * MOE_TILE, MOE_TILE), MOE_TILE)
        return pltpu.make_async_copy(x_hbm.at[rows], x_buf.at[slot], x_sem.at[slot])

    @pl.when(i == 0)
    def _():
        for j in range(X_SLOTS - 1):
            @pl.when(j < n_used)
            def _(j=j):
                fetch(j).start()

    @pl.when(i + (X_SLOTS - 1) < n_used)
    def _():
        fetch(i + (X_SLOTS - 1)).start()

    @pl.when(jnp.logical_or(i == 0, be_ref[i] != be_ref[jnp.maximum(i - 1, 0)]))
    def _():
        wg_s[...] = wg_ref[0].astype(BF16)
        wu_s[...] = wu_ref[0].astype(BF16)
        wd_s[...] = wd_ref[0].astype(BF16)

    @pl.when(i < n_used)
    def _():
        fetch(i).wait()
        x = _unpack_rows(x_buf[i % X_SLOTS]).astype(BF16)
        gate = jnp.dot(x, wg_s[...], preferred_element_type=F32)
        up = jnp.dot(x, wu_s[...], preferred_element_type=F32)
        hid = (jax.nn.silu(gate) * up).astype(BF16)
        y_ref[...] = _pack_rows(jnp.dot(hid, wd_s[...], preferred_element_type=F32))


def _experts(block_e, n_used, xs, w_gate, w_up, w_down):
    n_slots, dw = xs.shape
    _, d, de = w_gate.shape
    nb = n_slots // MOE_TILE
    w_in = pl.BlockSpec((1, d, de), lambda i, be, nu: (be[i], 0, 0))
    return pl.pallas_call(
        _expert_kernel,
        grid_spec=pltpu.PrefetchScalarGridSpec(
            num_scalar_prefetch=2,
            grid=(nb,),
            in_specs=[pl.BlockSpec(memory_space=pl.ANY), w_in, w_in,
                      pl.BlockSpec((1, de, d), lambda i, be, nu: (be[i], 0, 0))],
            out_specs=pl.BlockSpec((MOE_TILE, dw), lambda i, be, nu: (jnp.minimum(i, nu[0] - 1), 0)),
            scratch_shapes=[pltpu.VMEM((d, de), BF16), pltpu.VMEM((d, de), BF16), pltpu.VMEM((de, d), BF16),
                            pltpu.VMEM((X_SLOTS, MOE_TILE, dw), I32), pltpu.SemaphoreType.DMA((X_SLOTS,))]),
        out_shape=jax.ShapeDtypeStruct((n_slots, dw), I32),
        compiler_params=_params(("arbitrary",)),
        name="experts",
    )(block_e, n_used, xs, w_gate, w_up, w_down)


def _sc_worker_id():
    return lax.axis_index("s") * SC_CORES + lax.axis_index("c")


def _sc_dispatch(t_rows, dest3, n_slots):
    _, dw = t_rows.shape
    n_chunks, _, w = dest3.shape
    per_worker = n_chunks // SC_WORKERS
    assert per_worker % 2 == 0
    mesh = plsc.VectorSubcoreMesh(core_axis_name="c", subcore_axis_name="s")
    dt = t_rows.dtype

    @functools.partial(
        pl.kernel, mesh=mesh,
        out_type=jax.ShapeDtypeStruct((n_slots, dw), dt),
        scratch_types=[pltpu.VMEM((TOP_K, w), I32), pltpu.VMEM((TOP_K, w), I32),
                       pltpu.VMEM((w, dw), dt), pltpu.VMEM((w, dw), dt),
                       pltpu.SemaphoreType.DMA, pltpu.SemaphoreType.DMA, pltpu.SemaphoreType.DMA],
    )
    def scatter_rows(t_hbm, dest_hbm, xs_hbm, idx_a, idx_b, rows_a, rows_b, sem_load, sem_a, sem_b):
        wid = _sc_worker_id()

        def scatter(idx_v, rows_v, sem):
            return [pltpu.async_copy(rows_v, xs_hbm.at[idx_v.at[k]], sem) for k in range(TOP_K)]

        @pl.loop(0, per_worker, step=2)
        def _(j):
            ca = wid * per_worker + j
            cb = ca + 1
            pltpu.sync_copy(dest_hbm.at[ca], idx_a)
            pltpu.sync_copy(t_hbm.at[pl.ds(ca * w, w)], rows_a)
            load_idx = pltpu.async_copy(dest_hbm.at[cb], idx_b, sem_load)
            load_rows = pltpu.async_copy(t_hbm.at[pl.ds(cb * w, w)], rows_b, sem_load)
            out_a = scatter(idx_a, rows_a, sem_a)
            load_idx.wait()
            load_rows.wait()
            out_b = scatter(idx_b, rows_b, sem_b)
            for cp in out_a + out_b:
                cp.wait()

    return scatter_rows(t_rows, dest3)


def _sc_combine(y_rows, dest3):
    _, dw = y_rows.shape
    n_chunks, _, w = dest3.shape
    per_worker = n_chunks // SC_WORKERS
    mesh = plsc.VectorSubcoreMesh(core_axis_name="c", subcore_axis_name="s")
    dt = y_rows.dtype

    @functools.partial(
        pl.kernel, mesh=mesh,
        out_type=jax.ShapeDtypeStruct((TOP_K, n_chunks * w, dw), dt),
        scratch_types=[pltpu.VMEM((TOP_K, w), I32), pltpu.VMEM((w, dw), dt), pltpu.VMEM((w, dw), dt),
                       pltpu.SemaphoreType.DMA, pltpu.SemaphoreType.DMA,
                       pltpu.SemaphoreType.DMA, pltpu.SemaphoreType.DMA],
    )
    def gather_rows(y_hbm, dest_hbm, out_hbm, idx_v, buf0, buf1, gsem0, gsem1, wsem0, wsem1):
        wid = _sc_worker_id()
        bufs, gsems, wsems = (buf0, buf1), (gsem0, gsem1), (wsem0, wsem1)

        @pl.loop(0, per_worker)
        def _(j):
            c = wid * per_worker + j
            pltpu.sync_copy(dest_hbm.at[c], idx_v)
            gathers = [None] * TOP_K
            writes = [None] * TOP_K
            gathers[0] = pltpu.async_copy(y_hbm.at[idx_v.at[0]], bufs[0], gsems[0])
            for k in range(TOP_K):
                b = k % 2
                if k + 1 < TOP_K:
                    if k >= 1:
                        writes[k - 1].wait()
                    gathers[k + 1] = pltpu.async_copy(y_hbm.at[idx_v.at[k + 1]], bufs[1 - b], gsems[1 - b])
                gathers[k].wait()
                writes[k] = pltpu.async_copy(bufs[b], out_hbm.at[k, pl.ds(c * w, w)], wsems[b])
            writes[TOP_K - 2].wait()
            writes[TOP_K - 1].wait()

    return gather_rows(y_rows, dest3)


def _final_kernel(base_ref, y_ref, gw_ref, g_ref, *rest, normalize):
    o_ref = rest[-1]
    h = base_ref[...]
    gw = gw_ref[...]
    for k in range(TOP_K):
        h = h + gw[:, k:k + 1] * _unpack_rows(y_ref[k])
    o_ref[...] = _rms(h, g_ref[...]) if normalize else h


def _final(base, yg, gw_rows, g_final, normalize, out_prev, part, n_parts):
    n, d = base.shape
    t = min(FIN_TILE, n)
    blk0 = part * (n // t)
    in_specs = [pl.BlockSpec((t, d), lambda i: (i, 0)), pl.BlockSpec((TOP_K, t, d // 2), lambda i: (0, i, 0)),
                pl.BlockSpec((t, TOP_K), lambda i: (i, 0)), _full((1, d))]
    args = [base, yg, gw_rows, g_final]
    aliases = {}
    if out_prev is not None:
        in_specs.append(pl.BlockSpec(memory_space=pl.ANY))
        args.append(out_prev)
        aliases = {len(args) - 1: 0}
    return pl.pallas_call(
        functools.partial(_final_kernel, normalize=normalize),
        grid=(n // t,),
        in_specs=in_specs,
        out_specs=pl.BlockSpec((t, d), lambda i: (i + blk0, 0)),
        out_shape=jax.ShapeDtypeStruct((n * n_parts, d), F32),
        input_output_aliases=aliases,
        compiler_params=_params(("parallel",)),
        name="final",
    )(*args)


def _ssm_matrices(a_re, a_im, log_dt, b_re, b_im, c_re, c_im):
    lr, li = a_re.astype(F32), a_im.astype(F32)
    dt = jnp.exp(log_dt.astype(F32))[:, None]
    mag = jnp.exp(lr * dt)
    ab_re, ab_im = mag * jnp.cos(li * dt), mag * jnp.sin(li * dt)
    den = lr * lr + li * li
    zr, zi = ab_re - 1.0, ab_im
    k_re = (zr * lr + zi * li) / den
    k_im = (zi * lr - zr * li) / den
    br, bi = b_re.astype(F32), b_im.astype(F32)
    bb_re = k_re[..., None] * br - k_im[..., None] * bi
    bb_im = k_re[..., None] * bi + k_im[..., None] * br
    eye = jnp.eye(SSM_GROUPS, dtype=F32)

    def in_mat(bb):
        return jnp.einsum("gpc,gh->gchp", bb, eye).reshape(SSM_WIDTH, N_STATE)

    def out_mat(c):
        return jnp.einsum("gcp,gh->gphc", c.astype(F32), eye).reshape(N_STATE, SSM_WIDTH)

    bmat = jnp.concatenate([in_mat(bb_re), in_mat(bb_im)], axis=1)
    cmat = jnp.concatenate([out_mat(c_re), -out_mat(c_im)], axis=0)
    tile = lambda a: jnp.broadcast_to(a.reshape(1, N_STATE), (SUBLANES, N_STATE))
    return bmat.astype(BF16), tile(ab_re), tile(ab_im), cmat.astype(BF16)


def _plan_kernel(pstart_ref, eidx_ref, rank_ref, dest_ref, *, window):
    eidx = eidx_ref[...]
    dest = rank_ref[...]
    for e in range(N_EXPERTS):
        dest = dest + jnp.where(eidx == e, pstart_ref[e], 0)
    for c in range(dest.shape[1] // window):
        dest_ref[c] = dest[:, c * window:(c + 1) * window]


def _plan(pstart, eidx, rank, window):
    k, n = eidx.shape
    t = min(PLAN_TILE, n)
    cols = pl.BlockSpec((k, t), lambda i, ps: (0, i))
    return pl.pallas_call(
        functools.partial(_plan_kernel, window=window),
        grid_spec=pltpu.PrefetchScalarGridSpec(
            num_scalar_prefetch=1,
            grid=(n // t,),
            in_specs=[cols, cols],
            out_specs=pl.BlockSpec((t // window, k, window), lambda i, ps: (i, 0, 0))),
        out_shape=jax.ShapeDtypeStruct((n // window, k, window), I32),
        compiler_params=_params(("parallel",)),
        name="plan",
    )(pstart, eidx, rank)


def _block_schedule(counts, n_tok):
    padded = ((counts + MOE_TILE - 1) // MOE_TILE) * MOE_TILE
    pend = jnp.cumsum(padded)
    pstart = (pend - padded).astype(I32)
    n_slots = n_tok * TOP_K + N_EXPERTS * MOE_TILE
    nb = n_slots // MOE_TILE
    n_used = (pend[-1] // MOE_TILE).astype(I32)
    blk = jnp.minimum(jnp.arange(nb, dtype=I32), n_used - 1)
    block_e = jnp.sum((pend[None, :] <= (blk * MOE_TILE)[:, None]).astype(I32), axis=1)
    return pstart, jnp.minimum(block_e, N_EXPERTS - 1), n_used.reshape(1), n_slots


def kernel(x, mem, positions, g_mix, w_in, a_re, a_im, log_dt, b_re, b_im, c_re, c_im, d_skip, w_glu, g_ssm_out, lam_q1, lam_k1, lam_q2, lam_k2, g_sub, w_out, g_x, g_mem, wq_x, wk_x, wv_x, wo_x, g_ffn, w_router, router_bias, w_gate, w_up, w_down, ws_gate, ws_up, ws_down, g_final):
    b, l, d = x.shape
    n = b * l
    depth = w_in.shape[0]
    row = lambda a: a.reshape(1, -1).astype(F32)
    inv = ROPE_THETA ** (-jnp.arange(0, DA_QKDIM, 2, dtype=F32) / DA_QKDIM)
    inv_row = jnp.tile(inv, 128 // inv.shape[0]).reshape(1, 128)
    pos3 = positions.reshape(b, l, 1)

    h = x
    for i in range(depth):
        lambda_init = 0.8 - 0.6 * math.exp(-0.3 * i)
        u, q, k, v = _inproj(h, pos3, row(g_mix[i]), w_in[i].astype(BF16), inv_row)

        bmat, are8, aim8, cmat = _ssm_matrices(a_re[i], a_im[i], log_dt[i], b_re[i], b_im[i], c_re[i], c_im[i])
        u_tm = u.transpose(1, 0, 2).reshape(n, SSM_WIDTH)
        y_ssm = _ssm(u_tm, bmat, are8, aim8, cmat, row(d_skip[i]), w_glu[i].astype(BF16), row(g_ssm_out[i]), b)
        y_ssm = y_ssm.reshape(l, b, SSM_WIDTH).transpose(1, 0, 2)

        lam = (jnp.exp(jnp.sum(lam_q1[i].astype(F32) * lam_k1[i].astype(F32)))
               - jnp.exp(jnp.sum(lam_q2[i].astype(F32) * lam_k2[i].astype(F32))) + lambda_init).reshape(1)
        y_att = _diff_attention(q, k, v, lam.astype(F32), row(g_sub[i]), lambda_init)

        kmem, vmem = _memkv(mem, row(g_mem[i]), wk_x[i].astype(BF16), wv_x[i].astype(BF16))
        wo = w_out[i].astype(BF16)
        h2 = _mix(h, y_ssm, y_att, wo[:SSM_WIDTH], wo[SSM_WIDTH:], row(g_x[i]), wq_x[i].astype(BF16),
                  kmem, vmem, wo_x[i].astype(BF16))

        wsgu = jnp.concatenate([ws_gate[i], ws_up[i]], axis=1).astype(BF16)
        n_part = n // MOE_PARTS
        out = None
        for part in range(MOE_PARTS):
            t_pk, base, eidx, gw, rank, counts = _route(
                h2.reshape(n, d), row(g_ffn[i]), w_router[i].T.astype(F32),
                router_bias[i].reshape(N_EXPERTS, 1).astype(F32), wsgu, ws_down[i].astype(BF16), part, MOE_PARTS)
            pstart, block_e, n_used, n_slots = _block_schedule(counts[:, 0], n_part)
            dest3 = _plan(pstart, eidx, rank, SC_WINDOW)
            xs = _sc_dispatch(t_pk, dest3, n_slots)
            ys = _experts(block_e, n_used, xs, w_gate[i], w_up[i], w_down[i])
            yg = _sc_combine(ys, dest3)
            out = _final(base, yg, gw.T, row(g_final), i == depth - 1, out, part, MOE_PARTS)
        h = out.reshape(b, l, d)
    return h
```

```python
import functools
import math

import jax
import jax.numpy as jnp
from jax import lax
from jax.experimental import pallas as pl
from jax.experimental.pallas import tpu as pltpu
from jax.experimental.pallas import tpu_sc as plsc

F32 = jnp.float32
BF16 = jnp.bfloat16
I32 = jnp.int32

D_MODEL = 1024
SSM_WIDTH = 512
ATTN_WIDTH = 512
SSM_GROUP = 16
SSM_GROUPS = 32
SSM_STATE = 64
N_STATE = SSM_GROUPS * SSM_STATE
DA_HEADS = 4
DA_VDIM = 128
DA_QKDIM = 64
ROPE_THETA = 10000.0
X_HEADS = 4
X_HEAD_DIM = 256
N_EXPERTS = 64
TOP_K = 8
N_EXPERT_GROUPS = 8
PER_GROUP = N_EXPERTS // N_EXPERT_GROUPS
TOPK_GROUPS = 4
D_EXPERT = 256
ROUTED_SCALE = 2.5
EPS = 1e-6

VMEM_LIMIT_V7X = 56 * 1024 * 1024
SUBLANES = 8

IN_TILE = 512
SSM_STEPS = 64
SSM_COLS = 1024
SSM_CH = 128
SSM_SLAB = 512
ATT_TILE = 1024
ATT_ROWS = 256
MIX_TILE = 512
ROUTE_TILE = 512
MOE_TILE = 512
X_SLOTS = 3
PLAN_TILE = 2048
MOE_PARTS = 2
FIN_TILE = 256

SC_CORES = 2
SC_WORKERS = 32
SC_WINDOW = 64


def _params(sem):
    return pltpu.CompilerParams(dimension_semantics=sem, vmem_limit_bytes=VMEM_LIMIT_V7X)


def _rms(x, g):
    return x * lax.rsqrt(jnp.mean(x * x, axis=-1, keepdims=True) + EPS) * g


def _full(shape):
    return pl.BlockSpec(shape, lambda *_: (0,) * len(shape))


def _pack_rows(a):
    w = a.shape[1] // 2
    bits = lambda v: lax.bitcast_convert_type(v.astype(BF16).astype(F32), I32)
    return (bits(a[:, w:]) & jnp.int32(-65536)) | lax.shift_right_logical(bits(a[:, :w]), 16)


def _unpack_rows(p):
    lo = lax.bitcast_convert_type(lax.shift_left(p, 16), F32)
    hi = lax.bitcast_convert_type(p & jnp.int32(-65536), F32)
    return jnp.concatenate([lo, hi], axis=1)


def _inproj_kernel(x_ref, pos_ref, g_ref, w_ref, inv_ref, u_ref, q_ref, k_ref, v_ref):
    x = x_ref[0]
    a = _rms(x, g_ref[...]).astype(BF16)
    z = jnp.dot(a, w_ref[...], preferred_element_type=F32)
    u_ref[0] = z[:, :SSM_WIDTH]
    ang = pos_ref[0].astype(F32) * inv_ref[...]
    cos = jnp.concatenate([jnp.cos(ang)] * 4, axis=1)
    sin = jnp.concatenate([jnp.sin(ang)] * 4, axis=1)
    lane = lax.broadcasted_iota(I32, cos.shape, 1)
    first = (lane & (DA_QKDIM - 1)) < DA_QKDIM // 2
    half = DA_QKDIM // 2

    def rope(t):
        rot = jnp.where(first, -pltpu.roll(t, ATTN_WIDTH - half, 1), pltpu.roll(t, half, 1))
        return t * cos + rot * sin

    q = z[:, SSM_WIDTH:SSM_WIDTH + ATTN_WIDTH]
    k = z[:, SSM_WIDTH + ATTN_WIDTH:SSM_WIDTH + 2 * ATTN_WIDTH]
    q_ref[0] = (rope(q) * (DA_QKDIM ** -0.5)).astype(BF16)
    k_ref[0] = rope(k).astype(BF16)
    v_ref[0] = z[:, SSM_WIDTH + 2 * ATTN_WIDTH:].astype(BF16)


def _inproj(x, pos3, g_mix, w_in, inv_row):
    b, l, d = x.shape
    t = min(IN_TILE, l)
    n_out = w_in.shape[1]
    row = lambda w: pl.BlockSpec((1, t, w), lambda i, j: (i, j, 0))
    return pl.pallas_call(
        _inproj_kernel,
        grid=(b, l // t),
        in_specs=[row(d), row(1), _full((1, d)), _full((d, n_out)), _full((1, 128))],
        out_specs=[row(SSM_WIDTH), row(ATTN_WIDTH), row(ATTN_WIDTH), row(ATTN_WIDTH)],
        out_shape=[jax.ShapeDtypeStruct((b, l, SSM_WIDTH), F32)]
        + [jax.ShapeDtypeStruct((b, l, ATTN_WIDTH), BF16)] * 3,
        compiler_params=_params(("parallel", "parallel")),
        name="inproj",
    )(x, pos3, g_mix, w_in, inv_row)


def _ssm_kernel(u_ref, bm_ref, are_ref, aim_ref, cm_ref, dskip_ref, wglu_ref, g_ref, o_ref, bu_ref, st_ref,
                *, steps):
    @pl.when(pl.program_id(0) == 0)
    def _():
        st_ref[...] = jnp.zeros_like(st_ref)

    u = pltpu.einshape("bts->tbs", u_ref[...]).reshape(steps * SUBLANES, SSM_WIDTH)
    ub = u.astype(BF16)

    n_slabs = SSM_WIDTH // SSM_CH
    for s in range(n_slabs):
        ch = slice(s * SSM_CH, (s + 1) * SSM_CH)
        for part in (0, N_STATE):
            cols = slice(part + s * SSM_SLAB, part + (s + 1) * SSM_SLAB)
            bu_ref[:, cols] = jnp.dot(ub[:, ch], bm_ref[ch, cols], preferred_element_type=F32)

    for c0 in range(0, N_STATE, SSM_COLS):
        re = slice(c0, c0 + SSM_COLS)
        im = slice(N_STATE + c0, N_STATE + c0 + SSM_COLS)
        ar = are_ref[:, re]
        ai = aim_ref[:, re]

        def step(t, carry, re=re, im=im, ar=ar, ai=ai):
            sr, si = carry
            rows = pl.ds(pl.multiple_of(t * SUBLANES, SUBLANES), SUBLANES)
            nr = ar * sr - ai * si + bu_ref[rows, re]
            ni = ar * si + ai * sr + bu_ref[rows, im]
            bu_ref[rows, re] = nr
            bu_ref[rows, im] = ni
            return nr, ni

        sr, si = lax.fori_loop(0, steps, step, (st_ref[:, re], st_ref[:, im]))
        st_ref[:, re] = sr
        st_ref[:, im] = si

    ys = []
    for s in range(n_slabs):
        ch = slice(s * SSM_CH, (s + 1) * SSM_CH)
        acc = None
        for part in (0, N_STATE):
            cols = slice(part + s * SSM_SLAB, part + (s + 1) * SSM_SLAB)
            d = jnp.dot(bu_ref[:, cols].astype(BF16), cm_ref[cols, ch], preferred_element_type=F32)
            acc = d if acc is None else acc + d
        ys.append(acc)
    y = jnp.concatenate(ys, axis=1) + dskip_ref[...] * u
    y = jax.nn.gelu(y)
    y = y * jax.nn.sigmoid(jnp.dot(y.astype(BF16), wglu_ref[...], preferred_element_type=F32))
    out = _rms(y, g_ref[...]).reshape(steps, SUBLANES, SSM_WIDTH)
    o_ref[...] = pltpu.einshape("tbs->bts", out).astype(BF16)


def _ssm(u, bmat, a_re8, a_im8, cmat, d_skip, w_glu, g_out):
    batch, l, _ = u.shape
    assert batch == SUBLANES, "one time step of all sequences must fill the sublanes"
    steps = min(SSM_STEPS, l)
    rows = steps * batch
    seq = pl.BlockSpec((batch, steps, SSM_WIDTH), lambda i: (0, i, 0))
    return pl.pallas_call(
        functools.partial(_ssm_kernel, steps=steps),
        grid=(l // steps,),
        in_specs=[seq,
                  _full((SSM_WIDTH, 2 * N_STATE)), _full((SUBLANES, N_STATE)), _full((SUBLANES, N_STATE)),
                  _full((2 * N_STATE, SSM_WIDTH)), _full((1, SSM_WIDTH)), _full((SSM_WIDTH, SSM_WIDTH)),
                  _full((1, SSM_WIDTH))],
        out_specs=seq,
        out_shape=jax.ShapeDtypeStruct((batch, l, SSM_WIDTH), BF16),
        scratch_shapes=[pltpu.VMEM((rows, 2 * N_STATE), F32), pltpu.VMEM((SUBLANES, 2 * N_STATE), F32)],
        compiler_params=_params(("arbitrary",)),
        name="ssm",
    )(u, bmat, a_re8, a_im8, cmat, d_skip, w_glu, g_out)


def _attn_kernel(qi_ref, ki_ref, lam_ref, q_ref, k_ref, v_ref, g_ref, o_ref, qs_ref, m_ref, acc_ref, *,
                 tile, out_scale):
    qi = qi_ref[pl.program_id(2)]
    ki = ki_ref[pl.program_id(2)]

    @pl.when(ki == 0)
    def _():
        q = q_ref[0]
        lane = lax.broadcasted_iota(I32, q.shape, 1)
        zero = jnp.zeros_like(q)
        qs_ref[:tile] = jnp.where(lane < DA_QKDIM, q, zero)
        qs_ref[tile:] = jnp.where(lane >= DA_QKDIM, q, zero)
        m_ref[...] = jnp.full_like(m_ref, -jnp.inf)
        acc_ref[...] = jnp.zeros_like(acc_ref)

    def update(masked):
        v_ext = jnp.concatenate([v_ref[0], jnp.ones((tile, DA_VDIM), BF16)], axis=1)
        rb = min(ATT_ROWS, tile)
        n_rb = 2 * tile // rb

        def n_keys(r):
            return (r * rb) % tile + rb if masked else tile

        def scores(r):
            return lax.dot_general(qs_ref[r * rb:(r + 1) * rb], k_ref[0, :n_keys(r)], (((1,), (1,)), ((), ())),
                                   preferred_element_type=F32)

        s_next = scores(0)
        for r in range(n_rb):
            s = s_next
            if r + 1 < n_rb:
                s_next = scores(r + 1)
            rows = slice(r * rb, (r + 1) * rb)
            nk = n_keys(r)
            if masked:
                qpos = lax.broadcasted_iota(I32, s.shape, 0) + (r * rb) % tile
                s = jnp.where(lax.broadcasted_iota(I32, s.shape, 1) <= qpos, s, jnp.finfo(F32).min)
            m_old = m_ref[rows]
            m_new = jnp.maximum(m_old, jnp.max(s, axis=1, keepdims=True))
            p = jnp.exp(s - jnp.concatenate([m_new] * (nk // DA_VDIM), axis=1))
            alpha = jnp.exp(m_old - m_new)
            acc_ref[rows] = (jnp.concatenate([alpha, alpha], axis=1) * acc_ref[rows]
                             + jnp.dot(p.astype(BF16), v_ext[:nk], preferred_element_type=F32))
            m_ref[rows] = m_new

    @pl.when(ki < qi)
    def _():
        update(False)

    @pl.when(ki == qi)
    def _():
        update(True)
        o = acc_ref[:, :DA_VDIM] / acc_ref[:, DA_VDIM:]
        o = o[:tile] - lam_ref[0] * o[tile:]
        o_ref[0] = (_rms(o, g_ref[...]) * out_scale).astype(BF16)


def _diff_attention(q, k, v, lam, g_sub, lambda_init):
    b, l, _ = q.shape
    t = min(ATT_TILE, l)
    n = l // t
    pairs = [(qi, ki) for qi in range(n) for ki in range(qi + 1)]
    qi_tab = jnp.asarray([p[0] for p in pairs], I32)
    ki_tab = jnp.asarray([p[1] for p in pairs], I32)
    qspec = pl.BlockSpec((1, t, DA_VDIM), lambda bi, h, s, qt, kt: (bi, qt[s], h))
    kspec = pl.BlockSpec((1, t, DA_VDIM), lambda bi, h, s, qt, kt: (bi, kt[s], h))
    return pl.pallas_call(
        functools.partial(_attn_kernel, tile=t, out_scale=1.0 - lambda_init),
        grid_spec=pltpu.PrefetchScalarGridSpec(
            num_scalar_prefetch=2,
            grid=(b, DA_HEADS, len(pairs)),
            in_specs=[pl.BlockSpec(memory_space=pltpu.SMEM), qspec, kspec, kspec,
                      pl.BlockSpec((1, DA_VDIM), lambda bi, h, s, qt, kt: (0, 0))],
            out_specs=qspec,
            scratch_shapes=[pltpu.VMEM((2 * t, DA_VDIM), BF16), pltpu.VMEM((2 * t, DA_VDIM), F32),
                            pltpu.VMEM((2 * t, 2 * DA_VDIM), F32)]),
        out_shape=jax.ShapeDtypeStruct((b, l, ATTN_WIDTH), BF16),
        compiler_params=_params(("parallel", "parallel", "arbitrary")),
        name="diffattn",
    )(qi_tab, ki_tab, lam, q, k, v, g_sub)


def _memkv_kernel(m_ref, g_ref, wk_ref, wv_ref, k_ref, v_ref):
    a = _rms(m_ref[0], g_ref[...]).astype(BF16)
    k_ref[0] = jnp.dot(a, wk_ref[...], preferred_element_type=F32).astype(BF16)
    v_ref[0] = jnp.dot(a, wv_ref[...], preferred_element_type=F32).astype(BF16)


def _memkv(mem, g_mem, wk, wv):
    b, m, d = mem.shape
    blk = pl.BlockSpec((1, m, d), lambda i: (i, 0, 0))
    return pl.pallas_call(
        _memkv_kernel,
        grid=(b,),
        in_specs=[blk, _full((1, d)), _full((d, d)), _full((d, d))],
        out_specs=[blk, blk],
        out_shape=[jax.ShapeDtypeStruct((b, m, d), BF16)] * 2,
        compiler_params=_params(("parallel",)),
        name="memkv",
    )(mem, g_mem, wk, wv)


def _mix_kernel(x_ref, ys_ref, ya_ref, wo1_ref, wo2_ref, gx_ref, wq_ref, km_ref, vm_ref, wo_ref, h_ref):
    h = (x_ref[0]
         + jnp.dot(ys_ref[0], wo1_ref[...], preferred_element_type=F32)
         + jnp.dot(ya_ref[0], wo2_ref[...], preferred_element_type=F32))
    hq = _rms(h, gx_ref[...]).astype(BF16)
    q = jnp.dot(hq, wq_ref[...], preferred_element_type=F32).astype(BF16)
    outs = []
    for hd in range(X_HEADS):
        cols = slice(hd * X_HEAD_DIM, (hd + 1) * X_HEAD_DIM)
        s = lax.dot_general(q[:, cols], km_ref[0, :, cols], (((1,), (1,)), ((), ())),
                            preferred_element_type=F32) * (X_HEAD_DIM ** -0.5)
        s = s - jnp.max(s, axis=1, keepdims=True)
        p = jnp.exp(s)
        p = p / jnp.sum(p, axis=1, keepdims=True)
        outs.append(jnp.dot(p.astype(BF16), vm_ref[0, :, cols], preferred_element_type=F32).astype(BF16))
    o = jnp.concatenate(outs, axis=1)
    h_ref[0] = h + jnp.dot(o, wo_ref[...], preferred_element_type=F32)


def _mix(x, y_ssm, y_att, wo1, wo2, g_x, wq, kmem, vmem, wo):
    b, l, d = x.shape
    t = min(MIX_TILE, l)
    m = kmem.shape[1]
    row = lambda w: pl.BlockSpec((1, t, w), lambda i, j: (i, j, 0))
    mem = pl.BlockSpec((1, m, d), lambda i, j: (i, 0, 0))
    return pl.pallas_call(
        _mix_kernel,
        grid=(b, l // t),
        in_specs=[row(d), row(SSM_WIDTH), row(ATTN_WIDTH), _full((SSM_WIDTH, d)), _full((ATTN_WIDTH, d)),
                  _full((1, d)), _full((d, d)), mem, mem, _full((d, d))],
        out_specs=row(d),
        out_shape=jax.ShapeDtypeStruct((b, l, d), F32),
        compiler_params=_params(("parallel", "parallel")),
        name="mix",
    )(x, y_ssm, y_att, wo1, wo2, g_x, wq, kmem, vmem, wo)


def _first_index(hit, idx, sentinel):
    return jnp.min(jnp.where(hit, idx, sentinel), axis=0, keepdims=True)


def _route_kernel(h_ref, g_ref, wr_ref, bias_ref, wsgu_ref, wsd_ref, tri_ref,
                  t_ref, base_ref, eidx_ref, gw_ref, rank_ref, cnt_ref, carry_ref):
    @pl.when(pl.program_id(0) == 0)
    def _():
        carry_ref[...] = jnp.zeros_like(carry_ref)

    h = h_ref[...]
    t = _rms(h, g_ref[...])
    tb = t.astype(BF16)
    t_ref[...] = _pack_rows(t)
    gu = jnp.dot(tb, wsgu_ref[...], preferred_element_type=F32)
    hid = jax.nn.silu(gu[:, :D_EXPERT]) * gu[:, D_EXPERT:]
    base_ref[...] = h + jnp.dot(hid.astype(BF16), wsd_ref[...], preferred_element_type=F32)

    logits = lax.dot_general(wr_ref[...], t, (((1,), (1,)), ((), ())), preferred_element_type=F32,
                             precision=lax.Precision.HIGHEST)
    scores = jax.nn.sigmoid(logits)
    biased = scores + bias_ref[...]
    n_tok = scores.shape[1]
    neg = -jnp.inf
    sub = lax.broadcasted_iota(I32, (PER_GROUP, n_tok), 0)

    gs = []
    for g in range(N_EXPERT_GROUPS):
        blk = biased[g * PER_GROUP:(g + 1) * PER_GROUP]
        m1 = jnp.max(blk, axis=0, keepdims=True)
        i1 = _first_index(blk == m1, sub, PER_GROUP)
        m2 = jnp.max(jnp.where(sub == i1, neg, blk), axis=0, keepdims=True)
        gs.append(m1 + m2)
    gs = jnp.concatenate(gs, axis=0)

    gsel = jnp.zeros(gs.shape, jnp.bool_)
    for _ in range(TOPK_GROUPS):
        m = jnp.max(gs, axis=0, keepdims=True)
        hit = sub == _first_index(gs == m, sub, N_EXPERT_GROUPS)
        gsel = jnp.logical_or(gsel, hit)
        gs = jnp.where(hit, neg, gs)

    masked = jnp.concatenate(
        [jnp.where(gsel[g:g + 1], biased[g * PER_GROUP:(g + 1) * PER_GROUP], neg)
         for g in range(N_EXPERT_GROUPS)], axis=0)
    eid = lax.broadcasted_iota(I32, masked.shape, 0)
    sel = jnp.zeros(masked.shape, jnp.bool_)
    idxs, gws = [], []
    for _ in range(TOP_K):
        m = jnp.max(masked, axis=0, keepdims=True)
        i = _first_index(masked == m, eid, N_EXPERTS)
        hit = eid == i
        idxs.append(i)
        gws.append(jnp.sum(jnp.where(hit, scores, 0.0), axis=0, keepdims=True))
        sel = jnp.logical_or(sel, hit)
        masked = jnp.where(hit, neg, masked)
    eidx = jnp.concatenate(idxs, axis=0)
    gw = jnp.concatenate(gws, axis=0)
    gw = gw / jnp.sum(gw, axis=0, keepdims=True) * ROUTED_SCALE
    eidx_ref[...] = eidx
    gw_ref[...] = gw

    before = jnp.dot(sel.astype(BF16), tri_ref[...], preferred_element_type=F32) + carry_ref[...]
    rank_ref[...] = jnp.concatenate(
        [jnp.sum(jnp.where(eid == idxs[k], before, 0.0), axis=0, keepdims=True) for k in range(TOP_K)],
        axis=0).astype(I32)
    carry = carry_ref[...] + jnp.sum(sel.astype(F32), axis=1, keepdims=True)
    carry_ref[...] = carry
    cnt_ref[...] = carry.astype(I32)


def _route(h2, g_ffn, wr_t, bias_col, wsgu, wsd, part, n_parts):
    n_all, d = h2.shape
    n = n_all // n_parts
    t = min(ROUTE_TILE, n)
    blk0 = part * (n // t)
    tri = (lax.broadcasted_iota(I32, (t, t), 0) < lax.broadcasted_iota(I32, (t, t), 1)).astype(BF16)
    row = lambda w: pl.BlockSpec((t, w), lambda i: (i, 0))
    col = pl.BlockSpec((TOP_K, t), lambda i: (0, i))
    return pl.pallas_call(
        _route_kernel,
        grid=(n // t,),
        in_specs=[pl.BlockSpec((t, d), lambda i: (i + blk0, 0)), _full((1, d)), _full((N_EXPERTS, d)),
                  _full((N_EXPERTS, 1)),
                  _full((d, 2 * D_EXPERT)), _full((D_EXPERT, d)), _full((t, t))],
        out_specs=[row(d // 2), row(d), col, col, col, _full((N_EXPERTS, 1))],
        out_shape=[jax.ShapeDtypeStruct((n, d // 2), I32), jax.ShapeDtypeStruct((n, d), F32),
                   jax.ShapeDtypeStruct((TOP_K, n), I32), jax.ShapeDtypeStruct((TOP_K, n), F32),
                   jax.ShapeDtypeStruct((TOP_K, n), I32), jax.ShapeDtypeStruct((N_EXPERTS, 1), I32)],
        scratch_shapes=[pltpu.VMEM((N_EXPERTS, 1), F32)],
        compiler_params=_params(("arbitrary",)),
        name="route",
    )(h2, g_ffn, wr_t, bias_col, wsgu, wsd, tri)


def _expert_kernel(be_ref, nu_ref, x_hbm, wg_ref, wu_ref, wd_ref, y_ref, wg_s, wu_s, wd_s, x_buf, x_sem):
    i = pl.program_id(0)
    n_used = nu_ref[0]

    def fetch(step):
        slot = step % X_SLOTS
        rows = pl.ds(pl.multiple_of(step * MOE_TILE, MOE_TILE), MOE_TILE)
        return pltpu.make_async_copy(x_hbm.at[rows], x_buf.at[slot], x_sem.at[slot])

    @pl.when(i == 0)
    def _():
        for j in range(X_SLOTS - 1):
            @pl.when(j < n_used)
            def _(j=j):
                fetch(j).start()

    @pl.when(i + (X_SLOTS - 1) < n_used)
    def _():
        fetch(i + (X_SLOTS - 1)).start()

    @pl.when(jnp.logical_or(i == 0, be_ref[i] != be_ref[jnp.maximum(i - 1, 0)]))
    def _():
        wg_s[...] = wg_ref[0].astype(BF16)
        wu_s[...] = wu_ref[0].astype(BF16)
        wd_s[...] = wd_ref[0].astype(BF16)

    @pl.when(i < n_used)
    def _():
        fetch(i).wait()
        x = _unpack_rows(x_buf[i % X_SLOTS]).astype(BF16)
        gate = jnp.dot(x, wg_s[...], preferred_element_type=F32)
        up = jnp.dot(x, wu_s[...], preferred_element_type=F32)
        hid = (jax.nn.silu(gate) * up).astype(BF16)
        y_ref[...] = _pack_rows(jnp.dot(hid, wd_s[...], preferred_element_type=F32))


def _experts(block_e, n_used, xs, w_gate, w_up, w_down):
    n_slots, dw = xs.shape
    _, d, de = w_gate.shape
    nb = n_slots // MOE_TILE
    w_in = pl.BlockSpec((1, d, de), lambda i, be, nu: (be[i], 0, 0))
    return pl.pallas_call(
        _expert_kernel,
        grid_spec=pltpu.PrefetchScalarGridSpec(
            num_scalar_prefetch=2,
            grid=(nb,),
            in_specs=[pl.BlockSpec(memory_space=pl.ANY), w_in, w_in,
                      pl.BlockSpec((1, de, d), lambda i, be, nu: (be[i], 0, 0))],
            out_specs=pl.BlockSpec((MOE_TILE, dw), lambda i, be, nu: (jnp.minimum(i, nu[0] - 1), 0)),
            scratch_shapes=[pltpu.VMEM((d, de), BF16), pltpu.VMEM((d, de), BF16), pltpu.VMEM((de, d), BF16),
                            pltpu.VMEM((X_SLOTS, MOE_TILE, dw), I32), pltpu.SemaphoreType.DMA((X_SLOTS,))]),
        out_shape=jax.ShapeDtypeStruct((n_slots, dw), I32),
        compiler_params=_params(("arbitrary",)),
        name="experts",
    )(block_e, n_used, xs, w_gate, w_up, w_down)


def _sc_worker_id():
    return lax.axis_index("s") * SC_CORES + lax.axis_index("c")


def _sc_dispatch(t_rows, dest3, n_slots):
    _, dw = t_rows.shape
    n_chunks, _, w = dest3.shape
    per_worker = n_chunks // SC_WORKERS
    assert per_worker % 2 == 0
    mesh = plsc.VectorSubcoreMesh(core_axis_name="c", subcore_axis_name="s")
    dt = t_rows.dtype

    @functools.partial(
        pl.kernel, mesh=mesh,
        out_type=jax.ShapeDtypeStruct((n_slots, dw), dt),
        scratch_types=[pltpu.VMEM((TOP_K, w), I32), pltpu.VMEM((TOP_K, w), I32),
                       pltpu.VMEM((w, dw), dt), pltpu.VMEM((w, dw), dt),
                       pltpu.SemaphoreType.DMA, pltpu.SemaphoreType.DMA, pltpu.SemaphoreType.DMA],
    )
    def scatter_rows(t_hbm, dest_hbm, xs_hbm, idx_a, idx_b, rows_a, rows_b, sem_load, sem_a, sem_b):
        wid = _sc_worker_id()

        def scatter(idx_v, rows_v, sem):
            return [pltpu.async_copy(rows_v, xs_hbm.at[idx_v.at[k]], sem) for k in range(TOP_K)]

        @pl.loop(0, per_worker, step=2)
        def _(j):
            ca = wid * per_worker + j
            cb = ca + 1
            pltpu.sync_copy(dest_hbm.at[ca], idx_a)
            pltpu.sync_copy(t_hbm.at[pl.ds(ca * w, w)], rows_a)
            load_idx = pltpu.async_copy(dest_hbm.at[cb], idx_b, sem_load)
            load_rows = pltpu.async_copy(t_hbm.at[pl.ds(cb * w, w)], rows_b, sem_load)
            out_a = scatter(idx_a, rows_a, sem_a)
            load_idx.wait()
            load_rows.wait()
            out_b = scatter(idx_b, rows_b, sem_b)
            for cp in out_a + out_b:
                cp.wait()

    return scatter_rows(t_rows, dest3)


def _sc_combine(y_rows, dest3):
    _, dw = y_rows.shape
    n_chunks, _, w = dest3.shape
    per_worker = n_chunks // SC_WORKERS
    mesh = plsc.VectorSubcoreMesh(core_axis_name="c", subcore_axis_name="s")
    dt = y_rows.dtype

    @functools.partial(
        pl.kernel, mesh=mesh,
        out_type=jax.ShapeDtypeStruct((TOP_K, n_chunks * w, dw), dt),
        scratch_types=[pltpu.VMEM((TOP_K, w), I32), pltpu.VMEM((w, dw), dt), pltpu.VMEM((w, dw), dt),
                       pltpu.SemaphoreType.DMA, pltpu.SemaphoreType.DMA,
                       pltpu.SemaphoreType.DMA, pltpu.SemaphoreType.DMA],
    )
    def gather_rows(y_hbm, dest_hbm, out_hbm, idx_v, buf0, buf1, gsem0, gsem1, wsem0, wsem1):
        wid = _sc_worker_id()
        bufs, gsems, wsems = (buf0, buf1), (gsem0, gsem1), (wsem0, wsem1)

        @pl.loop(0, per_worker)
        def _(j):
            c = wid * per_worker + j
            pltpu.sync_copy(dest_hbm.at[c], idx_v)
            gathers = [None] * TOP_K
            writes = [None] * TOP_K
            gathers[0] = pltpu.async_copy(y_hbm.at[idx_v.at[0]], bufs[0], gsems[0])
            for k in range(TOP_K):
                b = k % 2
                if k + 1 < TOP_K:
                    if k >= 1:
                        writes[k - 1].wait()
                    gathers[k + 1] = pltpu.async_copy(y_hbm.at[idx_v.at[k + 1]], bufs[1 - b], gsems[1 - b])
                gathers[k].wait()
                writes[k] = pltpu.async_copy(bufs[b], out_hbm.at[k, pl.ds(c * w, w)], wsems[b])
            writes[TOP_K - 2].wait()
            writes[TOP_K - 1].wait()

    return gather_rows(y_rows, dest3)


def _final_kernel(base_ref, y_ref, gw_ref, g_ref, *rest, normalize):
    o_ref = rest[-1]
    h = base_ref[...]
    gw = gw_ref[...]
    for k in range(TOP_K):
        h = h + gw[:, k:k + 1] * _unpack_rows(y_ref[k])
    o_ref[...] = _rms(h, g_ref[...]) if normalize else h


def _final(base, yg, gw_rows, g_final, normalize, out_prev, part, n_parts):
    n, d = base.shape
    t = min(FIN_TILE, n)
    blk0 = part * (n // t)
    in_specs = [pl.BlockSpec((t, d), lambda i: (i, 0)), pl.BlockSpec((TOP_K, t, d // 2), lambda i: (0, i, 0)),
                pl.BlockSpec((t, TOP_K), lambda i: (i, 0)), _full((1, d))]
    args = [base, yg, gw_rows, g_final]
    aliases = {}
    if out_prev is not None:
        in_specs.append(pl.BlockSpec(memory_space=pl.ANY))
        args.append(out_prev)
        aliases = {len(args) - 1: 0}
    return pl.pallas_call(
        functools.partial(_final_kernel, normalize=normalize),
        grid=(n // t,),
        in_specs=in_specs,
        out_specs=pl.BlockSpec((t, d), lambda i: (i + blk0, 0)),
        out_shape=jax.ShapeDtypeStruct((n * n_parts, d), F32),
        input_output_aliases=aliases,
        compiler_params=_params(("parallel",)),
        name="final",
    )(*args)


def _ssm_matrices(a_re, a_im, log_dt, b_re, b_im, c_re, c_im):
    lr, li = a_re.astype(F32), a_im.astype(F32)
    dt = jnp.exp(log_dt.astype(F32))[:, None]
    mag = jnp.exp(lr * dt)
    ab_re, ab_im = mag * jnp.cos(li * dt), mag * jnp.sin(li * dt)
    den = lr * lr + li * li
    zr, zi = ab_re - 1.0, ab_im
    k_re = (zr * lr + zi * li) / den
    k_im = (zi * lr - zr * li) / den
    br, bi = b_re.astype(F32), b_im.astype(F32)
    bb_re = k_re[..., None] * br - k_im[..., None] * bi
    bb_im = k_re[..., None] * bi + k_im[..., None] * br
    eye = jnp.eye(SSM_GROUPS, dtype=F32)

    def in_mat(bb):
        return jnp.einsum("gpc,gh->gchp", bb, eye).reshape(SSM_WIDTH, N_STATE)

    def out_mat(c):
        return jnp.einsum("gcp,gh->gphc", c.astype(F32), eye).reshape(N_STATE, SSM_WIDTH)

    bmat = jnp.concatenate([in_mat(bb_re), in_mat(bb_im)], axis=1)
    cmat = jnp.concatenate([out_mat(c_re), -out_mat(c_im)], axis=0)
    tile = lambda a: jnp.broadcast_to(a.reshape(1, N_STATE), (SUBLANES, N_STATE))
    return bmat.astype(BF16), tile(ab_re), tile(ab_im), cmat.astype(BF16)


def _plan_kernel(pstart_ref, eidx_ref, rank_ref, dest_ref, *, window):
    eidx = eidx_ref[...]
    dest = rank_ref[...]
    for e in range(N_EXPERTS):
        dest = dest + jnp.where(eidx == e, pstart_ref[e], 0)
    for c in range(dest.shape[1] // window):
        dest_ref[c] = dest[:, c * window:(c + 1) * window]


def _plan(pstart, eidx, rank, window):
    k, n = eidx.shape
    t = min(PLAN_TILE, n)
    cols = pl.BlockSpec((k, t), lambda i, ps: (0, i))
    return pl.pallas_call(
        functools.partial(_plan_kernel, window=window),
        grid_spec=pltpu.PrefetchScalarGridSpec(
            num_scalar_prefetch=1,
            grid=(n // t,),
            in_specs=[cols, cols],
            out_specs=pl.BlockSpec((t // window, k, window), lambda i, ps: (i, 0, 0))),
        out_shape=jax.ShapeDtypeStruct((n // window, k, window), I32),
        compiler_params=_params(("parallel",)),
        name="plan",
    )(pstart, eidx, rank)


def _block_schedule(counts, n_tok):
    padded = ((counts + MOE_TILE - 1) // MOE_TILE) * MOE_TILE
    pend = jnp.cumsum(padded)
    pstart = (pend - padded).astype(I32)
    n_slots = n_tok * TOP_K + N_EXPERTS * MOE_TILE
    nb = n_slots // MOE_TILE
    n_used = (pend[-1] // MOE_TILE).astype(I32)
    blk = jnp.minimum(jnp.arange(nb, dtype=I32), n_used - 1)
    block_e = jnp.sum((pend[None, :] <= (blk * MOE_TILE)[:, None]).astype(I32), axis=1)
    return pstart, jnp.minimum(block_e, N_EXPERTS - 1), n_used.reshape(1), n_slots


def kernel(x, mem, positions, g_mix, w_in, a_re, a_im, log_dt, b_re, b_im, c_re, c_im, d_skip, w_glu, g_ssm_out, lam_q1, lam_k1, lam_q2, lam_k2, g_sub, w_out, g_x, g_mem, wq_x, wk_x, wv_x, wo_x, g_ffn, w_router, router_bias, w_gate, w_up, w_down, ws_gate, ws_up, ws_down, g_final):
    b, l, d = x.shape
    n = b * l
    depth = w_in.shape[0]
    row = lambda a: a.reshape(1, -1).astype(F32)
    inv = ROPE_THETA ** (-jnp.arange(0, DA_QKDIM, 2, dtype=F32) / DA_QKDIM)
    inv_row = jnp.tile(inv, 128 // inv.shape[0]).reshape(1, 128)
    pos3 = positions.reshape(b, l, 1)

    h = x
    for i in range(depth):
        lambda_init = 0.8 - 0.6 * math.exp(-0.3 * i)
        u, q, k, v = _inproj(h, pos3, row(g_mix[i]), w_in[i].astype(BF16), inv_row)

        bmat, are8, aim8, cmat = _ssm_matrices(a_re[i], a_im[i], log_dt[i], b_re[i], b_im[i], c_re[i], c_im[i])
        y_ssm = _ssm(u, bmat, are8, aim8, cmat, row(d_skip[i]), w_glu[i].astype(BF16), row(g_ssm_out[i]))

        lam = (jnp.exp(jnp.sum(lam_q1[i].astype(F32) * lam_k1[i].astype(F32)))
               - jnp.exp(jnp.sum(lam_q2[i].astype(F32) * lam_k2[i].astype(F32))) + lambda_init).reshape(1)
        y_att = _diff_attention(q, k, v, lam.astype(F32), row(g_sub[i]), lambda_init)

        kmem, vmem = _memkv(mem, row(g_mem[i]), wk_x[i].astype(BF16), wv_x[i].astype(BF16))
        wo = w_out[i].astype(BF16)
        h2 = _mix(h, y_ssm, y_att, wo[:SSM_WIDTH], wo[SSM_WIDTH:], row(g_x[i]), wq_x[i].astype(BF16),
                  kmem, vmem, wo_x[i].astype(BF16))

        wsgu = jnp.concatenate([ws_gate[i], ws_up[i]], axis=1).astype(BF16)
        n_part = n // MOE_PARTS
        out = None
        for part in range(MOE_PARTS):
            t_pk, base, eidx, gw, rank, counts = _route(
                h2.reshape(n, d), row(g_ffn[i]), w_router[i].T.astype(F32),
                router_bias[i].reshape(N_EXPERTS, 1).astype(F32), wsgu, ws_down[i].astype(BF16), part, MOE_PARTS)
            pstart, block_e, n_used, n_slots = _block_schedule(counts[:, 0], n_part)
            dest3 = _plan(pstart, eidx, rank, SC_WINDOW)
            xs = _sc_dispatch(t_pk, dest3, n_slots)
            ys = _experts(block_e, n_used, xs, w_gate[i], w_up[i], w_down[i])
            yg = _sc_combine(ys, dest3)
            out = _final(base, yg, gw.T, row(g_final), i == depth - 1, out, part, MOE_PARTS)
        h = out.reshape(b, l, d)
    return h
```

```python
import functools
import math

import jax
import jax.numpy as jnp
from jax import lax
from jax.experimental import pallas as pl
from jax.experimental.pallas import tpu as pltpu
from jax.experimental.pallas import tpu_sc as plsc

F32 = jnp.float32
BF16 = jnp.bfloat16
I32 = jnp.int32

D_MODEL = 1024
SSM_WIDTH = 512
ATTN_WIDTH = 512
SSM_GROUP = 16
SSM_GROUPS = 32
SSM_STATE = 64
N_STATE = SSM_GROUPS * SSM_STATE
DA_HEADS = 4
DA_VDIM = 128
DA_QKDIM = 64
ROPE_THETA = 10000.0
X_HEADS = 4
X_HEAD_DIM = 256
N_EXPERTS = 64
TOP_K = 8
N_EXPERT_GROUPS = 8
PER_GROUP = N_EXPERTS // N_EXPERT_GROUPS
TOPK_GROUPS = 4
D_EXPERT = 256
ROUTED_SCALE = 2.5
EPS = 1e-6

VMEM_LIMIT_V7X = 56 * 1024 * 1024
SUBLANES = 8

IN_TILE = 1024
SSM_STEPS = 128
SSM_COLS = 1024
SSM_CH = 128
SSM_SLAB = 512
ATT_TILE = 2048
ATT_ROWS = 256
MIX_TILE = 1024
ROUTE_TILE = 1024
MOE_TILE = 512
X_SLOTS = 3
PLAN_TILE = 2048
MOE_PARTS = 2
FIN_TILE = 256

SC_CORES = 2
SC_WORKERS = 32
SC_WINDOW = 64


def _params(sem):
    return pltpu.CompilerParams(dimension_semantics=sem, vmem_limit_bytes=VMEM_LIMIT_V7X)


def _rms(x, g):
    return x * lax.rsqrt(jnp.mean(x * x, axis=-1, keepdims=True) + EPS) * g


def _full(shape):
    return pl.BlockSpec(shape, lambda *_: (0,) * len(shape))


def _pack_rows(a):
    w = a.shape[1] // 2
    bits = lambda v: lax.bitcast_convert_type(v.astype(BF16).astype(F32), I32)
    return (bits(a[:, w:]) & jnp.int32(-65536)) | lax.shift_right_logical(bits(a[:, :w]), 16)


def _unpack_rows(p):
    lo = lax.bitcast_convert_type(lax.shift_left(p, 16), F32)
    hi = lax.bitcast_convert_type(p & jnp.int32(-65536), F32)
    return jnp.concatenate([lo, hi], axis=1)


def _inproj_kernel(x_ref, pos_ref, g_ref, w_ref, inv_ref, u_ref, q_ref, k_ref, v_ref):
    x = x_ref[0]
    a = _rms(x, g_ref[...]).astype(BF16)
    z = jnp.dot(a, w_ref[...], preferred_element_type=F32)
    u_ref[0] = z[:, :SSM_WIDTH]
    ang = pos_ref[0].astype(F32) * inv_ref[...]
    cos = jnp.concatenate([jnp.cos(ang)] * 4, axis=1)
    sin = jnp.concatenate([jnp.sin(ang)] * 4, axis=1)
    lane = lax.broadcasted_iota(I32, cos.shape, 1)
    first = (lane & (DA_QKDIM - 1)) < DA_QKDIM // 2
    half = DA_QKDIM // 2

    def rope(t):
        rot = jnp.where(first, -pltpu.roll(t, ATTN_WIDTH - half, 1), pltpu.roll(t, half, 1))
        return t * cos + rot * sin

    q = z[:, SSM_WIDTH:SSM_WIDTH + ATTN_WIDTH]
    k = z[:, SSM_WIDTH + ATTN_WIDTH:SSM_WIDTH + 2 * ATTN_WIDTH]
    q_ref[0] = (rope(q) * (DA_QKDIM ** -0.5)).astype(BF16)
    k_ref[0] = rope(k).astype(BF16)
    v_ref[0] = z[:, SSM_WIDTH + 2 * ATTN_WIDTH:].astype(BF16)


def _inproj(x, pos3, g_mix, w_in, inv_row):
    b, l, d = x.shape
    t = min(IN_TILE, l)
    n_out = w_in.shape[1]
    row = lambda w: pl.BlockSpec((1, t, w), lambda i, j: (i, j, 0))
    return pl.pallas_call(
        _inproj_kernel,
        grid=(b, l // t),
        in_specs=[row(d), row(1), _full((1, d)), _full((d, n_out)), _full((1, 128))],
        out_specs=[row(SSM_WIDTH), row(ATTN_WIDTH), row(ATTN_WIDTH), row(ATTN_WIDTH)],
        out_shape=[jax.ShapeDtypeStruct((b, l, SSM_WIDTH), F32)]
        + [jax.ShapeDtypeStruct((b, l, ATTN_WIDTH), BF16)] * 3,
        compiler_params=_params(("parallel", "parallel")),
        name="inproj",
    )(x, pos3, g_mix, w_in, inv_row)


def _ssm_kernel(u_ref, bm_ref, are_ref, aim_ref, cm_ref, dskip_ref, wglu_ref, g_ref, o_ref, bu_ref, st_ref,
                *, steps):
    @pl.when(pl.program_id(0) == 0)
    def _():
        st_ref[...] = jnp.zeros_like(st_ref)

    u = pltpu.einshape("bts->tbs", u_ref[...]).reshape(steps * SUBLANES, SSM_WIDTH)
    ub = u.astype(BF16)

    n_slabs = SSM_WIDTH // SSM_CH
    for s in range(n_slabs):
        ch = slice(s * SSM_CH, (s + 1) * SSM_CH)
        for part in (0, N_STATE):
            cols = slice(part + s * SSM_SLAB, part + (s + 1) * SSM_SLAB)
            bu_ref[:, cols] = jnp.dot(ub[:, ch], bm_ref[ch, cols], preferred_element_type=F32)

    for c0 in range(0, N_STATE, SSM_COLS):
        re = slice(c0, c0 + SSM_COLS)
        im = slice(N_STATE + c0, N_STATE + c0 + SSM_COLS)
        ar = are_ref[:, re]
        ai = aim_ref[:, re]

        def step(t, carry, re=re, im=im, ar=ar, ai=ai):
            sr, si = carry
            rows = pl.ds(pl.multiple_of(t * SUBLANES, SUBLANES), SUBLANES)
            nr = ar * sr - ai * si + bu_ref[rows, re]
            ni = ar * si + ai * sr + bu_ref[rows, im]
            bu_ref[rows, re] = nr
            bu_ref[rows, im] = ni
            return nr, ni

        sr, si = lax.fori_loop(0, steps, step, (st_ref[:, re], st_ref[:, im]))
        st_ref[:, re] = sr
        st_ref[:, im] = si

    ys = []
    for s in range(n_slabs):
        ch = slice(s * SSM_CH, (s + 1) * SSM_CH)
        acc = None
        for part in (0, N_STATE):
            cols = slice(part + s * SSM_SLAB, part + (s + 1) * SSM_SLAB)
            d = jnp.dot(bu_ref[:, cols].astype(BF16), cm_ref[cols, ch], preferred_element_type=F32)
            acc = d if acc is None else acc + d
        ys.append(acc)
    y = jnp.concatenate(ys, axis=1) + dskip_ref[...] * u
    y = jax.nn.gelu(y)
    y = y * jax.nn.sigmoid(jnp.dot(y.astype(BF16), wglu_ref[...], preferred_element_type=F32))
    out = _rms(y, g_ref[...]).reshape(steps, SUBLANES, SSM_WIDTH)
    o_ref[...] = pltpu.einshape("tbs->bts", out).astype(BF16)


def _ssm(u, bmat, a_re8, a_im8, cmat, d_skip, w_glu, g_out):
    batch, l, _ = u.shape
    assert batch == SUBLANES, "one time step of all sequences must fill the sublanes"
    steps = min(SSM_STEPS, l)
    rows = steps * batch
    seq = pl.BlockSpec((batch, steps, SSM_WIDTH), lambda i: (0, i, 0))
    return pl.pallas_call(
        functools.partial(_ssm_kernel, steps=steps),
        grid=(l // steps,),
        in_specs=[seq,
                  _full((SSM_WIDTH, 2 * N_STATE)), _full((SUBLANES, N_STATE)), _full((SUBLANES, N_STATE)),
                  _full((2 * N_STATE, SSM_WIDTH)), _full((1, SSM_WIDTH)), _full((SSM_WIDTH, SSM_WIDTH)),
                  _full((1, SSM_WIDTH))],
        out_specs=seq,
        out_shape=jax.ShapeDtypeStruct((batch, l, SSM_WIDTH), BF16),
        scratch_shapes=[pltpu.VMEM((rows, 2 * N_STATE), F32), pltpu.VMEM((SUBLANES, 2 * N_STATE), F32)],
        compiler_params=_params(("arbitrary",)),
        name="ssm",
    )(u, bmat, a_re8, a_im8, cmat, d_skip, w_glu, g_out)


def _attn_kernel(qi_ref, ki_ref, lam_ref, q_ref, k_ref, v_ref, g_ref, o_ref, qs_ref, m_ref, acc_ref, *,
                 tile, out_scale):
    qi = qi_ref[pl.program_id(2)]
    ki = ki_ref[pl.program_id(2)]

    @pl.when(ki == 0)
    def _():
        q = q_ref[0]
        lane = lax.broadcasted_iota(I32, q.shape, 1)
        zero = jnp.zeros_like(q)
        qs_ref[:tile] = jnp.where(lane < DA_QKDIM, q, zero)
        qs_ref[tile:] = jnp.where(lane >= DA_QKDIM, q, zero)
        m_ref[...] = jnp.full_like(m_ref, -jnp.inf)
        acc_ref[...] = jnp.zeros_like(acc_ref)

    def update(masked):
        v_ext = jnp.concatenate([v_ref[0], jnp.ones((tile, DA_VDIM), BF16)], axis=1)
        rb = min(ATT_ROWS, tile)
        n_rb = 2 * tile // rb

        def n_keys(r):
            return (r * rb) % tile + rb if masked else tile

        def scores(r):
            return lax.dot_general(qs_ref[r * rb:(r + 1) * rb], k_ref[0, :n_keys(r)], (((1,), (1,)), ((), ())),
                                   preferred_element_type=F32)

        s_next = scores(0)
        for r in range(n_rb):
            s = s_next
            if r + 1 < n_rb:
                s_next = scores(r + 1)
            rows = slice(r * rb, (r + 1) * rb)
            nk = n_keys(r)
            if masked:
                qpos = lax.broadcasted_iota(I32, s.shape, 0) + (r * rb) % tile
                s = jnp.where(lax.broadcasted_iota(I32, s.shape, 1) <= qpos, s, jnp.finfo(F32).min)
            m_old = m_ref[rows]
            m_new = jnp.maximum(m_old, jnp.max(s, axis=1, keepdims=True))
            p = jnp.exp(s - jnp.concatenate([m_new] * (nk // DA_VDIM), axis=1))
            alpha = jnp.exp(m_old - m_new)
            acc_ref[rows] = (jnp.concatenate([alpha, alpha], axis=1) * acc_ref[rows]
                             + jnp.dot(p.astype(BF16), v_ext[:nk], preferred_element_type=F32))
            m_ref[rows] = m_new

    @pl.when(ki < qi)
    def _():
        update(False)

    @pl.when(ki == qi)
    def _():
        update(True)
        o = acc_ref[:, :DA_VDIM] / acc_ref[:, DA_VDIM:]
        o = o[:tile] - lam_ref[0] * o[tile:]
        o_ref[0] = (_rms(o, g_ref[...]) * out_scale).astype(BF16)


def _diff_attention(q, k, v, lam, g_sub, lambda_init):
    b, l, _ = q.shape
    t = min(ATT_TILE, l)
    n = l // t
    pairs = [(qi, ki) for qi in range(n) for ki in range(qi + 1)]
    qi_tab = jnp.asarray([p[0] for p in pairs], I32)
    ki_tab = jnp.asarray([p[1] for p in pairs], I32)
    qspec = pl.BlockSpec((1, t, DA_VDIM), lambda bi, h, s, qt, kt: (bi, qt[s], h))
    kspec = pl.BlockSpec((1, t, DA_VDIM), lambda bi, h, s, qt, kt: (bi, kt[s], h))
    return pl.pallas_call(
        functools.partial(_attn_kernel, tile=t, out_scale=1.0 - lambda_init),
        grid_spec=pltpu.PrefetchScalarGridSpec(
            num_scalar_prefetch=2,
            grid=(b, DA_HEADS, len(pairs)),
            in_specs=[pl.BlockSpec(memory_space=pltpu.SMEM), qspec, kspec, kspec,
                      pl.BlockSpec((1, DA_VDIM), lambda bi, h, s, qt, kt: (0, 0))],
            out_specs=qspec,
            scratch_shapes=[pltpu.VMEM((2 * t, DA_VDIM), BF16), pltpu.VMEM((2 * t, DA_VDIM), F32),
                            pltpu.VMEM((2 * t, 2 * DA_VDIM), F32)]),
        out_shape=jax.ShapeDtypeStruct((b, l, ATTN_WIDTH), BF16),
        compiler_params=_params(("parallel", "parallel", "arbitrary")),
        name="diffattn",
    )(qi_tab, ki_tab, lam, q, k, v, g_sub)


def _memkv_kernel(m_ref, g_ref, wk_ref, wv_ref, k_ref, v_ref):
    a = _rms(m_ref[0], g_ref[...]).astype(BF16)
    k_ref[0] = jnp.dot(a, wk_ref[...], preferred_element_type=F32).astype(BF16)
    v_ref[0] = jnp.dot(a, wv_ref[...], preferred_element_type=F32).astype(BF16)


def _memkv(mem, g_mem, wk, wv):
    b, m, d = mem.shape
    blk = pl.BlockSpec((1, m, d), lambda i: (i, 0, 0))
    return pl.pallas_call(
        _memkv_kernel,
        grid=(b,),
        in_specs=[blk, _full((1, d)), _full((d, d)), _full((d, d))],
        out_specs=[blk, blk],
        out_shape=[jax.ShapeDtypeStruct((b, m, d), BF16)] * 2,
        compiler_params=_params(("parallel",)),
        name="memkv",
    )(mem, g_mem, wk, wv)


def _mix_kernel(x_ref, ys_ref, ya_ref, wo1_ref, wo2_ref, gx_ref, wq_ref, km_ref, vm_ref, wo_ref, h_ref):
    h = (x_ref[0]
         + jnp.dot(ys_ref[0], wo1_ref[...], preferred_element_type=F32)
         + jnp.dot(ya_ref[0], wo2_ref[...], preferred_element_type=F32))
    hq = _rms(h, gx_ref[...]).astype(BF16)
    q = jnp.dot(hq, wq_ref[...], preferred_element_type=F32).astype(BF16)
    outs = []
    for hd in range(X_HEADS):
        cols = slice(hd * X_HEAD_DIM, (hd + 1) * X_HEAD_DIM)
        s = lax.dot_general(q[:, cols], km_ref[0, :, cols], (((1,), (1,)), ((), ())),
                            preferred_element_type=F32) * (X_HEAD_DIM ** -0.5)
        s = s - jnp.max(s, axis=1, keepdims=True)
        p = jnp.exp(s)
        p = p / jnp.sum(p, axis=1, keepdims=True)
        outs.append(jnp.dot(p.astype(BF16), vm_ref[0, :, cols], preferred_element_type=F32).astype(BF16))
    o = jnp.concatenate(outs, axis=1)
    h_ref[0] = h + jnp.dot(o, wo_ref[...], preferred_element_type=F32)


def _mix(x, y_ssm, y_att, wo1, wo2, g_x, wq, kmem, vmem, wo):
    b, l, d = x.shape
    t = min(MIX_TILE, l)
    m = kmem.shape[1]
    row = lambda w: pl.BlockSpec((1, t, w), lambda i, j: (i, j, 0))
    mem = pl.BlockSpec((1, m, d), lambda i, j: (i, 0, 0))
    return pl.pallas_call(
        _mix_kernel,
        grid=(b, l // t),
        in_specs=[row(d), row(SSM_WIDTH), row(ATTN_WIDTH), _full((SSM_WIDTH, d)), _full((ATTN_WIDTH, d)),
                  _full((1, d)), _full((d, d)), mem, mem, _full((d, d))],
        out_specs=row(d),
        out_shape=jax.ShapeDtypeStruct((b, l, d), F32),
        compiler_params=_params(("parallel", "parallel")),
        name="mix",
    )(x, y_ssm, y_att, wo1, wo2, g_x, wq, kmem, vmem, wo)


def _first_index(hit, idx, sentinel):
    return jnp.min(jnp.where(hit, idx, sentinel), axis=0, keepdims=True)


def _route_kernel(h_ref, g_ref, wr_ref, bias_ref, wsgu_ref, wsd_ref, tri_ref,
                  t_ref, base_ref, eidx_ref, gw_ref, rank_ref, cnt_ref, carry_ref):
    @pl.when(pl.program_id(0) == 0)
    def _():
        carry_ref[...] = jnp.zeros_like(carry_ref)

    h = h_ref[...]
    t = _rms(h, g_ref[...])
    tb = t.astype(BF16)
    t_ref[...] = _pack_rows(t)
    gu = jnp.dot(tb, wsgu_ref[...], preferred_element_type=F32)
    hid = jax.nn.silu(gu[:, :D_EXPERT]) * gu[:, D_EXPERT:]
    base_ref[...] = h + jnp.dot(hid.astype(BF16), wsd_ref[...], preferred_element_type=F32)

    logits = lax.dot_general(wr_ref[...], t, (((1,), (1,)), ((), ())), preferred_element_type=F32,
                             precision=lax.Precision.HIGHEST)
    scores = jax.nn.sigmoid(logits)
    biased = scores + bias_ref[...]
    n_tok = scores.shape[1]
    neg = -jnp.inf
    sub = lax.broadcasted_iota(I32, (PER_GROUP, n_tok), 0)

    gs = []
    for g in range(N_EXPERT_GROUPS):
        blk = biased[g * PER_GROUP:(g + 1) * PER_GROUP]
        m1 = jnp.max(blk, axis=0, keepdims=True)
        i1 = _first_index(blk == m1, sub, PER_GROUP)
        m2 = jnp.max(jnp.where(sub == i1, neg, blk), axis=0, keepdims=True)
        gs.append(m1 + m2)
    gs = jnp.concatenate(gs, axis=0)

    gsel = jnp.zeros(gs.shape, jnp.bool_)
    for _ in range(TOPK_GROUPS):
        m = jnp.max(gs, axis=0, keepdims=True)
        hit = sub == _first_index(gs == m, sub, N_EXPERT_GROUPS)
        gsel = jnp.logical_or(gsel, hit)
        gs = jnp.where(hit, neg, gs)

    masked = jnp.concatenate(
        [jnp.where(gsel[g:g + 1], biased[g * PER_GROUP:(g + 1) * PER_GROUP], neg)
         for g in range(N_EXPERT_GROUPS)], axis=0)
    eid = lax.broadcasted_iota(I32, masked.shape, 0)
    sel = jnp.zeros(masked.shape, jnp.bool_)
    idxs, gws = [], []
    for _ in range(TOP_K):
        m = jnp.max(masked, axis=0, keepdims=True)
        i = _first_index(masked == m, eid, N_EXPERTS)
        hit = eid == i
        idxs.append(i)
        gws.append(jnp.sum(jnp.where(hit, scores, 0.0), axis=0, keepdims=True))
        sel = jnp.logical_or(sel, hit)
        masked = jnp.where(hit, neg, masked)
    eidx = jnp.concatenate(idxs, axis=0)
    gw = jnp.concatenate(gws, axis=0)
    gw = gw / jnp.sum(gw, axis=0, keepdims=True) * ROUTED_SCALE
    eidx_ref[...] = eidx
    gw_ref[...] = gw

    before = jnp.dot(sel.astype(BF16), tri_ref[...], preferred_element_type=F32) + carry_ref[...]
    rank_ref[...] = jnp.concatenate(
        [jnp.sum(jnp.where(eid == idxs[k], before, 0.0), axis=0, keepdims=True) for k in range(TOP_K)],
        axis=0).astype(I32)
    carry = carry_ref[...] + jnp.sum(sel.astype(F32), axis=1, keepdims=True)
    carry_ref[...] = carry
    cnt_ref[...] = carry.astype(I32)


def _route(h2, g_ffn, wr_t, bias_col, wsgu, wsd, part, n_parts):
    n_all, d = h2.shape
    n = n_all // n_parts
    t = min(ROUTE_TILE, n)
    blk0 = part * (n // t)
    tri = (lax.broadcasted_iota(I32, (t, t), 0) < lax.broadcasted_iota(I32, (t, t), 1)).astype(BF16)
    row = lambda w: pl.BlockSpec((t, w), lambda i: (i, 0))
    col = pl.BlockSpec((TOP_K, t), lambda i: (0, i))
    return pl.pallas_call(
        _route_kernel,
        grid=(n // t,),
        in_specs=[pl.BlockSpec((t, d), lambda i: (i + blk0, 0)), _full((1, d)), _full((N_EXPERTS, d)),
                  _full((N_EXPERTS, 1)),
                  _full((d, 2 * D_EXPERT)), _full((D_EXPERT, d)), _full((t, t))],
        out_specs=[row(d // 2), row(d), col, col, col, _full((N_EXPERTS, 1))],
        out_shape=[jax.ShapeDtypeStruct((n, d // 2), I32), jax.ShapeDtypeStruct((n, d), F32),
                   jax.ShapeDtypeStruct((TOP_K, n), I32), jax.ShapeDtypeStruct((TOP_K, n), F32),
                   jax.ShapeDtypeStruct((TOP_K, n), I32), jax.ShapeDtypeStruct((N_EXPERTS, 1), I32)],
        scratch_shapes=[pltpu.VMEM((N_EXPERTS, 1), F32)],
        compiler_params=_params(("arbitrary",)),
        name="route",
    )(h2, g_ffn, wr_t, bias_col, wsgu, wsd, tri)


def _expert_kernel(be_ref, nu_ref, x_hbm, wg_ref, wu_ref, wd_ref, y_ref, wg_s, wu_s, wd_s, x_buf, x_sem):
    i = pl.program_id(0)
    n_used = nu_ref[0]

    def fetch(step):
        slot = step % X_SLOTS
        rows = pl.ds(pl.multiple_of(step * MOE_TILE, MOE_TILE), MOE_TILE)
        return pltpu.make_async_copy(x_hbm.at[rows], x_buf.at[slot], x_sem.at[slot])

    @pl.when(i == 0)
    def _():
        for j in range(X_SLOTS - 1):
            @pl.when(j < n_used)
            def _(j=j):
                fetch(j).start()

    @pl.when(i + (X_SLOTS - 1) < n_used)
    def _():
        fetch(i + (X_SLOTS - 1)).start()

    @pl.when(jnp.logical_or(i == 0, be_ref[i] != be_ref[jnp.maximum(i - 1, 0)]))
    def _():
        wg_s[...] = wg_ref[0].astype(BF16)
        wu_s[...] = wu_ref[0].astype(BF16)
        wd_s[...] = wd_ref[0].astype(BF16)

    @pl.when(i < n_used)
    def _():
        fetch(i).wait()
        x = _unpack_rows(x_buf[i % X_SLOTS]).astype(BF16)
        gate = jnp.dot(x, wg_s[...], preferred_element_type=F32)
        up = jnp.dot(x, wu_s[...], preferred_element_type=F32)
        hid = (jax.nn.silu(gate) * up).astype(BF16)
        y_ref[...] = _pack_rows(jnp.dot(hid, wd_s[...], preferred_element_type=F32))


def _experts(block_e, n_used, xs, w_gate, w_up, w_down):
    n_slots, dw = xs.shape
    _, d, de = w_gate.shape
    nb = n_slots // MOE_TILE
    w_in = pl.BlockSpec((1, d, de), lambda i, be, nu: (be[i], 0, 0))
    return pl.pallas_call(
        _expert_kernel,
        grid_spec=pltpu.PrefetchScalarGridSpec(
            num_scalar_prefetch=2,
            grid=(nb,),
            in_specs=[pl.BlockSpec(memory_space=pl.ANY), w_in, w_in,
                      pl.BlockSpec((1, de, d), lambda i, be, nu: (be[i], 0, 0))],
            out_specs=pl.BlockSpec((MOE_TILE, dw), lambda i, be, nu: (jnp.minimum(i, nu[0] - 1), 0)),
            scratch_shapes=[pltpu.VMEM((d, de), BF16), pltpu.VMEM((d, de), BF16), pltpu.VMEM((de, d), BF16),
                            pltpu.VMEM((X_SLOTS, MOE_TILE, dw), I32), pltpu.SemaphoreType.DMA((X_SLOTS,))]),
        out_shape=jax.ShapeDtypeStruct((n_slots, dw), I32),
        compiler_params=_params(("arbitrary",)),
        name="experts",
    )(block_e, n_used, xs, w_gate, w_up, w_down)


def _sc_worker_id():
    return lax.axis_index("s") * SC_CORES + lax.axis_index("c")


def _sc_dispatch(t_rows, dest3, n_slots):
    _, dw = t_rows.shape
    n_chunks, _, w = dest3.shape
    per_worker = n_chunks // SC_WORKERS
    assert per_worker % 2 == 0
    mesh = plsc.VectorSubcoreMesh(core_axis_name="c", subcore_axis_name="s")
    dt = t_rows.dtype

    @functools.partial(
        pl.kernel, mesh=mesh,
        out_type=jax.ShapeDtypeStruct((n_slots, dw), dt),
        scratch_types=[pltpu.VMEM((TOP_K, w), I32), pltpu.VMEM((TOP_K, w), I32),
                       pltpu.VMEM((w, dw), dt), pltpu.VMEM((w, dw), dt),
                       pltpu.SemaphoreType.DMA, pltpu.SemaphoreType.DMA, pltpu.SemaphoreType.DMA],
    )
    def scatter_rows(t_hbm, dest_hbm, xs_hbm, idx_a, idx_b, rows_a, rows_b, sem_load, sem_a, sem_b):
        wid = _sc_worker_id()

        def scatter(idx_v, rows_v, sem):
            return [pltpu.async_copy(rows_v, xs_hbm.at[idx_v.at[k]], sem) for k in range(TOP_K)]

        @pl.loop(0, per_worker, step=2)
        def _(j):
            ca = wid * per_worker + j
            cb = ca + 1
            pltpu.sync_copy(dest_hbm.at[ca], idx_a)
            pltpu.sync_copy(t_hbm.at[pl.ds(ca * w, w)], rows_a)
            load_idx = pltpu.async_copy(dest_hbm.at[cb], idx_b, sem_load)
            load_rows = pltpu.async_copy(t_hbm.at[pl.ds(cb * w, w)], rows_b, sem_load)
            out_a = scatter(idx_a, rows_a, sem_a)
            load_idx.wait()
            load_rows.wait()
            out_b = scatter(idx_b, rows_b, sem_b)
            for cp in out_a + out_b:
                cp.wait()

    return scatter_rows(t_rows, dest3)


def _sc_combine(y_rows, dest3):
    _, dw = y_rows.shape
    n_chunks, _, w = dest3.shape
    per_worker = n_chunks // SC_WORKERS
    mesh = plsc.VectorSubcoreMesh(core_axis_name="c", subcore_axis_name="s")
    dt = y_rows.dtype

    @functools.partial(
        pl.kernel, mesh=mesh,
        out_type=jax.ShapeDtypeStruct((TOP_K, n_chunks * w, dw), dt),
        scratch_types=[pltpu.VMEM((TOP_K, w), I32), pltpu.VMEM((w, dw), dt), pltpu.VMEM((w, dw), dt),
                       pltpu.SemaphoreType.DMA, pltpu.SemaphoreType.DMA,
                       pltpu.SemaphoreType.DMA, pltpu.SemaphoreType.DMA],
    )
    def gather_rows(y_hbm, dest_hbm, out_hbm, idx_v, buf0, buf1, gsem0, gsem1, wsem0, wsem1):
        wid = _sc_worker_id()
        bufs, gsems, wsems = (buf0, buf1), (gsem0, gsem1), (wsem0, wsem1)

        @pl.loop(0, per_worker)
        def _(j):
            c = wid * per_worker + j
            pltpu.sync_copy(dest_hbm.at[c], idx_v)
            gathers = [None] * TOP_K
            writes = [None] * TOP_K
            gathers[0] = pltpu.async_copy(y_hbm.at[idx_v.at[0]], bufs[0], gsems[0])
            for k in range(TOP_K):
                b = k % 2
                if k + 1 < TOP_K:
                    if k >= 1:
                        writes[k - 1].wait()
                    gathers[k + 1] = pltpu.async_copy(y_hbm.at[idx_v.at[k + 1]], bufs[1 - b], gsems[1 - b])
                gathers[k].wait()
                writes[k] = pltpu.async_copy(bufs[b], out_hbm.at[k, pl.ds(c * w, w)], wsems[b])
            writes[TOP_K - 2].wait()
            writes[TOP_K - 1].wait()

    return gather_rows(y_rows, dest3)


def _final_kernel(base_ref, y_ref, gw_ref, g_ref, *rest, normalize):
    o_ref = rest[-1]
    h = base_ref[...]
    gw = gw_ref[...]
    for k in range(TOP_K):
        h = h + gw[:, k:k + 1] * _unpack_rows(y_ref[k])
    o_ref[...] = _rms(h, g_ref[...]) if normalize else h


def _final(base, yg, gw_rows, g_final, normalize, out_prev, part, n_parts):
    n, d = base.shape
    t = min(FIN_TILE, n)
    blk0 = part * (n // t)
    in_specs = [pl.BlockSpec((t, d), lambda i: (i, 0)), pl.BlockSpec((TOP_K, t, d // 2), lambda i: (0, i, 0)),
                pl.BlockSpec((t, TOP_K), lambda i: (i, 0)), _full((1, d))]
    args = [base, yg, gw_rows, g_final]
    aliases = {}
    if out_prev is not None:
        in_specs.append(pl.BlockSpec(memory_space=pl.ANY))
        args.append(out_prev)
        aliases = {len(args) - 1: 0}
    return pl.pallas_call(
        functools.partial(_final_kernel, normalize=normalize),
        grid=(n // t,),
        in_specs=in_specs,
        out_specs=pl.BlockSpec((t, d), lambda i: (i + blk0, 0)),
        out_shape=jax.ShapeDtypeStruct((n * n_parts, d), F32),
        input_output_aliases=aliases,
        compiler_params=_params(("parallel",)),
        name="final",
    )(*args)


def _ssm_matrices(a_re, a_im, log_dt, b_re, b_im, c_re, c_im):
    lr, li = a_re.astype(F32), a_im.astype(F32)
    dt = jnp.exp(log_dt.astype(F32))[:, None]
    mag = jnp.exp(lr * dt)
    ab_re, ab_im = mag * jnp.cos(li * dt), mag * jnp.sin(li * dt)
    den = lr * lr + li * li
    zr, zi = ab_re - 1.0, ab_im
    k_re = (zr * lr + zi * li) / den
    k_im = (zi * lr - zr * li) / den
    br, bi = b_re.astype(F32), b_im.astype(F32)
    bb_re = k_re[..., None] * br - k_im[..., None] * bi
    bb_im = k_re[..., None] * bi + k_im[..., None] * br
    eye = jnp.eye(SSM_GROUPS, dtype=F32)

    def in_mat(bb):
        return jnp.einsum("gpc,gh->gchp", bb, eye).reshape(SSM_WIDTH, N_STATE)

    def out_mat(c):
        return jnp.einsum("gcp,gh->gphc", c.astype(F32), eye).reshape(N_STATE, SSM_WIDTH)

    bmat = jnp.concatenate([in_mat(bb_re), in_mat(bb_im)], axis=1)
    cmat = jnp.concatenate([out_mat(c_re), -out_mat(c_im)], axis=0)
    tile = lambda a: jnp.broadcast_to(a.reshape(1, N_STATE), (SUBLANES, N_STATE))
    return bmat.astype(BF16), tile(ab_re), tile(ab_im), cmat.astype(BF16)


def _plan_kernel(pstart_ref, eidx_ref, rank_ref, dest_ref, *, window):
    eidx = eidx_ref[...]
    dest = rank_ref[...]
    for e in range(N_EXPERTS):
        dest = dest + jnp.where(eidx == e, pstart_ref[e], 0)
    for c in range(dest.shape[1] // window):
        dest_ref[c] = dest[:, c * window:(c + 1) * window]


def _plan(pstart, eidx, rank, window):
    k, n = eidx.shape
    t = min(PLAN_TILE, n)
    cols = pl.BlockSpec((k, t), lambda i, ps: (0, i))
    return pl.pallas_call(
        functools.partial(_plan_kernel, window=window),
        grid_spec=pltpu.PrefetchScalarGridSpec(
            num_scalar_prefetch=1,
            grid=(n // t,),
            in_specs=[cols, cols],
            out_specs=pl.BlockSpec((t // window, k, window), lambda i, ps: (i, 0, 0))),
        out_shape=jax.ShapeDtypeStruct((n // window, k, window), I32),
        compiler_params=_params(("parallel",)),
        name="plan",
    )(pstart, eidx, rank)


def _block_schedule(counts, n_tok):
    padded = ((counts + MOE_TILE - 1) // MOE_TILE) * MOE_TILE
    pend = jnp.cumsum(padded)
    pstart = (pend - padded).astype(I32)
    n_slots = n_tok * TOP_K + N_EXPERTS * MOE_TILE
    nb = n_slots // MOE_TILE
    n_used = (pend[-1] // MOE_TILE).astype(I32)
    blk = jnp.minimum(jnp.arange(nb, dtype=I32), n_used - 1)
    block_e = jnp.sum((pend[None, :] <= (blk * MOE_TILE)[:, None]).astype(I32), axis=1)
    return pstart, jnp.minimum(block_e, N_EXPERTS - 1), n_used.reshape(1), n_slots


def kernel(x, mem, positions, g_mix, w_in, a_re, a_im, log_dt, b_re, b_im, c_re, c_im, d_skip, w_glu, g_ssm_out, lam_q1, lam_k1, lam_q2, lam_k2, g_sub, w_out, g_x, g_mem, wq_x, wk_x, wv_x, wo_x, g_ffn, w_router, router_bias, w_gate, w_up, w_down, ws_gate, ws_up, ws_down, g_final):
    b, l, d = x.shape
    n = b * l
    depth = w_in.shape[0]
    row = lambda a: a.reshape(1, -1).astype(F32)
    inv = ROPE_THETA ** (-jnp.arange(0, DA_QKDIM, 2, dtype=F32) / DA_QKDIM)
    inv_row = jnp.tile(inv, 128 // inv.shape[0]).reshape(1, 128)
    pos3 = positions.reshape(b, l, 1)

    h = x
    for i in range(depth):
        lambda_init = 0.8 - 0.6 * math.exp(-0.3 * i)
        u, q, k, v = _inproj(h, pos3, row(g_mix[i]), w_in[i].astype(BF16), inv_row)

        bmat, are8, aim8, cmat = _ssm_matrices(a_re[i], a_im[i], log_dt[i], b_re[i], b_im[i], c_re[i], c_im[i])
        y_ssm = _ssm(u, bmat, are8, aim8, cmat, row(d_skip[i]), w_glu[i].astype(BF16), row(g_ssm_out[i]))

        lam = (jnp.exp(jnp.sum(lam_q1[i].astype(F32) * lam_k1[i].astype(F32)))
               - jnp.exp(jnp.sum(lam_q2[i].astype(F32) * lam_k2[i].astype(F32))) + lambda_init).reshape(1)
        y_att = _diff_attention(q, k, v, lam.astype(F32), row(g_sub[i]), lambda_init)

        kmem, vmem = _memkv(mem, row(g_mem[i]), wk_x[i].astype(BF16), wv_x[i].astype(BF16))
        wo = w_out[i].astype(BF16)
        h2 = _mix(h, y_ssm, y_att, wo[:SSM_WIDTH], wo[SSM_WIDTH:], row(g_x[i]), wq_x[i].astype(BF16),
                  kmem, vmem, wo_x[i].astype(BF16))

        wsgu = jnp.concatenate([ws_gate[i], ws_up[i]], axis=1).astype(BF16)
        n_part = n // MOE_PARTS
        out = None
        for part in range(MOE_PARTS):
            t_pk, base, eidx, gw, rank, counts = _route(
                h2.reshape(n, d), row(g_ffn[i]), w_router[i].T.astype(F32),
                router_bias[i].reshape(N_EXPERTS, 1).astype(F32), wsgu, ws_down[i].astype(BF16), part, MOE_PARTS)
            pstart, block_e, n_used, n_slots = _block_schedule(counts[:, 0], n_part)
            dest3 = _plan(pstart, eidx, rank, SC_WINDOW)
            xs = _sc_dispatch(t_pk, dest3, n_slots)
            ys = _experts(block_e, n_used, xs, w_gate[i], w_up[i], w_down[i])
            yg = _sc_combine(ys, dest3)
            out = _final(base, yg, gw.T, row(g_final), i == depth - 1, out, part, MOE_PARTS)
        h = out.reshape(b, l, d)
    return h
```

```python
import functools
import math

import jax
import jax.numpy as jnp
from jax import lax
from jax.experimental import pallas as pl
from jax.experimental.pallas import tpu as pltpu
from jax.experimental.pallas import tpu_sc as plsc

F32 = jnp.float32
BF16 = jnp.bfloat16
I32 = jnp.int32

D_MODEL = 1024
SSM_WIDTH = 512
ATTN_WIDTH = 512
SSM_GROUP = 16
SSM_GROUPS = 32
SSM_STATE = 64
N_STATE = SSM_GROUPS * SSM_STATE
DA_HEADS = 4
DA_VDIM = 128
DA_QKDIM = 64
ROPE_THETA = 10000.0
X_HEADS = 4
X_HEAD_DIM = 256
N_EXPERTS = 64
TOP_K = 8
N_EXPERT_GROUPS = 8
PER_GROUP = N_EXPERTS // N_EXPERT_GROUPS
TOPK_GROUPS = 4
D_EXPERT = 256
ROUTED_SCALE = 2.5
EPS = 1e-6

VMEM_LIMIT_V7X = 56 * 1024 * 1024
SUBLANES = 8

IN_TILE = 1024
SSM_STEPS = 128
SSM_COLS = 1024
SSM_CH = 128
SSM_SLAB = 512
ATT_TILE = 2048
ATT_ROWS = 256
MIX_TILE = 1024
ROUTE_TILE = 1024
MOE_TILE = 512
X_SLOTS = 3
PLAN_TILE = 2048
MOE_PARTS = 2
FIN_TILE = 256

SC_CORES = 2
SC_WORKERS = 32
SC_WINDOW = 64


def _params(sem):
    return pltpu.CompilerParams(dimension_semantics=sem, vmem_limit_bytes=VMEM_LIMIT_V7X)


def _rms(x, g):
    return x * lax.rsqrt(jnp.mean(x * x, axis=-1, keepdims=True) + EPS) * g


def _full(shape):
    return pl.BlockSpec(shape, lambda *_: (0,) * len(shape))


def _pack_rows(a):
    w = a.shape[1] // 2
    bits = lambda v: lax.bitcast_convert_type(v.astype(BF16).astype(F32), I32)
    return (bits(a[:, w:]) & jnp.int32(-65536)) | lax.shift_right_logical(bits(a[:, :w]), 16)


def _unpack_rows(p):
    lo = lax.bitcast_convert_type(lax.shift_left(p, 16), F32)
    hi = lax.bitcast_convert_type(p & jnp.int32(-65536), F32)
    return jnp.concatenate([lo, hi], axis=1)


def _inproj_kernel(x_ref, pos_ref, g_ref, w_ref, inv_ref, u_ref, q_ref, k_ref, v_ref):
    x = x_ref[0]
    a = _rms(x, g_ref[...]).astype(BF16)
    z = jnp.dot(a, w_ref[...], preferred_element_type=F32)
    u_ref[0] = z[:, :SSM_WIDTH]
    ang = pos_ref[0].astype(F32) * inv_ref[...]
    cos = jnp.concatenate([jnp.cos(ang)] * 4, axis=1)
    sin = jnp.concatenate([jnp.sin(ang)] * 4, axis=1)
    lane = lax.broadcasted_iota(I32, cos.shape, 1)
    first = (lane & (DA_QKDIM - 1)) < DA_QKDIM // 2
    half = DA_QKDIM // 2

    def rope(t):
        rot = jnp.where(first, -pltpu.roll(t, ATTN_WIDTH - half, 1), pltpu.roll(t, half, 1))
        return t * cos + rot * sin

    q = z[:, SSM_WIDTH:SSM_WIDTH + ATTN_WIDTH]
    k = z[:, SSM_WIDTH + ATTN_WIDTH:SSM_WIDTH + 2 * ATTN_WIDTH]
    q_ref[0] = (rope(q) * (DA_QKDIM ** -0.5)).astype(BF16)
    k_ref[0] = rope(k).astype(BF16)
    v_ref[0] = z[:, SSM_WIDTH + 2 * ATTN_WIDTH:].astype(BF16)


def _inproj(x, pos3, g_mix, w_in, inv_row):
    b, l, d = x.shape
    t = min(IN_TILE, l)
    n_out = w_in.shape[1]
    row = lambda w: pl.BlockSpec((1, t, w), lambda i, j: (i, j, 0))
    return pl.pallas_call(
        _inproj_kernel,
        grid=(b, l // t),
        in_specs=[row(d), row(1), _full((1, d)), _full((d, n_out)), _full((1, 128))],
        out_specs=[row(SSM_WIDTH), row(ATTN_WIDTH), row(ATTN_WIDTH), row(ATTN_WIDTH)],
        out_shape=[jax.ShapeDtypeStruct((b, l, SSM_WIDTH), F32)]
        + [jax.ShapeDtypeStruct((b, l, ATTN_WIDTH), BF16)] * 3,
        compiler_params=_params(("parallel", "parallel")),
        name="inproj",
    )(x, pos3, g_mix, w_in, inv_row)


def _ssm_kernel(u_ref, bm_ref, are_ref, aim_ref, cm_ref, dskip_ref, wglu_ref, g_ref, o_ref, bu_ref, st_ref,
                *, steps):
    @pl.when(pl.program_id(0) == 0)
    def _():
        st_ref[...] = jnp.zeros_like(st_ref)

    u = pltpu.einshape("bts->tbs", u_ref[...]).reshape(steps * SUBLANES, SSM_WIDTH)
    ub = u.astype(BF16)

    n_slabs = SSM_WIDTH // SSM_CH
    for s in range(n_slabs):
        ch = slice(s * SSM_CH, (s + 1) * SSM_CH)
        for part in (0, N_STATE):
            cols = slice(part + s * SSM_SLAB, part + (s + 1) * SSM_SLAB)
            bu_ref[:, cols] = jnp.dot(ub[:, ch], bm_ref[ch, cols], preferred_element_type=F32)

    for c0 in range(0, N_STATE, SSM_COLS):
        re = slice(c0, c0 + SSM_COLS)
        im = slice(N_STATE + c0, N_STATE + c0 + SSM_COLS)
        ar = are_ref[:, re]
        ai = aim_ref[:, re]

        def step(t, carry, re=re, im=im, ar=ar, ai=ai):
            sr, si = carry
            rows = pl.ds(pl.multiple_of(t * SUBLANES, SUBLANES), SUBLANES)
            nr = ar * sr - ai * si + bu_ref[rows, re]
            ni = ar * si + ai * sr + bu_ref[rows, im]
            bu_ref[rows, re] = nr
            bu_ref[rows, im] = ni
            return nr, ni

        sr, si = lax.fori_loop(0, steps, step, (st_ref[:, re], st_ref[:, im]))
        st_ref[:, re] = sr
        st_ref[:, im] = si

    ys = []
    for s in range(n_slabs):
        ch = slice(s * SSM_CH, (s + 1) * SSM_CH)
        acc = None
        for part in (0, N_STATE):
            cols = slice(part + s * SSM_SLAB, part + (s + 1) * SSM_SLAB)
            d = jnp.dot(bu_ref[:, cols].astype(BF16), cm_ref[cols, ch], preferred_element_type=F32)
            acc = d if acc is None else acc + d
        ys.append(acc)
    y = jnp.concatenate(ys, axis=1) + dskip_ref[...] * u
    y = jax.nn.gelu(y)
    y = y * jax.nn.sigmoid(jnp.dot(y.astype(BF16), wglu_ref[...], preferred_element_type=F32))
    out = _rms(y, g_ref[...]).reshape(steps, SUBLANES, SSM_WIDTH)
    o_ref[...] = pltpu.einshape("tbs->bts", out).astype(BF16)


def _ssm(u, bmat, a_re8, a_im8, cmat, d_skip, w_glu, g_out):
    batch, l, _ = u.shape
    assert batch == SUBLANES, "one time step of all sequences must fill the sublanes"
    steps = min(SSM_STEPS, l)
    rows = steps * batch
    seq = pl.BlockSpec((batch, steps, SSM_WIDTH), lambda i: (0, i, 0))
    return pl.pallas_call(
        functools.partial(_ssm_kernel, steps=steps),
        grid=(l // steps,),
        in_specs=[seq,
                  _full((SSM_WIDTH, 2 * N_STATE)), _full((SUBLANES, N_STATE)), _full((SUBLANES, N_STATE)),
                  _full((2 * N_STATE, SSM_WIDTH)), _full((1, SSM_WIDTH)), _full((SSM_WIDTH, SSM_WIDTH)),
                  _full((1, SSM_WIDTH))],
        out_specs=seq,
        out_shape=jax.ShapeDtypeStruct((batch, l, SSM_WIDTH), BF16),
        scratch_shapes=[pltpu.VMEM((rows, 2 * N_STATE), F32), pltpu.VMEM((SUBLANES, 2 * N_STATE), F32)],
        compiler_params=_params(("arbitrary",)),
        name="ssm",
    )(u, bmat, a_re8, a_im8, cmat, d_skip, w_glu, g_out)


def _attn_kernel(qi_ref, ki_ref, lam_ref, q_ref, k_ref, v_ref, g_ref, o_ref, qs_ref, m_ref, acc_ref, *,
                 tile, out_scale):
    qi = qi_ref[pl.program_id(2)]
    ki = ki_ref[pl.program_id(2)]

    @pl.when(ki == 0)
    def _():
        q = q_ref[0]
        lane = lax.broadcasted_iota(I32, q.shape, 1)
        zero = jnp.zeros_like(q)
        qs_ref[:tile] = jnp.where(lane < DA_QKDIM, q, zero)
        qs_ref[tile:] = jnp.where(lane >= DA_QKDIM, q, zero)
        m_ref[...] = jnp.full_like(m_ref, -jnp.inf)
        acc_ref[...] = jnp.zeros_like(acc_ref)

    def update(masked):
        v_ext = jnp.concatenate([v_ref[0], jnp.ones((tile, DA_VDIM), BF16)], axis=1)
        rb = min(ATT_ROWS, tile)
        n_rb = 2 * tile // rb

        def n_keys(r):
            return (r * rb) % tile + rb if masked else tile

        def scores(r):
            return lax.dot_general(qs_ref[r * rb:(r + 1) * rb], k_ref[0, :n_keys(r)], (((1,), (1,)), ((), ())),
                                   preferred_element_type=F32)

        s_next = scores(0)
        for r in range(n_rb):
            s = s_next
            if r + 1 < n_rb:
                s_next = scores(r + 1)
            rows = slice(r * rb, (r + 1) * rb)
            nk = n_keys(r)
            if masked:
                qpos = lax.broadcasted_iota(I32, s.shape, 0) + (r * rb) % tile
                s = jnp.where(lax.broadcasted_iota(I32, s.shape, 1) <= qpos, s, jnp.finfo(F32).min)
            m_old = m_ref[rows]
            m_new = jnp.maximum(m_old, jnp.max(s, axis=1, keepdims=True))
            p = jnp.exp(s - jnp.concatenate([m_new] * (nk // DA_VDIM), axis=1))
            alpha = jnp.exp(m_old - m_new)
            acc_ref[rows] = (jnp.concatenate([alpha, alpha], axis=1) * acc_ref[rows]
                             + jnp.dot(p.astype(BF16), v_ext[:nk], preferred_element_type=F32))
            m_ref[rows] = m_new

    @pl.when(ki < qi)
    def _():
        update(False)

    @pl.when(ki == qi)
    def _():
        update(True)
        o = acc_ref[:, :DA_VDIM] / acc_ref[:, DA_VDIM:]
        o = o[:tile] - lam_ref[0] * o[tile:]
        o_ref[0] = (_rms(o, g_ref[...]) * out_scale).astype(BF16)


def _diff_attention(q, k, v, lam, g_sub, lambda_init):
    b, l, _ = q.shape
    t = min(ATT_TILE, l)
    n = l // t
    pairs = [(qi, ki) for qi in range(n) for ki in range(qi + 1)]
    qi_tab = jnp.asarray([p[0] for p in pairs], I32)
    ki_tab = jnp.asarray([p[1] for p in pairs], I32)
    qspec = pl.BlockSpec((1, t, DA_VDIM), lambda bi, h, s, qt, kt: (bi, qt[s], h))
    kspec = pl.BlockSpec((1, t, DA_VDIM), lambda bi, h, s, qt, kt: (bi, kt[s], h))
    return pl.pallas_call(
        functools.partial(_attn_kernel, tile=t, out_scale=1.0 - lambda_init),
        grid_spec=pltpu.PrefetchScalarGridSpec(
            num_scalar_prefetch=2,
            grid=(b, DA_HEADS, len(pairs)),
            in_specs=[pl.BlockSpec(memory_space=pltpu.SMEM), qspec, kspec, kspec,
                      pl.BlockSpec((1, DA_VDIM), lambda bi, h, s, qt, kt: (0, 0))],
            out_specs=qspec,
            scratch_shapes=[pltpu.VMEM((2 * t, DA_VDIM), BF16), pltpu.VMEM((2 * t, DA_VDIM), F32),
                            pltpu.VMEM((2 * t, 2 * DA_VDIM), F32)]),
        out_shape=jax.ShapeDtypeStruct((b, l, ATTN_WIDTH), BF16),
        compiler_params=_params(("parallel", "parallel", "arbitrary")),
        name="diffattn",
    )(qi_tab, ki_tab, lam, q, k, v, g_sub)


def _memkv_kernel(m_ref, g_ref, wk_ref, wv_ref, k_ref, v_ref):
    a = _rms(m_ref[0], g_ref[...]).astype(BF16)
    k_ref[0] = jnp.dot(a, wk_ref[...], preferred_element_type=F32).astype(BF16)
    v_ref[0] = jnp.dot(a, wv_ref[...], preferred_element_type=F32).astype(BF16)


def _memkv(mem, g_mem, wk, wv):
    b, m, d = mem.shape
    blk = pl.BlockSpec((1, m, d), lambda i: (i, 0, 0))
    return pl.pallas_call(
        _memkv_kernel,
        grid=(b,),
        in_specs=[blk, _full((1, d)), _full((d, d)), _full((d, d))],
        out_specs=[blk, blk],
        out_shape=[jax.ShapeDtypeStruct((b, m, d), BF16)] * 2,
        compiler_params=_params(("parallel",)),
        name="memkv",
    )(mem, g_mem, wk, wv)


def _mix_kernel(x_ref, ys_ref, ya_ref, wo1_ref, wo2_ref, gx_ref, wq_ref, km_ref, vm_ref, wo_ref, h_ref):
    h = (x_ref[0]
         + jnp.dot(ys_ref[0], wo1_ref[...], preferred_element_type=F32)
         + jnp.dot(ya_ref[0], wo2_ref[...], preferred_element_type=F32))
    hq = _rms(h, gx_ref[...]).astype(BF16)
    q = jnp.dot(hq, wq_ref[...], preferred_element_type=F32).astype(BF16)
    outs = []
    for hd in range(X_HEADS):
        cols = slice(hd * X_HEAD_DIM, (hd + 1) * X_HEAD_DIM)
        s = lax.dot_general(q[:, cols], km_ref[0, :, cols], (((1,), (1,)), ((), ())),
                            preferred_element_type=F32) * (X_HEAD_DIM ** -0.5)
        s = s - jnp.max(s, axis=1, keepdims=True)
        p = jnp.exp(s)
        p = p / jnp.sum(p, axis=1, keepdims=True)
        outs.append(jnp.dot(p.astype(BF16), vm_ref[0, :, cols], preferred_element_type=F32).astype(BF16))
    o = jnp.concatenate(outs, axis=1)
    h_ref[0] = h + jnp.dot(o, wo_ref[...], preferred_element_type=F32)


def _mix(x, y_ssm, y_att, wo1, wo2, g_x, wq, kmem, vmem, wo):
    b, l, d = x.shape
    t = min(MIX_TILE, l)
    m = kmem.shape[1]
    row = lambda w: pl.BlockSpec((1, t, w), lambda i, j: (i, j, 0))
    mem = pl.BlockSpec((1, m, d), lambda i, j: (i, 0, 0))
    return pl.pallas_call(
        _mix_kernel,
        grid=(b, l // t),
        in_specs=[row(d), row(SSM_WIDTH), row(ATTN_WIDTH), _full((SSM_WIDTH, d)), _full((ATTN_WIDTH, d)),
                  _full((1, d)), _full((d, d)), mem, mem, _full((d, d))],
        out_specs=row(d),
        out_shape=jax.ShapeDtypeStruct((b, l, d), F32),
        compiler_params=_params(("parallel", "parallel")),
        name="mix",
    )(x, y_ssm, y_att, wo1, wo2, g_x, wq, kmem, vmem, wo)


def _first_index(hit, idx, sentinel):
    return jnp.min(jnp.where(hit, idx, sentinel), axis=0, keepdims=True)


def _route_kernel(h_ref, g_ref, wr_ref, bias_ref, wsgu_ref, wsd_ref, tri_ref,
                  t_ref, base_ref, eidx_ref, gw_ref, rank_ref, cnt_ref, carry_ref):
    @pl.when(pl.program_id(0) == 0)
    def _():
        carry_ref[...] = jnp.zeros_like(carry_ref)

    h = h_ref[...]
    t = _rms(h, g_ref[...])
    tb = t.astype(BF16)
    t_ref[...] = _pack_rows(t)
    gu = jnp.dot(tb, wsgu_ref[...], preferred_element_type=F32)
    hid = jax.nn.silu(gu[:, :D_EXPERT]) * gu[:, D_EXPERT:]
    base_ref[...] = h + jnp.dot(hid.astype(BF16), wsd_ref[...], preferred_element_type=F32)

    logits = lax.dot_general(wr_ref[...], t, (((1,), (1,)), ((), ())), preferred_element_type=F32,
                             precision=lax.Precision.HIGHEST)
    scores = jax.nn.sigmoid(logits)
    biased = scores + bias_ref[...]
    n_tok = scores.shape[1]
    neg = -jnp.inf
    sub = lax.broadcasted_iota(I32, (PER_GROUP, n_tok), 0)

    gs = []
    for g in range(N_EXPERT_GROUPS):
        blk = biased[g * PER_GROUP:(g + 1) * PER_GROUP]
        m1 = jnp.max(blk, axis=0, keepdims=True)
        i1 = _first_index(blk == m1, sub, PER_GROUP)
        m2 = jnp.max(jnp.where(sub == i1, neg, blk), axis=0, keepdims=True)
        gs.append(m1 + m2)
    gs = jnp.concatenate(gs, axis=0)

    gsel = jnp.zeros(gs.shape, jnp.bool_)
    for _ in range(TOPK_GROUPS):
        m = jnp.max(gs, axis=0, keepdims=True)
        hit = sub == _first_index(gs == m, sub, N_EXPERT_GROUPS)
        gsel = jnp.logical_or(gsel, hit)
        gs = jnp.where(hit, neg, gs)

    masked = jnp.concatenate(
        [jnp.where(gsel[g:g + 1], biased[g * PER_GROUP:(g + 1) * PER_GROUP], neg)
         for g in range(N_EXPERT_GROUPS)], axis=0)
    eid = lax.broadcasted_iota(I32, masked.shape, 0)
    sel = jnp.zeros(masked.shape, jnp.bool_)
    idxs, gws = [], []
    for _ in range(TOP_K):
        m = jnp.max(masked, axis=0, keepdims=True)
        i = _first_index(masked == m, eid, N_EXPERTS)
        hit = eid == i
        idxs.append(i)
        gws.append(jnp.sum(jnp.where(hit, scores, 0.0), axis=0, keepdims=True))
        sel = jnp.logical_or(sel, hit)
        masked = jnp.where(hit, neg, masked)
    eidx = jnp.concatenate(idxs, axis=0)
    gw = jnp.concatenate(gws, axis=0)
    gw = gw / jnp.sum(gw, axis=0, keepdims=True) * ROUTED_SCALE
    eidx_ref[...] = eidx
    gw_ref[...] = gw

    before = jnp.dot(sel.astype(BF16), tri_ref[...], preferred_element_type=F32) + carry_ref[...]
    rank_ref[...] = jnp.concatenate(
        [jnp.sum(jnp.where(eid == idxs[k], before, 0.0), axis=0, keepdims=True) for k in range(TOP_K)],
        axis=0).astype(I32)
    carry = carry_ref[...] + jnp.sum(sel.astype(F32), axis=1, keepdims=True)
    carry_ref[...] = carry
    cnt_ref[...] = carry.astype(I32)


def _route(h2, g_ffn, wr_t, bias_col, wsgu, wsd, part, n_parts):
    n_all, d = h2.shape
    n = n_all // n_parts
    t = min(ROUTE_TILE, n)
    blk0 = part * (n // t)
    tri = (lax.broadcasted_iota(I32, (t, t), 0) < lax.broadcasted_iota(I32, (t, t), 1)).astype(BF16)
    row = lambda w: pl.BlockSpec((t, w), lambda i: (i, 0))
    col = pl.BlockSpec((TOP_K, t), lambda i: (0, i))
    return pl.pallas_call(
        _route_kernel,
        grid=(n // t,),
        in_specs=[pl.BlockSpec((t, d), lambda i: (i + blk0, 0)), _full((1, d)), _full((N_EXPERTS, d)),
                  _full((N_EXPERTS, 1)),
                  _full((d, 2 * D_EXPERT)), _full((D_EXPERT, d)), _full((t, t))],
        out_specs=[row(d // 2), row(d), col, col, col, _full((N_EXPERTS, 1))],
        out_shape=[jax.ShapeDtypeStruct((n, d // 2), I32), jax.ShapeDtypeStruct((n, d), F32),
                   jax.ShapeDtypeStruct((TOP_K, n), I32), jax.ShapeDtypeStruct((TOP_K, n), F32),
                   jax.ShapeDtypeStruct((TOP_K, n), I32), jax.ShapeDtypeStruct((N_EXPERTS, 1), I32)],
        scratch_shapes=[pltpu.VMEM((N_EXPERTS, 1), F32)],
        compiler_params=_params(("arbitrary",)),
        name="route",
    )(h2, g_ffn, wr_t, bias_col, wsgu, wsd, tri)


def _expert_kernel(be_ref, first_ref, slot_ref, next_ref, nu_ref, x_hbm, wg_hbm, wu_hbm, wd_hbm, y_ref,
                   wg_s, wu_s, wd_s, x_buf, x_sem, wg_buf, wu_buf, wd_buf, w_sem):
    i = pl.program_id(0)
    n_used = nu_ref[0]

    def fetch(step):
        slot = step % X_SLOTS
        rows = pl.ds(pl.multiple_of(step * MOE_TILE, MOE_TILE), MOE_TILE)
        return pltpu.make_async_copy(x_hbm.at[rows], x_buf.at[slot], x_sem.at[slot])

    def weight_fetch(expert, slot):
        pairs = ((wg_hbm, wg_buf), (wu_hbm, wu_buf), (wd_hbm, wd_buf))
        return [pltpu.make_async_copy(src.at[expert], buf.at[slot], w_sem.at[slot, j])
                for j, (src, buf) in enumerate(pairs)]

    @pl.when(i == 0)
    def _():
        for j in range(X_SLOTS - 1):
            @pl.when(j < n_used)
            def _(j=j):
                fetch(j).start()
        for cp in weight_fetch(be_ref[0], 0):
            cp.start()

    @pl.when(i + (X_SLOTS - 1) < n_used)
    def _():
        fetch(i + (X_SLOTS - 1)).start()

    @pl.when(first_ref[i] == 1)
    def _():
        slot = slot_ref[i]

        @pl.when(next_ref[i] >= 0)
        def _():
            for cp in weight_fetch(next_ref[i], 1 - slot):
                cp.start()

        for cp in weight_fetch(be_ref[i], slot):
            cp.wait()
        wg_s[...] = wg_buf[slot].astype(BF16)
        wu_s[...] = wu_buf[slot].astype(BF16)
        wd_s[...] = wd_buf[slot].astype(BF16)

    @pl.when(i < n_used)
    def _():
        fetch(i).wait()
        x = _unpack_rows(x_buf[i % X_SLOTS]).astype(BF16)
        gate = jnp.dot(x, wg_s[...], preferred_element_type=F32)
        up = jnp.dot(x, wu_s[...], preferred_element_type=F32)
        hid = (jax.nn.silu(gate) * up).astype(BF16)
        y_ref[...] = _pack_rows(jnp.dot(hid, wd_s[...], preferred_element_type=F32))


def _experts(schedule, xs, w_gate, w_up, w_down):
    block_e, run_first, run_slot, run_next, n_used = schedule
    n_slots, dw = xs.shape
    _, d, de = w_gate.shape
    nb = n_slots // MOE_TILE
    hbm = pl.BlockSpec(memory_space=pl.ANY)
    return pl.pallas_call(
        _expert_kernel,
        grid_spec=pltpu.PrefetchScalarGridSpec(
            num_scalar_prefetch=5,
            grid=(nb,),
            in_specs=[hbm, hbm, hbm, hbm],
            out_specs=pl.BlockSpec((MOE_TILE, dw), lambda i, be, fi, sl, nx, nu: (jnp.minimum(i, nu[0] - 1), 0)),
            scratch_shapes=[pltpu.VMEM((d, de), BF16), pltpu.VMEM((d, de), BF16), pltpu.VMEM((de, d), BF16),
                            pltpu.VMEM((X_SLOTS, MOE_TILE, dw), I32), pltpu.SemaphoreType.DMA((X_SLOTS,)),
                            pltpu.VMEM((2, d, de), F32), pltpu.VMEM((2, d, de), F32), pltpu.VMEM((2, de, d), F32),
                            pltpu.SemaphoreType.DMA((2, 3))]),
        out_shape=jax.ShapeDtypeStruct((n_slots, dw), I32),
        compiler_params=_params(("arbitrary",)),
        name="experts",
    )(block_e, run_first, run_slot, run_next, n_used, xs, w_gate, w_up, w_down)


def _sc_worker_id():
    return lax.axis_index("s") * SC_CORES + lax.axis_index("c")


def _sc_dispatch(t_rows, dest3, n_slots):
    _, dw = t_rows.shape
    n_chunks, _, w = dest3.shape
    per_worker = n_chunks // SC_WORKERS
    assert per_worker % 2 == 0
    mesh = plsc.VectorSubcoreMesh(core_axis_name="c", subcore_axis_name="s")
    dt = t_rows.dtype

    @functools.partial(
        pl.kernel, mesh=mesh,
        out_type=jax.ShapeDtypeStruct((n_slots, dw), dt),
        scratch_types=[pltpu.VMEM((TOP_K, w), I32), pltpu.VMEM((TOP_K, w), I32),
                       pltpu.VMEM((w, dw), dt), pltpu.VMEM((w, dw), dt),
                       pltpu.SemaphoreType.DMA, pltpu.SemaphoreType.DMA, pltpu.SemaphoreType.DMA],
    )
    def scatter_rows(t_hbm, dest_hbm, xs_hbm, idx_a, idx_b, rows_a, rows_b, sem_load, sem_a, sem_b):
        wid = _sc_worker_id()

        def scatter(idx_v, rows_v, sem):
            return [pltpu.async_copy(rows_v, xs_hbm.at[idx_v.at[k]], sem) for k in range(TOP_K)]

        @pl.loop(0, per_worker, step=2)
        def _(j):
            ca = wid * per_worker + j
            cb = ca + 1
            pltpu.sync_copy(dest_hbm.at[ca], idx_a)
            pltpu.sync_copy(t_hbm.at[pl.ds(ca * w, w)], rows_a)
            load_idx = pltpu.async_copy(dest_hbm.at[cb], idx_b, sem_load)
            load_rows = pltpu.async_copy(t_hbm.at[pl.ds(cb * w, w)], rows_b, sem_load)
            out_a = scatter(idx_a, rows_a, sem_a)
            load_idx.wait()
            load_rows.wait()
            out_b = scatter(idx_b, rows_b, sem_b)
            for cp in out_a + out_b:
                cp.wait()

    return scatter_rows(t_rows, dest3)


def _sc_combine(y_rows, dest3):
    _, dw = y_rows.shape
    n_chunks, _, w = dest3.shape
    per_worker = n_chunks // SC_WORKERS
    mesh = plsc.VectorSubcoreMesh(core_axis_name="c", subcore_axis_name="s")
    dt = y_rows.dtype

    @functools.partial(
        pl.kernel, mesh=mesh,
        out_type=jax.ShapeDtypeStruct((TOP_K, n_chunks * w, dw), dt),
        scratch_types=[pltpu.VMEM((TOP_K, w), I32), pltpu.VMEM((w, dw), dt), pltpu.VMEM((w, dw), dt),
                       pltpu.SemaphoreType.DMA, pltpu.SemaphoreType.DMA,
                       pltpu.SemaphoreType.DMA, pltpu.SemaphoreType.DMA],
    )
    def gather_rows(y_hbm, dest_hbm, out_hbm, idx_v, buf0, buf1, gsem0, gsem1, wsem0, wsem1):
        wid = _sc_worker_id()
        bufs, gsems, wsems = (buf0, buf1), (gsem0, gsem1), (wsem0, wsem1)

        @pl.loop(0, per_worker)
        def _(j):
            c = wid * per_worker + j
            pltpu.sync_copy(dest_hbm.at[c], idx_v)
            gathers = [None] * TOP_K
            writes = [None] * TOP_K
            gathers[0] = pltpu.async_copy(y_hbm.at[idx_v.at[0]], bufs[0], gsems[0])
            for k in range(TOP_K):
                b = k % 2
                if k + 1 < TOP_K:
                    if k >= 1:
                        writes[k - 1].wait()
                    gathers[k + 1] = pltpu.async_copy(y_hbm.at[idx_v.at[k + 1]], bufs[1 - b], gsems[1 - b])
                gathers[k].wait()
                writes[k] = pltpu.async_copy(bufs[b], out_hbm.at[k, pl.ds(c * w, w)], wsems[b])
            writes[TOP_K - 2].wait()
            writes[TOP_K - 1].wait()

    return gather_rows(y_rows, dest3)


def _final_kernel(base_ref, y_ref, gw_ref, g_ref, *rest, normalize):
    o_ref = rest[-1]
    h = base_ref[...]
    gw = gw_ref[...]
    for k in range(TOP_K):
        h = h + gw[:, k:k + 1] * _unpack_rows(y_ref[k])
    o_ref[...] = _rms(h, g_ref[...]) if normalize else h


def _final(base, yg, gw_rows, g_final, normalize, out_prev, part, n_parts):
    n, d = base.shape
    t = min(FIN_TILE, n)
    blk0 = part * (n // t)
    in_specs = [pl.BlockSpec((t, d), lambda i: (i, 0)), pl.BlockSpec((TOP_K, t, d // 2), lambda i: (0, i, 0)),
                pl.BlockSpec((t, TOP_K), lambda i: (i, 0)), _full((1, d))]
    args = [base, yg, gw_rows, g_final]
    aliases = {}
    if out_prev is not None:
        in_specs.append(pl.BlockSpec(memory_space=pl.ANY))
        args.append(out_prev)
        aliases = {len(args) - 1: 0}
    return pl.pallas_call(
        functools.partial(_final_kernel, normalize=normalize),
        grid=(n // t,),
        in_specs=in_specs,
        out_specs=pl.BlockSpec((t, d), lambda i: (i + blk0, 0)),
        out_shape=jax.ShapeDtypeStruct((n * n_parts, d), F32),
        input_output_aliases=aliases,
        compiler_params=_params(("parallel",)),
        name="final",
    )(*args)


def _ssm_matrices(a_re, a_im, log_dt, b_re, b_im, c_re, c_im):
    lr, li = a_re.astype(F32), a_im.astype(F32)
    dt = jnp.exp(log_dt.astype(F32))[:, None]
    mag = jnp.exp(lr * dt)
    ab_re, ab_im = mag * jnp.cos(li * dt), mag * jnp.sin(li * dt)
    den = lr * lr + li * li
    zr, zi = ab_re - 1.0, ab_im
    k_re = (zr * lr + zi * li) / den
    k_im = (zi * lr - zr * li) / den
    br, bi = b_re.astype(F32), b_im.astype(F32)
    bb_re = k_re[..., None] * br - k_im[..., None] * bi
    bb_im = k_re[..., None] * bi + k_im[..., None] * br
    eye = jnp.eye(SSM_GROUPS, dtype=F32)

    def in_mat(bb):
        return jnp.einsum("gpc,gh->gchp", bb, eye).reshape(SSM_WIDTH, N_STATE)

    def out_mat(c):
        return jnp.einsum("gcp,gh->gphc", c.astype(F32), eye).reshape(N_STATE, SSM_WIDTH)

    bmat = jnp.concatenate([in_mat(bb_re), in_mat(bb_im)], axis=1)
    cmat = jnp.concatenate([out_mat(c_re), -out_mat(c_im)], axis=0)
    tile = lambda a: jnp.broadcast_to(a.reshape(1, N_STATE), (SUBLANES, N_STATE))
    return bmat.astype(BF16), tile(ab_re), tile(ab_im), cmat.astype(BF16)


def _plan_kernel(pstart_ref, eidx_ref, rank_ref, dest_ref, *, window):
    eidx = eidx_ref[...]
    dest = rank_ref[...]
    for e in range(N_EXPERTS):
        dest = dest + jnp.where(eidx == e, pstart_ref[e], 0)
    for c in range(dest.shape[1] // window):
        dest_ref[c] = dest[:, c * window:(c + 1) * window]


def _plan(pstart, eidx, rank, window):
    k, n = eidx.shape
    t = min(PLAN_TILE, n)
    cols = pl.BlockSpec((k, t), lambda i, ps: (0, i))
    return pl.pallas_call(
        functools.partial(_plan_kernel, window=window),
        grid_spec=pltpu.PrefetchScalarGridSpec(
            num_scalar_prefetch=1,
            grid=(n // t,),
            in_specs=[cols, cols],
            out_specs=pl.BlockSpec((t // window, k, window), lambda i, ps: (i, 0, 0))),
        out_shape=jax.ShapeDtypeStruct((n // window, k, window), I32),
        compiler_params=_params(("parallel",)),
        name="plan",
    )(pstart, eidx, rank)


def _block_schedule(counts, n_tok):
    padded = ((counts + MOE_TILE - 1) // MOE_TILE) * MOE_TILE
    pend = jnp.cumsum(padded)
    pstart = (pend - padded).astype(I32)
    n_slots = n_tok * TOP_K + N_EXPERTS * MOE_TILE
    nb = n_slots // MOE_TILE
    n_used = (pend[-1] // MOE_TILE).astype(I32)
    blk = jnp.arange(nb, dtype=I32)
    block_e = jnp.sum((pend[None, :] <= (jnp.minimum(blk, n_used - 1) * MOE_TILE)[:, None]).astype(I32), axis=1)
    block_e = jnp.minimum(block_e, N_EXPERTS - 1)
    eid = jnp.arange(N_EXPERTS, dtype=I32)
    active = padded > 0
    run_of = jnp.cumsum(active.astype(I32)) - 1
    later = jnp.where(active, eid, N_EXPERTS)
    next_e = jnp.concatenate([lax.cummin(later[::-1])[::-1][1:], jnp.full((1,), N_EXPERTS, I32)])
    next_e = jnp.where(next_e < N_EXPERTS, next_e, -1)
    run_first = jnp.logical_and(blk * MOE_TILE == pstart[block_e], blk < n_used).astype(I32)
    schedule = (block_e, run_first, (run_of[block_e] % 2).astype(I32), next_e[block_e].astype(I32), n_used.reshape(1))
    return pstart, schedule, n_slots


def kernel(x, mem, positions, g_mix, w_in, a_re, a_im, log_dt, b_re, b_im, c_re, c_im, d_skip, w_glu, g_ssm_out, lam_q1, lam_k1, lam_q2, lam_k2, g_sub, w_out, g_x, g_mem, wq_x, wk_x, wv_x, wo_x, g_ffn, w_router, router_bias, w_gate, w_up, w_down, ws_gate, ws_up, ws_down, g_final):
    b, l, d = x.shape
    n = b * l
    depth = w_in.shape[0]
    row = lambda a: a.reshape(1, -1).astype(F32)
    inv = ROPE_THETA ** (-jnp.arange(0, DA_QKDIM, 2, dtype=F32) / DA_QKDIM)
    inv_row = jnp.tile(inv, 128 // inv.shape[0]).reshape(1, 128)
    pos3 = positions.reshape(b, l, 1)

    h = x
    for i in range(depth):
        lambda_init = 0.8 - 0.6 * math.exp(-0.3 * i)
        u, q, k, v = _inproj(h, pos3, row(g_mix[i]), w_in[i].astype(BF16), inv_row)

        bmat, are8, aim8, cmat = _ssm_matrices(a_re[i], a_im[i], log_dt[i], b_re[i], b_im[i], c_re[i], c_im[i])
        y_ssm = _ssm(u, bmat, are8, aim8, cmat, row(d_skip[i]), w_glu[i].astype(BF16), row(g_ssm_out[i]))

        lam = (jnp.exp(jnp.sum(lam_q1[i].astype(F32) * lam_k1[i].astype(F32)))
               - jnp.exp(jnp.sum(lam_q2[i].astype(F32) * lam_k2[i].astype(F32))) + lambda_init).reshape(1)
        y_att = _diff_attention(q, k, v, lam.astype(F32), row(g_sub[i]), lambda_init)

        kmem, vmem = _memkv(mem, row(g_mem[i]), wk_x[i].astype(BF16), wv_x[i].astype(BF16))
        wo = w_out[i].astype(BF16)
        h2 = _mix(h, y_ssm, y_att, wo[:SSM_WIDTH], wo[SSM_WIDTH:], row(g_x[i]), wq_x[i].astype(BF16),
                  kmem, vmem, wo_x[i].astype(BF16))

        wsgu = jnp.concatenate([ws_gate[i], ws_up[i]], axis=1).astype(BF16)
        n_part = n // MOE_PARTS
        out = None
        for part in range(MOE_PARTS):
            t_pk, base, eidx, gw, rank, counts = _route(
                h2.reshape(n, d), row(g_ffn[i]), w_router[i].T.astype(F32),
                router_bias[i].reshape(N_EXPERTS, 1).astype(F32), wsgu, ws_down[i].astype(BF16), part, MOE_PARTS)
            pstart, schedule, n_slots = _block_schedule(counts[:, 0], n_part)
            dest3 = _plan(pstart, eidx, rank, SC_WINDOW)
            xs = _sc_dispatch(t_pk, dest3, n_slots)
            ys = _experts(schedule, xs, w_gate[i], w_up[i], w_down[i])
            yg = _sc_combine(ys, dest3)
            out = _final(base, yg, gw.T, row(g_final), i == depth - 1, out, part, MOE_PARTS)
        h = out.reshape(b, l, d)
    return h
```

```python
import functools
import math

import jax
import jax.numpy as jnp
from jax import lax
from jax.experimental import pallas as pl
from jax.experimental.pallas import tpu as pltpu
from jax.experimental.pallas import tpu_sc as plsc

F32 = jnp.float32
BF16 = jnp.bfloat16
I32 = jnp.int32

D_MODEL = 1024
SSM_WIDTH = 512
ATTN_WIDTH = 512
SSM_GROUP = 16
SSM_GROUPS = 32
SSM_STATE = 64
N_STATE = SSM_GROUPS * SSM_STATE
DA_HEADS = 4
DA_VDIM = 128
DA_QKDIM = 64
ROPE_THETA = 10000.0
X_HEADS = 4
X_HEAD_DIM = 256
N_EXPERTS = 64
TOP_K = 8
N_EXPERT_GROUPS = 8
PER_GROUP = N_EXPERTS // N_EXPERT_GROUPS
TOPK_GROUPS = 4
D_EXPERT = 256
ROUTED_SCALE = 2.5
EPS = 1e-6

VMEM_LIMIT_V7X = 56 * 1024 * 1024
SUBLANES = 8

IN_TILE = 1024
SSM_STEPS = 128
SSM_COLS = 1024
SSM_CH = 128
SSM_SLAB = 512
ATT_TILE = 2048
ATT_ROWS = 256
MIX_TILE = 1024
ROUTE_TILE = 1024
MOE_TILE = 512
X_SLOTS = 3
PLAN_TILE = 2048
MOE_PARTS = 2
FIN_TILE = 256

SC_CORES = 2
SC_WORKERS = 32
SC_WINDOW = 64


def _params(sem):
    return pltpu.CompilerParams(dimension_semantics=sem, vmem_limit_bytes=VMEM_LIMIT_V7X)


def _rms(x, g):
    return x * lax.rsqrt(jnp.mean(x * x, axis=-1, keepdims=True) + EPS) * g


def _full(shape):
    return pl.BlockSpec(shape, lambda *_: (0,) * len(shape))


def _pack_rows(a):
    w = a.shape[1] // 2
    bits = lambda v: lax.bitcast_convert_type(v.astype(BF16).astype(F32), I32)
    return (bits(a[:, w:]) & jnp.int32(-65536)) | lax.shift_right_logical(bits(a[:, :w]), 16)


def _unpack_rows(p):
    lo = lax.bitcast_convert_type(lax.shift_left(p, 16), F32)
    hi = lax.bitcast_convert_type(p & jnp.int32(-65536), F32)
    return jnp.concatenate([lo, hi], axis=1)


def _inproj_kernel(x_ref, pos_ref, g_ref, w_ref, inv_ref, u_ref, q_ref, k_ref, v_ref):
    x = x_ref[0]
    a = _rms(x, g_ref[...]).astype(BF16)
    z = jnp.dot(a, w_ref[...], preferred_element_type=F32)
    u_ref[0] = z[:, :SSM_WIDTH]
    ang = pos_ref[0].astype(F32) * inv_ref[...]
    cos = jnp.concatenate([jnp.cos(ang)] * 4, axis=1)
    sin = jnp.concatenate([jnp.sin(ang)] * 4, axis=1)
    lane = lax.broadcasted_iota(I32, cos.shape, 1)
    first = (lane & (DA_QKDIM - 1)) < DA_QKDIM // 2
    half = DA_QKDIM // 2

    def rope(t):
        rot = jnp.where(first, -pltpu.roll(t, ATTN_WIDTH - half, 1), pltpu.roll(t, half, 1))
        return t * cos + rot * sin

    q = z[:, SSM_WIDTH:SSM_WIDTH + ATTN_WIDTH]
    k = z[:, SSM_WIDTH + ATTN_WIDTH:SSM_WIDTH + 2 * ATTN_WIDTH]
    q_ref[0] = (rope(q) * (DA_QKDIM ** -0.5)).astype(BF16)
    k_ref[0] = rope(k).astype(BF16)
    v_ref[0] = z[:, SSM_WIDTH + 2 * ATTN_WIDTH:].astype(BF16)


def _inproj(x, pos3, g_mix, w_in, inv_row):
    b, l, d = x.shape
    t = min(IN_TILE, l)
    n_out = w_in.shape[1]
    row = lambda w: pl.BlockSpec((1, t, w), lambda i, j: (i, j, 0))
    return pl.pallas_call(
        _inproj_kernel,
        grid=(b, l // t),
        in_specs=[row(d), row(1), _full((1, d)), _full((d, n_out)), _full((1, 128))],
        out_specs=[row(SSM_WIDTH), row(ATTN_WIDTH), row(ATTN_WIDTH), row(ATTN_WIDTH)],
        out_shape=[jax.ShapeDtypeStruct((b, l, SSM_WIDTH), F32)]
        + [jax.ShapeDtypeStruct((b, l, ATTN_WIDTH), BF16)] * 3,
        compiler_params=_params(("parallel", "parallel")),
        name="inproj",
    )(x, pos3, g_mix, w_in, inv_row)


def _ssm_kernel(u_ref, bm_ref, are_ref, aim_ref, cm_ref, dskip_ref, wglu_ref, g_ref, o_ref, bu_ref, st_ref,
                *, steps):
    @pl.when(pl.program_id(0) == 0)
    def _():
        st_ref[...] = jnp.zeros_like(st_ref)

    u = pltpu.einshape("bts->tbs", u_ref[...]).reshape(steps * SUBLANES, SSM_WIDTH)
    ub = u.astype(BF16)

    n_slabs = SSM_WIDTH // SSM_CH
    for s in range(n_slabs):
        ch = slice(s * SSM_CH, (s + 1) * SSM_CH)
        for part in (0, N_STATE):
            cols = slice(part + s * SSM_SLAB, part + (s + 1) * SSM_SLAB)
            bu_ref[:, cols] = jnp.dot(ub[:, ch], bm_ref[ch, cols], preferred_element_type=F32)

    for c0 in range(0, N_STATE, SSM_COLS):
        re = slice(c0, c0 + SSM_COLS)
        im = slice(N_STATE + c0, N_STATE + c0 + SSM_COLS)
        ar = are_ref[:, re]
        ai = aim_ref[:, re]

        def step(t, carry, re=re, im=im, ar=ar, ai=ai):
            sr, si = carry
            rows = pl.ds(pl.multiple_of(t * SUBLANES, SUBLANES), SUBLANES)
            nr = ar * sr - ai * si + bu_ref[rows, re]
            ni = ar * si + ai * sr + bu_ref[rows, im]
            bu_ref[rows, re] = nr
            bu_ref[rows, im] = ni
            return nr, ni

        sr, si = lax.fori_loop(0, steps, step, (st_ref[:, re], st_ref[:, im]))
        st_ref[:, re] = sr
        st_ref[:, im] = si

    ys = []
    for s in range(n_slabs):
        ch = slice(s * SSM_CH, (s + 1) * SSM_CH)
        acc = None
        for part in (0, N_STATE):
            cols = slice(part + s * SSM_SLAB, part + (s + 1) * SSM_SLAB)
            d = jnp.dot(bu_ref[:, cols].astype(BF16), cm_ref[cols, ch], preferred_element_type=F32)
            acc = d if acc is None else acc + d
        ys.append(acc)
    y = jnp.concatenate(ys, axis=1) + dskip_ref[...] * u
    y = jax.nn.gelu(y)
    y = y * jax.nn.sigmoid(jnp.dot(y.astype(BF16), wglu_ref[...], preferred_element_type=F32))
    out = _rms(y, g_ref[...]).reshape(steps, SUBLANES, SSM_WIDTH)
    o_ref[...] = pltpu.einshape("tbs->bts", out).astype(BF16)


def _ssm(u, bmat, a_re8, a_im8, cmat, d_skip, w_glu, g_out):
    batch, l, _ = u.shape
    assert batch == SUBLANES, "one time step of all sequences must fill the sublanes"
    steps = min(SSM_STEPS, l)
    rows = steps * batch
    seq = pl.BlockSpec((batch, steps, SSM_WIDTH), lambda i: (0, i, 0))
    return pl.pallas_call(
        functools.partial(_ssm_kernel, steps=steps),
        grid=(l // steps,),
        in_specs=[seq,
                  _full((SSM_WIDTH, 2 * N_STATE)), _full((SUBLANES, N_STATE)), _full((SUBLANES, N_STATE)),
                  _full((2 * N_STATE, SSM_WIDTH)), _full((1, SSM_WIDTH)), _full((SSM_WIDTH, SSM_WIDTH)),
                  _full((1, SSM_WIDTH))],
        out_specs=seq,
        out_shape=jax.ShapeDtypeStruct((batch, l, SSM_WIDTH), BF16),
        scratch_shapes=[pltpu.VMEM((rows, 2 * N_STATE), F32), pltpu.VMEM((SUBLANES, 2 * N_STATE), F32)],
        compiler_params=_params(("arbitrary",)),
        name="ssm",
    )(u, bmat, a_re8, a_im8, cmat, d_skip, w_glu, g_out)


def _attn_kernel(qi_ref, ki_ref, lam_ref, q_ref, k_ref, v_ref, g_ref, o_ref, qs_ref, m_ref, acc_ref, *,
                 tile, out_scale):
    qi = qi_ref[pl.program_id(2)]
    ki = ki_ref[pl.program_id(2)]

    @pl.when(ki == 0)
    def _():
        q = q_ref[0]
        lane = lax.broadcasted_iota(I32, q.shape, 1)
        zero = jnp.zeros_like(q)
        qs_ref[:tile] = jnp.where(lane < DA_QKDIM, q, zero)
        qs_ref[tile:] = jnp.where(lane >= DA_QKDIM, q, zero)
        m_ref[...] = jnp.full_like(m_ref, -jnp.inf)
        acc_ref[...] = jnp.zeros_like(acc_ref)

    def update(masked):
        v_ext = jnp.concatenate([v_ref[0], jnp.ones((tile, DA_VDIM), BF16)], axis=1)
        rb = min(ATT_ROWS, tile)
        n_rb = 2 * tile // rb

        def n_keys(r):
            return (r * rb) % tile + rb if masked else tile

        def scores(r):
            return lax.dot_general(qs_ref[r * rb:(r + 1) * rb], k_ref[0, :n_keys(r)], (((1,), (1,)), ((), ())),
                                   preferred_element_type=F32)

        s_next = scores(0)
        for r in range(n_rb):
            s = s_next
            if r + 1 < n_rb:
                s_next = scores(r + 1)
            rows = slice(r * rb, (r + 1) * rb)
            nk = n_keys(r)
            if masked:
                qpos = lax.broadcasted_iota(I32, s.shape, 0) + (r * rb) % tile
                s = jnp.where(lax.broadcasted_iota(I32, s.shape, 1) <= qpos, s, jnp.finfo(F32).min)
            m_old = m_ref[rows]
            m_new = jnp.maximum(m_old, jnp.max(s, axis=1, keepdims=True))
            p = jnp.exp(s - jnp.concatenate([m_new] * (nk // DA_VDIM), axis=1))
            alpha = jnp.exp(m_old - m_new)
            acc_ref[rows] = (jnp.concatenate([alpha, alpha], axis=1) * acc_ref[rows]
                             + jnp.dot(p.astype(BF16), v_ext[:nk], preferred_element_type=F32))
            m_ref[rows] = m_new

    @pl.when(ki < qi)
    def _():
        update(False)

    @pl.when(ki == qi)
    def _():
        update(True)
        o = acc_ref[:, :DA_VDIM] / acc_ref[:, DA_VDIM:]
        o = o[:tile] - lam_ref[0] * o[tile:]
        o_ref[0] = (_rms(o, g_ref[...]) * out_scale).astype(BF16)


def _diff_attention(q, k, v, lam, g_sub, lambda_init):
    b, l, _ = q.shape
    t = min(ATT_TILE, l)
    n = l // t
    pairs = [(qi, ki) for qi in range(n) for ki in range(qi + 1)]
    qi_tab = jnp.asarray([p[0] for p in pairs], I32)
    ki_tab = jnp.asarray([p[1] for p in pairs], I32)
    qspec = pl.BlockSpec((1, t, DA_VDIM), lambda bi, h, s, qt, kt: (bi, qt[s], h))
    kspec = pl.BlockSpec((1, t, DA_VDIM), lambda bi, h, s, qt, kt: (bi, kt[s], h))
    return pl.pallas_call(
        functools.partial(_attn_kernel, tile=t, out_scale=1.0 - lambda_init),
        grid_spec=pltpu.PrefetchScalarGridSpec(
            num_scalar_prefetch=2,
            grid=(b, DA_HEADS, len(pairs)),
            in_specs=[pl.BlockSpec(memory_space=pltpu.SMEM), qspec, kspec, kspec,
                      pl.BlockSpec((1, DA_VDIM), lambda bi, h, s, qt, kt: (0, 0))],
            out_specs=qspec,
            scratch_shapes=[pltpu.VMEM((2 * t, DA_VDIM), BF16), pltpu.VMEM((2 * t, DA_VDIM), F32),
                            pltpu.VMEM((2 * t, 2 * DA_VDIM), F32)]),
        out_shape=jax.ShapeDtypeStruct((b, l, ATTN_WIDTH), BF16),
        compiler_params=_params(("parallel", "parallel", "arbitrary")),
        name="diffattn",
    )(qi_tab, ki_tab, lam, q, k, v, g_sub)


def _memkv_kernel(m_ref, g_ref, wk_ref, wv_ref, k_ref, v_ref):
    a = _rms(m_ref[0], g_ref[...]).astype(BF16)
    k_ref[0] = jnp.dot(a, wk_ref[...], preferred_element_type=F32).astype(BF16)
    v_ref[0] = jnp.dot(a, wv_ref[...], preferred_element_type=F32).astype(BF16)


def _memkv(mem, g_mem, wk, wv):
    b, m, d = mem.shape
    blk = pl.BlockSpec((1, m, d), lambda i: (i, 0, 0))
    return pl.pallas_call(
        _memkv_kernel,
        grid=(b,),
        in_specs=[blk, _full((1, d)), _full((d, d)), _full((d, d))],
        out_specs=[blk, blk],
        out_shape=[jax.ShapeDtypeStruct((b, m, d), BF16)] * 2,
        compiler_params=_params(("parallel",)),
        name="memkv",
    )(mem, g_mem, wk, wv)


def _mix_kernel(x_ref, ys_ref, ya_ref, wo1_ref, wo2_ref, gx_ref, wq_ref, km_ref, vm_ref, wo_ref, h_ref):
    h = (x_ref[0]
         + jnp.dot(ys_ref[0], wo1_ref[...], preferred_element_type=F32)
         + jnp.dot(ya_ref[0], wo2_ref[...], preferred_element_type=F32))
    hq = _rms(h, gx_ref[...]).astype(BF16)
    q = jnp.dot(hq, wq_ref[...], preferred_element_type=F32).astype(BF16)
    outs = []
    for hd in range(X_HEADS):
        cols = slice(hd * X_HEAD_DIM, (hd + 1) * X_HEAD_DIM)
        s = lax.dot_general(q[:, cols], km_ref[0, :, cols], (((1,), (1,)), ((), ())),
                            preferred_element_type=F32) * (X_HEAD_DIM ** -0.5)
        s = s - jnp.max(s, axis=1, keepdims=True)
        p = jnp.exp(s)
        p = p / jnp.sum(p, axis=1, keepdims=True)
        outs.append(jnp.dot(p.astype(BF16), vm_ref[0, :, cols], preferred_element_type=F32).astype(BF16))
    o = jnp.concatenate(outs, axis=1)
    h_ref[0] = h + jnp.dot(o, wo_ref[...], preferred_element_type=F32)


def _mix(x, y_ssm, y_att, wo1, wo2, g_x, wq, kmem, vmem, wo):
    b, l, d = x.shape
    t = min(MIX_TILE, l)
    m = kmem.shape[1]
    row = lambda w: pl.BlockSpec((1, t, w), lambda i, j: (i, j, 0))
    mem = pl.BlockSpec((1, m, d), lambda i, j: (i, 0, 0))
    return pl.pallas_call(
        _mix_kernel,
        grid=(b, l // t),
        in_specs=[row(d), row(SSM_WIDTH), row(ATTN_WIDTH), _full((SSM_WIDTH, d)), _full((ATTN_WIDTH, d)),
                  _full((1, d)), _full((d, d)), mem, mem, _full((d, d))],
        out_specs=row(d),
        out_shape=jax.ShapeDtypeStruct((b, l, d), F32),
        compiler_params=_params(("parallel", "parallel")),
        name="mix",
    )(x, y_ssm, y_att, wo1, wo2, g_x, wq, kmem, vmem, wo)


def _first_index(hit, idx, sentinel):
    return jnp.min(jnp.where(hit, idx, sentinel), axis=0, keepdims=True)


def _route_kernel(h_ref, g_ref, wr_ref, bias_ref, wsgu_ref, wsd_ref, tri_ref,
                  t_ref, base_ref, eidx_ref, gw_ref, rank_ref, cnt_ref, carry_ref):
    @pl.when(pl.program_id(0) == 0)
    def _():
        carry_ref[...] = jnp.zeros_like(carry_ref)

    h = h_ref[...]
    t = _rms(h, g_ref[...])
    tb = t.astype(BF16)
    t_ref[...] = _pack_rows(t)
    gu = jnp.dot(tb, wsgu_ref[...], preferred_element_type=F32)
    hid = jax.nn.silu(gu[:, :D_EXPERT]) * gu[:, D_EXPERT:]
    base_ref[...] = h + jnp.dot(hid.astype(BF16), wsd_ref[...], preferred_element_type=F32)

    logits = lax.dot_general(wr_ref[...], t, (((1,), (1,)), ((), ())), preferred_element_type=F32,
                             precision=lax.Precision.HIGHEST)
    scores = jax.nn.sigmoid(logits)
    biased = scores + bias_ref[...]
    n_tok = scores.shape[1]
    neg = -jnp.inf
    sub = lax.broadcasted_iota(I32, (PER_GROUP, n_tok), 0)

    gs = []
    for g in range(N_EXPERT_GROUPS):
        blk = biased[g * PER_GROUP:(g + 1) * PER_GROUP]
        m1 = jnp.max(blk, axis=0, keepdims=True)
        i1 = _first_index(blk == m1, sub, PER_GROUP)
        m2 = jnp.max(jnp.where(sub == i1, neg, blk), axis=0, keepdims=True)
        gs.append(m1 + m2)
    gs = jnp.concatenate(gs, axis=0)

    gsel = jnp.zeros(gs.shape, jnp.bool_)
    for _ in range(TOPK_GROUPS):
        m = jnp.max(gs, axis=0, keepdims=True)
        hit = sub == _first_index(gs == m, sub, N_EXPERT_GROUPS)
        gsel = jnp.logical_or(gsel, hit)
        gs = jnp.where(hit, neg, gs)

    masked = jnp.concatenate(
        [jnp.where(gsel[g:g + 1], biased[g * PER_GROUP:(g + 1) * PER_GROUP], neg)
         for g in range(N_EXPERT_GROUPS)], axis=0)
    eid = lax.broadcasted_iota(I32, masked.shape, 0)
    sel = jnp.zeros(masked.shape, jnp.bool_)
    idxs, gws = [], []
    for _ in range(TOP_K):
        m = jnp.max(masked, axis=0, keepdims=True)
        i = _first_index(masked == m, eid, N_EXPERTS)
        hit = eid == i
        idxs.append(i)
        gws.append(jnp.sum(jnp.where(hit, scores, 0.0), axis=0, keepdims=True))
        sel = jnp.logical_or(sel, hit)
        masked = jnp.where(hit, neg, masked)
    eidx = jnp.concatenate(idxs, axis=0)
    gw = jnp.concatenate(gws, axis=0)
    gw = gw / jnp.sum(gw, axis=0, keepdims=True) * ROUTED_SCALE
    eidx_ref[...] = eidx
    gw_ref[...] = gw

    before = jnp.dot(sel.astype(BF16), tri_ref[...], preferred_element_type=F32) + carry_ref[...]
    rank_ref[...] = jnp.concatenate(
        [jnp.sum(jnp.where(eid == idxs[k], before, 0.0), axis=0, keepdims=True) for k in range(TOP_K)],
        axis=0).astype(I32)
    carry = carry_ref[...] + jnp.sum(sel.astype(F32), axis=1, keepdims=True)
    carry_ref[...] = carry
    cnt_ref[...] = carry.astype(I32)


def _route(h2, g_ffn, wr_t, bias_col, wsgu, wsd, part, n_parts):
    n_all, d = h2.shape
    n = n_all // n_parts
    t = min(ROUTE_TILE, n)
    blk0 = part * (n // t)
    tri = (lax.broadcasted_iota(I32, (t, t), 0) < lax.broadcasted_iota(I32, (t, t), 1)).astype(BF16)
    row = lambda w: pl.BlockSpec((t, w), lambda i: (i, 0))
    col = pl.BlockSpec((TOP_K, t), lambda i: (0, i))
    return pl.pallas_call(
        _route_kernel,
        grid=(n // t,),
        in_specs=[pl.BlockSpec((t, d), lambda i: (i + blk0, 0)), _full((1, d)), _full((N_EXPERTS, d)),
                  _full((N_EXPERTS, 1)),
                  _full((d, 2 * D_EXPERT)), _full((D_EXPERT, d)), _full((t, t))],
        out_specs=[row(d // 2), row(d), col, col, col, _full((N_EXPERTS, 1))],
        out_shape=[jax.ShapeDtypeStruct((n, d // 2), I32), jax.ShapeDtypeStruct((n, d), F32),
                   jax.ShapeDtypeStruct((TOP_K, n), I32), jax.ShapeDtypeStruct((TOP_K, n), F32),
                   jax.ShapeDtypeStruct((TOP_K, n), I32), jax.ShapeDtypeStruct((N_EXPERTS, 1), I32)],
        scratch_shapes=[pltpu.VMEM((N_EXPERTS, 1), F32)],
        compiler_params=_params(("arbitrary",)),
        name="route",
    )(h2, g_ffn, wr_t, bias_col, wsgu, wsd, tri)


def _expert_kernel(be_ref, first_ref, slot_ref, next_ref, nu_ref, x_hbm, wg_hbm, wu_hbm, wd_hbm, y_ref,
                   wg_s, wu_s, wd_s, x_buf, x_sem, wg_buf, wu_buf, wd_buf, w_sem):
    i = pl.program_id(0)
    n_used = nu_ref[0]

    def fetch(step):
        slot = step % X_SLOTS
        rows = pl.ds(pl.multiple_of(step * MOE_TILE, MOE_TILE), MOE_TILE)
        return pltpu.make_async_copy(x_hbm.at[rows], x_buf.at[slot], x_sem.at[slot])

    def weight_fetch(expert, slot):
        pairs = ((wg_hbm, wg_buf), (wu_hbm, wu_buf), (wd_hbm, wd_buf))
        return [pltpu.make_async_copy(src.at[expert], buf.at[slot], w_sem.at[slot, j])
                for j, (src, buf) in enumerate(pairs)]

    @pl.when(i == 0)
    def _():
        for j in range(X_SLOTS - 1):
            @pl.when(j < n_used)
            def _(j=j):
                fetch(j).start()
        for cp in weight_fetch(be_ref[0], 0):
            cp.start()

    @pl.when(i + (X_SLOTS - 1) < n_used)
    def _():
        fetch(i + (X_SLOTS - 1)).start()

    @pl.when(first_ref[i] == 1)
    def _():
        slot = slot_ref[i]

        @pl.when(next_ref[i] >= 0)
        def _():
            for cp in weight_fetch(next_ref[i], 1 - slot):
                cp.start()

        for cp in weight_fetch(be_ref[i], slot):
            cp.wait()
        wg_s[...] = wg_buf[slot].astype(BF16)
        wu_s[...] = wu_buf[slot].astype(BF16)
        wd_s[...] = wd_buf[slot].astype(BF16)

    @pl.when(i < n_used)
    def _():
        fetch(i).wait()
        x = _unpack_rows(x_buf[i % X_SLOTS]).astype(BF16)
        gate = jnp.dot(x, wg_s[...], preferred_element_type=F32)
        up = jnp.dot(x, wu_s[...], preferred_element_type=F32)
        hid = (jax.nn.silu(gate) * up).astype(BF16)
        y_ref[...] = _pack_rows(jnp.dot(hid, wd_s[...], preferred_element_type=F32))


def _experts(schedule, xs, w_gate, w_up, w_down):
    block_e, run_first, run_slot, run_next, n_used = schedule
    n_slots, dw = xs.shape
    _, d, de = w_gate.shape
    nb = n_slots // MOE_TILE
    hbm = pl.BlockSpec(memory_space=pl.ANY)
    return pl.pallas_call(
        _expert_kernel,
        grid_spec=pltpu.PrefetchScalarGridSpec(
            num_scalar_prefetch=5,
            grid=(nb,),
            in_specs=[hbm, hbm, hbm, hbm],
            out_specs=pl.BlockSpec((MOE_TILE, dw), lambda i, be, fi, sl, nx, nu: (jnp.minimum(i, nu[0] - 1), 0)),
            scratch_shapes=[pltpu.VMEM((d, de), BF16), pltpu.VMEM((d, de), BF16), pltpu.VMEM((de, d), BF16),
                            pltpu.VMEM((X_SLOTS, MOE_TILE, dw), I32), pltpu.SemaphoreType.DMA((X_SLOTS,)),
                            pltpu.VMEM((2, d, de), F32), pltpu.VMEM((2, d, de), F32), pltpu.VMEM((2, de, d), F32),
                            pltpu.SemaphoreType.DMA((2, 3))]),
        out_shape=jax.ShapeDtypeStruct((n_slots, dw), I32),
        compiler_params=_params(("arbitrary",)),
        name="experts",
    )(block_e, run_first, run_slot, run_next, n_used, xs, w_gate, w_up, w_down)


def _sc_worker_id():
    return lax.axis_index("s") * SC_CORES + lax.axis_index("c")


def _sc_dispatch(t_rows, dest3, n_slots):
    _, dw = t_rows.shape
    n_chunks, _, w = dest3.shape
    per_worker = n_chunks // SC_WORKERS
    assert per_worker % 2 == 0
    mesh = plsc.VectorSubcoreMesh(core_axis_name="c", subcore_axis_name="s")
    dt = t_rows.dtype

    @functools.partial(
        pl.kernel, mesh=mesh,
        out_type=jax.ShapeDtypeStruct((n_slots, dw), dt),
        scratch_types=[pltpu.VMEM((TOP_K, w), I32), pltpu.VMEM((TOP_K, w), I32),
                       pltpu.VMEM((w, dw), dt), pltpu.VMEM((w, dw), dt),
                       pltpu.SemaphoreType.DMA, pltpu.SemaphoreType.DMA, pltpu.SemaphoreType.DMA],
    )
    def scatter_rows(t_hbm, dest_hbm, xs_hbm, idx_a, idx_b, rows_a, rows_b, sem_load, sem_a, sem_b):
        wid = _sc_worker_id()

        def scatter(idx_v, rows_v, sem):
            return [pltpu.async_copy(rows_v, xs_hbm.at[idx_v.at[k]], sem) for k in range(TOP_K)]

        @pl.loop(0, per_worker, step=2)
        def _(j):
            ca = wid * per_worker + j
            cb = ca + 1
            pltpu.sync_copy(dest_hbm.at[ca], idx_a)
            pltpu.sync_copy(t_hbm.at[pl.ds(ca * w, w)], rows_a)
            load_idx = pltpu.async_copy(dest_hbm.at[cb], idx_b, sem_load)
            load_rows = pltpu.async_copy(t_hbm.at[pl.ds(cb * w, w)], rows_b, sem_load)
            out_a = scatter(idx_a, rows_a, sem_a)
            load_idx.wait()
            load_rows.wait()
            out_b = scatter(idx_b, rows_b, sem_b)
            for cp in out_a + out_b:
                cp.wait()

    return scatter_rows(t_rows, dest3)


def _sc_combine(y_rows, dest3):
    _, dw = y_rows.shape
    n_chunks, _, w = dest3.shape
    per_worker = n_chunks // SC_WORKERS
    mesh = plsc.VectorSubcoreMesh(core_axis_name="c", subcore_axis_name="s")
    dt = y_rows.dtype

    @functools.partial(
        pl.kernel, mesh=mesh,
        out_type=jax.ShapeDtypeStruct((TOP_K, n_chunks * w, dw), dt),
        scratch_types=[pltpu.VMEM((TOP_K, w), I32), pltpu.VMEM((w, dw), dt), pltpu.VMEM((w, dw), dt),
                       pltpu.SemaphoreType.DMA, pltpu.SemaphoreType.DMA,
                       pltpu.SemaphoreType.DMA, pltpu.SemaphoreType.DMA],
    )
    def gather_rows(y_hbm, dest_hbm, out_hbm, idx_v, buf0, buf1, gsem0, gsem1, wsem0, wsem1):
        wid = _sc_worker_id()
        bufs, gsems, wsems = (buf0, buf1), (gsem0, gsem1), (wsem0, wsem1)

        @pl.loop(0, per_worker)
        def _(j):
            c = wid * per_worker + j
            pltpu.sync_copy(dest_hbm.at[c], idx_v)
            gathers = [None] * TOP_K
            writes = [None] * TOP_K
            gathers[0] = pltpu.async_copy(y_hbm.at[idx_v.at[0]], bufs[0], gsems[0])
            for k in range(TOP_K):
                b = k % 2
                if k + 1 < TOP_K:
                    if k >= 1:
                        writes[k - 1].wait()
                    gathers[k + 1] = pltpu.async_copy(y_hbm.at[idx_v.at[k + 1]], bufs[1 - b], gsems[1 - b])
                gathers[k].wait()
                writes[k] = pltpu.async_copy(bufs[b], out_hbm.at[k, pl.ds(c * w, w)], wsems[b])
            writes[TOP_K - 2].wait()
            writes[TOP_K - 1].wait()

    return gather_rows(y_rows, dest3)


def _final_kernel(base_ref, y_ref, gw_ref, g_ref, *rest, normalize):
    o_ref = rest[-1]
    h = base_ref[...]
    gw = gw_ref[...]
    for k in range(TOP_K):
        h = h + gw[:, k:k + 1] * _unpack_rows(y_ref[k])
    o_ref[...] = _rms(h, g_ref[...]) if normalize else h


def _final(base, yg, gw_rows, g_final, normalize, out_prev, part, n_parts):
    n, d = base.shape
    t = min(FIN_TILE, n)
    blk0 = part * (n // t)
    in_specs = [pl.BlockSpec((t, d), lambda i: (i, 0)), pl.BlockSpec((TOP_K, t, d // 2), lambda i: (0, i, 0)),
                pl.BlockSpec((t, TOP_K), lambda i: (i, 0)), _full((1, d))]
    args = [base, yg, gw_rows, g_final]
    aliases = {}
    if out_prev is not None:
        in_specs.append(pl.BlockSpec(memory_space=pl.ANY))
        args.append(out_prev)
        aliases = {len(args) - 1: 0}
    return pl.pallas_call(
        functools.partial(_final_kernel, normalize=normalize),
        grid=(n // t,),
        in_specs=in_specs,
        out_specs=pl.BlockSpec((t, d), lambda i: (i + blk0, 0)),
        out_shape=jax.ShapeDtypeStruct((n * n_parts, d), F32),
        input_output_aliases=aliases,
        compiler_params=_params(("parallel",)),
        name="final",
    )(*args)


def _ssm_matrices(a_re, a_im, log_dt, b_re, b_im, c_re, c_im):
    lr, li = a_re.astype(F32), a_im.astype(F32)
    dt = jnp.exp(log_dt.astype(F32))[:, None]
    mag = jnp.exp(lr * dt)
    ab_re, ab_im = mag * jnp.cos(li * dt), mag * jnp.sin(li * dt)
    den = lr * lr + li * li
    zr, zi = ab_re - 1.0, ab_im
    k_re = (zr * lr + zi * li) / den
    k_im = (zi * lr - zr * li) / den
    br, bi = b_re.astype(F32), b_im.astype(F32)
    bb_re = k_re[..., None] * br - k_im[..., None] * bi
    bb_im = k_re[..., None] * bi + k_im[..., None] * br
    eye = jnp.eye(SSM_GROUPS, dtype=F32)

    def in_mat(bb):
        return jnp.einsum("gpc,gh->gchp", bb, eye).reshape(SSM_WIDTH, N_STATE)

    def out_mat(c):
        return jnp.einsum("gcp,gh->gphc", c.astype(F32), eye).reshape(N_STATE, SSM_WIDTH)

    bmat = jnp.concatenate([in_mat(bb_re), in_mat(bb_im)], axis=1)
    cmat = jnp.concatenate([out_mat(c_re), -out_mat(c_im)], axis=0)
    tile = lambda a: jnp.broadcast_to(a.reshape(1, N_STATE), (SUBLANES, N_STATE))
    return bmat.astype(BF16), tile(ab_re), tile(ab_im), cmat.astype(BF16)


def _plan_kernel(pstart_ref, eidx_ref, rank_ref, dest_ref, *, window):
    eidx = eidx_ref[...]
    dest = rank_ref[...]
    for e in range(N_EXPERTS):
        dest = dest + jnp.where(eidx == e, pstart_ref[e], 0)
    for c in range(dest.shape[1] // window):
        dest_ref[c] = dest[:, c * window:(c + 1) * window]


def _plan(pstart, eidx, rank, window):
    k, n = eidx.shape
    t = min(PLAN_TILE, n)
    cols = pl.BlockSpec((k, t), lambda i, ps: (0, i))
    return pl.pallas_call(
        functools.partial(_plan_kernel, window=window),
        grid_spec=pltpu.PrefetchScalarGridSpec(
            num_scalar_prefetch=1,
            grid=(n // t,),
            in_specs=[cols, cols],
            out_specs=pl.BlockSpec((t // window, k, window), lambda i, ps: (i, 0, 0))),
        out_shape=jax.ShapeDtypeStruct((n // window, k, window), I32),
        compiler_params=_params(("parallel",)),
        name="plan",
    )(pstart, eidx, rank)


def _block_schedule(counts, n_tok):
    padded = ((counts + MOE_TILE - 1) // MOE_TILE) * MOE_TILE
    pend = jnp.cumsum(padded)
    pstart = (pend - padded).astype(I32)
    n_slots = n_tok * TOP_K + N_EXPERTS * MOE_TILE
    nb = n_slots // MOE_TILE
    n_used = (pend[-1] // MOE_TILE).astype(I32)
    blk = jnp.arange(nb, dtype=I32)
    block_e = jnp.sum((pend[None, :] <= (jnp.minimum(blk, n_used - 1) * MOE_TILE)[:, None]).astype(I32), axis=1)
    block_e = jnp.minimum(block_e, N_EXPERTS - 1)
    eid = jnp.arange(N_EXPERTS, dtype=I32)
    active = padded > 0
    run_of = jnp.cumsum(active.astype(I32)) - 1
    later = jnp.where(jnp.logical_and(active[None, :], eid[None, :] > eid[:, None]), eid[None, :], N_EXPERTS)
    next_e = jnp.min(later, axis=1)
    next_e = jnp.where(next_e < N_EXPERTS, next_e, -1)
    onehot = block_e[:, None] == eid[None, :]
    lookup = lambda table: jnp.sum(jnp.where(onehot, table[None, :], 0), axis=1).astype(I32)
    run_first = jnp.logical_and(blk * MOE_TILE == lookup(pstart), blk < n_used).astype(I32)
    schedule = (block_e, run_first, lookup(run_of % 2), lookup(next_e), n_used.reshape(1))
    return pstart, schedule, n_slots


def kernel(x, mem, positions, g_mix, w_in, a_re, a_im, log_dt, b_re, b_im, c_re, c_im, d_skip, w_glu, g_ssm_out, lam_q1, lam_k1, lam_q2, lam_k2, g_sub, w_out, g_x, g_mem, wq_x, wk_x, wv_x, wo_x, g_ffn, w_router, router_bias, w_gate, w_up, w_down, ws_gate, ws_up, ws_down, g_final):
    b, l, d = x.shape
    n = b * l
    depth = w_in.shape[0]
    row = lambda a: a.reshape(1, -1).astype(F32)
    inv = ROPE_THETA ** (-jnp.arange(0, DA_QKDIM, 2, dtype=F32) / DA_QKDIM)
    inv_row = jnp.tile(inv, 128 // inv.shape[0]).reshape(1, 128)
    pos3 = positions.reshape(b, l, 1)

    h = x
    for i in range(depth):
        lambda_init = 0.8 - 0.6 * math.exp(-0.3 * i)
        u, q, k, v = _inproj(h, pos3, row(g_mix[i]), w_in[i].astype(BF16), inv_row)

        bmat, are8, aim8, cmat = _ssm_matrices(a_re[i], a_im[i], log_dt[i], b_re[i], b_im[i], c_re[i], c_im[i])
        y_ssm = _ssm(u, bmat, are8, aim8, cmat, row(d_skip[i]), w_glu[i].astype(BF16), row(g_ssm_out[i]))

        lam = (jnp.exp(jnp.sum(lam_q1[i].astype(F32) * lam_k1[i].astype(F32)))
               - jnp.exp(jnp.sum(lam_q2[i].astype(F32) * lam_k2[i].astype(F32))) + lambda_init).reshape(1)
        y_att = _diff_attention(q, k, v, lam.astype(F32), row(g_sub[i]), lambda_init)

        kmem, vmem = _memkv(mem, row(g_mem[i]), wk_x[i].astype(BF16), wv_x[i].astype(BF16))
        wo = w_out[i].astype(BF16)
        h2 = _mix(h, y_ssm, y_att, wo[:SSM_WIDTH], wo[SSM_WIDTH:], row(g_x[i]), wq_x[i].astype(BF16),
                  kmem, vmem, wo_x[i].astype(BF16))

        wsgu = jnp.concatenate([ws_gate[i], ws_up[i]], axis=1).astype(BF16)
        n_part = n // MOE_PARTS
        out = None
        for part in range(MOE_PARTS):
            t_pk, base, eidx, gw, rank, counts = _route(
                h2.reshape(n, d), row(g_ffn[i]), w_router[i].T.astype(F32),
                router_bias[i].reshape(N_EXPERTS, 1).astype(F32), wsgu, ws_down[i].astype(BF16), part, MOE_PARTS)
            pstart, schedule, n_slots = _block_schedule(counts[:, 0], n_part)
            dest3 = _plan(pstart, eidx, rank, SC_WINDOW)
            xs = _sc_dispatch(t_pk, dest3, n_slots)
            ys = _experts(schedule, xs, w_gate[i], w_up[i], w_down[i])
            yg = _sc_combine(ys, dest3)
            out = _final(base, yg, gw.T, row(g_final), i == depth - 1, out, part, MOE_PARTS)
        h = out.reshape(b, l, d)
    return h
```

```python
import functools
import math

import jax
import jax.numpy as jnp
from jax import lax
from jax.experimental import pallas as pl
from jax.experimental.pallas import tpu as pltpu
from jax.experimental.pallas import tpu_sc as plsc

F32 = jnp.float32
BF16 = jnp.bfloat16
I32 = jnp.int32

D_MODEL = 1024
SSM_WIDTH = 512
ATTN_WIDTH = 512
SSM_GROUP = 16
SSM_GROUPS = 32
SSM_STATE = 64
N_STATE = SSM_GROUPS * SSM_STATE
DA_HEADS = 4
DA_VDIM = 128
DA_QKDIM = 64
ROPE_THETA = 10000.0
X_HEADS = 4
X_HEAD_DIM = 256
N_EXPERTS = 64
TOP_K = 8
N_EXPERT_GROUPS = 8
PER_GROUP = N_EXPERTS // N_EXPERT_GROUPS
TOPK_GROUPS = 4
D_EXPERT = 256
ROUTED_SCALE = 2.5
EPS = 1e-6

VMEM_LIMIT_V7X = 56 * 1024 * 1024
SUBLANES = 8

IN_TILE = 1024
SSM_STEPS = 128
SSM_COLS = 1024
SSM_CH = 128
SSM_SLAB = 512
ATT_TILE = 2048
ATT_ROWS = 256
MIX_TILE = 1024
ROUTE_TILE = 1024
MOE_TILE = 512
X_SLOTS = 3
PLAN_TILE = 2048
MOE_PARTS = 2
FIN_TILE = 256

SC_CORES = 2
SC_WORKERS = 32
SC_WINDOW = 64


def _params(sem):
    return pltpu.CompilerParams(dimension_semantics=sem, vmem_limit_bytes=VMEM_LIMIT_V7X)


def _rms(x, g):
    return x * lax.rsqrt(jnp.mean(x * x, axis=-1, keepdims=True) + EPS) * g


def _full(shape):
    return pl.BlockSpec(shape, lambda *_: (0,) * len(shape))


def _pack_rows(a):
    w = a.shape[1] // 2
    bits = lambda v: lax.bitcast_convert_type(v.astype(BF16).astype(F32), I32)
    return (bits(a[:, w:]) & jnp.int32(-65536)) | lax.shift_right_logical(bits(a[:, :w]), 16)


def _unpack_rows(p):
    lo = lax.bitcast_convert_type(lax.shift_left(p, 16), F32)
    hi = lax.bitcast_convert_type(p & jnp.int32(-65536), F32)
    return jnp.concatenate([lo, hi], axis=1)


def _inproj_kernel(x_ref, pos_ref, g_ref, w_ref, inv_ref, u_ref, q_ref, k_ref, v_ref):
    x = x_ref[0]
    a = _rms(x, g_ref[...]).astype(BF16)
    z = jnp.dot(a, w_ref[...], preferred_element_type=F32)
    u_ref[0] = z[:, :SSM_WIDTH]
    ang = pos_ref[0].astype(F32) * inv_ref[...]
    cos = jnp.concatenate([jnp.cos(ang)] * 4, axis=1)
    sin = jnp.concatenate([jnp.sin(ang)] * 4, axis=1)
    lane = lax.broadcasted_iota(I32, cos.shape, 1)
    first = (lane & (DA_QKDIM - 1)) < DA_QKDIM // 2
    half = DA_QKDIM // 2

    def rope(t):
        rot = jnp.where(first, -pltpu.roll(t, ATTN_WIDTH - half, 1), pltpu.roll(t, half, 1))
        return t * cos + rot * sin

    q = z[:, SSM_WIDTH:SSM_WIDTH + ATTN_WIDTH]
    k = z[:, SSM_WIDTH + ATTN_WIDTH:SSM_WIDTH + 2 * ATTN_WIDTH]
    q_ref[0] = (rope(q) * (DA_QKDIM ** -0.5)).astype(BF16)
    k_ref[0] = rope(k).astype(BF16)
    v_ref[0] = z[:, SSM_WIDTH + 2 * ATTN_WIDTH:].astype(BF16)


def _inproj(x, pos3, g_mix, w_in, inv_row):
    b, l, d = x.shape
    t = min(IN_TILE, l)
    n_out = w_in.shape[1]
    row = lambda w: pl.BlockSpec((1, t, w), lambda i, j: (i, j, 0))
    return pl.pallas_call(
        _inproj_kernel,
        grid=(b, l // t),
        in_specs=[row(d), row(1), _full((1, d)), _full((d, n_out)), _full((1, 128))],
        out_specs=[row(SSM_WIDTH), row(ATTN_WIDTH), row(ATTN_WIDTH), row(ATTN_WIDTH)],
        out_shape=[jax.ShapeDtypeStruct((b, l, SSM_WIDTH), F32)]
        + [jax.ShapeDtypeStruct((b, l, ATTN_WIDTH), BF16)] * 3,
        compiler_params=_params(("parallel", "parallel")),
        name="inproj",
    )(x, pos3, g_mix, w_in, inv_row)


def _ssm_kernel(u_ref, bm_ref, are_ref, aim_ref, cm_ref, dskip_ref, wglu_ref, g_ref, o_ref, bu_ref, st_ref,
                *, steps):
    @pl.when(pl.program_id(0) == 0)
    def _():
        st_ref[...] = jnp.zeros_like(st_ref)

    u = pltpu.einshape("bts->tbs", u_ref[...]).reshape(steps * SUBLANES, SSM_WIDTH)
    ub = u.astype(BF16)

    n_slabs = SSM_WIDTH // SSM_CH
    for s in range(n_slabs):
        ch = slice(s * SSM_CH, (s + 1) * SSM_CH)
        for part in (0, N_STATE):
            cols = slice(part + s * SSM_SLAB, part + (s + 1) * SSM_SLAB)
            bu_ref[:, cols] = jnp.dot(ub[:, ch], bm_ref[ch, cols], preferred_element_type=F32)

    for c0 in range(0, N_STATE, SSM_COLS):
        re = slice(c0, c0 + SSM_COLS)
        im = slice(N_STATE + c0, N_STATE + c0 + SSM_COLS)
        ar = are_ref[:, re]
        ai = aim_ref[:, re]

        def step(t, carry, re=re, im=im, ar=ar, ai=ai):
            sr, si = carry
            rows = pl.ds(pl.multiple_of(t * SUBLANES, SUBLANES), SUBLANES)
            nr = ar * sr - ai * si + bu_ref[rows, re]
            ni = ar * si + ai * sr + bu_ref[rows, im]
            bu_ref[rows, re] = nr
            bu_ref[rows, im] = ni
            return nr, ni

        sr, si = lax.fori_loop(0, steps, step, (st_ref[:, re], st_ref[:, im]))
        st_ref[:, re] = sr
        st_ref[:, im] = si

    ys = []
    for s in range(n_slabs):
        ch = slice(s * SSM_CH, (s + 1) * SSM_CH)
        acc = None
        for part in (0, N_STATE):
            cols = slice(part + s * SSM_SLAB, part + (s + 1) * SSM_SLAB)
            d = jnp.dot(bu_ref[:, cols].astype(BF16), cm_ref[cols, ch], preferred_element_type=F32)
            acc = d if acc is None else acc + d
        ys.append(acc)
    y = jnp.concatenate(ys, axis=1) + dskip_ref[...] * u
    y = jax.nn.gelu(y)
    y = y * jax.nn.sigmoid(jnp.dot(y.astype(BF16), wglu_ref[...], preferred_element_type=F32))
    out = _rms(y, g_ref[...]).reshape(steps, SUBLANES, SSM_WIDTH)
    o_ref[...] = pltpu.einshape("tbs->bts", out).astype(BF16)


def _ssm(u, bmat, a_re8, a_im8, cmat, d_skip, w_glu, g_out):
    batch, l, _ = u.shape
    assert batch == SUBLANES, "one time step of all sequences must fill the sublanes"
    steps = min(SSM_STEPS, l)
    rows = steps * batch
    seq = pl.BlockSpec((batch, steps, SSM_WIDTH), lambda i: (0, i, 0))
    return pl.pallas_call(
        functools.partial(_ssm_kernel, steps=steps),
        grid=(l // steps,),
        in_specs=[seq,
                  _full((SSM_WIDTH, 2 * N_STATE)), _full((SUBLANES, N_STATE)), _full((SUBLANES, N_STATE)),
                  _full((2 * N_STATE, SSM_WIDTH)), _full((1, SSM_WIDTH)), _full((SSM_WIDTH, SSM_WIDTH)),
                  _full((1, SSM_WIDTH))],
        out_specs=seq,
        out_shape=jax.ShapeDtypeStruct((batch, l, SSM_WIDTH), BF16),
        scratch_shapes=[pltpu.VMEM((rows, 2 * N_STATE), F32), pltpu.VMEM((SUBLANES, 2 * N_STATE), F32)],
        compiler_params=_params(("arbitrary",)),
        name="ssm",
    )(u, bmat, a_re8, a_im8, cmat, d_skip, w_glu, g_out)


def _attn_kernel(qi_ref, ki_ref, lam_ref, q_ref, k_ref, v_ref, g_ref, o_ref, qs_ref, m_ref, acc_ref, *,
                 tile, out_scale):
    qi = qi_ref[pl.program_id(2)]
    ki = ki_ref[pl.program_id(2)]

    @pl.when(ki == 0)
    def _():
        q = q_ref[0]
        lane = lax.broadcasted_iota(I32, q.shape, 1)
        zero = jnp.zeros_like(q)
        qs_ref[:tile] = jnp.where(lane < DA_QKDIM, q, zero)
        qs_ref[tile:] = jnp.where(lane >= DA_QKDIM, q, zero)
        m_ref[...] = jnp.full_like(m_ref, -jnp.inf)
        acc_ref[...] = jnp.zeros_like(acc_ref)

    def update(masked):
        v_ext = jnp.concatenate([v_ref[0], jnp.ones((tile, DA_VDIM), BF16)], axis=1)
        rb = min(ATT_ROWS, tile)
        n_rb = 2 * tile // rb

        def n_keys(r):
            return (r * rb) % tile + rb if masked else tile

        def scores(r):
            return lax.dot_general(qs_ref[r * rb:(r + 1) * rb], k_ref[0, :n_keys(r)], (((1,), (1,)), ((), ())),
                                   preferred_element_type=F32)

        s_next = scores(0)
        for r in range(n_rb):
            s = s_next
            if r + 1 < n_rb:
                s_next = scores(r + 1)
            rows = slice(r * rb, (r + 1) * rb)
            nk = n_keys(r)
            if masked:
                tri = lax.broadcasted_iota(I32, (rb, rb), 1) <= lax.broadcasted_iota(I32, (rb, rb), 0)
                edge = jnp.where(tri, s[:, nk - rb:], jnp.finfo(F32).min)
                s = edge if nk == rb else jnp.concatenate([s[:, :nk - rb], edge], axis=1)
            m_old = m_ref[rows]
            m_new = jnp.maximum(m_old, jnp.max(s, axis=1, keepdims=True))
            p = jnp.exp(s - jnp.concatenate([m_new] * (nk // DA_VDIM), axis=1))
            alpha = jnp.exp(m_old - m_new)
            acc_ref[rows] = (jnp.concatenate([alpha, alpha], axis=1) * acc_ref[rows]
                             + jnp.dot(p.astype(BF16), v_ext[:nk], preferred_element_type=F32))
            m_ref[rows] = m_new

    @pl.when(ki < qi)
    def _():
        update(False)

    @pl.when(ki == qi)
    def _():
        update(True)
        o = acc_ref[:, :DA_VDIM] / acc_ref[:, DA_VDIM:]
        o = o[:tile] - lam_ref[0] * o[tile:]
        o_ref[0] = (_rms(o, g_ref[...]) * out_scale).astype(BF16)


def _diff_attention(q, k, v, lam, g_sub, lambda_init):
    b, l, _ = q.shape
    t = min(ATT_TILE, l)
    n = l // t
    pairs = [(qi, ki) for qi in range(n) for ki in range(qi + 1)]
    qi_tab = jnp.asarray([p[0] for p in pairs], I32)
    ki_tab = jnp.asarray([p[1] for p in pairs], I32)
    qspec = pl.BlockSpec((1, t, DA_VDIM), lambda bi, h, s, qt, kt: (bi, qt[s], h))
    kspec = pl.BlockSpec((1, t, DA_VDIM), lambda bi, h, s, qt, kt: (bi, kt[s], h))
    return pl.pallas_call(
        functools.partial(_attn_kernel, tile=t, out_scale=1.0 - lambda_init),
        grid_spec=pltpu.PrefetchScalarGridSpec(
            num_scalar_prefetch=2,
            grid=(b, DA_HEADS, len(pairs)),
            in_specs=[pl.BlockSpec(memory_space=pltpu.SMEM), qspec, kspec, kspec,
                      pl.BlockSpec((1, DA_VDIM), lambda bi, h, s, qt, kt: (0, 0))],
            out_specs=qspec,
            scratch_shapes=[pltpu.VMEM((2 * t, DA_VDIM), BF16), pltpu.VMEM((2 * t, DA_VDIM), F32),
                            pltpu.VMEM((2 * t, 2 * DA_VDIM), F32)]),
        out_shape=jax.ShapeDtypeStruct((b, l, ATTN_WIDTH), BF16),
        compiler_params=_params(("parallel", "parallel", "arbitrary")),
        name="diffattn",
    )(qi_tab, ki_tab, lam, q, k, v, g_sub)


def _memkv_kernel(m_ref, g_ref, wk_ref, wv_ref, k_ref, v_ref):
    a = _rms(m_ref[0], g_ref[...]).astype(BF16)
    k_ref[0] = jnp.dot(a, wk_ref[...], preferred_element_type=F32).astype(BF16)
    v_ref[0] = jnp.dot(a, wv_ref[...], preferred_element_type=F32).astype(BF16)


def _memkv(mem, g_mem, wk, wv):
    b, m, d = mem.shape
    blk = pl.BlockSpec((1, m, d), lambda i: (i, 0, 0))
    return pl.pallas_call(
        _memkv_kernel,
        grid=(b,),
        in_specs=[blk, _full((1, d)), _full((d, d)), _full((d, d))],
        out_specs=[blk, blk],
        out_shape=[jax.ShapeDtypeStruct((b, m, d), BF16)] * 2,
        compiler_params=_params(("parallel",)),
        name="memkv",
    )(mem, g_mem, wk, wv)


def _mix_kernel(x_ref, ys_ref, ya_ref, wo1_ref, wo2_ref, gx_ref, wq_ref, km_ref, vm_ref, wo_ref, h_ref):
    h = (x_ref[0]
         + jnp.dot(ys_ref[0], wo1_ref[...], preferred_element_type=F32)
         + jnp.dot(ya_ref[0], wo2_ref[...], preferred_element_type=F32))
    hq = _rms(h, gx_ref[...]).astype(BF16)
    q = jnp.dot(hq, wq_ref[...], preferred_element_type=F32).astype(BF16)
    outs = []
    for hd in range(X_HEADS):
        cols = slice(hd * X_HEAD_DIM, (hd + 1) * X_HEAD_DIM)
        s = lax.dot_general(q[:, cols], km_ref[0, :, cols], (((1,), (1,)), ((), ())),
                            preferred_element_type=F32) * (X_HEAD_DIM ** -0.5)
        s = s - jnp.max(s, axis=1, keepdims=True)
        p = jnp.exp(s)
        p = p / jnp.sum(p, axis=1, keepdims=True)
        outs.append(jnp.dot(p.astype(BF16), vm_ref[0, :, cols], preferred_element_type=F32).astype(BF16))
    o = jnp.concatenate(outs, axis=1)
    h_ref[0] = h + jnp.dot(o, wo_ref[...], preferred_element_type=F32)


def _mix(x, y_ssm, y_att, wo1, wo2, g_x, wq, kmem, vmem, wo):
    b, l, d = x.shape
    t = min(MIX_TILE, l)
    m = kmem.shape[1]
    row = lambda w: pl.BlockSpec((1, t, w), lambda i, j: (i, j, 0))
    mem = pl.BlockSpec((1, m, d), lambda i, j: (i, 0, 0))
    return pl.pallas_call(
        _mix_kernel,
        grid=(b, l // t),
        in_specs=[row(d), row(SSM_WIDTH), row(ATTN_WIDTH), _full((SSM_WIDTH, d)), _full((ATTN_WIDTH, d)),
                  _full((1, d)), _full((d, d)), mem, mem, _full((d, d))],
        out_specs=row(d),
        out_shape=jax.ShapeDtypeStruct((b, l, d), F32),
        compiler_params=_params(("parallel", "parallel")),
        name="mix",
    )(x, y_ssm, y_att, wo1, wo2, g_x, wq, kmem, vmem, wo)


def _first_index(hit, idx, sentinel):
    return jnp.min(jnp.where(hit, idx, sentinel), axis=0, keepdims=True)


def _route_kernel(h_ref, g_ref, wr_ref, bias_ref, wsgu_ref, wsd_ref, tri_ref,
                  t_ref, base_ref, eidx_ref, gw_ref, rank_ref, cnt_ref, carry_ref):
    @pl.when(pl.program_id(0) == 0)
    def _():
        carry_ref[...] = jnp.zeros_like(carry_ref)

    h = h_ref[...]
    t = _rms(h, g_ref[...])
    tb = t.astype(BF16)
    t_ref[...] = _pack_rows(t)
    gu = jnp.dot(tb, wsgu_ref[...], preferred_element_type=F32)
    hid = jax.nn.silu(gu[:, :D_EXPERT]) * gu[:, D_EXPERT:]
    base_ref[...] = h + jnp.dot(hid.astype(BF16), wsd_ref[...], preferred_element_type=F32)

    nt = (((1,), (1,)), ((), ()))
    t_lo = (t - tb.astype(F32)).astype(BF16)
    both = lax.dot_general(wr_ref[...], tb, nt, preferred_element_type=F32)
    logits = (both[:N_EXPERTS] + both[N_EXPERTS:]
              + lax.dot_general(wr_ref[:N_EXPERTS], t_lo, nt, preferred_element_type=F32))
    scores = jax.nn.sigmoid(logits)
    biased = scores + bias_ref[...]
    n_tok = scores.shape[1]
    neg = -jnp.inf
    sub = lax.broadcasted_iota(I32, (PER_GROUP, n_tok), 0)

    gs = []
    for g in range(N_EXPERT_GROUPS):
        blk = biased[g * PER_GROUP:(g + 1) * PER_GROUP]
        m1 = jnp.max(blk, axis=0, keepdims=True)
        i1 = _first_index(blk == m1, sub, PER_GROUP)
        m2 = jnp.max(jnp.where(sub == i1, neg, blk), axis=0, keepdims=True)
        gs.append(m1 + m2)
    gs = jnp.concatenate(gs, axis=0)

    gsel = jnp.zeros(gs.shape, jnp.bool_)
    for _ in range(TOPK_GROUPS):
        m = jnp.max(gs, axis=0, keepdims=True)
        hit = sub == _first_index(gs == m, sub, N_EXPERT_GROUPS)
        gsel = jnp.logical_or(gsel, hit)
        gs = jnp.where(hit, neg, gs)

    masked = jnp.concatenate(
        [jnp.where(gsel[g:g + 1], biased[g * PER_GROUP:(g + 1) * PER_GROUP], neg)
         for g in range(N_EXPERT_GROUPS)], axis=0)
    eid = lax.broadcasted_iota(I32, masked.shape, 0)
    sel = jnp.zeros(masked.shape, jnp.bool_)
    idxs, gws = [], []
    for _ in range(TOP_K):
        m = jnp.max(masked, axis=0, keepdims=True)
        i = _first_index(masked == m, eid, N_EXPERTS)
        hit = eid == i
        idxs.append(i)
        gws.append(jnp.sum(jnp.where(hit, scores, 0.0), axis=0, keepdims=True))
        sel = jnp.logical_or(sel, hit)
        masked = jnp.where(hit, neg, masked)
    eidx = jnp.concatenate(idxs, axis=0)
    gw = jnp.concatenate(gws, axis=0)
    gw = gw / jnp.sum(gw, axis=0, keepdims=True) * ROUTED_SCALE
    eidx_ref[...] = eidx
    gw_ref[...] = gw

    before = jnp.dot(sel.astype(BF16), tri_ref[...], preferred_element_type=F32) + carry_ref[...]
    rank_ref[...] = jnp.concatenate(
        [jnp.sum(jnp.where(eid == idxs[k], before, 0.0), axis=0, keepdims=True) for k in range(TOP_K)],
        axis=0).astype(I32)
    carry = carry_ref[...] + jnp.sum(sel.astype(F32), axis=1, keepdims=True)
    carry_ref[...] = carry
    cnt_ref[...] = carry.astype(I32)


def _route(h2, g_ffn, wr_t, bias_col, wsgu, wsd, part, n_parts):
    n_all, d = h2.shape
    n = n_all // n_parts
    t = min(ROUTE_TILE, n)
    blk0 = part * (n // t)
    tri = (lax.broadcasted_iota(I32, (t, t), 0) < lax.broadcasted_iota(I32, (t, t), 1)).astype(BF16)
    row = lambda w: pl.BlockSpec((t, w), lambda i: (i, 0))
    col = pl.BlockSpec((TOP_K, t), lambda i: (0, i))
    return pl.pallas_call(
        _route_kernel,
        grid=(n // t,),
        in_specs=[pl.BlockSpec((t, d), lambda i: (i + blk0, 0)), _full((1, d)), _full((2 * N_EXPERTS, d)),
                  _full((N_EXPERTS, 1)),
                  _full((d, 2 * D_EXPERT)), _full((D_EXPERT, d)), _full((t, t))],
        out_specs=[row(d // 2), row(d), col, col, col, _full((N_EXPERTS, 1))],
        out_shape=[jax.ShapeDtypeStruct((n, d // 2), I32), jax.ShapeDtypeStruct((n, d), F32),
                   jax.ShapeDtypeStruct((TOP_K, n), I32), jax.ShapeDtypeStruct((TOP_K, n), F32),
                   jax.ShapeDtypeStruct((TOP_K, n), I32), jax.ShapeDtypeStruct((N_EXPERTS, 1), I32)],
        scratch_shapes=[pltpu.VMEM((N_EXPERTS, 1), F32)],
        compiler_params=_params(("arbitrary",)),
        name="route",
    )(h2, g_ffn, wr_t, bias_col, wsgu, wsd, tri)


def _expert_kernel(be_ref, first_ref, slot_ref, next_ref, nu_ref, x_hbm, wg_hbm, wu_hbm, wd_hbm, y_ref,
                   wg_s, wu_s, wd_s, x_buf, x_sem, wg_buf, wu_buf, wd_buf, w_sem):
    i = pl.program_id(0)
    n_used = nu_ref[0]

    def fetch(step):
        slot = step % X_SLOTS
        rows = pl.ds(pl.multiple_of(step * MOE_TILE, MOE_TILE), MOE_TILE)
        return pltpu.make_async_copy(x_hbm.at[rows], x_buf.at[slot], x_sem.at[slot])

    def weight_fetch(expert, slot):
        pairs = ((wg_hbm, wg_buf), (wu_hbm, wu_buf), (wd_hbm, wd_buf))
        return [pltpu.make_async_copy(src.at[expert], buf.at[slot], w_sem.at[slot, j])
                for j, (src, buf) in enumerate(pairs)]

    @pl.when(i == 0)
    def _():
        for j in range(X_SLOTS - 1):
            @pl.when(j < n_used)
            def _(j=j):
                fetch(j).start()
        for cp in weight_fetch(be_ref[0], 0):
            cp.start()

    @pl.when(i + (X_SLOTS - 1) < n_used)
    def _():
        fetch(i + (X_SLOTS - 1)).start()

    @pl.when(first_ref[i] == 1)
    def _():
        slot = slot_ref[i]

        @pl.when(next_ref[i] >= 0)
        def _():
            for cp in weight_fetch(next_ref[i], 1 - slot):
                cp.start()

        for cp in weight_fetch(be_ref[i], slot):
            cp.wait()
        wg_s[...] = wg_buf[slot].astype(BF16)
        wu_s[...] = wu_buf[slot].astype(BF16)
        wd_s[...] = wd_buf[slot].astype(BF16)

    @pl.when(i < n_used)
    def _():
        fetch(i).wait()
        x = _unpack_rows(x_buf[i % X_SLOTS]).astype(BF16)
        gate = jnp.dot(x, wg_s[...], preferred_element_type=F32)
        up = jnp.dot(x, wu_s[...], preferred_element_type=F32)
        hid = (jax.nn.silu(gate) * up).astype(BF16)
        y_ref[...] = _pack_rows(jnp.dot(hid, wd_s[...], preferred_element_type=F32))


def _experts(schedule, xs, w_gate, w_up, w_down):
    block_e, run_first, run_slot, run_next, n_used = schedule
    n_slots, dw = xs.shape
    _, d, de = w_gate.shape
    nb = n_slots // MOE_TILE
    hbm = pl.BlockSpec(memory_space=pl.ANY)
    return pl.pallas_call(
        _expert_kernel,
        grid_spec=pltpu.PrefetchScalarGridSpec(
            num_scalar_prefetch=5,
            grid=(nb,),
            in_specs=[hbm, hbm, hbm, hbm],
            out_specs=pl.BlockSpec((MOE_TILE, dw), lambda i, be, fi, sl, nx, nu: (jnp.minimum(i, nu[0] - 1), 0)),
            scratch_shapes=[pltpu.VMEM((d, de), BF16), pltpu.VMEM((d, de), BF16), pltpu.VMEM((de, d), BF16),
                            pltpu.VMEM((X_SLOTS, MOE_TILE, dw), I32), pltpu.SemaphoreType.DMA((X_SLOTS,)),
                            pltpu.VMEM((2, d, de), F32), pltpu.VMEM((2, d, de), F32), pltpu.VMEM((2, de, d), F32),
                            pltpu.SemaphoreType.DMA((2, 3))]),
        out_shape=jax.ShapeDtypeStruct((n_slots, dw), I32),
        compiler_params=_params(("arbitrary",)),
        name="experts",
    )(block_e, run_first, run_slot, run_next, n_used, xs, w_gate, w_up, w_down)


def _sc_worker_id():
    return lax.axis_index("s") * SC_CORES + lax.axis_index("c")


def _sc_dispatch(t_rows, dest3, n_slots):
    _, dw = t_rows.shape
    n_chunks, _, w = dest3.shape
    per_worker = n_chunks // SC_WORKERS
    assert per_worker % 2 == 0
    mesh = plsc.VectorSubcoreMesh(core_axis_name="c", subcore_axis_name="s")
    dt = t_rows.dtype

    @functools.partial(
        pl.kernel, mesh=mesh,
        out_type=jax.ShapeDtypeStruct((n_slots, dw), dt),
        scratch_types=[pltpu.VMEM((TOP_K, w), I32), pltpu.VMEM((TOP_K, w), I32),
                       pltpu.VMEM((w, dw), dt), pltpu.VMEM((w, dw), dt),
                       pltpu.SemaphoreType.DMA, pltpu.SemaphoreType.DMA, pltpu.SemaphoreType.DMA],
    )
    def scatter_rows(t_hbm, dest_hbm, xs_hbm, idx_a, idx_b, rows_a, rows_b, sem_load, sem_a, sem_b):
        wid = _sc_worker_id()

        def scatter(idx_v, rows_v, sem):
            return [pltpu.async_copy(rows_v, xs_hbm.at[idx_v.at[k]], sem) for k in range(TOP_K)]

        @pl.loop(0, per_worker, step=2)
        def _(j):
            ca = wid * per_worker + j
            cb = ca + 1
            pltpu.sync_copy(dest_hbm.at[ca], idx_a)
            pltpu.sync_copy(t_hbm.at[pl.ds(ca * w, w)], rows_a)
            load_idx = pltpu.async_copy(dest_hbm.at[cb], idx_b, sem_load)
            load_rows = pltpu.async_copy(t_hbm.at[pl.ds(cb * w, w)], rows_b, sem_load)
            out_a = scatter(idx_a, rows_a, sem_a)
            load_idx.wait()
            load_rows.wait()
            out_b = scatter(idx_b, rows_b, sem_b)
            for cp in out_a + out_b:
                cp.wait()

    return scatter_rows(t_rows, dest3)


def _sc_combine(y_rows, dest3):
    _, dw = y_rows.shape
    n_chunks, _, w = dest3.shape
    per_worker = n_chunks // SC_WORKERS
    mesh = plsc.VectorSubcoreMesh(core_axis_name="c", subcore_axis_name="s")
    dt = y_rows.dtype

    @functools.partial(
        pl.kernel, mesh=mesh,
        out_type=jax.ShapeDtypeStruct((TOP_K, n_chunks * w, dw), dt),
        scratch_types=[pltpu.VMEM((TOP_K, w), I32), pltpu.VMEM((w, dw), dt), pltpu.VMEM((w, dw), dt),
                       pltpu.SemaphoreType.DMA, pltpu.SemaphoreType.DMA,
                       pltpu.SemaphoreType.DMA, pltpu.SemaphoreType.DMA],
    )
    def gather_rows(y_hbm, dest_hbm, out_hbm, idx_v, buf0, buf1, gsem0, gsem1, wsem0, wsem1):
        wid = _sc_worker_id()
        bufs, gsems, wsems = (buf0, buf1), (gsem0, gsem1), (wsem0, wsem1)

        @pl.loop(0, per_worker)
        def _(j):
            c = wid * per_worker + j
            pltpu.sync_copy(dest_hbm.at[c], idx_v)
            gathers = [None] * TOP_K
            writes = [None] * TOP_K
            gathers[0] = pltpu.async_copy(y_hbm.at[idx_v.at[0]], bufs[0], gsems[0])
            for k in range(TOP_K):
                b = k % 2
                if k + 1 < TOP_K:
                    if k >= 1:
                        writes[k - 1].wait()
                    gathers[k + 1] = pltpu.async_copy(y_hbm.at[idx_v.at[k + 1]], bufs[1 - b], gsems[1 - b])
                gathers[k].wait()
                writes[k] = pltpu.async_copy(bufs[b], out_hbm.at[k, pl.ds(c * w, w)], wsems[b])
            writes[TOP_K - 2].wait()
            writes[TOP_K - 1].wait()

    return gather_rows(y_rows, dest3)


def _final_kernel(base_ref, y_ref, gw_ref, g_ref, *rest, normalize):
    o_ref = rest[-1]
    h = base_ref[...]
    gw = gw_ref[...]
    for k in range(TOP_K):
        h = h + gw[:, k:k + 1] * _unpack_rows(y_ref[k])
    o_ref[...] = _rms(h, g_ref[...]) if normalize else h


def _final(base, yg, gw_rows, g_final, normalize, out_prev, part, n_parts):
    n, d = base.shape
    t = min(FIN_TILE, n)
    blk0 = part * (n // t)
    in_specs = [pl.BlockSpec((t, d), lambda i: (i, 0)), pl.BlockSpec((TOP_K, t, d // 2), lambda i: (0, i, 0)),
                pl.BlockSpec((t, TOP_K), lambda i: (i, 0)), _full((1, d))]
    args = [base, yg, gw_rows, g_final]
    aliases = {}
    if out_prev is not None:
        in_specs.append(pl.BlockSpec(memory_space=pl.ANY))
        args.append(out_prev)
        aliases = {len(args) - 1: 0}
    return pl.pallas_call(
        functools.partial(_final_kernel, normalize=normalize),
        grid=(n // t,),
        in_specs=in_specs,
        out_specs=pl.BlockSpec((t, d), lambda i: (i + blk0, 0)),
        out_shape=jax.ShapeDtypeStruct((n * n_parts, d), F32),
        input_output_aliases=aliases,
        compiler_params=_params(("parallel",)),
        name="final",
    )(*args)


def _ssm_matrices(a_re, a_im, log_dt, b_re, b_im, c_re, c_im):
    lr, li = a_re.astype(F32), a_im.astype(F32)
    dt = jnp.exp(log_dt.astype(F32))[:, None]
    mag = jnp.exp(lr * dt)
    ab_re, ab_im = mag * jnp.cos(li * dt), mag * jnp.sin(li * dt)
    den = lr * lr + li * li
    zr, zi = ab_re - 1.0, ab_im
    k_re = (zr * lr + zi * li) / den
    k_im = (zi * lr - zr * li) / den
    br, bi = b_re.astype(F32), b_im.astype(F32)
    bb_re = k_re[..., None] * br - k_im[..., None] * bi
    bb_im = k_re[..., None] * bi + k_im[..., None] * br
    eye = jnp.eye(SSM_GROUPS, dtype=F32)

    def in_mat(bb):
        return jnp.einsum("gpc,gh->gchp", bb, eye).reshape(SSM_WIDTH, N_STATE)

    def out_mat(c):
        return jnp.einsum("gcp,gh->gphc", c.astype(F32), eye).reshape(N_STATE, SSM_WIDTH)

    bmat = jnp.concatenate([in_mat(bb_re), in_mat(bb_im)], axis=1)
    cmat = jnp.concatenate([out_mat(c_re), -out_mat(c_im)], axis=0)
    tile = lambda a: jnp.broadcast_to(a.reshape(1, N_STATE), (SUBLANES, N_STATE))
    return bmat.astype(BF16), tile(ab_re), tile(ab_im), cmat.astype(BF16)


def _plan_kernel(pstart_ref, eidx_ref, rank_ref, dest_ref, *, window):
    eidx = eidx_ref[...]
    dest = rank_ref[...]
    for e in range(N_EXPERTS):
        dest = dest + jnp.where(eidx == e, pstart_ref[e], 0)
    for c in range(dest.shape[1] // window):
        dest_ref[c] = dest[:, c * window:(c + 1) * window]


def _plan(pstart, eidx, rank, window):
    k, n = eidx.shape
    t = min(PLAN_TILE, n)
    cols = pl.BlockSpec((k, t), lambda i, ps: (0, i))
    return pl.pallas_call(
        functools.partial(_plan_kernel, window=window),
        grid_spec=pltpu.PrefetchScalarGridSpec(
            num_scalar_prefetch=1,
            grid=(n // t,),
            in_specs=[cols, cols],
            out_specs=pl.BlockSpec((t // window, k, window), lambda i, ps: (i, 0, 0))),
        out_shape=jax.ShapeDtypeStruct((n // window, k, window), I32),
        compiler_params=_params(("parallel",)),
        name="plan",
    )(pstart, eidx, rank)


def _block_schedule(counts, n_tok):
    padded = ((counts + MOE_TILE - 1) // MOE_TILE) * MOE_TILE
    pend = jnp.cumsum(padded)
    pstart = (pend - padded).astype(I32)
    n_slots = n_tok * TOP_K + N_EXPERTS * MOE_TILE
    nb = n_slots // MOE_TILE
    n_used = (pend[-1] // MOE_TILE).astype(I32)
    blk = jnp.arange(nb, dtype=I32)
    block_e = jnp.sum((pend[None, :] <= (jnp.minimum(blk, n_used - 1) * MOE_TILE)[:, None]).astype(I32), axis=1)
    block_e = jnp.minimum(block_e, N_EXPERTS - 1)
    eid = jnp.arange(N_EXPERTS, dtype=I32)
    active = padded > 0
    run_of = jnp.cumsum(active.astype(I32)) - 1
    later = jnp.where(jnp.logical_and(active[None, :], eid[None, :] > eid[:, None]), eid[None, :], N_EXPERTS)
    next_e = jnp.min(later, axis=1)
    next_e = jnp.where(next_e < N_EXPERTS, next_e, -1)
    onehot = block_e[:, None] == eid[None, :]
    lookup = lambda table: jnp.sum(jnp.where(onehot, table[None, :], 0), axis=1).astype(I32)
    run_first = jnp.logical_and(blk * MOE_TILE == lookup(pstart), blk < n_used).astype(I32)
    schedule = (block_e, run_first, lookup(run_of % 2), lookup(next_e), n_used.reshape(1))
    return pstart, schedule, n_slots


def kernel(x, mem, positions, g_mix, w_in, a_re, a_im, log_dt, b_re, b_im, c_re, c_im, d_skip, w_glu, g_ssm_out, lam_q1, lam_k1, lam_q2, lam_k2, g_sub, w_out, g_x, g_mem, wq_x, wk_x, wv_x, wo_x, g_ffn, w_router, router_bias, w_gate, w_up, w_down, ws_gate, ws_up, ws_down, g_final):
    b, l, d = x.shape
    n = b * l
    depth = w_in.shape[0]
    row = lambda a: a.reshape(1, -1).astype(F32)
    inv = ROPE_THETA ** (-jnp.arange(0, DA_QKDIM, 2, dtype=F32) / DA_QKDIM)
    inv_row = jnp.tile(inv, 128 // inv.shape[0]).reshape(1, 128)
    pos3 = positions.reshape(b, l, 1)

    h = x
    for i in range(depth):
        lambda_init = 0.8 - 0.6 * math.exp(-0.3 * i)
        u, q, k, v = _inproj(h, pos3, row(g_mix[i]), w_in[i].astype(BF16), inv_row)

        bmat, are8, aim8, cmat = _ssm_matrices(a_re[i], a_im[i], log_dt[i], b_re[i], b_im[i], c_re[i], c_im[i])
        y_ssm = _ssm(u, bmat, are8, aim8, cmat, row(d_skip[i]), w_glu[i].astype(BF16), row(g_ssm_out[i]))

        lam = (jnp.exp(jnp.sum(lam_q1[i].astype(F32) * lam_k1[i].astype(F32)))
               - jnp.exp(jnp.sum(lam_q2[i].astype(F32) * lam_k2[i].astype(F32))) + lambda_init).reshape(1)
        y_att = _diff_attention(q, k, v, lam.astype(F32), row(g_sub[i]), lambda_init)

        kmem, vmem = _memkv(mem, row(g_mem[i]), wk_x[i].astype(BF16), wv_x[i].astype(BF16))
        wo = w_out[i].astype(BF16)
        h2 = _mix(h, y_ssm, y_att, wo[:SSM_WIDTH], wo[SSM_WIDTH:], row(g_x[i]), wq_x[i].astype(BF16),
                  kmem, vmem, wo_x[i].astype(BF16))

        wsgu = jnp.concatenate([ws_gate[i], ws_up[i]], axis=1).astype(BF16)
        wr = w_router[i].T.astype(F32)
        wr_hi = wr.astype(BF16)
        wr_split = jnp.concatenate([wr_hi, (wr - wr_hi.astype(F32)).astype(BF16)], axis=0)
        n_part = n // MOE_PARTS
        out = None
        for part in range(MOE_PARTS):
            t_pk, base, eidx, gw, rank, counts = _route(
                h2.reshape(n, d), row(g_ffn[i]), wr_split,
                router_bias[i].reshape(N_EXPERTS, 1).astype(F32), wsgu, ws_down[i].astype(BF16), part, MOE_PARTS)
            pstart, schedule, n_slots = _block_schedule(counts[:, 0], n_part)
            dest3 = _plan(pstart, eidx, rank, SC_WINDOW)
            xs = _sc_dispatch(t_pk, dest3, n_slots)
            ys = _experts(schedule, xs, w_gate[i], w_up[i], w_down[i])
            yg = _sc_combine(ys, dest3)
            out = _final(base, yg, gw.T, row(g_final), i == depth - 1, out, part, MOE_PARTS)
        h = out.reshape(b, l, d)
    return h
```

```python
import functools
import math

import jax
import jax.numpy as jnp
from jax import lax
from jax.experimental import pallas as pl
from jax.experimental.pallas import tpu as pltpu
from jax.experimental.pallas import tpu_sc as plsc

F32 = jnp.float32
BF16 = jnp.bfloat16
I32 = jnp.int32

D_MODEL = 1024
SSM_WIDTH = 512
ATTN_WIDTH = 512
SSM_GROUP = 16
SSM_GROUPS = 32
SSM_STATE = 64
N_STATE = SSM_GROUPS * SSM_STATE
DA_HEADS = 4
DA_VDIM = 128
DA_QKDIM = 64
ROPE_THETA = 10000.0
X_HEADS = 4
X_HEAD_DIM = 256
N_EXPERTS = 64
TOP_K = 8
N_EXPERT_GROUPS = 8
PER_GROUP = N_EXPERTS // N_EXPERT_GROUPS
TOPK_GROUPS = 4
D_EXPERT = 256
ROUTED_SCALE = 2.5
EPS = 1e-6

VMEM_LIMIT_V7X = 56 * 1024 * 1024
SUBLANES = 8

IN_TILE = 1024
SSM_STEPS = 128
SSM_COLS = 1024
SSM_CH = 128
SSM_SLAB = 512
ATT_TILE = 2048
ATT_ROWS = 256
MIX_TILE = 1024
ROUTE_TILE = 1024
MOE_TILE = 512
X_SLOTS = 3
PLAN_TILE = 2048
MOE_PARTS = 2
FIN_TILE = 256

SC_CORES = 2
SC_WORKERS = 32
SC_WINDOW = 64


def _params(sem):
    return pltpu.CompilerParams(dimension_semantics=sem, vmem_limit_bytes=VMEM_LIMIT_V7X)


def _rms(x, g):
    return x * lax.rsqrt(jnp.mean(x * x, axis=-1, keepdims=True) + EPS) * g


def _full(shape):
    return pl.BlockSpec(shape, lambda *_: (0,) * len(shape))


def _pack_rows(a):
    w = a.shape[1] // 2
    bits = lambda v: lax.bitcast_convert_type(v.astype(BF16).astype(F32), I32)
    return (bits(a[:, w:]) & jnp.int32(-65536)) | lax.shift_right_logical(bits(a[:, :w]), 16)


def _unpack_rows(p):
    lo = lax.bitcast_convert_type(lax.shift_left(p, 16), F32)
    hi = lax.bitcast_convert_type(p & jnp.int32(-65536), F32)
    return jnp.concatenate([lo, hi], axis=1)


def _inproj_kernel(x_ref, pos_ref, g_ref, w_ref, inv_ref, u_ref, q_ref, k_ref, v_ref):
    x = x_ref[0]
    a = _rms(x, g_ref[...]).astype(BF16)
    z = jnp.dot(a, w_ref[...], preferred_element_type=F32)
    u_ref[0] = z[:, :SSM_WIDTH]
    ang = pos_ref[0].astype(F32) * inv_ref[...]
    cos = jnp.concatenate([jnp.cos(ang)] * 4, axis=1)
    sin = jnp.concatenate([jnp.sin(ang)] * 4, axis=1)
    lane = lax.broadcasted_iota(I32, cos.shape, 1)
    first = (lane & (DA_QKDIM - 1)) < DA_QKDIM // 2
    half = DA_QKDIM // 2

    def rope(t):
        rot = jnp.where(first, -pltpu.roll(t, ATTN_WIDTH - half, 1), pltpu.roll(t, half, 1))
        return t * cos + rot * sin

    q = z[:, SSM_WIDTH:SSM_WIDTH + ATTN_WIDTH]
    k = z[:, SSM_WIDTH + ATTN_WIDTH:SSM_WIDTH + 2 * ATTN_WIDTH]
    q_ref[0] = (rope(q) * (DA_QKDIM ** -0.5)).astype(BF16)
    k_ref[0] = rope(k).astype(BF16)
    v_ref[0] = z[:, SSM_WIDTH + 2 * ATTN_WIDTH:].astype(BF16)


def _inproj(x, pos3, g_mix, w_in, inv_row):
    b, l, d = x.shape
    t = min(IN_TILE, l)
    n_out = w_in.shape[1]
    row = lambda w: pl.BlockSpec((1, t, w), lambda i, j: (i, j, 0))
    return pl.pallas_call(
        _inproj_kernel,
        grid=(b, l // t),
        in_specs=[row(d), row(1), _full((1, d)), _full((d, n_out)), _full((1, 128))],
        out_specs=[row(SSM_WIDTH), row(ATTN_WIDTH), row(ATTN_WIDTH), row(ATTN_WIDTH)],
        out_shape=[jax.ShapeDtypeStruct((b, l, SSM_WIDTH), F32)]
        + [jax.ShapeDtypeStruct((b, l, ATTN_WIDTH), BF16)] * 3,
        compiler_params=_params(("parallel", "parallel")),
        name="inproj",
    )(x, pos3, g_mix, w_in, inv_row)


def _ssm_kernel(u_ref, bm_ref, are_ref, aim_ref, cm_ref, dskip_ref, wglu_ref, g_ref, o_ref, bu_ref, st_ref,
                *, steps):
    @pl.when(pl.program_id(0) == 0)
    def _():
        st_ref[...] = jnp.zeros_like(st_ref)

    u = pltpu.einshape("bts->tbs", u_ref[...]).reshape(steps * SUBLANES, SSM_WIDTH)
    ub = u.astype(BF16)

    n_slabs = SSM_WIDTH // SSM_CH
    for s in range(n_slabs):
        ch = slice(s * SSM_CH, (s + 1) * SSM_CH)
        for part in (0, N_STATE):
            cols = slice(part + s * SSM_SLAB, part + (s + 1) * SSM_SLAB)
            bu_ref[:, cols] = jnp.dot(ub[:, ch], bm_ref[ch, cols], preferred_element_type=F32)

    for c0 in range(0, N_STATE, SSM_COLS):
        re = slice(c0, c0 + SSM_COLS)
        im = slice(N_STATE + c0, N_STATE + c0 + SSM_COLS)
        ar = are_ref[:, re]
        ai = aim_ref[:, re]

        sr, si = st_ref[:, re], st_ref[:, im]
        for t in range(steps):
            rows = slice(t * SUBLANES, (t + 1) * SUBLANES)
            sr, si = ar * sr - ai * si + bu_ref[rows, re], ar * si + ai * sr + bu_ref[rows, im]
            bu_ref[rows, re] = sr
            bu_ref[rows, im] = si
        st_ref[:, re] = sr
        st_ref[:, im] = si

    ys = []
    for s in range(n_slabs):
        ch = slice(s * SSM_CH, (s + 1) * SSM_CH)
        acc = None
        for part in (0, N_STATE):
            cols = slice(part + s * SSM_SLAB, part + (s + 1) * SSM_SLAB)
            d = jnp.dot(bu_ref[:, cols].astype(BF16), cm_ref[cols, ch], preferred_element_type=F32)
            acc = d if acc is None else acc + d
        ys.append(acc)
    y = jnp.concatenate(ys, axis=1) + dskip_ref[...] * u
    y = jax.nn.gelu(y)
    y = y * jax.nn.sigmoid(jnp.dot(y.astype(BF16), wglu_ref[...], preferred_element_type=F32))
    out = _rms(y, g_ref[...]).reshape(steps, SUBLANES, SSM_WIDTH)
    o_ref[...] = pltpu.einshape("tbs->bts", out).astype(BF16)


def _ssm(u, bmat, a_re8, a_im8, cmat, d_skip, w_glu, g_out):
    batch, l, _ = u.shape
    assert batch == SUBLANES, "one time step of all sequences must fill the sublanes"
    steps = min(SSM_STEPS, l)
    rows = steps * batch
    seq = pl.BlockSpec((batch, steps, SSM_WIDTH), lambda i: (0, i, 0))
    return pl.pallas_call(
        functools.partial(_ssm_kernel, steps=steps),
        grid=(l // steps,),
        in_specs=[seq,
                  _full((SSM_WIDTH, 2 * N_STATE)), _full((SUBLANES, N_STATE)), _full((SUBLANES, N_STATE)),
                  _full((2 * N_STATE, SSM_WIDTH)), _full((1, SSM_WIDTH)), _full((SSM_WIDTH, SSM_WIDTH)),
                  _full((1, SSM_WIDTH))],
        out_specs=seq,
        out_shape=jax.ShapeDtypeStruct((batch, l, SSM_WIDTH), BF16),
        scratch_shapes=[pltpu.VMEM((rows, 2 * N_STATE), F32), pltpu.VMEM((SUBLANES, 2 * N_STATE), F32)],
        compiler_params=_params(("arbitrary",)),
        name="ssm",
    )(u, bmat, a_re8, a_im8, cmat, d_skip, w_glu, g_out)


def _attn_kernel(qi_ref, ki_ref, lam_ref, q_ref, k_ref, v_ref, g_ref, o_ref, qs_ref, m_ref, acc_ref, *,
                 tile, out_scale):
    qi = qi_ref[pl.program_id(2)]
    ki = ki_ref[pl.program_id(2)]

    @pl.when(ki == 0)
    def _():
        q = q_ref[0]
        lane = lax.broadcasted_iota(I32, q.shape, 1)
        zero = jnp.zeros_like(q)
        qs_ref[:tile] = jnp.where(lane < DA_QKDIM, q, zero)
        qs_ref[tile:] = jnp.where(lane >= DA_QKDIM, q, zero)
        m_ref[...] = jnp.full_like(m_ref, -jnp.inf)
        acc_ref[...] = jnp.zeros_like(acc_ref)

    def update(masked):
        v_ext = jnp.concatenate([v_ref[0], jnp.ones((tile, DA_VDIM), BF16)], axis=1)
        rb = min(ATT_ROWS, tile)
        n_rb = 2 * tile // rb

        def n_keys(r):
            return (r * rb) % tile + rb if masked else tile

        def scores(r):
            return lax.dot_general(qs_ref[r * rb:(r + 1) * rb], k_ref[0, :n_keys(r)], (((1,), (1,)), ((), ())),
                                   preferred_element_type=F32)

        s_next = scores(0)
        for r in range(n_rb):
            s = s_next
            if r + 1 < n_rb:
                s_next = scores(r + 1)
            rows = slice(r * rb, (r + 1) * rb)
            nk = n_keys(r)
            if masked:
                tri = lax.broadcasted_iota(I32, (rb, rb), 1) <= lax.broadcasted_iota(I32, (rb, rb), 0)
                edge = jnp.where(tri, s[:, nk - rb:], jnp.finfo(F32).min)
                s = edge if nk == rb else jnp.concatenate([s[:, :nk - rb], edge], axis=1)
            m_old = m_ref[rows]
            m_new = jnp.maximum(m_old, jnp.max(s, axis=1, keepdims=True))
            p = jnp.exp(s - jnp.concatenate([m_new] * (nk // DA_VDIM), axis=1))
            alpha = jnp.exp(m_old - m_new)
            acc_ref[rows] = (jnp.concatenate([alpha, alpha], axis=1) * acc_ref[rows]
                             + jnp.dot(p.astype(BF16), v_ext[:nk], preferred_element_type=F32))
            m_ref[rows] = m_new

    @pl.when(ki < qi)
    def _():
        update(False)

    @pl.when(ki == qi)
    def _():
        update(True)
        o = acc_ref[:, :DA_VDIM] / acc_ref[:, DA_VDIM:]
        o = o[:tile] - lam_ref[0] * o[tile:]
        o_ref[0] = (_rms(o, g_ref[...]) * out_scale).astype(BF16)


def _diff_attention(q, k, v, lam, g_sub, lambda_init):
    b, l, _ = q.shape
    t = min(ATT_TILE, l)
    n = l // t
    pairs = [(qi, ki) for qi in range(n) for ki in range(qi + 1)]
    qi_tab = jnp.asarray([p[0] for p in pairs], I32)
    ki_tab = jnp.asarray([p[1] for p in pairs], I32)
    qspec = pl.BlockSpec((1, t, DA_VDIM), lambda bi, h, s, qt, kt: (bi, qt[s], h))
    kspec = pl.BlockSpec((1, t, DA_VDIM), lambda bi, h, s, qt, kt: (bi, kt[s], h))
    return pl.pallas_call(
        functools.partial(_attn_kernel, tile=t, out_scale=1.0 - lambda_init),
        grid_spec=pltpu.PrefetchScalarGridSpec(
            num_scalar_prefetch=2,
            grid=(b, DA_HEADS, len(pairs)),
            in_specs=[pl.BlockSpec(memory_space=pltpu.SMEM), qspec, kspec, kspec,
                      pl.BlockSpec((1, DA_VDIM), lambda bi, h, s, qt, kt: (0, 0))],
            out_specs=qspec,
            scratch_shapes=[pltpu.VMEM((2 * t, DA_VDIM), BF16), pltpu.VMEM((2 * t, DA_VDIM), F32),
                            pltpu.VMEM((2 * t, 2 * DA_VDIM), F32)]),
        out_shape=jax.ShapeDtypeStruct((b, l, ATTN_WIDTH), BF16),
        compiler_params=_params(("parallel", "parallel", "arbitrary")),
        name="diffattn",
    )(qi_tab, ki_tab, lam, q, k, v, g_sub)


def _memkv_kernel(m_ref, g_ref, wk_ref, wv_ref, k_ref, v_ref):
    a = _rms(m_ref[0], g_ref[...]).astype(BF16)
    k_ref[0] = jnp.dot(a, wk_ref[...], preferred_element_type=F32).astype(BF16)
    v_ref[0] = jnp.dot(a, wv_ref[...], preferred_element_type=F32).astype(BF16)


def _memkv(mem, g_mem, wk, wv):
    b, m, d = mem.shape
    blk = pl.BlockSpec((1, m, d), lambda i: (i, 0, 0))
    return pl.pallas_call(
        _memkv_kernel,
        grid=(b,),
        in_specs=[blk, _full((1, d)), _full((d, d)), _full((d, d))],
        out_specs=[blk, blk],
        out_shape=[jax.ShapeDtypeStruct((b, m, d), BF16)] * 2,
        compiler_params=_params(("parallel",)),
        name="memkv",
    )(mem, g_mem, wk, wv)


def _mix_kernel(x_ref, ys_ref, ya_ref, wo1_ref, wo2_ref, gx_ref, wq_ref, km_ref, vm_ref, wo_ref, h_ref):
    h = (x_ref[0]
         + jnp.dot(ys_ref[0], wo1_ref[...], preferred_element_type=F32)
         + jnp.dot(ya_ref[0], wo2_ref[...], preferred_element_type=F32))
    hq = _rms(h, gx_ref[...]).astype(BF16)
    q = jnp.dot(hq, wq_ref[...], preferred_element_type=F32).astype(BF16)
    outs = []
    for hd in range(X_HEADS):
        cols = slice(hd * X_HEAD_DIM, (hd + 1) * X_HEAD_DIM)
        s = lax.dot_general(q[:, cols], km_ref[0, :, cols], (((1,), (1,)), ((), ())),
                            preferred_element_type=F32) * (X_HEAD_DIM ** -0.5)
        s = s - jnp.max(s, axis=1, keepdims=True)
        p = jnp.exp(s)
        p = p / jnp.sum(p, axis=1, keepdims=True)
        outs.append(jnp.dot(p.astype(BF16), vm_ref[0, :, cols], preferred_element_type=F32).astype(BF16))
    o = jnp.concatenate(outs, axis=1)
    h_ref[0] = h + jnp.dot(o, wo_ref[...], preferred_element_type=F32)


def _mix(x, y_ssm, y_att, wo1, wo2, g_x, wq, kmem, vmem, wo):
    b, l, d = x.shape
    t = min(MIX_TILE, l)
    m = kmem.shape[1]
    row = lambda w: pl.BlockSpec((1, t, w), lambda i, j: (i, j, 0))
    mem = pl.BlockSpec((1, m, d), lambda i, j: (i, 0, 0))
    return pl.pallas_call(
        _mix_kernel,
        grid=(b, l // t),
        in_specs=[row(d), row(SSM_WIDTH), row(ATTN_WIDTH), _full((SSM_WIDTH, d)), _full((ATTN_WIDTH, d)),
                  _full((1, d)), _full((d, d)), mem, mem, _full((d, d))],
        out_specs=row(d),
        out_shape=jax.ShapeDtypeStruct((b, l, d), F32),
        compiler_params=_params(("parallel", "parallel")),
        name="mix",
    )(x, y_ssm, y_att, wo1, wo2, g_x, wq, kmem, vmem, wo)


def _first_index(hit, idx, sentinel):
    return jnp.min(jnp.where(hit, idx, sentinel), axis=0, keepdims=True)


def _route_kernel(h_ref, g_ref, wr_ref, bias_ref, wsgu_ref, wsd_ref, tri_ref,
                  t_ref, base_ref, eidx_ref, gw_ref, rank_ref, cnt_ref, carry_ref):
    @pl.when(pl.program_id(0) == 0)
    def _():
        carry_ref[...] = jnp.zeros_like(carry_ref)

    h = h_ref[...]
    t = _rms(h, g_ref[...])
    tb = t.astype(BF16)
    t_ref[...] = _pack_rows(t)
    gu = jnp.dot(tb, wsgu_ref[...], preferred_element_type=F32)
    hid = jax.nn.silu(gu[:, :D_EXPERT]) * gu[:, D_EXPERT:]
    base_ref[...] = h + jnp.dot(hid.astype(BF16), wsd_ref[...], preferred_element_type=F32)

    nt = (((1,), (1,)), ((), ()))
    t_lo = (t - tb.astype(F32)).astype(BF16)
    both = lax.dot_general(wr_ref[...], tb, nt, preferred_element_type=F32)
    logits = (both[:N_EXPERTS] + both[N_EXPERTS:]
              + lax.dot_general(wr_ref[:N_EXPERTS], t_lo, nt, preferred_element_type=F32))
    scores = jax.nn.sigmoid(logits)
    biased = scores + bias_ref[...]
    n_tok = scores.shape[1]
    neg = -jnp.inf
    sub = lax.broadcasted_iota(I32, (PER_GROUP, n_tok), 0)

    gs = []
    for g in range(N_EXPERT_GROUPS):
        blk = biased[g * PER_GROUP:(g + 1) * PER_GROUP]
        m1 = jnp.max(blk, axis=0, keepdims=True)
        i1 = _first_index(blk == m1, sub, PER_GROUP)
        m2 = jnp.max(jnp.where(sub == i1, neg, blk), axis=0, keepdims=True)
        gs.append(m1 + m2)
    gs = jnp.concatenate(gs, axis=0)

    gsel = jnp.zeros(gs.shape, jnp.bool_)
    for _ in range(TOPK_GROUPS):
        m = jnp.max(gs, axis=0, keepdims=True)
        hit = sub == _first_index(gs == m, sub, N_EXPERT_GROUPS)
        gsel = jnp.logical_or(gsel, hit)
        gs = jnp.where(hit, neg, gs)

    masked = jnp.concatenate(
        [jnp.where(gsel[g:g + 1], biased[g * PER_GROUP:(g + 1) * PER_GROUP], neg)
         for g in range(N_EXPERT_GROUPS)], axis=0)
    eid = lax.broadcasted_iota(I32, masked.shape, 0)
    sel = jnp.zeros(masked.shape, jnp.bool_)
    idxs, gws = [], []
    for _ in range(TOP_K):
        m = jnp.max(masked, axis=0, keepdims=True)
        i = _first_index(masked == m, eid, N_EXPERTS)
        hit = eid == i
        idxs.append(i)
        gws.append(jnp.sum(jnp.where(hit, scores, 0.0), axis=0, keepdims=True))
        sel = jnp.logical_or(sel, hit)
        masked = jnp.where(hit, neg, masked)
    eidx = jnp.concatenate(idxs, axis=0)
    gw = jnp.concatenate(gws, axis=0)
    gw = gw / jnp.sum(gw, axis=0, keepdims=True) * ROUTED_SCALE
    eidx_ref[...] = eidx
    gw_ref[...] = gw

    before = jnp.dot(sel.astype(BF16), tri_ref[...], preferred_element_type=F32) + carry_ref[...]
    rank_ref[...] = jnp.concatenate(
        [jnp.sum(jnp.where(eid == idxs[k], before, 0.0), axis=0, keepdims=True) for k in range(TOP_K)],
        axis=0).astype(I32)
    carry = carry_ref[...] + jnp.sum(sel.astype(F32), axis=1, keepdims=True)
    carry_ref[...] = carry
    cnt_ref[...] = carry.astype(I32)


def _route(h2, g_ffn, wr_t, bias_col, wsgu, wsd, part, n_parts):
    n_all, d = h2.shape
    n = n_all // n_parts
    t = min(ROUTE_TILE, n)
    blk0 = part * (n // t)
    tri = (lax.broadcasted_iota(I32, (t, t), 0) < lax.broadcasted_iota(I32, (t, t), 1)).astype(BF16)
    row = lambda w: pl.BlockSpec((t, w), lambda i: (i, 0))
    col = pl.BlockSpec((TOP_K, t), lambda i: (0, i))
    return pl.pallas_call(
        _route_kernel,
        grid=(n // t,),
        in_specs=[pl.BlockSpec((t, d), lambda i: (i + blk0, 0)), _full((1, d)), _full((2 * N_EXPERTS, d)),
                  _full((N_EXPERTS, 1)),
                  _full((d, 2 * D_EXPERT)), _full((D_EXPERT, d)), _full((t, t))],
        out_specs=[row(d // 2), row(d), col, col, col, _full((N_EXPERTS, 1))],
        out_shape=[jax.ShapeDtypeStruct((n, d // 2), I32), jax.ShapeDtypeStruct((n, d), F32),
                   jax.ShapeDtypeStruct((TOP_K, n), I32), jax.ShapeDtypeStruct((TOP_K, n), F32),
                   jax.ShapeDtypeStruct((TOP_K, n), I32), jax.ShapeDtypeStruct((N_EXPERTS, 1), I32)],
        scratch_shapes=[pltpu.VMEM((N_EXPERTS, 1), F32)],
        compiler_params=_params(("arbitrary",)),
        name="route",
    )(h2, g_ffn, wr_t, bias_col, wsgu, wsd, tri)


def _expert_kernel(be_ref, first_ref, slot_ref, next_ref, nu_ref, x_hbm, wg_hbm, wu_hbm, wd_hbm, y_ref,
                   wg_s, wu_s, wd_s, x_buf, x_sem, wg_buf, wu_buf, wd_buf, w_sem):
    i = pl.program_id(0)
    n_used = nu_ref[0]

    def fetch(step):
        slot = step % X_SLOTS
        rows = pl.ds(pl.multiple_of(step * MOE_TILE, MOE_TILE), MOE_TILE)
        return pltpu.make_async_copy(x_hbm.at[rows], x_buf.at[slot], x_sem.at[slot])

    def weight_fetch(expert, slot):
        pairs = ((wg_hbm, wg_buf), (wu_hbm, wu_buf), (wd_hbm, wd_buf))
        return [pltpu.make_async_copy(src.at[expert], buf.at[slot], w_sem.at[slot, j])
                for j, (src, buf) in enumerate(pairs)]

    @pl.when(i == 0)
    def _():
        for j in range(X_SLOTS - 1):
            @pl.when(j < n_used)
            def _(j=j):
                fetch(j).start()
        for cp in weight_fetch(be_ref[0], 0):
            cp.start()

    @pl.when(i + (X_SLOTS - 1) < n_used)
    def _():
        fetch(i + (X_SLOTS - 1)).start()

    @pl.when(first_ref[i] == 1)
    def _():
        slot = slot_ref[i]

        @pl.when(next_ref[i] >= 0)
        def _():
            for cp in weight_fetch(next_ref[i], 1 - slot):
                cp.start()

        for cp in weight_fetch(be_ref[i], slot):
            cp.wait()
        wg_s[...] = wg_buf[slot].astype(BF16)
        wu_s[...] = wu_buf[slot].astype(BF16)
        wd_s[...] = wd_buf[slot].astype(BF16)

    @pl.when(i < n_used)
    def _():
        fetch(i).wait()
        x = _unpack_rows(x_buf[i % X_SLOTS]).astype(BF16)
        gate = jnp.dot(x, wg_s[...], preferred_element_type=F32)
        up = jnp.dot(x, wu_s[...], preferred_element_type=F32)
        hid = (jax.nn.silu(gate) * up).astype(BF16)
        y_ref[...] = _pack_rows(jnp.dot(hid, wd_s[...], preferred_element_type=F32))


def _experts(schedule, xs, w_gate, w_up, w_down):
    block_e, run_first, run_slot, run_next, n_used = schedule
    n_slots, dw = xs.shape
    _, d, de = w_gate.shape
    nb = n_slots // MOE_TILE
    hbm = pl.BlockSpec(memory_space=pl.ANY)
    return pl.pallas_call(
        _expert_kernel,
        grid_spec=pltpu.PrefetchScalarGridSpec(
            num_scalar_prefetch=5,
            grid=(nb,),
            in_specs=[hbm, hbm, hbm, hbm],
            out_specs=pl.BlockSpec((MOE_TILE, dw), lambda i, be, fi, sl, nx, nu: (jnp.minimum(i, nu[0] - 1), 0)),
            scratch_shapes=[pltpu.VMEM((d, de), BF16), pltpu.VMEM((d, de), BF16), pltpu.VMEM((de, d), BF16),
                            pltpu.VMEM((X_SLOTS, MOE_TILE, dw), I32), pltpu.SemaphoreType.DMA((X_SLOTS,)),
                            pltpu.VMEM((2, d, de), F32), pltpu.VMEM((2, d, de), F32), pltpu.VMEM((2, de, d), F32),
                            pltpu.SemaphoreType.DMA((2, 3))]),
        out_shape=jax.ShapeDtypeStruct((n_slots, dw), I32),
        compiler_params=_params(("arbitrary",)),
        name="experts",
    )(block_e, run_first, run_slot, run_next, n_used, xs, w_gate, w_up, w_down)


def _sc_worker_id():
    return lax.axis_index("s") * SC_CORES + lax.axis_index("c")


def _sc_dispatch(t_rows, dest3, n_slots):
    _, dw = t_rows.shape
    n_chunks, _, w = dest3.shape
    per_worker = n_chunks // SC_WORKERS
    assert per_worker % 2 == 0
    mesh = plsc.VectorSubcoreMesh(core_axis_name="c", subcore_axis_name="s")
    dt = t_rows.dtype

    @functools.partial(
        pl.kernel, mesh=mesh,
        out_type=jax.ShapeDtypeStruct((n_slots, dw), dt),
        scratch_types=[pltpu.VMEM((TOP_K, w), I32), pltpu.VMEM((TOP_K, w), I32),
                       pltpu.VMEM((w, dw), dt), pltpu.VMEM((w, dw), dt),
                       pltpu.SemaphoreType.DMA, pltpu.SemaphoreType.DMA, pltpu.SemaphoreType.DMA],
    )
    def scatter_rows(t_hbm, dest_hbm, xs_hbm, idx_a, idx_b, rows_a, rows_b, sem_load, sem_a, sem_b):
        wid = _sc_worker_id()

        def scatter(idx_v, rows_v, sem):
            return [pltpu.async_copy(rows_v, xs_hbm.at[idx_v.at[k]], sem) for k in range(TOP_K)]

        @pl.loop(0, per_worker, step=2)
        def _(j):
            ca = wid * per_worker + j
            cb = ca + 1
            pltpu.sync_copy(dest_hbm.at[ca], idx_a)
            pltpu.sync_copy(t_hbm.at[pl.ds(ca * w, w)], rows_a)
            load_idx = pltpu.async_copy(dest_hbm.at[cb], idx_b, sem_load)
            load_rows = pltpu.async_copy(t_hbm.at[pl.ds(cb * w, w)], rows_b, sem_load)
            out_a = scatter(idx_a, rows_a, sem_a)
            load_idx.wait()
            load_rows.wait()
            out_b = scatter(idx_b, rows_b, sem_b)
            for cp in out_a + out_b:
                cp.wait()

    return scatter_rows(t_rows, dest3)


def _sc_combine(y_rows, dest3):
    _, dw = y_rows.shape
    n_chunks, _, w = dest3.shape
    per_worker = n_chunks // SC_WORKERS
    mesh = plsc.VectorSubcoreMesh(core_axis_name="c", subcore_axis_name="s")
    dt = y_rows.dtype

    @functools.partial(
        pl.kernel, mesh=mesh,
        out_type=jax.ShapeDtypeStruct((TOP_K, n_chunks * w, dw), dt),
        scratch_types=[pltpu.VMEM((TOP_K, w), I32), pltpu.VMEM((w, dw), dt), pltpu.VMEM((w, dw), dt),
                       pltpu.SemaphoreType.DMA, pltpu.SemaphoreType.DMA,
                       pltpu.SemaphoreType.DMA, pltpu.SemaphoreType.DMA],
    )
    def gather_rows(y_hbm, dest_hbm, out_hbm, idx_v, buf0, buf1, gsem0, gsem1, wsem0, wsem1):
        wid = _sc_worker_id()
        bufs, gsems, wsems = (buf0, buf1), (gsem0, gsem1), (wsem0, wsem1)

        @pl.loop(0, per_worker)
        def _(j):
            c = wid * per_worker + j
            pltpu.sync_copy(dest_hbm.at[c], idx_v)
            gathers = [None] * TOP_K
            writes = [None] * TOP_K
            gathers[0] = pltpu.async_copy(y_hbm.at[idx_v.at[0]], bufs[0], gsems[0])
            for k in range(TOP_K):
                b = k % 2
                if k + 1 < TOP_K:
                    if k >= 1:
                        writes[k - 1].wait()
                    gathers[k + 1] = pltpu.async_copy(y_hbm.at[idx_v.at[k + 1]], bufs[1 - b], gsems[1 - b])
                gathers[k].wait()
                writes[k] = pltpu.async_copy(bufs[b], out_hbm.at[k, pl.ds(c * w, w)], wsems[b])
            writes[TOP_K - 2].wait()
            writes[TOP_K - 1].wait()

    return gather_rows(y_rows, dest3)


def _final_kernel(base_ref, y_ref, gw_ref, g_ref, *rest, normalize):
    o_ref = rest[-1]
    h = base_ref[...]
    gw = gw_ref[...]
    for k in range(TOP_K):
        h = h + gw[:, k:k + 1] * _unpack_rows(y_ref[k])
    o_ref[...] = _rms(h, g_ref[...]) if normalize else h


def _final(base, yg, gw_rows, g_final, normalize, out_prev, part, n_parts):
    n, d = base.shape
    t = min(FIN_TILE, n)
    blk0 = part * (n // t)
    in_specs = [pl.BlockSpec((t, d), lambda i: (i, 0)), pl.BlockSpec((TOP_K, t, d // 2), lambda i: (0, i, 0)),
                pl.BlockSpec((t, TOP_K), lambda i: (i, 0)), _full((1, d))]
    args = [base, yg, gw_rows, g_final]
    aliases = {}
    if out_prev is not None:
        in_specs.append(pl.BlockSpec(memory_space=pl.ANY))
        args.append(out_prev)
        aliases = {len(args) - 1: 0}
    return pl.pallas_call(
        functools.partial(_final_kernel, normalize=normalize),
        grid=(n // t,),
        in_specs=in_specs,
        out_specs=pl.BlockSpec((t, d), lambda i: (i + blk0, 0)),
        out_shape=jax.ShapeDtypeStruct((n * n_parts, d), F32),
        input_output_aliases=aliases,
        compiler_params=_params(("parallel",)),
        name="final",
    )(*args)


def _ssm_matrices(a_re, a_im, log_dt, b_re, b_im, c_re, c_im):
    lr, li = a_re.astype(F32), a_im.astype(F32)
    dt = jnp.exp(log_dt.astype(F32))[:, None]
    mag = jnp.exp(lr * dt)
    ab_re, ab_im = mag * jnp.cos(li * dt), mag * jnp.sin(li * dt)
    den = lr * lr + li * li
    zr, zi = ab_re - 1.0, ab_im
    k_re = (zr * lr + zi * li) / den
    k_im = (zi * lr - zr * li) / den
    br, bi = b_re.astype(F32), b_im.astype(F32)
    bb_re = k_re[..., None] * br - k_im[..., None] * bi
    bb_im = k_re[..., None] * bi + k_im[..., None] * br
    eye = jnp.eye(SSM_GROUPS, dtype=F32)

    def in_mat(bb):
        return jnp.einsum("gpc,gh->gchp", bb, eye).reshape(SSM_WIDTH, N_STATE)

    def out_mat(c):
        return jnp.einsum("gcp,gh->gphc", c.astype(F32), eye).reshape(N_STATE, SSM_WIDTH)

    bmat = jnp.concatenate([in_mat(bb_re), in_mat(bb_im)], axis=1)
    cmat = jnp.concatenate([out_mat(c_re), -out_mat(c_im)], axis=0)
    tile = lambda a: jnp.broadcast_to(a.reshape(1, N_STATE), (SUBLANES, N_STATE))
    return bmat.astype(BF16), tile(ab_re), tile(ab_im), cmat.astype(BF16)


def _plan_kernel(pstart_ref, eidx_ref, rank_ref, dest_ref, *, window):
    eidx = eidx_ref[...]
    dest = rank_ref[...]
    for e in range(N_EXPERTS):
        dest = dest + jnp.where(eidx == e, pstart_ref[e], 0)
    for c in range(dest.shape[1] // window):
        dest_ref[c] = dest[:, c * window:(c + 1) * window]


def _plan(pstart, eidx, rank, window):
    k, n = eidx.shape
    t = min(PLAN_TILE, n)
    cols = pl.BlockSpec((k, t), lambda i, ps: (0, i))
    return pl.pallas_call(
        functools.partial(_plan_kernel, window=window),
        grid_spec=pltpu.PrefetchScalarGridSpec(
            num_scalar_prefetch=1,
            grid=(n // t,),
            in_specs=[cols, cols],
            out_specs=pl.BlockSpec((t // window, k, window), lambda i, ps: (i, 0, 0))),
        out_shape=jax.ShapeDtypeStruct((n // window, k, window), I32),
        compiler_params=_params(("parallel",)),
        name="plan",
    )(pstart, eidx, rank)


def _block_schedule(counts, n_tok):
    padded = ((counts + MOE_TILE - 1) // MOE_TILE) * MOE_TILE
    pend = jnp.cumsum(padded)
    pstart = (pend - padded).astype(I32)
    n_slots = n_tok * TOP_K + N_EXPERTS * MOE_TILE
    nb = n_slots // MOE_TILE
    n_used = (pend[-1] // MOE_TILE).astype(I32)
    blk = jnp.arange(nb, dtype=I32)
    block_e = jnp.sum((pend[None, :] <= (jnp.minimum(blk, n_used - 1) * MOE_TILE)[:, None]).astype(I32), axis=1)
    block_e = jnp.minimum(block_e, N_EXPERTS - 1)
    eid = jnp.arange(N_EXPERTS, dtype=I32)
    active = padded > 0
    run_of = jnp.cumsum(active.astype(I32)) - 1
    later = jnp.where(jnp.logical_and(active[None, :], eid[None, :] > eid[:, None]), eid[None, :], N_EXPERTS)
    next_e = jnp.min(later, axis=1)
    next_e = jnp.where(next_e < N_EXPERTS, next_e, -1)
    onehot = block_e[:, None] == eid[None, :]
    lookup = lambda table: jnp.sum(jnp.where(onehot, table[None, :], 0), axis=1).astype(I32)
    run_first = jnp.logical_and(blk * MOE_TILE == lookup(pstart), blk < n_used).astype(I32)
    schedule = (block_e, run_first, lookup(run_of % 2), lookup(next_e), n_used.reshape(1))
    return pstart, schedule, n_slots


def kernel(x, mem, positions, g_mix, w_in, a_re, a_im, log_dt, b_re, b_im, c_re, c_im, d_skip, w_glu, g_ssm_out, lam_q1, lam_k1, lam_q2, lam_k2, g_sub, w_out, g_x, g_mem, wq_x, wk_x, wv_x, wo_x, g_ffn, w_router, router_bias, w_gate, w_up, w_down, ws_gate, ws_up, ws_down, g_final):
    b, l, d = x.shape
    n = b * l
    depth = w_in.shape[0]
    row = lambda a: a.reshape(1, -1).astype(F32)
    inv = ROPE_THETA ** (-jnp.arange(0, DA_QKDIM, 2, dtype=F32) / DA_QKDIM)
    inv_row = jnp.tile(inv, 128 // inv.shape[0]).reshape(1, 128)
    pos3 = positions.reshape(b, l, 1)

    h = x
    for i in range(depth):
        lambda_init = 0.8 - 0.6 * math.exp(-0.3 * i)
        u, q, k, v = _inproj(h, pos3, row(g_mix[i]), w_in[i].astype(BF16), inv_row)

        bmat, are8, aim8, cmat = _ssm_matrices(a_re[i], a_im[i], log_dt[i], b_re[i], b_im[i], c_re[i], c_im[i])
        y_ssm = _ssm(u, bmat, are8, aim8, cmat, row(d_skip[i]), w_glu[i].astype(BF16), row(g_ssm_out[i]))

        lam = (jnp.exp(jnp.sum(lam_q1[i].astype(F32) * lam_k1[i].astype(F32)))
               - jnp.exp(jnp.sum(lam_q2[i].astype(F32) * lam_k2[i].astype(F32))) + lambda_init).reshape(1)
        y_att = _diff_attention(q, k, v, lam.astype(F32), row(g_sub[i]), lambda_init)

        kmem, vmem = _memkv(mem, row(g_mem[i]), wk_x[i].astype(BF16), wv_x[i].astype(BF16))
        wo = w_out[i].astype(BF16)
        h2 = _mix(h, y_ssm, y_att, wo[:SSM_WIDTH], wo[SSM_WIDTH:], row(g_x[i]), wq_x[i].astype(BF16),
                  kmem, vmem, wo_x[i].astype(BF16))

        wsgu = jnp.concatenate([ws_gate[i], ws_up[i]], axis=1).astype(BF16)
        wr = w_router[i].T.astype(F32)
        wr_hi = wr.astype(BF16)
        wr_split = jnp.concatenate([wr_hi, (wr - wr_hi.astype(F32)).astype(BF16)], axis=0)
        n_part = n // MOE_PARTS
        out = None
        for part in range(MOE_PARTS):
            t_pk, base, eidx, gw, rank, counts = _route(
                h2.reshape(n, d), row(g_ffn[i]), wr_split,
                router_bias[i].reshape(N_EXPERTS, 1).astype(F32), wsgu, ws_down[i].astype(BF16), part, MOE_PARTS)
            pstart, schedule, n_slots = _block_schedule(counts[:, 0], n_part)
            dest3 = _plan(pstart, eidx, rank, SC_WINDOW)
            xs = _sc_dispatch(t_pk, dest3, n_slots)
            ys = _experts(schedule, xs, w_gate[i], w_up[i], w_down[i])
            yg = _sc_combine(ys, dest3)
            out = _final(base, yg, gw.T, row(g_final), i == depth - 1, out, part, MOE_PARTS)
        h = out.reshape(b, l, d)
    return h
```

```python
import functools
import math

import jax
import jax.numpy as jnp
from jax import lax
from jax.experimental import pallas as pl
from jax.experimental.pallas import tpu as pltpu
from jax.experimental.pallas import tpu_sc as plsc

F32 = jnp.float32
BF16 = jnp.bfloat16
I32 = jnp.int32

D_MODEL = 1024
SSM_WIDTH = 512
ATTN_WIDTH = 512
SSM_GROUP = 16
SSM_GROUPS = 32
SSM_STATE = 64
N_STATE = SSM_GROUPS * SSM_STATE
DA_HEADS = 4
DA_VDIM = 128
DA_QKDIM = 64
ROPE_THETA = 10000.0
X_HEADS = 4
X_HEAD_DIM = 256
N_EXPERTS = 64
TOP_K = 8
N_EXPERT_GROUPS = 8
PER_GROUP = N_EXPERTS // N_EXPERT_GROUPS
TOPK_GROUPS = 4
D_EXPERT = 256
ROUTED_SCALE = 2.5
EPS = 1e-6

VMEM_LIMIT_V7X = 56 * 1024 * 1024
SUBLANES = 8
LANES = 128
HALF_WORD_BITS = 16
HIGH_HALF_MASK = -(1 << HALF_WORD_BITS)

IN_TILE = 1024
SSM_STEPS = 128
SSM_COLS = 1024
SSM_CH = 128
SSM_SLAB = 512
ATT_TILE = 2048
ATT_ROWS = 256
MIX_TILE = 1024
ROUTE_TILE = 1024
MOE_TILE = 512
X_SLOTS = 3
PLAN_TILE = 2048
MOE_PARTS = 2
FIN_TILE = 512

SC_CORES = 2
SC_WORKERS = 32
SC_WINDOW = 64


def _params(sem):
    return pltpu.CompilerParams(dimension_semantics=sem, vmem_limit_bytes=VMEM_LIMIT_V7X)


def _rms(x, g):
    return x * lax.rsqrt(jnp.mean(x * x, axis=-1, keepdims=True) + EPS) * g


def _full(shape):
    return pl.BlockSpec(shape, lambda *_: (0,) * len(shape))


def _pack_rows(a):
    w = a.shape[1] // 2
    bits = lambda v: lax.bitcast_convert_type(v.astype(BF16).astype(F32), I32)
    return (bits(a[:, w:]) & jnp.int32(HIGH_HALF_MASK)) | lax.shift_right_logical(bits(a[:, :w]), HALF_WORD_BITS)


def _unpack_rows(p):
    lo = lax.bitcast_convert_type(lax.shift_left(p, HALF_WORD_BITS), F32)
    hi = lax.bitcast_convert_type(p & jnp.int32(HIGH_HALF_MASK), F32)
    return jnp.concatenate([lo, hi], axis=1)


def _inproj_kernel(x_ref, pos_ref, g_ref, w_ref, inv_ref, u_ref, q_ref, k_ref, v_ref):
    x = x_ref[0]
    a = _rms(x, g_ref[...]).astype(BF16)
    z = jnp.dot(a, w_ref[...], preferred_element_type=F32)
    u_ref[0] = z[:, :SSM_WIDTH]
    ang = pos_ref[0].astype(F32) * inv_ref[...]
    cos = jnp.concatenate([jnp.cos(ang)] * (ATTN_WIDTH // LANES), axis=1)
    sin = jnp.concatenate([jnp.sin(ang)] * (ATTN_WIDTH // LANES), axis=1)
    lane = lax.broadcasted_iota(I32, cos.shape, 1)
    first = (lane & (DA_QKDIM - 1)) < DA_QKDIM // 2
    half = DA_QKDIM // 2

    def rope(t):
        rot = jnp.where(first, -pltpu.roll(t, ATTN_WIDTH - half, 1), pltpu.roll(t, half, 1))
        return t * cos + rot * sin

    q = z[:, SSM_WIDTH:SSM_WIDTH + ATTN_WIDTH]
    k = z[:, SSM_WIDTH + ATTN_WIDTH:SSM_WIDTH + 2 * ATTN_WIDTH]
    q_ref[0] = (rope(q) * (DA_QKDIM ** -0.5)).astype(BF16)
    k_ref[0] = rope(k).astype(BF16)
    v_ref[0] = z[:, SSM_WIDTH + 2 * ATTN_WIDTH:].astype(BF16)


def _inproj(x, pos3, g_mix, w_in, inv_row):
    b, l, d = x.shape
    t = min(IN_TILE, l)
    n_out = w_in.shape[1]
    row = lambda w: pl.BlockSpec((1, t, w), lambda i, j: (i, j, 0))
    return pl.pallas_call(
        _inproj_kernel,
        grid=(b, l // t),
        in_specs=[row(d), row(1), _full((1, d)), _full((d, n_out)), _full((1, LANES))],
        out_specs=[row(SSM_WIDTH), row(ATTN_WIDTH), row(ATTN_WIDTH), row(ATTN_WIDTH)],
        out_shape=[jax.ShapeDtypeStruct((b, l, SSM_WIDTH), F32)]
        + [jax.ShapeDtypeStruct((b, l, ATTN_WIDTH), BF16)] * 3,
        compiler_params=_params(("parallel", "parallel")),
        name="inproj",
    )(x, pos3, g_mix, w_in, inv_row)


def _ssm_kernel(u_ref, bm_ref, are_ref, aim_ref, cm_ref, dskip_ref, wglu_ref, g_ref, o_ref, bu_ref, st_ref,
                *, steps):
    @pl.when(pl.program_id(0) == 0)
    def _():
        st_ref[...] = jnp.zeros_like(st_ref)

    u = pltpu.einshape("bts->tbs", u_ref[...]).reshape(steps * SUBLANES, SSM_WIDTH)
    ub = u.astype(BF16)

    n_slabs = SSM_WIDTH // SSM_CH
    for s in range(n_slabs):
        ch = slice(s * SSM_CH, (s + 1) * SSM_CH)
        for part in (0, N_STATE):
            cols = slice(part + s * SSM_SLAB, part + (s + 1) * SSM_SLAB)
            bu_ref[:, cols] = jnp.dot(ub[:, ch], bm_ref[ch, cols], preferred_element_type=F32)

    for c0 in range(0, N_STATE, SSM_COLS):
        re = slice(c0, c0 + SSM_COLS)
        im = slice(N_STATE + c0, N_STATE + c0 + SSM_COLS)
        ar = are_ref[:, re]
        ai = aim_ref[:, re]

        sr, si = st_ref[:, re], st_ref[:, im]
        for t in range(steps):
            rows = slice(t * SUBLANES, (t + 1) * SUBLANES)
            sr, si = ar * sr - ai * si + bu_ref[rows, re], ar * si + ai * sr + bu_ref[rows, im]
            bu_ref[rows, re] = sr
            bu_ref[rows, im] = si
        st_ref[:, re] = sr
        st_ref[:, im] = si

    ys = []
    for s in range(n_slabs):
        ch = slice(s * SSM_CH, (s + 1) * SSM_CH)
        acc = None
        for part in (0, N_STATE):
            cols = slice(part + s * SSM_SLAB, part + (s + 1) * SSM_SLAB)
            d = jnp.dot(bu_ref[:, cols].astype(BF16), cm_ref[cols, ch], preferred_element_type=F32)
            acc = d if acc is None else acc + d
        ys.append(acc)
    y = jnp.concatenate(ys, axis=1) + dskip_ref[...] * u
    y = jax.nn.gelu(y)
    y = y * jax.nn.sigmoid(jnp.dot(y.astype(BF16), wglu_ref[...], preferred_element_type=F32))
    out = _rms(y, g_ref[...]).reshape(steps, SUBLANES, SSM_WIDTH)
    o_ref[...] = pltpu.einshape("tbs->bts", out).astype(BF16)


def _ssm(u, bmat, a_re8, a_im8, cmat, d_skip, w_glu, g_out):
    batch, l, _ = u.shape
    assert batch == SUBLANES, "one time step of all sequences must fill the sublanes"
    steps = min(SSM_STEPS, l)
    rows = steps * batch
    seq = pl.BlockSpec((batch, steps, SSM_WIDTH), lambda i: (0, i, 0))
    return pl.pallas_call(
        functools.partial(_ssm_kernel, steps=steps),
        grid=(l // steps,),
        in_specs=[seq,
                  _full((SSM_WIDTH, 2 * N_STATE)), _full((SUBLANES, N_STATE)), _full((SUBLANES, N_STATE)),
                  _full((2 * N_STATE, SSM_WIDTH)), _full((1, SSM_WIDTH)), _full((SSM_WIDTH, SSM_WIDTH)),
                  _full((1, SSM_WIDTH))],
        out_specs=seq,
        out_shape=jax.ShapeDtypeStruct((batch, l, SSM_WIDTH), BF16),
        scratch_shapes=[pltpu.VMEM((rows, 2 * N_STATE), F32), pltpu.VMEM((SUBLANES, 2 * N_STATE), F32)],
        compiler_params=_params(("arbitrary",)),
        name="ssm",
    )(u, bmat, a_re8, a_im8, cmat, d_skip, w_glu, g_out)


def _attn_kernel(qi_ref, ki_ref, lam_ref, q_ref, k_ref, v_ref, g_ref, o_ref, qs_ref, m_ref, acc_ref, *,
                 tile, out_scale):
    qi = qi_ref[pl.program_id(2)]
    ki = ki_ref[pl.program_id(2)]

    @pl.when(ki == 0)
    def _():
        q = q_ref[0]
        lane = lax.broadcasted_iota(I32, q.shape, 1)
        zero = jnp.zeros_like(q)
        qs_ref[:tile] = jnp.where(lane < DA_QKDIM, q, zero)
        qs_ref[tile:] = jnp.where(lane >= DA_QKDIM, q, zero)
        m_ref[...] = jnp.full_like(m_ref, -jnp.inf)
        acc_ref[...] = jnp.zeros_like(acc_ref)

    def update(masked):
        v_ext = jnp.concatenate([v_ref[0], jnp.ones((tile, DA_VDIM), BF16)], axis=1)
        rb = min(ATT_ROWS, tile)
        n_rb = 2 * tile // rb

        def n_keys(r):
            return (r * rb) % tile + rb if masked else tile

        def scores(r):
            return lax.dot_general(qs_ref[r * rb:(r + 1) * rb], k_ref[0, :n_keys(r)], (((1,), (1,)), ((), ())),
                                   preferred_element_type=F32)

        s_next = scores(0)
        for r in range(n_rb):
            s = s_next
            if r + 1 < n_rb:
                s_next = scores(r + 1)
            rows = slice(r * rb, (r + 1) * rb)
            nk = n_keys(r)
            if masked:
                tri = lax.broadcasted_iota(I32, (rb, rb), 1) <= lax.broadcasted_iota(I32, (rb, rb), 0)
                edge = jnp.where(tri, s[:, nk - rb:], jnp.finfo(F32).min)
                s = edge if nk == rb else jnp.concatenate([s[:, :nk - rb], edge], axis=1)
            m_old = m_ref[rows]
            m_new = jnp.maximum(m_old, jnp.max(s, axis=1, keepdims=True))
            p = jnp.exp(s - jnp.concatenate([m_new] * (nk // DA_VDIM), axis=1))
            alpha = jnp.exp(m_old - m_new)
            acc_ref[rows] = (jnp.concatenate([alpha, alpha], axis=1) * acc_ref[rows]
                             + jnp.dot(p.astype(BF16), v_ext[:nk], preferred_element_type=F32))
            m_ref[rows] = m_new

    @pl.when(ki < qi)
    def _():
        update(False)

    @pl.when(ki == qi)
    def _():
        update(True)
        o = acc_ref[:, :DA_VDIM] / acc_ref[:, DA_VDIM:]
        o = o[:tile] - lam_ref[0] * o[tile:]
        o_ref[0] = (_rms(o, g_ref[...]) * out_scale).astype(BF16)


def _diff_attention(q, k, v, lam, g_sub, lambda_init):
    b, l, _ = q.shape
    t = min(ATT_TILE, l)
    n = l // t
    pairs = [(qi, ki) for qi in range(n) for ki in range(qi + 1)]
    qi_tab = jnp.asarray([p[0] for p in pairs], I32)
    ki_tab = jnp.asarray([p[1] for p in pairs], I32)
    qspec = pl.BlockSpec((1, t, DA_VDIM), lambda bi, h, s, qt, kt: (bi, qt[s], h))
    kspec = pl.BlockSpec((1, t, DA_VDIM), lambda bi, h, s, qt, kt: (bi, kt[s], h))
    return pl.pallas_call(
        functools.partial(_attn_kernel, tile=t, out_scale=1.0 - lambda_init),
        grid_spec=pltpu.PrefetchScalarGridSpec(
            num_scalar_prefetch=2,
            grid=(b, DA_HEADS, len(pairs)),
            in_specs=[pl.BlockSpec(memory_space=pltpu.SMEM), qspec, kspec, kspec,
                      pl.BlockSpec((1, DA_VDIM), lambda bi, h, s, qt, kt: (0, 0))],
            out_specs=qspec,
            scratch_shapes=[pltpu.VMEM((2 * t, DA_VDIM), BF16), pltpu.VMEM((2 * t, DA_VDIM), F32),
                            pltpu.VMEM((2 * t, 2 * DA_VDIM), F32)]),
        out_shape=jax.ShapeDtypeStruct((b, l, ATTN_WIDTH), BF16),
        compiler_params=_params(("parallel", "parallel", "arbitrary")),
        name="diffattn",
    )(qi_tab, ki_tab, lam, q, k, v, g_sub)


def _memkv_kernel(m_ref, g_ref, wk_ref, wv_ref, k_ref, v_ref):
    a = _rms(m_ref[0], g_ref[...]).astype(BF16)
    k_ref[0] = jnp.dot(a, wk_ref[...], preferred_element_type=F32).astype(BF16)
    v_ref[0] = jnp.dot(a, wv_ref[...], preferred_element_type=F32).astype(BF16)


def _memkv(mem, g_mem, wk, wv):
    b, m, d = mem.shape
    blk = pl.BlockSpec((1, m, d), lambda i: (i, 0, 0))
    return pl.pallas_call(
        _memkv_kernel,
        grid=(b,),
        in_specs=[blk, _full((1, d)), _full((d, d)), _full((d, d))],
        out_specs=[blk, blk],
        out_shape=[jax.ShapeDtypeStruct((b, m, d), BF16)] * 2,
        compiler_params=_params(("parallel",)),
        name="memkv",
    )(mem, g_mem, wk, wv)


def _mix_kernel(x_ref, ys_ref, ya_ref, wo1_ref, wo2_ref, gx_ref, wq_ref, km_ref, vm_ref, wo_ref, h_ref):
    h = (x_ref[0]
         + jnp.dot(ys_ref[0], wo1_ref[...], preferred_element_type=F32)
         + jnp.dot(ya_ref[0], wo2_ref[...], preferred_element_type=F32))
    hq = _rms(h, gx_ref[...]).astype(BF16)
    q = jnp.dot(hq, wq_ref[...], preferred_element_type=F32).astype(BF16)
    outs = []
    for hd in range(X_HEADS):
        cols = slice(hd * X_HEAD_DIM, (hd + 1) * X_HEAD_DIM)
        s = lax.dot_general(q[:, cols], km_ref[0, :, cols], (((1,), (1,)), ((), ())),
                            preferred_element_type=F32) * (X_HEAD_DIM ** -0.5)
        s = s - jnp.max(s, axis=1, keepdims=True)
        p = jnp.exp(s)
        p = p / jnp.sum(p, axis=1, keepdims=True)
        outs.append(jnp.dot(p.astype(BF16), vm_ref[0, :, cols], preferred_element_type=F32).astype(BF16))
    o = jnp.concatenate(outs, axis=1)
    h_ref[0] = h + jnp.dot(o, wo_ref[...], preferred_element_type=F32)


def _mix(x, y_ssm, y_att, wo1, wo2, g_x, wq, kmem, vmem, wo):
    b, l, d = x.shape
    t = min(MIX_TILE, l)
    m = kmem.shape[1]
    row = lambda w: pl.BlockSpec((1, t, w), lambda i, j: (i, j, 0))
    mem = pl.BlockSpec((1, m, d), lambda i, j: (i, 0, 0))
    return pl.pallas_call(
        _mix_kernel,
        grid=(b, l // t),
        in_specs=[row(d), row(SSM_WIDTH), row(ATTN_WIDTH), _full((SSM_WIDTH, d)), _full((ATTN_WIDTH, d)),
                  _full((1, d)), _full((d, d)), mem, mem, _full((d, d))],
        out_specs=row(d),
        out_shape=jax.ShapeDtypeStruct((b, l, d), F32),
        compiler_params=_params(("parallel", "parallel")),
        name="mix",
    )(x, y_ssm, y_att, wo1, wo2, g_x, wq, kmem, vmem, wo)


def _first_index(hit, idx, sentinel):
    return jnp.min(jnp.where(hit, idx, sentinel), axis=0, keepdims=True)


def _route_kernel(h_ref, g_ref, wr_ref, bias_ref, wsgu_ref, wsd_ref, tri_ref,
                  t_ref, base_ref, eidx_ref, gw_ref, rank_ref, cnt_ref, carry_ref):
    @pl.when(pl.program_id(0) == 0)
    def _():
        carry_ref[...] = jnp.zeros_like(carry_ref)

    h = h_ref[...]
    t = _rms(h, g_ref[...])
    tb = t.astype(BF16)
    t_ref[...] = _pack_rows(t)
    gu = jnp.dot(tb, wsgu_ref[...], preferred_element_type=F32)
    hid = jax.nn.silu(gu[:, :D_EXPERT]) * gu[:, D_EXPERT:]
    base_ref[...] = h + jnp.dot(hid.astype(BF16), wsd_ref[...], preferred_element_type=F32)

    nt = (((1,), (1,)), ((), ()))
    t_lo = (t - tb.astype(F32)).astype(BF16)
    both = lax.dot_general(wr_ref[...], tb, nt, preferred_element_type=F32)
    logits = (both[:N_EXPERTS] + both[N_EXPERTS:]
              + lax.dot_general(wr_ref[:N_EXPERTS], t_lo, nt, preferred_element_type=F32))
    scores = jax.nn.sigmoid(logits)
    biased = scores + bias_ref[...]
    n_tok = scores.shape[1]
    neg = -jnp.inf
    sub = lax.broadcasted_iota(I32, (PER_GROUP, n_tok), 0)

    gs = []
    for g in range(N_EXPERT_GROUPS):
        blk = biased[g * PER_GROUP:(g + 1) * PER_GROUP]
        m1 = jnp.max(blk, axis=0, keepdims=True)
        i1 = _first_index(blk == m1, sub, PER_GROUP)
        m2 = jnp.max(jnp.where(sub == i1, neg, blk), axis=0, keepdims=True)
        gs.append(m1 + m2)
    gs = jnp.concatenate(gs, axis=0)

    gsel = jnp.zeros(gs.shape, jnp.bool_)
    for _ in range(TOPK_GROUPS):
        m = jnp.max(gs, axis=0, keepdims=True)
        hit = sub == _first_index(gs == m, sub, N_EXPERT_GROUPS)
        gsel = jnp.logical_or(gsel, hit)
        gs = jnp.where(hit, neg, gs)

    masked = jnp.concatenate(
        [jnp.where(gsel[g:g + 1], biased[g * PER_GROUP:(g + 1) * PER_GROUP], neg)
         for g in range(N_EXPERT_GROUPS)], axis=0)
    eid = lax.broadcasted_iota(I32, masked.shape, 0)
    sel = jnp.zeros(masked.shape, jnp.bool_)
    idxs, gws = [], []
    for _ in range(TOP_K):
        m = jnp.max(masked, axis=0, keepdims=True)
        i = _first_index(masked == m, eid, N_EXPERTS)
        hit = eid == i
        idxs.append(i)
        gws.append(jnp.sum(jnp.where(hit, scores, 0.0), axis=0, keepdims=True))
        sel = jnp.logical_or(sel, hit)
        masked = jnp.where(hit, neg, masked)
    eidx = jnp.concatenate(idxs, axis=0)
    gw = jnp.concatenate(gws, axis=0)
    gw = gw / jnp.sum(gw, axis=0, keepdims=True) * ROUTED_SCALE
    eidx_ref[...] = eidx
    gw_ref[...] = gw

    before = jnp.dot(sel.astype(BF16), tri_ref[...], preferred_element_type=F32) + carry_ref[...]
    rank_ref[...] = jnp.concatenate(
        [jnp.sum(jnp.where(eid == idxs[k], before, 0.0), axis=0, keepdims=True) for k in range(TOP_K)],
        axis=0).astype(I32)
    carry = carry_ref[...] + jnp.sum(sel.astype(F32), axis=1, keepdims=True)
    carry_ref[...] = carry
    cnt_ref[...] = carry.astype(I32)


def _route(h2, g_ffn, wr_t, bias_col, wsgu, wsd, part, n_parts):
    n_all, d = h2.shape
    n = n_all // n_parts
    t = min(ROUTE_TILE, n)
    blk0 = part * (n // t)
    tri = (lax.broadcasted_iota(I32, (t, t), 0) < lax.broadcasted_iota(I32, (t, t), 1)).astype(BF16)
    row = lambda w: pl.BlockSpec((t, w), lambda i: (i, 0))
    col = pl.BlockSpec((TOP_K, t), lambda i: (0, i))
    return pl.pallas_call(
        _route_kernel,
        grid=(n // t,),
        in_specs=[pl.BlockSpec((t, d), lambda i: (i + blk0, 0)), _full((1, d)), _full((2 * N_EXPERTS, d)),
                  _full((N_EXPERTS, 1)),
                  _full((d, 2 * D_EXPERT)), _full((D_EXPERT, d)), _full((t, t))],
        out_specs=[row(d // 2), row(d), col, col, col, _full((N_EXPERTS, 1))],
        out_shape=[jax.ShapeDtypeStruct((n, d // 2), I32), jax.ShapeDtypeStruct((n, d), F32),
                   jax.ShapeDtypeStruct((TOP_K, n), I32), jax.ShapeDtypeStruct((TOP_K, n), F32),
                   jax.ShapeDtypeStruct((TOP_K, n), I32), jax.ShapeDtypeStruct((N_EXPERTS, 1), I32)],
        scratch_shapes=[pltpu.VMEM((N_EXPERTS, 1), F32)],
        compiler_params=_params(("arbitrary",)),
        name="route",
    )(h2, g_ffn, wr_t, bias_col, wsgu, wsd, tri)


def _expert_kernel(be_ref, first_ref, slot_ref, next_ref, nu_ref, x_hbm, wg_hbm, wu_hbm, wd_hbm, y_ref,
                   wg_s, wu_s, wd_s, x_buf, x_sem, wg_buf, wu_buf, wd_buf, w_sem):
    i = pl.program_id(0)
    n_used = nu_ref[0]

    def fetch(step):
        slot = step % X_SLOTS
        rows = pl.ds(pl.multiple_of(step * MOE_TILE, MOE_TILE), MOE_TILE)
        return pltpu.make_async_copy(x_hbm.at[rows], x_buf.at[slot], x_sem.at[slot])

    def weight_fetch(expert, slot):
        pairs = ((wg_hbm, wg_buf), (wu_hbm, wu_buf), (wd_hbm, wd_buf))
        return [pltpu.make_async_copy(src.at[expert], buf.at[slot], w_sem.at[slot, j])
                for j, (src, buf) in enumerate(pairs)]

    @pl.when(i == 0)
    def _():
        for j in range(X_SLOTS - 1):
            @pl.when(j < n_used)
            def _(j=j):
                fetch(j).start()
        for cp in weight_fetch(be_ref[0], 0):
            cp.start()

    @pl.when(i + (X_SLOTS - 1) < n_used)
    def _():
        fetch(i + (X_SLOTS - 1)).start()

    @pl.when(first_ref[i] == 1)
    def _():
        slot = slot_ref[i]

        @pl.when(next_ref[i] >= 0)
        def _():
            for cp in weight_fetch(next_ref[i], 1 - slot):
                cp.start()

        for cp in weight_fetch(be_ref[i], slot):
            cp.wait()
        wg_s[...] = wg_buf[slot].astype(BF16)
        wu_s[...] = wu_buf[slot].astype(BF16)
        wd_s[...] = wd_buf[slot].astype(BF16)

    @pl.when(i < n_used)
    def _():
        fetch(i).wait()
        x = _unpack_rows(x_buf[i % X_SLOTS]).astype(BF16)
        gate = jnp.dot(x, wg_s[...], preferred_element_type=F32)
        up = jnp.dot(x, wu_s[...], preferred_element_type=F32)
        hid = (jax.nn.silu(gate) * up).astype(BF16)
        y_ref[...] = _pack_rows(jnp.dot(hid, wd_s[...], preferred_element_type=F32))


def _experts(schedule, xs, w_gate, w_up, w_down):
    block_e, run_first, run_slot, run_next, n_used = schedule
    n_slots, dw = xs.shape
    _, d, de = w_gate.shape
    nb = n_slots // MOE_TILE
    hbm = pl.BlockSpec(memory_space=pl.ANY)
    return pl.pallas_call(
        _expert_kernel,
        grid_spec=pltpu.PrefetchScalarGridSpec(
            num_scalar_prefetch=5,
            grid=(nb,),
            in_specs=[hbm, hbm, hbm, hbm],
            out_specs=pl.BlockSpec((MOE_TILE, dw), lambda i, be, fi, sl, nx, nu: (jnp.minimum(i, nu[0] - 1), 0)),
            scratch_shapes=[pltpu.VMEM((d, de), BF16), pltpu.VMEM((d, de), BF16), pltpu.VMEM((de, d), BF16),
                            pltpu.VMEM((X_SLOTS, MOE_TILE, dw), I32), pltpu.SemaphoreType.DMA((X_SLOTS,)),
                            pltpu.VMEM((2, d, de), F32), pltpu.VMEM((2, d, de), F32), pltpu.VMEM((2, de, d), F32),
                            pltpu.SemaphoreType.DMA((2, 3))]),
        out_shape=jax.ShapeDtypeStruct((n_slots, dw), I32),
        compiler_params=_params(("arbitrary",)),
        name="experts",
    )(block_e, run_first, run_slot, run_next, n_used, xs, w_gate, w_up, w_down)


def _sc_worker_id():
    return lax.axis_index("s") * SC_CORES + lax.axis_index("c")


def _sc_dispatch(t_rows, dest3, n_slots):
    _, dw = t_rows.shape
    n_chunks, _, w = dest3.shape
    per_worker = n_chunks // SC_WORKERS
    assert per_worker % 2 == 0
    mesh = plsc.VectorSubcoreMesh(core_axis_name="c", subcore_axis_name="s")
    dt = t_rows.dtype

    @functools.partial(
        pl.kernel, mesh=mesh,
        out_type=jax.ShapeDtypeStruct((n_slots, dw), dt),
        scratch_types=[pltpu.VMEM((TOP_K, w), I32), pltpu.VMEM((TOP_K, w), I32),
                       pltpu.VMEM((w, dw), dt), pltpu.VMEM((w, dw), dt),
                       pltpu.SemaphoreType.DMA, pltpu.SemaphoreType.DMA, pltpu.SemaphoreType.DMA],
    )
    def scatter_rows(t_hbm, dest_hbm, xs_hbm, idx_a, idx_b, rows_a, rows_b, sem_load, sem_a, sem_b):
        wid = _sc_worker_id()

        def scatter(idx_v, rows_v, sem):
            return [pltpu.async_copy(rows_v, xs_hbm.at[idx_v.at[k]], sem) for k in range(TOP_K)]

        @pl.loop(0, per_worker, step=2)
        def _(j):
            ca = wid * per_worker + j
            cb = ca + 1
            pltpu.sync_copy(dest_hbm.at[ca], idx_a)
            pltpu.sync_copy(t_hbm.at[pl.ds(ca * w, w)], rows_a)
            load_idx = pltpu.async_copy(dest_hbm.at[cb], idx_b, sem_load)
            load_rows = pltpu.async_copy(t_hbm.at[pl.ds(cb * w, w)], rows_b, sem_load)
            out_a = scatter(idx_a, rows_a, sem_a)
            load_idx.wait()
            load_rows.wait()
            out_b = scatter(idx_b, rows_b, sem_b)
            for cp in out_a + out_b:
                cp.wait()

    return scatter_rows(t_rows, dest3)


def _sc_combine(y_rows, dest3):
    _, dw = y_rows.shape
    n_chunks, _, w = dest3.shape
    per_worker = n_chunks // SC_WORKERS
    mesh = plsc.VectorSubcoreMesh(core_axis_name="c", subcore_axis_name="s")
    dt = y_rows.dtype

    @functools.partial(
        pl.kernel, mesh=mesh,
        out_type=jax.ShapeDtypeStruct((TOP_K, n_chunks * w, dw), dt),
        scratch_types=[pltpu.VMEM((TOP_K, w), I32), pltpu.VMEM((w, dw), dt), pltpu.VMEM((w, dw), dt),
                       pltpu.SemaphoreType.DMA, pltpu.SemaphoreType.DMA,
                       pltpu.SemaphoreType.DMA, pltpu.SemaphoreType.DMA],
    )
    def gather_rows(y_hbm, dest_hbm, out_hbm, idx_v, buf0, buf1, gsem0, gsem1, wsem0, wsem1):
        wid = _sc_worker_id()
        bufs, gsems, wsems = (buf0, buf1), (gsem0, gsem1), (wsem0, wsem1)

        @pl.loop(0, per_worker)
        def _(j):
            c = wid * per_worker + j
            pltpu.sync_copy(dest_hbm.at[c], idx_v)
            gathers = [None] * TOP_K
            writes = [None] * TOP_K
            gathers[0] = pltpu.async_copy(y_hbm.at[idx_v.at[0]], bufs[0], gsems[0])
            for k in range(TOP_K):
                b = k % 2
                if k + 1 < TOP_K:
                    if k >= 1:
                        writes[k - 1].wait()
                    gathers[k + 1] = pltpu.async_copy(y_hbm.at[idx_v.at[k + 1]], bufs[1 - b], gsems[1 - b])
                gathers[k].wait()
                writes[k] = pltpu.async_copy(bufs[b], out_hbm.at[k, pl.ds(c * w, w)], wsems[b])
            writes[TOP_K - 2].wait()
            writes[TOP_K - 1].wait()

    return gather_rows(y_rows, dest3)


def _final_kernel(base_ref, y_ref, gw_ref, g_ref, *rest, normalize):
    o_ref = rest[-1]
    h = base_ref[...]
    gw = gw_ref[...]
    for k in range(TOP_K):
        h = h + gw[:, k:k + 1] * _unpack_rows(y_ref[k])
    o_ref[...] = _rms(h, g_ref[...]) if normalize else h


def _final(base, yg, gw_rows, g_final, normalize, out_prev, part, n_parts):
    n, d = base.shape
    t = min(FIN_TILE, n)
    blk0 = part * (n // t)
    in_specs = [pl.BlockSpec((t, d), lambda i: (i, 0)), pl.BlockSpec((TOP_K, t, d // 2), lambda i: (0, i, 0)),
                pl.BlockSpec((t, TOP_K), lambda i: (i, 0)), _full((1, d))]
    args = [base, yg, gw_rows, g_final]
    aliases = {}
    if out_prev is not None:
        in_specs.append(pl.BlockSpec(memory_space=pl.ANY))
        args.append(out_prev)
        aliases = {len(args) - 1: 0}
    return pl.pallas_call(
        functools.partial(_final_kernel, normalize=normalize),
        grid=(n // t,),
        in_specs=in_specs,
        out_specs=pl.BlockSpec((t, d), lambda i: (i + blk0, 0)),
        out_shape=jax.ShapeDtypeStruct((n * n_parts, d), F32),
        input_output_aliases=aliases,
        compiler_params=_params(("parallel",)),
        name="final",
    )(*args)


def _ssm_matrices(a_re, a_im, log_dt, b_re, b_im, c_re, c_im):
    lr, li = a_re.astype(F32), a_im.astype(F32)
    dt = jnp.exp(log_dt.astype(F32))[:, None]
    mag = jnp.exp(lr * dt)
    ab_re, ab_im = mag * jnp.cos(li * dt), mag * jnp.sin(li * dt)
    den = lr * lr + li * li
    zr, zi = ab_re - 1.0, ab_im
    k_re = (zr * lr + zi * li) / den
    k_im = (zi * lr - zr * li) / den
    br, bi = b_re.astype(F32), b_im.astype(F32)
    bb_re = k_re[..., None] * br - k_im[..., None] * bi
    bb_im = k_re[..., None] * bi + k_im[..., None] * br
    eye = jnp.eye(SSM_GROUPS, dtype=F32)

    def in_mat(bb):
        return jnp.einsum("gpc,gh->gchp", bb, eye).reshape(SSM_WIDTH, N_STATE)

    def out_mat(c):
        return jnp.einsum("gcp,gh->gphc", c.astype(F32), eye).reshape(N_STATE, SSM_WIDTH)

    bmat = jnp.concatenate([in_mat(bb_re), in_mat(bb_im)], axis=1)
    cmat = jnp.concatenate([out_mat(c_re), -out_mat(c_im)], axis=0)
    tile = lambda a: jnp.broadcast_to(a.reshape(1, N_STATE), (SUBLANES, N_STATE))
    return bmat.astype(BF16), tile(ab_re), tile(ab_im), cmat.astype(BF16)


def _plan_kernel(pstart_ref, eidx_ref, rank_ref, dest_ref, *, window):
    eidx = eidx_ref[...]
    dest = rank_ref[...]
    for e in range(N_EXPERTS):
        dest = dest + jnp.where(eidx == e, pstart_ref[e], 0)
    for c in range(dest.shape[1] // window):
        dest_ref[c] = dest[:, c * window:(c + 1) * window]


def _plan(pstart, eidx, rank, window):
    k, n = eidx.shape
    t = min(PLAN_TILE, n)
    cols = pl.BlockSpec((k, t), lambda i, ps: (0, i))
    return pl.pallas_call(
        functools.partial(_plan_kernel, window=window),
        grid_spec=pltpu.PrefetchScalarGridSpec(
            num_scalar_prefetch=1,
            grid=(n // t,),
            in_specs=[cols, cols],
            out_specs=pl.BlockSpec((t // window, k, window), lambda i, ps: (i, 0, 0))),
        out_shape=jax.ShapeDtypeStruct((n // window, k, window), I32),
        compiler_params=_params(("parallel",)),
        name="plan",
    )(pstart, eidx, rank)


def _block_schedule(counts, n_tok):
    padded = ((counts + MOE_TILE - 1) // MOE_TILE) * MOE_TILE
    pend = jnp.cumsum(padded)
    pstart = (pend - padded).astype(I32)
    n_slots = n_tok * TOP_K + N_EXPERTS * MOE_TILE
    nb = n_slots // MOE_TILE
    n_used = (pend[-1] // MOE_TILE).astype(I32)
    blk = jnp.arange(nb, dtype=I32)
    block_e = jnp.sum((pend[None, :] <= (jnp.minimum(blk, n_used - 1) * MOE_TILE)[:, None]).astype(I32), axis=1)
    block_e = jnp.minimum(block_e, N_EXPERTS - 1)
    eid = jnp.arange(N_EXPERTS, dtype=I32)
    active = padded > 0
    run_of = jnp.cumsum(active.astype(I32)) - 1
    later = jnp.where(jnp.logical_and(active[None, :], eid[None, :] > eid[:, None]), eid[None, :], N_EXPERTS)
    next_e = jnp.min(later, axis=1)
    next_e = jnp.where(next_e < N_EXPERTS, next_e, -1)
    onehot = block_e[:, None] == eid[None, :]
    lookup = lambda table: jnp.sum(jnp.where(onehot, table[None, :], 0), axis=1).astype(I32)
    run_first = jnp.logical_and(blk * MOE_TILE == lookup(pstart), blk < n_used).astype(I32)
    schedule = (block_e, run_first, lookup(run_of % 2), lookup(next_e), n_used.reshape(1))
    return pstart, schedule, n_slots


def kernel(x, mem, positions, g_mix, w_in, a_re, a_im, log_dt, b_re, b_im, c_re, c_im, d_skip, w_glu, g_ssm_out, lam_q1, lam_k1, lam_q2, lam_k2, g_sub, w_out, g_x, g_mem, wq_x, wk_x, wv_x, wo_x, g_ffn, w_router, router_bias, w_gate, w_up, w_down, ws_gate, ws_up, ws_down, g_final):
    b, l, d = x.shape
    n = b * l
    depth = w_in.shape[0]
    row = lambda a: a.reshape(1, -1).astype(F32)
    inv = ROPE_THETA ** (-jnp.arange(0, DA_QKDIM, 2, dtype=F32) / DA_QKDIM)
    inv_row = jnp.tile(inv, LANES // inv.shape[0]).reshape(1, LANES)
    pos3 = positions.reshape(b, l, 1)

    h = x
    for i in range(depth):
        lambda_init = 0.8 - 0.6 * math.exp(-0.3 * i)
        u, q, k, v = _inproj(h, pos3, row(g_mix[i]), w_in[i].astype(BF16), inv_row)

        bmat, are8, aim8, cmat = _ssm_matrices(a_re[i], a_im[i], log_dt[i], b_re[i], b_im[i], c_re[i], c_im[i])
        y_ssm = _ssm(u, bmat, are8, aim8, cmat, row(d_skip[i]), w_glu[i].astype(BF16), row(g_ssm_out[i]))

        lam = (jnp.exp(jnp.sum(lam_q1[i].astype(F32) * lam_k1[i].astype(F32)))
               - jnp.exp(jnp.sum(lam_q2[i].astype(F32) * lam_k2[i].astype(F32))) + lambda_init).reshape(1)
        y_att = _diff_attention(q, k, v, lam.astype(F32), row(g_sub[i]), lambda_init)

        kmem, vmem = _memkv(mem, row(g_mem[i]), wk_x[i].astype(BF16), wv_x[i].astype(BF16))
        wo = w_out[i].astype(BF16)
        h2 = _mix(h, y_ssm, y_att, wo[:SSM_WIDTH], wo[SSM_WIDTH:], row(g_x[i]), wq_x[i].astype(BF16),
                  kmem, vmem, wo_x[i].astype(BF16))

        wsgu = jnp.concatenate([ws_gate[i], ws_up[i]], axis=1).astype(BF16)
        wr = w_router[i].T.astype(F32)
        wr_hi = wr.astype(BF16)
        wr_split = jnp.concatenate([wr_hi, (wr - wr_hi.astype(F32)).astype(BF16)], axis=0)
        n_part = n // MOE_PARTS
        out = None
        for part in range(MOE_PARTS):
            t_pk, base, eidx, gw, rank, counts = _route(
                h2.reshape(n, d), row(g_ffn[i]), wr_split,
                router_bias[i].reshape(N_EXPERTS, 1).astype(F32), wsgu, ws_down[i].astype(BF16), part, MOE_PARTS)
            pstart, schedule, n_slots = _block_schedule(counts[:, 0], n_part)
            dest3 = _plan(pstart, eidx, rank, SC_WINDOW)
            xs = _sc_dispatch(t_pk, dest3, n_slots)
            ys = _experts(schedule, xs, w_gate[i], w_up[i], w_down[i])
            yg = _sc_combine(ys, dest3)
            out = _final(base, yg, gw.T, row(g_final), i == depth - 1, out, part, MOE_PARTS)
        h = out.reshape(b, l, d)
    return h
```

```python
import functools
import math

import jax
import jax.numpy as jnp
from jax import lax
from jax.experimental import pallas as pl
from jax.experimental.pallas import tpu as pltpu
from jax.experimental.pallas import tpu_sc as plsc

F32 = jnp.float32
BF16 = jnp.bfloat16
I32 = jnp.int32

D_MODEL = 1024
SSM_WIDTH = 512
ATTN_WIDTH = 512
SSM_GROUP = 16
SSM_GROUPS = 32
SSM_STATE = 64
N_STATE = SSM_GROUPS * SSM_STATE
DA_HEADS = 4
DA_VDIM = 128
DA_QKDIM = 64
ROPE_THETA = 10000.0
ROPE_FREQS = DA_QKDIM // 2
X_HEADS = 4
X_HEAD_DIM = 256
N_EXPERTS = 64
TOP_K = 8
N_EXPERT_GROUPS = 8
PER_GROUP = N_EXPERTS // N_EXPERT_GROUPS
TOPK_GROUPS = 4
D_EXPERT = 256
ROUTED_SCALE = 2.5
EPS = 1e-6

VMEM_LIMIT_V7X = 56 * 1024 * 1024
SUBLANES = 8
LANES = 128
HALF_WORD_BITS = 16
HIGH_HALF_MASK = -(1 << HALF_WORD_BITS)

IN_TILE = 1024
SSM_STEPS = 128
SSM_COLS = 1024
SSM_CH = 128
SSM_SLAB = 512
ATT_TILE = 2048
ATT_ROWS = 256
MIX_TILE = 1024
ROUTE_TILE = 1024
MOE_TILE = 512
X_SLOTS = 3
PLAN_TILE = 2048
MOE_PARTS = 2
FIN_TILE = 512

SC_CORES = 2
SC_WORKERS = 32
SC_WINDOW = 64


def _params(sem):
    return pltpu.CompilerParams(dimension_semantics=sem, vmem_limit_bytes=VMEM_LIMIT_V7X)


def _rms(x, g):
    return x * lax.rsqrt(jnp.mean(x * x, axis=-1, keepdims=True) + EPS) * g


def _full(shape):
    return pl.BlockSpec(shape, lambda *_: (0,) * len(shape))


def _pack_rows(a):
    w = a.shape[1] // 2
    bits = lambda v: lax.bitcast_convert_type(v.astype(BF16).astype(F32), I32)
    return (bits(a[:, w:]) & jnp.int32(HIGH_HALF_MASK)) | lax.shift_right_logical(bits(a[:, :w]), HALF_WORD_BITS)


def _unpack_rows(p):
    lo = lax.bitcast_convert_type(lax.shift_left(p, HALF_WORD_BITS), F32)
    hi = lax.bitcast_convert_type(p & jnp.int32(HIGH_HALF_MASK), F32)
    return jnp.concatenate([lo, hi], axis=1)


def _inproj_kernel(x_ref, pos_ref, g_ref, w_ref, inv_ref, u_ref, q_ref, k_ref, v_ref):
    x = x_ref[0]
    a = _rms(x, g_ref[...]).astype(BF16)
    z = jnp.dot(a, w_ref[...], preferred_element_type=F32)
    u_ref[0] = z[:, :SSM_WIDTH]
    n_grp = LANES // ROPE_FREQS
    tq = x.shape[0] // n_grp
    pos = pos_ref[0].astype(F32)
    lane_c = lax.broadcasted_iota(I32, (tq, LANES), 1)
    pos_c = jnp.zeros((tq, LANES), F32)
    for j in range(n_grp):
        pos_c = jnp.where(lane_c // ROPE_FREQS == j, pos[j * tq:(j + 1) * tq], pos_c)
    ang = pos_c * inv_ref[...]

    def spread(c):
        rows = []
        for j in range(n_grp):
            y = c if j == 0 else pltpu.roll(c, LANES - ROPE_FREQS * j, 1)
            w = ROPE_FREQS
            while w < LANES:
                y = jnp.where(lane_c < w, y, pltpu.roll(y, w, 1))
                w *= 2
            rows.append(y)
        return jnp.concatenate(rows, axis=0)

    cos = jnp.concatenate([spread(jnp.cos(ang))] * (ATTN_WIDTH // LANES), axis=1)
    sin = jnp.concatenate([spread(jnp.sin(ang))] * (ATTN_WIDTH // LANES), axis=1)
    lane = lax.broadcasted_iota(I32, cos.shape, 1)
    first = (lane & (DA_QKDIM - 1)) < DA_QKDIM // 2
    half = DA_QKDIM // 2

    def rope(t):
        rot = jnp.where(first, -pltpu.roll(t, ATTN_WIDTH - half, 1), pltpu.roll(t, half, 1))
        return t * cos + rot * sin

    q = z[:, SSM_WIDTH:SSM_WIDTH + ATTN_WIDTH]
    k = z[:, SSM_WIDTH + ATTN_WIDTH:SSM_WIDTH + 2 * ATTN_WIDTH]
    q_ref[0] = (rope(q) * (DA_QKDIM ** -0.5)).astype(BF16)
    k_ref[0] = rope(k).astype(BF16)
    v_ref[0] = z[:, SSM_WIDTH + 2 * ATTN_WIDTH:].astype(BF16)


def _inproj(x, pos3, g_mix, w_in, inv_row):
    b, l, d = x.shape
    t = min(IN_TILE, l)
    n_out = w_in.shape[1]
    row = lambda w: pl.BlockSpec((1, t, w), lambda i, j: (i, j, 0))
    return pl.pallas_call(
        _inproj_kernel,
        grid=(b, l // t),
        in_specs=[row(d), row(1), _full((1, d)), _full((d, n_out)), _full((1, LANES))],
        out_specs=[row(SSM_WIDTH), row(ATTN_WIDTH), row(ATTN_WIDTH), row(ATTN_WIDTH)],
        out_shape=[jax.ShapeDtypeStruct((b, l, SSM_WIDTH), F32)]
        + [jax.ShapeDtypeStruct((b, l, ATTN_WIDTH), BF16)] * 3,
        compiler_params=_params(("parallel", "parallel")),
        name="inproj",
    )(x, pos3, g_mix, w_in, inv_row)


def _ssm_kernel(u_ref, bm_ref, are_ref, aim_ref, cm_ref, dskip_ref, wglu_ref, g_ref, o_ref, bu_ref, st_ref,
                *, steps):
    @pl.when(pl.program_id(0) == 0)
    def _():
        st_ref[...] = jnp.zeros_like(st_ref)

    u = pltpu.einshape("bts->tbs", u_ref[...]).reshape(steps * SUBLANES, SSM_WIDTH)
    ub = u.astype(BF16)

    n_slabs = SSM_WIDTH // SSM_CH
    for s in range(n_slabs):
        ch = slice(s * SSM_CH, (s + 1) * SSM_CH)
        for part in (0, N_STATE):
            cols = slice(part + s * SSM_SLAB, part + (s + 1) * SSM_SLAB)
            bu_ref[:, cols] = jnp.dot(ub[:, ch], bm_ref[ch, cols], preferred_element_type=F32)

    for c0 in range(0, N_STATE, SSM_COLS):
        re = slice(c0, c0 + SSM_COLS)
        im = slice(N_STATE + c0, N_STATE + c0 + SSM_COLS)
        ar = are_ref[:, re]
        ai = aim_ref[:, re]

        sr, si = st_ref[:, re], st_ref[:, im]
        for t in range(steps):
            rows = slice(t * SUBLANES, (t + 1) * SUBLANES)
            sr, si = ar * sr - ai * si + bu_ref[rows, re], ar * si + ai * sr + bu_ref[rows, im]
            bu_ref[rows, re] = sr
            bu_ref[rows, im] = si
        st_ref[:, re] = sr
        st_ref[:, im] = si

    ys = []
    for s in range(n_slabs):
        ch = slice(s * SSM_CH, (s + 1) * SSM_CH)
        acc = None
        for part in (0, N_STATE):
            cols = slice(part + s * SSM_SLAB, part + (s + 1) * SSM_SLAB)
            d = jnp.dot(bu_ref[:, cols].astype(BF16), cm_ref[cols, ch], preferred_element_type=F32)
            acc = d if acc is None else acc + d
        ys.append(acc)
    y = jnp.concatenate(ys, axis=1) + dskip_ref[...] * u
    y = jax.nn.gelu(y)
    y = y * jax.nn.sigmoid(jnp.dot(y.astype(BF16), wglu_ref[...], preferred_element_type=F32))
    out = _rms(y, g_ref[...]).reshape(steps, SUBLANES, SSM_WIDTH)
    o_ref[...] = pltpu.einshape("tbs->bts", out).astype(BF16)


def _ssm(u, bmat, a_re8, a_im8, cmat, d_skip, w_glu, g_out):
    batch, l, _ = u.shape
    assert batch == SUBLANES, "one time step of all sequences must fill the sublanes"
    steps = min(SSM_STEPS, l)
    rows = steps * batch
    seq = pl.BlockSpec((batch, steps, SSM_WIDTH), lambda i: (0, i, 0))
    return pl.pallas_call(
        functools.partial(_ssm_kernel, steps=steps),
        grid=(l // steps,),
        in_specs=[seq,
                  _full((SSM_WIDTH, 2 * N_STATE)), _full((SUBLANES, N_STATE)), _full((SUBLANES, N_STATE)),
                  _full((2 * N_STATE, SSM_WIDTH)), _full((1, SSM_WIDTH)), _full((SSM_WIDTH, SSM_WIDTH)),
                  _full((1, SSM_WIDTH))],
        out_specs=seq,
        out_shape=jax.ShapeDtypeStruct((batch, l, SSM_WIDTH), BF16),
        scratch_shapes=[pltpu.VMEM((rows, 2 * N_STATE), F32), pltpu.VMEM((SUBLANES, 2 * N_STATE), F32)],
        compiler_params=_params(("arbitrary",)),
        name="ssm",
    )(u, bmat, a_re8, a_im8, cmat, d_skip, w_glu, g_out)


def _attn_kernel(qi_ref, ki_ref, lam_ref, q_ref, k_ref, v_ref, g_ref, o_ref, qs_ref, m_ref, acc_ref, *,
                 tile, out_scale):
    qi = qi_ref[pl.program_id(2)]
    ki = ki_ref[pl.program_id(2)]

    @pl.when(ki == 0)
    def _():
        q = q_ref[0]
        lane = lax.broadcasted_iota(I32, q.shape, 1)
        zero = jnp.zeros_like(q)
        qs_ref[:tile] = jnp.where(lane < DA_QKDIM, q, zero)
        qs_ref[tile:] = jnp.where(lane >= DA_QKDIM, q, zero)
        m_ref[...] = jnp.full_like(m_ref, -jnp.inf)
        acc_ref[...] = jnp.zeros_like(acc_ref)

    def update(masked):
        v_ext = jnp.concatenate([v_ref[0], jnp.ones((tile, DA_VDIM), BF16)], axis=1)
        rb = min(ATT_ROWS, tile)
        n_rb = 2 * tile // rb

        def n_keys(r):
            return (r * rb) % tile + rb if masked else tile

        def scores(r):
            return lax.dot_general(qs_ref[r * rb:(r + 1) * rb], k_ref[0, :n_keys(r)], (((1,), (1,)), ((), ())),
                                   preferred_element_type=F32)

        s_next = scores(0)
        for r in range(n_rb):
            s = s_next
            if r + 1 < n_rb:
                s_next = scores(r + 1)
            rows = slice(r * rb, (r + 1) * rb)
            nk = n_keys(r)
            if masked:
                tri = lax.broadcasted_iota(I32, (rb, rb), 1) <= lax.broadcasted_iota(I32, (rb, rb), 0)
                edge = jnp.where(tri, s[:, nk - rb:], jnp.finfo(F32).min)
                s = edge if nk == rb else jnp.concatenate([s[:, :nk - rb], edge], axis=1)
            m_old = m_ref[rows]
            m_new = jnp.maximum(m_old, jnp.max(s, axis=1, keepdims=True))
            p = jnp.exp(s - jnp.concatenate([m_new] * (nk // DA_VDIM), axis=1))
            alpha = jnp.exp(m_old - m_new)
            acc_ref[rows] = (jnp.concatenate([alpha, alpha], axis=1) * acc_ref[rows]
                             + jnp.dot(p.astype(BF16), v_ext[:nk], preferred_element_type=F32))
            m_ref[rows] = m_new

    @pl.when(ki < qi)
    def _():
        update(False)

    @pl.when(ki == qi)
    def _():
        update(True)
        o = acc_ref[:, :DA_VDIM] / acc_ref[:, DA_VDIM:]
        o = o[:tile] - lam_ref[0] * o[tile:]
        o_ref[0] = (_rms(o, g_ref[...]) * out_scale).astype(BF16)


def _diff_attention(q, k, v, lam, g_sub, lambda_init):
    b, l, _ = q.shape
    t = min(ATT_TILE, l)
    n = l // t
    pairs = [(qi, ki) for qi in range(n) for ki in range(qi + 1)]
    qi_tab = jnp.asarray([p[0] for p in pairs], I32)
    ki_tab = jnp.asarray([p[1] for p in pairs], I32)
    qspec = pl.BlockSpec((1, t, DA_VDIM), lambda bi, h, s, qt, kt: (bi, qt[s], h))
    kspec = pl.BlockSpec((1, t, DA_VDIM), lambda bi, h, s, qt, kt: (bi, kt[s], h))
    return pl.pallas_call(
        functools.partial(_attn_kernel, tile=t, out_scale=1.0 - lambda_init),
        grid_spec=pltpu.PrefetchScalarGridSpec(
            num_scalar_prefetch=2,
            grid=(b, DA_HEADS, len(pairs)),
            in_specs=[pl.BlockSpec(memory_space=pltpu.SMEM), qspec, kspec, kspec,
                      pl.BlockSpec((1, DA_VDIM), lambda bi, h, s, qt, kt: (0, 0))],
            out_specs=qspec,
            scratch_shapes=[pltpu.VMEM((2 * t, DA_VDIM), BF16), pltpu.VMEM((2 * t, DA_VDIM), F32),
                            pltpu.VMEM((2 * t, 2 * DA_VDIM), F32)]),
        out_shape=jax.ShapeDtypeStruct((b, l, ATTN_WIDTH), BF16),
        compiler_params=_params(("parallel", "parallel", "arbitrary")),
        name="diffattn",
    )(qi_tab, ki_tab, lam, q, k, v, g_sub)


def _memkv_kernel(m_ref, g_ref, wk_ref, wv_ref, k_ref, v_ref):
    a = _rms(m_ref[0], g_ref[...]).astype(BF16)
    k_ref[0] = jnp.dot(a, wk_ref[...], preferred_element_type=F32).astype(BF16)
    v_ref[0] = jnp.dot(a, wv_ref[...], preferred_element_type=F32).astype(BF16)


def _memkv(mem, g_mem, wk, wv):
    b, m, d = mem.shape
    blk = pl.BlockSpec((1, m, d), lambda i: (i, 0, 0))
    return pl.pallas_call(
        _memkv_kernel,
        grid=(b,),
        in_specs=[blk, _full((1, d)), _full((d, d)), _full((d, d))],
        out_specs=[blk, blk],
        out_shape=[jax.ShapeDtypeStruct((b, m, d), BF16)] * 2,
        compiler_params=_params(("parallel",)),
        name="memkv",
    )(mem, g_mem, wk, wv)


def _mix_kernel(x_ref, ys_ref, ya_ref, wo1_ref, wo2_ref, gx_ref, wq_ref, km_ref, vm_ref, wo_ref, h_ref):
    h = (x_ref[0]
         + jnp.dot(ys_ref[0], wo1_ref[...], preferred_element_type=F32)
         + jnp.dot(ya_ref[0], wo2_ref[...], preferred_element_type=F32))
    hq = _rms(h, gx_ref[...]).astype(BF16)
    q = jnp.dot(hq, wq_ref[...], preferred_element_type=F32).astype(BF16)
    outs = []
    for hd in range(X_HEADS):
        cols = slice(hd * X_HEAD_DIM, (hd + 1) * X_HEAD_DIM)
        s = lax.dot_general(q[:, cols], km_ref[0, :, cols], (((1,), (1,)), ((), ())),
                            preferred_element_type=F32) * (X_HEAD_DIM ** -0.5)
        s = s - jnp.max(s, axis=1, keepdims=True)
        p = jnp.exp(s)
        p = p / jnp.sum(p, axis=1, keepdims=True)
        outs.append(jnp.dot(p.astype(BF16), vm_ref[0, :, cols], preferred_element_type=F32).astype(BF16))
    o = jnp.concatenate(outs, axis=1)
    h_ref[0] = h + jnp.dot(o, wo_ref[...], preferred_element_type=F32)


def _mix(x, y_ssm, y_att, wo1, wo2, g_x, wq, kmem, vmem, wo):
    b, l, d = x.shape
    t = min(MIX_TILE, l)
    m = kmem.shape[1]
    row = lambda w: pl.BlockSpec((1, t, w), lambda i, j: (i, j, 0))
    mem = pl.BlockSpec((1, m, d), lambda i, j: (i, 0, 0))
    return pl.pallas_call(
        _mix_kernel,
        grid=(b, l // t),
        in_specs=[row(d), row(SSM_WIDTH), row(ATTN_WIDTH), _full((SSM_WIDTH, d)), _full((ATTN_WIDTH, d)),
                  _full((1, d)), _full((d, d)), mem, mem, _full((d, d))],
        out_specs=row(d),
        out_shape=jax.ShapeDtypeStruct((b, l, d), F32),
        compiler_params=_params(("parallel", "parallel")),
        name="mix",
    )(x, y_ssm, y_att, wo1, wo2, g_x, wq, kmem, vmem, wo)


def _first_index(hit, idx, sentinel):
    return jnp.min(jnp.where(hit, idx, sentinel), axis=0, keepdims=True)


def _route_kernel(h_ref, g_ref, wr_ref, bias_ref, wsgu_ref, wsd_ref, tri_ref,
                  t_ref, base_ref, eidx_ref, gw_ref, rank_ref, cnt_ref, carry_ref):
    @pl.when(pl.program_id(0) == 0)
    def _():
        carry_ref[...] = jnp.zeros_like(carry_ref)

    h = h_ref[...]
    t = _rms(h, g_ref[...])
    tb = t.astype(BF16)
    t_ref[...] = _pack_rows(t)
    gu = jnp.dot(tb, wsgu_ref[...], preferred_element_type=F32)
    hid = jax.nn.silu(gu[:, :D_EXPERT]) * gu[:, D_EXPERT:]
    base_ref[...] = h + jnp.dot(hid.astype(BF16), wsd_ref[...], preferred_element_type=F32)

    nt = (((1,), (1,)), ((), ()))
    t_lo = (t - tb.astype(F32)).astype(BF16)
    both = lax.dot_general(wr_ref[...], tb, nt, preferred_element_type=F32)
    logits = (both[:N_EXPERTS] + both[N_EXPERTS:]
              + lax.dot_general(wr_ref[:N_EXPERTS], t_lo, nt, preferred_element_type=F32))
    scores = jax.nn.sigmoid(logits)
    biased = scores + bias_ref[...]
    n_tok = scores.shape[1]
    neg = -jnp.inf
    sub = lax.broadcasted_iota(I32, (PER_GROUP, n_tok), 0)

    gs = []
    for g in range(N_EXPERT_GROUPS):
        blk = biased[g * PER_GROUP:(g + 1) * PER_GROUP]
        m1 = jnp.max(blk, axis=0, keepdims=True)
        i1 = _first_index(blk == m1, sub, PER_GROUP)
        m2 = jnp.max(jnp.where(sub == i1, neg, blk), axis=0, keepdims=True)
        gs.append(m1 + m2)
    gs = jnp.concatenate(gs, axis=0)

    gsel = jnp.zeros(gs.shape, jnp.bool_)
    for _ in range(TOPK_GROUPS):
        m = jnp.max(gs, axis=0, keepdims=True)
        hit = sub == _first_index(gs == m, sub, N_EXPERT_GROUPS)
        gsel = jnp.logical_or(gsel, hit)
        gs = jnp.where(hit, neg, gs)

    masked = jnp.concatenate(
        [jnp.where(gsel[g:g + 1], biased[g * PER_GROUP:(g + 1) * PER_GROUP], neg)
         for g in range(N_EXPERT_GROUPS)], axis=0)
    eid = lax.broadcasted_iota(I32, masked.shape, 0)
    sel = jnp.zeros(masked.shape, jnp.bool_)
    idxs, gws = [], []
    for _ in range(TOP_K):
        m = jnp.max(masked, axis=0, keepdims=True)
        i = _first_index(masked == m, eid, N_EXPERTS)
        hit = eid == i
        idxs.append(i)
        gws.append(jnp.sum(jnp.where(hit, scores, 0.0), axis=0, keepdims=True))
        sel = jnp.logical_or(sel, hit)
        masked = jnp.where(hit, neg, masked)
    eidx = jnp.concatenate(idxs, axis=0)
    gw = jnp.concatenate(gws, axis=0)
    gw = gw / jnp.sum(gw, axis=0, keepdims=True) * ROUTED_SCALE
    eidx_ref[...] = eidx
    gw_ref[...] = gw

    before = jnp.dot(sel.astype(BF16), tri_ref[...], preferred_element_type=F32) + carry_ref[...]
    rank_ref[...] = jnp.concatenate(
        [jnp.sum(jnp.where(eid == idxs[k], before, 0.0), axis=0, keepdims=True) for k in range(TOP_K)],
        axis=0).astype(I32)
    carry = carry_ref[...] + jnp.sum(sel.astype(F32), axis=1, keepdims=True)
    carry_ref[...] = carry
    cnt_ref[...] = carry.astype(I32)


def _route(h2, g_ffn, wr_t, bias_col, wsgu, wsd, part, n_parts):
    n_all, d = h2.shape
    n = n_all // n_parts
    t = min(ROUTE_TILE, n)
    blk0 = part * (n // t)
    tri = (lax.broadcasted_iota(I32, (t, t), 0) < lax.broadcasted_iota(I32, (t, t), 1)).astype(BF16)
    row = lambda w: pl.BlockSpec((t, w), lambda i: (i, 0))
    col = pl.BlockSpec((TOP_K, t), lambda i: (0, i))
    return pl.pallas_call(
        _route_kernel,
        grid=(n // t,),
        in_specs=[pl.BlockSpec((t, d), lambda i: (i + blk0, 0)), _full((1, d)), _full((2 * N_EXPERTS, d)),
                  _full((N_EXPERTS, 1)),
                  _full((d, 2 * D_EXPERT)), _full((D_EXPERT, d)), _full((t, t))],
        out_specs=[row(d // 2), row(d), col, col, col, _full((N_EXPERTS, 1))],
        out_shape=[jax.ShapeDtypeStruct((n, d // 2), I32), jax.ShapeDtypeStruct((n, d), F32),
                   jax.ShapeDtypeStruct((TOP_K, n), I32), jax.ShapeDtypeStruct((TOP_K, n), F32),
                   jax.ShapeDtypeStruct((TOP_K, n), I32), jax.ShapeDtypeStruct((N_EXPERTS, 1), I32)],
        scratch_shapes=[pltpu.VMEM((N_EXPERTS, 1), F32)],
        compiler_params=_params(("arbitrary",)),
        name="route",
    )(h2, g_ffn, wr_t, bias_col, wsgu, wsd, tri)


def _expert_kernel(be_ref, first_ref, slot_ref, next_ref, nu_ref, x_hbm, wg_hbm, wu_hbm, wd_hbm, y_ref,
                   wg_s, wu_s, wd_s, x_buf, x_sem, wg_buf, wu_buf, wd_buf, w_sem):
    i = pl.program_id(0)
    n_used = nu_ref[0]

    def fetch(step):
        slot = step % X_SLOTS
        rows = pl.ds(pl.multiple_of(step * MOE_TILE, MOE_TILE), MOE_TILE)
        return pltpu.make_async_copy(x_hbm.at[rows], x_buf.at[slot], x_sem.at[slot])

    def weight_fetch(expert, slot):
        pairs = ((wg_hbm, wg_buf), (wu_hbm, wu_buf), (wd_hbm, wd_buf))
        return [pltpu.make_async_copy(src.at[expert], buf.at[slot], w_sem.at[slot, j])
                for j, (src, buf) in enumerate(pairs)]

    @pl.when(i == 0)
    def _():
        for j in range(X_SLOTS - 1):
            @pl.when(j < n_used)
            def _(j=j):
                fetch(j).start()
        for cp in weight_fetch(be_ref[0], 0):
            cp.start()

    @pl.when(i + (X_SLOTS - 1) < n_used)
    def _():
        fetch(i + (X_SLOTS - 1)).start()

    @pl.when(first_ref[i] == 1)
    def _():
        slot = slot_ref[i]

        @pl.when(next_ref[i] >= 0)
        def _():
            for cp in weight_fetch(next_ref[i], 1 - slot):
                cp.start()

        for cp in weight_fetch(be_ref[i], slot):
            cp.wait()
        wg_s[...] = wg_buf[slot].astype(BF16)
        wu_s[...] = wu_buf[slot].astype(BF16)
        wd_s[...] = wd_buf[slot].astype(BF16)

    @pl.when(i < n_used)
    def _():
        fetch(i).wait()
        x = _unpack_rows(x_buf[i % X_SLOTS]).astype(BF16)
        gate = jnp.dot(x, wg_s[...], preferred_element_type=F32)
        up = jnp.dot(x, wu_s[...], preferred_element_type=F32)
        hid = (jax.nn.silu(gate) * up).astype(BF16)
        y_ref[...] = _pack_rows(jnp.dot(hid, wd_s[...], preferred_element_type=F32))


def _experts(schedule, xs, w_gate, w_up, w_down):
    block_e, run_first, run_slot, run_next, n_used = schedule
    n_slots, dw = xs.shape
    _, d, de = w_gate.shape
    nb = n_slots // MOE_TILE
    hbm = pl.BlockSpec(memory_space=pl.ANY)
    return pl.pallas_call(
        _expert_kernel,
        grid_spec=pltpu.PrefetchScalarGridSpec(
            num_scalar_prefetch=5,
            grid=(nb,),
            in_specs=[hbm, hbm, hbm, hbm],
            out_specs=pl.BlockSpec((MOE_TILE, dw), lambda i, be, fi, sl, nx, nu: (jnp.minimum(i, nu[0] - 1), 0)),
            scratch_shapes=[pltpu.VMEM((d, de), BF16), pltpu.VMEM((d, de), BF16), pltpu.VMEM((de, d), BF16),
                            pltpu.VMEM((X_SLOTS, MOE_TILE, dw), I32), pltpu.SemaphoreType.DMA((X_SLOTS,)),
                            pltpu.VMEM((2, d, de), F32), pltpu.VMEM((2, d, de), F32), pltpu.VMEM((2, de, d), F32),
                            pltpu.SemaphoreType.DMA((2, 3))]),
        out_shape=jax.ShapeDtypeStruct((n_slots, dw), I32),
        compiler_params=_params(("arbitrary",)),
        name="experts",
    )(block_e, run_first, run_slot, run_next, n_used, xs, w_gate, w_up, w_down)


def _sc_worker_id():
    return lax.axis_index("s") * SC_CORES + lax.axis_index("c")


def _sc_dispatch(t_rows, dest3, n_slots):
    _, dw = t_rows.shape
    n_chunks, _, w = dest3.shape
    per_worker = n_chunks // SC_WORKERS
    assert per_worker % 2 == 0
    mesh = plsc.VectorSubcoreMesh(core_axis_name="c", subcore_axis_name="s")
    dt = t_rows.dtype

    @functools.partial(
        pl.kernel, mesh=mesh,
        out_type=jax.ShapeDtypeStruct((n_slots, dw), dt),
        scratch_types=[pltpu.VMEM((TOP_K, w), I32), pltpu.VMEM((TOP_K, w), I32),
                       pltpu.VMEM((w, dw), dt), pltpu.VMEM((w, dw), dt),
                       pltpu.SemaphoreType.DMA, pltpu.SemaphoreType.DMA, pltpu.SemaphoreType.DMA],
    )
    def scatter_rows(t_hbm, dest_hbm, xs_hbm, idx_a, idx_b, rows_a, rows_b, sem_load, sem_a, sem_b):
        wid = _sc_worker_id()

        def scatter(idx_v, rows_v, sem):
            return [pltpu.async_copy(rows_v, xs_hbm.at[idx_v.at[k]], sem) for k in range(TOP_K)]

        @pl.loop(0, per_worker, step=2)
        def _(j):
            ca = wid * per_worker + j
            cb = ca + 1
            pltpu.sync_copy(dest_hbm.at[ca], idx_a)
            pltpu.sync_copy(t_hbm.at[pl.ds(ca * w, w)], rows_a)
            load_idx = pltpu.async_copy(dest_hbm.at[cb], idx_b, sem_load)
            load_rows = pltpu.async_copy(t_hbm.at[pl.ds(cb * w, w)], rows_b, sem_load)
            out_a = scatter(idx_a, rows_a, sem_a)
            load_idx.wait()
            load_rows.wait()
            out_b = scatter(idx_b, rows_b, sem_b)
            for cp in out_a + out_b:
                cp.wait()

    return scatter_rows(t_rows, dest3)


def _sc_combine(y_rows, dest3):
    _, dw = y_rows.shape
    n_chunks, _, w = dest3.shape
    per_worker = n_chunks // SC_WORKERS
    mesh = plsc.VectorSubcoreMesh(core_axis_name="c", subcore_axis_name="s")
    dt = y_rows.dtype

    @functools.partial(
        pl.kernel, mesh=mesh,
        out_type=jax.ShapeDtypeStruct((TOP_K, n_chunks * w, dw), dt),
        scratch_types=[pltpu.VMEM((TOP_K, w), I32), pltpu.VMEM((w, dw), dt), pltpu.VMEM((w, dw), dt),
                       pltpu.SemaphoreType.DMA, pltpu.SemaphoreType.DMA,
                       pltpu.SemaphoreType.DMA, pltpu.SemaphoreType.DMA],
    )
    def gather_rows(y_hbm, dest_hbm, out_hbm, idx_v, buf0, buf1, gsem0, gsem1, wsem0, wsem1):
        wid = _sc_worker_id()
        bufs, gsems, wsems = (buf0, buf1), (gsem0, gsem1), (wsem0, wsem1)

        @pl.loop(0, per_worker)
        def _(j):
            c = wid * per_worker + j
            pltpu.sync_copy(dest_hbm.at[c], idx_v)
            gathers = [None] * TOP_K
            writes = [None] * TOP_K
            gathers[0] = pltpu.async_copy(y_hbm.at[idx_v.at[0]], bufs[0], gsems[0])
            for k in range(TOP_K):
                b = k % 2
                if k + 1 < TOP_K:
                    if k >= 1:
                        writes[k - 1].wait()
                    gathers[k + 1] = pltpu.async_copy(y_hbm.at[idx_v.at[k + 1]], bufs[1 - b], gsems[1 - b])
                gathers[k].wait()
                writes[k] = pltpu.async_copy(bufs[b], out_hbm.at[k, pl.ds(c * w, w)], wsems[b])
            writes[TOP_K - 2].wait()
            writes[TOP_K - 1].wait()

    return gather_rows(y_rows, dest3)


def _final_kernel(base_ref, y_ref, gw_ref, g_ref, *rest, normalize):
    o_ref = rest[-1]
    h = base_ref[...]
    gw = gw_ref[...]
    for k in range(TOP_K):
        h = h + gw[:, k:k + 1] * _unpack_rows(y_ref[k])
    o_ref[...] = _rms(h, g_ref[...]) if normalize else h


def _final(base, yg, gw_rows, g_final, normalize, out_prev, part, n_parts):
    n, d = base.shape
    t = min(FIN_TILE, n)
    blk0 = part * (n // t)
    in_specs = [pl.BlockSpec((t, d), lambda i: (i, 0)), pl.BlockSpec((TOP_K, t, d // 2), lambda i: (0, i, 0)),
                pl.BlockSpec((t, TOP_K), lambda i: (i, 0)), _full((1, d))]
    args = [base, yg, gw_rows, g_final]
    aliases = {}
    if out_prev is not None:
        in_specs.append(pl.BlockSpec(memory_space=pl.ANY))
        args.append(out_prev)
        aliases = {len(args) - 1: 0}
    return pl.pallas_call(
        functools.partial(_final_kernel, normalize=normalize),
        grid=(n // t,),
        in_specs=in_specs,
        out_specs=pl.BlockSpec((t, d), lambda i: (i + blk0, 0)),
        out_shape=jax.ShapeDtypeStruct((n * n_parts, d), F32),
        input_output_aliases=aliases,
        compiler_params=_params(("parallel",)),
        name="final",
    )(*args)


def _ssm_matrices(a_re, a_im, log_dt, b_re, b_im, c_re, c_im):
    lr, li = a_re.astype(F32), a_im.astype(F32)
    dt = jnp.exp(log_dt.astype(F32))[:, None]
    mag = jnp.exp(lr * dt)
    ab_re, ab_im = mag * jnp.cos(li * dt), mag * jnp.sin(li * dt)
    den = lr * lr + li * li
    zr, zi = ab_re - 1.0, ab_im
    k_re = (zr * lr + zi * li) / den
    k_im = (zi * lr - zr * li) / den
    br, bi = b_re.astype(F32), b_im.astype(F32)
    bb_re = k_re[..., None] * br - k_im[..., None] * bi
    bb_im = k_re[..., None] * bi + k_im[..., None] * br
    eye = jnp.eye(SSM_GROUPS, dtype=F32)

    def in_mat(bb):
        return jnp.einsum("gpc,gh->gchp", bb, eye).reshape(SSM_WIDTH, N_STATE)

    def out_mat(c):
        return jnp.einsum("gcp,gh->gphc", c.astype(F32), eye).reshape(N_STATE, SSM_WIDTH)

    bmat = jnp.concatenate([in_mat(bb_re), in_mat(bb_im)], axis=1)
    cmat = jnp.concatenate([out_mat(c_re), -out_mat(c_im)], axis=0)
    tile = lambda a: jnp.broadcast_to(a.reshape(1, N_STATE), (SUBLANES, N_STATE))
    return bmat.astype(BF16), tile(ab_re), tile(ab_im), cmat.astype(BF16)


def _plan_kernel(pstart_ref, eidx_ref, rank_ref, dest_ref, *, window):
    eidx = eidx_ref[...]
    dest = rank_ref[...]
    for e in range(N_EXPERTS):
        dest = dest + jnp.where(eidx == e, pstart_ref[e], 0)
    for c in range(dest.shape[1] // window):
        dest_ref[c] = dest[:, c * window:(c + 1) * window]


def _plan(pstart, eidx, rank, window):
    k, n = eidx.shape
    t = min(PLAN_TILE, n)
    cols = pl.BlockSpec((k, t), lambda i, ps: (0, i))
    return pl.pallas_call(
        functools.partial(_plan_kernel, window=window),
        grid_spec=pltpu.PrefetchScalarGridSpec(
            num_scalar_prefetch=1,
            grid=(n // t,),
            in_specs=[cols, cols],
            out_specs=pl.BlockSpec((t // window, k, window), lambda i, ps: (i, 0, 0))),
        out_shape=jax.ShapeDtypeStruct((n // window, k, window), I32),
        compiler_params=_params(("parallel",)),
        name="plan",
    )(pstart, eidx, rank)


def _block_schedule(counts, n_tok):
    padded = ((counts + MOE_TILE - 1) // MOE_TILE) * MOE_TILE
    pend = jnp.cumsum(padded)
    pstart = (pend - padded).astype(I32)
    n_slots = n_tok * TOP_K + N_EXPERTS * MOE_TILE
    nb = n_slots // MOE_TILE
    n_used = (pend[-1] // MOE_TILE).astype(I32)
    blk = jnp.arange(nb, dtype=I32)
    block_e = jnp.sum((pend[None, :] <= (jnp.minimum(blk, n_used - 1) * MOE_TILE)[:, None]).astype(I32), axis=1)
    block_e = jnp.minimum(block_e, N_EXPERTS - 1)
    eid = jnp.arange(N_EXPERTS, dtype=I32)
    active = padded > 0
    run_of = jnp.cumsum(active.astype(I32)) - 1
    later = jnp.where(jnp.logical_and(active[None, :], eid[None, :] > eid[:, None]), eid[None, :], N_EXPERTS)
    next_e = jnp.min(later, axis=1)
    next_e = jnp.where(next_e < N_EXPERTS, next_e, -1)
    onehot = block_e[:, None] == eid[None, :]
    lookup = lambda table: jnp.sum(jnp.where(onehot, table[None, :], 0), axis=1).astype(I32)
    run_first = jnp.logical_and(blk * MOE_TILE == lookup(pstart), blk < n_used).astype(I32)
    schedule = (block_e, run_first, lookup(run_of % 2), lookup(next_e), n_used.reshape(1))
    return pstart, schedule, n_slots


def kernel(x, mem, positions, g_mix, w_in, a_re, a_im, log_dt, b_re, b_im, c_re, c_im, d_skip, w_glu, g_ssm_out, lam_q1, lam_k1, lam_q2, lam_k2, g_sub, w_out, g_x, g_mem, wq_x, wk_x, wv_x, wo_x, g_ffn, w_router, router_bias, w_gate, w_up, w_down, ws_gate, ws_up, ws_down, g_final):
    b, l, d = x.shape
    n = b * l
    depth = w_in.shape[0]
    row = lambda a: a.reshape(1, -1).astype(F32)
    inv = ROPE_THETA ** (-jnp.arange(0, DA_QKDIM, 2, dtype=F32) / DA_QKDIM)
    inv_row = jnp.tile(inv, LANES // inv.shape[0]).reshape(1, LANES)
    pos3 = positions.reshape(b, l, 1)

    h = x
    for i in range(depth):
        lambda_init = 0.8 - 0.6 * math.exp(-0.3 * i)
        u, q, k, v = _inproj(h, pos3, row(g_mix[i]), w_in[i].astype(BF16), inv_row)

        bmat, are8, aim8, cmat = _ssm_matrices(a_re[i], a_im[i], log_dt[i], b_re[i], b_im[i], c_re[i], c_im[i])
        y_ssm = _ssm(u, bmat, are8, aim8, cmat, row(d_skip[i]), w_glu[i].astype(BF16), row(g_ssm_out[i]))

        lam = (jnp.exp(jnp.sum(lam_q1[i].astype(F32) * lam_k1[i].astype(F32)))
               - jnp.exp(jnp.sum(lam_q2[i].astype(F32) * lam_k2[i].astype(F32))) + lambda_init).reshape(1)
        y_att = _diff_attention(q, k, v, lam.astype(F32), row(g_sub[i]), lambda_init)

        kmem, vmem = _memkv(mem, row(g_mem[i]), wk_x[i].astype(BF16), wv_x[i].astype(BF16))
        wo = w_out[i].astype(BF16)
        h2 = _mix(h, y_ssm, y_att, wo[:SSM_WIDTH], wo[SSM_WIDTH:], row(g_x[i]), wq_x[i].astype(BF16),
                  kmem, vmem, wo_x[i].astype(BF16))

        wsgu = jnp.concatenate([ws_gate[i], ws_up[i]], axis=1).astype(BF16)
        wr = w_router[i].T.astype(F32)
        wr_hi = wr.astype(BF16)
        wr_split = jnp.concatenate([wr_hi, (wr - wr_hi.astype(F32)).astype(BF16)], axis=0)
        n_part = n // MOE_PARTS
        out = None
        for part in range(MOE_PARTS):
            t_pk, base, eidx, gw, rank, counts = _route(
                h2.reshape(n, d), row(g_ffn[i]), wr_split,
                router_bias[i].reshape(N_EXPERTS, 1).astype(F32), wsgu, ws_down[i].astype(BF16), part, MOE_PARTS)
            pstart, schedule, n_slots = _block_schedule(counts[:, 0], n_part)
            dest3 = _plan(pstart, eidx, rank, SC_WINDOW)
            xs = _sc_dispatch(t_pk, dest3, n_slots)
            ys = _experts(schedule, xs, w_gate[i], w_up[i], w_down[i])
            yg = _sc_combine(ys, dest3)
            out = _final(base, yg, gw.T, row(g_final), i == depth - 1, out, part, MOE_PARTS)
        h = out.reshape(b, l, d)
    return h
```

```python
import functools
import math

import jax
import jax.numpy as jnp
from jax import lax
from jax.experimental import pallas as pl
from jax.experimental.pallas import tpu as pltpu
from jax.experimental.pallas import tpu_sc as plsc

F32 = jnp.float32
BF16 = jnp.bfloat16
I32 = jnp.int32

D_MODEL = 1024
SSM_WIDTH = 512
ATTN_WIDTH = 512
SSM_GROUP = 16
SSM_GROUPS = 32
SSM_STATE = 64
N_STATE = SSM_GROUPS * SSM_STATE
DA_HEADS = 4
DA_VDIM = 128
DA_QKDIM = 64
ROPE_THETA = 10000.0
ROPE_FREQS = DA_QKDIM // 2
X_HEADS = 4
X_HEAD_DIM = 256
N_EXPERTS = 64
TOP_K = 8
N_EXPERT_GROUPS = 8
PER_GROUP = N_EXPERTS // N_EXPERT_GROUPS
TOPK_GROUPS = 4
D_EXPERT = 256
ROUTED_SCALE = 2.5
EPS = 1e-6

VMEM_LIMIT_V7X = 56 * 1024 * 1024
SUBLANES = 8
LANES = 128
HALF_WORD_BITS = 16
HIGH_HALF_MASK = -(1 << HALF_WORD_BITS)

IN_TILE = 1024
SSM_STEPS = 128
SSM_COLS = 1024
SSM_CH = 128
SSM_SLAB = 512
ATT_TILE = 2048
ATT_ROWS = 256
MIX_TILE = 1024
ROUTE_TILE = 1024
MOE_TILE = 512
X_SLOTS = 3
PLAN_TILE = 2048
MOE_PARTS = 2
FIN_TILE = 512

SC_CORES = 2
SC_WORKERS = 32
SC_WINDOW = 64


def _params(sem):
    return pltpu.CompilerParams(dimension_semantics=sem, vmem_limit_bytes=VMEM_LIMIT_V7X)


def _rms(x, g):
    return x * lax.rsqrt(jnp.mean(x * x, axis=-1, keepdims=True) + EPS) * g


def _full(shape):
    return pl.BlockSpec(shape, lambda *_: (0,) * len(shape))


def _pack_rows(a):
    w = a.shape[1] // 2
    bits = lambda v: lax.bitcast_convert_type(v.astype(BF16).astype(F32), I32)
    return (bits(a[:, w:]) & jnp.int32(HIGH_HALF_MASK)) | lax.shift_right_logical(bits(a[:, :w]), HALF_WORD_BITS)


def _unpack_rows(p):
    lo = lax.bitcast_convert_type(lax.shift_left(p, HALF_WORD_BITS), F32)
    hi = lax.bitcast_convert_type(p & jnp.int32(HIGH_HALF_MASK), F32)
    return jnp.concatenate([lo, hi], axis=1)


def _inproj_kernel(x_ref, pos_ref, g_ref, w_ref, inv_ref, u_ref, q_ref, k_ref, v_ref):
    x = x_ref[0]
    a = _rms(x, g_ref[...]).astype(BF16)
    z = jnp.dot(a, w_ref[...], preferred_element_type=F32)
    u_ref[0] = z[:, :SSM_WIDTH]
    n_grp = LANES // ROPE_FREQS
    tq = x.shape[0] // n_grp
    pos = pos_ref[0].astype(F32)
    lane_c = lax.broadcasted_iota(I32, (tq, LANES), 1)
    pos_c = jnp.zeros((tq, LANES), F32)
    for j in range(n_grp):
        pos_c = jnp.where(lane_c // ROPE_FREQS == j, pos[j * tq:(j + 1) * tq], pos_c)
    ang = pos_c * inv_ref[...]

    def spread(c):
        rows = []
        for j in range(n_grp):
            y = c if j == 0 else pltpu.roll(c, LANES - ROPE_FREQS * j, 1)
            w = ROPE_FREQS
            while w < LANES:
                y = jnp.where(lane_c < w, y, pltpu.roll(y, w, 1))
                w *= 2
            rows.append(y)
        return jnp.concatenate(rows, axis=0)

    cos = jnp.concatenate([spread(jnp.cos(ang))] * (ATTN_WIDTH // LANES), axis=1)
    sin = jnp.concatenate([spread(jnp.sin(ang))] * (ATTN_WIDTH // LANES), axis=1)
    lane = lax.broadcasted_iota(I32, cos.shape, 1)
    first = (lane & (DA_QKDIM - 1)) < DA_QKDIM // 2
    half = DA_QKDIM // 2

    def rope(t):
        rot = jnp.where(first, -pltpu.roll(t, ATTN_WIDTH - half, 1), pltpu.roll(t, half, 1))
        return t * cos + rot * sin

    q = z[:, SSM_WIDTH:SSM_WIDTH + ATTN_WIDTH]
    k = z[:, SSM_WIDTH + ATTN_WIDTH:SSM_WIDTH + 2 * ATTN_WIDTH]
    q_ref[0] = (rope(q) * (DA_QKDIM ** -0.5)).astype(BF16)
    k_ref[0] = rope(k).astype(BF16)
    v_ref[0] = z[:, SSM_WIDTH + 2 * ATTN_WIDTH:].astype(BF16)


def _inproj(x, pos3, g_mix, w_in, inv_row):
    b, l, d = x.shape
    t = min(IN_TILE, l)
    n_out = w_in.shape[1]
    row = lambda w: pl.BlockSpec((1, t, w), lambda i, j: (i, j, 0))
    return pl.pallas_call(
        _inproj_kernel,
        grid=(b, l // t),
        in_specs=[row(d), row(1), _full((1, d)), _full((d, n_out)), _full((1, LANES))],
        out_specs=[row(SSM_WIDTH), row(ATTN_WIDTH), row(ATTN_WIDTH), row(ATTN_WIDTH)],
        out_shape=[jax.ShapeDtypeStruct((b, l, SSM_WIDTH), F32)]
        + [jax.ShapeDtypeStruct((b, l, ATTN_WIDTH), BF16)] * 3,
        compiler_params=_params(("parallel", "parallel")),
        name="inproj",
    )(x, pos3, g_mix, w_in, inv_row)


def _ssm_kernel(u_ref, bm_ref, are_ref, aim_ref, cm_ref, dm_ref, dskip_ref, wglu_ref, g_ref, o_ref,
                c_ref, st_ref, up_ref, *, steps):
    @pl.when(pl.program_id(0) == 0)
    def _():
        st_ref[...] = jnp.zeros_like(st_ref)
        up_ref[...] = jnp.zeros_like(up_ref)

    pairs = steps // 2
    rows = pairs * SUBLANES
    u = pltpu.einshape("bts->tbs", u_ref[...]).reshape(pairs, 2, SUBLANES, SSM_WIDTH)
    u_even = u[:, 0].reshape(rows, SSM_WIDTH)
    u_odd = u[:, 1].reshape(rows, SSM_WIDTH)
    u_before = jnp.concatenate([up_ref[...], u_odd[:rows - SUBLANES]], axis=0)
    up_ref[...] = u_odd[rows - SUBLANES:]
    ue, uo, ub = u_even.astype(BF16), u_odd.astype(BF16), u_before.astype(BF16)

    n_slabs = SSM_WIDTH // SSM_CH
    for s in range(n_slabs):
        ch = slice(s * SSM_CH, (s + 1) * SSM_CH)
        lhs = jnp.concatenate([ue[:, ch], ub[:, ch]], axis=1)
        for part in (0, 1):
            cols = slice(part * N_STATE + s * SSM_SLAB, part * N_STATE + (s + 1) * SSM_SLAB)
            c_ref[:, cols] = jnp.dot(lhs, bm_ref[s, :, part * SSM_SLAB:(part + 1) * SSM_SLAB],
                                     preferred_element_type=F32)

    for c0 in range(0, N_STATE, SSM_COLS):
        re = slice(c0, c0 + SSM_COLS)
        im = slice(N_STATE + c0, N_STATE + c0 + SSM_COLS)
        ar = are_ref[:, re]
        ai = aim_ref[:, re]
        sr, si = st_ref[:, re], st_ref[:, im]
        for k in range(pairs):
            blk = slice(k * SUBLANES, (k + 1) * SUBLANES)
            sr, si = ar * sr - ai * si + c_ref[blk, re], ar * si + ai * sr + c_ref[blk, im]
            c_ref[blk, re] = sr
            c_ref[blk, im] = si
        st_ref[:, re] = sr
        st_ref[:, im] = si

    y_even, y_odd = [], []
    for s in range(n_slabs):
        ch = slice(s * SSM_CH, (s + 1) * SSM_CH)
        acc = jnp.dot(uo[:, ch], dm_ref[s], preferred_element_type=F32)
        both = None
        for part in (0, 1):
            cols = slice(part * N_STATE + s * SSM_SLAB, part * N_STATE + (s + 1) * SSM_SLAB)
            d = jnp.dot(c_ref[:, cols].astype(BF16), cm_ref[s, part * SSM_SLAB:(part + 1) * SSM_SLAB],
                        preferred_element_type=F32)
            both = d if both is None else both + d
        y_even.append(both[:, :SSM_CH])
        y_odd.append(both[:, SSM_CH:] + acc)
    y = jnp.concatenate([jnp.concatenate(y_even, axis=1), jnp.concatenate(y_odd, axis=1)], axis=0)
    y = y + dskip_ref[...] * jnp.concatenate([u_even, u_odd], axis=0)
    y = jax.nn.gelu(y)
    y = y * jax.nn.sigmoid(jnp.dot(y.astype(BF16), wglu_ref[...], preferred_element_type=F32))
    out = _rms(y, g_ref[...])
    out = jnp.stack([out[:rows].reshape(pairs, SUBLANES, SSM_WIDTH),
                     out[rows:].reshape(pairs, SUBLANES, SSM_WIDTH)], axis=1).reshape(steps, SUBLANES, SSM_WIDTH)
    o_ref[...] = pltpu.einshape("tbs->bts", out).astype(BF16)


def _ssm(u, mats, d_skip, w_glu, g_out):
    bmat, a2_re8, a2_im8, cmat, dmat = mats
    batch, l, _ = u.shape
    assert batch == SUBLANES, "one time step of all sequences must fill the sublanes"
    steps = min(SSM_STEPS, l)
    assert steps % 2 == 0
    rows = steps // 2 * batch
    n_slabs = SSM_WIDTH // SSM_CH
    seq = pl.BlockSpec((batch, steps, SSM_WIDTH), lambda i: (0, i, 0))
    return pl.pallas_call(
        functools.partial(_ssm_kernel, steps=steps),
        grid=(l // steps,),
        in_specs=[seq,
                  _full((n_slabs, 2 * SSM_CH, 2 * SSM_SLAB)), _full((SUBLANES, N_STATE)), _full((SUBLANES, N_STATE)),
                  _full((n_slabs, 2 * SSM_SLAB, 2 * SSM_CH)), _full((n_slabs, SSM_CH, SSM_CH)),
                  _full((1, SSM_WIDTH)), _full((SSM_WIDTH, SSM_WIDTH)), _full((1, SSM_WIDTH))],
        out_specs=seq,
        out_shape=jax.ShapeDtypeStruct((batch, l, SSM_WIDTH), BF16),
        scratch_shapes=[pltpu.VMEM((rows, 2 * N_STATE), F32), pltpu.VMEM((SUBLANES, 2 * N_STATE), F32),
                        pltpu.VMEM((SUBLANES, SSM_WIDTH), F32)],
        compiler_params=_params(("arbitrary",)),
        name="ssm",
    )(u, bmat, a2_re8, a2_im8, cmat, dmat, d_skip, w_glu, g_out)


def _attn_kernel(qi_ref, ki_ref, lam_ref, q_ref, k_ref, v_ref, g_ref, o_ref, qs_ref, m_ref, acc_ref, *,
                 tile, out_scale):
    qi = qi_ref[pl.program_id(2)]
    ki = ki_ref[pl.program_id(2)]

    @pl.when(ki == 0)
    def _():
        q = q_ref[0]
        lane = lax.broadcasted_iota(I32, q.shape, 1)
        zero = jnp.zeros_like(q)
        qs_ref[:tile] = jnp.where(lane < DA_QKDIM, q, zero)
        qs_ref[tile:] = jnp.where(lane >= DA_QKDIM, q, zero)
        m_ref[...] = jnp.full_like(m_ref, -jnp.inf)
        acc_ref[...] = jnp.zeros_like(acc_ref)

    def update(masked):
        v_ext = jnp.concatenate([v_ref[0], jnp.ones((tile, DA_VDIM), BF16)], axis=1)
        rb = min(ATT_ROWS, tile)
        n_rb = 2 * tile // rb

        def n_keys(r):
            return (r * rb) % tile + rb if masked else tile

        def scores(r):
            return lax.dot_general(qs_ref[r * rb:(r + 1) * rb], k_ref[0, :n_keys(r)], (((1,), (1,)), ((), ())),
                                   preferred_element_type=F32)

        s_next = scores(0)
        for r in range(n_rb):
            s = s_next
            if r + 1 < n_rb:
                s_next = scores(r + 1)
            rows = slice(r * rb, (r + 1) * rb)
            nk = n_keys(r)
            if masked:
                tri = lax.broadcasted_iota(I32, (rb, rb), 1) <= lax.broadcasted_iota(I32, (rb, rb), 0)
                edge = jnp.where(tri, s[:, nk - rb:], jnp.finfo(F32).min)
                s = edge if nk == rb else jnp.concatenate([s[:, :nk - rb], edge], axis=1)
            m_old = m_ref[rows]
            m_new = jnp.maximum(m_old, jnp.max(s, axis=1, keepdims=True))
            p = jnp.exp(s - jnp.concatenate([m_new] * (nk // DA_VDIM), axis=1))
            alpha = jnp.exp(m_old - m_new)
            acc_ref[rows] = (jnp.concatenate([alpha, alpha], axis=1) * acc_ref[rows]
                             + jnp.dot(p.astype(BF16), v_ext[:nk], preferred_element_type=F32))
            m_ref[rows] = m_new

    @pl.when(ki < qi)
    def _():
        update(False)

    @pl.when(ki == qi)
    def _():
        update(True)
        o = acc_ref[:, :DA_VDIM] / acc_ref[:, DA_VDIM:]
        o = o[:tile] - lam_ref[0] * o[tile:]
        o_ref[0] = (_rms(o, g_ref[...]) * out_scale).astype(BF16)


def _diff_attention(q, k, v, lam, g_sub, lambda_init):
    b, l, _ = q.shape
    t = min(ATT_TILE, l)
    n = l // t
    pairs = [(qi, ki) for qi in range(n) for ki in range(qi + 1)]
    qi_tab = jnp.asarray([p[0] for p in pairs], I32)
    ki_tab = jnp.asarray([p[1] for p in pairs], I32)
    qspec = pl.BlockSpec((1, t, DA_VDIM), lambda bi, h, s, qt, kt: (bi, qt[s], h))
    kspec = pl.BlockSpec((1, t, DA_VDIM), lambda bi, h, s, qt, kt: (bi, kt[s], h))
    return pl.pallas_call(
        functools.partial(_attn_kernel, tile=t, out_scale=1.0 - lambda_init),
        grid_spec=pltpu.PrefetchScalarGridSpec(
            num_scalar_prefetch=2,
            grid=(b, DA_HEADS, len(pairs)),
            in_specs=[pl.BlockSpec(memory_space=pltpu.SMEM), qspec, kspec, kspec,
                      pl.BlockSpec((1, DA_VDIM), lambda bi, h, s, qt, kt: (0, 0))],
            out_specs=qspec,
            scratch_shapes=[pltpu.VMEM((2 * t, DA_VDIM), BF16), pltpu.VMEM((2 * t, DA_VDIM), F32),
                            pltpu.VMEM((2 * t, 2 * DA_VDIM), F32)]),
        out_shape=jax.ShapeDtypeStruct((b, l, ATTN_WIDTH), BF16),
        compiler_params=_params(("parallel", "parallel", "arbitrary")),
        name="diffattn",
    )(qi_tab, ki_tab, lam, q, k, v, g_sub)


def _memkv_kernel(m_ref, g_ref, wk_ref, wv_ref, k_ref, v_ref):
    a = _rms(m_ref[0], g_ref[...]).astype(BF16)
    k_ref[0] = jnp.dot(a, wk_ref[...], preferred_element_type=F32).astype(BF16)
    v_ref[0] = jnp.dot(a, wv_ref[...], preferred_element_type=F32).astype(BF16)


def _memkv(mem, g_mem, wk, wv):
    b, m, d = mem.shape
    blk = pl.BlockSpec((1, m, d), lambda i: (i, 0, 0))
    return pl.pallas_call(
        _memkv_kernel,
        grid=(b,),
        in_specs=[blk, _full((1, d)), _full((d, d)), _full((d, d))],
        out_specs=[blk, blk],
        out_shape=[jax.ShapeDtypeStruct((b, m, d), BF16)] * 2,
        compiler_params=_params(("parallel",)),
        name="memkv",
    )(mem, g_mem, wk, wv)


def _mix_kernel(x_ref, ys_ref, ya_ref, wo1_ref, wo2_ref, gx_ref, wq_ref, km_ref, vm_ref, wo_ref, h_ref):
    h = (x_ref[0]
         + jnp.dot(ys_ref[0], wo1_ref[...], preferred_element_type=F32)
         + jnp.dot(ya_ref[0], wo2_ref[...], preferred_element_type=F32))
    hq = _rms(h, gx_ref[...]).astype(BF16)
    q = jnp.dot(hq, wq_ref[...], preferred_element_type=F32).astype(BF16)
    outs = []
    for hd in range(X_HEADS):
        cols = slice(hd * X_HEAD_DIM, (hd + 1) * X_HEAD_DIM)
        s = lax.dot_general(q[:, cols], km_ref[0, :, cols], (((1,), (1,)), ((), ())),
                            preferred_element_type=F32) * (X_HEAD_DIM ** -0.5)
        s = s - jnp.max(s, axis=1, keepdims=True)
        p = jnp.exp(s)
        p = p / jnp.sum(p, axis=1, keepdims=True)
        outs.append(jnp.dot(p.astype(BF16), vm_ref[0, :, cols], preferred_element_type=F32).astype(BF16))
    o = jnp.concatenate(outs, axis=1)
    h_ref[0] = h + jnp.dot(o, wo_ref[...], preferred_element_type=F32)


def _mix(x, y_ssm, y_att, wo1, wo2, g_x, wq, kmem, vmem, wo):
    b, l, d = x.shape
    t = min(MIX_TILE, l)
    m = kmem.shape[1]
    row = lambda w: pl.BlockSpec((1, t, w), lambda i, j: (i, j, 0))
    mem = pl.BlockSpec((1, m, d), lambda i, j: (i, 0, 0))
    return pl.pallas_call(
        _mix_kernel,
        grid=(b, l // t),
        in_specs=[row(d), row(SSM_WIDTH), row(ATTN_WIDTH), _full((SSM_WIDTH, d)), _full((ATTN_WIDTH, d)),
                  _full((1, d)), _full((d, d)), mem, mem, _full((d, d))],
        out_specs=row(d),
        out_shape=jax.ShapeDtypeStruct((b, l, d), F32),
        compiler_params=_params(("parallel", "parallel")),
        name="mix",
    )(x, y_ssm, y_att, wo1, wo2, g_x, wq, kmem, vmem, wo)


def _first_index(hit, idx, sentinel):
    return jnp.min(jnp.where(hit, idx, sentinel), axis=0, keepdims=True)


def _route_kernel(h_ref, g_ref, wr_ref, bias_ref, wsgu_ref, wsd_ref, tri_ref,
                  t_ref, base_ref, eidx_ref, gw_ref, rank_ref, cnt_ref, carry_ref):
    @pl.when(pl.program_id(0) == 0)
    def _():
        carry_ref[...] = jnp.zeros_like(carry_ref)

    h = h_ref[...]
    t = _rms(h, g_ref[...])
    tb = t.astype(BF16)
    t_ref[...] = _pack_rows(t)
    gu = jnp.dot(tb, wsgu_ref[...], preferred_element_type=F32)
    hid = jax.nn.silu(gu[:, :D_EXPERT]) * gu[:, D_EXPERT:]
    base_ref[...] = h + jnp.dot(hid.astype(BF16), wsd_ref[...], preferred_element_type=F32)

    nt = (((1,), (1,)), ((), ()))
    t_lo = (t - tb.astype(F32)).astype(BF16)
    both = lax.dot_general(wr_ref[...], tb, nt, preferred_element_type=F32)
    logits = (both[:N_EXPERTS] + both[N_EXPERTS:]
              + lax.dot_general(wr_ref[:N_EXPERTS], t_lo, nt, preferred_element_type=F32))
    scores = jax.nn.sigmoid(logits)
    biased = scores + bias_ref[...]
    n_tok = scores.shape[1]
    neg = -jnp.inf
    sub = lax.broadcasted_iota(I32, (PER_GROUP, n_tok), 0)

    gs = []
    for g in range(N_EXPERT_GROUPS):
        blk = biased[g * PER_GROUP:(g + 1) * PER_GROUP]
        m1 = jnp.max(blk, axis=0, keepdims=True)
        i1 = _first_index(blk == m1, sub, PER_GROUP)
        m2 = jnp.max(jnp.where(sub == i1, neg, blk), axis=0, keepdims=True)
        gs.append(m1 + m2)
    gs = jnp.concatenate(gs, axis=0)

    gsel = jnp.zeros(gs.shape, jnp.bool_)
    for _ in range(TOPK_GROUPS):
        m = jnp.max(gs, axis=0, keepdims=True)
        hit = sub == _first_index(gs == m, sub, N_EXPERT_GROUPS)
        gsel = jnp.logical_or(gsel, hit)
        gs = jnp.where(hit, neg, gs)

    masked = jnp.concatenate(
        [jnp.where(gsel[g:g + 1], biased[g * PER_GROUP:(g + 1) * PER_GROUP], neg)
         for g in range(N_EXPERT_GROUPS)], axis=0)
    eid = lax.broadcasted_iota(I32, masked.shape, 0)
    sel = jnp.zeros(masked.shape, jnp.bool_)
    idxs, gws = [], []
    for _ in range(TOP_K):
        m = jnp.max(masked, axis=0, keepdims=True)
        i = _first_index(masked == m, eid, N_EXPERTS)
        hit = eid == i
        idxs.append(i)
        gws.append(jnp.sum(jnp.where(hit, scores, 0.0), axis=0, keepdims=True))
        sel = jnp.logical_or(sel, hit)
        masked = jnp.where(hit, neg, masked)
    eidx = jnp.concatenate(idxs, axis=0)
    gw = jnp.concatenate(gws, axis=0)
    gw = gw / jnp.sum(gw, axis=0, keepdims=True) * ROUTED_SCALE
    eidx_ref[...] = eidx
    gw_ref[...] = gw

    before = jnp.dot(sel.astype(BF16), tri_ref[...], preferred_element_type=F32) + carry_ref[...]
    rank_ref[...] = jnp.concatenate(
        [jnp.sum(jnp.where(eid == idxs[k], before, 0.0), axis=0, keepdims=True) for k in range(TOP_K)],
        axis=0).astype(I32)
    carry = carry_ref[...] + jnp.sum(sel.astype(F32), axis=1, keepdims=True)
    carry_ref[...] = carry
    cnt_ref[...] = carry.astype(I32)


def _route(h2, g_ffn, wr_t, bias_col, wsgu, wsd, part, n_parts):
    n_all, d = h2.shape
    n = n_all // n_parts
    t = min(ROUTE_TILE, n)
    blk0 = part * (n // t)
    tri = (lax.broadcasted_iota(I32, (t, t), 0) < lax.broadcasted_iota(I32, (t, t), 1)).astype(BF16)
    row = lambda w: pl.BlockSpec((t, w), lambda i: (i, 0))
    col = pl.BlockSpec((TOP_K, t), lambda i: (0, i))
    return pl.pallas_call(
        _route_kernel,
        grid=(n // t,),
        in_specs=[pl.BlockSpec((t, d), lambda i: (i + blk0, 0)), _full((1, d)), _full((2 * N_EXPERTS, d)),
                  _full((N_EXPERTS, 1)),
                  _full((d, 2 * D_EXPERT)), _full((D_EXPERT, d)), _full((t, t))],
        out_specs=[row(d // 2), row(d), col, col, col, _full((N_EXPERTS, 1))],
        out_shape=[jax.ShapeDtypeStruct((n, d // 2), I32), jax.ShapeDtypeStruct((n, d), F32),
                   jax.ShapeDtypeStruct((TOP_K, n), I32), jax.ShapeDtypeStruct((TOP_K, n), F32),
                   jax.ShapeDtypeStruct((TOP_K, n), I32), jax.ShapeDtypeStruct((N_EXPERTS, 1), I32)],
        scratch_shapes=[pltpu.VMEM((N_EXPERTS, 1), F32)],
        compiler_params=_params(("arbitrary",)),
        name="route",
    )(h2, g_ffn, wr_t, bias_col, wsgu, wsd, tri)


def _expert_kernel(be_ref, first_ref, slot_ref, next_ref, nu_ref, x_hbm, wg_hbm, wu_hbm, wd_hbm, y_ref,
                   wg_s, wu_s, wd_s, x_buf, x_sem, wg_buf, wu_buf, wd_buf, w_sem):
    i = pl.program_id(0)
    n_used = nu_ref[0]

    def fetch(step):
        slot = step % X_SLOTS
        rows = pl.ds(pl.multiple_of(step * MOE_TILE, MOE_TILE), MOE_TILE)
        return pltpu.make_async_copy(x_hbm.at[rows], x_buf.at[slot], x_sem.at[slot])

    def weight_fetch(expert, slot):
        pairs = ((wg_hbm, wg_buf), (wu_hbm, wu_buf), (wd_hbm, wd_buf))
        return [pltpu.make_async_copy(src.at[expert], buf.at[slot], w_sem.at[slot, j])
                for j, (src, buf) in enumerate(pairs)]

    @pl.when(i == 0)
    def _():
        for j in range(X_SLOTS - 1):
            @pl.when(j < n_used)
            def _(j=j):
                fetch(j).start()
        for cp in weight_fetch(be_ref[0], 0):
            cp.start()

    @pl.when(i + (X_SLOTS - 1) < n_used)
    def _():
        fetch(i + (X_SLOTS - 1)).start()

    @pl.when(first_ref[i] == 1)
    def _():
        slot = slot_ref[i]

        @pl.when(next_ref[i] >= 0)
        def _():
            for cp in weight_fetch(next_ref[i], 1 - slot):
                cp.start()

        for cp in weight_fetch(be_ref[i], slot):
            cp.wait()
        wg_s[...] = wg_buf[slot].astype(BF16)
        wu_s[...] = wu_buf[slot].astype(BF16)
        wd_s[...] = wd_buf[slot].astype(BF16)

    @pl.when(i < n_used)
    def _():
        fetch(i).wait()
        x = _unpack_rows(x_buf[i % X_SLOTS]).astype(BF16)
        gate = jnp.dot(x, wg_s[...], preferred_element_type=F32)
        up = jnp.dot(x, wu_s[...], preferred_element_type=F32)
        hid = (jax.nn.silu(gate) * up).astype(BF16)
        y_ref[...] = _pack_rows(jnp.dot(hid, wd_s[...], preferred_element_type=F32))


def _experts(schedule, xs, w_gate, w_up, w_down):
    block_e, run_first, run_slot, run_next, n_used = schedule
    n_slots, dw = xs.shape
    _, d, de = w_gate.shape
    nb = n_slots // MOE_TILE
    hbm = pl.BlockSpec(memory_space=pl.ANY)
    return pl.pallas_call(
        _expert_kernel,
        grid_spec=pltpu.PrefetchScalarGridSpec(
            num_scalar_prefetch=5,
            grid=(nb,),
            in_specs=[hbm, hbm, hbm, hbm],
            out_specs=pl.BlockSpec((MOE_TILE, dw), lambda i, be, fi, sl, nx, nu: (jnp.minimum(i, nu[0] - 1), 0)),
            scratch_shapes=[pltpu.VMEM((d, de), BF16), pltpu.VMEM((d, de), BF16), pltpu.VMEM((de, d), BF16),
                            pltpu.VMEM((X_SLOTS, MOE_TILE, dw), I32), pltpu.SemaphoreType.DMA((X_SLOTS,)),
                            pltpu.VMEM((2, d, de), F32), pltpu.VMEM((2, d, de), F32), pltpu.VMEM((2, de, d), F32),
                            pltpu.SemaphoreType.DMA((2, 3))]),
        out_shape=jax.ShapeDtypeStruct((n_slots, dw), I32),
        compiler_params=_params(("arbitrary",)),
        name="experts",
    )(block_e, run_first, run_slot, run_next, n_used, xs, w_gate, w_up, w_down)


def _sc_worker_id():
    return lax.axis_index("s") * SC_CORES + lax.axis_index("c")


def _sc_dispatch(t_rows, dest3, n_slots):
    _, dw = t_rows.shape
    n_chunks, _, w = dest3.shape
    per_worker = n_chunks // SC_WORKERS
    assert per_worker % 2 == 0
    mesh = plsc.VectorSubcoreMesh(core_axis_name="c", subcore_axis_name="s")
    dt = t_rows.dtype

    @functools.partial(
        pl.kernel, mesh=mesh,
        out_type=jax.ShapeDtypeStruct((n_slots, dw), dt),
        scratch_types=[pltpu.VMEM((TOP_K, w), I32), pltpu.VMEM((TOP_K, w), I32),
                       pltpu.VMEM((w, dw), dt), pltpu.VMEM((w, dw), dt),
                       pltpu.SemaphoreType.DMA, pltpu.SemaphoreType.DMA, pltpu.SemaphoreType.DMA],
    )
    def scatter_rows(t_hbm, dest_hbm, xs_hbm, idx_a, idx_b, rows_a, rows_b, sem_load, sem_a, sem_b):
        wid = _sc_worker_id()

        def scatter(idx_v, rows_v, sem):
            return [pltpu.async_copy(rows_v, xs_hbm.at[idx_v.at[k]], sem) for k in range(TOP_K)]

        @pl.loop(0, per_worker, step=2)
        def _(j):
            ca = wid * per_worker + j
            cb = ca + 1
            pltpu.sync_copy(dest_hbm.at[ca], idx_a)
            pltpu.sync_copy(t_hbm.at[pl.ds(ca * w, w)], rows_a)
            load_idx = pltpu.async_copy(dest_hbm.at[cb], idx_b, sem_load)
            load_rows = pltpu.async_copy(t_hbm.at[pl.ds(cb * w, w)], rows_b, sem_load)
            out_a = scatter(idx_a, rows_a, sem_a)
            load_idx.wait()
            load_rows.wait()
            out_b = scatter(idx_b, rows_b, sem_b)
            for cp in out_a + out_b:
                cp.wait()

    return scatter_rows(t_rows, dest3)


def _sc_combine(y_rows, dest3):
    _, dw = y_rows.shape
    n_chunks, _, w = dest3.shape
    per_worker = n_chunks // SC_WORKERS
    mesh = plsc.VectorSubcoreMesh(core_axis_name="c", subcore_axis_name="s")
    dt = y_rows.dtype

    @functools.partial(
        pl.kernel, mesh=mesh,
        out_type=jax.ShapeDtypeStruct((TOP_K, n_chunks * w, dw), dt),
        scratch_types=[pltpu.VMEM((TOP_K, w), I32), pltpu.VMEM((w, dw), dt), pltpu.VMEM((w, dw), dt),
                       pltpu.SemaphoreType.DMA, pltpu.SemaphoreType.DMA,
                       pltpu.SemaphoreType.DMA, pltpu.SemaphoreType.DMA],
    )
    def gather_rows(y_hbm, dest_hbm, out_hbm, idx_v, buf0, buf1, gsem0, gsem1, wsem0, wsem1):
        wid = _sc_worker_id()
        bufs, gsems, wsems = (buf0, buf1), (gsem0, gsem1), (wsem0, wsem1)

        @pl.loop(0, per_worker)
        def _(j):
            c = wid * per_worker + j
            pltpu.sync_copy(dest_hbm.at[c], idx_v)
            gathers = [None] * TOP_K
            writes = [None] * TOP_K
            gathers[0] = pltpu.async_copy(y_hbm.at[idx_v.at[0]], bufs[0], gsems[0])
            for k in range(TOP_K):
                b = k % 2
                if k + 1 < TOP_K:
                    if k >= 1:
                        writes[k - 1].wait()
                    gathers[k + 1] = pltpu.async_copy(y_hbm.at[idx_v.at[k + 1]], bufs[1 - b], gsems[1 - b])
                gathers[k].wait()
                writes[k] = pltpu.async_copy(bufs[b], out_hbm.at[k, pl.ds(c * w, w)], wsems[b])
            writes[TOP_K - 2].wait()
            writes[TOP_K - 1].wait()

    return gather_rows(y_rows, dest3)


def _final_kernel(base_ref, y_ref, gw_ref, g_ref, *rest, normalize):
    o_ref = rest[-1]
    h = base_ref[...]
    gw = gw_ref[...]
    for k in range(TOP_K):
        h = h + gw[:, k:k + 1] * _unpack_rows(y_ref[k])
    o_ref[...] = _rms(h, g_ref[...]) if normalize else h


def _final(base, yg, gw_rows, g_final, normalize, out_prev, part, n_parts):
    n, d = base.shape
    t = min(FIN_TILE, n)
    blk0 = part * (n // t)
    in_specs = [pl.BlockSpec((t, d), lambda i: (i, 0)), pl.BlockSpec((TOP_K, t, d // 2), lambda i: (0, i, 0)),
                pl.BlockSpec((t, TOP_K), lambda i: (i, 0)), _full((1, d))]
    args = [base, yg, gw_rows, g_final]
    aliases = {}
    if out_prev is not None:
        in_specs.append(pl.BlockSpec(memory_space=pl.ANY))
        args.append(out_prev)
        aliases = {len(args) - 1: 0}
    return pl.pallas_call(
        functools.partial(_final_kernel, normalize=normalize),
        grid=(n // t,),
        in_specs=in_specs,
        out_specs=pl.BlockSpec((t, d), lambda i: (i + blk0, 0)),
        out_shape=jax.ShapeDtypeStruct((n * n_parts, d), F32),
        input_output_aliases=aliases,
        compiler_params=_params(("parallel",)),
        name="final",
    )(*args)


def _ssm_matrices(a_re, a_im, log_dt, b_re, b_im, c_re, c_im):
    lr, li = a_re.astype(F32), a_im.astype(F32)
    dt = jnp.exp(log_dt.astype(F32))[:, None]
    mag = jnp.exp(lr * dt)
    ab_re, ab_im = mag * jnp.cos(li * dt), mag * jnp.sin(li * dt)
    den = lr * lr + li * li
    zr, zi = ab_re - 1.0, ab_im
    k_re = (zr * lr + zi * li) / den
    k_im = (zi * lr - zr * li) / den
    br, bi = b_re.astype(F32), b_im.astype(F32)
    bb_re = k_re[..., None] * br - k_im[..., None] * bi
    bb_im = k_re[..., None] * bi + k_im[..., None] * br
    ar, ai = ab_re[..., None], ab_im[..., None]
    ab_b_re, ab_b_im = ar * bb_re - ai * bb_im, ar * bb_im + ai * bb_re
    cr, ci = c_re.astype(F32), c_im.astype(F32)
    car, cai = ab_re[:, None, :], ab_im[:, None, :]
    ca_re, ca_im = cr * car - ci * cai, cr * cai + ci * car
    direct = jnp.einsum("gcp,gpd->gdc", cr, bb_re) - jnp.einsum("gcp,gpd->gdc", ci, bb_im)
    eye = jnp.eye(SSM_GROUPS, dtype=F32)
    n_slabs = SSM_WIDTH // SSM_CH

    def in_mat(bb):
        return jnp.einsum("gpc,gh->gchp", bb, eye).reshape(SSM_WIDTH, N_STATE)

    def out_mat(c):
        return jnp.einsum("gcp,gh->gphc", c, eye).reshape(N_STATE, SSM_WIDTH)

    bmat, cmat, dmat = [], [], []
    dense_d = jnp.einsum("gdc,gh->gdhc", direct, eye).reshape(SSM_WIDTH, SSM_WIDTH)
    for s in range(n_slabs):
        st = slice(s * SSM_SLAB, (s + 1) * SSM_SLAB)
        ch = slice(s * SSM_CH, (s + 1) * SSM_CH)
        top = jnp.concatenate([in_mat(bb_re)[ch, st], in_mat(bb_im)[ch, st]], axis=1)
        bot = jnp.concatenate([in_mat(ab_b_re)[ch, st], in_mat(ab_b_im)[ch, st]], axis=1)
        bmat.append(jnp.concatenate([top, bot], axis=0))
        re_rows = jnp.concatenate([out_mat(cr)[st, ch], out_mat(ca_re)[st, ch]], axis=1)
        im_rows = jnp.concatenate([-out_mat(ci)[st, ch], -out_mat(ca_im)[st, ch]], axis=1)
        cmat.append(jnp.concatenate([re_rows, im_rows], axis=0))
        dmat.append(dense_d[ch, ch])
    a2_re, a2_im = ab_re * ab_re - ab_im * ab_im, 2.0 * ab_re * ab_im
    tile = lambda a: jnp.broadcast_to(a.reshape(1, N_STATE), (SUBLANES, N_STATE))
    return (jnp.stack(bmat).astype(BF16), tile(a2_re), tile(a2_im), jnp.stack(cmat).astype(BF16),
            jnp.stack(dmat).astype(BF16))


def _plan_kernel(pstart_ref, eidx_ref, rank_ref, dest_ref, *, window):
    eidx = eidx_ref[...]
    dest = rank_ref[...]
    for e in range(N_EXPERTS):
        dest = dest + jnp.where(eidx == e, pstart_ref[e], 0)
    for c in range(dest.shape[1] // window):
        dest_ref[c] = dest[:, c * window:(c + 1) * window]


def _plan(pstart, eidx, rank, window):
    k, n = eidx.shape
    t = min(PLAN_TILE, n)
    cols = pl.BlockSpec((k, t), lambda i, ps: (0, i))
    return pl.pallas_call(
        functools.partial(_plan_kernel, window=window),
        grid_spec=pltpu.PrefetchScalarGridSpec(
            num_scalar_prefetch=1,
            grid=(n // t,),
            in_specs=[cols, cols],
            out_specs=pl.BlockSpec((t // window, k, window), lambda i, ps: (i, 0, 0))),
        out_shape=jax.ShapeDtypeStruct((n // window, k, window), I32),
        compiler_params=_params(("parallel",)),
        name="plan",
    )(pstart, eidx, rank)


def _block_schedule(counts, n_tok):
    padded = ((counts + MOE_TILE - 1) // MOE_TILE) * MOE_TILE
    pend = jnp.cumsum(padded)
    pstart = (pend - padded).astype(I32)
    n_slots = n_tok * TOP_K + N_EXPERTS * MOE_TILE
    nb = n_slots // MOE_TILE
    n_used = (pend[-1] // MOE_TILE).astype(I32)
    blk = jnp.arange(nb, dtype=I32)
    block_e = jnp.sum((pend[None, :] <= (jnp.minimum(blk, n_used - 1) * MOE_TILE)[:, None]).astype(I32), axis=1)
    block_e = jnp.minimum(block_e, N_EXPERTS - 1)
    eid = jnp.arange(N_EXPERTS, dtype=I32)
    active = padded > 0
    run_of = jnp.cumsum(active.astype(I32)) - 1
    later = jnp.where(jnp.logical_and(active[None, :], eid[None, :] > eid[:, None]), eid[None, :], N_EXPERTS)
    next_e = jnp.min(later, axis=1)
    next_e = jnp.where(next_e < N_EXPERTS, next_e, -1)
    onehot = block_e[:, None] == eid[None, :]
    lookup = lambda table: jnp.sum(jnp.where(onehot, table[None, :], 0), axis=1).astype(I32)
    run_first = jnp.logical_and(blk * MOE_TILE == lookup(pstart), blk < n_used).astype(I32)
    schedule = (block_e, run_first, lookup(run_of % 2), lookup(next_e), n_used.reshape(1))
    return pstart, schedule, n_slots


def kernel(x, mem, positions, g_mix, w_in, a_re, a_im, log_dt, b_re, b_im, c_re, c_im, d_skip, w_glu, g_ssm_out, lam_q1, lam_k1, lam_q2, lam_k2, g_sub, w_out, g_x, g_mem, wq_x, wk_x, wv_x, wo_x, g_ffn, w_router, router_bias, w_gate, w_up, w_down, ws_gate, ws_up, ws_down, g_final):
    b, l, d = x.shape
    n = b * l
    depth = w_in.shape[0]
    row = lambda a: a.reshape(1, -1).astype(F32)
    inv = ROPE_THETA ** (-jnp.arange(0, DA_QKDIM, 2, dtype=F32) / DA_QKDIM)
    inv_row = jnp.tile(inv, LANES // inv.shape[0]).reshape(1, LANES)
    pos3 = positions.reshape(b, l, 1)

    h = x
    for i in range(depth):
        lambda_init = 0.8 - 0.6 * math.exp(-0.3 * i)
        u, q, k, v = _inproj(h, pos3, row(g_mix[i]), w_in[i].astype(BF16), inv_row)

        ssm_mats = _ssm_matrices(a_re[i], a_im[i], log_dt[i], b_re[i], b_im[i], c_re[i], c_im[i])
        y_ssm = _ssm(u, ssm_mats, row(d_skip[i]), w_glu[i].astype(BF16), row(g_ssm_out[i]))

        lam = (jnp.exp(jnp.sum(lam_q1[i].astype(F32) * lam_k1[i].astype(F32)))
               - jnp.exp(jnp.sum(lam_q2[i].astype(F32) * lam_k2[i].astype(F32))) + lambda_init).reshape(1)
        y_att = _diff_attention(q, k, v, lam.astype(F32), row(g_sub[i]), lambda_init)

        kmem, vmem = _memkv(mem, row(g_mem[i]), wk_x[i].astype(BF16), wv_x[i].astype(BF16))
        wo = w_out[i].astype(BF16)
        h2 = _mix(h, y_ssm, y_att, wo[:SSM_WIDTH], wo[SSM_WIDTH:], row(g_x[i]), wq_x[i].astype(BF16),
                  kmem, vmem, wo_x[i].astype(BF16))

        wsgu = jnp.concatenate([ws_gate[i], ws_up[i]], axis=1).astype(BF16)
        wr = w_router[i].T.astype(F32)
        wr_hi = wr.astype(BF16)
        wr_split = jnp.concatenate([wr_hi, (wr - wr_hi.astype(F32)).astype(BF16)], axis=0)
        n_part = n // MOE_PARTS
        out = None
        for part in range(MOE_PARTS):
            t_pk, base, eidx, gw, rank, counts = _route(
                h2.reshape(n, d), row(g_ffn[i]), wr_split,
                router_bias[i].reshape(N_EXPERTS, 1).astype(F32), wsgu, ws_down[i].astype(BF16), part, MOE_PARTS)
            pstart, schedule, n_slots = _block_schedule(counts[:, 0], n_part)
            dest3 = _plan(pstart, eidx, rank, SC_WINDOW)
            xs = _sc_dispatch(t_pk, dest3, n_slots)
            ys = _experts(schedule, xs, w_gate[i], w_up[i], w_down[i])
            yg = _sc_combine(ys, dest3)
            out = _final(base, yg, gw.T, row(g_final), i == depth - 1, out, part, MOE_PARTS)
        h = out.reshape(b, l, d)
    return h
```

```python
import functools
import math

import jax
import jax.numpy as jnp
from jax import lax
from jax.experimental import pallas as pl
from jax.experimental.pallas import tpu as pltpu
from jax.experimental.pallas import tpu_sc as plsc

F32 = jnp.float32
BF16 = jnp.bfloat16
I32 = jnp.int32

D_MODEL = 1024
SSM_WIDTH = 512
ATTN_WIDTH = 512
SSM_GROUP = 16
SSM_GROUPS = 32
SSM_STATE = 64
N_STATE = SSM_GROUPS * SSM_STATE
DA_HEADS = 4
DA_VDIM = 128
DA_QKDIM = 64
ROPE_THETA = 10000.0
ROPE_FREQS = DA_QKDIM // 2
X_HEADS = 4
X_HEAD_DIM = 256
N_EXPERTS = 64
TOP_K = 8
N_EXPERT_GROUPS = 8
PER_GROUP = N_EXPERTS // N_EXPERT_GROUPS
TOPK_GROUPS = 4
D_EXPERT = 256
ROUTED_SCALE = 2.5
EPS = 1e-6

VMEM_LIMIT_V7X = 56 * 1024 * 1024
SUBLANES = 8
LANES = 128
HALF_WORD_BITS = 16
HIGH_HALF_MASK = -(1 << HALF_WORD_BITS)

IN_TILE = 1024
SSM_STEPS = 128
SSM_COLS = 1024
SSM_CH = 128
SSM_SLAB = 512
ATT_TILE = 2048
ATT_ROWS = 256
MIX_TILE = 1024
ROUTE_TILE = 1024
MOE_TILE = 512
X_SLOTS = 3
PLAN_TILE = 2048
MOE_PARTS = 2
FIN_TILE = 512

SC_CORES = 2
SC_WORKERS = 32
SC_WINDOW = 64


def _params(sem):
    return pltpu.CompilerParams(dimension_semantics=sem, vmem_limit_bytes=VMEM_LIMIT_V7X)


def _rms(x, g):
    return x * lax.rsqrt(jnp.mean(x * x, axis=-1, keepdims=True) + EPS) * g


def _full(shape):
    return pl.BlockSpec(shape, lambda *_: (0,) * len(shape))


def _pack_rows(a):
    w = a.shape[1] // 2
    bits = lambda v: lax.bitcast_convert_type(v.astype(BF16).astype(F32), I32)
    return (bits(a[:, w:]) & jnp.int32(HIGH_HALF_MASK)) | lax.shift_right_logical(bits(a[:, :w]), HALF_WORD_BITS)


def _unpack_rows(p):
    lo = lax.bitcast_convert_type(lax.shift_left(p, HALF_WORD_BITS), F32)
    hi = lax.bitcast_convert_type(p & jnp.int32(HIGH_HALF_MASK), F32)
    return jnp.concatenate([lo, hi], axis=1)


def _inproj_kernel(x_ref, pos_ref, g_ref, w_ref, inv_ref, u_ref, q_ref, k_ref, v_ref):
    x = x_ref[0]
    a = _rms(x, g_ref[...]).astype(BF16)
    z = jnp.dot(a, w_ref[...], preferred_element_type=F32)
    u_ref[0] = z[:, :SSM_WIDTH]
    n_grp = LANES // ROPE_FREQS
    tq = x.shape[0] // n_grp
    pos = pos_ref[0].astype(F32)
    lane_c = lax.broadcasted_iota(I32, (tq, LANES), 1)
    pos_c = jnp.zeros((tq, LANES), F32)
    for j in range(n_grp):
        pos_c = jnp.where(lane_c // ROPE_FREQS == j, pos[j * tq:(j + 1) * tq], pos_c)
    ang = pos_c * inv_ref[...]

    def spread(c):
        rows = []
        for j in range(n_grp):
            y = c if j == 0 else pltpu.roll(c, LANES - ROPE_FREQS * j, 1)
            w = ROPE_FREQS
            while w < LANES:
                y = jnp.where(lane_c < w, y, pltpu.roll(y, w, 1))
                w *= 2
            rows.append(y)
        return jnp.concatenate(rows, axis=0)

    cos = jnp.concatenate([spread(jnp.cos(ang))] * (ATTN_WIDTH // LANES), axis=1)
    sin = jnp.concatenate([spread(jnp.sin(ang))] * (ATTN_WIDTH // LANES), axis=1)
    lane = lax.broadcasted_iota(I32, cos.shape, 1)
    first = (lane & (DA_QKDIM - 1)) < DA_QKDIM // 2
    half = DA_QKDIM // 2

    def rope(t):
        rot = jnp.where(first, -pltpu.roll(t, ATTN_WIDTH - half, 1), pltpu.roll(t, half, 1))
        return t * cos + rot * sin

    q = z[:, SSM_WIDTH:SSM_WIDTH + ATTN_WIDTH]
    k = z[:, SSM_WIDTH + ATTN_WIDTH:SSM_WIDTH + 2 * ATTN_WIDTH]
    q_ref[0] = (rope(q) * (DA_QKDIM ** -0.5)).astype(BF16)
    k_ref[0] = rope(k).astype(BF16)
    v_ref[0] = z[:, SSM_WIDTH + 2 * ATTN_WIDTH:].astype(BF16)


def _inproj(x, pos3, g_mix, w_in, inv_row):
    b, l, d = x.shape
    t = min(IN_TILE, l)
    n_out = w_in.shape[1]
    row = lambda w: pl.BlockSpec((1, t, w), lambda i, j: (i, j, 0))
    return pl.pallas_call(
        _inproj_kernel,
        grid=(b, l // t),
        in_specs=[row(d), row(1), _full((1, d)), _full((d, n_out)), _full((1, LANES))],
        out_specs=[row(SSM_WIDTH), row(ATTN_WIDTH), row(ATTN_WIDTH), row(ATTN_WIDTH)],
        out_shape=[jax.ShapeDtypeStruct((b, l, SSM_WIDTH), F32)]
        + [jax.ShapeDtypeStruct((b, l, ATTN_WIDTH), BF16)] * 3,
        compiler_params=_params(("parallel", "parallel")),
        name="inproj",
    )(x, pos3, g_mix, w_in, inv_row)


def _ssm_kernel(u_ref, bm_ref, are_ref, aim_ref, cm_ref, dm_ref, dskip_ref, wglu_ref, g_ref, o_ref,
                c_ref, st_ref, up_ref, *, steps):
    @pl.when(pl.program_id(0) == 0)
    def _():
        st_ref[...] = jnp.zeros_like(st_ref)
        up_ref[...] = jnp.zeros_like(up_ref)

    pairs = steps // 2
    rows = pairs * SUBLANES
    u = pltpu.einshape("bts->tbs", u_ref[...]).reshape(pairs, 2, SUBLANES, SSM_WIDTH)
    u_even = u[:, 0].reshape(rows, SSM_WIDTH)
    u_odd = u[:, 1].reshape(rows, SSM_WIDTH)
    u_before = jnp.concatenate([up_ref[...], u_odd[:rows - SUBLANES]], axis=0)
    up_ref[...] = u_odd[rows - SUBLANES:]
    ue, uo, ub = u_even.astype(BF16), u_odd.astype(BF16), u_before.astype(BF16)

    n_slabs = SSM_WIDTH // SSM_CH
    for s in range(n_slabs):
        ch = slice(s * SSM_CH, (s + 1) * SSM_CH)
        lhs = jnp.concatenate([ue[:, ch], ub[:, ch]], axis=1)
        for part in (0, 1):
            cols = slice(part * N_STATE + s * SSM_SLAB, part * N_STATE + (s + 1) * SSM_SLAB)
            c_ref[:, cols] = jnp.dot(lhs, bm_ref[s, :, part * SSM_SLAB:(part + 1) * SSM_SLAB],
                                     preferred_element_type=F32)

    for c0 in range(0, N_STATE, SSM_COLS):
        re = slice(c0, c0 + SSM_COLS)
        im = slice(N_STATE + c0, N_STATE + c0 + SSM_COLS)
        ar = are_ref[:, re]
        ai = aim_ref[:, re]
        sr, si = st_ref[:, re], st_ref[:, im]
        for k in range(pairs):
            blk = slice(k * SUBLANES, (k + 1) * SUBLANES)
            sr, si = ar * sr - ai * si + c_ref[blk, re], ar * si + ai * sr + c_ref[blk, im]
            c_ref[blk, re] = sr
            c_ref[blk, im] = si
        st_ref[:, re] = sr
        st_ref[:, im] = si

    y_even, y_odd = [], []
    for s in range(n_slabs):
        ch = slice(s * SSM_CH, (s + 1) * SSM_CH)
        acc = jnp.dot(uo[:, ch], dm_ref[s], preferred_element_type=F32)
        both = None
        for part in (0, 1):
            cols = slice(part * N_STATE + s * SSM_SLAB, part * N_STATE + (s + 1) * SSM_SLAB)
            d = jnp.dot(c_ref[:, cols].astype(BF16), cm_ref[s, part * SSM_SLAB:(part + 1) * SSM_SLAB],
                        preferred_element_type=F32)
            both = d if both is None else both + d
        y_even.append(both[:, :SSM_CH])
        y_odd.append(both[:, SSM_CH:] + acc)
    y = jnp.concatenate([jnp.concatenate(y_even, axis=1), jnp.concatenate(y_odd, axis=1)], axis=0)
    y = y + dskip_ref[...] * jnp.concatenate([u_even, u_odd], axis=0)
    y = jax.nn.gelu(y)
    y = y * jax.nn.sigmoid(jnp.dot(y.astype(BF16), wglu_ref[...], preferred_element_type=F32))
    out = _rms(y, g_ref[...])
    out = jnp.stack([out[:rows].reshape(pairs, SUBLANES, SSM_WIDTH),
                     out[rows:].reshape(pairs, SUBLANES, SSM_WIDTH)], axis=1).reshape(steps, SUBLANES, SSM_WIDTH)
    o_ref[...] = pltpu.einshape("tbs->bts", out).astype(BF16)


def _ssm(u, mats, d_skip, w_glu, g_out):
    bmat, a2_re8, a2_im8, cmat, dmat = mats
    batch, l, _ = u.shape
    assert batch == SUBLANES, "one time step of all sequences must fill the sublanes"
    steps = min(SSM_STEPS, l)
    assert steps % 2 == 0
    rows = steps // 2 * batch
    n_slabs = SSM_WIDTH // SSM_CH
    seq = pl.BlockSpec((batch, steps, SSM_WIDTH), lambda i: (0, i, 0))
    return pl.pallas_call(
        functools.partial(_ssm_kernel, steps=steps),
        grid=(l // steps,),
        in_specs=[seq,
                  _full((n_slabs, 2 * SSM_CH, 2 * SSM_SLAB)), _full((SUBLANES, N_STATE)), _full((SUBLANES, N_STATE)),
                  _full((n_slabs, 2 * SSM_SLAB, 2 * SSM_CH)), _full((n_slabs, SSM_CH, SSM_CH)),
                  _full((1, SSM_WIDTH)), _full((SSM_WIDTH, SSM_WIDTH)), _full((1, SSM_WIDTH))],
        out_specs=seq,
        out_shape=jax.ShapeDtypeStruct((batch, l, SSM_WIDTH), BF16),
        scratch_shapes=[pltpu.VMEM((rows, 2 * N_STATE), F32), pltpu.VMEM((SUBLANES, 2 * N_STATE), F32),
                        pltpu.VMEM((SUBLANES, SSM_WIDTH), F32)],
        compiler_params=_params(("arbitrary",)),
        name="ssm",
    )(u, bmat, a2_re8, a2_im8, cmat, dmat, d_skip, w_glu, g_out)


def _attn_kernel(qi_ref, ki_ref, lam_ref, q_ref, k_ref, v_ref, g_ref, o_ref, qs_ref, m_ref, acc_ref, *,
                 tile, out_scale):
    qi = qi_ref[pl.program_id(2)]
    ki = ki_ref[pl.program_id(2)]

    @pl.when(ki == 0)
    def _():
        q = q_ref[0]
        lane = lax.broadcasted_iota(I32, q.shape, 1)
        zero = jnp.zeros_like(q)
        qs_ref[:tile] = jnp.where(lane < DA_QKDIM, q, zero)
        qs_ref[tile:] = jnp.where(lane >= DA_QKDIM, q, zero)
        m_ref[...] = jnp.full_like(m_ref, -jnp.inf)
        acc_ref[...] = jnp.zeros_like(acc_ref)

    def update(masked):
        v_ext = jnp.concatenate([v_ref[0], jnp.ones((tile, DA_VDIM), BF16)], axis=1)
        rb = min(ATT_ROWS, tile)
        n_rb = 2 * tile // rb

        def n_keys(r):
            return (r * rb) % tile + rb if masked else tile

        def scores(r):
            return lax.dot_general(qs_ref[r * rb:(r + 1) * rb], k_ref[0, :n_keys(r)], (((1,), (1,)), ((), ())),
                                   preferred_element_type=F32)

        s_next = scores(0)
        for r in range(n_rb):
            s = s_next
            if r + 1 < n_rb:
                s_next = scores(r + 1)
            rows = slice(r * rb, (r + 1) * rb)
            nk = n_keys(r)
            if masked:
                tri = lax.broadcasted_iota(I32, (rb, rb), 1) <= lax.broadcasted_iota(I32, (rb, rb), 0)
                edge = jnp.where(tri, s[:, nk - rb:], jnp.finfo(F32).min)
                s = edge if nk == rb else jnp.concatenate([s[:, :nk - rb], edge], axis=1)
            m_old = m_ref[rows]
            m_new = jnp.maximum(m_old, jnp.max(s, axis=1, keepdims=True))
            p = jnp.exp(s - jnp.concatenate([m_new] * (nk // DA_VDIM), axis=1))
            alpha = jnp.exp(m_old - m_new)
            acc_ref[rows] = (jnp.concatenate([alpha, alpha], axis=1) * acc_ref[rows]
                             + jnp.dot(p.astype(BF16), v_ext[:nk], preferred_element_type=F32))
            m_ref[rows] = m_new

    @pl.when(ki < qi)
    def _():
        update(False)

    @pl.when(ki == qi)
    def _():
        update(True)
        o = acc_ref[:, :DA_VDIM] / acc_ref[:, DA_VDIM:]
        o = o[:tile] - lam_ref[0] * o[tile:]
        o_ref[0] = (_rms(o, g_ref[...]) * out_scale).astype(BF16)


def _diff_attention(q, k, v, lam, g_sub, lambda_init):
    b, l, _ = q.shape
    t = min(ATT_TILE, l)
    n = l // t
    pairs = [(qi, ki) for qi in range(n) for ki in range(qi + 1)]
    qi_tab = jnp.asarray([p[0] for p in pairs], I32)
    ki_tab = jnp.asarray([p[1] for p in pairs], I32)
    qspec = pl.BlockSpec((1, t, DA_VDIM), lambda bi, h, s, qt, kt: (bi, qt[s], h))
    kspec = pl.BlockSpec((1, t, DA_VDIM), lambda bi, h, s, qt, kt: (bi, kt[s], h))
    return pl.pallas_call(
        functools.partial(_attn_kernel, tile=t, out_scale=1.0 - lambda_init),
        grid_spec=pltpu.PrefetchScalarGridSpec(
            num_scalar_prefetch=2,
            grid=(b, DA_HEADS, len(pairs)),
            in_specs=[pl.BlockSpec(memory_space=pltpu.SMEM), qspec, kspec, kspec,
                      pl.BlockSpec((1, DA_VDIM), lambda bi, h, s, qt, kt: (0, 0))],
            out_specs=qspec,
            scratch_shapes=[pltpu.VMEM((2 * t, DA_VDIM), BF16), pltpu.VMEM((2 * t, DA_VDIM), F32),
                            pltpu.VMEM((2 * t, 2 * DA_VDIM), F32)]),
        out_shape=jax.ShapeDtypeStruct((b, l, ATTN_WIDTH), BF16),
        compiler_params=_params(("parallel", "parallel", "arbitrary")),
        name="diffattn",
    )(qi_tab, ki_tab, lam, q, k, v, g_sub)


def _memkv_kernel(m_ref, g_ref, wk_ref, wv_ref, k_ref, v_ref):
    a = _rms(m_ref[0], g_ref[...]).astype(BF16)
    k_ref[0] = jnp.dot(a, wk_ref[...], preferred_element_type=F32).astype(BF16)
    v_ref[0] = jnp.dot(a, wv_ref[...], preferred_element_type=F32).astype(BF16)


def _memkv(mem, g_mem, wk, wv):
    b, m, d = mem.shape
    blk = pl.BlockSpec((1, m, d), lambda i: (i, 0, 0))
    return pl.pallas_call(
        _memkv_kernel,
        grid=(b,),
        in_specs=[blk, _full((1, d)), _full((d, d)), _full((d, d))],
        out_specs=[blk, blk],
        out_shape=[jax.ShapeDtypeStruct((b, m, d), BF16)] * 2,
        compiler_params=_params(("parallel",)),
        name="memkv",
    )(mem, g_mem, wk, wv)


def _mix_kernel(x_ref, ys_ref, ya_ref, wo1_ref, wo2_ref, gx_ref, wq_ref, km_ref, vm_ref, wo_ref, h_ref):
    h = (x_ref[0]
         + jnp.dot(ys_ref[0], wo1_ref[...], preferred_element_type=F32)
         + jnp.dot(ya_ref[0], wo2_ref[...], preferred_element_type=F32))
    hq = _rms(h, gx_ref[...]).astype(BF16)
    q = jnp.dot(hq, wq_ref[...], preferred_element_type=F32).astype(BF16)
    outs = []
    for hd in range(X_HEADS):
        cols = slice(hd * X_HEAD_DIM, (hd + 1) * X_HEAD_DIM)
        s = lax.dot_general(q[:, cols], km_ref[0, :, cols], (((1,), (1,)), ((), ())),
                            preferred_element_type=F32) * (X_HEAD_DIM ** -0.5)
        s = s - jnp.max(s, axis=1, keepdims=True)
        p = jnp.exp(s)
        p = p / jnp.sum(p, axis=1, keepdims=True)
        outs.append(jnp.dot(p.astype(BF16), vm_ref[0, :, cols], preferred_element_type=F32).astype(BF16))
    o = jnp.concatenate(outs, axis=1)
    h_ref[0] = h + jnp.dot(o, wo_ref[...], preferred_element_type=F32)


def _mix(x, y_ssm, y_att, wo1, wo2, g_x, wq, kmem, vmem, wo):
    b, l, d = x.shape
    t = min(MIX_TILE, l)
    m = kmem.shape[1]
    row = lambda w: pl.BlockSpec((1, t, w), lambda i, j: (i, j, 0))
    mem = pl.BlockSpec((1, m, d), lambda i, j: (i, 0, 0))
    return pl.pallas_call(
        _mix_kernel,
        grid=(b, l // t),
        in_specs=[row(d), row(SSM_WIDTH), row(ATTN_WIDTH), _full((SSM_WIDTH, d)), _full((ATTN_WIDTH, d)),
                  _full((1, d)), _full((d, d)), mem, mem, _full((d, d))],
        out_specs=row(d),
        out_shape=jax.ShapeDtypeStruct((b, l, d), F32),
        compiler_params=_params(("parallel", "parallel")),
        name="mix",
    )(x, y_ssm, y_att, wo1, wo2, g_x, wq, kmem, vmem, wo)


def _first_index(hit, idx, sentinel):
    return jnp.min(jnp.where(hit, idx, sentinel), axis=0, keepdims=True)


def _route_kernel(h_ref, g_ref, wr_ref, bias_ref, wsgu_ref, wsd_ref, tri_ref,
                  t_ref, base_ref, eidx_ref, gw_ref, rank_ref, cnt_ref, carry_ref):
    @pl.when(pl.program_id(0) == 0)
    def _():
        carry_ref[...] = jnp.zeros_like(carry_ref)

    h = h_ref[...]
    t = _rms(h, g_ref[...])
    tb = t.astype(BF16)
    t_ref[...] = _pack_rows(t)
    gu = jnp.dot(tb, wsgu_ref[...], preferred_element_type=F32)
    hid = jax.nn.silu(gu[:, :D_EXPERT]) * gu[:, D_EXPERT:]
    base_ref[...] = h + jnp.dot(hid.astype(BF16), wsd_ref[...], preferred_element_type=F32)

    nt = (((1,), (1,)), ((), ()))
    t_lo = (t - tb.astype(F32)).astype(BF16)
    both = lax.dot_general(wr_ref[...], tb, nt, preferred_element_type=F32)
    logits = (both[:N_EXPERTS] + both[N_EXPERTS:]
              + lax.dot_general(wr_ref[:N_EXPERTS], t_lo, nt, preferred_element_type=F32))
    scores = jax.nn.sigmoid(logits)
    biased = scores + bias_ref[...]
    n_tok = scores.shape[1]
    neg = -jnp.inf
    sub = lax.broadcasted_iota(I32, (PER_GROUP, n_tok), 0)

    gs = []
    for g in range(N_EXPERT_GROUPS):
        blk = biased[g * PER_GROUP:(g + 1) * PER_GROUP]
        m1 = jnp.max(blk, axis=0, keepdims=True)
        i1 = _first_index(blk == m1, sub, PER_GROUP)
        m2 = jnp.max(jnp.where(sub == i1, neg, blk), axis=0, keepdims=True)
        gs.append(m1 + m2)
    gs = jnp.concatenate(gs, axis=0)

    gsel = jnp.zeros(gs.shape, jnp.bool_)
    for _ in range(TOPK_GROUPS):
        m = jnp.max(gs, axis=0, keepdims=True)
        hit = sub == _first_index(gs == m, sub, N_EXPERT_GROUPS)
        gsel = jnp.logical_or(gsel, hit)
        gs = jnp.where(hit, neg, gs)

    masked = jnp.concatenate(
        [jnp.where(gsel[g:g + 1], biased[g * PER_GROUP:(g + 1) * PER_GROUP], neg)
         for g in range(N_EXPERT_GROUPS)], axis=0)
    eid = lax.broadcasted_iota(I32, masked.shape, 0)
    sel = jnp.zeros(masked.shape, jnp.bool_)
    idxs, gws = [], []
    for _ in range(TOP_K):
        m = jnp.max(masked, axis=0, keepdims=True)
        i = _first_index(masked == m, eid, N_EXPERTS)
        hit = eid == i
        idxs.append(i)
        gws.append(jnp.sum(jnp.where(hit, scores, 0.0), axis=0, keepdims=True))
        sel = jnp.logical_or(sel, hit)
        masked = jnp.where(hit, neg, masked)
    eidx = jnp.concatenate(idxs, axis=0)
    gw = jnp.concatenate(gws, axis=0)
    gw = gw / jnp.sum(gw, axis=0, keepdims=True) * ROUTED_SCALE
    eidx_ref[...] = eidx
    gw_ref[...] = gw

    before = jnp.dot(sel.astype(BF16), tri_ref[...], preferred_element_type=F32) + carry_ref[...]
    rank_ref[...] = jnp.concatenate(
        [jnp.sum(jnp.where(eid == idxs[k], before, 0.0), axis=0, keepdims=True) for k in range(TOP_K)],
        axis=0).astype(I32)
    carry = carry_ref[...] + jnp.sum(sel.astype(F32), axis=1, keepdims=True)
    carry_ref[...] = carry
    cnt_ref[...] = carry.astype(I32)


def _route(h2, g_ffn, wr_t, bias_col, wsgu, wsd, part, n_parts):
    n_all, d = h2.shape
    n = n_all // n_parts
    t = min(ROUTE_TILE, n)
    blk0 = part * (n // t)
    tri = (lax.broadcasted_iota(I32, (t, t), 0) < lax.broadcasted_iota(I32, (t, t), 1)).astype(BF16)
    row = lambda w: pl.BlockSpec((t, w), lambda i: (i, 0))
    col = pl.BlockSpec((TOP_K, t), lambda i: (0, i))
    return pl.pallas_call(
        _route_kernel,
        grid=(n // t,),
        in_specs=[pl.BlockSpec((t, d), lambda i: (i + blk0, 0)), _full((1, d)), _full((2 * N_EXPERTS, d)),
                  _full((N_EXPERTS, 1)),
                  _full((d, 2 * D_EXPERT)), _full((D_EXPERT, d)), _full((t, t))],
        out_specs=[row(d // 2), row(d), col, col, col, _full((N_EXPERTS, 1))],
        out_shape=[jax.ShapeDtypeStruct((n, d // 2), I32), jax.ShapeDtypeStruct((n, d), F32),
                   jax.ShapeDtypeStruct((TOP_K, n), I32), jax.ShapeDtypeStruct((TOP_K, n), F32),
                   jax.ShapeDtypeStruct((TOP_K, n), I32), jax.ShapeDtypeStruct((N_EXPERTS, 1), I32)],
        scratch_shapes=[pltpu.VMEM((N_EXPERTS, 1), F32)],
        compiler_params=_params(("arbitrary",)),
        name="route",
    )(h2, g_ffn, wr_t, bias_col, wsgu, wsd, tri)


def _expert_kernel(be_ref, first_ref, slot_ref, next_ref, nu_ref, x_hbm, wg_hbm, wu_hbm, wd_hbm, y_ref,
                   wg_s, wu_s, wd_s, x_buf, x_sem, wg_buf, wu_buf, wd_buf, w_sem):
    i = pl.program_id(0)
    n_used = nu_ref[0]

    def fetch(step):
        slot = step % X_SLOTS
        rows = pl.ds(pl.multiple_of(step * MOE_TILE, MOE_TILE), MOE_TILE)
        return pltpu.make_async_copy(x_hbm.at[rows], x_buf.at[slot], x_sem.at[slot])

    def weight_fetch(expert, slot):
        pairs = ((wg_hbm, wg_buf), (wu_hbm, wu_buf), (wd_hbm, wd_buf))
        return [pltpu.make_async_copy(src.at[expert], buf.at[slot], w_sem.at[slot, j])
                for j, (src, buf) in enumerate(pairs)]

    @pl.when(i == 0)
    def _():
        for j in range(X_SLOTS - 1):
            @pl.when(j < n_used)
            def _(j=j):
                fetch(j).start()
        for cp in weight_fetch(be_ref[0], 0):
            cp.start()

    @pl.when(i + (X_SLOTS - 1) < n_used)
    def _():
        fetch(i + (X_SLOTS - 1)).start()

    @pl.when(first_ref[i] == 1)
    def _():
        slot = slot_ref[i]

        @pl.when(next_ref[i] >= 0)
        def _():
            for cp in weight_fetch(next_ref[i], 1 - slot):
                cp.start()

        for cp in weight_fetch(be_ref[i], slot):
            cp.wait()
        wg_s[...] = wg_buf[slot].astype(BF16)
        wu_s[...] = wu_buf[slot].astype(BF16)
        wd_s[...] = wd_buf[slot].astype(BF16)

    @pl.when(i < n_used)
    def _():
        fetch(i).wait()
        x = _unpack_rows(x_buf[i % X_SLOTS]).astype(BF16)
        gate = jnp.dot(x, wg_s[...], preferred_element_type=F32)
        up = jnp.dot(x, wu_s[...], preferred_element_type=F32)
        hid = (jax.nn.silu(gate) * up).astype(BF16)
        y_ref[...] = _pack_rows(jnp.dot(hid, wd_s[...], preferred_element_type=F32))


def _experts(schedule, xs, w_gate, w_up, w_down):
    block_e, run_first, run_slot, run_next, n_used = schedule
    n_slots, dw = xs.shape
    _, d, de = w_gate.shape
    nb = n_slots // MOE_TILE
    hbm = pl.BlockSpec(memory_space=pl.ANY)
    return pl.pallas_call(
        _expert_kernel,
        grid_spec=pltpu.PrefetchScalarGridSpec(
            num_scalar_prefetch=5,
            grid=(nb,),
            in_specs=[hbm, hbm, hbm, hbm],
            out_specs=pl.BlockSpec((MOE_TILE, dw), lambda i, be, fi, sl, nx, nu: (jnp.minimum(i, nu[0] - 1), 0)),
            scratch_shapes=[pltpu.VMEM((d, de), BF16), pltpu.VMEM((d, de), BF16), pltpu.VMEM((de, d), BF16),
                            pltpu.VMEM((X_SLOTS, MOE_TILE, dw), I32), pltpu.SemaphoreType.DMA((X_SLOTS,)),
                            pltpu.VMEM((2, d, de), F32), pltpu.VMEM((2, d, de), F32), pltpu.VMEM((2, de, d), F32),
                            pltpu.SemaphoreType.DMA((2, 3))]),
        out_shape=jax.ShapeDtypeStruct((n_slots, dw), I32),
        compiler_params=_params(("arbitrary",)),
        name="experts",
    )(block_e, run_first, run_slot, run_next, n_used, xs, w_gate, w_up, w_down)


def _sc_worker_id():
    return lax.axis_index("s") * SC_CORES + lax.axis_index("c")


def _sc_dispatch(t_rows, dest3, n_slots):
    _, dw = t_rows.shape
    n_chunks, _, w = dest3.shape
    per_worker = n_chunks // SC_WORKERS
    assert per_worker % 2 == 0
    mesh = plsc.VectorSubcoreMesh(core_axis_name="c", subcore_axis_name="s")
    dt = t_rows.dtype

    @functools.partial(
        pl.kernel, mesh=mesh,
        out_type=jax.ShapeDtypeStruct((n_slots, dw), dt),
        scratch_types=[pltpu.VMEM((TOP_K, w), I32), pltpu.VMEM((TOP_K, w), I32),
                       pltpu.VMEM((w, dw), dt), pltpu.VMEM((w, dw), dt),
                       pltpu.SemaphoreType.DMA, pltpu.SemaphoreType.DMA, pltpu.SemaphoreType.DMA],
    )
    def scatter_rows(t_hbm, dest_hbm, xs_hbm, idx_a, idx_b, rows_a, rows_b, sem_load, sem_a, sem_b):
        wid = _sc_worker_id()

        def scatter(idx_v, rows_v, sem):
            return [pltpu.async_copy(rows_v, xs_hbm.at[idx_v.at[k]], sem) for k in range(TOP_K)]

        @pl.loop(0, per_worker, step=2)
        def _(j):
            ca = wid * per_worker + j
            cb = ca + 1
            pltpu.sync_copy(dest_hbm.at[ca], idx_a)
            pltpu.sync_copy(t_hbm.at[pl.ds(ca * w, w)], rows_a)
            load_idx = pltpu.async_copy(dest_hbm.at[cb], idx_b, sem_load)
            load_rows = pltpu.async_copy(t_hbm.at[pl.ds(cb * w, w)], rows_b, sem_load)
            out_a = scatter(idx_a, rows_a, sem_a)
            load_idx.wait()
            load_rows.wait()
            out_b = scatter(idx_b, rows_b, sem_b)
            for cp in out_a + out_b:
                cp.wait()

    return scatter_rows(t_rows, dest3)


def _sc_combine(y_rows, dest3):
    _, dw = y_rows.shape
    n_chunks, _, w = dest3.shape
    per_worker = n_chunks // SC_WORKERS
    mesh = plsc.VectorSubcoreMesh(core_axis_name="c", subcore_axis_name="s")
    dt = y_rows.dtype

    @functools.partial(
        pl.kernel, mesh=mesh,
        out_type=jax.ShapeDtypeStruct((TOP_K, n_chunks * w, dw), dt),
        scratch_types=[pltpu.VMEM((TOP_K, w), I32), pltpu.VMEM((w, dw), dt), pltpu.VMEM((w, dw), dt),
                       pltpu.SemaphoreType.DMA, pltpu.SemaphoreType.DMA,
                       pltpu.SemaphoreType.DMA, pltpu.SemaphoreType.DMA],
    )
    def gather_rows(y_hbm, dest_hbm, out_hbm, idx_v, buf0, buf1, gsem0, gsem1, wsem0, wsem1):
        wid = _sc_worker_id()
        bufs, gsems, wsems = (buf0, buf1), (gsem0, gsem1), (wsem0, wsem1)

        @pl.loop(0, per_worker)
        def _(j):
            c = wid * per_worker + j
            pltpu.sync_copy(dest_hbm.at[c], idx_v)
            gathers = [None] * TOP_K
            writes = [None] * TOP_K
            gathers[0] = pltpu.async_copy(y_hbm.at[idx_v.at[0]], bufs[0], gsems[0])
            for k in range(TOP_K):
                b = k % 2
                if k + 1 < TOP_K:
                    if k >= 1:
                        writes[k - 1].wait()
                    gathers[k + 1] = pltpu.async_copy(y_hbm.at[idx_v.at[k + 1]], bufs[1 - b], gsems[1 - b])
                gathers[k].wait()
                writes[k] = pltpu.async_copy(bufs[b], out_hbm.at[k, pl.ds(c * w, w)], wsems[b])
            writes[TOP_K - 2].wait()
            writes[TOP_K - 1].wait()

    return gather_rows(y_rows, dest3)


def _final_kernel(base_ref, y_ref, gw_ref, g_ref, *rest, normalize):
    o_ref = rest[-1]
    h = base_ref[...]
    gw = gw_ref[...]
    for k in range(TOP_K):
        h = h + gw[:, k:k + 1] * _unpack_rows(y_ref[k])
    o_ref[...] = _rms(h, g_ref[...]) if normalize else h


def _final(base, yg, gw_rows, g_final, normalize, out_prev, part, n_parts):
    n, d = base.shape
    t = min(FIN_TILE, n)
    blk0 = part * (n // t)
    in_specs = [pl.BlockSpec((t, d), lambda i: (i, 0)), pl.BlockSpec((TOP_K, t, d // 2), lambda i: (0, i, 0)),
                pl.BlockSpec((t, TOP_K), lambda i: (i, 0)), _full((1, d))]
    args = [base, yg, gw_rows, g_final]
    aliases = {}
    if out_prev is not None:
        in_specs.append(pl.BlockSpec(memory_space=pl.ANY))
        args.append(out_prev)
        aliases = {len(args) - 1: 0}
    return pl.pallas_call(
        functools.partial(_final_kernel, normalize=normalize),
        grid=(n // t,),
        in_specs=in_specs,
        out_specs=pl.BlockSpec((t, d), lambda i: (i + blk0, 0)),
        out_shape=jax.ShapeDtypeStruct((n * n_parts, d), F32),
        input_output_aliases=aliases,
        compiler_params=_params(("parallel",)),
        name="final",
    )(*args)


def _ssm_matrices(a_re, a_im, log_dt, b_re, b_im, c_re, c_im):
    lr, li = a_re.astype(F32), a_im.astype(F32)
    dt = jnp.exp(log_dt.astype(F32))[:, None]
    mag = jnp.exp(lr * dt)
    ab_re, ab_im = mag * jnp.cos(li * dt), mag * jnp.sin(li * dt)
    den = lr * lr + li * li
    zr, zi = ab_re - 1.0, ab_im
    k_re = (zr * lr + zi * li) / den
    k_im = (zi * lr - zr * li) / den
    br, bi = b_re.astype(F32), b_im.astype(F32)
    bb_re = k_re[..., None] * br - k_im[..., None] * bi
    bb_im = k_re[..., None] * bi + k_im[..., None] * br
    ar, ai = ab_re[..., None], ab_im[..., None]
    ab_b_re, ab_b_im = ar * bb_re - ai * bb_im, ar * bb_im + ai * bb_re
    cr, ci = c_re.astype(F32), c_im.astype(F32)
    car, cai = ab_re[:, None, :], ab_im[:, None, :]
    ca_re, ca_im = cr * car - ci * cai, cr * cai + ci * car
    direct = jnp.einsum("gcp,gpd->gdc", cr, bb_re) - jnp.einsum("gcp,gpd->gdc", ci, bb_im)
    n_slabs = SSM_WIDTH // SSM_CH
    per_slab = SSM_GROUPS // n_slabs
    eye = jnp.eye(per_slab, dtype=F32)

    def expand_blocks(bb):
        blocks = jnp.einsum("sgpc,gh->sgchp", bb.reshape(n_slabs, per_slab, SSM_STATE, SSM_GROUP), eye)
        return blocks.reshape(n_slabs, SSM_CH, SSM_SLAB)

    def contract_blocks(c):
        blocks = jnp.einsum("sgcp,gh->sgphc", c.reshape(n_slabs, per_slab, SSM_GROUP, SSM_STATE), eye)
        return blocks.reshape(n_slabs, SSM_SLAB, SSM_CH)

    bmat = jnp.concatenate([jnp.concatenate([expand_blocks(bb_re), expand_blocks(bb_im)], axis=2),
                            jnp.concatenate([expand_blocks(ab_b_re), expand_blocks(ab_b_im)], axis=2)], axis=1)
    cmat = jnp.concatenate([jnp.concatenate([contract_blocks(cr), contract_blocks(ca_re)], axis=2),
                            jnp.concatenate([-contract_blocks(ci), -contract_blocks(ca_im)], axis=2)], axis=1)
    dmat = jnp.einsum("sgdc,gh->sgdhc", direct.reshape(n_slabs, per_slab, SSM_GROUP, SSM_GROUP), eye)
    dmat = dmat.reshape(n_slabs, SSM_CH, SSM_CH)
    a2_re, a2_im = ab_re * ab_re - ab_im * ab_im, 2.0 * ab_re * ab_im
    tile = lambda a: jnp.broadcast_to(a.reshape(1, N_STATE), (SUBLANES, N_STATE))
    return bmat.astype(BF16), tile(a2_re), tile(a2_im), cmat.astype(BF16), dmat.astype(BF16)


def _plan_kernel(pstart_ref, eidx_ref, rank_ref, dest_ref, *, window):
    eidx = eidx_ref[...]
    dest = rank_ref[...]
    for e in range(N_EXPERTS):
        dest = dest + jnp.where(eidx == e, pstart_ref[e], 0)
    for c in range(dest.shape[1] // window):
        dest_ref[c] = dest[:, c * window:(c + 1) * window]


def _plan(pstart, eidx, rank, window):
    k, n = eidx.shape
    t = min(PLAN_TILE, n)
    cols = pl.BlockSpec((k, t), lambda i, ps: (0, i))
    return pl.pallas_call(
        functools.partial(_plan_kernel, window=window),
        grid_spec=pltpu.PrefetchScalarGridSpec(
            num_scalar_prefetch=1,
            grid=(n // t,),
            in_specs=[cols, cols],
            out_specs=pl.BlockSpec((t // window, k, window), lambda i, ps: (i, 0, 0))),
        out_shape=jax.ShapeDtypeStruct((n // window, k, window), I32),
        compiler_params=_params(("parallel",)),
        name="plan",
    )(pstart, eidx, rank)


def _block_schedule(counts, n_tok):
    padded = ((counts + MOE_TILE - 1) // MOE_TILE) * MOE_TILE
    pend = jnp.cumsum(padded)
    pstart = (pend - padded).astype(I32)
    n_slots = n_tok * TOP_K + N_EXPERTS * MOE_TILE
    nb = n_slots // MOE_TILE
    n_used = (pend[-1] // MOE_TILE).astype(I32)
    blk = jnp.arange(nb, dtype=I32)
    block_e = jnp.sum((pend[None, :] <= (jnp.minimum(blk, n_used - 1) * MOE_TILE)[:, None]).astype(I32), axis=1)
    block_e = jnp.minimum(block_e, N_EXPERTS - 1)
    eid = jnp.arange(N_EXPERTS, dtype=I32)
    active = padded > 0
    run_of = jnp.cumsum(active.astype(I32)) - 1
    later = jnp.where(jnp.logical_and(active[None, :], eid[None, :] > eid[:, None]), eid[None, :], N_EXPERTS)
    next_e = jnp.min(later, axis=1)
    next_e = jnp.where(next_e < N_EXPERTS, next_e, -1)
    onehot = block_e[:, None] == eid[None, :]
    lookup = lambda table: jnp.sum(jnp.where(onehot, table[None, :], 0), axis=1).astype(I32)
    run_first = jnp.logical_and(blk * MOE_TILE == lookup(pstart), blk < n_used).astype(I32)
    schedule = (block_e, run_first, lookup(run_of % 2), lookup(next_e), n_used.reshape(1))
    return pstart, schedule, n_slots


def kernel(x, mem, positions, g_mix, w_in, a_re, a_im, log_dt, b_re, b_im, c_re, c_im, d_skip, w_glu, g_ssm_out, lam_q1, lam_k1, lam_q2, lam_k2, g_sub, w_out, g_x, g_mem, wq_x, wk_x, wv_x, wo_x, g_ffn, w_router, router_bias, w_gate, w_up, w_down, ws_gate, ws_up, ws_down, g_final):
    b, l, d = x.shape
    n = b * l
    depth = w_in.shape[0]
    row = lambda a: a.reshape(1, -1).astype(F32)
    inv = ROPE_THETA ** (-jnp.arange(0, DA_QKDIM, 2, dtype=F32) / DA_QKDIM)
    inv_row = jnp.tile(inv, LANES // inv.shape[0]).reshape(1, LANES)
    pos3 = positions.reshape(b, l, 1)

    h = x
    for i in range(depth):
        lambda_init = 0.8 - 0.6 * math.exp(-0.3 * i)
        u, q, k, v = _inproj(h, pos3, row(g_mix[i]), w_in[i].astype(BF16), inv_row)

        ssm_mats = _ssm_matrices(a_re[i], a_im[i], log_dt[i], b_re[i], b_im[i], c_re[i], c_im[i])
        y_ssm = _ssm(u, ssm_mats, row(d_skip[i]), w_glu[i].astype(BF16), row(g_ssm_out[i]))

        lam = (jnp.exp(jnp.sum(lam_q1[i].astype(F32) * lam_k1[i].astype(F32)))
               - jnp.exp(jnp.sum(lam_q2[i].astype(F32) * lam_k2[i].astype(F32))) + lambda_init).reshape(1)
        y_att = _diff_attention(q, k, v, lam.astype(F32), row(g_sub[i]), lambda_init)

        kmem, vmem = _memkv(mem, row(g_mem[i]), wk_x[i].astype(BF16), wv_x[i].astype(BF16))
        wo = w_out[i].astype(BF16)
        h2 = _mix(h, y_ssm, y_att, wo[:SSM_WIDTH], wo[SSM_WIDTH:], row(g_x[i]), wq_x[i].astype(BF16),
                  kmem, vmem, wo_x[i].astype(BF16))

        wsgu = jnp.concatenate([ws_gate[i], ws_up[i]], axis=1).astype(BF16)
        wr = w_router[i].T.astype(F32)
        wr_hi = wr.astype(BF16)
        wr_split = jnp.concatenate([wr_hi, (wr - wr_hi.astype(F32)).astype(BF16)], axis=0)
        n_part = n // MOE_PARTS
        out = None
        for part in range(MOE_PARTS):
            t_pk, base, eidx, gw, rank, counts = _route(
                h2.reshape(n, d), row(g_ffn[i]), wr_split,
                router_bias[i].reshape(N_EXPERTS, 1).astype(F32), wsgu, ws_down[i].astype(BF16), part, MOE_PARTS)
            pstart, schedule, n_slots = _block_schedule(counts[:, 0], n_part)
            dest3 = _plan(pstart, eidx, rank, SC_WINDOW)
            xs = _sc_dispatch(t_pk, dest3, n_slots)
            ys = _experts(schedule, xs, w_gate[i], w_up[i], w_down[i])
            yg = _sc_combine(ys, dest3)
            out = _final(base, yg, gw.T, row(g_final), i == depth - 1, out, part, MOE_PARTS)
        h = out.reshape(b, l, d)
    return h
```

```python
import functools
import math

import jax
import jax.numpy as jnp
from jax import lax
from jax.experimental import pallas as pl
from jax.experimental.pallas import tpu as pltpu
from jax.experimental.pallas import tpu_sc as plsc

F32 = jnp.float32
BF16 = jnp.bfloat16
I32 = jnp.int32

D_MODEL = 1024
SSM_WIDTH = 512
ATTN_WIDTH = 512
SSM_GROUP = 16
SSM_GROUPS = 32
SSM_STATE = 64
N_STATE = SSM_GROUPS * SSM_STATE
DA_HEADS = 4
DA_VDIM = 128
DA_QKDIM = 64
ROPE_THETA = 10000.0
ROPE_FREQS = DA_QKDIM // 2
X_HEADS = 4
X_HEAD_DIM = 256
N_EXPERTS = 64
TOP_K = 8
N_EXPERT_GROUPS = 8
PER_GROUP = N_EXPERTS // N_EXPERT_GROUPS
TOPK_GROUPS = 4
D_EXPERT = 256
ROUTED_SCALE = 2.5
EPS = 1e-6

VMEM_LIMIT_V7X = 56 * 1024 * 1024
SUBLANES = 8
LANES = 128
HALF_WORD_BITS = 16
HIGH_HALF_MASK = -(1 << HALF_WORD_BITS)

IN_TILE = 1024
SSM_STEPS = 128
SSM_COLS = 1024
SSM_CH = 128
SSM_SLAB = 512
ATT_TILE = 2048
ATT_ROWS = 256
MIX_TILE = 1024
ROUTE_TILE = 1024
MOE_TILE = 512
X_SLOTS = 3
PLAN_TILE = 2048
MOE_PARTS = 2
FIN_TILE = 512

SC_CORES = 2
SC_WORKERS = 32
SC_WINDOW = 64


def _params(sem):
    return pltpu.CompilerParams(dimension_semantics=sem, vmem_limit_bytes=VMEM_LIMIT_V7X)


def _rms(x, g):
    return x * lax.rsqrt(jnp.mean(x * x, axis=-1, keepdims=True) + EPS) * g


def _full(shape):
    return pl.BlockSpec(shape, lambda *_: (0,) * len(shape))


def _pack_rows(a):
    w = a.shape[1] // 2
    bits = lambda v: lax.bitcast_convert_type(v.astype(BF16).astype(F32), I32)
    return (bits(a[:, w:]) & jnp.int32(HIGH_HALF_MASK)) | lax.shift_right_logical(bits(a[:, :w]), HALF_WORD_BITS)


def _unpack_rows(p):
    lo = lax.bitcast_convert_type(lax.shift_left(p, HALF_WORD_BITS), F32)
    hi = lax.bitcast_convert_type(p & jnp.int32(HIGH_HALF_MASK), F32)
    return jnp.concatenate([lo, hi], axis=1)


def _inproj_kernel(x_ref, pos_ref, g_ref, w_ref, inv_ref, u_ref, q_ref, k_ref, v_ref):
    x = x_ref[0]
    a = _rms(x, g_ref[...]).astype(BF16)
    z = jnp.dot(a, w_ref[...], preferred_element_type=F32)
    u_ref[0] = z[:, :SSM_WIDTH]
    n_grp = LANES // ROPE_FREQS
    tq = x.shape[0] // n_grp
    pos = pos_ref[0].astype(F32)
    lane_c = lax.broadcasted_iota(I32, (tq, LANES), 1)
    pos_c = jnp.zeros((tq, LANES), F32)
    for j in range(n_grp):
        pos_c = jnp.where(lane_c // ROPE_FREQS == j, pos[j * tq:(j + 1) * tq], pos_c)
    ang = pos_c * inv_ref[...]

    def spread(c):
        rows = []
        for j in range(n_grp):
            y = c if j == 0 else pltpu.roll(c, LANES - ROPE_FREQS * j, 1)
            w = ROPE_FREQS
            while w < LANES:
                y = jnp.where(lane_c < w, y, pltpu.roll(y, w, 1))
                w *= 2
            rows.append(y)
        return jnp.concatenate(rows, axis=0)

    cos = jnp.concatenate([spread(jnp.cos(ang))] * (ATTN_WIDTH // LANES), axis=1)
    sin = jnp.concatenate([spread(jnp.sin(ang))] * (ATTN_WIDTH // LANES), axis=1)
    lane = lax.broadcasted_iota(I32, cos.shape, 1)
    first = (lane & (DA_QKDIM - 1)) < DA_QKDIM // 2
    half = DA_QKDIM // 2

    def rope(t):
        rot = jnp.where(first, -pltpu.roll(t, ATTN_WIDTH - half, 1), pltpu.roll(t, half, 1))
        return t * cos + rot * sin

    q = z[:, SSM_WIDTH:SSM_WIDTH + ATTN_WIDTH]
    k = z[:, SSM_WIDTH + ATTN_WIDTH:SSM_WIDTH + 2 * ATTN_WIDTH]
    q_ref[0] = (rope(q) * (DA_QKDIM ** -0.5)).astype(BF16)
    k_ref[0] = rope(k).astype(BF16)
    v_ref[0] = z[:, SSM_WIDTH + 2 * ATTN_WIDTH:].astype(BF16)


def _inproj(x, pos3, g_mix, w_in, inv_row):
    b, l, d = x.shape
    t = min(IN_TILE, l)
    n_out = w_in.shape[1]
    row = lambda w: pl.BlockSpec((1, t, w), lambda i, j: (i, j, 0))
    return pl.pallas_call(
        _inproj_kernel,
        grid=(b, l // t),
        in_specs=[row(d), row(1), _full((1, d)), _full((d, n_out)), _full((1, LANES))],
        out_specs=[row(SSM_WIDTH), row(ATTN_WIDTH), row(ATTN_WIDTH), row(ATTN_WIDTH)],
        out_shape=[jax.ShapeDtypeStruct((b, l, SSM_WIDTH), F32)]
        + [jax.ShapeDtypeStruct((b, l, ATTN_WIDTH), BF16)] * 3,
        compiler_params=_params(("parallel", "parallel")),
        name="inproj",
    )(x, pos3, g_mix, w_in, inv_row)


def _ssm_kernel(u_ref, bm_ref, are_ref, aim_ref, cm_ref, dm_ref, dskip_ref, wglu_ref, g_ref, o_ref,
                c_ref, st_ref, up_ref, *, steps):
    @pl.when(pl.program_id(0) == 0)
    def _():
        st_ref[...] = jnp.zeros_like(st_ref)
        up_ref[...] = jnp.zeros_like(up_ref)

    pairs = steps // 2
    rows = pairs * SUBLANES
    u = pltpu.einshape("bts->tbs", u_ref[...]).reshape(pairs, 2, SUBLANES, SSM_WIDTH)
    u_even = u[:, 0].reshape(rows, SSM_WIDTH)
    u_odd = u[:, 1].reshape(rows, SSM_WIDTH)
    u_before = jnp.concatenate([up_ref[...], u_odd[:rows - SUBLANES]], axis=0)
    up_ref[...] = u_odd[rows - SUBLANES:]
    ue, uo, ub = u_even.astype(BF16), u_odd.astype(BF16), u_before.astype(BF16)

    n_slabs = SSM_WIDTH // SSM_CH
    for s in range(n_slabs):
        ch = slice(s * SSM_CH, (s + 1) * SSM_CH)
        lhs = jnp.concatenate([ue[:, ch], ub[:, ch]], axis=1)
        for part in (0, 1):
            cols = slice(part * N_STATE + s * SSM_SLAB, part * N_STATE + (s + 1) * SSM_SLAB)
            c_ref[:, cols] = jnp.dot(lhs, bm_ref[s, :, part * SSM_SLAB:(part + 1) * SSM_SLAB],
                                     preferred_element_type=F32)

    for c0 in range(0, N_STATE, SSM_COLS):
        re = slice(c0, c0 + SSM_COLS)
        im = slice(N_STATE + c0, N_STATE + c0 + SSM_COLS)
        ar = are_ref[:, re]
        ai = aim_ref[:, re]
        sr, si = st_ref[:, re], st_ref[:, im]
        for k in range(pairs):
            blk = slice(k * SUBLANES, (k + 1) * SUBLANES)
            sr, si = ar * sr - ai * si + c_ref[blk, re], ar * si + ai * sr + c_ref[blk, im]
            c_ref[blk, re] = sr
            c_ref[blk, im] = si
        st_ref[:, re] = sr
        st_ref[:, im] = si

    y_even, y_odd = [], []
    for s in range(n_slabs):
        ch = slice(s * SSM_CH, (s + 1) * SSM_CH)
        acc = jnp.dot(uo[:, ch], dm_ref[s], preferred_element_type=F32)
        both = None
        for part in (0, 1):
            cols = slice(part * N_STATE + s * SSM_SLAB, part * N_STATE + (s + 1) * SSM_SLAB)
            d = jnp.dot(c_ref[:, cols].astype(BF16), cm_ref[s, part * SSM_SLAB:(part + 1) * SSM_SLAB],
                        preferred_element_type=F32)
            both = d if both is None else both + d
        y_even.append(both[:, :SSM_CH])
        y_odd.append(both[:, SSM_CH:] + acc)
    y = jnp.concatenate([jnp.concatenate(y_even, axis=1), jnp.concatenate(y_odd, axis=1)], axis=0)
    y = y + dskip_ref[...] * jnp.concatenate([u_even, u_odd], axis=0)
    y = jax.nn.gelu(y)
    y = y * jax.nn.sigmoid(jnp.dot(y.astype(BF16), wglu_ref[...], preferred_element_type=F32))
    out = _rms(y, g_ref[...])
    out = jnp.stack([out[:rows].reshape(pairs, SUBLANES, SSM_WIDTH),
                     out[rows:].reshape(pairs, SUBLANES, SSM_WIDTH)], axis=1).reshape(steps, SUBLANES, SSM_WIDTH)
    o_ref[...] = pltpu.einshape("tbs->bts", out).astype(BF16)


def _ssm(u, mats, d_skip, w_glu, g_out):
    bmat, a2_re8, a2_im8, cmat, dmat = mats
    batch, l, _ = u.shape
    assert batch == SUBLANES, "one time step of all sequences must fill the sublanes"
    steps = min(SSM_STEPS, l)
    assert steps % 2 == 0
    rows = steps // 2 * batch
    n_slabs = SSM_WIDTH // SSM_CH
    seq = pl.BlockSpec((batch, steps, SSM_WIDTH), lambda i: (0, i, 0))
    return pl.pallas_call(
        functools.partial(_ssm_kernel, steps=steps),
        grid=(l // steps,),
        in_specs=[seq,
                  _full((n_slabs, 2 * SSM_CH, 2 * SSM_SLAB)), _full((SUBLANES, N_STATE)), _full((SUBLANES, N_STATE)),
                  _full((n_slabs, 2 * SSM_SLAB, 2 * SSM_CH)), _full((n_slabs, SSM_CH, SSM_CH)),
                  _full((1, SSM_WIDTH)), _full((SSM_WIDTH, SSM_WIDTH)), _full((1, SSM_WIDTH))],
        out_specs=seq,
        out_shape=jax.ShapeDtypeStruct((batch, l, SSM_WIDTH), BF16),
        scratch_shapes=[pltpu.VMEM((rows, 2 * N_STATE), F32), pltpu.VMEM((SUBLANES, 2 * N_STATE), F32),
                        pltpu.VMEM((SUBLANES, SSM_WIDTH), F32)],
        compiler_params=_params(("arbitrary",)),
        name="ssm",
    )(u, bmat, a2_re8, a2_im8, cmat, dmat, d_skip, w_glu, g_out)


def _attn_kernel(qi_ref, ki_ref, lam_ref, q_ref, k_ref, v_ref, g_ref, o_ref, qs_ref, m_ref, acc_ref, *,
                 tile, out_scale):
    qi = qi_ref[pl.program_id(2)]
    ki = ki_ref[pl.program_id(2)]

    @pl.when(ki == 0)
    def _():
        q = q_ref[0]
        lane = lax.broadcasted_iota(I32, q.shape, 1)
        zero = jnp.zeros_like(q)
        qs_ref[:tile] = jnp.where(lane < DA_QKDIM, q, zero)
        qs_ref[tile:] = jnp.where(lane >= DA_QKDIM, q, zero)
        m_ref[...] = jnp.full_like(m_ref, -jnp.inf)
        acc_ref[...] = jnp.zeros_like(acc_ref)

    def update(masked):
        v_ext = jnp.concatenate([v_ref[0], jnp.ones((tile, DA_VDIM), BF16)], axis=1)
        rb = min(ATT_ROWS, tile)
        n_rb = 2 * tile // rb

        def n_keys(r):
            return (r * rb) % tile + rb if masked else tile

        def scores(r):
            return lax.dot_general(qs_ref[r * rb:(r + 1) * rb], k_ref[0, :n_keys(r)], (((1,), (1,)), ((), ())),
                                   preferred_element_type=F32)

        s_next = scores(0)
        for r in range(n_rb):
            s = s_next
            if r + 1 < n_rb:
                s_next = scores(r + 1)
            rows = slice(r * rb, (r + 1) * rb)
            nk = n_keys(r)
            if masked:
                tri = lax.broadcasted_iota(I32, (rb, rb), 1) <= lax.broadcasted_iota(I32, (rb, rb), 0)
                edge = jnp.where(tri, s[:, nk - rb:], jnp.finfo(F32).min)
                s = edge if nk == rb else jnp.concatenate([s[:, :nk - rb], edge], axis=1)
            m_old = m_ref[rows]
            m_new = jnp.maximum(m_old, jnp.max(s, axis=1, keepdims=True))
            p = jnp.exp(s - jnp.concatenate([m_new] * (nk // DA_VDIM), axis=1))
            alpha = jnp.exp(m_old - m_new)
            acc_ref[rows] = (jnp.concatenate([alpha, alpha], axis=1) * acc_ref[rows]
                             + jnp.dot(p.astype(BF16), v_ext[:nk], preferred_element_type=F32))
            m_ref[rows] = m_new

    @pl.when(ki < qi)
    def _():
        update(False)

    @pl.when(ki == qi)
    def _():
        update(True)
        o = acc_ref[:, :DA_VDIM] / acc_ref[:, DA_VDIM:]
        o = o[:tile] - lam_ref[0] * o[tile:]
        o_ref[0] = (_rms(o, g_ref[...]) * out_scale).astype(BF16)


def _diff_attention(q, k, v, lam, g_sub, lambda_init):
    b, l, _ = q.shape
    t = min(ATT_TILE, l)
    n = l // t
    pairs = [(qi, ki) for qi in range(n) for ki in range(qi + 1)]
    qi_tab = jnp.asarray([p[0] for p in pairs], I32)
    ki_tab = jnp.asarray([p[1] for p in pairs], I32)
    qspec = pl.BlockSpec((1, t, DA_VDIM), lambda bi, h, s, qt, kt: (bi, qt[s], h))
    kspec = pl.BlockSpec((1, t, DA_VDIM), lambda bi, h, s, qt, kt: (bi, kt[s], h))
    return pl.pallas_call(
        functools.partial(_attn_kernel, tile=t, out_scale=1.0 - lambda_init),
        grid_spec=pltpu.PrefetchScalarGridSpec(
            num_scalar_prefetch=2,
            grid=(b, DA_HEADS, len(pairs)),
            in_specs=[pl.BlockSpec(memory_space=pltpu.SMEM), qspec, kspec, kspec,
                      pl.BlockSpec((1, DA_VDIM), lambda bi, h, s, qt, kt: (0, 0))],
            out_specs=qspec,
            scratch_shapes=[pltpu.VMEM((2 * t, DA_VDIM), BF16), pltpu.VMEM((2 * t, DA_VDIM), F32),
                            pltpu.VMEM((2 * t, 2 * DA_VDIM), F32)]),
        out_shape=jax.ShapeDtypeStruct((b, l, ATTN_WIDTH), BF16),
        compiler_params=_params(("parallel", "parallel", "arbitrary")),
        name="diffattn",
    )(qi_tab, ki_tab, lam, q, k, v, g_sub)


def _memkv_kernel(m_ref, g_ref, wk_ref, wv_ref, k_ref, v_ref):
    a = _rms(m_ref[0], g_ref[...]).astype(BF16)
    k_ref[0] = jnp.dot(a, wk_ref[...], preferred_element_type=F32).astype(BF16)
    v_ref[0] = jnp.dot(a, wv_ref[...], preferred_element_type=F32).astype(BF16)


def _memkv(mem, g_mem, wk, wv):
    b, m, d = mem.shape
    blk = pl.BlockSpec((1, m, d), lambda i: (i, 0, 0))
    return pl.pallas_call(
        _memkv_kernel,
        grid=(b,),
        in_specs=[blk, _full((1, d)), _full((d, d)), _full((d, d))],
        out_specs=[blk, blk],
        out_shape=[jax.ShapeDtypeStruct((b, m, d), BF16)] * 2,
        compiler_params=_params(("parallel",)),
        name="memkv",
    )(mem, g_mem, wk, wv)


def _mix_kernel(x_ref, ys_ref, ya_ref, wo1_ref, wo2_ref, gx_ref, wq_ref, km_ref, vm_ref, wo_ref, h_ref):
    h = (x_ref[0]
         + jnp.dot(ys_ref[0], wo1_ref[...], preferred_element_type=F32)
         + jnp.dot(ya_ref[0], wo2_ref[...], preferred_element_type=F32))
    hq = _rms(h, gx_ref[...]).astype(BF16)
    q = jnp.dot(hq, wq_ref[...], preferred_element_type=F32).astype(BF16)
    outs = []
    for hd in range(X_HEADS):
        cols = slice(hd * X_HEAD_DIM, (hd + 1) * X_HEAD_DIM)
        s = lax.dot_general(q[:, cols], km_ref[0, :, cols], (((1,), (1,)), ((), ())),
                            preferred_element_type=F32) * (X_HEAD_DIM ** -0.5)
        s = s - jnp.max(s, axis=1, keepdims=True)
        p = jnp.exp(s)
        p = p / jnp.sum(p, axis=1, keepdims=True)
        outs.append(jnp.dot(p.astype(BF16), vm_ref[0, :, cols], preferred_element_type=F32).astype(BF16))
    o = jnp.concatenate(outs, axis=1)
    h_ref[0] = h + jnp.dot(o, wo_ref[...], preferred_element_type=F32)


def _mix(x, y_ssm, y_att, wo1, wo2, g_x, wq, kmem, vmem, wo):
    b, l, d = x.shape
    t = min(MIX_TILE, l)
    m = kmem.shape[1]
    row = lambda w: pl.BlockSpec((1, t, w), lambda i, j: (i, j, 0))
    mem = pl.BlockSpec((1, m, d), lambda i, j: (i, 0, 0))
    return pl.pallas_call(
        _mix_kernel,
        grid=(b, l // t),
        in_specs=[row(d), row(SSM_WIDTH), row(ATTN_WIDTH), _full((SSM_WIDTH, d)), _full((ATTN_WIDTH, d)),
                  _full((1, d)), _full((d, d)), mem, mem, _full((d, d))],
        out_specs=row(d),
        out_shape=jax.ShapeDtypeStruct((b, l, d), F32),
        compiler_params=_params(("parallel", "parallel")),
        name="mix",
    )(x, y_ssm, y_att, wo1, wo2, g_x, wq, kmem, vmem, wo)


def _first_index(hit, idx, sentinel):
    return jnp.min(jnp.where(hit, idx, sentinel), axis=0, keepdims=True)


def _route_kernel(h_ref, g_ref, wr_ref, bias_ref, wsgu_ref, wsd_ref, tri_ref,
                  t_ref, base_ref, eidx_ref, gw_ref, rank_ref, cnt_ref, carry_ref):
    @pl.when(pl.program_id(0) == 0)
    def _():
        carry_ref[...] = jnp.zeros_like(carry_ref)

    h = h_ref[...]
    t = _rms(h, g_ref[...])
    tb = t.astype(BF16)
    t_ref[...] = _pack_rows(t)
    gu = jnp.dot(tb, wsgu_ref[...], preferred_element_type=F32)
    hid = jax.nn.silu(gu[:, :D_EXPERT]) * gu[:, D_EXPERT:]
    base_ref[...] = h + jnp.dot(hid.astype(BF16), wsd_ref[...], preferred_element_type=F32)

    nt = (((1,), (1,)), ((), ()))
    t_lo = (t - tb.astype(F32)).astype(BF16)
    both = lax.dot_general(wr_ref[...], tb, nt, preferred_element_type=F32)
    logits = (both[:N_EXPERTS] + both[N_EXPERTS:]
              + lax.dot_general(wr_ref[:N_EXPERTS], t_lo, nt, preferred_element_type=F32))
    scores = jax.nn.sigmoid(logits)
    biased = scores + bias_ref[...]
    n_tok = scores.shape[1]
    neg = -jnp.inf
    sub = lax.broadcasted_iota(I32, (PER_GROUP, n_tok), 0)

    gs = []
    for g in range(N_EXPERT_GROUPS):
        blk = biased[g * PER_GROUP:(g + 1) * PER_GROUP]
        m1 = jnp.max(blk, axis=0, keepdims=True)
        i1 = _first_index(blk == m1, sub, PER_GROUP)
        m2 = jnp.max(jnp.where(sub == i1, neg, blk), axis=0, keepdims=True)
        gs.append(m1 + m2)
    gs = jnp.concatenate(gs, axis=0)

    gsel = jnp.zeros(gs.shape, jnp.bool_)
    for _ in range(TOPK_GROUPS):
        m = jnp.max(gs, axis=0, keepdims=True)
        hit = sub == _first_index(gs == m, sub, N_EXPERT_GROUPS)
        gsel = jnp.logical_or(gsel, hit)
        gs = jnp.where(hit, neg, gs)

    masked = jnp.concatenate(
        [jnp.where(gsel[g:g + 1], biased[g * PER_GROUP:(g + 1) * PER_GROUP], neg)
         for g in range(N_EXPERT_GROUPS)], axis=0)
    eid = lax.broadcasted_iota(I32, masked.shape, 0)
    sel = jnp.zeros(masked.shape, jnp.bool_)
    idxs, gws = [], []
    for _ in range(TOP_K):
        m = jnp.max(masked, axis=0, keepdims=True)
        i = _first_index(masked == m, eid, N_EXPERTS)
        hit = eid == i
        idxs.append(i)
        gws.append(jnp.sum(jnp.where(hit, scores, 0.0), axis=0, keepdims=True))
        sel = jnp.logical_or(sel, hit)
        masked = jnp.where(hit, neg, masked)
    eidx = jnp.concatenate(idxs, axis=0)
    gw = jnp.concatenate(gws, axis=0)
    gw = gw / jnp.sum(gw, axis=0, keepdims=True) * ROUTED_SCALE
    eidx_ref[...] = eidx
    gw_ref[...] = gw

    before = jnp.dot(sel.astype(BF16), tri_ref[...], preferred_element_type=F32) + carry_ref[...]
    rank_ref[...] = jnp.concatenate(
        [jnp.sum(jnp.where(eid == idxs[k], before, 0.0), axis=0, keepdims=True) for k in range(TOP_K)],
        axis=0).astype(I32)
    carry = carry_ref[...] + jnp.sum(sel.astype(F32), axis=1, keepdims=True)
    carry_ref[...] = carry
    cnt_ref[...] = carry.astype(I32)


def _route(h2, g_ffn, wr_t, bias_col, wsgu, wsd, part, n_parts):
    n_all, d = h2.shape
    n = n_all // n_parts
    t = min(ROUTE_TILE, n)
    blk0 = part * (n // t)
    tri = (lax.broadcasted_iota(I32, (t, t), 0) < lax.broadcasted_iota(I32, (t, t), 1)).astype(BF16)
    row = lambda w: pl.BlockSpec((t, w), lambda i: (i, 0))
    col = pl.BlockSpec((TOP_K, t), lambda i: (0, i))
    return pl.pallas_call(
        _route_kernel,
        grid=(n // t,),
        in_specs=[pl.BlockSpec((t, d), lambda i: (i + blk0, 0)), _full((1, d)), _full((2 * N_EXPERTS, d)),
                  _full((N_EXPERTS, 1)),
                  _full((d, 2 * D_EXPERT)), _full((D_EXPERT, d)), _full((t, t))],
        out_specs=[row(d // 2), row(d), col, col, col, _full((N_EXPERTS, 1))],
        out_shape=[jax.ShapeDtypeStruct((n, d // 2), I32), jax.ShapeDtypeStruct((n, d), F32),
                   jax.ShapeDtypeStruct((TOP_K, n), I32), jax.ShapeDtypeStruct((TOP_K, n), F32),
                   jax.ShapeDtypeStruct((TOP_K, n), I32), jax.ShapeDtypeStruct((N_EXPERTS, 1), I32)],
        scratch_shapes=[pltpu.VMEM((N_EXPERTS, 1), F32)],
        compiler_params=_params(("arbitrary",)),
        name="route",
    )(h2, g_ffn, wr_t, bias_col, wsgu, wsd, tri)


def _expert_kernel(be_ref, first_ref, slot_ref, next_ref, nu_ref, x_hbm, wg_hbm, wu_hbm, wd_hbm, y_ref,
                   wg_s, wu_s, wd_s, x_buf, x_sem, wg_buf, wu_buf, wd_buf, w_sem):
    i = pl.program_id(0)
    n_used = nu_ref[0]

    def fetch(step):
        slot = step % X_SLOTS
        rows = pl.ds(pl.multiple_of(step * MOE_TILE, MOE_TILE), MOE_TILE)
        return pltpu.make_async_copy(x_hbm.at[rows], x_buf.at[slot], x_sem.at[slot])

    def weight_fetch(expert, slot):
        pairs = ((wg_hbm, wg_buf), (wu_hbm, wu_buf), (wd_hbm, wd_buf))
        return [pltpu.make_async_copy(src.at[expert], buf.at[slot], w_sem.at[slot, j])
                for j, (src, buf) in enumerate(pairs)]

    @pl.when(i == 0)
    def _():
        for j in range(X_SLOTS - 1):
            @pl.when(j < n_used)
            def _(j=j):
                fetch(j).start()
        for cp in weight_fetch(be_ref[0], 0):
            cp.start()

    @pl.when(i + (X_SLOTS - 1) < n_used)
    def _():
        fetch(i + (X_SLOTS - 1)).start()

    @pl.when(first_ref[i] == 1)
    def _():
        slot = slot_ref[i]

        @pl.when(next_ref[i] >= 0)
        def _():
            for cp in weight_fetch(next_ref[i], 1 - slot):
                cp.start()

        for cp in weight_fetch(be_ref[i], slot):
            cp.wait()
        wg_s[...] = wg_buf[slot].astype(BF16)
        wu_s[...] = wu_buf[slot].astype(BF16)
        wd_s[...] = wd_buf[slot].astype(BF16)

    @pl.when(i < n_used)
    def _():
        fetch(i).wait()
        x = _unpack_rows(x_buf[i % X_SLOTS]).astype(BF16)
        gate = jnp.dot(x, wg_s[...], preferred_element_type=F32)
        up = jnp.dot(x, wu_s[...], preferred_element_type=F32)
        hid = (jax.nn.silu(gate) * up).astype(BF16)
        y_ref[...] = _pack_rows(jnp.dot(hid, wd_s[...], preferred_element_type=F32))


def _experts(schedule, xs, w_gate, w_up, w_down):
    block_e, run_first, run_slot, run_next, n_used = schedule
    n_slots, dw = xs.shape
    _, d, de = w_gate.shape
    nb = n_slots // MOE_TILE
    hbm = pl.BlockSpec(memory_space=pl.ANY)
    return pl.pallas_call(
        _expert_kernel,
        grid_spec=pltpu.PrefetchScalarGridSpec(
            num_scalar_prefetch=5,
            grid=(nb,),
            in_specs=[hbm, hbm, hbm, hbm],
            out_specs=pl.BlockSpec((MOE_TILE, dw), lambda i, be, fi, sl, nx, nu: (jnp.minimum(i, nu[0] - 1), 0)),
            scratch_shapes=[pltpu.VMEM((d, de), BF16), pltpu.VMEM((d, de), BF16), pltpu.VMEM((de, d), BF16),
                            pltpu.VMEM((X_SLOTS, MOE_TILE, dw), I32), pltpu.SemaphoreType.DMA((X_SLOTS,)),
                            pltpu.VMEM((2, d, de), F32), pltpu.VMEM((2, d, de), F32), pltpu.VMEM((2, de, d), F32),
                            pltpu.SemaphoreType.DMA((2, 3))]),
        out_shape=jax.ShapeDtypeStruct((n_slots, dw), I32),
        compiler_params=_params(("arbitrary",)),
        name="experts",
    )(block_e, run_first, run_slot, run_next, n_used, xs, w_gate, w_up, w_down)


def _sc_worker_id():
    return lax.axis_index("s") * SC_CORES + lax.axis_index("c")


def _sc_dispatch(t_rows, dest3, n_slots):
    _, dw = t_rows.shape
    n_chunks, _, w = dest3.shape
    per_worker = n_chunks // SC_WORKERS
    assert per_worker % 2 == 0
    mesh = plsc.VectorSubcoreMesh(core_axis_name="c", subcore_axis_name="s")
    dt = t_rows.dtype

    @functools.partial(
        pl.kernel, mesh=mesh,
        out_type=jax.ShapeDtypeStruct((n_slots, dw), dt),
        scratch_types=[pltpu.VMEM((TOP_K, w), I32), pltpu.VMEM((TOP_K, w), I32),
                       pltpu.VMEM((w, dw), dt), pltpu.VMEM((w, dw), dt),
                       pltpu.SemaphoreType.DMA, pltpu.SemaphoreType.DMA, pltpu.SemaphoreType.DMA],
    )
    def scatter_rows(t_hbm, dest_hbm, xs_hbm, idx_a, idx_b, rows_a, rows_b, sem_load, sem_a, sem_b):
        wid = _sc_worker_id()

        def scatter(idx_v, rows_v, sem):
            return [pltpu.async_copy(rows_v, xs_hbm.at[idx_v.at[k]], sem) for k in range(TOP_K)]

        @pl.loop(0, per_worker, step=2)
        def _(j):
            ca = wid * per_worker + j
            cb = ca + 1
            pltpu.sync_copy(dest_hbm.at[ca], idx_a)
            pltpu.sync_copy(t_hbm.at[pl.ds(ca * w, w)], rows_a)
            load_idx = pltpu.async_copy(dest_hbm.at[cb], idx_b, sem_load)
            load_rows = pltpu.async_copy(t_hbm.at[pl.ds(cb * w, w)], rows_b, sem_load)
            out_a = scatter(idx_a, rows_a, sem_a)
            load_idx.wait()
            load_rows.wait()
            out_b = scatter(idx_b, rows_b, sem_b)
            for cp in out_a + out_b:
                cp.wait()

    return scatter_rows(t_rows, dest3)


def _sc_combine(y_rows, dest3):
    _, dw = y_rows.shape
    n_chunks, _, w = dest3.shape
    per_worker = n_chunks // SC_WORKERS
    mesh = plsc.VectorSubcoreMesh(core_axis_name="c", subcore_axis_name="s")
    dt = y_rows.dtype

    @functools.partial(
        pl.kernel, mesh=mesh,
        out_type=jax.ShapeDtypeStruct((TOP_K, n_chunks * w, dw), dt),
        scratch_types=[pltpu.VMEM((TOP_K, w), I32), pltpu.VMEM((w, dw), dt), pltpu.VMEM((w, dw), dt),
                       pltpu.SemaphoreType.DMA, pltpu.SemaphoreType.DMA,
                       pltpu.SemaphoreType.DMA, pltpu.SemaphoreType.DMA],
    )
    def gather_rows(y_hbm, dest_hbm, out_hbm, idx_v, buf0, buf1, gsem0, gsem1, wsem0, wsem1):
        wid = _sc_worker_id()
        bufs, gsems, wsems = (buf0, buf1), (gsem0, gsem1), (wsem0, wsem1)

        @pl.loop(0, per_worker)
        def _(j):
            c = wid * per_worker + j
            pltpu.sync_copy(dest_hbm.at[c], idx_v)
            gathers = [None] * TOP_K
            writes = [None] * TOP_K
            gathers[0] = pltpu.async_copy(y_hbm.at[idx_v.at[0]], bufs[0], gsems[0])
            for k in range(TOP_K):
                b = k % 2
                if k + 1 < TOP_K:
                    if k >= 1:
                        writes[k - 1].wait()
                    gathers[k + 1] = pltpu.async_copy(y_hbm.at[idx_v.at[k + 1]], bufs[1 - b], gsems[1 - b])
                gathers[k].wait()
                writes[k] = pltpu.async_copy(bufs[b], out_hbm.at[k, pl.ds(c * w, w)], wsems[b])
            writes[TOP_K - 2].wait()
            writes[TOP_K - 1].wait()

    return gather_rows(y_rows, dest3)


def _final_kernel(base_ref, y_ref, gw_ref, g_ref, *rest, normalize):
    o_ref = rest[-1]
    h = base_ref[...]
    eye = (lax.broadcasted_iota(I32, (TOP_K, TOP_K), 0) == lax.broadcasted_iota(I32, (TOP_K, TOP_K), 1)).astype(F32)
    gw = lax.dot_general(gw_ref[...], eye, (((0,), (0,)), ((), ())), preferred_element_type=F32,
                         precision=lax.Precision.HIGHEST)
    for k in range(TOP_K):
        h = h + gw[:, k:k + 1] * _unpack_rows(y_ref[k])
    o_ref[...] = _rms(h, g_ref[...]) if normalize else h


def _final(base, yg, gw, g_final, normalize, out_prev, part, n_parts):
    n, d = base.shape
    t = min(FIN_TILE, n)
    blk0 = part * (n // t)
    in_specs = [pl.BlockSpec((t, d), lambda i: (i, 0)), pl.BlockSpec((TOP_K, t, d // 2), lambda i: (0, i, 0)),
                pl.BlockSpec((TOP_K, t), lambda i: (0, i)), _full((1, d))]
    args = [base, yg, gw, g_final]
    aliases = {}
    if out_prev is not None:
        in_specs.append(pl.BlockSpec(memory_space=pl.ANY))
        args.append(out_prev)
        aliases = {len(args) - 1: 0}
    return pl.pallas_call(
        functools.partial(_final_kernel, normalize=normalize),
        grid=(n // t,),
        in_specs=in_specs,
        out_specs=pl.BlockSpec((t, d), lambda i: (i + blk0, 0)),
        out_shape=jax.ShapeDtypeStruct((n * n_parts, d), F32),
        input_output_aliases=aliases,
        compiler_params=_params(("parallel",)),
        name="final",
    )(*args)


def _ssm_matrices(a_re, a_im, log_dt, b_re, b_im, c_re, c_im):
    lr, li = a_re.astype(F32), a_im.astype(F32)
    dt = jnp.exp(log_dt.astype(F32))[:, None]
    mag = jnp.exp(lr * dt)
    ab_re, ab_im = mag * jnp.cos(li * dt), mag * jnp.sin(li * dt)
    den = lr * lr + li * li
    zr, zi = ab_re - 1.0, ab_im
    k_re = (zr * lr + zi * li) / den
    k_im = (zi * lr - zr * li) / den
    br, bi = b_re.astype(F32), b_im.astype(F32)
    bb_re = k_re[..., None] * br - k_im[..., None] * bi
    bb_im = k_re[..., None] * bi + k_im[..., None] * br
    ar, ai = ab_re[..., None], ab_im[..., None]
    ab_b_re, ab_b_im = ar * bb_re - ai * bb_im, ar * bb_im + ai * bb_re
    cr, ci = c_re.astype(F32), c_im.astype(F32)
    car, cai = ab_re[:, None, :], ab_im[:, None, :]
    ca_re, ca_im = cr * car - ci * cai, cr * cai + ci * car
    direct = jnp.einsum("gcp,gpd->gdc", cr, bb_re) - jnp.einsum("gcp,gpd->gdc", ci, bb_im)
    n_slabs = SSM_WIDTH // SSM_CH
    per_slab = SSM_GROUPS // n_slabs
    eye = jnp.eye(per_slab, dtype=F32)

    def expand_blocks(bb):
        blocks = jnp.einsum("sgpc,gh->sgchp", bb.reshape(n_slabs, per_slab, SSM_STATE, SSM_GROUP), eye)
        return blocks.reshape(n_slabs, SSM_CH, SSM_SLAB)

    def contract_blocks(c):
        blocks = jnp.einsum("sgcp,gh->sgphc", c.reshape(n_slabs, per_slab, SSM_GROUP, SSM_STATE), eye)
        return blocks.reshape(n_slabs, SSM_SLAB, SSM_CH)

    bmat = jnp.concatenate([jnp.concatenate([expand_blocks(bb_re), expand_blocks(bb_im)], axis=2),
                            jnp.concatenate([expand_blocks(ab_b_re), expand_blocks(ab_b_im)], axis=2)], axis=1)
    cmat = jnp.concatenate([jnp.concatenate([contract_blocks(cr), contract_blocks(ca_re)], axis=2),
                            jnp.concatenate([-contract_blocks(ci), -contract_blocks(ca_im)], axis=2)], axis=1)
    dmat = jnp.einsum("sgdc,gh->sgdhc", direct.reshape(n_slabs, per_slab, SSM_GROUP, SSM_GROUP), eye)
    dmat = dmat.reshape(n_slabs, SSM_CH, SSM_CH)
    a2_re, a2_im = ab_re * ab_re - ab_im * ab_im, 2.0 * ab_re * ab_im
    tile = lambda a: jnp.broadcast_to(a.reshape(1, N_STATE), (SUBLANES, N_STATE))
    return bmat.astype(BF16), tile(a2_re), tile(a2_im), cmat.astype(BF16), dmat.astype(BF16)


def _plan_kernel(pstart_ref, eidx_ref, rank_ref, dest_ref, *, window):
    eidx = eidx_ref[...]
    dest = rank_ref[...]
    for e in range(N_EXPERTS):
        dest = dest + jnp.where(eidx == e, pstart_ref[e], 0)
    for c in range(dest.shape[1] // window):
        dest_ref[c] = dest[:, c * window:(c + 1) * window]


def _plan(pstart, eidx, rank, window):
    k, n = eidx.shape
    t = min(PLAN_TILE, n)
    cols = pl.BlockSpec((k, t), lambda i, ps: (0, i))
    return pl.pallas_call(
        functools.partial(_plan_kernel, window=window),
        grid_spec=pltpu.PrefetchScalarGridSpec(
            num_scalar_prefetch=1,
            grid=(n // t,),
            in_specs=[cols, cols],
            out_specs=pl.BlockSpec((t // window, k, window), lambda i, ps: (i, 0, 0))),
        out_shape=jax.ShapeDtypeStruct((n // window, k, window), I32),
        compiler_params=_params(("parallel",)),
        name="plan",
    )(pstart, eidx, rank)


def _block_schedule(counts, n_tok):
    padded = ((counts + MOE_TILE - 1) // MOE_TILE) * MOE_TILE
    pend = jnp.cumsum(padded)
    pstart = (pend - padded).astype(I32)
    n_slots = n_tok * TOP_K + N_EXPERTS * MOE_TILE
    nb = n_slots // MOE_TILE
    n_used = (pend[-1] // MOE_TILE).astype(I32)
    blk = jnp.arange(nb, dtype=I32)
    block_e = jnp.sum((pend[None, :] <= (jnp.minimum(blk, n_used - 1) * MOE_TILE)[:, None]).astype(I32), axis=1)
    block_e = jnp.minimum(block_e, N_EXPERTS - 1)
    eid = jnp.arange(N_EXPERTS, dtype=I32)
    active = padded > 0
    run_of = jnp.cumsum(active.astype(I32)) - 1
    later = jnp.where(jnp.logical_and(active[None, :], eid[None, :] > eid[:, None]), eid[None, :], N_EXPERTS)
    next_e = jnp.min(later, axis=1)
    next_e = jnp.where(next_e < N_EXPERTS, next_e, -1)
    onehot = block_e[:, None] == eid[None, :]
    lookup = lambda table: jnp.sum(jnp.where(onehot, table[None, :], 0), axis=1).astype(I32)
    run_first = jnp.logical_and(blk * MOE_TILE == lookup(pstart), blk < n_used).astype(I32)
    schedule = (block_e, run_first, lookup(run_of % 2), lookup(next_e), n_used.reshape(1))
    return pstart, schedule, n_slots


def kernel(x, mem, positions, g_mix, w_in, a_re, a_im, log_dt, b_re, b_im, c_re, c_im, d_skip, w_glu, g_ssm_out, lam_q1, lam_k1, lam_q2, lam_k2, g_sub, w_out, g_x, g_mem, wq_x, wk_x, wv_x, wo_x, g_ffn, w_router, router_bias, w_gate, w_up, w_down, ws_gate, ws_up, ws_down, g_final):
    b, l, d = x.shape
    n = b * l
    depth = w_in.shape[0]
    row = lambda a: a.reshape(1, -1).astype(F32)
    inv = ROPE_THETA ** (-jnp.arange(0, DA_QKDIM, 2, dtype=F32) / DA_QKDIM)
    inv_row = jnp.tile(inv, LANES // inv.shape[0]).reshape(1, LANES)
    pos3 = positions.reshape(b, l, 1)

    h = x
    for i in range(depth):
        lambda_init = 0.8 - 0.6 * math.exp(-0.3 * i)
        u, q, k, v = _inproj(h, pos3, row(g_mix[i]), w_in[i].astype(BF16), inv_row)

        ssm_mats = _ssm_matrices(a_re[i], a_im[i], log_dt[i], b_re[i], b_im[i], c_re[i], c_im[i])
        y_ssm = _ssm(u, ssm_mats, row(d_skip[i]), w_glu[i].astype(BF16), row(g_ssm_out[i]))

        lam = (jnp.exp(jnp.sum(lam_q1[i].astype(F32) * lam_k1[i].astype(F32)))
               - jnp.exp(jnp.sum(lam_q2[i].astype(F32) * lam_k2[i].astype(F32))) + lambda_init).reshape(1)
        y_att = _diff_attention(q, k, v, lam.astype(F32), row(g_sub[i]), lambda_init)

        kmem, vmem = _memkv(mem, row(g_mem[i]), wk_x[i].astype(BF16), wv_x[i].astype(BF16))
        wo = w_out[i].astype(BF16)
        h2 = _mix(h, y_ssm, y_att, wo[:SSM_WIDTH], wo[SSM_WIDTH:], row(g_x[i]), wq_x[i].astype(BF16),
                  kmem, vmem, wo_x[i].astype(BF16))

        wsgu = jnp.concatenate([ws_gate[i], ws_up[i]], axis=1).astype(BF16)
        wr = w_router[i].T.astype(F32)
        wr_hi = wr.astype(BF16)
        wr_split = jnp.concatenate([wr_hi, (wr - wr_hi.astype(F32)).astype(BF16)], axis=0)
        n_part = n // MOE_PARTS
        out = None
        for part in range(MOE_PARTS):
            t_pk, base, eidx, gw, rank, counts = _route(
                h2.reshape(n, d), row(g_ffn[i]), wr_split,
                router_bias[i].reshape(N_EXPERTS, 1).astype(F32), wsgu, ws_down[i].astype(BF16), part, MOE_PARTS)
            pstart, schedule, n_slots = _block_schedule(counts[:, 0], n_part)
            dest3 = _plan(pstart, eidx, rank, SC_WINDOW)
            xs = _sc_dispatch(t_pk, dest3, n_slots)
            ys = _experts(schedule, xs, w_gate[i], w_up[i], w_down[i])
            yg = _sc_combine(ys, dest3)
            out = _final(base, yg, gw, row(g_final), i == depth - 1, out, part, MOE_PARTS)
        h = out.reshape(b, l, d)
    return h
```

```python
import functools
import math

import jax
import jax.numpy as jnp
from jax import lax
from jax.experimental import pallas as pl
from jax.experimental.pallas import tpu as pltpu
from jax.experimental.pallas import tpu_sc as plsc

F32 = jnp.float32
BF16 = jnp.bfloat16
I32 = jnp.int32

SSM_WIDTH = 512
ATTN_WIDTH = 512
SSM_GROUP = 16
SSM_GROUPS = 32
SSM_STATE = 64
N_STATE = SSM_GROUPS * SSM_STATE
DA_HEADS = 4
DA_VDIM = 128
DA_QKDIM = 64
ROPE_THETA = 10000.0
ROPE_FREQS = DA_QKDIM // 2
X_HEADS = 4
X_HEAD_DIM = 256
N_EXPERTS = 64
TOP_K = 8
N_EXPERT_GROUPS = 8
PER_GROUP = N_EXPERTS // N_EXPERT_GROUPS
TOPK_GROUPS = 4
D_EXPERT = 256
ROUTED_SCALE = 2.5
EPS = 1e-6

VMEM_LIMIT_V7X = 56 * 1024 * 1024
SUBLANES = 8
LANES = 128
HALF_WORD_BITS = 16
HIGH_HALF_MASK = -(1 << HALF_WORD_BITS)

IN_TILE = 1024
SSM_STEPS = 128
SSM_COLS = 1024
SSM_CH = 128
SSM_SLAB = 512
ATT_TILE = 2048
ATT_ROWS = 256
MIX_TILE = 1024
ROUTE_TILE = 1024
MOE_TILE = 512
X_SLOTS = 3
PLAN_TILE = 2048
MOE_PARTS = 2
FIN_TILE = 512

SC_CORES = 2
SC_WORKERS = 32
SC_WINDOW = 64


def _params(sem):
    return pltpu.CompilerParams(dimension_semantics=sem, vmem_limit_bytes=VMEM_LIMIT_V7X)


def _rms(x, g):
    return x * lax.rsqrt(jnp.mean(x * x, axis=-1, keepdims=True) + EPS) * g


def _full(shape):
    return pl.BlockSpec(shape, lambda *_: (0,) * len(shape))


def _pack_rows(a):
    w = a.shape[1] // 2
    bits = lambda v: lax.bitcast_convert_type(v.astype(BF16).astype(F32), I32)
    return (bits(a[:, w:]) & jnp.int32(HIGH_HALF_MASK)) | lax.shift_right_logical(bits(a[:, :w]), HALF_WORD_BITS)


def _unpack_rows(p):
    lo = lax.bitcast_convert_type(lax.shift_left(p, HALF_WORD_BITS), F32)
    hi = lax.bitcast_convert_type(p & jnp.int32(HIGH_HALF_MASK), F32)
    return jnp.concatenate([lo, hi], axis=1)


def _inproj_kernel(x_ref, pos_ref, g_ref, w_ref, inv_ref, u_ref, q_ref, k_ref, v_ref):
    x = x_ref[0]
    a = _rms(x, g_ref[...]).astype(BF16)
    z = jnp.dot(a, w_ref[...], preferred_element_type=F32)
    u_ref[0] = z[:, :SSM_WIDTH]
    n_grp = LANES // ROPE_FREQS
    tq = x.shape[0] // n_grp
    pos = pos_ref[0].astype(F32)
    lane_c = lax.broadcasted_iota(I32, (tq, LANES), 1)
    pos_c = jnp.zeros((tq, LANES), F32)
    for j in range(n_grp):
        pos_c = jnp.where(lane_c // ROPE_FREQS == j, pos[j * tq:(j + 1) * tq], pos_c)
    ang = pos_c * inv_ref[...]

    def spread(c):
        rows = []
        for j in range(n_grp):
            y = c if j == 0 else pltpu.roll(c, LANES - ROPE_FREQS * j, 1)
            w = ROPE_FREQS
            while w < LANES:
                y = jnp.where(lane_c < w, y, pltpu.roll(y, w, 1))
                w *= 2
            rows.append(y)
        return jnp.concatenate(rows, axis=0)

    cos = jnp.concatenate([spread(jnp.cos(ang))] * (ATTN_WIDTH // LANES), axis=1)
    sin = jnp.concatenate([spread(jnp.sin(ang))] * (ATTN_WIDTH // LANES), axis=1)
    lane = lax.broadcasted_iota(I32, cos.shape, 1)
    first = (lane & (DA_QKDIM - 1)) < DA_QKDIM // 2
    half = DA_QKDIM // 2

    def rope(t):
        rot = jnp.where(first, -pltpu.roll(t, ATTN_WIDTH - half, 1), pltpu.roll(t, half, 1))
        return t * cos + rot * sin

    q = z[:, SSM_WIDTH:SSM_WIDTH + ATTN_WIDTH]
    k = z[:, SSM_WIDTH + ATTN_WIDTH:SSM_WIDTH + 2 * ATTN_WIDTH]
    q_ref[0] = (rope(q) * (DA_QKDIM ** -0.5)).astype(BF16)
    k_ref[0] = rope(k).astype(BF16)
    v_ref[0] = z[:, SSM_WIDTH + 2 * ATTN_WIDTH:].astype(BF16)


def _inproj(x, pos3, g_mix, w_in, inv_row):
    b, l, d = x.shape
    t = min(IN_TILE, l)
    n_out = w_in.shape[1]
    row = lambda w: pl.BlockSpec((1, t, w), lambda i, j: (i, j, 0))
    return pl.pallas_call(
        _inproj_kernel,
        grid=(b, l // t),
        in_specs=[row(d), row(1), _full((1, d)), _full((d, n_out)), _full((1, LANES))],
        out_specs=[row(SSM_WIDTH), row(ATTN_WIDTH), row(ATTN_WIDTH), row(ATTN_WIDTH)],
        out_shape=[jax.ShapeDtypeStruct((b, l, SSM_WIDTH), F32)]
        + [jax.ShapeDtypeStruct((b, l, ATTN_WIDTH), BF16)] * 3,
        compiler_params=_params(("parallel", "parallel")),
        name="inproj",
    )(x, pos3, g_mix, w_in, inv_row)


def _ssm_kernel(u_ref, bm_ref, are_ref, aim_ref, cm_ref, dm_ref, dskip_ref, wglu_ref, g_ref, o_ref,
                c_ref, st_ref, up_ref, *, steps):
    @pl.when(pl.program_id(0) == 0)
    def _():
        st_ref[...] = jnp.zeros_like(st_ref)
        up_ref[...] = jnp.zeros_like(up_ref)

    pairs = steps // 2
    rows = pairs * SUBLANES
    u = pltpu.einshape("bts->tbs", u_ref[...]).reshape(pairs, 2, SUBLANES, SSM_WIDTH)
    u_even = u[:, 0].reshape(rows, SSM_WIDTH)
    u_odd = u[:, 1].reshape(rows, SSM_WIDTH)
    u_before = jnp.concatenate([up_ref[...], u_odd[:rows - SUBLANES]], axis=0)
    up_ref[...] = u_odd[rows - SUBLANES:]
    ue, uo, ub = u_even.astype(BF16), u_odd.astype(BF16), u_before.astype(BF16)

    n_slabs = SSM_WIDTH // SSM_CH
    for s in range(n_slabs):
        ch = slice(s * SSM_CH, (s + 1) * SSM_CH)
        lhs = jnp.concatenate([ue[:, ch], ub[:, ch]], axis=1)
        for part in (0, 1):
            cols = slice(part * N_STATE + s * SSM_SLAB, part * N_STATE + (s + 1) * SSM_SLAB)
            c_ref[:, cols] = jnp.dot(lhs, bm_ref[s, :, part * SSM_SLAB:(part + 1) * SSM_SLAB],
                                     preferred_element_type=F32)

    for c0 in range(0, N_STATE, SSM_COLS):
        re = slice(c0, c0 + SSM_COLS)
        im = slice(N_STATE + c0, N_STATE + c0 + SSM_COLS)
        ar = are_ref[:, re]
        ai = aim_ref[:, re]
        sr, si = st_ref[:, re], st_ref[:, im]
        for k in range(pairs):
            blk = slice(k * SUBLANES, (k + 1) * SUBLANES)
            sr, si = ar * sr - ai * si + c_ref[blk, re], ar * si + ai * sr + c_ref[blk, im]
            c_ref[blk, re] = sr
            c_ref[blk, im] = si
        st_ref[:, re] = sr
        st_ref[:, im] = si

    y_even, y_odd = [], []
    for s in range(n_slabs):
        ch = slice(s * SSM_CH, (s + 1) * SSM_CH)
        acc = jnp.dot(uo[:, ch], dm_ref[s], preferred_element_type=F32)
        both = None
        for part in (0, 1):
            cols = slice(part * N_STATE + s * SSM_SLAB, part * N_STATE + (s + 1) * SSM_SLAB)
            d = jnp.dot(c_ref[:, cols].astype(BF16), cm_ref[s, part * SSM_SLAB:(part + 1) * SSM_SLAB],
                        preferred_element_type=F32)
            both = d if both is None else both + d
        y_even.append(both[:, :SSM_CH])
        y_odd.append(both[:, SSM_CH:] + acc)
    y = jnp.concatenate([jnp.concatenate(y_even, axis=1), jnp.concatenate(y_odd, axis=1)], axis=0)
    y = y + dskip_ref[...] * jnp.concatenate([u_even, u_odd], axis=0)
    y = jax.nn.gelu(y)
    y = y * jax.nn.sigmoid(jnp.dot(y.astype(BF16), wglu_ref[...], preferred_element_type=F32))
    out = _rms(y, g_ref[...])
    out = jnp.stack([out[:rows].reshape(pairs, SUBLANES, SSM_WIDTH),
                     out[rows:].reshape(pairs, SUBLANES, SSM_WIDTH)], axis=1).reshape(steps, SUBLANES, SSM_WIDTH)
    o_ref[...] = pltpu.einshape("tbs->bts", out).astype(BF16)


def _ssm(u, mats, d_skip, w_glu, g_out):
    bmat, a2_re8, a2_im8, cmat, dmat = mats
    batch, l, _ = u.shape
    assert batch == SUBLANES, "one time step of all sequences must fill the sublanes"
    steps = min(SSM_STEPS, l)
    assert steps % 2 == 0
    rows = steps // 2 * batch
    n_slabs = SSM_WIDTH // SSM_CH
    seq = pl.BlockSpec((batch, steps, SSM_WIDTH), lambda i: (0, i, 0))
    return pl.pallas_call(
        functools.partial(_ssm_kernel, steps=steps),
        grid=(l // steps,),
        in_specs=[seq,
                  _full((n_slabs, 2 * SSM_CH, 2 * SSM_SLAB)), _full((SUBLANES, N_STATE)), _full((SUBLANES, N_STATE)),
                  _full((n_slabs, 2 * SSM_SLAB, 2 * SSM_CH)), _full((n_slabs, SSM_CH, SSM_CH)),
                  _full((1, SSM_WIDTH)), _full((SSM_WIDTH, SSM_WIDTH)), _full((1, SSM_WIDTH))],
        out_specs=seq,
        out_shape=jax.ShapeDtypeStruct((batch, l, SSM_WIDTH), BF16),
        scratch_shapes=[pltpu.VMEM((rows, 2 * N_STATE), F32), pltpu.VMEM((SUBLANES, 2 * N_STATE), F32),
                        pltpu.VMEM((SUBLANES, SSM_WIDTH), F32)],
        compiler_params=_params(("arbitrary",)),
        name="ssm",
    )(u, bmat, a2_re8, a2_im8, cmat, dmat, d_skip, w_glu, g_out)


def _attn_kernel(qi_ref, ki_ref, lam_ref, q_ref, k_ref, v_ref, g_ref, o_ref, qs_ref, m_ref, acc_ref, *,
                 tile, out_scale):
    qi = qi_ref[pl.program_id(2)]
    ki = ki_ref[pl.program_id(2)]

    @pl.when(ki == 0)
    def _():
        q = q_ref[0]
        lane = lax.broadcasted_iota(I32, q.shape, 1)
        zero = jnp.zeros_like(q)
        qs_ref[:tile] = jnp.where(lane < DA_QKDIM, q, zero)
        qs_ref[tile:] = jnp.where(lane >= DA_QKDIM, q, zero)
        m_ref[...] = jnp.full_like(m_ref, -jnp.inf)
        acc_ref[...] = jnp.zeros_like(acc_ref)

    def update(masked):
        v_ext = jnp.concatenate([v_ref[0], jnp.ones((tile, DA_VDIM), BF16)], axis=1)
        rb = min(ATT_ROWS, tile)
        n_rb = 2 * tile // rb

        def n_keys(r):
            return (r * rb) % tile + rb if masked else tile

        def scores(r):
            return lax.dot_general(qs_ref[r * rb:(r + 1) * rb], k_ref[0, :n_keys(r)], (((1,), (1,)), ((), ())),
                                   preferred_element_type=F32)

        s_next = scores(0)
        for r in range(n_rb):
            s = s_next
            if r + 1 < n_rb:
                s_next = scores(r + 1)
            rows = slice(r * rb, (r + 1) * rb)
            nk = n_keys(r)
            if masked:
                tri = lax.broadcasted_iota(I32, (rb, rb), 1) <= lax.broadcasted_iota(I32, (rb, rb), 0)
                edge = jnp.where(tri, s[:, nk - rb:], jnp.finfo(F32).min)
                s = edge if nk == rb else jnp.concatenate([s[:, :nk - rb], edge], axis=1)
            m_old = m_ref[rows]
            m_new = jnp.maximum(m_old, jnp.max(s, axis=1, keepdims=True))
            p = jnp.exp(s - jnp.concatenate([m_new] * (nk // DA_VDIM), axis=1))
            alpha = jnp.exp(m_old - m_new)
            acc_ref[rows] = (jnp.concatenate([alpha, alpha], axis=1) * acc_ref[rows]
                             + jnp.dot(p.astype(BF16), v_ext[:nk], preferred_element_type=F32))
            m_ref[rows] = m_new

    @pl.when(ki < qi)
    def _():
        update(False)

    @pl.when(ki == qi)
    def _():
        update(True)
        o = acc_ref[:, :DA_VDIM] / acc_ref[:, DA_VDIM:]
        o = o[:tile] - lam_ref[0] * o[tile:]
        o_ref[0] = (_rms(o, g_ref[...]) * out_scale).astype(BF16)


def _diff_attention(q, k, v, lam, g_sub, lambda_init):
    b, l, _ = q.shape
    t = min(ATT_TILE, l)
    n = l // t
    pairs = [(qi, ki) for qi in range(n) for ki in range(qi + 1)]
    qi_tab = jnp.asarray([p[0] for p in pairs], I32)
    ki_tab = jnp.asarray([p[1] for p in pairs], I32)
    qspec = pl.BlockSpec((1, t, DA_VDIM), lambda bi, h, s, qt, kt: (bi, qt[s], h))
    kspec = pl.BlockSpec((1, t, DA_VDIM), lambda bi, h, s, qt, kt: (bi, kt[s], h))
    return pl.pallas_call(
        functools.partial(_attn_kernel, tile=t, out_scale=1.0 - lambda_init),
        grid_spec=pltpu.PrefetchScalarGridSpec(
            num_scalar_prefetch=2,
            grid=(b, DA_HEADS, len(pairs)),
            in_specs=[pl.BlockSpec(memory_space=pltpu.SMEM), qspec, kspec, kspec,
                      pl.BlockSpec((1, DA_VDIM), lambda bi, h, s, qt, kt: (0, 0))],
            out_specs=qspec,
            scratch_shapes=[pltpu.VMEM((2 * t, DA_VDIM), BF16), pltpu.VMEM((2 * t, DA_VDIM), F32),
                            pltpu.VMEM((2 * t, 2 * DA_VDIM), F32)]),
        out_shape=jax.ShapeDtypeStruct((b, l, ATTN_WIDTH), BF16),
        compiler_params=_params(("parallel", "parallel", "arbitrary")),
        name="diffattn",
    )(qi_tab, ki_tab, lam, q, k, v, g_sub)


def _memkv_kernel(m_ref, g_ref, wk_ref, wv_ref, k_ref, v_ref):
    a = _rms(m_ref[0], g_ref[...]).astype(BF16)
    k_ref[0] = jnp.dot(a, wk_ref[...], preferred_element_type=F32).astype(BF16)
    v_ref[0] = jnp.dot(a, wv_ref[...], preferred_element_type=F32).astype(BF16)


def _memkv(mem, g_mem, wk, wv):
    b, m, d = mem.shape
    blk = pl.BlockSpec((1, m, d), lambda i: (i, 0, 0))
    return pl.pallas_call(
        _memkv_kernel,
        grid=(b,),
        in_specs=[blk, _full((1, d)), _full((d, d)), _full((d, d))],
        out_specs=[blk, blk],
        out_shape=[jax.ShapeDtypeStruct((b, m, d), BF16)] * 2,
        compiler_params=_params(("parallel",)),
        name="memkv",
    )(mem, g_mem, wk, wv)


def _mix_kernel(x_ref, ys_ref, ya_ref, wo1_ref, wo2_ref, gx_ref, wq_ref, km_ref, vm_ref, wo_ref, h_ref):
    h = (x_ref[0]
         + jnp.dot(ys_ref[0], wo1_ref[...], preferred_element_type=F32)
         + jnp.dot(ya_ref[0], wo2_ref[...], preferred_element_type=F32))
    hq = _rms(h, gx_ref[...]).astype(BF16)
    q = jnp.dot(hq, wq_ref[...], preferred_element_type=F32).astype(BF16)
    outs = []
    for hd in range(X_HEADS):
        cols = slice(hd * X_HEAD_DIM, (hd + 1) * X_HEAD_DIM)
        s = lax.dot_general(q[:, cols], km_ref[0, :, cols], (((1,), (1,)), ((), ())),
                            preferred_element_type=F32) * (X_HEAD_DIM ** -0.5)
        s = s - jnp.max(s, axis=1, keepdims=True)
        p = jnp.exp(s)
        p = p / jnp.sum(p, axis=1, keepdims=True)
        outs.append(jnp.dot(p.astype(BF16), vm_ref[0, :, cols], preferred_element_type=F32).astype(BF16))
    o = jnp.concatenate(outs, axis=1)
    h_ref[0] = h + jnp.dot(o, wo_ref[...], preferred_element_type=F32)


def _mix(x, y_ssm, y_att, wo1, wo2, g_x, wq, kmem, vmem, wo):
    b, l, d = x.shape
    t = min(MIX_TILE, l)
    m = kmem.shape[1]
    row = lambda w: pl.BlockSpec((1, t, w), lambda i, j: (i, j, 0))
    mem = pl.BlockSpec((1, m, d), lambda i, j: (i, 0, 0))
    return pl.pallas_call(
        _mix_kernel,
        grid=(b, l // t),
        in_specs=[row(d), row(SSM_WIDTH), row(ATTN_WIDTH), _full((SSM_WIDTH, d)), _full((ATTN_WIDTH, d)),
                  _full((1, d)), _full((d, d)), mem, mem, _full((d, d))],
        out_specs=row(d),
        out_shape=jax.ShapeDtypeStruct((b, l, d), F32),
        compiler_params=_params(("parallel", "parallel")),
        name="mix",
    )(x, y_ssm, y_att, wo1, wo2, g_x, wq, kmem, vmem, wo)


def _first_index(hit, idx, sentinel):
    return jnp.min(jnp.where(hit, idx, sentinel), axis=0, keepdims=True)


def _route_kernel(h_ref, g_ref, wr_ref, bias_ref, wsgu_ref, wsd_ref, tri_ref,
                  t_ref, base_ref, eidx_ref, gw_ref, rank_ref, cnt_ref, carry_ref):
    @pl.when(pl.program_id(0) == 0)
    def _():
        carry_ref[...] = jnp.zeros_like(carry_ref)

    h = h_ref[...]
    t = _rms(h, g_ref[...])
    tb = t.astype(BF16)
    t_ref[...] = _pack_rows(t)
    gu = jnp.dot(tb, wsgu_ref[...], preferred_element_type=F32)
    hid = jax.nn.silu(gu[:, :D_EXPERT]) * gu[:, D_EXPERT:]
    base_ref[...] = h + jnp.dot(hid.astype(BF16), wsd_ref[...], preferred_element_type=F32)

    nt = (((1,), (1,)), ((), ()))
    t_lo = (t - tb.astype(F32)).astype(BF16)
    both = lax.dot_general(wr_ref[...], tb, nt, preferred_element_type=F32)
    logits = (both[:N_EXPERTS] + both[N_EXPERTS:]
              + lax.dot_general(wr_ref[:N_EXPERTS], t_lo, nt, preferred_element_type=F32))
    scores = jax.nn.sigmoid(logits)
    biased = scores + bias_ref[...]
    n_tok = scores.shape[1]
    neg = -jnp.inf
    sub = lax.broadcasted_iota(I32, (PER_GROUP, n_tok), 0)

    gs = []
    for g in range(N_EXPERT_GROUPS):
        blk = biased[g * PER_GROUP:(g + 1) * PER_GROUP]
        m1 = jnp.max(blk, axis=0, keepdims=True)
        i1 = _first_index(blk == m1, sub, PER_GROUP)
        m2 = jnp.max(jnp.where(sub == i1, neg, blk), axis=0, keepdims=True)
        gs.append(m1 + m2)
    gs = jnp.concatenate(gs, axis=0)

    gsel = jnp.zeros(gs.shape, jnp.bool_)
    for _ in range(TOPK_GROUPS):
        m = jnp.max(gs, axis=0, keepdims=True)
        hit = sub == _first_index(gs == m, sub, N_EXPERT_GROUPS)
        gsel = jnp.logical_or(gsel, hit)
        gs = jnp.where(hit, neg, gs)

    masked = jnp.concatenate(
        [jnp.where(gsel[g:g + 1], biased[g * PER_GROUP:(g + 1) * PER_GROUP], neg)
         for g in range(N_EXPERT_GROUPS)], axis=0)
    eid = lax.broadcasted_iota(I32, masked.shape, 0)
    sel = jnp.zeros(masked.shape, jnp.bool_)
    idxs, gws = [], []
    for _ in range(TOP_K):
        m = jnp.max(masked, axis=0, keepdims=True)
        i = _first_index(masked == m, eid, N_EXPERTS)
        hit = eid == i
        idxs.append(i)
        gws.append(jnp.sum(jnp.where(hit, scores, 0.0), axis=0, keepdims=True))
        sel = jnp.logical_or(sel, hit)
        masked = jnp.where(hit, neg, masked)
    eidx = jnp.concatenate(idxs, axis=0)
    gw = jnp.concatenate(gws, axis=0)
    gw = gw / jnp.sum(gw, axis=0, keepdims=True) * ROUTED_SCALE
    eidx_ref[...] = eidx
    gw_ref[...] = gw

    before = jnp.dot(sel.astype(BF16), tri_ref[...], preferred_element_type=F32) + carry_ref[...]
    rank_ref[...] = jnp.concatenate(
        [jnp.sum(jnp.where(eid == idxs[k], before, 0.0), axis=0, keepdims=True) for k in range(TOP_K)],
        axis=0).astype(I32)
    carry = carry_ref[...] + jnp.sum(sel.astype(F32), axis=1, keepdims=True)
    carry_ref[...] = carry
    cnt_ref[...] = carry.astype(I32)


def _route(h2, g_ffn, wr_t, bias_col, wsgu, wsd, part, n_parts):
    n_all, d = h2.shape
    n = n_all // n_parts
    t = min(ROUTE_TILE, n)
    blk0 = part * (n // t)
    tri = (lax.broadcasted_iota(I32, (t, t), 0) < lax.broadcasted_iota(I32, (t, t), 1)).astype(BF16)
    row = lambda w: pl.BlockSpec((t, w), lambda i: (i, 0))
    col = pl.BlockSpec((TOP_K, t), lambda i: (0, i))
    return pl.pallas_call(
        _route_kernel,
        grid=(n // t,),
        in_specs=[pl.BlockSpec((t, d), lambda i: (i + blk0, 0)), _full((1, d)), _full((2 * N_EXPERTS, d)),
                  _full((N_EXPERTS, 1)),
                  _full((d, 2 * D_EXPERT)), _full((D_EXPERT, d)), _full((t, t))],
        out_specs=[row(d // 2), row(d), col, col, col, _full((N_EXPERTS, 1))],
        out_shape=[jax.ShapeDtypeStruct((n, d // 2), I32), jax.ShapeDtypeStruct((n, d), F32),
                   jax.ShapeDtypeStruct((TOP_K, n), I32), jax.ShapeDtypeStruct((TOP_K, n), F32),
                   jax.ShapeDtypeStruct((TOP_K, n), I32), jax.ShapeDtypeStruct((N_EXPERTS, 1), I32)],
        scratch_shapes=[pltpu.VMEM((N_EXPERTS, 1), F32)],
        compiler_params=_params(("arbitrary",)),
        name="route",
    )(h2, g_ffn, wr_t, bias_col, wsgu, wsd, tri)


def _expert_kernel(be_ref, first_ref, slot_ref, next_ref, nu_ref, x_hbm, wg_hbm, wu_hbm, wd_hbm, y_ref,
                   wg_s, wu_s, wd_s, x_buf, x_sem, wg_buf, wu_buf, wd_buf, w_sem):
    i = pl.program_id(0)
    n_used = nu_ref[0]

    def fetch(step):
        slot = step % X_SLOTS
        rows = pl.ds(pl.multiple_of(step * MOE_TILE, MOE_TILE), MOE_TILE)
        return pltpu.make_async_copy(x_hbm.at[rows], x_buf.at[slot], x_sem.at[slot])

    def weight_fetch(expert, slot):
        pairs = ((wg_hbm, wg_buf), (wu_hbm, wu_buf), (wd_hbm, wd_buf))
        return [pltpu.make_async_copy(src.at[expert], buf.at[slot], w_sem.at[slot, j])
                for j, (src, buf) in enumerate(pairs)]

    @pl.when(i == 0)
    def _():
        for j in range(X_SLOTS - 1):
            @pl.when(j < n_used)
            def _(j=j):
                fetch(j).start()
        for cp in weight_fetch(be_ref[0], 0):
            cp.start()

    @pl.when(i + (X_SLOTS - 1) < n_used)
    def _():
        fetch(i + (X_SLOTS - 1)).start()

    @pl.when(first_ref[i] == 1)
    def _():
        slot = slot_ref[i]

        @pl.when(next_ref[i] >= 0)
        def _():
            for cp in weight_fetch(next_ref[i], 1 - slot):
                cp.start()

        for cp in weight_fetch(be_ref[i], slot):
            cp.wait()
        wg_s[...] = wg_buf[slot].astype(BF16)
        wu_s[...] = wu_buf[slot].astype(BF16)
        wd_s[...] = wd_buf[slot].astype(BF16)

    @pl.when(i < n_used)
    def _():
        fetch(i).wait()
        x = _unpack_rows(x_buf[i % X_SLOTS]).astype(BF16)
        gate = jnp.dot(x, wg_s[...], preferred_element_type=F32)
        up = jnp.dot(x, wu_s[...], preferred_element_type=F32)
        hid = (jax.nn.silu(gate) * up).astype(BF16)
        y_ref[...] = _pack_rows(jnp.dot(hid, wd_s[...], preferred_element_type=F32))


def _experts(schedule, xs, w_gate, w_up, w_down):
    block_e, run_first, run_slot, run_next, n_used = schedule
    n_slots, dw = xs.shape
    _, d, de = w_gate.shape
    nb = n_slots // MOE_TILE
    hbm = pl.BlockSpec(memory_space=pl.ANY)
    return pl.pallas_call(
        _expert_kernel,
        grid_spec=pltpu.PrefetchScalarGridSpec(
            num_scalar_prefetch=5,
            grid=(nb,),
            in_specs=[hbm, hbm, hbm, hbm],
            out_specs=pl.BlockSpec((MOE_TILE, dw), lambda i, be, fi, sl, nx, nu: (jnp.minimum(i, nu[0] - 1), 0)),
            scratch_shapes=[pltpu.VMEM((d, de), BF16), pltpu.VMEM((d, de), BF16), pltpu.VMEM((de, d), BF16),
                            pltpu.VMEM((X_SLOTS, MOE_TILE, dw), I32), pltpu.SemaphoreType.DMA((X_SLOTS,)),
                            pltpu.VMEM((2, d, de), F32), pltpu.VMEM((2, d, de), F32), pltpu.VMEM((2, de, d), F32),
                            pltpu.SemaphoreType.DMA((2, 3))]),
        out_shape=jax.ShapeDtypeStruct((n_slots, dw), I32),
        compiler_params=_params(("arbitrary",)),
        name="experts",
    )(block_e, run_first, run_slot, run_next, n_used, xs, w_gate, w_up, w_down)


def _sc_worker_id():
    return lax.axis_index("s") * SC_CORES + lax.axis_index("c")


def _sc_dispatch(t_rows, dest3, n_slots):
    _, dw = t_rows.shape
    n_chunks, _, w = dest3.shape
    per_worker = n_chunks // SC_WORKERS
    assert per_worker % 2 == 0
    mesh = plsc.VectorSubcoreMesh(core_axis_name="c", subcore_axis_name="s")
    dt = t_rows.dtype

    @functools.partial(
        pl.kernel, mesh=mesh,
        out_type=jax.ShapeDtypeStruct((n_slots, dw), dt),
        scratch_types=[pltpu.VMEM((TOP_K, w), I32), pltpu.VMEM((TOP_K, w), I32),
                       pltpu.VMEM((w, dw), dt), pltpu.VMEM((w, dw), dt),
                       pltpu.SemaphoreType.DMA, pltpu.SemaphoreType.DMA, pltpu.SemaphoreType.DMA],
    )
    def scatter_rows(t_hbm, dest_hbm, xs_hbm, idx_a, idx_b, rows_a, rows_b, sem_load, sem_a, sem_b):
        wid = _sc_worker_id()

        def scatter(idx_v, rows_v, sem):
            return [pltpu.async_copy(rows_v, xs_hbm.at[idx_v.at[k]], sem) for k in range(TOP_K)]

        @pl.loop(0, per_worker, step=2)
        def _(j):
            ca = wid * per_worker + j
            cb = ca + 1
            pltpu.sync_copy(dest_hbm.at[ca], idx_a)
            pltpu.sync_copy(t_hbm.at[pl.ds(ca * w, w)], rows_a)
            load_idx = pltpu.async_copy(dest_hbm.at[cb], idx_b, sem_load)
            load_rows = pltpu.async_copy(t_hbm.at[pl.ds(cb * w, w)], rows_b, sem_load)
            out_a = scatter(idx_a, rows_a, sem_a)
            load_idx.wait()
            load_rows.wait()
            out_b = scatter(idx_b, rows_b, sem_b)
            for cp in out_a + out_b:
                cp.wait()

    return scatter_rows(t_rows, dest3)


def _sc_combine(y_rows, dest3):
    _, dw = y_rows.shape
    n_chunks, _, w = dest3.shape
    per_worker = n_chunks // SC_WORKERS
    mesh = plsc.VectorSubcoreMesh(core_axis_name="c", subcore_axis_name="s")
    dt = y_rows.dtype

    @functools.partial(
        pl.kernel, mesh=mesh,
        out_type=jax.ShapeDtypeStruct((TOP_K, n_chunks * w, dw), dt),
        scratch_types=[pltpu.VMEM((TOP_K, w), I32), pltpu.VMEM((w, dw), dt), pltpu.VMEM((w, dw), dt),
                       pltpu.SemaphoreType.DMA, pltpu.SemaphoreType.DMA,
                       pltpu.SemaphoreType.DMA, pltpu.SemaphoreType.DMA],
    )
    def gather_rows(y_hbm, dest_hbm, out_hbm, idx_v, buf0, buf1, gsem0, gsem1, wsem0, wsem1):
        wid = _sc_worker_id()
        bufs, gsems, wsems = (buf0, buf1), (gsem0, gsem1), (wsem0, wsem1)

        @pl.loop(0, per_worker)
        def _(j):
            c = wid * per_worker + j
            pltpu.sync_copy(dest_hbm.at[c], idx_v)
            gathers = [None] * TOP_K
            writes = [None] * TOP_K
            gathers[0] = pltpu.async_copy(y_hbm.at[idx_v.at[0]], bufs[0], gsems[0])
            for k in range(TOP_K):
                b = k % 2
                if k + 1 < TOP_K:
                    if k >= 1:
                        writes[k - 1].wait()
                    gathers[k + 1] = pltpu.async_copy(y_hbm.at[idx_v.at[k + 1]], bufs[1 - b], gsems[1 - b])
                gathers[k].wait()
                writes[k] = pltpu.async_copy(bufs[b], out_hbm.at[k, pl.ds(c * w, w)], wsems[b])
            writes[TOP_K - 2].wait()
            writes[TOP_K - 1].wait()

    return gather_rows(y_rows, dest3)


def _final_kernel(base_ref, y_ref, gw_ref, g_ref, *rest, normalize):
    o_ref = rest[-1]
    h = base_ref[...]
    eye = (lax.broadcasted_iota(I32, (TOP_K, TOP_K), 0) == lax.broadcasted_iota(I32, (TOP_K, TOP_K), 1)).astype(F32)
    gw = lax.dot_general(gw_ref[...], eye, (((0,), (0,)), ((), ())), preferred_element_type=F32,
                         precision=lax.Precision.HIGHEST)
    for k in range(TOP_K):
        h = h + gw[:, k:k + 1] * _unpack_rows(y_ref[k])
    o_ref[...] = _rms(h, g_ref[...]) if normalize else h


def _final(base, yg, gw, g_final, normalize, out_prev, part, n_parts):
    n, d = base.shape
    t = min(FIN_TILE, n)
    blk0 = part * (n // t)
    in_specs = [pl.BlockSpec((t, d), lambda i: (i, 0)), pl.BlockSpec((TOP_K, t, d // 2), lambda i: (0, i, 0)),
                pl.BlockSpec((TOP_K, t), lambda i: (0, i)), _full((1, d))]
    args = [base, yg, gw, g_final]
    aliases = {}
    if out_prev is not None:
        in_specs.append(pl.BlockSpec(memory_space=pl.ANY))
        args.append(out_prev)
        aliases = {len(args) - 1: 0}
    return pl.pallas_call(
        functools.partial(_final_kernel, normalize=normalize),
        grid=(n // t,),
        in_specs=in_specs,
        out_specs=pl.BlockSpec((t, d), lambda i: (i + blk0, 0)),
        out_shape=jax.ShapeDtypeStruct((n * n_parts, d), F32),
        input_output_aliases=aliases,
        compiler_params=_params(("parallel",)),
        name="final",
    )(*args)


def _ssm_matrices(a_re, a_im, log_dt, b_re, b_im, c_re, c_im):
    lr, li = a_re.astype(F32), a_im.astype(F32)
    dt = jnp.exp(log_dt.astype(F32))[:, None]
    mag = jnp.exp(lr * dt)
    ab_re, ab_im = mag * jnp.cos(li * dt), mag * jnp.sin(li * dt)
    den = lr * lr + li * li
    zr, zi = ab_re - 1.0, ab_im
    k_re = (zr * lr + zi * li) / den
    k_im = (zi * lr - zr * li) / den
    br, bi = b_re.astype(F32), b_im.astype(F32)
    bb_re = k_re[..., None] * br - k_im[..., None] * bi
    bb_im = k_re[..., None] * bi + k_im[..., None] * br
    ar, ai = ab_re[..., None], ab_im[..., None]
    ab_b_re, ab_b_im = ar * bb_re - ai * bb_im, ar * bb_im + ai * bb_re
    cr, ci = c_re.astype(F32), c_im.astype(F32)
    car, cai = ab_re[:, None, :], ab_im[:, None, :]
    ca_re, ca_im = cr * car - ci * cai, cr * cai + ci * car
    direct = jnp.einsum("gcp,gpd->gdc", cr, bb_re) - jnp.einsum("gcp,gpd->gdc", ci, bb_im)
    n_slabs = SSM_WIDTH // SSM_CH
    per_slab = SSM_GROUPS // n_slabs
    eye = jnp.eye(per_slab, dtype=F32)

    def expand_blocks(bb):
        blocks = jnp.einsum("sgpc,gh->sgchp", bb.reshape(n_slabs, per_slab, SSM_STATE, SSM_GROUP), eye)
        return blocks.reshape(n_slabs, SSM_CH, SSM_SLAB)

    def contract_blocks(c):
        blocks = jnp.einsum("sgcp,gh->sgphc", c.reshape(n_slabs, per_slab, SSM_GROUP, SSM_STATE), eye)
        return blocks.reshape(n_slabs, SSM_SLAB, SSM_CH)

    bmat = jnp.concatenate([jnp.concatenate([expand_blocks(bb_re), expand_blocks(bb_im)], axis=2),
                            jnp.concatenate([expand_blocks(ab_b_re), expand_blocks(ab_b_im)], axis=2)], axis=1)
    cmat = jnp.concatenate([jnp.concatenate([contract_blocks(cr), contract_blocks(ca_re)], axis=2),
                            jnp.concatenate([-contract_blocks(ci), -contract_blocks(ca_im)], axis=2)], axis=1)
    dmat = jnp.einsum("sgdc,gh->sgdhc", direct.reshape(n_slabs, per_slab, SSM_GROUP, SSM_GROUP), eye)
    dmat = dmat.reshape(n_slabs, SSM_CH, SSM_CH)
    a2_re, a2_im = ab_re * ab_re - ab_im * ab_im, 2.0 * ab_re * ab_im
    tile = lambda a: jnp.broadcast_to(a.reshape(1, N_STATE), (SUBLANES, N_STATE))
    return bmat.astype(BF16), tile(a2_re), tile(a2_im), cmat.astype(BF16), dmat.astype(BF16)


def _plan_kernel(pstart_ref, eidx_ref, rank_ref, dest_ref, *, window):
    eidx = eidx_ref[...]
    dest = rank_ref[...]
    for e in range(N_EXPERTS):
        dest = dest + jnp.where(eidx == e, pstart_ref[e], 0)
    for c in range(dest.shape[1] // window):
        dest_ref[c] = dest[:, c * window:(c + 1) * window]


def _plan(pstart, eidx, rank, window):
    k, n = eidx.shape
    t = min(PLAN_TILE, n)
    cols = pl.BlockSpec((k, t), lambda i, ps: (0, i))
    return pl.pallas_call(
        functools.partial(_plan_kernel, window=window),
        grid_spec=pltpu.PrefetchScalarGridSpec(
            num_scalar_prefetch=1,
            grid=(n // t,),
            in_specs=[cols, cols],
            out_specs=pl.BlockSpec((t // window, k, window), lambda i, ps: (i, 0, 0))),
        out_shape=jax.ShapeDtypeStruct((n // window, k, window), I32),
        compiler_params=_params(("parallel",)),
        name="plan",
    )(pstart, eidx, rank)


def _block_schedule(counts, n_tok):
    padded = ((counts + MOE_TILE - 1) // MOE_TILE) * MOE_TILE
    pend = jnp.cumsum(padded)
    pstart = (pend - padded).astype(I32)
    n_slots = n_tok * TOP_K + N_EXPERTS * MOE_TILE
    nb = n_slots // MOE_TILE
    n_used = (pend[-1] // MOE_TILE).astype(I32)
    blk = jnp.arange(nb, dtype=I32)
    block_e = jnp.sum((pend[None, :] <= (jnp.minimum(blk, n_used - 1) * MOE_TILE)[:, None]).astype(I32), axis=1)
    block_e = jnp.minimum(block_e, N_EXPERTS - 1)
    eid = jnp.arange(N_EXPERTS, dtype=I32)
    active = padded > 0
    run_of = jnp.cumsum(active.astype(I32)) - 1
    later = jnp.where(jnp.logical_and(active[None, :], eid[None, :] > eid[:, None]), eid[None, :], N_EXPERTS)
    next_e = jnp.min(later, axis=1)
    next_e = jnp.where(next_e < N_EXPERTS, next_e, -1)
    onehot = block_e[:, None] == eid[None, :]
    lookup = lambda table: jnp.sum(jnp.where(onehot, table[None, :], 0), axis=1).astype(I32)
    run_first = jnp.logical_and(blk * MOE_TILE == lookup(pstart), blk < n_used).astype(I32)
    schedule = (block_e, run_first, lookup(run_of % 2), lookup(next_e), n_used.reshape(1))
    return pstart, schedule, n_slots


def kernel(x, mem, positions, g_mix, w_in, a_re, a_im, log_dt, b_re, b_im, c_re, c_im, d_skip, w_glu, g_ssm_out, lam_q1, lam_k1, lam_q2, lam_k2, g_sub, w_out, g_x, g_mem, wq_x, wk_x, wv_x, wo_x, g_ffn, w_router, router_bias, w_gate, w_up, w_down, ws_gate, ws_up, ws_down, g_final):
    b, l, d = x.shape
    n = b * l
    depth = w_in.shape[0]
    row = lambda a: a.reshape(1, -1).astype(F32)
    inv = ROPE_THETA ** (-jnp.arange(0, DA_QKDIM, 2, dtype=F32) / DA_QKDIM)
    inv_row = jnp.tile(inv, LANES // inv.shape[0]).reshape(1, LANES)
    pos3 = positions.reshape(b, l, 1)

    h = x
    for i in range(depth):
        lambda_init = 0.8 - 0.6 * math.exp(-0.3 * i)
        u, q, k, v = _inproj(h, pos3, row(g_mix[i]), w_in[i].astype(BF16), inv_row)

        ssm_mats = _ssm_matrices(a_re[i], a_im[i], log_dt[i], b_re[i], b_im[i], c_re[i], c_im[i])
        y_ssm = _ssm(u, ssm_mats, row(d_skip[i]), w_glu[i].astype(BF16), row(g_ssm_out[i]))

        lam = (jnp.exp(jnp.sum(lam_q1[i].astype(F32) * lam_k1[i].astype(F32)))
               - jnp.exp(jnp.sum(lam_q2[i].astype(F32) * lam_k2[i].astype(F32))) + lambda_init).reshape(1)
        y_att = _diff_attention(q, k, v, lam.astype(F32), row(g_sub[i]), lambda_init)

        kmem, vmem = _memkv(mem, row(g_mem[i]), wk_x[i].astype(BF16), wv_x[i].astype(BF16))
        wo = w_out[i].astype(BF16)
        h2 = _mix(h, y_ssm, y_att, wo[:SSM_WIDTH], wo[SSM_WIDTH:], row(g_x[i]), wq_x[i].astype(BF16),
                  kmem, vmem, wo_x[i].astype(BF16))

        wsgu = jnp.concatenate([ws_gate[i], ws_up[i]], axis=1).astype(BF16)
        wr = w_router[i].T.astype(F32)
        wr_hi = wr.astype(BF16)
        wr_split = jnp.concatenate([wr_hi, (wr - wr_hi.astype(F32)).astype(BF16)], axis=0)
        n_part = n // MOE_PARTS
        out = None
        for part in range(MOE_PARTS):
            t_pk, base, eidx, gw, rank, counts = _route(
                h2.reshape(n, d), row(g_ffn[i]), wr_split,
                router_bias[i].reshape(N_EXPERTS, 1).astype(F32), wsgu, ws_down[i].astype(BF16), part, MOE_PARTS)
            pstart, schedule, n_slots = _block_schedule(counts[:, 0], n_part)
            dest3 = _plan(pstart, eidx, rank, SC_WINDOW)
            xs = _sc_dispatch(t_pk, dest3, n_slots)
            ys = _experts(schedule, xs, w_gate[i], w_up[i], w_down[i])
            yg = _sc_combine(ys, dest3)
            out = _final(base, yg, gw, row(g_final), i == depth - 1, out, part, MOE_PARTS)
        h = out.reshape(b, l, d)
    return h
```

```python
import functools
import math

import jax
import jax.numpy as jnp
from jax import lax
from jax.experimental import pallas as pl
from jax.experimental.pallas import tpu as pltpu
from jax.experimental.pallas import tpu_sc as plsc

F32 = jnp.float32
BF16 = jnp.bfloat16
I32 = jnp.int32

SSM_WIDTH = 512
ATTN_WIDTH = 512
SSM_GROUP = 16
SSM_GROUPS = 32
SSM_STATE = 64
N_STATE = SSM_GROUPS * SSM_STATE
DA_HEADS = 4
DA_VDIM = 128
DA_QKDIM = 64
ROPE_THETA = 10000.0
ROPE_FREQS = DA_QKDIM // 2
X_HEADS = 4
X_HEAD_DIM = 256
N_EXPERTS = 64
TOP_K = 8
N_EXPERT_GROUPS = 8
PER_GROUP = N_EXPERTS // N_EXPERT_GROUPS
TOPK_GROUPS = 4
D_EXPERT = 256
ROUTED_SCALE = 2.5
EPS = 1e-6

VMEM_LIMIT_V7X = 56 * 1024 * 1024
SUBLANES = 8
LANES = 128
HALF_WORD_BITS = 16
HIGH_HALF_MASK = -(1 << HALF_WORD_BITS)

IN_TILE = 1024
SSM_STEPS = 128
SSM_COLS = 1024
SSM_CH = 128
SSM_SLAB = 512
ATT_TILE = 2048
ATT_ROWS = 256
MIX_TILE = 1024
ROUTE_TILE = 1024
MOE_TILE = 512
X_SLOTS = 3
PLAN_TILE = 2048
MOE_PARTS = 2
FIN_TILE = 512

SC_CORES = 2
SC_WORKERS = 32
SC_WINDOW = 64


def _params(sem):
    return pltpu.CompilerParams(dimension_semantics=sem, vmem_limit_bytes=VMEM_LIMIT_V7X)


def _rms(x, g):
    return x * lax.rsqrt(jnp.mean(x * x, axis=-1, keepdims=True) + EPS) * g


def _full(shape):
    return pl.BlockSpec(shape, lambda *_: (0,) * len(shape))


def _pack_rows(a):
    w = a.shape[1] // 2
    bits = lambda v: lax.bitcast_convert_type(v.astype(BF16).astype(F32), I32)
    return (bits(a[:, w:]) & jnp.int32(HIGH_HALF_MASK)) | lax.shift_right_logical(bits(a[:, :w]), HALF_WORD_BITS)


def _unpack_rows(p):
    lo = lax.bitcast_convert_type(lax.shift_left(p, HALF_WORD_BITS), F32)
    hi = lax.bitcast_convert_type(p & jnp.int32(HIGH_HALF_MASK), F32)
    return jnp.concatenate([lo, hi], axis=1)


def _inproj_kernel(x_ref, pos_ref, g_ref, w_ref, inv_ref, u_ref, q_ref, k_ref, v_ref):
    x = x_ref[0]
    a = _rms(x, g_ref[...]).astype(BF16)
    z = jnp.dot(a, w_ref[...], preferred_element_type=F32)
    u_ref[0] = z[:, :SSM_WIDTH]
    n_grp = LANES // ROPE_FREQS
    tq = x.shape[0] // n_grp
    pos = pos_ref[0].astype(F32)
    lane_c = lax.broadcasted_iota(I32, (tq, LANES), 1)
    pos_c = jnp.zeros((tq, LANES), F32)
    for j in range(n_grp):
        pos_c = jnp.where(lane_c // ROPE_FREQS == j, pos[j * tq:(j + 1) * tq], pos_c)
    ang = pos_c * inv_ref[...]

    def spread(c):
        rows = []
        for j in range(n_grp):
            y = c if j == 0 else pltpu.roll(c, LANES - ROPE_FREQS * j, 1)
            w = ROPE_FREQS
            while w < LANES:
                y = jnp.where(lane_c < w, y, pltpu.roll(y, w, 1))
                w *= 2
            rows.append(y)
        return jnp.concatenate(rows, axis=0)

    cos = jnp.concatenate([spread(jnp.cos(ang))] * (ATTN_WIDTH // LANES), axis=1)
    sin = jnp.concatenate([spread(jnp.sin(ang))] * (ATTN_WIDTH // LANES), axis=1)
    lane = lax.broadcasted_iota(I32, cos.shape, 1)
    first = (lane & (DA_QKDIM - 1)) < DA_QKDIM // 2
    half = DA_QKDIM // 2

    def rope(t):
        rot = jnp.where(first, -pltpu.roll(t, ATTN_WIDTH - half, 1), pltpu.roll(t, half, 1))
        return t * cos + rot * sin

    q = z[:, SSM_WIDTH:SSM_WIDTH + ATTN_WIDTH]
    k = z[:, SSM_WIDTH + ATTN_WIDTH:SSM_WIDTH + 2 * ATTN_WIDTH]
    q_ref[0] = (rope(q) * (DA_QKDIM ** -0.5)).astype(BF16)
    k_ref[0] = rope(k).astype(BF16)
    v_ref[0] = z[:, SSM_WIDTH + 2 * ATTN_WIDTH:].astype(BF16)


def _inproj(x, pos3, g_mix, w_in, inv_row):
    b, l, d = x.shape
    t = min(IN_TILE, l)
    n_out = w_in.shape[1]
    row = lambda w: pl.BlockSpec((1, t, w), lambda i, j: (i, j, 0))
    return pl.pallas_call(
        _inproj_kernel,
        grid=(b, l // t),
        in_specs=[row(d), row(1), _full((1, d)), _full((d, n_out)), _full((1, LANES))],
        out_specs=[row(SSM_WIDTH), row(ATTN_WIDTH), row(ATTN_WIDTH), row(ATTN_WIDTH)],
        out_shape=[jax.ShapeDtypeStruct((b, l, SSM_WIDTH), F32)]
        + [jax.ShapeDtypeStruct((b, l, ATTN_WIDTH), BF16)] * 3,
        compiler_params=_params(("parallel", "parallel")),
        name="inproj",
    )(x, pos3, g_mix, w_in, inv_row)


def _ssm_kernel(u_ref, bm_ref, are_ref, aim_ref, cm_ref, dm_ref, dskip_ref, wglu_ref, g_ref, o_ref,
                c_ref, st_ref, up_ref, *, steps):
    @pl.when(pl.program_id(0) == 0)
    def _():
        st_ref[...] = jnp.zeros_like(st_ref)
        up_ref[...] = jnp.zeros_like(up_ref)

    pairs = steps // 2
    rows = pairs * SUBLANES
    u = pltpu.einshape("bts->tbs", u_ref[...]).reshape(pairs, 2, SUBLANES, SSM_WIDTH)
    u_even = u[:, 0].reshape(rows, SSM_WIDTH)
    u_odd = u[:, 1].reshape(rows, SSM_WIDTH)
    u_before = jnp.concatenate([up_ref[...], u_odd[:rows - SUBLANES]], axis=0)
    up_ref[...] = u_odd[rows - SUBLANES:]
    ue, uo, ub = u_even.astype(BF16), u_odd.astype(BF16), u_before.astype(BF16)

    n_slabs = SSM_WIDTH // SSM_CH
    for s in range(n_slabs):
        ch = slice(s * SSM_CH, (s + 1) * SSM_CH)
        lhs = jnp.concatenate([ue[:, ch], ub[:, ch]], axis=1)
        for part in (0, 1):
            cols = slice(part * N_STATE + s * SSM_SLAB, part * N_STATE + (s + 1) * SSM_SLAB)
            c_ref[:, cols] = jnp.dot(lhs, bm_ref[s, :, part * SSM_SLAB:(part + 1) * SSM_SLAB],
                                     preferred_element_type=F32)

    for c0 in range(0, N_STATE, SSM_COLS):
        re = slice(c0, c0 + SSM_COLS)
        im = slice(N_STATE + c0, N_STATE + c0 + SSM_COLS)
        ar = are_ref[:, re]
        ai = aim_ref[:, re]
        sr, si = st_ref[:, re], st_ref[:, im]
        for k in range(pairs):
            blk = slice(k * SUBLANES, (k + 1) * SUBLANES)
            sr, si = ar * sr - ai * si + c_ref[blk, re], ar * si + ai * sr + c_ref[blk, im]
            c_ref[blk, re] = sr
            c_ref[blk, im] = si
        st_ref[:, re] = sr
        st_ref[:, im] = si

    y_even, y_odd = [], []
    for s in range(n_slabs):
        ch = slice(s * SSM_CH, (s + 1) * SSM_CH)
        acc = jnp.dot(uo[:, ch], dm_ref[s], preferred_element_type=F32)
        both = None
        for part in (0, 1):
            cols = slice(part * N_STATE + s * SSM_SLAB, part * N_STATE + (s + 1) * SSM_SLAB)
            d = jnp.dot(c_ref[:, cols].astype(BF16), cm_ref[s, part * SSM_SLAB:(part + 1) * SSM_SLAB],
                        preferred_element_type=F32)
            both = d if both is None else both + d
        y_even.append(both[:, :SSM_CH])
        y_odd.append(both[:, SSM_CH:] + acc)
    y = jnp.concatenate([jnp.concatenate(y_even, axis=1), jnp.concatenate(y_odd, axis=1)], axis=0)
    y = y + dskip_ref[...] * jnp.concatenate([u_even, u_odd], axis=0)
    y = jax.nn.gelu(y)
    y = y * jax.nn.sigmoid(jnp.dot(y.astype(BF16), wglu_ref[...], preferred_element_type=F32))
    out = _rms(y, g_ref[...])
    out = jnp.stack([out[:rows].reshape(pairs, SUBLANES, SSM_WIDTH),
                     out[rows:].reshape(pairs, SUBLANES, SSM_WIDTH)], axis=1).reshape(steps, SUBLANES, SSM_WIDTH)
    o_ref[...] = pltpu.einshape("tbs->bts", out).astype(BF16)


def _ssm(u, mats, d_skip, w_glu, g_out):
    bmat, a2_re8, a2_im8, cmat, dmat = mats
    batch, l, _ = u.shape
    assert batch == SUBLANES, "one time step of all sequences must fill the sublanes"
    steps = min(SSM_STEPS, l)
    assert steps % 2 == 0
    rows = steps // 2 * batch
    n_slabs = SSM_WIDTH // SSM_CH
    seq = pl.BlockSpec((batch, steps, SSM_WIDTH), lambda i: (0, i, 0))
    return pl.pallas_call(
        functools.partial(_ssm_kernel, steps=steps),
        grid=(l // steps,),
        in_specs=[seq,
                  _full((n_slabs, 2 * SSM_CH, 2 * SSM_SLAB)), _full((SUBLANES, N_STATE)), _full((SUBLANES, N_STATE)),
                  _full((n_slabs, 2 * SSM_SLAB, 2 * SSM_CH)), _full((n_slabs, SSM_CH, SSM_CH)),
                  _full((1, SSM_WIDTH)), _full((SSM_WIDTH, SSM_WIDTH)), _full((1, SSM_WIDTH))],
        out_specs=seq,
        out_shape=jax.ShapeDtypeStruct((batch, l, SSM_WIDTH), BF16),
        scratch_shapes=[pltpu.VMEM((rows, 2 * N_STATE), F32), pltpu.VMEM((SUBLANES, 2 * N_STATE), F32),
                        pltpu.VMEM((SUBLANES, SSM_WIDTH), F32)],
        compiler_params=_params(("arbitrary",)),
        name="ssm",
    )(u, bmat, a2_re8, a2_im8, cmat, dmat, d_skip, w_glu, g_out)


def _attn_kernel(qi_ref, ki_ref, lam_ref, q_ref, k_ref, v_ref, g_ref, o_ref, qs_ref, m_ref, acc_ref, *,
                 tile, out_scale):
    qi = qi_ref[pl.program_id(2)]
    ki = ki_ref[pl.program_id(2)]

    @pl.when(ki == 0)
    def _():
        q = q_ref[0]
        lane = lax.broadcasted_iota(I32, q.shape, 1)
        zero = jnp.zeros_like(q)
        qs_ref[:tile] = jnp.where(lane < DA_QKDIM, q, zero)
        qs_ref[tile:] = jnp.where(lane >= DA_QKDIM, q, zero)
        m_ref[...] = jnp.full_like(m_ref, -jnp.inf)
        acc_ref[...] = jnp.zeros_like(acc_ref)

    def update(masked):
        v_ext = jnp.concatenate([v_ref[0], jnp.ones((tile, DA_VDIM), BF16)], axis=1)
        rb = min(ATT_ROWS, tile)
        n_rb = 2 * tile // rb

        def n_keys(r):
            return (r * rb) % tile + rb if masked else tile

        def scores(r):
            return lax.dot_general(qs_ref[r * rb:(r + 1) * rb], k_ref[0, :n_keys(r)], (((1,), (1,)), ((), ())),
                                   preferred_element_type=F32)

        s_next = scores(0)
        for r in range(n_rb):
            s = s_next
            if r + 1 < n_rb:
                s_next = scores(r + 1)
            rows = slice(r * rb, (r + 1) * rb)
            nk = n_keys(r)
            if masked:
                tri = lax.broadcasted_iota(I32, (rb, rb), 1) <= lax.broadcasted_iota(I32, (rb, rb), 0)
                edge = jnp.where(tri, s[:, nk - rb:], jnp.finfo(F32).min)
                s = edge if nk == rb else jnp.concatenate([s[:, :nk - rb], edge], axis=1)
            m_old = m_ref[rows]
            m_new = jnp.maximum(m_old, jnp.max(s, axis=1, keepdims=True))
            p = jnp.exp(s - jnp.concatenate([m_new] * (nk // DA_VDIM), axis=1))
            alpha = jnp.exp(m_old - m_new)
            acc_ref[rows] = (jnp.concatenate([alpha, alpha], axis=1) * acc_ref[rows]
                             + jnp.dot(p.astype(BF16), v_ext[:nk], preferred_element_type=F32))
            m_ref[rows] = m_new

    @pl.when(ki < qi)
    def _():
        update(False)

    @pl.when(ki == qi)
    def _():
        update(True)
        o = acc_ref[:, :DA_VDIM] / acc_ref[:, DA_VDIM:]
        o = o[:tile] - lam_ref[0] * o[tile:]
        o_ref[0] = (_rms(o, g_ref[...]) * out_scale).astype(BF16)


def _diff_attention(q, k, v, lam, g_sub, lambda_init, b0, nb):
    _, l, _ = q.shape
    t = min(ATT_TILE, l)
    n = l // t
    pairs = [(qi, ki) for qi in range(n) for ki in range(qi + 1)]
    qi_tab = jnp.asarray([p[0] for p in pairs], I32)
    ki_tab = jnp.asarray([p[1] for p in pairs], I32)
    qspec = pl.BlockSpec((1, t, DA_VDIM), lambda bi, h, s, qt, kt: (bi + b0, qt[s], h))
    kspec = pl.BlockSpec((1, t, DA_VDIM), lambda bi, h, s, qt, kt: (bi + b0, kt[s], h))
    ospec = pl.BlockSpec((1, t, DA_VDIM), lambda bi, h, s, qt, kt: (bi, qt[s], h))
    return pl.pallas_call(
        functools.partial(_attn_kernel, tile=t, out_scale=1.0 - lambda_init),
        grid_spec=pltpu.PrefetchScalarGridSpec(
            num_scalar_prefetch=2,
            grid=(nb, DA_HEADS, len(pairs)),
            in_specs=[pl.BlockSpec(memory_space=pltpu.SMEM), qspec, kspec, kspec,
                      pl.BlockSpec((1, DA_VDIM), lambda bi, h, s, qt, kt: (0, 0))],
            out_specs=ospec,
            scratch_shapes=[pltpu.VMEM((2 * t, DA_VDIM), BF16), pltpu.VMEM((2 * t, DA_VDIM), F32),
                            pltpu.VMEM((2 * t, 2 * DA_VDIM), F32)]),
        out_shape=jax.ShapeDtypeStruct((nb, l, ATTN_WIDTH), BF16),
        compiler_params=_params(("parallel", "parallel", "arbitrary")),
        name="diffattn",
    )(qi_tab, ki_tab, lam, q, k, v, g_sub)


def _memkv_kernel(m_ref, g_ref, wk_ref, wv_ref, k_ref, v_ref):
    a = _rms(m_ref[0], g_ref[...]).astype(BF16)
    k_ref[0] = jnp.dot(a, wk_ref[...], preferred_element_type=F32).astype(BF16)
    v_ref[0] = jnp.dot(a, wv_ref[...], preferred_element_type=F32).astype(BF16)


def _memkv(mem, g_mem, wk, wv):
    b, m, d = mem.shape
    blk = pl.BlockSpec((1, m, d), lambda i: (i, 0, 0))
    return pl.pallas_call(
        _memkv_kernel,
        grid=(b,),
        in_specs=[blk, _full((1, d)), _full((d, d)), _full((d, d))],
        out_specs=[blk, blk],
        out_shape=[jax.ShapeDtypeStruct((b, m, d), BF16)] * 2,
        compiler_params=_params(("parallel",)),
        name="memkv",
    )(mem, g_mem, wk, wv)


def _mix_kernel(x_ref, ys_ref, ya_ref, wo1_ref, wo2_ref, gx_ref, wq_ref, km_ref, vm_ref, wo_ref, h_ref):
    h = (x_ref[0]
         + jnp.dot(ys_ref[0], wo1_ref[...], preferred_element_type=F32)
         + jnp.dot(ya_ref[0], wo2_ref[...], preferred_element_type=F32))
    hq = _rms(h, gx_ref[...]).astype(BF16)
    q = jnp.dot(hq, wq_ref[...], preferred_element_type=F32).astype(BF16)
    outs = []
    for hd in range(X_HEADS):
        cols = slice(hd * X_HEAD_DIM, (hd + 1) * X_HEAD_DIM)
        s = lax.dot_general(q[:, cols], km_ref[0, :, cols], (((1,), (1,)), ((), ())),
                            preferred_element_type=F32) * (X_HEAD_DIM ** -0.5)
        s = s - jnp.max(s, axis=1, keepdims=True)
        p = jnp.exp(s)
        p = p / jnp.sum(p, axis=1, keepdims=True)
        outs.append(jnp.dot(p.astype(BF16), vm_ref[0, :, cols], preferred_element_type=F32).astype(BF16))
    o = jnp.concatenate(outs, axis=1)
    h_ref[0] = h + jnp.dot(o, wo_ref[...], preferred_element_type=F32)


def _mix(x, y_ssm, y_att, wo1, wo2, g_x, wq, kmem, vmem, wo, b0):
    _, l, d = x.shape
    nb = y_att.shape[0]
    t = min(MIX_TILE, l)
    m = kmem.shape[1]
    row = lambda w: pl.BlockSpec((1, t, w), lambda i, j: (i, j, 0))
    full_row = lambda w: pl.BlockSpec((1, t, w), lambda i, j: (i + b0, j, 0))
    mem = pl.BlockSpec((1, m, d), lambda i, j: (i + b0, 0, 0))
    return pl.pallas_call(
        _mix_kernel,
        grid=(nb, l // t),
        in_specs=[full_row(d), full_row(SSM_WIDTH), row(ATTN_WIDTH), _full((SSM_WIDTH, d)), _full((ATTN_WIDTH, d)),
                  _full((1, d)), _full((d, d)), mem, mem, _full((d, d))],
        out_specs=row(d),
        out_shape=jax.ShapeDtypeStruct((nb, l, d), F32),
        compiler_params=_params(("parallel", "parallel")),
        name="mix",
    )(x, y_ssm, y_att, wo1, wo2, g_x, wq, kmem, vmem, wo)


def _first_index(hit, idx, sentinel):
    return jnp.min(jnp.where(hit, idx, sentinel), axis=0, keepdims=True)


def _route_kernel(h_ref, g_ref, wr_ref, bias_ref, wsgu_ref, wsd_ref, tri_ref,
                  t_ref, base_ref, eidx_ref, gw_ref, rank_ref, cnt_ref, carry_ref):
    @pl.when(pl.program_id(0) == 0)
    def _():
        carry_ref[...] = jnp.zeros_like(carry_ref)

    h = h_ref[...]
    t = _rms(h, g_ref[...])
    tb = t.astype(BF16)
    t_ref[...] = _pack_rows(t)
    gu = jnp.dot(tb, wsgu_ref[...], preferred_element_type=F32)
    hid = jax.nn.silu(gu[:, :D_EXPERT]) * gu[:, D_EXPERT:]
    base_ref[...] = h + jnp.dot(hid.astype(BF16), wsd_ref[...], preferred_element_type=F32)

    nt = (((1,), (1,)), ((), ()))
    t_lo = (t - tb.astype(F32)).astype(BF16)
    both = lax.dot_general(wr_ref[...], tb, nt, preferred_element_type=F32)
    logits = (both[:N_EXPERTS] + both[N_EXPERTS:]
              + lax.dot_general(wr_ref[:N_EXPERTS], t_lo, nt, preferred_element_type=F32))
    scores = jax.nn.sigmoid(logits)
    biased = scores + bias_ref[...]
    n_tok = scores.shape[1]
    neg = -jnp.inf
    sub = lax.broadcasted_iota(I32, (PER_GROUP, n_tok), 0)

    gs = []
    for g in range(N_EXPERT_GROUPS):
        blk = biased[g * PER_GROUP:(g + 1) * PER_GROUP]
        m1 = jnp.max(blk, axis=0, keepdims=True)
        i1 = _first_index(blk == m1, sub, PER_GROUP)
        m2 = jnp.max(jnp.where(sub == i1, neg, blk), axis=0, keepdims=True)
        gs.append(m1 + m2)
    gs = jnp.concatenate(gs, axis=0)

    gsel = jnp.zeros(gs.shape, jnp.bool_)
    for _ in range(TOPK_GROUPS):
        m = jnp.max(gs, axis=0, keepdims=True)
        hit = sub == _first_index(gs == m, sub, N_EXPERT_GROUPS)
        gsel = jnp.logical_or(gsel, hit)
        gs = jnp.where(hit, neg, gs)

    masked = jnp.concatenate(
        [jnp.where(gsel[g:g + 1], biased[g * PER_GROUP:(g + 1) * PER_GROUP], neg)
         for g in range(N_EXPERT_GROUPS)], axis=0)
    eid = lax.broadcasted_iota(I32, masked.shape, 0)
    sel = jnp.zeros(masked.shape, jnp.bool_)
    idxs, gws = [], []
    for _ in range(TOP_K):
        m = jnp.max(masked, axis=0, keepdims=True)
        i = _first_index(masked == m, eid, N_EXPERTS)
        hit = eid == i
        idxs.append(i)
        gws.append(jnp.sum(jnp.where(hit, scores, 0.0), axis=0, keepdims=True))
        sel = jnp.logical_or(sel, hit)
        masked = jnp.where(hit, neg, masked)
    eidx = jnp.concatenate(idxs, axis=0)
    gw = jnp.concatenate(gws, axis=0)
    gw = gw / jnp.sum(gw, axis=0, keepdims=True) * ROUTED_SCALE
    eidx_ref[...] = eidx
    gw_ref[...] = gw

    before = jnp.dot(sel.astype(BF16), tri_ref[...], preferred_element_type=F32) + carry_ref[...]
    rank_ref[...] = jnp.concatenate(
        [jnp.sum(jnp.where(eid == idxs[k], before, 0.0), axis=0, keepdims=True) for k in range(TOP_K)],
        axis=0).astype(I32)
    carry = carry_ref[...] + jnp.sum(sel.astype(F32), axis=1, keepdims=True)
    carry_ref[...] = carry
    cnt_ref[...] = carry.astype(I32)


def _route(h2, g_ffn, wr_t, bias_col, wsgu, wsd, part, n_parts):
    n_all, d = h2.shape
    n = n_all // n_parts
    t = min(ROUTE_TILE, n)
    blk0 = part * (n // t)
    tri = (lax.broadcasted_iota(I32, (t, t), 0) < lax.broadcasted_iota(I32, (t, t), 1)).astype(BF16)
    row = lambda w: pl.BlockSpec((t, w), lambda i: (i, 0))
    col = pl.BlockSpec((TOP_K, t), lambda i: (0, i))
    return pl.pallas_call(
        _route_kernel,
        grid=(n // t,),
        in_specs=[pl.BlockSpec((t, d), lambda i: (i + blk0, 0)), _full((1, d)), _full((2 * N_EXPERTS, d)),
                  _full((N_EXPERTS, 1)),
                  _full((d, 2 * D_EXPERT)), _full((D_EXPERT, d)), _full((t, t))],
        out_specs=[row(d // 2), row(d), col, col, col, _full((N_EXPERTS, 1))],
        out_shape=[jax.ShapeDtypeStruct((n, d // 2), I32), jax.ShapeDtypeStruct((n, d), F32),
                   jax.ShapeDtypeStruct((TOP_K, n), I32), jax.ShapeDtypeStruct((TOP_K, n), F32),
                   jax.ShapeDtypeStruct((TOP_K, n), I32), jax.ShapeDtypeStruct((N_EXPERTS, 1), I32)],
        scratch_shapes=[pltpu.VMEM((N_EXPERTS, 1), F32)],
        compiler_params=_params(("arbitrary",)),
        name="route",
    )(h2, g_ffn, wr_t, bias_col, wsgu, wsd, tri)


def _expert_kernel(be_ref, first_ref, slot_ref, next_ref, nu_ref, x_hbm, wg_hbm, wu_hbm, wd_hbm, y_ref,
                   wg_s, wu_s, wd_s, x_buf, x_sem, wg_buf, wu_buf, wd_buf, w_sem):
    i = pl.program_id(0)
    n_used = nu_ref[0]

    def fetch(step):
        slot = step % X_SLOTS
        rows = pl.ds(pl.multiple_of(step * MOE_TILE, MOE_TILE), MOE_TILE)
        return pltpu.make_async_copy(x_hbm.at[rows], x_buf.at[slot], x_sem.at[slot])

    def weight_fetch(expert, slot):
        pairs = ((wg_hbm, wg_buf), (wu_hbm, wu_buf), (wd_hbm, wd_buf))
        return [pltpu.make_async_copy(src.at[expert], buf.at[slot], w_sem.at[slot, j])
                for j, (src, buf) in enumerate(pairs)]

    @pl.when(i == 0)
    def _():
        for j in range(X_SLOTS - 1):
            @pl.when(j < n_used)
            def _(j=j):
                fetch(j).start()
        for cp in weight_fetch(be_ref[0], 0):
            cp.start()

    @pl.when(i + (X_SLOTS - 1) < n_used)
    def _():
        fetch(i + (X_SLOTS - 1)).start()

    @pl.when(first_ref[i] == 1)
    def _():
        slot = slot_ref[i]

        @pl.when(next_ref[i] >= 0)
        def _():
            for cp in weight_fetch(next_ref[i], 1 - slot):
                cp.start()

        for cp in weight_fetch(be_ref[i], slot):
            cp.wait()
        wg_s[...] = wg_buf[slot].astype(BF16)
        wu_s[...] = wu_buf[slot].astype(BF16)
        wd_s[...] = wd_buf[slot].astype(BF16)

    @pl.when(i < n_used)
    def _():
        fetch(i).wait()
        x = _unpack_rows(x_buf[i % X_SLOTS]).astype(BF16)
        gate = jnp.dot(x, wg_s[...], preferred_element_type=F32)
        up = jnp.dot(x, wu_s[...], preferred_element_type=F32)
        hid = (jax.nn.silu(gate) * up).astype(BF16)
        y_ref[...] = _pack_rows(jnp.dot(hid, wd_s[...], preferred_element_type=F32))


def _experts(schedule, xs, w_gate, w_up, w_down):
    block_e, run_first, run_slot, run_next, n_used = schedule
    n_slots, dw = xs.shape
    _, d, de = w_gate.shape
    nb = n_slots // MOE_TILE
    hbm = pl.BlockSpec(memory_space=pl.ANY)
    return pl.pallas_call(
        _expert_kernel,
        grid_spec=pltpu.PrefetchScalarGridSpec(
            num_scalar_prefetch=5,
            grid=(nb,),
            in_specs=[hbm, hbm, hbm, hbm],
            out_specs=pl.BlockSpec((MOE_TILE, dw), lambda i, be, fi, sl, nx, nu: (jnp.minimum(i, nu[0] - 1), 0)),
            scratch_shapes=[pltpu.VMEM((d, de), BF16), pltpu.VMEM((d, de), BF16), pltpu.VMEM((de, d), BF16),
                            pltpu.VMEM((X_SLOTS, MOE_TILE, dw), I32), pltpu.SemaphoreType.DMA((X_SLOTS,)),
                            pltpu.VMEM((2, d, de), F32), pltpu.VMEM((2, d, de), F32), pltpu.VMEM((2, de, d), F32),
                            pltpu.SemaphoreType.DMA((2, 3))]),
        out_shape=jax.ShapeDtypeStruct((n_slots, dw), I32),
        compiler_params=_params(("arbitrary",)),
        name="experts",
    )(block_e, run_first, run_slot, run_next, n_used, xs, w_gate, w_up, w_down)


def _sc_worker_id():
    return lax.axis_index("s") * SC_CORES + lax.axis_index("c")


def _sc_dispatch(t_rows, dest3, n_slots):
    _, dw = t_rows.shape
    n_chunks, _, w = dest3.shape
    per_worker = n_chunks // SC_WORKERS
    assert per_worker % 2 == 0
    mesh = plsc.VectorSubcoreMesh(core_axis_name="c", subcore_axis_name="s")
    dt = t_rows.dtype

    @functools.partial(
        pl.kernel, mesh=mesh,
        out_type=jax.ShapeDtypeStruct((n_slots, dw), dt),
        scratch_types=[pltpu.VMEM((TOP_K, w), I32), pltpu.VMEM((TOP_K, w), I32),
                       pltpu.VMEM((w, dw), dt), pltpu.VMEM((w, dw), dt),
                       pltpu.SemaphoreType.DMA, pltpu.SemaphoreType.DMA, pltpu.SemaphoreType.DMA],
    )
    def scatter_rows(t_hbm, dest_hbm, xs_hbm, idx_a, idx_b, rows_a, rows_b, sem_load, sem_a, sem_b):
        wid = _sc_worker_id()

        def scatter(idx_v, rows_v, sem):
            return [pltpu.async_copy(rows_v, xs_hbm.at[idx_v.at[k]], sem) for k in range(TOP_K)]

        @pl.loop(0, per_worker, step=2)
        def _(j):
            ca = wid * per_worker + j
            cb = ca + 1
            pltpu.sync_copy(dest_hbm.at[ca], idx_a)
            pltpu.sync_copy(t_hbm.at[pl.ds(ca * w, w)], rows_a)
            load_idx = pltpu.async_copy(dest_hbm.at[cb], idx_b, sem_load)
            load_rows = pltpu.async_copy(t_hbm.at[pl.ds(cb * w, w)], rows_b, sem_load)
            out_a = scatter(idx_a, rows_a, sem_a)
            load_idx.wait()
            load_rows.wait()
            out_b = scatter(idx_b, rows_b, sem_b)
            for cp in out_a + out_b:
                cp.wait()

    return scatter_rows(t_rows, dest3)


def _sc_combine(y_rows, dest3):
    _, dw = y_rows.shape
    n_chunks, _, w = dest3.shape
    per_worker = n_chunks // SC_WORKERS
    mesh = plsc.VectorSubcoreMesh(core_axis_name="c", subcore_axis_name="s")
    dt = y_rows.dtype

    @functools.partial(
        pl.kernel, mesh=mesh,
        out_type=jax.ShapeDtypeStruct((TOP_K, n_chunks * w, dw), dt),
        scratch_types=[pltpu.VMEM((TOP_K, w), I32), pltpu.VMEM((w, dw), dt), pltpu.VMEM((w, dw), dt),
                       pltpu.SemaphoreType.DMA, pltpu.SemaphoreType.DMA,
                       pltpu.SemaphoreType.DMA, pltpu.SemaphoreType.DMA],
    )
    def gather_rows(y_hbm, dest_hbm, out_hbm, idx_v, buf0, buf1, gsem0, gsem1, wsem0, wsem1):
        wid = _sc_worker_id()
        bufs, gsems, wsems = (buf0, buf1), (gsem0, gsem1), (wsem0, wsem1)

        @pl.loop(0, per_worker)
        def _(j):
            c = wid * per_worker + j
            pltpu.sync_copy(dest_hbm.at[c], idx_v)
            gathers = [None] * TOP_K
            writes = [None] * TOP_K
            gathers[0] = pltpu.async_copy(y_hbm.at[idx_v.at[0]], bufs[0], gsems[0])
            for k in range(TOP_K):
                b = k % 2
                if k + 1 < TOP_K:
                    if k >= 1:
                        writes[k - 1].wait()
                    gathers[k + 1] = pltpu.async_copy(y_hbm.at[idx_v.at[k + 1]], bufs[1 - b], gsems[1 - b])
                gathers[k].wait()
                writes[k] = pltpu.async_copy(bufs[b], out_hbm.at[k, pl.ds(c * w, w)], wsems[b])
            writes[TOP_K - 2].wait()
            writes[TOP_K - 1].wait()

    return gather_rows(y_rows, dest3)


def _final_kernel(base_ref, y_ref, gw_ref, g_ref, *rest, normalize):
    o_ref = rest[-1]
    h = base_ref[...]
    eye = (lax.broadcasted_iota(I32, (TOP_K, TOP_K), 0) == lax.broadcasted_iota(I32, (TOP_K, TOP_K), 1)).astype(F32)
    gw = lax.dot_general(gw_ref[...], eye, (((0,), (0,)), ((), ())), preferred_element_type=F32,
                         precision=lax.Precision.HIGHEST)
    for k in range(TOP_K):
        h = h + gw[:, k:k + 1] * _unpack_rows(y_ref[k])
    o_ref[...] = _rms(h, g_ref[...]) if normalize else h


def _final(base, yg, gw, g_final, normalize, out_prev, part, n_parts):
    n, d = base.shape
    t = min(FIN_TILE, n)
    blk0 = part * (n // t)
    in_specs = [pl.BlockSpec((t, d), lambda i: (i, 0)), pl.BlockSpec((TOP_K, t, d // 2), lambda i: (0, i, 0)),
                pl.BlockSpec((TOP_K, t), lambda i: (0, i)), _full((1, d))]
    args = [base, yg, gw, g_final]
    aliases = {}
    if out_prev is not None:
        in_specs.append(pl.BlockSpec(memory_space=pl.ANY))
        args.append(out_prev)
        aliases = {len(args) - 1: 0}
    return pl.pallas_call(
        functools.partial(_final_kernel, normalize=normalize),
        grid=(n // t,),
        in_specs=in_specs,
        out_specs=pl.BlockSpec((t, d), lambda i: (i + blk0, 0)),
        out_shape=jax.ShapeDtypeStruct((n * n_parts, d), F32),
        input_output_aliases=aliases,
        compiler_params=_params(("parallel",)),
        name="final",
    )(*args)


def _ssm_matrices(a_re, a_im, log_dt, b_re, b_im, c_re, c_im):
    lr, li = a_re.astype(F32), a_im.astype(F32)
    dt = jnp.exp(log_dt.astype(F32))[:, None]
    mag = jnp.exp(lr * dt)
    ab_re, ab_im = mag * jnp.cos(li * dt), mag * jnp.sin(li * dt)
    den = lr * lr + li * li
    zr, zi = ab_re - 1.0, ab_im
    k_re = (zr * lr + zi * li) / den
    k_im = (zi * lr - zr * li) / den
    br, bi = b_re.astype(F32), b_im.astype(F32)
    bb_re = k_re[..., None] * br - k_im[..., None] * bi
    bb_im = k_re[..., None] * bi + k_im[..., None] * br
    ar, ai = ab_re[..., None], ab_im[..., None]
    ab_b_re, ab_b_im = ar * bb_re - ai * bb_im, ar * bb_im + ai * bb_re
    cr, ci = c_re.astype(F32), c_im.astype(F32)
    car, cai = ab_re[:, None, :], ab_im[:, None, :]
    ca_re, ca_im = cr * car - ci * cai, cr * cai + ci * car
    direct = jnp.einsum("gcp,gpd->gdc", cr, bb_re) - jnp.einsum("gcp,gpd->gdc", ci, bb_im)
    n_slabs = SSM_WIDTH // SSM_CH
    per_slab = SSM_GROUPS // n_slabs
    eye = jnp.eye(per_slab, dtype=F32)

    def expand_blocks(bb):
        blocks = jnp.einsum("sgpc,gh->sgchp", bb.reshape(n_slabs, per_slab, SSM_STATE, SSM_GROUP), eye)
        return blocks.reshape(n_slabs, SSM_CH, SSM_SLAB)

    def contract_blocks(c):
        blocks = jnp.einsum("sgcp,gh->sgphc", c.reshape(n_slabs, per_slab, SSM_GROUP, SSM_STATE), eye)
        return blocks.reshape(n_slabs, SSM_SLAB, SSM_CH)

    bmat = jnp.concatenate([jnp.concatenate([expand_blocks(bb_re), expand_blocks(bb_im)], axis=2),
                            jnp.concatenate([expand_blocks(ab_b_re), expand_blocks(ab_b_im)], axis=2)], axis=1)
    cmat = jnp.concatenate([jnp.concatenate([contract_blocks(cr), contract_blocks(ca_re)], axis=2),
                            jnp.concatenate([-contract_blocks(ci), -contract_blocks(ca_im)], axis=2)], axis=1)
    dmat = jnp.einsum("sgdc,gh->sgdhc", direct.reshape(n_slabs, per_slab, SSM_GROUP, SSM_GROUP), eye)
    dmat = dmat.reshape(n_slabs, SSM_CH, SSM_CH)
    a2_re, a2_im = ab_re * ab_re - ab_im * ab_im, 2.0 * ab_re * ab_im
    tile = lambda a: jnp.broadcast_to(a.reshape(1, N_STATE), (SUBLANES, N_STATE))
    return bmat.astype(BF16), tile(a2_re), tile(a2_im), cmat.astype(BF16), dmat.astype(BF16)


def _plan_kernel(pstart_ref, eidx_ref, rank_ref, dest_ref, *, window):
    eidx = eidx_ref[...]
    dest = rank_ref[...]
    for e in range(N_EXPERTS):
        dest = dest + jnp.where(eidx == e, pstart_ref[e], 0)
    for c in range(dest.shape[1] // window):
        dest_ref[c] = dest[:, c * window:(c + 1) * window]


def _plan(pstart, eidx, rank, window):
    k, n = eidx.shape
    t = min(PLAN_TILE, n)
    cols = pl.BlockSpec((k, t), lambda i, ps: (0, i))
    return pl.pallas_call(
        functools.partial(_plan_kernel, window=window),
        grid_spec=pltpu.PrefetchScalarGridSpec(
            num_scalar_prefetch=1,
            grid=(n // t,),
            in_specs=[cols, cols],
            out_specs=pl.BlockSpec((t // window, k, window), lambda i, ps: (i, 0, 0))),
        out_shape=jax.ShapeDtypeStruct((n // window, k, window), I32),
        compiler_params=_params(("parallel",)),
        name="plan",
    )(pstart, eidx, rank)


def _block_schedule(counts, n_tok):
    padded = ((counts + MOE_TILE - 1) // MOE_TILE) * MOE_TILE
    pend = jnp.cumsum(padded)
    pstart = (pend - padded).astype(I32)
    n_slots = n_tok * TOP_K + N_EXPERTS * MOE_TILE
    nb = n_slots // MOE_TILE
    n_used = (pend[-1] // MOE_TILE).astype(I32)
    blk = jnp.arange(nb, dtype=I32)
    block_e = jnp.sum((pend[None, :] <= (jnp.minimum(blk, n_used - 1) * MOE_TILE)[:, None]).astype(I32), axis=1)
    block_e = jnp.minimum(block_e, N_EXPERTS - 1)
    eid = jnp.arange(N_EXPERTS, dtype=I32)
    active = padded > 0
    run_of = jnp.cumsum(active.astype(I32)) - 1
    later = jnp.where(jnp.logical_and(active[None, :], eid[None, :] > eid[:, None]), eid[None, :], N_EXPERTS)
    next_e = jnp.min(later, axis=1)
    next_e = jnp.where(next_e < N_EXPERTS, next_e, -1)
    onehot = block_e[:, None] == eid[None, :]
    lookup = lambda table: jnp.sum(jnp.where(onehot, table[None, :], 0), axis=1).astype(I32)
    run_first = jnp.logical_and(blk * MOE_TILE == lookup(pstart), blk < n_used).astype(I32)
    schedule = (block_e, run_first, lookup(run_of % 2), lookup(next_e), n_used.reshape(1))
    return pstart, schedule, n_slots


def kernel(x, mem, positions, g_mix, w_in, a_re, a_im, log_dt, b_re, b_im, c_re, c_im, d_skip, w_glu, g_ssm_out, lam_q1, lam_k1, lam_q2, lam_k2, g_sub, w_out, g_x, g_mem, wq_x, wk_x, wv_x, wo_x, g_ffn, w_router, router_bias, w_gate, w_up, w_down, ws_gate, ws_up, ws_down, g_final):
    b, l, d = x.shape
    n = b * l
    depth = w_in.shape[0]
    row = lambda a: a.reshape(1, -1).astype(F32)
    inv = ROPE_THETA ** (-jnp.arange(0, DA_QKDIM, 2, dtype=F32) / DA_QKDIM)
    inv_row = jnp.tile(inv, LANES // inv.shape[0]).reshape(1, LANES)
    pos3 = positions.reshape(b, l, 1)

    h = x
    for i in range(depth):
        lambda_init = 0.8 - 0.6 * math.exp(-0.3 * i)
        u, q, k, v = _inproj(h, pos3, row(g_mix[i]), w_in[i].astype(BF16), inv_row)

        ssm_mats = _ssm_matrices(a_re[i], a_im[i], log_dt[i], b_re[i], b_im[i], c_re[i], c_im[i])
        y_ssm = _ssm(u, ssm_mats, row(d_skip[i]), w_glu[i].astype(BF16), row(g_ssm_out[i]))

        lam = (jnp.exp(jnp.sum(lam_q1[i].astype(F32) * lam_k1[i].astype(F32)))
               - jnp.exp(jnp.sum(lam_q2[i].astype(F32) * lam_k2[i].astype(F32))) + lambda_init).reshape(1)
        kmem, vmem = _memkv(mem, row(g_mem[i]), wk_x[i].astype(BF16), wv_x[i].astype(BF16))
        wo = w_out[i].astype(BF16)
        wsgu = jnp.concatenate([ws_gate[i], ws_up[i]], axis=1).astype(BF16)
        wr = w_router[i].T.astype(F32)
        wr_hi = wr.astype(BF16)
        wr_split = jnp.concatenate([wr_hi, (wr - wr_hi.astype(F32)).astype(BF16)], axis=0)

        nb = b // MOE_PARTS
        n_part = nb * l

        def attend(part):
            return _diff_attention(q, k, v, lam.astype(F32), row(g_sub[i]), lambda_init, part * nb, nb)

        def route_and_dispatch(part, y_att):
            h2 = _mix(h, y_ssm, y_att, wo[:SSM_WIDTH], wo[SSM_WIDTH:], row(g_x[i]), wq_x[i].astype(BF16),
                      kmem, vmem, wo_x[i].astype(BF16), part * nb)
            t_pk, base, eidx, gw, rank, counts = _route(
                h2.reshape(n_part, d), row(g_ffn[i]), wr_split,
                router_bias[i].reshape(N_EXPERTS, 1).astype(F32), wsgu, ws_down[i].astype(BF16), 0, 1)
            pstart, schedule, n_slots = _block_schedule(counts[:, 0], n_part)
            dest3 = _plan(pstart, eidx, rank, SC_WINDOW)
            xs = _sc_dispatch(t_pk, dest3, n_slots)
            return base, gw, schedule, dest3, xs

        def experts_and_combine(state):
            base, gw, schedule, dest3, xs = state
            ys = _experts(schedule, xs, w_gate[i], w_up[i], w_down[i])
            return base, gw, _sc_combine(ys, dest3)

        assert MOE_PARTS == 2
        state_a = route_and_dispatch(0, attend(0))
        y_att_b = attend(1)
        done_a = experts_and_combine(state_a)
        state_b = route_and_dispatch(1, y_att_b)
        out = _final(done_a[0], done_a[2], done_a[1], row(g_final), i == depth - 1, None, 0, MOE_PARTS)
        done_b = experts_and_combine(state_b)
        out = _final(done_b[0], done_b[2], done_b[1], row(g_final), i == depth - 1, out, 1, MOE_PARTS)
        h = out.reshape(b, l, d)
    return h
```

```python
import functools
import math

import jax
import jax.numpy as jnp
from jax import lax
from jax.experimental import pallas as pl
from jax.experimental.pallas import tpu as pltpu
from jax.experimental.pallas import tpu_sc as plsc

F32 = jnp.float32
BF16 = jnp.bfloat16
I32 = jnp.int32

SSM_WIDTH = 512
ATTN_WIDTH = 512
SSM_GROUP = 16
SSM_GROUPS = 32
SSM_STATE = 64
N_STATE = SSM_GROUPS * SSM_STATE
DA_HEADS = 4
DA_VDIM = 128
DA_QKDIM = 64
ROPE_THETA = 10000.0
ROPE_FREQS = DA_QKDIM // 2
X_HEADS = 4
X_HEAD_DIM = 256
N_EXPERTS = 64
TOP_K = 8
N_EXPERT_GROUPS = 8
PER_GROUP = N_EXPERTS // N_EXPERT_GROUPS
TOPK_GROUPS = 4
D_EXPERT = 256
ROUTED_SCALE = 2.5
EPS = 1e-6

VMEM_LIMIT_V7X = 56 * 1024 * 1024
SUBLANES = 8
LANES = 128
HALF_WORD_BITS = 16
HIGH_HALF_MASK = -(1 << HALF_WORD_BITS)

IN_TILE = 1024
SSM_STEPS = 128
SSM_COLS = 1024
SSM_CH = 128
SSM_SLAB = 512
ATT_TILE = 2048
ATT_ROWS = 256
MIX_TILE = 1024
ROUTE_TILE = 1024
MOE_TILE = 512
X_SLOTS = 3
PLAN_TILE = 2048
MOE_PARTS = 2
FIN_TILE = 512

SC_CORES = 2
SC_WORKERS = 32
SC_WINDOW = 64


def _params(sem):
    return pltpu.CompilerParams(dimension_semantics=sem, vmem_limit_bytes=VMEM_LIMIT_V7X)


def _rms(x, g):
    return x * lax.rsqrt(jnp.mean(x * x, axis=-1, keepdims=True) + EPS) * g


def _full(shape):
    return pl.BlockSpec(shape, lambda *_: (0,) * len(shape))


def _pack_rows(a):
    w = a.shape[1] // 2
    bits = lambda v: lax.bitcast_convert_type(v.astype(BF16).astype(F32), I32)
    return (bits(a[:, w:]) & jnp.int32(HIGH_HALF_MASK)) | lax.shift_right_logical(bits(a[:, :w]), HALF_WORD_BITS)


def _unpack_rows(p):
    lo = lax.bitcast_convert_type(lax.shift_left(p, HALF_WORD_BITS), F32)
    hi = lax.bitcast_convert_type(p & jnp.int32(HIGH_HALF_MASK), F32)
    return jnp.concatenate([lo, hi], axis=1)


def _inproj_kernel(x_ref, pos_ref, g_ref, w_ref, inv_ref, u_ref, q_ref, k_ref, v_ref):
    x = x_ref[0]
    a = _rms(x, g_ref[...]).astype(BF16)
    z = jnp.dot(a, w_ref[...], preferred_element_type=F32)
    u_ref[0] = z[:, :SSM_WIDTH]
    n_grp = LANES // ROPE_FREQS
    tq = x.shape[0] // n_grp
    pos = pos_ref[0].astype(F32)
    lane_c = lax.broadcasted_iota(I32, (tq, LANES), 1)
    pos_c = jnp.zeros((tq, LANES), F32)
    for j in range(n_grp):
        pos_c = jnp.where(lane_c // ROPE_FREQS == j, pos[j * tq:(j + 1) * tq], pos_c)
    ang = pos_c * inv_ref[...]

    def spread(c):
        rows = []
        for j in range(n_grp):
            y = c if j == 0 else pltpu.roll(c, LANES - ROPE_FREQS * j, 1)
            w = ROPE_FREQS
            while w < LANES:
                y = jnp.where(lane_c < w, y, pltpu.roll(y, w, 1))
                w *= 2
            rows.append(y)
        return jnp.concatenate(rows, axis=0)

    cos = jnp.concatenate([spread(jnp.cos(ang))] * (ATTN_WIDTH // LANES), axis=1)
    sin = jnp.concatenate([spread(jnp.sin(ang))] * (ATTN_WIDTH // LANES), axis=1)
    lane = lax.broadcasted_iota(I32, cos.shape, 1)
    first = (lane & (DA_QKDIM - 1)) < DA_QKDIM // 2
    half = DA_QKDIM // 2

    def rope(t):
        rot = jnp.where(first, -pltpu.roll(t, ATTN_WIDTH - half, 1), pltpu.roll(t, half, 1))
        return t * cos + rot * sin

    q = z[:, SSM_WIDTH:SSM_WIDTH + ATTN_WIDTH]
    k = z[:, SSM_WIDTH + ATTN_WIDTH:SSM_WIDTH + 2 * ATTN_WIDTH]
    q_ref[0] = (rope(q) * (DA_QKDIM ** -0.5)).astype(BF16)
    k_ref[0] = rope(k).astype(BF16)
    v_ref[0] = z[:, SSM_WIDTH + 2 * ATTN_WIDTH:].astype(BF16)


def _inproj(x, pos3, g_mix, w_in, inv_row):
    b, l, d = x.shape
    t = min(IN_TILE, l)
    n_out = w_in.shape[1]
    row = lambda w: pl.BlockSpec((1, t, w), lambda i, j: (i, j, 0))
    return pl.pallas_call(
        _inproj_kernel,
        grid=(b, l // t),
        in_specs=[row(d), row(1), _full((1, d)), _full((d, n_out)), _full((1, LANES))],
        out_specs=[row(SSM_WIDTH), row(ATTN_WIDTH), row(ATTN_WIDTH), row(ATTN_WIDTH)],
        out_shape=[jax.ShapeDtypeStruct((b, l, SSM_WIDTH), F32)]
        + [jax.ShapeDtypeStruct((b, l, ATTN_WIDTH), BF16)] * 3,
        compiler_params=_params(("parallel", "parallel")),
        name="inproj",
    )(x, pos3, g_mix, w_in, inv_row)


def _ssm_kernel(u_ref, bm_ref, are_ref, aim_ref, cm_ref, dm_ref, dskip_ref, wglu_ref, g_ref, o_ref,
                c_ref, st_ref, up_ref, *, steps):
    @pl.when(pl.program_id(0) == 0)
    def _():
        st_ref[...] = jnp.zeros_like(st_ref)
        up_ref[...] = jnp.zeros_like(up_ref)

    pairs = steps // 2
    rows = pairs * SUBLANES
    u = pltpu.einshape("bts->tbs", u_ref[...]).reshape(pairs, 2, SUBLANES, SSM_WIDTH)
    u_even = u[:, 0].reshape(rows, SSM_WIDTH)
    u_odd = u[:, 1].reshape(rows, SSM_WIDTH)
    u_before = jnp.concatenate([up_ref[...], u_odd[:rows - SUBLANES]], axis=0)
    up_ref[...] = u_odd[rows - SUBLANES:]
    ue, uo, ub = u_even.astype(BF16), u_odd.astype(BF16), u_before.astype(BF16)

    n_slabs = SSM_WIDTH // SSM_CH
    for s in range(n_slabs):
        ch = slice(s * SSM_CH, (s + 1) * SSM_CH)
        lhs = jnp.concatenate([ue[:, ch], ub[:, ch]], axis=1)
        for part in (0, 1):
            cols = slice(part * N_STATE + s * SSM_SLAB, part * N_STATE + (s + 1) * SSM_SLAB)
            c_ref[:, cols] = jnp.dot(lhs, bm_ref[s, :, part * SSM_SLAB:(part + 1) * SSM_SLAB],
                                     preferred_element_type=F32)

    for c0 in range(0, N_STATE, SSM_COLS):
        re = slice(c0, c0 + SSM_COLS)
        im = slice(N_STATE + c0, N_STATE + c0 + SSM_COLS)
        ar = are_ref[:, re]
        ai = aim_ref[:, re]
        sr, si = st_ref[:, re], st_ref[:, im]
        for k in range(pairs):
            blk = slice(k * SUBLANES, (k + 1) * SUBLANES)
            sr, si = ar * sr - ai * si + c_ref[blk, re], ar * si + ai * sr + c_ref[blk, im]
            c_ref[blk, re] = sr
            c_ref[blk, im] = si
        st_ref[:, re] = sr
        st_ref[:, im] = si

    y_even, y_odd = [], []
    for s in range(n_slabs):
        ch = slice(s * SSM_CH, (s + 1) * SSM_CH)
        acc = jnp.dot(uo[:, ch], dm_ref[s], preferred_element_type=F32)
        both = None
        for part in (0, 1):
            cols = slice(part * N_STATE + s * SSM_SLAB, part * N_STATE + (s + 1) * SSM_SLAB)
            d = jnp.dot(c_ref[:, cols].astype(BF16), cm_ref[s, part * SSM_SLAB:(part + 1) * SSM_SLAB],
                        preferred_element_type=F32)
            both = d if both is None else both + d
        y_even.append(both[:, :SSM_CH])
        y_odd.append(both[:, SSM_CH:] + acc)
    y = jnp.concatenate([jnp.concatenate(y_even, axis=1), jnp.concatenate(y_odd, axis=1)], axis=0)
    y = y + dskip_ref[...] * jnp.concatenate([u_even, u_odd], axis=0)
    y = jax.nn.gelu(y)
    y = y * jax.nn.sigmoid(jnp.dot(y.astype(BF16), wglu_ref[...], preferred_element_type=F32))
    out = _rms(y, g_ref[...])
    out = jnp.stack([out[:rows].reshape(pairs, SUBLANES, SSM_WIDTH),
                     out[rows:].reshape(pairs, SUBLANES, SSM_WIDTH)], axis=1).reshape(steps, SUBLANES, SSM_WIDTH)
    o_ref[...] = pltpu.einshape("tbs->bts", out).astype(BF16)


def _ssm(u, mats, d_skip, w_glu, g_out):
    bmat, a2_re8, a2_im8, cmat, dmat = mats
    batch, l, _ = u.shape
    assert batch == SUBLANES, "one time step of all sequences must fill the sublanes"
    steps = min(SSM_STEPS, l)
    assert steps % 2 == 0
    rows = steps // 2 * batch
    n_slabs = SSM_WIDTH // SSM_CH
    seq = pl.BlockSpec((batch, steps, SSM_WIDTH), lambda i: (0, i, 0))
    return pl.pallas_call(
        functools.partial(_ssm_kernel, steps=steps),
        grid=(l // steps,),
        in_specs=[seq,
                  _full((n_slabs, 2 * SSM_CH, 2 * SSM_SLAB)), _full((SUBLANES, N_STATE)), _full((SUBLANES, N_STATE)),
                  _full((n_slabs, 2 * SSM_SLAB, 2 * SSM_CH)), _full((n_slabs, SSM_CH, SSM_CH)),
                  _full((1, SSM_WIDTH)), _full((SSM_WIDTH, SSM_WIDTH)), _full((1, SSM_WIDTH))],
        out_specs=seq,
        out_shape=jax.ShapeDtypeStruct((batch, l, SSM_WIDTH), BF16),
        scratch_shapes=[pltpu.VMEM((rows, 2 * N_STATE), F32), pltpu.VMEM((SUBLANES, 2 * N_STATE), F32),
                        pltpu.VMEM((SUBLANES, SSM_WIDTH), F32)],
        compiler_params=_params(("arbitrary",)),
        name="ssm",
    )(u, bmat, a2_re8, a2_im8, cmat, dmat, d_skip, w_glu, g_out)


def _attn_kernel(qi_ref, ki_ref, lam_ref, q_ref, k_ref, v_ref, g_ref, o_ref, qs_ref, m_ref, acc_ref, *,
                 tile, out_scale):
    qi = qi_ref[pl.program_id(2)]
    ki = ki_ref[pl.program_id(2)]

    @pl.when(ki == 0)
    def _():
        q = q_ref[0]
        lane = lax.broadcasted_iota(I32, q.shape, 1)
        zero = jnp.zeros_like(q)
        qs_ref[:tile] = jnp.where(lane < DA_QKDIM, q, zero)
        qs_ref[tile:] = jnp.where(lane >= DA_QKDIM, q, zero)
        m_ref[...] = jnp.full_like(m_ref, -jnp.inf)
        acc_ref[...] = jnp.zeros_like(acc_ref)

    def update(masked):
        v_ext = jnp.concatenate([v_ref[0], jnp.ones((tile, DA_VDIM), BF16)], axis=1)
        rb = min(ATT_ROWS, tile)
        n_rb = 2 * tile // rb

        def n_keys(r):
            return (r * rb) % tile + rb if masked else tile

        def scores(r):
            return lax.dot_general(qs_ref[r * rb:(r + 1) * rb], k_ref[0, :n_keys(r)], (((1,), (1,)), ((), ())),
                                   preferred_element_type=F32)

        s_next = scores(0)
        for r in range(n_rb):
            s = s_next
            if r + 1 < n_rb:
                s_next = scores(r + 1)
            rows = slice(r * rb, (r + 1) * rb)
            nk = n_keys(r)
            if masked:
                tri = lax.broadcasted_iota(I32, (rb, rb), 1) <= lax.broadcasted_iota(I32, (rb, rb), 0)
                edge = jnp.where(tri, s[:, nk - rb:], jnp.finfo(F32).min)
                s = edge if nk == rb else jnp.concatenate([s[:, :nk - rb], edge], axis=1)
            m_old = m_ref[rows]
            m_new = jnp.maximum(m_old, jnp.max(s, axis=1, keepdims=True))
            p = jnp.exp(s - jnp.concatenate([m_new] * (nk // DA_VDIM), axis=1))
            alpha = jnp.exp(m_old - m_new)
            acc_ref[rows] = (jnp.concatenate([alpha, alpha], axis=1) * acc_ref[rows]
                             + jnp.dot(p.astype(BF16), v_ext[:nk], preferred_element_type=F32))
            m_ref[rows] = m_new

    @pl.when(ki < qi)
    def _():
        update(False)

    @pl.when(ki == qi)
    def _():
        update(True)
        o = acc_ref[:, :DA_VDIM] / acc_ref[:, DA_VDIM:]
        o = o[:tile] - lam_ref[0] * o[tile:]
        o_ref[0] = (_rms(o, g_ref[...]) * out_scale).astype(BF16)


def _diff_attention(q, k, v, lam, g_sub, lambda_init, b0, nb):
    _, l, _ = q.shape
    t = min(ATT_TILE, l)
    n = l // t
    pairs = [(qi, ki) for qi in range(n) for ki in range(qi + 1)]
    qi_tab = jnp.asarray([p[0] for p in pairs], I32)
    ki_tab = jnp.asarray([p[1] for p in pairs], I32)
    qspec = pl.BlockSpec((1, t, DA_VDIM), lambda bi, h, s, qt, kt: (bi + b0, qt[s], h))
    kspec = pl.BlockSpec((1, t, DA_VDIM), lambda bi, h, s, qt, kt: (bi + b0, kt[s], h))
    ospec = pl.BlockSpec((1, t, DA_VDIM), lambda bi, h, s, qt, kt: (bi, qt[s], h))
    return pl.pallas_call(
        functools.partial(_attn_kernel, tile=t, out_scale=1.0 - lambda_init),
        grid_spec=pltpu.PrefetchScalarGridSpec(
            num_scalar_prefetch=2,
            grid=(nb, DA_HEADS, len(pairs)),
            in_specs=[pl.BlockSpec(memory_space=pltpu.SMEM), qspec, kspec, kspec,
                      pl.BlockSpec((1, DA_VDIM), lambda bi, h, s, qt, kt: (0, 0))],
            out_specs=ospec,
            scratch_shapes=[pltpu.VMEM((2 * t, DA_VDIM), BF16), pltpu.VMEM((2 * t, DA_VDIM), F32),
                            pltpu.VMEM((2 * t, 2 * DA_VDIM), F32)]),
        out_shape=jax.ShapeDtypeStruct((nb, l, ATTN_WIDTH), BF16),
        compiler_params=_params(("parallel", "parallel", "arbitrary")),
        name="diffattn",
    )(qi_tab, ki_tab, lam, q, k, v, g_sub)


def _memkv_kernel(m_ref, g_ref, wk_ref, wv_ref, k_ref, v_ref):
    a = _rms(m_ref[0], g_ref[...]).astype(BF16)
    k_ref[0] = jnp.dot(a, wk_ref[...], preferred_element_type=F32).astype(BF16)
    v_ref[0] = jnp.dot(a, wv_ref[...], preferred_element_type=F32).astype(BF16)


def _memkv(mem, g_mem, wk, wv):
    b, m, d = mem.shape
    blk = pl.BlockSpec((1, m, d), lambda i: (i, 0, 0))
    return pl.pallas_call(
        _memkv_kernel,
        grid=(b,),
        in_specs=[blk, _full((1, d)), _full((d, d)), _full((d, d))],
        out_specs=[blk, blk],
        out_shape=[jax.ShapeDtypeStruct((b, m, d), BF16)] * 2,
        compiler_params=_params(("parallel",)),
        name="memkv",
    )(mem, g_mem, wk, wv)


def _mix_kernel(x_ref, ys_ref, ya_ref, wo1_ref, wo2_ref, gx_ref, wq_ref, km_ref, vm_ref, wo_ref, *rest):
    h_ref = rest[-1]
    h = (x_ref[0]
         + jnp.dot(ys_ref[0], wo1_ref[...], preferred_element_type=F32)
         + jnp.dot(ya_ref[0], wo2_ref[...], preferred_element_type=F32))
    hq = _rms(h, gx_ref[...]).astype(BF16)
    q = jnp.dot(hq, wq_ref[...], preferred_element_type=F32).astype(BF16)
    outs = []
    for hd in range(X_HEADS):
        cols = slice(hd * X_HEAD_DIM, (hd + 1) * X_HEAD_DIM)
        s = lax.dot_general(q[:, cols], km_ref[0, :, cols], (((1,), (1,)), ((), ())),
                            preferred_element_type=F32) * (X_HEAD_DIM ** -0.5)
        s = s - jnp.max(s, axis=1, keepdims=True)
        p = jnp.exp(s)
        p = p / jnp.sum(p, axis=1, keepdims=True)
        outs.append(jnp.dot(p.astype(BF16), vm_ref[0, :, cols], preferred_element_type=F32).astype(BF16))
    o = jnp.concatenate(outs, axis=1)
    h_ref[0] = h + jnp.dot(o, wo_ref[...], preferred_element_type=F32)


def _mix(x, y_ssm, y_att, wo1, wo2, g_x, wq, kmem, vmem, wo, b0, after=None):
    _, l, d = x.shape
    nb = y_att.shape[0]
    t = min(MIX_TILE, l)
    m = kmem.shape[1]
    row = lambda w: pl.BlockSpec((1, t, w), lambda i, j: (i, j, 0))
    full_row = lambda w: pl.BlockSpec((1, t, w), lambda i, j: (i + b0, j, 0))
    mem = pl.BlockSpec((1, m, d), lambda i, j: (i + b0, 0, 0))
    in_specs = [full_row(d), full_row(SSM_WIDTH), row(ATTN_WIDTH), _full((SSM_WIDTH, d)), _full((ATTN_WIDTH, d)),
                _full((1, d)), _full((d, d)), mem, mem, _full((d, d))]
    args = [x, y_ssm, y_att, wo1, wo2, g_x, wq, kmem, vmem, wo]
    if after is not None:
        in_specs.append(pl.BlockSpec(memory_space=pl.ANY))
        args.append(after)
    return pl.pallas_call(
        _mix_kernel,
        grid=(nb, l // t),
        in_specs=in_specs,
        out_specs=row(d),
        out_shape=jax.ShapeDtypeStruct((nb, l, d), F32),
        compiler_params=_params(("parallel", "parallel")),
        name="mix",
    )(*args)


def _first_index(hit, idx, sentinel):
    return jnp.min(jnp.where(hit, idx, sentinel), axis=0, keepdims=True)


def _route_kernel(h_ref, g_ref, wr_ref, bias_ref, wsgu_ref, wsd_ref, tri_ref,
                  t_ref, base_ref, eidx_ref, gw_ref, rank_ref, cnt_ref, carry_ref):
    @pl.when(pl.program_id(0) == 0)
    def _():
        carry_ref[...] = jnp.zeros_like(carry_ref)

    h = h_ref[...]
    t = _rms(h, g_ref[...])
    tb = t.astype(BF16)
    t_ref[...] = _pack_rows(t)
    gu = jnp.dot(tb, wsgu_ref[...], preferred_element_type=F32)
    hid = jax.nn.silu(gu[:, :D_EXPERT]) * gu[:, D_EXPERT:]
    base_ref[...] = h + jnp.dot(hid.astype(BF16), wsd_ref[...], preferred_element_type=F32)

    nt = (((1,), (1,)), ((), ()))
    t_lo = (t - tb.astype(F32)).astype(BF16)
    both = lax.dot_general(wr_ref[...], tb, nt, preferred_element_type=F32)
    logits = (both[:N_EXPERTS] + both[N_EXPERTS:]
              + lax.dot_general(wr_ref[:N_EXPERTS], t_lo, nt, preferred_element_type=F32))
    scores = jax.nn.sigmoid(logits)
    biased = scores + bias_ref[...]
    n_tok = scores.shape[1]
    neg = -jnp.inf
    sub = lax.broadcasted_iota(I32, (PER_GROUP, n_tok), 0)

    gs = []
    for g in range(N_EXPERT_GROUPS):
        blk = biased[g * PER_GROUP:(g + 1) * PER_GROUP]
        m1 = jnp.max(blk, axis=0, keepdims=True)
        i1 = _first_index(blk == m1, sub, PER_GROUP)
        m2 = jnp.max(jnp.where(sub == i1, neg, blk), axis=0, keepdims=True)
        gs.append(m1 + m2)
    gs = jnp.concatenate(gs, axis=0)

    gsel = jnp.zeros(gs.shape, jnp.bool_)
    for _ in range(TOPK_GROUPS):
        m = jnp.max(gs, axis=0, keepdims=True)
        hit = sub == _first_index(gs == m, sub, N_EXPERT_GROUPS)
        gsel = jnp.logical_or(gsel, hit)
        gs = jnp.where(hit, neg, gs)

    masked = jnp.concatenate(
        [jnp.where(gsel[g:g + 1], biased[g * PER_GROUP:(g + 1) * PER_GROUP], neg)
         for g in range(N_EXPERT_GROUPS)], axis=0)
    eid = lax.broadcasted_iota(I32, masked.shape, 0)
    sel = jnp.zeros(masked.shape, jnp.bool_)
    idxs, gws = [], []
    for _ in range(TOP_K):
        m = jnp.max(masked, axis=0, keepdims=True)
        i = _first_index(masked == m, eid, N_EXPERTS)
        hit = eid == i
        idxs.append(i)
        gws.append(jnp.sum(jnp.where(hit, scores, 0.0), axis=0, keepdims=True))
        sel = jnp.logical_or(sel, hit)
        masked = jnp.where(hit, neg, masked)
    eidx = jnp.concatenate(idxs, axis=0)
    gw = jnp.concatenate(gws, axis=0)
    gw = gw / jnp.sum(gw, axis=0, keepdims=True) * ROUTED_SCALE
    eidx_ref[...] = eidx
    gw_ref[...] = gw

    before = jnp.dot(sel.astype(BF16), tri_ref[...], preferred_element_type=F32) + carry_ref[...]
    rank_ref[...] = jnp.concatenate(
        [jnp.sum(jnp.where(eid == idxs[k], before, 0.0), axis=0, keepdims=True) for k in range(TOP_K)],
        axis=0).astype(I32)
    carry = carry_ref[...] + jnp.sum(sel.astype(F32), axis=1, keepdims=True)
    carry_ref[...] = carry
    cnt_ref[...] = carry.astype(I32)


def _route(h2, g_ffn, wr_t, bias_col, wsgu, wsd, part, n_parts):
    n_all, d = h2.shape
    n = n_all // n_parts
    t = min(ROUTE_TILE, n)
    blk0 = part * (n // t)
    tri = (lax.broadcasted_iota(I32, (t, t), 0) < lax.broadcasted_iota(I32, (t, t), 1)).astype(BF16)
    row = lambda w: pl.BlockSpec((t, w), lambda i: (i, 0))
    col = pl.BlockSpec((TOP_K, t), lambda i: (0, i))
    return pl.pallas_call(
        _route_kernel,
        grid=(n // t,),
        in_specs=[pl.BlockSpec((t, d), lambda i: (i + blk0, 0)), _full((1, d)), _full((2 * N_EXPERTS, d)),
                  _full((N_EXPERTS, 1)),
                  _full((d, 2 * D_EXPERT)), _full((D_EXPERT, d)), _full((t, t))],
        out_specs=[row(d // 2), row(d), col, col, col, _full((N_EXPERTS, 1))],
        out_shape=[jax.ShapeDtypeStruct((n, d // 2), I32), jax.ShapeDtypeStruct((n, d), F32),
                   jax.ShapeDtypeStruct((TOP_K, n), I32), jax.ShapeDtypeStruct((TOP_K, n), F32),
                   jax.ShapeDtypeStruct((TOP_K, n), I32), jax.ShapeDtypeStruct((N_EXPERTS, 1), I32)],
        scratch_shapes=[pltpu.VMEM((N_EXPERTS, 1), F32)],
        compiler_params=_params(("arbitrary",)),
        name="route",
    )(h2, g_ffn, wr_t, bias_col, wsgu, wsd, tri)


def _expert_kernel(be_ref, first_ref, slot_ref, next_ref, nu_ref, x_hbm, wg_hbm, wu_hbm, wd_hbm, y_ref,
                   wg_s, wu_s, wd_s, x_buf, x_sem, wg_buf, wu_buf, wd_buf, w_sem):
    i = pl.program_id(0)
    n_used = nu_ref[0]

    def fetch(step):
        slot = step % X_SLOTS
        rows = pl.ds(pl.multiple_of(step * MOE_TILE, MOE_TILE), MOE_TILE)
        return pltpu.make_async_copy(x_hbm.at[rows], x_buf.at[slot], x_sem.at[slot])

    def weight_fetch(expert, slot):
        pairs = ((wg_hbm, wg_buf), (wu_hbm, wu_buf), (wd_hbm, wd_buf))
        return [pltpu.make_async_copy(src.at[expert], buf.at[slot], w_sem.at[slot, j])
                for j, (src, buf) in enumerate(pairs)]

    @pl.when(i == 0)
    def _():
        for j in range(X_SLOTS - 1):
            @pl.when(j < n_used)
            def _(j=j):
                fetch(j).start()
        for cp in weight_fetch(be_ref[0], 0):
            cp.start()

    @pl.when(i + (X_SLOTS - 1) < n_used)
    def _():
        fetch(i + (X_SLOTS - 1)).start()

    @pl.when(first_ref[i] == 1)
    def _():
        slot = slot_ref[i]

        @pl.when(next_ref[i] >= 0)
        def _():
            for cp in weight_fetch(next_ref[i], 1 - slot):
                cp.start()

        for cp in weight_fetch(be_ref[i], slot):
            cp.wait()
        wg_s[...] = wg_buf[slot].astype(BF16)
        wu_s[...] = wu_buf[slot].astype(BF16)
        wd_s[...] = wd_buf[slot].astype(BF16)

    @pl.when(i < n_used)
    def _():
        fetch(i).wait()
        x = _unpack_rows(x_buf[i % X_SLOTS]).astype(BF16)
        gate = jnp.dot(x, wg_s[...], preferred_element_type=F32)
        up = jnp.dot(x, wu_s[...], preferred_element_type=F32)
        hid = (jax.nn.silu(gate) * up).astype(BF16)
        y_ref[...] = _pack_rows(jnp.dot(hid, wd_s[...], preferred_element_type=F32))


def _experts(schedule, xs, w_gate, w_up, w_down):
    block_e, run_first, run_slot, run_next, n_used = schedule
    n_slots, dw = xs.shape
    _, d, de = w_gate.shape
    nb = n_slots // MOE_TILE
    hbm = pl.BlockSpec(memory_space=pl.ANY)
    return pl.pallas_call(
        _expert_kernel,
        grid_spec=pltpu.PrefetchScalarGridSpec(
            num_scalar_prefetch=5,
            grid=(nb,),
            in_specs=[hbm, hbm, hbm, hbm],
            out_specs=pl.BlockSpec((MOE_TILE, dw), lambda i, be, fi, sl, nx, nu: (jnp.minimum(i, nu[0] - 1), 0)),
            scratch_shapes=[pltpu.VMEM((d, de), BF16), pltpu.VMEM((d, de), BF16), pltpu.VMEM((de, d), BF16),
                            pltpu.VMEM((X_SLOTS, MOE_TILE, dw), I32), pltpu.SemaphoreType.DMA((X_SLOTS,)),
                            pltpu.VMEM((2, d, de), F32), pltpu.VMEM((2, d, de), F32), pltpu.VMEM((2, de, d), F32),
                            pltpu.SemaphoreType.DMA((2, 3))]),
        out_shape=jax.ShapeDtypeStruct((n_slots, dw), I32),
        compiler_params=_params(("arbitrary",)),
        name="experts",
    )(block_e, run_first, run_slot, run_next, n_used, xs, w_gate, w_up, w_down)


def _sc_worker_id():
    return lax.axis_index("s") * SC_CORES + lax.axis_index("c")


def _sc_dispatch(t_rows, dest3, n_slots):
    _, dw = t_rows.shape
    n_chunks, _, w = dest3.shape
    per_worker = n_chunks // SC_WORKERS
    assert per_worker % 2 == 0
    mesh = plsc.VectorSubcoreMesh(core_axis_name="c", subcore_axis_name="s")
    dt = t_rows.dtype

    @functools.partial(
        pl.kernel, mesh=mesh,
        out_type=jax.ShapeDtypeStruct((n_slots, dw), dt),
        scratch_types=[pltpu.VMEM((TOP_K, w), I32), pltpu.VMEM((TOP_K, w), I32),
                       pltpu.VMEM((w, dw), dt), pltpu.VMEM((w, dw), dt),
                       pltpu.SemaphoreType.DMA, pltpu.SemaphoreType.DMA, pltpu.SemaphoreType.DMA],
    )
    def scatter_rows(t_hbm, dest_hbm, xs_hbm, idx_a, idx_b, rows_a, rows_b, sem_load, sem_a, sem_b):
        wid = _sc_worker_id()

        def scatter(idx_v, rows_v, sem):
            return [pltpu.async_copy(rows_v, xs_hbm.at[idx_v.at[k]], sem) for k in range(TOP_K)]

        @pl.loop(0, per_worker, step=2)
        def _(j):
            ca = wid * per_worker + j
            cb = ca + 1
            pltpu.sync_copy(dest_hbm.at[ca], idx_a)
            pltpu.sync_copy(t_hbm.at[pl.ds(ca * w, w)], rows_a)
            load_idx = pltpu.async_copy(dest_hbm.at[cb], idx_b, sem_load)
            load_rows = pltpu.async_copy(t_hbm.at[pl.ds(cb * w, w)], rows_b, sem_load)
            out_a = scatter(idx_a, rows_a, sem_a)
            load_idx.wait()
            load_rows.wait()
            out_b = scatter(idx_b, rows_b, sem_b)
            for cp in out_a + out_b:
                cp.wait()

    return scatter_rows(t_rows, dest3)


def _sc_combine(y_rows, dest3):
    _, dw = y_rows.shape
    n_chunks, _, w = dest3.shape
    per_worker = n_chunks // SC_WORKERS
    mesh = plsc.VectorSubcoreMesh(core_axis_name="c", subcore_axis_name="s")
    dt = y_rows.dtype

    @functools.partial(
        pl.kernel, mesh=mesh,
        out_type=jax.ShapeDtypeStruct((TOP_K, n_chunks * w, dw), dt),
        scratch_types=[pltpu.VMEM((TOP_K, w), I32), pltpu.VMEM((w, dw), dt), pltpu.VMEM((w, dw), dt),
                       pltpu.SemaphoreType.DMA, pltpu.SemaphoreType.DMA,
                       pltpu.SemaphoreType.DMA, pltpu.SemaphoreType.DMA],
    )
    def gather_rows(y_hbm, dest_hbm, out_hbm, idx_v, buf0, buf1, gsem0, gsem1, wsem0, wsem1):
        wid = _sc_worker_id()
        bufs, gsems, wsems = (buf0, buf1), (gsem0, gsem1), (wsem0, wsem1)

        @pl.loop(0, per_worker)
        def _(j):
            c = wid * per_worker + j
            pltpu.sync_copy(dest_hbm.at[c], idx_v)
            gathers = [None] * TOP_K
            writes = [None] * TOP_K
            gathers[0] = pltpu.async_copy(y_hbm.at[idx_v.at[0]], bufs[0], gsems[0])
            for k in range(TOP_K):
                b = k % 2
                if k + 1 < TOP_K:
                    if k >= 1:
                        writes[k - 1].wait()
                    gathers[k + 1] = pltpu.async_copy(y_hbm.at[idx_v.at[k + 1]], bufs[1 - b], gsems[1 - b])
                gathers[k].wait()
                writes[k] = pltpu.async_copy(bufs[b], out_hbm.at[k, pl.ds(c * w, w)], wsems[b])
            writes[TOP_K - 2].wait()
            writes[TOP_K - 1].wait()

    return gather_rows(y_rows, dest3)


def _final_kernel(base_ref, y_ref, gw_ref, g_ref, *rest, normalize):
    o_ref = rest[-1]
    h = base_ref[...]
    eye = (lax.broadcasted_iota(I32, (TOP_K, TOP_K), 0) == lax.broadcasted_iota(I32, (TOP_K, TOP_K), 1)).astype(F32)
    gw = lax.dot_general(gw_ref[...], eye, (((0,), (0,)), ((), ())), preferred_element_type=F32,
                         precision=lax.Precision.HIGHEST)
    for k in range(TOP_K):
        h = h + gw[:, k:k + 1] * _unpack_rows(y_ref[k])
    o_ref[...] = _rms(h, g_ref[...]) if normalize else h


def _final(base, yg, gw, g_final, normalize, out_prev, part, n_parts):
    n, d = base.shape
    t = min(FIN_TILE, n)
    blk0 = part * (n // t)
    in_specs = [pl.BlockSpec((t, d), lambda i: (i, 0)), pl.BlockSpec((TOP_K, t, d // 2), lambda i: (0, i, 0)),
                pl.BlockSpec((TOP_K, t), lambda i: (0, i)), _full((1, d))]
    args = [base, yg, gw, g_final]
    aliases = {}
    if out_prev is not None:
        in_specs.append(pl.BlockSpec(memory_space=pl.ANY))
        args.append(out_prev)
        aliases = {len(args) - 1: 0}
    return pl.pallas_call(
        functools.partial(_final_kernel, normalize=normalize),
        grid=(n // t,),
        in_specs=in_specs,
        out_specs=pl.BlockSpec((t, d), lambda i: (i + blk0, 0)),
        out_shape=jax.ShapeDtypeStruct((n * n_parts, d), F32),
        input_output_aliases=aliases,
        compiler_params=_params(("parallel",)),
        name="final",
    )(*args)


def _ssm_matrices(a_re, a_im, log_dt, b_re, b_im, c_re, c_im):
    lr, li = a_re.astype(F32), a_im.astype(F32)
    dt = jnp.exp(log_dt.astype(F32))[:, None]
    mag = jnp.exp(lr * dt)
    ab_re, ab_im = mag * jnp.cos(li * dt), mag * jnp.sin(li * dt)
    den = lr * lr + li * li
    zr, zi = ab_re - 1.0, ab_im
    k_re = (zr * lr + zi * li) / den
    k_im = (zi * lr - zr * li) / den
    br, bi = b_re.astype(F32), b_im.astype(F32)
    bb_re = k_re[..., None] * br - k_im[..., None] * bi
    bb_im = k_re[..., None] * bi + k_im[..., None] * br
    ar, ai = ab_re[..., None], ab_im[..., None]
    ab_b_re, ab_b_im = ar * bb_re - ai * bb_im, ar * bb_im + ai * bb_re
    cr, ci = c_re.astype(F32), c_im.astype(F32)
    car, cai = ab_re[:, None, :], ab_im[:, None, :]
    ca_re, ca_im = cr * car - ci * cai, cr * cai + ci * car
    direct = jnp.einsum("gcp,gpd->gdc", cr, bb_re) - jnp.einsum("gcp,gpd->gdc", ci, bb_im)
    n_slabs = SSM_WIDTH // SSM_CH
    per_slab = SSM_GROUPS // n_slabs
    eye = jnp.eye(per_slab, dtype=F32)

    def expand_blocks(bb):
        blocks = jnp.einsum("sgpc,gh->sgchp", bb.reshape(n_slabs, per_slab, SSM_STATE, SSM_GROUP), eye)
        return blocks.reshape(n_slabs, SSM_CH, SSM_SLAB)

    def contract_blocks(c):
        blocks = jnp.einsum("sgcp,gh->sgphc", c.reshape(n_slabs, per_slab, SSM_GROUP, SSM_STATE), eye)
        return blocks.reshape(n_slabs, SSM_SLAB, SSM_CH)

    bmat = jnp.concatenate([jnp.concatenate([expand_blocks(bb_re), expand_blocks(bb_im)], axis=2),
                            jnp.concatenate([expand_blocks(ab_b_re), expand_blocks(ab_b_im)], axis=2)], axis=1)
    cmat = jnp.concatenate([jnp.concatenate([contract_blocks(cr), contract_blocks(ca_re)], axis=2),
                            jnp.concatenate([-contract_blocks(ci), -contract_blocks(ca_im)], axis=2)], axis=1)
    dmat = jnp.einsum("sgdc,gh->sgdhc", direct.reshape(n_slabs, per_slab, SSM_GROUP, SSM_GROUP), eye)
    dmat = dmat.reshape(n_slabs, SSM_CH, SSM_CH)
    a2_re, a2_im = ab_re * ab_re - ab_im * ab_im, 2.0 * ab_re * ab_im
    tile = lambda a: jnp.broadcast_to(a.reshape(1, N_STATE), (SUBLANES, N_STATE))
    return bmat.astype(BF16), tile(a2_re), tile(a2_im), cmat.astype(BF16), dmat.astype(BF16)


def _plan_kernel(pstart_ref, eidx_ref, rank_ref, dest_ref, *, window):
    eidx = eidx_ref[...]
    dest = rank_ref[...]
    for e in range(N_EXPERTS):
        dest = dest + jnp.where(eidx == e, pstart_ref[e], 0)
    for c in range(dest.shape[1] // window):
        dest_ref[c] = dest[:, c * window:(c + 1) * window]


def _plan(pstart, eidx, rank, window):
    k, n = eidx.shape
    t = min(PLAN_TILE, n)
    cols = pl.BlockSpec((k, t), lambda i, ps: (0, i))
    return pl.pallas_call(
        functools.partial(_plan_kernel, window=window),
        grid_spec=pltpu.PrefetchScalarGridSpec(
            num_scalar_prefetch=1,
            grid=(n // t,),
            in_specs=[cols, cols],
            out_specs=pl.BlockSpec((t // window, k, window), lambda i, ps: (i, 0, 0))),
        out_shape=jax.ShapeDtypeStruct((n // window, k, window), I32),
        compiler_params=_params(("parallel",)),
        name="plan",
    )(pstart, eidx, rank)


def _block_schedule(counts, n_tok):
    padded = ((counts + MOE_TILE - 1) // MOE_TILE) * MOE_TILE
    pend = jnp.cumsum(padded)
    pstart = (pend - padded).astype(I32)
    n_slots = n_tok * TOP_K + N_EXPERTS * MOE_TILE
    nb = n_slots // MOE_TILE
    n_used = (pend[-1] // MOE_TILE).astype(I32)
    blk = jnp.arange(nb, dtype=I32)
    block_e = jnp.sum((pend[None, :] <= (jnp.minimum(blk, n_used - 1) * MOE_TILE)[:, None]).astype(I32), axis=1)
    block_e = jnp.minimum(block_e, N_EXPERTS - 1)
    eid = jnp.arange(N_EXPERTS, dtype=I32)
    active = padded > 0
    run_of = jnp.cumsum(active.astype(I32)) - 1
    later = jnp.where(jnp.logical_and(active[None, :], eid[None, :] > eid[:, None]), eid[None, :], N_EXPERTS)
    next_e = jnp.min(later, axis=1)
    next_e = jnp.where(next_e < N_EXPERTS, next_e, -1)
    onehot = block_e[:, None] == eid[None, :]
    lookup = lambda table: jnp.sum(jnp.where(onehot, table[None, :], 0), axis=1).astype(I32)
    run_first = jnp.logical_and(blk * MOE_TILE == lookup(pstart), blk < n_used).astype(I32)
    schedule = (block_e, run_first, lookup(run_of % 2), lookup(next_e), n_used.reshape(1))
    return pstart, schedule, n_slots


def kernel(x, mem, positions, g_mix, w_in, a_re, a_im, log_dt, b_re, b_im, c_re, c_im, d_skip, w_glu, g_ssm_out, lam_q1, lam_k1, lam_q2, lam_k2, g_sub, w_out, g_x, g_mem, wq_x, wk_x, wv_x, wo_x, g_ffn, w_router, router_bias, w_gate, w_up, w_down, ws_gate, ws_up, ws_down, g_final):
    b, l, d = x.shape
    n = b * l
    depth = w_in.shape[0]
    row = lambda a: a.reshape(1, -1).astype(F32)
    inv = ROPE_THETA ** (-jnp.arange(0, DA_QKDIM, 2, dtype=F32) / DA_QKDIM)
    inv_row = jnp.tile(inv, LANES // inv.shape[0]).reshape(1, LANES)
    pos3 = positions.reshape(b, l, 1)

    h = x
    for i in range(depth):
        lambda_init = 0.8 - 0.6 * math.exp(-0.3 * i)
        u, q, k, v = _inproj(h, pos3, row(g_mix[i]), w_in[i].astype(BF16), inv_row)

        ssm_mats = _ssm_matrices(a_re[i], a_im[i], log_dt[i], b_re[i], b_im[i], c_re[i], c_im[i])
        y_ssm = _ssm(u, ssm_mats, row(d_skip[i]), w_glu[i].astype(BF16), row(g_ssm_out[i]))

        lam = (jnp.exp(jnp.sum(lam_q1[i].astype(F32) * lam_k1[i].astype(F32)))
               - jnp.exp(jnp.sum(lam_q2[i].astype(F32) * lam_k2[i].astype(F32))) + lambda_init).reshape(1)
        kmem, vmem = _memkv(mem, row(g_mem[i]), wk_x[i].astype(BF16), wv_x[i].astype(BF16))
        wo = w_out[i].astype(BF16)
        wsgu = jnp.concatenate([ws_gate[i], ws_up[i]], axis=1).astype(BF16)
        wr = w_router[i].T.astype(F32)
        wr_hi = wr.astype(BF16)
        wr_split = jnp.concatenate([wr_hi, (wr - wr_hi.astype(F32)).astype(BF16)], axis=0)

        nb = b // MOE_PARTS
        n_part = nb * l

        def attend(part):
            return _diff_attention(q, k, v, lam.astype(F32), row(g_sub[i]), lambda_init, part * nb, nb)

        def route_and_dispatch(part, y_att, after=None):
            h2 = _mix(h, y_ssm, y_att, wo[:SSM_WIDTH], wo[SSM_WIDTH:], row(g_x[i]), wq_x[i].astype(BF16),
                      kmem, vmem, wo_x[i].astype(BF16), part * nb, after)
            t_pk, base, eidx, gw, rank, counts = _route(
                h2.reshape(n_part, d), row(g_ffn[i]), wr_split,
                router_bias[i].reshape(N_EXPERTS, 1).astype(F32), wsgu, ws_down[i].astype(BF16), 0, 1)
            pstart, schedule, n_slots = _block_schedule(counts[:, 0], n_part)
            dest3 = _plan(pstart, eidx, rank, SC_WINDOW)
            xs = _sc_dispatch(t_pk, dest3, n_slots)
            return base, gw, schedule, dest3, xs

        def experts_and_combine(state):
            base, gw, schedule, dest3, xs = state
            ys = _experts(schedule, xs, w_gate[i], w_up[i], w_down[i])
            return base, gw, _sc_combine(ys, dest3), ys

        assert MOE_PARTS == 2
        state_a = route_and_dispatch(0, attend(0))
        y_att_b = attend(1)
        done_a = experts_and_combine(state_a)
        state_b = route_and_dispatch(1, y_att_b, after=done_a[3])
        out = _final(done_a[0], done_a[2], done_a[1], row(g_final), i == depth - 1, None, 0, MOE_PARTS)
        done_b = experts_and_combine(state_b)
        out = _final(done_b[0], done_b[2], done_b[1], row(g_final), i == depth - 1, out, 1, MOE_PARTS)
        h = out.reshape(b, l, d)
    return h
```

```python
import functools
import math

import jax
import jax.numpy as jnp
from jax import lax
from jax.experimental import pallas as pl
from jax.experimental.pallas import tpu as pltpu
from jax.experimental.pallas import tpu_sc as plsc

F32 = jnp.float32
BF16 = jnp.bfloat16
I32 = jnp.int32

SSM_WIDTH = 512
ATTN_WIDTH = 512
SSM_GROUP = 16
SSM_GROUPS = 32
SSM_STATE = 64
N_STATE = SSM_GROUPS * SSM_STATE
DA_HEADS = 4
DA_VDIM = 128
DA_QKDIM = 64
ROPE_THETA = 10000.0
ROPE_FREQS = DA_QKDIM // 2
X_HEADS = 4
X_HEAD_DIM = 256
N_EXPERTS = 64
TOP_K = 8
N_EXPERT_GROUPS = 8
PER_GROUP = N_EXPERTS // N_EXPERT_GROUPS
TOPK_GROUPS = 4
D_EXPERT = 256
ROUTED_SCALE = 2.5
EPS = 1e-6

VMEM_LIMIT_V7X = 56 * 1024 * 1024
SUBLANES = 8
LANES = 128
HALF_WORD_BITS = 16
HIGH_HALF_MASK = -(1 << HALF_WORD_BITS)

IN_TILE = 1024
SSM_STEPS = 128
SSM_COLS = 1024
SSM_CH = 128
SSM_SLAB = 512
ATT_TILE = 2048
ATT_ROWS = 256
MIX_TILE = 1024
ROUTE_TILE = 1024
MOE_TILE = 512
X_SLOTS = 3
PLAN_TILE = 2048
MOE_PARTS = 2
FIN_TILE = 512

SC_CORES = 2
SC_WORKERS = 32
SC_WINDOW = 64


def _params(sem):
    return pltpu.CompilerParams(dimension_semantics=sem, vmem_limit_bytes=VMEM_LIMIT_V7X)


def _rms(x, g):
    return x * lax.rsqrt(jnp.mean(x * x, axis=-1, keepdims=True) + EPS) * g


def _full(shape):
    return pl.BlockSpec(shape, lambda *_: (0,) * len(shape))


def _pack_rows(a):
    w = a.shape[1] // 2
    bits = lambda v: lax.bitcast_convert_type(v.astype(BF16).astype(F32), I32)
    return (bits(a[:, w:]) & jnp.int32(HIGH_HALF_MASK)) | lax.shift_right_logical(bits(a[:, :w]), HALF_WORD_BITS)


def _unpack_rows(p):
    lo = lax.bitcast_convert_type(lax.shift_left(p, HALF_WORD_BITS), F32)
    hi = lax.bitcast_convert_type(p & jnp.int32(HIGH_HALF_MASK), F32)
    return jnp.concatenate([lo, hi], axis=1)


def _inproj_kernel(x_ref, pos_ref, g_ref, w_ref, inv_ref, u_ref, q_ref, k_ref, v_ref):
    x = x_ref[0]
    a = _rms(x, g_ref[...]).astype(BF16)
    z = jnp.dot(a, w_ref[...], preferred_element_type=F32)
    u_ref[0] = z[:, :SSM_WIDTH]
    n_grp = LANES // ROPE_FREQS
    tq = x.shape[0] // n_grp
    pos = pos_ref[0].astype(F32)
    lane_c = lax.broadcasted_iota(I32, (tq, LANES), 1)
    pos_c = jnp.zeros((tq, LANES), F32)
    for j in range(n_grp):
        pos_c = jnp.where(lane_c // ROPE_FREQS == j, pos[j * tq:(j + 1) * tq], pos_c)
    ang = pos_c * inv_ref[...]

    def spread(c):
        rows = []
        for j in range(n_grp):
            y = c if j == 0 else pltpu.roll(c, LANES - ROPE_FREQS * j, 1)
            w = ROPE_FREQS
            while w < LANES:
                y = jnp.where(lane_c < w, y, pltpu.roll(y, w, 1))
                w *= 2
            rows.append(y)
        return jnp.concatenate(rows, axis=0)

    cos = jnp.concatenate([spread(jnp.cos(ang))] * (ATTN_WIDTH // LANES), axis=1)
    sin = jnp.concatenate([spread(jnp.sin(ang))] * (ATTN_WIDTH // LANES), axis=1)
    lane = lax.broadcasted_iota(I32, cos.shape, 1)
    first = (lane & (DA_QKDIM - 1)) < DA_QKDIM // 2
    half = DA_QKDIM // 2

    def rope(t):
        rot = jnp.where(first, -pltpu.roll(t, ATTN_WIDTH - half, 1), pltpu.roll(t, half, 1))
        return t * cos + rot * sin

    q = z[:, SSM_WIDTH:SSM_WIDTH + ATTN_WIDTH]
    k = z[:, SSM_WIDTH + ATTN_WIDTH:SSM_WIDTH + 2 * ATTN_WIDTH]
    q_ref[0] = (rope(q) * (DA_QKDIM ** -0.5)).astype(BF16)
    k_ref[0] = rope(k).astype(BF16)
    v_ref[0] = z[:, SSM_WIDTH + 2 * ATTN_WIDTH:].astype(BF16)


def _inproj(x, pos3, g_mix, w_in, inv_row):
    b, l, d = x.shape
    t = min(IN_TILE, l)
    n_out = w_in.shape[1]
    row = lambda w: pl.BlockSpec((1, t, w), lambda i, j: (i, j, 0))
    return pl.pallas_call(
        _inproj_kernel,
        grid=(b, l // t),
        in_specs=[row(d), row(1), _full((1, d)), _full((d, n_out)), _full((1, LANES))],
        out_specs=[row(SSM_WIDTH), row(ATTN_WIDTH), row(ATTN_WIDTH), row(ATTN_WIDTH)],
        out_shape=[jax.ShapeDtypeStruct((b, l, SSM_WIDTH), F32)]
        + [jax.ShapeDtypeStruct((b, l, ATTN_WIDTH), BF16)] * 3,
        compiler_params=_params(("parallel", "parallel")),
        name="inproj",
    )(x, pos3, g_mix, w_in, inv_row)


def _ssm_kernel(u_ref, bm_ref, are_ref, aim_ref, cm_ref, dm_ref, dskip_ref, wglu_ref, g_ref, o_ref,
                c_ref, st_ref, up_ref, *, steps):
    @pl.when(pl.program_id(0) == 0)
    def _():
        st_ref[...] = jnp.zeros_like(st_ref)
        up_ref[...] = jnp.zeros_like(up_ref)

    pairs = steps // 2
    rows = pairs * SUBLANES
    u = pltpu.einshape("bts->tbs", u_ref[...]).reshape(pairs, 2, SUBLANES, SSM_WIDTH)
    u_even = u[:, 0].reshape(rows, SSM_WIDTH)
    u_odd = u[:, 1].reshape(rows, SSM_WIDTH)
    u_before = jnp.concatenate([up_ref[...], u_odd[:rows - SUBLANES]], axis=0)
    up_ref[...] = u_odd[rows - SUBLANES:]
    ue, uo, ub = u_even.astype(BF16), u_odd.astype(BF16), u_before.astype(BF16)

    n_slabs = SSM_WIDTH // SSM_CH
    for s in range(n_slabs):
        ch = slice(s * SSM_CH, (s + 1) * SSM_CH)
        lhs = jnp.concatenate([ue[:, ch], ub[:, ch]], axis=1)
        for part in (0, 1):
            cols = slice(part * N_STATE + s * SSM_SLAB, part * N_STATE + (s + 1) * SSM_SLAB)
            c_ref[:, cols] = jnp.dot(lhs, bm_ref[s, :, part * SSM_SLAB:(part + 1) * SSM_SLAB],
                                     preferred_element_type=F32)

    for c0 in range(0, N_STATE, SSM_COLS):
        re = slice(c0, c0 + SSM_COLS)
        im = slice(N_STATE + c0, N_STATE + c0 + SSM_COLS)
        ar = are_ref[:, re]
        ai = aim_ref[:, re]
        sr, si = st_ref[:, re], st_ref[:, im]
        for k in range(pairs):
            blk = slice(k * SUBLANES, (k + 1) * SUBLANES)
            sr, si = ar * sr - ai * si + c_ref[blk, re], ar * si + ai * sr + c_ref[blk, im]
            c_ref[blk, re] = sr
            c_ref[blk, im] = si
        st_ref[:, re] = sr
        st_ref[:, im] = si

    y_even, y_odd = [], []
    for s in range(n_slabs):
        ch = slice(s * SSM_CH, (s + 1) * SSM_CH)
        acc = jnp.dot(uo[:, ch], dm_ref[s], preferred_element_type=F32)
        both = None
        for part in (0, 1):
            cols = slice(part * N_STATE + s * SSM_SLAB, part * N_STATE + (s + 1) * SSM_SLAB)
            d = jnp.dot(c_ref[:, cols].astype(BF16), cm_ref[s, part * SSM_SLAB:(part + 1) * SSM_SLAB],
                        preferred_element_type=F32)
            both = d if both is None else both + d
        y_even.append(both[:, :SSM_CH])
        y_odd.append(both[:, SSM_CH:] + acc)
    y = jnp.concatenate([jnp.concatenate(y_even, axis=1), jnp.concatenate(y_odd, axis=1)], axis=0)
    y = y + dskip_ref[...] * jnp.concatenate([u_even, u_odd], axis=0)
    y = jax.nn.gelu(y)
    y = y * jax.nn.sigmoid(jnp.dot(y.astype(BF16), wglu_ref[...], preferred_element_type=F32))
    out = _rms(y, g_ref[...])
    out = jnp.stack([out[:rows].reshape(pairs, SUBLANES, SSM_WIDTH),
                     out[rows:].reshape(pairs, SUBLANES, SSM_WIDTH)], axis=1).reshape(steps, SUBLANES, SSM_WIDTH)
    o_ref[...] = pltpu.einshape("tbs->bts", out).astype(BF16)


def _ssm(u, mats, d_skip, w_glu, g_out):
    bmat, a2_re8, a2_im8, cmat, dmat = mats
    batch, l, _ = u.shape
    assert batch == SUBLANES, "one time step of all sequences must fill the sublanes"
    steps = min(SSM_STEPS, l)
    assert steps % 2 == 0
    rows = steps // 2 * batch
    n_slabs = SSM_WIDTH // SSM_CH
    seq = pl.BlockSpec((batch, steps, SSM_WIDTH), lambda i: (0, i, 0))
    return pl.pallas_call(
        functools.partial(_ssm_kernel, steps=steps),
        grid=(l // steps,),
        in_specs=[seq,
                  _full((n_slabs, 2 * SSM_CH, 2 * SSM_SLAB)), _full((SUBLANES, N_STATE)), _full((SUBLANES, N_STATE)),
                  _full((n_slabs, 2 * SSM_SLAB, 2 * SSM_CH)), _full((n_slabs, SSM_CH, SSM_CH)),
                  _full((1, SSM_WIDTH)), _full((SSM_WIDTH, SSM_WIDTH)), _full((1, SSM_WIDTH))],
        out_specs=seq,
        out_shape=jax.ShapeDtypeStruct((batch, l, SSM_WIDTH), BF16),
        scratch_shapes=[pltpu.VMEM((rows, 2 * N_STATE), F32), pltpu.VMEM((SUBLANES, 2 * N_STATE), F32),
                        pltpu.VMEM((SUBLANES, SSM_WIDTH), F32)],
        compiler_params=_params(("arbitrary",)),
        name="ssm",
    )(u, bmat, a2_re8, a2_im8, cmat, dmat, d_skip, w_glu, g_out)


def _attn_kernel(qi_ref, ki_ref, lam_ref, q_ref, k_ref, v_ref, g_ref, o_ref, qs_ref, m_ref, acc_ref, *,
                 tile, out_scale):
    qi = qi_ref[pl.program_id(2)]
    ki = ki_ref[pl.program_id(2)]

    @pl.when(ki == 0)
    def _():
        q = q_ref[0]
        lane = lax.broadcasted_iota(I32, q.shape, 1)
        zero = jnp.zeros_like(q)
        qs_ref[:tile] = jnp.where(lane < DA_QKDIM, q, zero)
        qs_ref[tile:] = jnp.where(lane >= DA_QKDIM, q, zero)
        m_ref[...] = jnp.full_like(m_ref, -jnp.inf)
        acc_ref[...] = jnp.zeros_like(acc_ref)

    def update(masked):
        v_ext = jnp.concatenate([v_ref[0], jnp.ones((tile, DA_VDIM), BF16)], axis=1)
        rb = min(ATT_ROWS, tile)
        n_rb = 2 * tile // rb

        def n_keys(r):
            return (r * rb) % tile + rb if masked else tile

        def scores(r):
            return lax.dot_general(qs_ref[r * rb:(r + 1) * rb], k_ref[0, :n_keys(r)], (((1,), (1,)), ((), ())),
                                   preferred_element_type=F32)

        s_next = scores(0)
        for r in range(n_rb):
            s = s_next
            if r + 1 < n_rb:
                s_next = scores(r + 1)
            rows = slice(r * rb, (r + 1) * rb)
            nk = n_keys(r)
            if masked:
                tri = lax.broadcasted_iota(I32, (rb, rb), 1) <= lax.broadcasted_iota(I32, (rb, rb), 0)
                edge = jnp.where(tri, s[:, nk - rb:], jnp.finfo(F32).min)
                s = edge if nk == rb else jnp.concatenate([s[:, :nk - rb], edge], axis=1)
            m_old = m_ref[rows]
            m_new = jnp.maximum(m_old, jnp.max(s, axis=1, keepdims=True))
            p = jnp.exp(s - jnp.concatenate([m_new] * (nk // DA_VDIM), axis=1))
            alpha = jnp.exp(m_old - m_new)
            acc_ref[rows] = (jnp.concatenate([alpha, alpha], axis=1) * acc_ref[rows]
                             + jnp.dot(p.astype(BF16), v_ext[:nk], preferred_element_type=F32))
            m_ref[rows] = m_new

    @pl.when(ki < qi)
    def _():
        update(False)

    @pl.when(ki == qi)
    def _():
        update(True)
        o = acc_ref[:, :DA_VDIM] / acc_ref[:, DA_VDIM:]
        o = o[:tile] - lam_ref[0] * o[tile:]
        o_ref[0] = (_rms(o, g_ref[...]) * out_scale).astype(BF16)


def _diff_attention(q, k, v, lam, g_sub, lambda_init, b0, nb):
    _, l, _ = q.shape
    t = min(ATT_TILE, l)
    n = l // t
    pairs = [(qi, ki) for qi in range(n) for ki in range(qi + 1)]
    qi_tab = jnp.asarray([p[0] for p in pairs], I32)
    ki_tab = jnp.asarray([p[1] for p in pairs], I32)
    qspec = pl.BlockSpec((1, t, DA_VDIM), lambda bi, h, s, qt, kt: (bi + b0, qt[s], h))
    kspec = pl.BlockSpec((1, t, DA_VDIM), lambda bi, h, s, qt, kt: (bi + b0, kt[s], h))
    ospec = pl.BlockSpec((1, t, DA_VDIM), lambda bi, h, s, qt, kt: (bi, qt[s], h))
    return pl.pallas_call(
        functools.partial(_attn_kernel, tile=t, out_scale=1.0 - lambda_init),
        grid_spec=pltpu.PrefetchScalarGridSpec(
            num_scalar_prefetch=2,
            grid=(nb, DA_HEADS, len(pairs)),
            in_specs=[pl.BlockSpec(memory_space=pltpu.SMEM), qspec, kspec, kspec,
                      pl.BlockSpec((1, DA_VDIM), lambda bi, h, s, qt, kt: (0, 0))],
            out_specs=ospec,
            scratch_shapes=[pltpu.VMEM((2 * t, DA_VDIM), BF16), pltpu.VMEM((2 * t, DA_VDIM), F32),
                            pltpu.VMEM((2 * t, 2 * DA_VDIM), F32)]),
        out_shape=jax.ShapeDtypeStruct((nb, l, ATTN_WIDTH), BF16),
        compiler_params=_params(("parallel", "parallel", "arbitrary")),
        name="diffattn",
    )(qi_tab, ki_tab, lam, q, k, v, g_sub)


def _memkv_kernel(m_ref, g_ref, wk_ref, wv_ref, k_ref, v_ref):
    a = _rms(m_ref[0], g_ref[...]).astype(BF16)
    k_ref[0] = jnp.dot(a, wk_ref[...], preferred_element_type=F32).astype(BF16)
    v_ref[0] = jnp.dot(a, wv_ref[...], preferred_element_type=F32).astype(BF16)


def _memkv(mem, g_mem, wk, wv):
    b, m, d = mem.shape
    blk = pl.BlockSpec((1, m, d), lambda i: (i, 0, 0))
    return pl.pallas_call(
        _memkv_kernel,
        grid=(b,),
        in_specs=[blk, _full((1, d)), _full((d, d)), _full((d, d))],
        out_specs=[blk, blk],
        out_shape=[jax.ShapeDtypeStruct((b, m, d), BF16)] * 2,
        compiler_params=_params(("parallel",)),
        name="memkv",
    )(mem, g_mem, wk, wv)


def _mix_kernel(x_ref, ys_ref, ya_ref, wo1_ref, wo2_ref, gx_ref, wq_ref, km_ref, vm_ref, wo_ref, *rest):
    h_ref = rest[-1]
    h = (x_ref[0]
         + jnp.dot(ys_ref[0], wo1_ref[...], preferred_element_type=F32)
         + jnp.dot(ya_ref[0], wo2_ref[...], preferred_element_type=F32))
    hq = _rms(h, gx_ref[...]).astype(BF16)
    q = jnp.dot(hq, wq_ref[...], preferred_element_type=F32).astype(BF16)
    outs = []
    for hd in range(X_HEADS):
        cols = slice(hd * X_HEAD_DIM, (hd + 1) * X_HEAD_DIM)
        s = lax.dot_general(q[:, cols], km_ref[0, :, cols], (((1,), (1,)), ((), ())),
                            preferred_element_type=F32) * (X_HEAD_DIM ** -0.5)
        s = s - jnp.max(s, axis=1, keepdims=True)
        p = jnp.exp(s)
        p = p / jnp.sum(p, axis=1, keepdims=True)
        outs.append(jnp.dot(p.astype(BF16), vm_ref[0, :, cols], preferred_element_type=F32).astype(BF16))
    o = jnp.concatenate(outs, axis=1)
    h_ref[0] = h + jnp.dot(o, wo_ref[...], preferred_element_type=F32)


def _mix(x, y_ssm, y_att, wo1, wo2, g_x, wq, kmem, vmem, wo, b0, after=None):
    _, l, d = x.shape
    nb = y_att.shape[0]
    t = min(MIX_TILE, l)
    m = kmem.shape[1]
    row = lambda w: pl.BlockSpec((1, t, w), lambda i, j: (i, j, 0))
    full_row = lambda w: pl.BlockSpec((1, t, w), lambda i, j: (i + b0, j, 0))
    mem = pl.BlockSpec((1, m, d), lambda i, j: (i + b0, 0, 0))
    in_specs = [full_row(d), full_row(SSM_WIDTH), row(ATTN_WIDTH), _full((SSM_WIDTH, d)), _full((ATTN_WIDTH, d)),
                _full((1, d)), _full((d, d)), mem, mem, _full((d, d))]
    args = [x, y_ssm, y_att, wo1, wo2, g_x, wq, kmem, vmem, wo]
    if after is not None:
        in_specs.append(pl.BlockSpec(memory_space=pl.ANY))
        args.append(after)
    return pl.pallas_call(
        _mix_kernel,
        grid=(nb, l // t),
        in_specs=in_specs,
        out_specs=row(d),
        out_shape=jax.ShapeDtypeStruct((nb, l, d), F32),
        compiler_params=_params(("parallel", "parallel")),
        name="mix",
    )(*args)


def _first_index(hit, idx, sentinel):
    return jnp.min(jnp.where(hit, idx, sentinel), axis=0, keepdims=True)


def _route_kernel(h_ref, g_ref, wr_ref, bias_ref, wsgu_ref, wsd_ref, tri_ref,
                  t_ref, base_ref, eidx_ref, gw_ref, rank_ref, cnt_ref, carry_ref):
    @pl.when(pl.program_id(0) == 0)
    def _():
        carry_ref[...] = jnp.zeros_like(carry_ref)

    h = h_ref[...]
    t = _rms(h, g_ref[...])
    tb = t.astype(BF16)
    t_ref[...] = _pack_rows(t)
    gu = jnp.dot(tb, wsgu_ref[...], preferred_element_type=F32)
    hid = jax.nn.silu(gu[:, :D_EXPERT]) * gu[:, D_EXPERT:]
    base_ref[...] = h + jnp.dot(hid.astype(BF16), wsd_ref[...], preferred_element_type=F32)

    nt = (((1,), (1,)), ((), ()))
    t_lo = (t - tb.astype(F32)).astype(BF16)
    both = lax.dot_general(wr_ref[...], tb, nt, preferred_element_type=F32)
    logits = (both[:N_EXPERTS] + both[N_EXPERTS:]
              + lax.dot_general(wr_ref[:N_EXPERTS], t_lo, nt, preferred_element_type=F32))
    scores = jax.nn.sigmoid(logits)
    biased = scores + bias_ref[...]
    n_tok = scores.shape[1]
    neg = -jnp.inf
    sub = lax.broadcasted_iota(I32, (PER_GROUP, n_tok), 0)

    gs = []
    for g in range(N_EXPERT_GROUPS):
        blk = biased[g * PER_GROUP:(g + 1) * PER_GROUP]
        m1 = jnp.max(blk, axis=0, keepdims=True)
        i1 = _first_index(blk == m1, sub, PER_GROUP)
        m2 = jnp.max(jnp.where(sub == i1, neg, blk), axis=0, keepdims=True)
        gs.append(m1 + m2)
    gs = jnp.concatenate(gs, axis=0)

    gsel = jnp.zeros(gs.shape, jnp.bool_)
    for _ in range(TOPK_GROUPS):
        m = jnp.max(gs, axis=0, keepdims=True)
        hit = sub == _first_index(gs == m, sub, N_EXPERT_GROUPS)
        gsel = jnp.logical_or(gsel, hit)
        gs = jnp.where(hit, neg, gs)

    masked = jnp.concatenate(
        [jnp.where(gsel[g:g + 1], biased[g * PER_GROUP:(g + 1) * PER_GROUP], neg)
         for g in range(N_EXPERT_GROUPS)], axis=0)
    eid = lax.broadcasted_iota(I32, masked.shape, 0)
    sel = jnp.zeros(masked.shape, jnp.bool_)
    idxs, gws = [], []
    for _ in range(TOP_K):
        m = jnp.max(masked, axis=0, keepdims=True)
        i = _first_index(masked == m, eid, N_EXPERTS)
        hit = eid == i
        idxs.append(i)
        gws.append(jnp.sum(jnp.where(hit, scores, 0.0), axis=0, keepdims=True))
        sel = jnp.logical_or(sel, hit)
        masked = jnp.where(hit, neg, masked)
    eidx = jnp.concatenate(idxs, axis=0)
    gw = jnp.concatenate(gws, axis=0)
    gw = gw / jnp.sum(gw, axis=0, keepdims=True) * ROUTED_SCALE
    eidx_ref[...] = eidx
    gw_ref[...] = gw

    before = jnp.dot(sel.astype(BF16), tri_ref[...], preferred_element_type=F32) + carry_ref[...]
    rank_ref[...] = jnp.concatenate(
        [jnp.sum(jnp.where(eid == idxs[k], before, 0.0), axis=0, keepdims=True) for k in range(TOP_K)],
        axis=0).astype(I32)
    carry = carry_ref[...] + jnp.sum(sel.astype(F32), axis=1, keepdims=True)
    carry_ref[...] = carry
    cnt_ref[...] = carry.astype(I32)


def _route(h2, g_ffn, wr_t, bias_col, wsgu, wsd, part, n_parts):
    n_all, d = h2.shape
    n = n_all // n_parts
    t = min(ROUTE_TILE, n)
    blk0 = part * (n // t)
    tri = (lax.broadcasted_iota(I32, (t, t), 0) < lax.broadcasted_iota(I32, (t, t), 1)).astype(BF16)
    row = lambda w: pl.BlockSpec((t, w), lambda i: (i, 0))
    col = pl.BlockSpec((TOP_K, t), lambda i: (0, i))
    return pl.pallas_call(
        _route_kernel,
        grid=(n // t,),
        in_specs=[pl.BlockSpec((t, d), lambda i: (i + blk0, 0)), _full((1, d)), _full((2 * N_EXPERTS, d)),
                  _full((N_EXPERTS, 1)),
                  _full((d, 2 * D_EXPERT)), _full((D_EXPERT, d)), _full((t, t))],
        out_specs=[row(d // 2), row(d), col, col, col, _full((N_EXPERTS, 1))],
        out_shape=[jax.ShapeDtypeStruct((n, d // 2), I32), jax.ShapeDtypeStruct((n, d), F32),
                   jax.ShapeDtypeStruct((TOP_K, n), I32), jax.ShapeDtypeStruct((TOP_K, n), F32),
                   jax.ShapeDtypeStruct((TOP_K, n), I32), jax.ShapeDtypeStruct((N_EXPERTS, 1), I32)],
        scratch_shapes=[pltpu.VMEM((N_EXPERTS, 1), F32)],
        compiler_params=_params(("arbitrary",)),
        name="route",
    )(h2, g_ffn, wr_t, bias_col, wsgu, wsd, tri)


def _expert_kernel(be_ref, first_ref, slot_ref, next_ref, nu_ref, x_hbm, wg_hbm, wu_hbm, wd_hbm, y_ref,
                   wg_s, wu_s, wd_s, x_buf, x_sem, wg_buf, wu_buf, wd_buf, w_sem):
    i = pl.program_id(0)
    n_used = nu_ref[0]

    def fetch(step):
        slot = step % X_SLOTS
        rows = pl.ds(pl.multiple_of(step * MOE_TILE, MOE_TILE), MOE_TILE)
        return pltpu.make_async_copy(x_hbm.at[rows], x_buf.at[slot], x_sem.at[slot])

    def weight_fetch(expert, slot):
        pairs = ((wg_hbm, wg_buf), (wu_hbm, wu_buf), (wd_hbm, wd_buf))
        return [pltpu.make_async_copy(src.at[expert], buf.at[slot], w_sem.at[slot, j])
                for j, (src, buf) in enumerate(pairs)]

    @pl.when(i == 0)
    def _():
        for j in range(X_SLOTS - 1):
            @pl.when(j < n_used)
            def _(j=j):
                fetch(j).start()
        for cp in weight_fetch(be_ref[0], 0):
            cp.start()

    @pl.when(i + (X_SLOTS - 1) < n_used)
    def _():
        fetch(i + (X_SLOTS - 1)).start()

    @pl.when(first_ref[i] == 1)
    def _():
        slot = slot_ref[i]

        @pl.when(next_ref[i] >= 0)
        def _():
            for cp in weight_fetch(next_ref[i], 1 - slot):
                cp.start()

        for cp in weight_fetch(be_ref[i], slot):
            cp.wait()
        wg_s[...] = wg_buf[slot].astype(BF16)
        wu_s[...] = wu_buf[slot].astype(BF16)
        wd_s[...] = wd_buf[slot].astype(BF16)

    @pl.when(i < n_used)
    def _():
        fetch(i).wait()
        x = _unpack_rows(x_buf[i % X_SLOTS]).astype(BF16)
        gate = jnp.dot(x, wg_s[...], preferred_element_type=F32)
        up = jnp.dot(x, wu_s[...], preferred_element_type=F32)
        hid = (jax.nn.silu(gate) * up).astype(BF16)
        y_ref[...] = _pack_rows(jnp.dot(hid, wd_s[...], preferred_element_type=F32))


def _experts(schedule, xs, w_gate, w_up, w_down):
    block_e, run_first, run_slot, run_next, n_used = schedule
    n_slots, dw = xs.shape
    _, d, de = w_gate.shape
    nb = n_slots // MOE_TILE
    hbm = pl.BlockSpec(memory_space=pl.ANY)
    return pl.pallas_call(
        _expert_kernel,
        grid_spec=pltpu.PrefetchScalarGridSpec(
            num_scalar_prefetch=5,
            grid=(nb,),
            in_specs=[hbm, hbm, hbm, hbm],
            out_specs=pl.BlockSpec((MOE_TILE, dw), lambda i, be, fi, sl, nx, nu: (jnp.minimum(i, nu[0] - 1), 0)),
            scratch_shapes=[pltpu.VMEM((d, de), BF16), pltpu.VMEM((d, de), BF16), pltpu.VMEM((de, d), BF16),
                            pltpu.VMEM((X_SLOTS, MOE_TILE, dw), I32), pltpu.SemaphoreType.DMA((X_SLOTS,)),
                            pltpu.VMEM((2, d, de), F32), pltpu.VMEM((2, d, de), F32), pltpu.VMEM((2, de, d), F32),
                            pltpu.SemaphoreType.DMA((2, 3))]),
        out_shape=jax.ShapeDtypeStruct((n_slots, dw), I32),
        compiler_params=_params(("arbitrary",)),
        name="experts",
    )(block_e, run_first, run_slot, run_next, n_used, xs, w_gate, w_up, w_down)


def _sc_worker_id():
    return lax.axis_index("s") * SC_CORES + lax.axis_index("c")


def _sc_dispatch(t_rows, dest3, n_slots):
    _, dw = t_rows.shape
    n_chunks, _, w = dest3.shape
    per_worker = n_chunks // SC_WORKERS
    assert per_worker % 2 == 0
    mesh = plsc.VectorSubcoreMesh(core_axis_name="c", subcore_axis_name="s")
    dt = t_rows.dtype

    @functools.partial(
        pl.kernel, mesh=mesh,
        out_type=jax.ShapeDtypeStruct((n_slots, dw), dt),
        scratch_types=[pltpu.VMEM((TOP_K, w), I32), pltpu.VMEM((TOP_K, w), I32),
                       pltpu.VMEM((w, dw), dt), pltpu.VMEM((w, dw), dt),
                       pltpu.SemaphoreType.DMA, pltpu.SemaphoreType.DMA, pltpu.SemaphoreType.DMA],
    )
    def scatter_rows(t_hbm, dest_hbm, xs_hbm, idx_a, idx_b, rows_a, rows_b, sem_load, sem_a, sem_b):
        wid = _sc_worker_id()

        def scatter(idx_v, rows_v, sem):
            return [pltpu.async_copy(rows_v, xs_hbm.at[idx_v.at[k]], sem) for k in range(TOP_K)]

        @pl.loop(0, per_worker, step=2)
        def _(j):
            ca = wid * per_worker + j
            cb = ca + 1
            pltpu.sync_copy(dest_hbm.at[ca], idx_a)
            pltpu.sync_copy(t_hbm.at[pl.ds(ca * w, w)], rows_a)
            load_idx = pltpu.async_copy(dest_hbm.at[cb], idx_b, sem_load)
            load_rows = pltpu.async_copy(t_hbm.at[pl.ds(cb * w, w)], rows_b, sem_load)
            out_a = scatter(idx_a, rows_a, sem_a)
            load_idx.wait()
            load_rows.wait()
            out_b = scatter(idx_b, rows_b, sem_b)
            for cp in out_a + out_b:
                cp.wait()

    return scatter_rows(t_rows, dest3)


def _sc_combine(y_rows, dest3):
    _, dw = y_rows.shape
    n_chunks, _, w = dest3.shape
    per_worker = n_chunks // SC_WORKERS
    mesh = plsc.VectorSubcoreMesh(core_axis_name="c", subcore_axis_name="s")
    dt = y_rows.dtype

    @functools.partial(
        pl.kernel, mesh=mesh,
        out_type=jax.ShapeDtypeStruct((TOP_K, n_chunks * w, dw), dt),
        scratch_types=[pltpu.VMEM((TOP_K, w), I32), pltpu.VMEM((w, dw), dt), pltpu.VMEM((w, dw), dt),
                       pltpu.SemaphoreType.DMA, pltpu.SemaphoreType.DMA,
                       pltpu.SemaphoreType.DMA, pltpu.SemaphoreType.DMA],
    )
    def gather_rows(y_hbm, dest_hbm, out_hbm, idx_v, buf0, buf1, gsem0, gsem1, wsem0, wsem1):
        wid = _sc_worker_id()
        bufs, gsems, wsems = (buf0, buf1), (gsem0, gsem1), (wsem0, wsem1)

        @pl.loop(0, per_worker)
        def _(j):
            c = wid * per_worker + j
            pltpu.sync_copy(dest_hbm.at[c], idx_v)
            gathers = [None] * TOP_K
            writes = [None] * TOP_K
            gathers[0] = pltpu.async_copy(y_hbm.at[idx_v.at[0]], bufs[0], gsems[0])
            for k in range(TOP_K):
                b = k % 2
                if k + 1 < TOP_K:
                    if k >= 1:
                        writes[k - 1].wait()
                    gathers[k + 1] = pltpu.async_copy(y_hbm.at[idx_v.at[k + 1]], bufs[1 - b], gsems[1 - b])
                gathers[k].wait()
                writes[k] = pltpu.async_copy(bufs[b], out_hbm.at[k, pl.ds(c * w, w)], wsems[b])
            writes[TOP_K - 2].wait()
            writes[TOP_K - 1].wait()

    return gather_rows(y_rows, dest3)


def _final_kernel(base_ref, y_ref, gw_ref, g_ref, *rest, normalize):
    o_ref = rest[-1]
    h = base_ref[...]
    eye = (lax.broadcasted_iota(I32, (TOP_K, TOP_K), 0) == lax.broadcasted_iota(I32, (TOP_K, TOP_K), 1)).astype(F32)
    gw = lax.dot_general(gw_ref[...], eye, (((0,), (0,)), ((), ())), preferred_element_type=F32,
                         precision=lax.Precision.HIGHEST)
    for k in range(TOP_K):
        h = h + gw[:, k:k + 1] * _unpack_rows(y_ref[k])
    o_ref[...] = _rms(h, g_ref[...]) if normalize else h


def _final(base, yg, gw, g_final, normalize, out_prev, part, n_parts):
    n, d = base.shape
    t = min(FIN_TILE, n)
    blk0 = part * (n // t)
    in_specs = [pl.BlockSpec((t, d), lambda i: (i, 0)), pl.BlockSpec((TOP_K, t, d // 2), lambda i: (0, i, 0)),
                pl.BlockSpec((TOP_K, t), lambda i: (0, i)), _full((1, d))]
    args = [base, yg, gw, g_final]
    aliases = {}
    if out_prev is not None:
        in_specs.append(pl.BlockSpec(memory_space=pl.ANY))
        args.append(out_prev)
        aliases = {len(args) - 1: 0}
    return pl.pallas_call(
        functools.partial(_final_kernel, normalize=normalize),
        grid=(n // t,),
        in_specs=in_specs,
        out_specs=pl.BlockSpec((t, d), lambda i: (i + blk0, 0)),
        out_shape=jax.ShapeDtypeStruct((n * n_parts, d), F32),
        input_output_aliases=aliases,
        compiler_params=_params(("parallel",)),
        name="final",
    )(*args)


def _ssm_matrices(a_re, a_im, log_dt, b_re, b_im, c_re, c_im):
    lr, li = a_re.astype(F32), a_im.astype(F32)
    dt = jnp.exp(log_dt.astype(F32))[:, None]
    mag = jnp.exp(lr * dt)
    ab_re, ab_im = mag * jnp.cos(li * dt), mag * jnp.sin(li * dt)
    den = lr * lr + li * li
    zr, zi = ab_re - 1.0, ab_im
    k_re = (zr * lr + zi * li) / den
    k_im = (zi * lr - zr * li) / den
    br, bi = b_re.astype(F32), b_im.astype(F32)
    bb_re = k_re[..., None] * br - k_im[..., None] * bi
    bb_im = k_re[..., None] * bi + k_im[..., None] * br
    ar, ai = ab_re[..., None], ab_im[..., None]
    ab_b_re, ab_b_im = ar * bb_re - ai * bb_im, ar * bb_im + ai * bb_re
    cr, ci = c_re.astype(F32), c_im.astype(F32)
    car, cai = ab_re[:, None, :], ab_im[:, None, :]
    ca_re, ca_im = cr * car - ci * cai, cr * cai + ci * car
    direct = jnp.einsum("gcp,gpd->gdc", cr, bb_re) - jnp.einsum("gcp,gpd->gdc", ci, bb_im)
    n_slabs = SSM_WIDTH // SSM_CH
    per_slab = SSM_GROUPS // n_slabs
    eye = jnp.eye(per_slab, dtype=F32)

    def expand_blocks(bb):
        blocks = jnp.einsum("sgpc,gh->sgchp", bb.reshape(n_slabs, per_slab, SSM_STATE, SSM_GROUP), eye)
        return blocks.reshape(n_slabs, SSM_CH, SSM_SLAB)

    def contract_blocks(c):
        blocks = jnp.einsum("sgcp,gh->sgphc", c.reshape(n_slabs, per_slab, SSM_GROUP, SSM_STATE), eye)
        return blocks.reshape(n_slabs, SSM_SLAB, SSM_CH)

    bmat = jnp.concatenate([jnp.concatenate([expand_blocks(bb_re), expand_blocks(bb_im)], axis=2),
                            jnp.concatenate([expand_blocks(ab_b_re), expand_blocks(ab_b_im)], axis=2)], axis=1)
    cmat = jnp.concatenate([jnp.concatenate([contract_blocks(cr), contract_blocks(ca_re)], axis=2),
                            jnp.concatenate([-contract_blocks(ci), -contract_blocks(ca_im)], axis=2)], axis=1)
    dmat = jnp.einsum("sgdc,gh->sgdhc", direct.reshape(n_slabs, per_slab, SSM_GROUP, SSM_GROUP), eye)
    dmat = dmat.reshape(n_slabs, SSM_CH, SSM_CH)
    a2_re, a2_im = ab_re * ab_re - ab_im * ab_im, 2.0 * ab_re * ab_im
    tile = lambda a: jnp.broadcast_to(a.reshape(1, N_STATE), (SUBLANES, N_STATE))
    return bmat.astype(BF16), tile(a2_re), tile(a2_im), cmat.astype(BF16), dmat.astype(BF16)


def _plan_kernel(pstart_ref, eidx_ref, rank_ref, *rest, window):
    dest_ref = rest[-1]
    eidx = eidx_ref[...]
    dest = rank_ref[...]
    for e in range(N_EXPERTS):
        dest = dest + jnp.where(eidx == e, pstart_ref[e], 0)
    for c in range(dest.shape[1] // window):
        dest_ref[c] = dest[:, c * window:(c + 1) * window]


def _plan(pstart, eidx, rank, window, after=None):
    k, n = eidx.shape
    t = min(PLAN_TILE, n)
    cols = pl.BlockSpec((k, t), lambda i, ps: (0, i))
    in_specs, args = [cols, cols], [pstart, eidx, rank]
    if after is not None:
        in_specs.append(pl.BlockSpec(memory_space=pl.ANY))
        args.append(after)
    return pl.pallas_call(
        functools.partial(_plan_kernel, window=window),
        grid_spec=pltpu.PrefetchScalarGridSpec(
            num_scalar_prefetch=1,
            grid=(n // t,),
            in_specs=in_specs,
            out_specs=pl.BlockSpec((t // window, k, window), lambda i, ps: (i, 0, 0))),
        out_shape=jax.ShapeDtypeStruct((n // window, k, window), I32),
        compiler_params=_params(("parallel",)),
        name="plan",
    )(*args)


def _block_schedule(counts, n_tok):
    padded = ((counts + MOE_TILE - 1) // MOE_TILE) * MOE_TILE
    pend = jnp.cumsum(padded)
    pstart = (pend - padded).astype(I32)
    n_slots = n_tok * TOP_K + N_EXPERTS * MOE_TILE
    nb = n_slots // MOE_TILE
    n_used = (pend[-1] // MOE_TILE).astype(I32)
    blk = jnp.arange(nb, dtype=I32)
    block_e = jnp.sum((pend[None, :] <= (jnp.minimum(blk, n_used - 1) * MOE_TILE)[:, None]).astype(I32), axis=1)
    block_e = jnp.minimum(block_e, N_EXPERTS - 1)
    eid = jnp.arange(N_EXPERTS, dtype=I32)
    active = padded > 0
    run_of = jnp.cumsum(active.astype(I32)) - 1
    later = jnp.where(jnp.logical_and(active[None, :], eid[None, :] > eid[:, None]), eid[None, :], N_EXPERTS)
    next_e = jnp.min(later, axis=1)
    next_e = jnp.where(next_e < N_EXPERTS, next_e, -1)
    onehot = block_e[:, None] == eid[None, :]
    lookup = lambda table: jnp.sum(jnp.where(onehot, table[None, :], 0), axis=1).astype(I32)
    run_first = jnp.logical_and(blk * MOE_TILE == lookup(pstart), blk < n_used).astype(I32)
    schedule = (block_e, run_first, lookup(run_of % 2), lookup(next_e), n_used.reshape(1))
    return pstart, schedule, n_slots


def kernel(x, mem, positions, g_mix, w_in, a_re, a_im, log_dt, b_re, b_im, c_re, c_im, d_skip, w_glu, g_ssm_out, lam_q1, lam_k1, lam_q2, lam_k2, g_sub, w_out, g_x, g_mem, wq_x, wk_x, wv_x, wo_x, g_ffn, w_router, router_bias, w_gate, w_up, w_down, ws_gate, ws_up, ws_down, g_final):
    b, l, d = x.shape
    n = b * l
    depth = w_in.shape[0]
    row = lambda a: a.reshape(1, -1).astype(F32)
    inv = ROPE_THETA ** (-jnp.arange(0, DA_QKDIM, 2, dtype=F32) / DA_QKDIM)
    inv_row = jnp.tile(inv, LANES // inv.shape[0]).reshape(1, LANES)
    pos3 = positions.reshape(b, l, 1)

    h = x
    for i in range(depth):
        lambda_init = 0.8 - 0.6 * math.exp(-0.3 * i)
        u, q, k, v = _inproj(h, pos3, row(g_mix[i]), w_in[i].astype(BF16), inv_row)

        ssm_mats = _ssm_matrices(a_re[i], a_im[i], log_dt[i], b_re[i], b_im[i], c_re[i], c_im[i])
        y_ssm = _ssm(u, ssm_mats, row(d_skip[i]), w_glu[i].astype(BF16), row(g_ssm_out[i]))

        lam = (jnp.exp(jnp.sum(lam_q1[i].astype(F32) * lam_k1[i].astype(F32)))
               - jnp.exp(jnp.sum(lam_q2[i].astype(F32) * lam_k2[i].astype(F32))) + lambda_init).reshape(1)
        kmem, vmem = _memkv(mem, row(g_mem[i]), wk_x[i].astype(BF16), wv_x[i].astype(BF16))
        wo = w_out[i].astype(BF16)
        wsgu = jnp.concatenate([ws_gate[i], ws_up[i]], axis=1).astype(BF16)
        wr = w_router[i].T.astype(F32)
        wr_hi = wr.astype(BF16)
        wr_split = jnp.concatenate([wr_hi, (wr - wr_hi.astype(F32)).astype(BF16)], axis=0)

        nb = b // MOE_PARTS
        n_part = nb * l

        def attend(part):
            return _diff_attention(q, k, v, lam.astype(F32), row(g_sub[i]), lambda_init, part * nb, nb)

        def route_and_dispatch(part, y_att, mix_after=None, plan_after=None):
            h2 = _mix(h, y_ssm, y_att, wo[:SSM_WIDTH], wo[SSM_WIDTH:], row(g_x[i]), wq_x[i].astype(BF16),
                      kmem, vmem, wo_x[i].astype(BF16), part * nb, mix_after)
            t_pk, base, eidx, gw, rank, counts = _route(
                h2.reshape(n_part, d), row(g_ffn[i]), wr_split,
                router_bias[i].reshape(N_EXPERTS, 1).astype(F32), wsgu, ws_down[i].astype(BF16), 0, 1)
            pstart, schedule, n_slots = _block_schedule(counts[:, 0], n_part)
            dest3 = _plan(pstart, eidx, rank, SC_WINDOW, plan_after)
            xs = _sc_dispatch(t_pk, dest3, n_slots)
            return base, gw, schedule, dest3, xs

        def experts_and_combine(state):
            base, gw, schedule, dest3, xs = state
            ys = _experts(schedule, xs, w_gate[i], w_up[i], w_down[i])
            return base, gw, _sc_combine(ys, dest3), ys

        assert MOE_PARTS == 2
        state_a = route_and_dispatch(0, attend(0))
        y_att_b = attend(1)
        done_a = experts_and_combine(state_a)
        state_b = route_and_dispatch(1, y_att_b, mix_after=done_a[3], plan_after=done_a[2])
        out = _final(done_a[0], done_a[2], done_a[1], row(g_final), i == depth - 1, None, 0, MOE_PARTS)
        done_b = experts_and_combine(state_b)
        out = _final(done_b[0], done_b[2], done_b[1], row(g_final), i == depth - 1, out, 1, MOE_PARTS)
        h = out.reshape(b, l, d)
    return h
```

```python
import functools
import math

import jax
import jax.numpy as jnp
from jax import lax
from jax.experimental import pallas as pl
from jax.experimental.pallas import tpu as pltpu
from jax.experimental.pallas import tpu_sc as plsc

F32 = jnp.float32
BF16 = jnp.bfloat16
I32 = jnp.int32

SSM_WIDTH = 512
ATTN_WIDTH = 512
SSM_GROUP = 16
SSM_GROUPS = 32
SSM_STATE = 64
N_STATE = SSM_GROUPS * SSM_STATE
DA_HEADS = 4
DA_VDIM = 128
DA_QKDIM = 64
ROPE_THETA = 10000.0
ROPE_FREQS = DA_QKDIM // 2
X_HEADS = 4
X_HEAD_DIM = 256
N_EXPERTS = 64
TOP_K = 8
N_EXPERT_GROUPS = 8
PER_GROUP = N_EXPERTS // N_EXPERT_GROUPS
TOPK_GROUPS = 4
D_EXPERT = 256
ROUTED_SCALE = 2.5
EPS = 1e-6

VMEM_LIMIT_V7X = 56 * 1024 * 1024
SUBLANES = 8
LANES = 128
HALF_WORD_BITS = 16
HIGH_HALF_MASK = -(1 << HALF_WORD_BITS)

IN_TILE = 1024
SSM_STEPS = 128
SSM_COLS = 1024
SSM_CH = 128
SSM_SLAB = 512
ATT_TILE = 2048
ATT_ROWS = 256
MIX_TILE = 1024
ROUTE_TILE = 1024
MOE_TILE = 512
X_SLOTS = 3
PLAN_TILE = 2048
MOE_PARTS = 2
FIN_TILE = 512

SC_CORES = 2
SC_WORKERS = 32
SC_WINDOW = 64


def _params(sem):
    return pltpu.CompilerParams(dimension_semantics=sem, vmem_limit_bytes=VMEM_LIMIT_V7X)


def _rms(x, g):
    return x * lax.rsqrt(jnp.mean(x * x, axis=-1, keepdims=True) + EPS) * g


def _full(shape):
    return pl.BlockSpec(shape, lambda *_: (0,) * len(shape))


def _pack_rows(a):
    w = a.shape[1] // 2
    bits = lambda v: lax.bitcast_convert_type(v.astype(BF16).astype(F32), I32)
    return (bits(a[:, w:]) & jnp.int32(HIGH_HALF_MASK)) | lax.shift_right_logical(bits(a[:, :w]), HALF_WORD_BITS)


def _unpack_rows(p):
    lo = lax.bitcast_convert_type(lax.shift_left(p, HALF_WORD_BITS), F32)
    hi = lax.bitcast_convert_type(p & jnp.int32(HIGH_HALF_MASK), F32)
    return jnp.concatenate([lo, hi], axis=1)


def _inproj_kernel(x_ref, pos_ref, g_ref, w_ref, inv_ref, u_ref, q_ref, k_ref, v_ref):
    x = x_ref[0]
    a = _rms(x, g_ref[...]).astype(BF16)
    z = jnp.dot(a, w_ref[...], preferred_element_type=F32)
    u_ref[0] = z[:, :SSM_WIDTH]
    n_grp = LANES // ROPE_FREQS
    tq = x.shape[0] // n_grp
    pos = pos_ref[0].astype(F32)
    lane_c = lax.broadcasted_iota(I32, (tq, LANES), 1)
    pos_c = jnp.zeros((tq, LANES), F32)
    for j in range(n_grp):
        pos_c = jnp.where(lane_c // ROPE_FREQS == j, pos[j * tq:(j + 1) * tq], pos_c)
    ang = pos_c * inv_ref[...]

    def spread(c):
        rows = []
        for j in range(n_grp):
            y = c if j == 0 else pltpu.roll(c, LANES - ROPE_FREQS * j, 1)
            w = ROPE_FREQS
            while w < LANES:
                y = jnp.where(lane_c < w, y, pltpu.roll(y, w, 1))
                w *= 2
            rows.append(y)
        return jnp.concatenate(rows, axis=0)

    cos = jnp.concatenate([spread(jnp.cos(ang))] * (ATTN_WIDTH // LANES), axis=1)
    sin = jnp.concatenate([spread(jnp.sin(ang))] * (ATTN_WIDTH // LANES), axis=1)
    lane = lax.broadcasted_iota(I32, cos.shape, 1)
    first = (lane & (DA_QKDIM - 1)) < DA_QKDIM // 2
    half = DA_QKDIM // 2

    def rope(t):
        rot = jnp.where(first, -pltpu.roll(t, ATTN_WIDTH - half, 1), pltpu.roll(t, half, 1))
        return t * cos + rot * sin

    q = z[:, SSM_WIDTH:SSM_WIDTH + ATTN_WIDTH]
    k = z[:, SSM_WIDTH + ATTN_WIDTH:SSM_WIDTH + 2 * ATTN_WIDTH]
    q_ref[0] = (rope(q) * (DA_QKDIM ** -0.5)).astype(BF16)
    k_ref[0] = rope(k).astype(BF16)
    v_ref[0] = z[:, SSM_WIDTH + 2 * ATTN_WIDTH:].astype(BF16)


def _inproj(x, pos3, g_mix, w_in, inv_row):
    b, l, d = x.shape
    t = min(IN_TILE, l)
    n_out = w_in.shape[1]
    row = lambda w: pl.BlockSpec((1, t, w), lambda i, j: (i, j, 0))
    return pl.pallas_call(
        _inproj_kernel,
        grid=(b, l // t),
        in_specs=[row(d), row(1), _full((1, d)), _full((d, n_out)), _full((1, LANES))],
        out_specs=[row(SSM_WIDTH), row(ATTN_WIDTH), row(ATTN_WIDTH), row(ATTN_WIDTH)],
        out_shape=[jax.ShapeDtypeStruct((b, l, SSM_WIDTH), F32)]
        + [jax.ShapeDtypeStruct((b, l, ATTN_WIDTH), BF16)] * 3,
        compiler_params=_params(("parallel", "parallel")),
        name="inproj",
    )(x, pos3, g_mix, w_in, inv_row)


def _ssm_kernel(u_ref, bm_ref, are_ref, aim_ref, cm_ref, dm_ref, dskip_ref, wglu_ref, g_ref, o_ref,
                c_ref, st_ref, up_ref, *, steps):
    @pl.when(pl.program_id(0) == 0)
    def _():
        st_ref[...] = jnp.zeros_like(st_ref)
        up_ref[...] = jnp.zeros_like(up_ref)

    pairs = steps // 2
    rows = pairs * SUBLANES
    u = pltpu.einshape("bts->tbs", u_ref[...]).reshape(pairs, 2, SUBLANES, SSM_WIDTH)
    u_even = u[:, 0].reshape(rows, SSM_WIDTH)
    u_odd = u[:, 1].reshape(rows, SSM_WIDTH)
    u_before = jnp.concatenate([up_ref[...], u_odd[:rows - SUBLANES]], axis=0)
    up_ref[...] = u_odd[rows - SUBLANES:]
    ue, uo, ub = u_even.astype(BF16), u_odd.astype(BF16), u_before.astype(BF16)

    n_slabs = SSM_WIDTH // SSM_CH
    for s in range(n_slabs):
        ch = slice(s * SSM_CH, (s + 1) * SSM_CH)
        lhs = jnp.concatenate([ue[:, ch], ub[:, ch]], axis=1)
        for part in (0, 1):
            cols = slice(part * N_STATE + s * SSM_SLAB, part * N_STATE + (s + 1) * SSM_SLAB)
            c_ref[:, cols] = jnp.dot(lhs, bm_ref[s, :, part * SSM_SLAB:(part + 1) * SSM_SLAB],
                                     preferred_element_type=F32)

    for c0 in range(0, N_STATE, SSM_COLS):
        re = slice(c0, c0 + SSM_COLS)
        im = slice(N_STATE + c0, N_STATE + c0 + SSM_COLS)
        ar = are_ref[:, re]
        ai = aim_ref[:, re]
        sr, si = st_ref[:, re], st_ref[:, im]
        for k in range(pairs):
            blk = slice(k * SUBLANES, (k + 1) * SUBLANES)
            sr, si = ar * sr - ai * si + c_ref[blk, re], ar * si + ai * sr + c_ref[blk, im]
            c_ref[blk, re] = sr
            c_ref[blk, im] = si
        st_ref[:, re] = sr
        st_ref[:, im] = si

    y_even, y_odd = [], []
    for s in range(n_slabs):
        ch = slice(s * SSM_CH, (s + 1) * SSM_CH)
        acc = jnp.dot(uo[:, ch], dm_ref[s], preferred_element_type=F32)
        both = None
        for part in (0, 1):
            cols = slice(part * N_STATE + s * SSM_SLAB, part * N_STATE + (s + 1) * SSM_SLAB)
            d = jnp.dot(c_ref[:, cols].astype(BF16), cm_ref[s, part * SSM_SLAB:(part + 1) * SSM_SLAB],
                        preferred_element_type=F32)
            both = d if both is None else both + d
        y_even.append(both[:, :SSM_CH])
        y_odd.append(both[:, SSM_CH:] + acc)
    y = jnp.concatenate([jnp.concatenate(y_even, axis=1), jnp.concatenate(y_odd, axis=1)], axis=0)
    y = y + dskip_ref[...] * jnp.concatenate([u_even, u_odd], axis=0)
    y = jax.nn.gelu(y)
    y = y * jax.nn.sigmoid(jnp.dot(y.astype(BF16), wglu_ref[...], preferred_element_type=F32))
    out = _rms(y, g_ref[...])
    out = jnp.stack([out[:rows].reshape(pairs, SUBLANES, SSM_WIDTH),
                     out[rows:].reshape(pairs, SUBLANES, SSM_WIDTH)], axis=1).reshape(steps, SUBLANES, SSM_WIDTH)
    o_ref[...] = pltpu.einshape("tbs->bts", out).astype(BF16)


def _ssm(u, mats, d_skip, w_glu, g_out):
    bmat, a2_re8, a2_im8, cmat, dmat = mats
    batch, l, _ = u.shape
    assert batch == SUBLANES, "one time step of all sequences must fill the sublanes"
    steps = min(SSM_STEPS, l)
    assert steps % 2 == 0
    rows = steps // 2 * batch
    n_slabs = SSM_WIDTH // SSM_CH
    seq = pl.BlockSpec((batch, steps, SSM_WIDTH), lambda i: (0, i, 0))
    return pl.pallas_call(
        functools.partial(_ssm_kernel, steps=steps),
        grid=(l // steps,),
        in_specs=[seq,
                  _full((n_slabs, 2 * SSM_CH, 2 * SSM_SLAB)), _full((SUBLANES, N_STATE)), _full((SUBLANES, N_STATE)),
                  _full((n_slabs, 2 * SSM_SLAB, 2 * SSM_CH)), _full((n_slabs, SSM_CH, SSM_CH)),
                  _full((1, SSM_WIDTH)), _full((SSM_WIDTH, SSM_WIDTH)), _full((1, SSM_WIDTH))],
        out_specs=seq,
        out_shape=jax.ShapeDtypeStruct((batch, l, SSM_WIDTH), BF16),
        scratch_shapes=[pltpu.VMEM((rows, 2 * N_STATE), F32), pltpu.VMEM((SUBLANES, 2 * N_STATE), F32),
                        pltpu.VMEM((SUBLANES, SSM_WIDTH), F32)],
        compiler_params=_params(("arbitrary",)),
        name="ssm",
    )(u, bmat, a2_re8, a2_im8, cmat, dmat, d_skip, w_glu, g_out)


def _attn_kernel(qi_ref, ki_ref, lam_ref, q_ref, k_ref, v_ref, g_ref, o_ref, qs_ref, m_ref, acc_ref, *,
                 tile, out_scale):
    qi = qi_ref[pl.program_id(2)]
    ki = ki_ref[pl.program_id(2)]

    @pl.when(ki == 0)
    def _():
        q = q_ref[0]
        lane = lax.broadcasted_iota(I32, q.shape, 1)
        zero = jnp.zeros_like(q)
        qs_ref[:tile] = jnp.where(lane < DA_QKDIM, q, zero)
        qs_ref[tile:] = jnp.where(lane >= DA_QKDIM, q, zero)
        m_ref[...] = jnp.full_like(m_ref, -jnp.inf)
        acc_ref[...] = jnp.zeros_like(acc_ref)

    def update(masked):
        v_ext = jnp.concatenate([v_ref[0], jnp.ones((tile, DA_VDIM), BF16)], axis=1)
        rb = min(ATT_ROWS, tile)
        n_rb = 2 * tile // rb

        def n_keys(r):
            return (r * rb) % tile + rb if masked else tile

        def scores(r):
            return lax.dot_general(qs_ref[r * rb:(r + 1) * rb], k_ref[0, :n_keys(r)], (((1,), (1,)), ((), ())),
                                   preferred_element_type=F32)

        s_next = scores(0)
        for r in range(n_rb):
            s = s_next
            if r + 1 < n_rb:
                s_next = scores(r + 1)
            rows = slice(r * rb, (r + 1) * rb)
            nk = n_keys(r)
            if masked:
                tri = lax.broadcasted_iota(I32, (rb, rb), 1) <= lax.broadcasted_iota(I32, (rb, rb), 0)
                edge = jnp.where(tri, s[:, nk - rb:], jnp.finfo(F32).min)
                s = edge if nk == rb else jnp.concatenate([s[:, :nk - rb], edge], axis=1)
            m_old = m_ref[rows]
            m_new = jnp.maximum(m_old, jnp.max(s, axis=1, keepdims=True))
            p = jnp.exp(s - jnp.concatenate([m_new] * (nk // DA_VDIM), axis=1))
            alpha = jnp.exp(m_old - m_new)
            acc_ref[rows] = (jnp.concatenate([alpha, alpha], axis=1) * acc_ref[rows]
                             + jnp.dot(p.astype(BF16), v_ext[:nk], preferred_element_type=F32))
            m_ref[rows] = m_new

    @pl.when(ki < qi)
    def _():
        update(False)

    @pl.when(ki == qi)
    def _():
        update(True)
        o = acc_ref[:, :DA_VDIM] / acc_ref[:, DA_VDIM:]
        o = o[:tile] - lam_ref[0] * o[tile:]
        o_ref[0] = (_rms(o, g_ref[...]) * out_scale).astype(BF16)


def _diff_attention(q, k, v, lam, g_sub, lambda_init, b0, nb):
    _, l, _ = q.shape
    t = min(ATT_TILE, l)
    n = l // t
    pairs = [(qi, ki) for qi in range(n) for ki in range(qi + 1)]
    qi_tab = jnp.asarray([p[0] for p in pairs], I32)
    ki_tab = jnp.asarray([p[1] for p in pairs], I32)
    qspec = pl.BlockSpec((1, t, DA_VDIM), lambda bi, h, s, qt, kt: (bi + b0, qt[s], h))
    kspec = pl.BlockSpec((1, t, DA_VDIM), lambda bi, h, s, qt, kt: (bi + b0, kt[s], h))
    ospec = pl.BlockSpec((1, t, DA_VDIM), lambda bi, h, s, qt, kt: (bi, qt[s], h))
    return pl.pallas_call(
        functools.partial(_attn_kernel, tile=t, out_scale=1.0 - lambda_init),
        grid_spec=pltpu.PrefetchScalarGridSpec(
            num_scalar_prefetch=2,
            grid=(nb, DA_HEADS, len(pairs)),
            in_specs=[pl.BlockSpec(memory_space=pltpu.SMEM), qspec, kspec, kspec,
                      pl.BlockSpec((1, DA_VDIM), lambda bi, h, s, qt, kt: (0, 0))],
            out_specs=ospec,
            scratch_shapes=[pltpu.VMEM((2 * t, DA_VDIM), BF16), pltpu.VMEM((2 * t, DA_VDIM), F32),
                            pltpu.VMEM((2 * t, 2 * DA_VDIM), F32)]),
        out_shape=jax.ShapeDtypeStruct((nb, l, ATTN_WIDTH), BF16),
        compiler_params=_params(("parallel", "parallel", "arbitrary")),
        name="diffattn",
    )(qi_tab, ki_tab, lam, q, k, v, g_sub)


def _memkv_kernel(m_ref, g_ref, wk_ref, wv_ref, k_ref, v_ref):
    a = _rms(m_ref[0], g_ref[...]).astype(BF16)
    k_ref[0] = jnp.dot(a, wk_ref[...], preferred_element_type=F32).astype(BF16)
    v_ref[0] = jnp.dot(a, wv_ref[...], preferred_element_type=F32).astype(BF16)


def _memkv(mem, g_mem, wk, wv):
    b, m, d = mem.shape
    blk = pl.BlockSpec((1, m, d), lambda i: (i, 0, 0))
    return pl.pallas_call(
        _memkv_kernel,
        grid=(b,),
        in_specs=[blk, _full((1, d)), _full((d, d)), _full((d, d))],
        out_specs=[blk, blk],
        out_shape=[jax.ShapeDtypeStruct((b, m, d), BF16)] * 2,
        compiler_params=_params(("parallel",)),
        name="memkv",
    )(mem, g_mem, wk, wv)


def _mix_kernel(x_ref, ys_ref, ya_ref, wo1_ref, wo2_ref, gx_ref, wq_ref, km_ref, vm_ref, wo_ref, *rest):
    h_ref = rest[-1]
    h = (x_ref[0]
         + jnp.dot(ys_ref[0], wo1_ref[...], preferred_element_type=F32)
         + jnp.dot(ya_ref[0], wo2_ref[...], preferred_element_type=F32))
    hq = _rms(h, gx_ref[...]).astype(BF16)
    q = jnp.dot(hq, wq_ref[...], preferred_element_type=F32).astype(BF16)
    outs = []
    for hd in range(X_HEADS):
        cols = slice(hd * X_HEAD_DIM, (hd + 1) * X_HEAD_DIM)
        s = lax.dot_general(q[:, cols], km_ref[0, :, cols], (((1,), (1,)), ((), ())),
                            preferred_element_type=F32) * (X_HEAD_DIM ** -0.5)
        s = s - jnp.max(s, axis=1, keepdims=True)
        p = jnp.exp(s)
        p = p / jnp.sum(p, axis=1, keepdims=True)
        outs.append(jnp.dot(p.astype(BF16), vm_ref[0, :, cols], preferred_element_type=F32).astype(BF16))
    o = jnp.concatenate(outs, axis=1)
    h_ref[0] = h + jnp.dot(o, wo_ref[...], preferred_element_type=F32)


def _mix(x, y_ssm, y_att, wo1, wo2, g_x, wq, kmem, vmem, wo, b0, after=None):
    _, l, d = x.shape
    nb = y_att.shape[0]
    t = min(MIX_TILE, l)
    m = kmem.shape[1]
    row = lambda w: pl.BlockSpec((1, t, w), lambda i, j: (i, j, 0))
    full_row = lambda w: pl.BlockSpec((1, t, w), lambda i, j: (i + b0, j, 0))
    mem = pl.BlockSpec((1, m, d), lambda i, j: (i + b0, 0, 0))
    in_specs = [full_row(d), full_row(SSM_WIDTH), row(ATTN_WIDTH), _full((SSM_WIDTH, d)), _full((ATTN_WIDTH, d)),
                _full((1, d)), _full((d, d)), mem, mem, _full((d, d))]
    args = [x, y_ssm, y_att, wo1, wo2, g_x, wq, kmem, vmem, wo]
    if after is not None:
        in_specs.append(pl.BlockSpec(memory_space=pl.ANY))
        args.append(after)
    return pl.pallas_call(
        _mix_kernel,
        grid=(nb, l // t),
        in_specs=in_specs,
        out_specs=row(d),
        out_shape=jax.ShapeDtypeStruct((nb, l, d), F32),
        compiler_params=_params(("parallel", "parallel")),
        name="mix",
    )(*args)


def _first_index(hit, idx, sentinel):
    return jnp.min(jnp.where(hit, idx, sentinel), axis=0, keepdims=True)


def _route_kernel(h_ref, g_ref, wr_ref, bias_ref, wsgu_ref, wsd_ref, tri_ref,
                  t_ref, base_ref, eidx_ref, gw_ref, rank_ref, cnt_ref, carry_ref):
    @pl.when(pl.program_id(0) == 0)
    def _():
        carry_ref[...] = jnp.zeros_like(carry_ref)

    h = h_ref[...]
    t = _rms(h, g_ref[...])
    tb = t.astype(BF16)
    t_ref[...] = _pack_rows(t)
    gu = jnp.dot(tb, wsgu_ref[...], preferred_element_type=F32)
    hid = jax.nn.silu(gu[:, :D_EXPERT]) * gu[:, D_EXPERT:]
    base_ref[...] = h + jnp.dot(hid.astype(BF16), wsd_ref[...], preferred_element_type=F32)

    nt = (((1,), (1,)), ((), ()))
    t_lo = (t - tb.astype(F32)).astype(BF16)
    both = lax.dot_general(wr_ref[...], tb, nt, preferred_element_type=F32)
    logits = (both[:N_EXPERTS] + both[N_EXPERTS:]
              + lax.dot_general(wr_ref[:N_EXPERTS], t_lo, nt, preferred_element_type=F32))
    scores = jax.nn.sigmoid(logits)
    biased = scores + bias_ref[...]
    n_tok = scores.shape[1]
    neg = -jnp.inf
    sub = lax.broadcasted_iota(I32, (PER_GROUP, n_tok), 0)

    gs = []
    for g in range(N_EXPERT_GROUPS):
        blk = biased[g * PER_GROUP:(g + 1) * PER_GROUP]
        m1 = jnp.max(blk, axis=0, keepdims=True)
        i1 = _first_index(blk == m1, sub, PER_GROUP)
        m2 = jnp.max(jnp.where(sub == i1, neg, blk), axis=0, keepdims=True)
        gs.append(m1 + m2)
    gs = jnp.concatenate(gs, axis=0)

    gsel = jnp.zeros(gs.shape, jnp.bool_)
    for _ in range(TOPK_GROUPS):
        m = jnp.max(gs, axis=0, keepdims=True)
        hit = sub == _first_index(gs == m, sub, N_EXPERT_GROUPS)
        gsel = jnp.logical_or(gsel, hit)
        gs = jnp.where(hit, neg, gs)

    masked = jnp.concatenate(
        [jnp.where(gsel[g:g + 1], biased[g * PER_GROUP:(g + 1) * PER_GROUP], neg)
         for g in range(N_EXPERT_GROUPS)], axis=0)
    eid = lax.broadcasted_iota(I32, masked.shape, 0)
    sel = jnp.zeros(masked.shape, jnp.bool_)
    idxs, gws = [], []
    for _ in range(TOP_K):
        m = jnp.max(masked, axis=0, keepdims=True)
        i = _first_index(masked == m, eid, N_EXPERTS)
        hit = eid == i
        idxs.append(i)
        gws.append(jnp.sum(jnp.where(hit, scores, 0.0), axis=0, keepdims=True))
        sel = jnp.logical_or(sel, hit)
        masked = jnp.where(hit, neg, masked)
    eidx = jnp.concatenate(idxs, axis=0)
    gw = jnp.concatenate(gws, axis=0)
    gw = gw / jnp.sum(gw, axis=0, keepdims=True) * ROUTED_SCALE
    eidx_ref[...] = eidx
    gw_ref[...] = gw

    before = jnp.dot(sel.astype(BF16), tri_ref[...], preferred_element_type=F32) + carry_ref[...]
    rank_ref[...] = jnp.concatenate(
        [jnp.sum(jnp.where(eid == idxs[k], before, 0.0), axis=0, keepdims=True) for k in range(TOP_K)],
        axis=0).astype(I32)
    carry = carry_ref[...] + jnp.sum(sel.astype(F32), axis=1, keepdims=True)
    carry_ref[...] = carry
    cnt_ref[...] = carry.astype(I32)


def _route(h2, g_ffn, wr_t, bias_col, wsgu, wsd, part, n_parts):
    n_all, d = h2.shape
    n = n_all // n_parts
    t = min(ROUTE_TILE, n)
    blk0 = part * (n // t)
    tri = (lax.broadcasted_iota(I32, (t, t), 0) < lax.broadcasted_iota(I32, (t, t), 1)).astype(BF16)
    row = lambda w: pl.BlockSpec((t, w), lambda i: (i, 0))
    col = pl.BlockSpec((TOP_K, t), lambda i: (0, i))
    return pl.pallas_call(
        _route_kernel,
        grid=(n // t,),
        in_specs=[pl.BlockSpec((t, d), lambda i: (i + blk0, 0)), _full((1, d)), _full((2 * N_EXPERTS, d)),
                  _full((N_EXPERTS, 1)),
                  _full((d, 2 * D_EXPERT)), _full((D_EXPERT, d)), _full((t, t))],
        out_specs=[row(d // 2), row(d), col, col, col, _full((N_EXPERTS, 1))],
        out_shape=[jax.ShapeDtypeStruct((n, d // 2), I32), jax.ShapeDtypeStruct((n, d), F32),
                   jax.ShapeDtypeStruct((TOP_K, n), I32), jax.ShapeDtypeStruct((TOP_K, n), F32),
                   jax.ShapeDtypeStruct((TOP_K, n), I32), jax.ShapeDtypeStruct((N_EXPERTS, 1), I32)],
        scratch_shapes=[pltpu.VMEM((N_EXPERTS, 1), F32)],
        compiler_params=_params(("arbitrary",)),
        name="route",
    )(h2, g_ffn, wr_t, bias_col, wsgu, wsd, tri)


def _expert_kernel(be_ref, first_ref, slot_ref, next_ref, nu_ref, x_hbm, wg_hbm, wu_hbm, wd_hbm, after_hbm, y_ref,
                   wg_s, wu_s, wd_s, x_buf, x_sem, wg_buf, wu_buf, wd_buf, w_sem):
    i = pl.program_id(0)
    n_used = nu_ref[0]

    def fetch(step):
        slot = step % X_SLOTS
        rows = pl.ds(pl.multiple_of(step * MOE_TILE, MOE_TILE), MOE_TILE)
        return pltpu.make_async_copy(x_hbm.at[rows], x_buf.at[slot], x_sem.at[slot])

    def weight_fetch(expert, slot):
        pairs = ((wg_hbm, wg_buf), (wu_hbm, wu_buf), (wd_hbm, wd_buf))
        return [pltpu.make_async_copy(src.at[expert], buf.at[slot], w_sem.at[slot, j])
                for j, (src, buf) in enumerate(pairs)]

    @pl.when(i == 0)
    def _():
        for j in range(X_SLOTS - 1):
            @pl.when(j < n_used)
            def _(j=j):
                fetch(j).start()
        for cp in weight_fetch(be_ref[0], 0):
            cp.start()

    @pl.when(i + (X_SLOTS - 1) < n_used)
    def _():
        fetch(i + (X_SLOTS - 1)).start()

    @pl.when(first_ref[i] == 1)
    def _():
        slot = slot_ref[i]

        @pl.when(next_ref[i] >= 0)
        def _():
            for cp in weight_fetch(next_ref[i], 1 - slot):
                cp.start()

        for cp in weight_fetch(be_ref[i], slot):
            cp.wait()
        wg_s[...] = wg_buf[slot].astype(BF16)
        wu_s[...] = wu_buf[slot].astype(BF16)
        wd_s[...] = wd_buf[slot].astype(BF16)

    @pl.when(i < n_used)
    def _():
        fetch(i).wait()
        x = _unpack_rows(x_buf[i % X_SLOTS]).astype(BF16)
        gate = jnp.dot(x, wg_s[...], preferred_element_type=F32)
        up = jnp.dot(x, wu_s[...], preferred_element_type=F32)
        hid = (jax.nn.silu(gate) * up).astype(BF16)
        y_ref[...] = _pack_rows(jnp.dot(hid, wd_s[...], preferred_element_type=F32))


def _experts(schedule, xs, w_gate, w_up, w_down, after):
    block_e, run_first, run_slot, run_next, n_used = schedule
    n_slots, dw = xs.shape
    _, d, de = w_gate.shape
    nb = n_slots // MOE_TILE
    hbm = pl.BlockSpec(memory_space=pl.ANY)
    return pl.pallas_call(
        _expert_kernel,
        grid_spec=pltpu.PrefetchScalarGridSpec(
            num_scalar_prefetch=5,
            grid=(nb,),
            in_specs=[hbm, hbm, hbm, hbm, hbm],
            out_specs=pl.BlockSpec((MOE_TILE, dw), lambda i, be, fi, sl, nx, nu: (jnp.minimum(i, nu[0] - 1), 0)),
            scratch_shapes=[pltpu.VMEM((d, de), BF16), pltpu.VMEM((d, de), BF16), pltpu.VMEM((de, d), BF16),
                            pltpu.VMEM((X_SLOTS, MOE_TILE, dw), I32), pltpu.SemaphoreType.DMA((X_SLOTS,)),
                            pltpu.VMEM((2, d, de), F32), pltpu.VMEM((2, d, de), F32), pltpu.VMEM((2, de, d), F32),
                            pltpu.SemaphoreType.DMA((2, 3))]),
        out_shape=jax.ShapeDtypeStruct((n_slots, dw), I32),
        compiler_params=_params(("arbitrary",)),
        name="experts",
    )(block_e, run_first, run_slot, run_next, n_used, xs, w_gate, w_up, w_down, after)


def _sc_worker_id():
    return lax.axis_index("s") * SC_CORES + lax.axis_index("c")


def _sc_dispatch(t_rows, dest3, n_slots):
    _, dw = t_rows.shape
    n_chunks, _, w = dest3.shape
    per_worker = n_chunks // SC_WORKERS
    assert per_worker % 2 == 0
    mesh = plsc.VectorSubcoreMesh(core_axis_name="c", subcore_axis_name="s")
    dt = t_rows.dtype

    @functools.partial(
        pl.kernel, mesh=mesh,
        out_type=jax.ShapeDtypeStruct((n_slots, dw), dt),
        scratch_types=[pltpu.VMEM((TOP_K, w), I32), pltpu.VMEM((TOP_K, w), I32),
                       pltpu.VMEM((w, dw), dt), pltpu.VMEM((w, dw), dt),
                       pltpu.SemaphoreType.DMA, pltpu.SemaphoreType.DMA, pltpu.SemaphoreType.DMA],
    )
    def scatter_rows(t_hbm, dest_hbm, xs_hbm, idx_a, idx_b, rows_a, rows_b, sem_load, sem_a, sem_b):
        wid = _sc_worker_id()

        def scatter(idx_v, rows_v, sem):
            return [pltpu.async_copy(rows_v, xs_hbm.at[idx_v.at[k]], sem) for k in range(TOP_K)]

        @pl.loop(0, per_worker, step=2)
        def _(j):
            ca = wid * per_worker + j
            cb = ca + 1
            pltpu.sync_copy(dest_hbm.at[ca], idx_a)
            pltpu.sync_copy(t_hbm.at[pl.ds(ca * w, w)], rows_a)
            load_idx = pltpu.async_copy(dest_hbm.at[cb], idx_b, sem_load)
            load_rows = pltpu.async_copy(t_hbm.at[pl.ds(cb * w, w)], rows_b, sem_load)
            out_a = scatter(idx_a, rows_a, sem_a)
            load_idx.wait()
            load_rows.wait()
            out_b = scatter(idx_b, rows_b, sem_b)
            for cp in out_a + out_b:
                cp.wait()

    return scatter_rows(t_rows, dest3)


def _sc_combine(y_rows, dest3):
    _, dw = y_rows.shape
    n_chunks, _, w = dest3.shape
    per_worker = n_chunks // SC_WORKERS
    mesh = plsc.VectorSubcoreMesh(core_axis_name="c", subcore_axis_name="s")
    dt = y_rows.dtype

    @functools.partial(
        pl.kernel, mesh=mesh,
        out_type=jax.ShapeDtypeStruct((TOP_K, n_chunks * w, dw), dt),
        scratch_types=[pltpu.VMEM((TOP_K, w), I32), pltpu.VMEM((w, dw), dt), pltpu.VMEM((w, dw), dt),
                       pltpu.SemaphoreType.DMA, pltpu.SemaphoreType.DMA,
                       pltpu.SemaphoreType.DMA, pltpu.SemaphoreType.DMA],
    )
    def gather_rows(y_hbm, dest_hbm, out_hbm, idx_v, buf0, buf1, gsem0, gsem1, wsem0, wsem1):
        wid = _sc_worker_id()
        bufs, gsems, wsems = (buf0, buf1), (gsem0, gsem1), (wsem0, wsem1)

        @pl.loop(0, per_worker)
        def _(j):
            c = wid * per_worker + j
            pltpu.sync_copy(dest_hbm.at[c], idx_v)
            gathers = [None] * TOP_K
            writes = [None] * TOP_K
            gathers[0] = pltpu.async_copy(y_hbm.at[idx_v.at[0]], bufs[0], gsems[0])
            for k in range(TOP_K):
                b = k % 2
                if k + 1 < TOP_K:
                    if k >= 1:
                        writes[k - 1].wait()
                    gathers[k + 1] = pltpu.async_copy(y_hbm.at[idx_v.at[k + 1]], bufs[1 - b], gsems[1 - b])
                gathers[k].wait()
                writes[k] = pltpu.async_copy(bufs[b], out_hbm.at[k, pl.ds(c * w, w)], wsems[b])
            writes[TOP_K - 2].wait()
            writes[TOP_K - 1].wait()

    return gather_rows(y_rows, dest3)


def _final_kernel(base_ref, y_ref, gw_ref, g_ref, *rest, normalize):
    o_ref = rest[-1]
    h = base_ref[...]
    eye = (lax.broadcasted_iota(I32, (TOP_K, TOP_K), 0) == lax.broadcasted_iota(I32, (TOP_K, TOP_K), 1)).astype(F32)
    gw = lax.dot_general(gw_ref[...], eye, (((0,), (0,)), ((), ())), preferred_element_type=F32,
                         precision=lax.Precision.HIGHEST)
    for k in range(TOP_K):
        h = h + gw[:, k:k + 1] * _unpack_rows(y_ref[k])
    o_ref[...] = _rms(h, g_ref[...]) if normalize else h


def _final(base, yg, gw, g_final, normalize, out_prev, part, n_parts):
    n, d = base.shape
    t = min(FIN_TILE, n)
    blk0 = part * (n // t)
    in_specs = [pl.BlockSpec((t, d), lambda i: (i, 0)), pl.BlockSpec((TOP_K, t, d // 2), lambda i: (0, i, 0)),
                pl.BlockSpec((TOP_K, t), lambda i: (0, i)), _full((1, d))]
    args = [base, yg, gw, g_final]
    aliases = {}
    if out_prev is not None:
        in_specs.append(pl.BlockSpec(memory_space=pl.ANY))
        args.append(out_prev)
        aliases = {len(args) - 1: 0}
    return pl.pallas_call(
        functools.partial(_final_kernel, normalize=normalize),
        grid=(n // t,),
        in_specs=in_specs,
        out_specs=pl.BlockSpec((t, d), lambda i: (i + blk0, 0)),
        out_shape=jax.ShapeDtypeStruct((n * n_parts, d), F32),
        input_output_aliases=aliases,
        compiler_params=_params(("parallel",)),
        name="final",
    )(*args)


def _ssm_matrices(a_re, a_im, log_dt, b_re, b_im, c_re, c_im):
    lr, li = a_re.astype(F32), a_im.astype(F32)
    dt = jnp.exp(log_dt.astype(F32))[:, None]
    mag = jnp.exp(lr * dt)
    ab_re, ab_im = mag * jnp.cos(li * dt), mag * jnp.sin(li * dt)
    den = lr * lr + li * li
    zr, zi = ab_re - 1.0, ab_im
    k_re = (zr * lr + zi * li) / den
    k_im = (zi * lr - zr * li) / den
    br, bi = b_re.astype(F32), b_im.astype(F32)
    bb_re = k_re[..., None] * br - k_im[..., None] * bi
    bb_im = k_re[..., None] * bi + k_im[..., None] * br
    ar, ai = ab_re[..., None], ab_im[..., None]
    ab_b_re, ab_b_im = ar * bb_re - ai * bb_im, ar * bb_im + ai * bb_re
    cr, ci = c_re.astype(F32), c_im.astype(F32)
    car, cai = ab_re[:, None, :], ab_im[:, None, :]
    ca_re, ca_im = cr * car - ci * cai, cr * cai + ci * car
    direct = jnp.einsum("gcp,gpd->gdc", cr, bb_re) - jnp.einsum("gcp,gpd->gdc", ci, bb_im)
    n_slabs = SSM_WIDTH // SSM_CH
    per_slab = SSM_GROUPS // n_slabs
    eye = jnp.eye(per_slab, dtype=F32)

    def expand_blocks(bb):
        blocks = jnp.einsum("sgpc,gh->sgchp", bb.reshape(n_slabs, per_slab, SSM_STATE, SSM_GROUP), eye)
        return blocks.reshape(n_slabs, SSM_CH, SSM_SLAB)

    def contract_blocks(c):
        blocks = jnp.einsum("sgcp,gh->sgphc", c.reshape(n_slabs, per_slab, SSM_GROUP, SSM_STATE), eye)
        return blocks.reshape(n_slabs, SSM_SLAB, SSM_CH)

    bmat = jnp.concatenate([jnp.concatenate([expand_blocks(bb_re), expand_blocks(bb_im)], axis=2),
                            jnp.concatenate([expand_blocks(ab_b_re), expand_blocks(ab_b_im)], axis=2)], axis=1)
    cmat = jnp.concatenate([jnp.concatenate([contract_blocks(cr), contract_blocks(ca_re)], axis=2),
                            jnp.concatenate([-contract_blocks(ci), -contract_blocks(ca_im)], axis=2)], axis=1)
    dmat = jnp.einsum("sgdc,gh->sgdhc", direct.reshape(n_slabs, per_slab, SSM_GROUP, SSM_GROUP), eye)
    dmat = dmat.reshape(n_slabs, SSM_CH, SSM_CH)
    a2_re, a2_im = ab_re * ab_re - ab_im * ab_im, 2.0 * ab_re * ab_im
    tile = lambda a: jnp.broadcast_to(a.reshape(1, N_STATE), (SUBLANES, N_STATE))
    return bmat.astype(BF16), tile(a2_re), tile(a2_im), cmat.astype(BF16), dmat.astype(BF16)


def _plan_kernel(pstart_ref, eidx_ref, rank_ref, *rest, window):
    dest_ref = rest[-1]
    eidx = eidx_ref[...]
    dest = rank_ref[...]
    for e in range(N_EXPERTS):
        dest = dest + jnp.where(eidx == e, pstart_ref[e], 0)
    for c in range(dest.shape[1] // window):
        dest_ref[c] = dest[:, c * window:(c + 1) * window]


def _plan(pstart, eidx, rank, window, after=None):
    k, n = eidx.shape
    t = min(PLAN_TILE, n)
    cols = pl.BlockSpec((k, t), lambda i, ps: (0, i))
    in_specs, args = [cols, cols], [pstart, eidx, rank]
    if after is not None:
        in_specs.append(pl.BlockSpec(memory_space=pl.ANY))
        args.append(after)
    return pl.pallas_call(
        functools.partial(_plan_kernel, window=window),
        grid_spec=pltpu.PrefetchScalarGridSpec(
            num_scalar_prefetch=1,
            grid=(n // t,),
            in_specs=in_specs,
            out_specs=pl.BlockSpec((t // window, k, window), lambda i, ps: (i, 0, 0))),
        out_shape=jax.ShapeDtypeStruct((n // window, k, window), I32),
        compiler_params=_params(("parallel",)),
        name="plan",
    )(*args)


def _block_schedule(counts, n_tok):
    padded = ((counts + MOE_TILE - 1) // MOE_TILE) * MOE_TILE
    pend = jnp.cumsum(padded)
    pstart = (pend - padded).astype(I32)
    n_slots = n_tok * TOP_K + N_EXPERTS * MOE_TILE
    nb = n_slots // MOE_TILE
    n_used = (pend[-1] // MOE_TILE).astype(I32)
    blk = jnp.arange(nb, dtype=I32)
    block_e = jnp.sum((pend[None, :] <= (jnp.minimum(blk, n_used - 1) * MOE_TILE)[:, None]).astype(I32), axis=1)
    block_e = jnp.minimum(block_e, N_EXPERTS - 1)
    eid = jnp.arange(N_EXPERTS, dtype=I32)
    active = padded > 0
    run_of = jnp.cumsum(active.astype(I32)) - 1
    later = jnp.where(jnp.logical_and(active[None, :], eid[None, :] > eid[:, None]), eid[None, :], N_EXPERTS)
    next_e = jnp.min(later, axis=1)
    next_e = jnp.where(next_e < N_EXPERTS, next_e, -1)
    onehot = block_e[:, None] == eid[None, :]
    lookup = lambda table: jnp.sum(jnp.where(onehot, table[None, :], 0), axis=1).astype(I32)
    run_first = jnp.logical_and(blk * MOE_TILE == lookup(pstart), blk < n_used).astype(I32)
    schedule = (block_e, run_first, lookup(run_of % 2), lookup(next_e), n_used.reshape(1))
    return pstart, schedule, n_slots


def kernel(x, mem, positions, g_mix, w_in, a_re, a_im, log_dt, b_re, b_im, c_re, c_im, d_skip, w_glu, g_ssm_out, lam_q1, lam_k1, lam_q2, lam_k2, g_sub, w_out, g_x, g_mem, wq_x, wk_x, wv_x, wo_x, g_ffn, w_router, router_bias, w_gate, w_up, w_down, ws_gate, ws_up, ws_down, g_final):
    b, l, d = x.shape
    n = b * l
    depth = w_in.shape[0]
    row = lambda a: a.reshape(1, -1).astype(F32)
    inv = ROPE_THETA ** (-jnp.arange(0, DA_QKDIM, 2, dtype=F32) / DA_QKDIM)
    inv_row = jnp.tile(inv, LANES // inv.shape[0]).reshape(1, LANES)
    pos3 = positions.reshape(b, l, 1)

    h = x
    for i in range(depth):
        lambda_init = 0.8 - 0.6 * math.exp(-0.3 * i)
        u, q, k, v = _inproj(h, pos3, row(g_mix[i]), w_in[i].astype(BF16), inv_row)

        ssm_mats = _ssm_matrices(a_re[i], a_im[i], log_dt[i], b_re[i], b_im[i], c_re[i], c_im[i])
        y_ssm = _ssm(u, ssm_mats, row(d_skip[i]), w_glu[i].astype(BF16), row(g_ssm_out[i]))

        lam = (jnp.exp(jnp.sum(lam_q1[i].astype(F32) * lam_k1[i].astype(F32)))
               - jnp.exp(jnp.sum(lam_q2[i].astype(F32) * lam_k2[i].astype(F32))) + lambda_init).reshape(1)
        kmem, vmem = _memkv(mem, row(g_mem[i]), wk_x[i].astype(BF16), wv_x[i].astype(BF16))
        wo = w_out[i].astype(BF16)
        wsgu = jnp.concatenate([ws_gate[i], ws_up[i]], axis=1).astype(BF16)
        wr = w_router[i].T.astype(F32)
        wr_hi = wr.astype(BF16)
        wr_split = jnp.concatenate([wr_hi, (wr - wr_hi.astype(F32)).astype(BF16)], axis=0)

        nb = b // MOE_PARTS
        n_part = nb * l

        def attend(part):
            return _diff_attention(q, k, v, lam.astype(F32), row(g_sub[i]), lambda_init, part * nb, nb)

        def route_and_dispatch(part, y_att, mix_after=None, plan_after=None):
            h2 = _mix(h, y_ssm, y_att, wo[:SSM_WIDTH], wo[SSM_WIDTH:], row(g_x[i]), wq_x[i].astype(BF16),
                      kmem, vmem, wo_x[i].astype(BF16), part * nb, mix_after)
            t_pk, base, eidx, gw, rank, counts = _route(
                h2.reshape(n_part, d), row(g_ffn[i]), wr_split,
                router_bias[i].reshape(N_EXPERTS, 1).astype(F32), wsgu, ws_down[i].astype(BF16), 0, 1)
            pstart, schedule, n_slots = _block_schedule(counts[:, 0], n_part)
            dest3 = _plan(pstart, eidx, rank, SC_WINDOW, plan_after)
            xs = _sc_dispatch(t_pk, dest3, n_slots)
            return base, gw, schedule, dest3, xs

        def experts_and_combine(state, after):
            base, gw, schedule, dest3, xs = state
            ys = _experts(schedule, xs, w_gate[i], w_up[i], w_down[i], after)
            return base, gw, _sc_combine(ys, dest3), ys

        assert MOE_PARTS == 2
        state_a = route_and_dispatch(0, attend(0))
        y_att_b = attend(1)
        done_a = experts_and_combine(state_a, after=y_att_b)
        state_b = route_and_dispatch(1, y_att_b, mix_after=done_a[3], plan_after=done_a[2])
        out = _final(done_a[0], done_a[2], done_a[1], row(g_final), i == depth - 1, None, 0, MOE_PARTS)
        done_b = experts_and_combine(state_b, after=out)
        out = _final(done_b[0], done_b[2], done_b[1], row(g_final), i == depth - 1, out, 1, MOE_PARTS)
        h = out.reshape(b, l, d)
    return h
```
